```python
import jax
import jax.numpy as jnp
from jax import lax
import numpy as np

D_MODEL = 1024
BATCH = 8
SEQ = 2048
DEPTH = 2
DEC_BATCH = 8
DEC_SEQ = 64
PAST_LEN = 4096

CHUNK = 64
N_MEM = 256
N_EVEN = (DEPTH + 1) // 2
N_ODD = DEPTH // 2
EPS = 1e-6
A_GROUPS = 4
A_DIM = D_MODEL // 2
A_GDIM = A_DIM // A_GROUPS
A_CHUNK = 128
B_HEADS = 4
B_DIM = D_MODEL // 2
B_HDIM = B_DIM // B_HEADS
EVEN_SPLITS = (A_DIM, 2 * A_DIM, 2 * A_DIM + B_DIM, 2 * A_DIM + 2 * B_DIM, 2 * A_DIM + 3 * B_DIM)
EVEN_IN = 2 * A_DIM + 4 * B_DIM
C_HEADS = 16
C_KV_HEADS = 4
C_HDIM = D_MODEL // C_HEADS
C_GROUP = C_HEADS // C_KV_HEADS
WINDOW = 128
ROPE_THETA = 10000.0
ODD_SPLITS = (C_HEADS * C_HDIM, (C_HEADS + C_KV_HEADS) * C_HDIM)
ODD_IN = (C_HEADS + 2 * C_KV_HEADS) * C_HDIM
X_HEADS = 4
X_HDIM = 128
D_FF = 2816
N_EXPERTS = 8
TOP_K = 2

kernel_name = 'hybrid_streaming_encoder_step'

F32 = jnp.float32


def rms_norm(x, g):
    xf = x.astype(F32)
    y = xf * lax.rsqrt(jnp.mean(xf * xf, axis=-1, keepdims=True) + EPS)
    return (y * g.astype(F32)).astype(x.dtype)


def layer_norm(x, g, b):
    xf = x.astype(F32)
    xc = xf - jnp.mean(xf, axis=-1, keepdims=True)
    y = xc * lax.rsqrt(jnp.mean(xc * xc, axis=-1, keepdims=True) + EPS)
    return (y * g.astype(F32) + b.astype(F32)).astype(x.dtype)


def rope(x, pos):
    half = x.shape[-1] // 2
    inv = 1.0 / (ROPE_THETA ** (jnp.arange(half, dtype=F32) / half))
    ang = pos.astype(F32)[:, None] * inv[None, :]
    cos = jnp.cos(ang)[None, :, None, :]
    sin = jnp.sin(ang)[None, :, None, :]
    xf = x.astype(F32)
    x1, x2 = xf[..., :half], xf[..., half:]
    return jnp.concatenate([x1 * cos - x2 * sin, x2 * cos + x1 * sin], axis=-1).astype(x.dtype)


def gmlp_spatial_gate(u, v, w_s, b_s, ln_g, ln_b):
    B, L = u.shape[:2]
    n = min(A_CHUNK, L)
    v = layer_norm(jax.nn.gelu(v), ln_g, ln_b)
    w = jnp.where(jnp.tril(jnp.ones((n, n), bool))[None], w_s[:, :n, :n], jnp.zeros((), w_s.dtype))
    vc = v.reshape(B, L // n, n, A_GROUPS, A_GDIM)
    s = jnp.einsum('gts,bcsgd->bctgd', w, vc) + jnp.transpose(b_s[:, :n])[None, None, :, :, None]
    out = jax.nn.gelu(u) * s.reshape(B, L, A_GROUPS, A_GDIM)
    return out.reshape(B, L, A_DIM), v


def hgrn2_scan(q, f_logit, i, lb, S0):
    B, L, H, K = q.shape
    q = jax.nn.silu(q.astype(F32))
    f = lb + (1.0 - lb) * jax.nn.sigmoid(f_logit.astype(F32))
    logf = jnp.log(f)
    k = 1.0 - f
    n = min(CHUNK, L)
    nc = L // n

    def blocks(t):
        return t.reshape(B, nc, n, H, t.shape[-1]).transpose(1, 0, 3, 2, 4)

    causal = jnp.tril(jnp.ones((n, n), bool))[:, :, None]

    def step(S, blk):
        qb, kb, vb, gb = blk
        G = jnp.cumsum(gb, axis=2)
        o = jnp.einsum('bhtk,bhkv->bhtv', qb * jnp.exp(G), S)
        decay = jnp.exp(jnp.where(causal, G[:, :, :, None, :] - G[:, :, None, :, :], -jnp.inf))
        att = jnp.einsum('bhtk,bhsk,bhtsk->bhts', qb, kb, decay)
        o = o + jnp.einsum('bhts,bhsv->bhtv', att, vb)
        G_end = G[:, :, -1:, :]
        S = jnp.exp(G_end[:, :, 0, :, None]) * S + jnp.einsum('bhsk,bhsv->bhkv', kb * jnp.exp(G_end - G), vb)
        return S, o

    S, o = lax.scan(step, S0.astype(F32), (blocks(q), blocks(k), blocks(i.astype(F32)), blocks(logf)))
    return o.transpose(1, 0, 3, 2, 4).reshape(B, L, H, -1), S


def even_mixer(h, S0, w_in, w_out, w_s, b_s, ln_g, ln_b, lb, out_g):
    B, L, _ = h.shape
    u, v, q, f, i, g = jnp.split(h @ w_in, EVEN_SPLITS, axis=-1)
    grp = lambda t: t.reshape(B, L, A_GROUPS, A_GDIM)
    hds = lambda t: t.reshape(B, L, B_HEADS, B_HDIM)
    a_out, v_rows = gmlp_spatial_gate(grp(u), grp(v), w_s, b_s, ln_g, ln_b)
    o, S = hgrn2_scan(hds(q), hds(f), hds(i), lb.reshape(B_HEADS, B_HDIM), S0)
    b_out = rms_norm(o, out_g) * jax.nn.silu(hds(g).astype(F32))
    mixed = jnp.concatenate([a_out.astype(h.dtype), b_out.reshape(B, L, B_DIM).astype(h.dtype)], axis=-1)
    return mixed @ w_out, v_rows, S


def attn_qkv(h, pos, w_in, q_g, k_g):
    B, L, _ = h.shape
    q, k, v = jnp.split(h @ w_in, ODD_SPLITS, axis=-1)
    q = rope(rms_norm(q.reshape(B, L, C_HEADS, C_HDIM), q_g), pos)
    k = rope(rms_norm(k.reshape(B, L, C_KV_HEADS, C_HDIM), k_g), pos)
    return q, k, v.reshape(B, L, C_KV_HEADS, C_HDIM)


def sink_softmax(s, sinks):
    sk = jnp.broadcast_to(sinks.astype(F32).reshape(C_KV_HEADS, C_GROUP, 1, 1), s.shape[:-1] + (1,))
    return jax.nn.softmax(jnp.concatenate([s, sk], axis=-1), axis=-1)[..., :-1]


def swa_prompt(q, k, v, sinks):
    B, L = q.shape[:2]
    nc = L // CHUNK
    nb = WINDOW // CHUNK
    pad = ((0, 0), (WINDOW, 0), (0, 0), (0, 0))
    kp = jnp.pad(k, pad).reshape(B, nc + nb, CHUNK, C_KV_HEADS, C_HDIM)
    vp = jnp.pad(v, pad).reshape(B, nc + nb, CHUNK, C_KV_HEADS, C_HDIM)
    kw = jnp.concatenate([kp[:, j:j + nc] for j in range(nb + 1)], axis=2)
    vw = jnp.concatenate([vp[:, j:j + nc] for j in range(nb + 1)], axis=2)
    key_pos = jnp.arange(nc)[:, None] * CHUNK - WINDOW + jnp.arange(WINDOW + CHUNK)[None, :]
    valid = key_pos >= 0
    qc = q.reshape(B, nc, CHUNK, C_KV_HEADS, C_GROUP, C_HDIM)
    s = jnp.einsum('bnqhgd,bnkhd->bnhgqk', qc, kw).astype(F32) * (C_HDIM ** -0.5)
    s = jnp.where(valid[None, :, None, None, None, :], s, -jnp.inf)
    p = sink_softmax(s, sinks)
    o = jnp.einsum('bnhgqk,bnkhd->bnqhgd', p.astype(v.dtype), vw)
    return o.reshape(B, L, C_HEADS * C_HDIM)


def swa_sample(q, k, v, ck, cv, sinks):
    B, L = q.shape[:2]
    kk = jnp.concatenate([ck.astype(k.dtype), k], axis=1)
    vv = jnp.concatenate([cv.astype(v.dtype), v], axis=1)
    qg = q.reshape(B, L, C_KV_HEADS, C_GROUP, C_HDIM)
    s = jnp.einsum('bqhgd,bkhd->bhgqk', qg, kk).astype(F32) * (C_HDIM ** -0.5)
    p = sink_softmax(s, sinks)
    o = jnp.einsum('bhgqk,bkhd->bqhgd', p.astype(vv.dtype), vv)
    return o.reshape(B, L, C_HEADS * C_HDIM)


def memory_kv(mem, g_mem, w_k, w_v, k_g):
    B, M, _ = mem.shape
    m = rms_norm(mem, g_mem)
    mk = rms_norm((m @ w_k).reshape(B, M, X_HEADS, X_HDIM), k_g)
    mv = (m @ w_v).reshape(B, M, X_HEADS, X_HDIM)
    return mk, mv


def cross_attn(h, mk, mv, w_q, w_o, q_g):
    B, L, _ = h.shape
    q = rms_norm((h @ w_q).reshape(B, L, X_HEADS, X_HDIM), q_g)
    s = jnp.einsum('bqhd,bkhd->bhqk', q, mk.astype(q.dtype)).astype(F32) * (X_HDIM ** -0.5)
    p = jax.nn.softmax(s, axis=-1)
    o = jnp.einsum('bhqk,bkhd->bqhd', p.astype(mv.dtype), mv)
    return o.reshape(B, L, X_HEADS * X_HDIM) @ w_o


def swiglu(h, w1, w3, w2):
    return (jax.nn.silu(h @ w1) * (h @ w3)) @ w2


def moe_swiglu(h, router, w1, w3, w2):
    logits = (h @ router).astype(F32)
    top_v, top_i = lax.top_k(logits, TOP_K)
    gates = jax.nn.softmax(top_v, axis=-1)
    gate_full = jnp.sum(jax.nn.one_hot(top_i, N_EXPERTS, dtype=F32) * gates[..., None], axis=-2)
    out = jnp.zeros(h.shape, F32)
    for e in range(N_EXPERTS):
        out = out + gate_full[..., e:e + 1] * swiglu(h, w1[e], w3[e], w2[e]).astype(F32)
    return out


def setup_inputs(seed: int = 0) -> dict:
    key = jax.random.key(seed)
    keys = jax.random.split(key, 64)
    ctr = [0]

    def nrm(shape, scale=1.0):
        kk = keys[ctr[0]]
        ctr[0] += 1
        return jax.random.normal(kk, shape, F32) * scale

    def gain(shape):
        return 1.0 + 0.05 * nrm(shape)

    swa_rows = min(WINDOW, PAST_LEN)
    d = D_MODEL
    return {
        'x_prompt': nrm((BATCH, SEQ, d)),
        'x_sample': nrm((DEC_BATCH, DEC_SEQ, d)),
        'mem_prompt': nrm((BATCH, N_MEM, d)),
        'cache_mem_k': nrm((DEPTH, DEC_BATCH, N_MEM, X_HEADS, X_HDIM)),
        'cache_mem_v': nrm((DEPTH, DEC_BATCH, N_MEM, X_HEADS, X_HDIM), 0.5),
        'state_hgrn': nrm((N_EVEN, DEC_BATCH, B_HEADS, B_HDIM, B_HDIM), 0.5),
        'cache_swa_k': nrm((N_ODD, DEC_BATCH, swa_rows, C_KV_HEADS, C_HDIM)),
        'cache_swa_v': nrm((N_ODD, DEC_BATCH, swa_rows, C_KV_HEADS, C_HDIM), 0.5),
        'norm_mix': gain((DEPTH, d)),
        'norm_xattn': gain((DEPTH, d)),
        'norm_ffn': gain((DEPTH, d)),
        'even_w_in': nrm((N_EVEN, d, EVEN_IN), d ** -0.5),
        'even_w_out': nrm((N_EVEN, A_DIM + B_DIM, d), (A_DIM + B_DIM) ** -0.5),
        'gmlp_w_s': nrm((N_EVEN, A_GROUPS, A_CHUNK, A_CHUNK), A_CHUNK ** -0.5),
        'gmlp_b_s': gain((N_EVEN, A_GROUPS, A_CHUNK)),
        'gmlp_ln_g': gain((N_EVEN, A_GROUPS, A_GDIM)),
        'gmlp_ln_b': nrm((N_EVEN, A_GROUPS, A_GDIM), 0.02),
        'hgrn_lb_logits': nrm((N_EVEN + 1, B_DIM), 0.5),
        'hgrn_out_norm': gain((N_EVEN, B_HDIM)),
        'attn_w_in': nrm((N_ODD, d, ODD_IN), d ** -0.5),
        'attn_w_out': nrm((N_ODD, C_HEADS * C_HDIM, d), (C_HEADS * C_HDIM) ** -0.5),
        'attn_q_norm': gain((N_ODD, C_HDIM)),
        'attn_k_norm': gain((N_ODD, C_HDIM)),
        'attn_sinks': nrm((N_ODD, C_HEADS), 0.5),
        'xattn_mem_norm': gain((DEPTH, d)),
        'xattn_w_q': nrm((DEPTH, d, X_HEADS * X_HDIM), d ** -0.5),
        'xattn_w_k': nrm((DEPTH, d, X_HEADS * X_HDIM), d ** -0.5),
        'xattn_w_v': nrm((DEPTH, d, X_HEADS * X_HDIM), d ** -0.5),
        'xattn_w_o': nrm((DEPTH, X_HEADS * X_HDIM, d), (X_HEADS * X_HDIM) ** -0.5),
        'xattn_q_norm': gain((DEPTH, X_HDIM)),
        'xattn_k_norm': gain((DEPTH, X_HDIM)),
        'ffn_w1': nrm((N_EVEN, d, D_FF), d ** -0.5),
        'ffn_w3': nrm((N_EVEN, d, D_FF), d ** -0.5),
        'ffn_w2': nrm((N_EVEN, D_FF, d), D_FF ** -0.5),
        'moe_router': nrm((N_ODD, d, N_EXPERTS), d ** -0.5),
        'moe_w1': nrm((N_ODD, N_EXPERTS, d, D_FF), d ** -0.5),
        'moe_w3': nrm((N_ODD, N_EXPERTS, d, D_FF), d ** -0.5),
        'moe_w2': nrm((N_ODD, N_EXPERTS, D_FF, d), D_FF ** -0.5),
    }


def reference(x_prompt, x_sample, mem_prompt, cache_mem_k, cache_mem_v, state_hgrn, cache_swa_k, cache_swa_v,
              norm_mix, norm_xattn, norm_ffn,
              even_w_in, even_w_out, gmlp_w_s, gmlp_b_s, gmlp_ln_g, gmlp_ln_b, hgrn_lb_logits, hgrn_out_norm,
              attn_w_in, attn_w_out, attn_q_norm, attn_k_norm, attn_sinks,
              xattn_mem_norm, xattn_w_q, xattn_w_k, xattn_w_v, xattn_w_o, xattn_q_norm, xattn_k_norm,
              ffn_w1, ffn_w3, ffn_w2, moe_router, moe_w1, moe_w3, moe_w2):
    lb_all = jnp.cumsum(jax.nn.softmax(hgrn_lb_logits.astype(F32), axis=0), axis=0)
    L_p = x_prompt.shape[1]
    pos_p = jnp.arange(L_p, dtype=jnp.int32)
    pos_s = PAST_LEN + jnp.arange(x_sample.shape[1], dtype=jnp.int32)
    xp, xs = x_prompt, x_sample
    mem_k_p, mem_v_p, hgrn_p, gmlp_v_s, hgrn_s = [], [], [], [], []
    swk_p, swv_p, swk_s, swv_s = [], [], [], []
    for l in range(DEPTH):
        j = l // 2
        hp = rms_norm(xp, norm_mix[l])
        hs = rms_norm(xs, norm_mix[l])
        if l % 2 == 0:
            wts = (even_w_in[j], even_w_out[j], gmlp_w_s[j], gmlp_b_s[j], gmlp_ln_g[j], gmlp_ln_b[j],
                   lb_all[j], hgrn_out_norm[j])
            S0 = jnp.zeros((xp.shape[0], B_HEADS, B_HDIM, B_HDIM), F32)
            mp, _, Sp = even_mixer(hp, S0, *wts)
            ms, v_rows, Ss = even_mixer(hs, state_hgrn[j], *wts)
            hgrn_p.append(Sp)
            hgrn_s.append(Ss)
            gmlp_v_s.append(v_rows)
        else:
            qp, kp, vp = attn_qkv(hp, pos_p, attn_w_in[j], attn_q_norm[j], attn_k_norm[j])
            mp = swa_prompt(qp, kp, vp, attn_sinks[j]) @ attn_w_out[j]
            qs, ks, vs = attn_qkv(hs, pos_s, attn_w_in[j], attn_q_norm[j], attn_k_norm[j])
            ms = swa_sample(qs, ks, vs, cache_swa_k[j], cache_swa_v[j], attn_sinks[j]) @ attn_w_out[j]
            rows = min(WINDOW, L_p)
            swk_p.append(kp[:, L_p - rows:])
            swv_p.append(vp[:, L_p - rows:])
            swk_s.append(ks)
            swv_s.append(vs)
        xp = xp + mp.astype(xp.dtype)
        xs = xs + ms.astype(xs.dtype)
        mk, mv = memory_kv(mem_prompt, xattn_mem_norm[l], xattn_w_k[l], xattn_w_v[l], xattn_k_norm[l])
        mem_k_p.append(mk)
        mem_v_p.append(mv)
        xp = xp + cross_attn(rms_norm(xp, norm_xattn[l]), mk, mv, xattn_w_q[l], xattn_w_o[l],
                             xattn_q_norm[l]).astype(xp.dtype)
        xs = xs + cross_attn(rms_norm(xs, norm_xattn[l]), cache_mem_k[l], cache_mem_v[l], xattn_w_q[l],
                             xattn_w_o[l], xattn_q_norm[l]).astype(xs.dtype)
        hp = rms_norm(xp, norm_ffn[l])
        hs = rms_norm(xs, norm_ffn[l])
        if l % 2 == 0:
            xp = xp + swiglu(hp, ffn_w1[j], ffn_w3[j], ffn_w2[j]).astype(xp.dtype)
            xs = xs + swiglu(hs, ffn_w1[j], ffn_w3[j], ffn_w2[j]).astype(xs.dtype)
        else:
            xp = xp + moe_swiglu(hp, moe_router[j], moe_w1[j], moe_w3[j], moe_w2[j]).astype(xp.dtype)
            xs = xs + moe_swiglu(hs, moe_router[j], moe_w1[j], moe_w3[j], moe_w2[j]).astype(xs.dtype)
    return (xp, xs, jnp.stack(mem_k_p), jnp.stack(mem_v_p), jnp.stack(hgrn_p), jnp.stack(gmlp_v_s),
            jnp.stack(hgrn_s), jnp.stack(swk_p), jnp.stack(swv_p), jnp.stack(swk_s), jnp.stack(swv_s))
```

```python
import functools

import numpy as np
import jax
import jax.numpy as jnp
from jax import lax
from jax.experimental import pallas as pl
from jax.experimental.pallas import tpu as pltpu

F32 = jnp.float32
BF16 = jnp.bfloat16

D_MODEL = 1024
EPS = 1e-6
CHUNK = 64
A_GROUPS = 4
A_DIM = D_MODEL // 2
A_GDIM = A_DIM // A_GROUPS
A_CHUNK = 128
B_HEADS = 4
B_DIM = D_MODEL // 2
B_HDIM = B_DIM // B_HEADS
EVEN_IN = 2 * A_DIM + 4 * B_DIM
C_HEADS = 16
C_KV_HEADS = 4
C_HDIM = D_MODEL // C_HEADS
C_GROUP = C_HEADS // C_KV_HEADS
C_KV_DIM = C_KV_HEADS * C_HDIM
WINDOW = 128
ROPE_THETA = 10000.0
PAST_LEN = 4096
ODD_IN = (C_HEADS + 2 * C_KV_HEADS) * C_HDIM
X_HEADS = 4
X_HDIM = 128
X_DIM = X_HEADS * X_HDIM
D_FF = 2816
N_EXPERTS = 8
TOP_K = 2

LANES = 128
ROW_TILE = 512
FF_CHUNK = 1408
N_FF_CHUNKS = D_FF // FF_CHUNK
N_LEVELS = 6
VMEM_LIMIT = 52 * 1024 * 1024


def _cparams(n_axes):
    return pltpu.CompilerParams(dimension_semantics=("arbitrary",) * n_axes, vmem_limit_bytes=VMEM_LIMIT)


def _dot(a, b):
    return jnp.dot(a, b, preferred_element_type=F32)


def _dot_nt(a, b):
    return lax.dot_general(a, b, (((1,), (1,)), ((), ())), preferred_element_type=F32)


def _rms(x, g):
    return x * lax.rsqrt(jnp.mean(x * x, axis=-1, keepdims=True) + EPS) * g


def _silu(x):
    return x * (1.0 / (1.0 + jnp.exp(-x)))


def _split3(x):
    hi = x.astype(BF16)
    r1 = x - hi.astype(F32)
    mid = r1.astype(BF16)
    lo = (r1 - mid.astype(F32)).astype(BF16)
    return hi, mid, lo


def _cumsum_matrix():
    r = np.arange(CHUNK)
    s = np.arange(CHUNK)
    blocks = [(s[None, :] <= r[:, None])]
    for l in range(N_LEVELS):
        h = 1 << l
        ref = (r & ~(2 * h - 1)) + h - 1
        blocks.append(s[None, :] <= ref[:, None])
    w = np.concatenate(blocks, axis=0).astype(np.float32)
    return np.concatenate([w, w, w], axis=1)


def _level_masks():
    t = np.arange(CHUNK)[:, None]
    s = np.arange(CHUNK)[None, :]
    masks = []
    for l in range(N_LEVELS):
        masks.append(((t >> (l + 1)) == (s >> (l + 1))) & (((t >> l) & 1) == 1) & (((s >> l) & 1) == 0))
    masks.append(t == s)
    return np.stack(masks).astype(np.float32)


def _even_mixer_body(x_ref, s0_ref, nrm_ref, win_ref, wout_ref, ws_ref, bs_ref, lng_ref, lnb_ref, lb_ref, og_ref,
                     wcum_ref, lmask_ref, *rest, n_seq, seq_rows, gchunk, emit_v):
    if emit_v:
        y_ref, sout_ref, v_ref, proj_scr, mixed_scr, st_scr = rest
    else:
        y_ref, sout_ref, proj_scr, mixed_scr, st_scr = rest
        v_ref = None
    rows = n_seq * seq_rows
    j = pl.program_id(1)

    @pl.when(j == 0)
    def _():
        for s in range(n_seq):
            for hd in range(B_HEADS):
                st_scr[s * B_HEADS + hd] = s0_ref[s, hd].T

    x = x_ref[...]
    h = _rms(x, nrm_ref[...]).astype(BF16)
    n_pieces = EVEN_IN // ROW_TILE
    for n in range(n_pieces):
        cs = slice(n * ROW_TILE, (n + 1) * ROW_TILE)
        proj_scr[:, cs] = _dot(h, win_ref[:, cs])

    for g in range(A_GROUPS):
        gs = slice(g * A_GDIM, (g + 1) * A_GDIM)
        vg = jax.nn.gelu(proj_scr[:, A_DIM + g * A_GDIM:A_DIM + (g + 1) * A_GDIM])
        vc = vg - jnp.mean(vg, axis=-1, keepdims=True)
        vn = vc * lax.rsqrt(jnp.mean(vc * vc, axis=-1, keepdims=True) + EPS) * lng_ref[g] + lnb_ref[g]
        if v_ref is not None:
            v_ref[:, gs] = vn
        ug = jax.nn.gelu(proj_scr[:, gs])
        vb = vn.astype(BF16)
        for c in range(rows // gchunk):
            rs = slice(c * gchunk, (c + 1) * gchunk)
            sp = _dot(ws_ref[g], vb[rs]) + bs_ref[g]
            mixed_scr[rs, gs] = ug[rs] * sp

    q0, f0, i0, g0 = (2 * A_DIM + k * B_DIM for k in range(4))
    lb = lb_ref[...]
    fg = lb + (1.0 - lb) * jax.nn.sigmoid(proj_scr[:, f0:f0 + B_DIM])
    proj_scr[:, 0:B_DIM] = jnp.log(fg)
    proj_scr[:, B_DIM:2 * B_DIM] = 1.0 - fg
    proj_scr[:, q0:q0 + B_DIM] = _silu(proj_scr[:, q0:q0 + B_DIM])
    og = og_ref[...]
    chunks_per_seq = seq_rows // CHUNK

    def chunk_body(c, carry):
        r0 = pl.multiple_of(c * CHUNK, CHUNK)
        rs = pl.ds(r0, CHUNK)
        sidx = (c // chunks_per_seq) * B_HEADS
        hi, mid, lo = _split3(proj_scr[rs, 0:B_DIM])
        gg = _dot(wcum_ref[...], jnp.concatenate([hi, mid, lo], axis=0))
        for hd in range(B_HEADS):
            hs = slice(hd * B_HDIM, (hd + 1) * B_HDIM)
            G = gg[0:CHUNK, hs]
            q = proj_scr[rs, q0 + hd * B_HDIM:q0 + (hd + 1) * B_HDIM]
            k = proj_scr[rs, B_DIM + hd * B_HDIM:B_DIM + (hd + 1) * B_HDIM]
            v = proj_scr[rs, i0 + hd * B_HDIM:i0 + (hd + 1) * B_HDIM]
            vb = v.astype(BF16)
            st = st_scr[sidx + hd]
            o = _dot_nt((q * jnp.exp(G)).astype(BF16), st.astype(BF16))
            att = jnp.zeros((CHUNK, CHUNK), F32)
            for l in range(N_LEVELS + 1):
                if l < N_LEVELS:
                    e = jnp.exp(-jnp.abs(G - gg[(l + 1) * CHUNK:(l + 2) * CHUNK, hs]))
                    a = _dot_nt((q * e).astype(BF16), (k * e).astype(BF16))
                else:
                    a = _dot_nt(q.astype(BF16), k.astype(BF16))
                att = jnp.where(lmask_ref[l] > 0.5, a, att)
            o = o + _dot(att.astype(BF16), vb)
            g_end = G[CHUNK - 1:CHUNK, :]
            kd = (k * jnp.exp(g_end - G)).astype(BF16)
            st_scr[sidx + hd] = st * jnp.exp(g_end) + _dot(v.T.astype(BF16), kd)
            on = o * lax.rsqrt(jnp.mean(o * o, axis=-1, keepdims=True) + EPS) * og
            gate = _silu(proj_scr[rs, g0 + hd * B_HDIM:g0 + (hd + 1) * B_HDIM])
            mixed_scr[rs, A_DIM + hd * B_HDIM:A_DIM + (hd + 1) * B_HDIM] = on * gate
        return carry

    lax.fori_loop(0, rows // CHUNK, chunk_body, 0)

    y_ref[...] = x + _dot(mixed_scr[...].astype(BF16), wout_ref[...])

    @pl.when(j == pl.num_programs(1) - 1)
    def _():
        for s in range(n_seq):
            for hd in range(B_HEADS):
                sout_ref[s, hd] = st_scr[s * B_HEADS + hd].T


def _even_mixer(x, s0, consts, *, n_batch, tiles_per_batch, n_seq, seq_rows, gchunk, emit_v, out_rows, out_block0,
                alias_buf=None):
    rows = n_seq * seq_rows
    grid = (n_batch, tiles_per_batch)
    tile = lambda b, j: (b * tiles_per_batch + j, 0)
    whole = lambda *shape: pl.BlockSpec(shape, lambda b, j: (0,) * len(shape))
    in_specs = [
        pl.BlockSpec((rows, D_MODEL), tile),
        pl.BlockSpec((n_seq, B_HEADS, B_HDIM, B_HDIM), lambda b, j: (b, 0, 0, 0)),
        whole(1, D_MODEL), whole(D_MODEL, EVEN_IN), whole(D_MODEL, D_MODEL),
        whole(A_GROUPS, gchunk, gchunk), whole(A_GROUPS, gchunk, LANES),
        whole(A_GROUPS, 1, A_GDIM), whole(A_GROUPS, 1, A_GDIM), whole(1, B_DIM), whole(1, B_HDIM),
        whole((N_LEVELS + 1) * CHUNK, 3 * CHUNK), whole(N_LEVELS + 1, CHUNK, CHUNK),
    ]
    args = [x, s0] + list(consts)
    out_shape = [jax.ShapeDtypeStruct((out_rows, D_MODEL), F32),
                 jax.ShapeDtypeStruct((n_batch * n_seq, B_HEADS, B_HDIM, B_HDIM), F32)]
    out_specs = [pl.BlockSpec((rows, D_MODEL), lambda b, j: (out_block0 + b * tiles_per_batch + j, 0)),
                 pl.BlockSpec((n_seq, B_HEADS, B_HDIM, B_HDIM), lambda b, j: (b, 0, 0, 0))]
    if emit_v:
        out_shape.append(jax.ShapeDtypeStruct((n_batch * tiles_per_batch * rows, A_DIM), F32))
        out_specs.append(pl.BlockSpec((rows, A_DIM), tile))
    aliases = {}
    if alias_buf is not None:
        in_specs.append(pl.BlockSpec(memory_space=pl.ANY))
        args.append(alias_buf)
        aliases = {len(args) - 1: 0}
    body = functools.partial(_even_mixer_body, n_seq=n_seq, seq_rows=seq_rows, gchunk=gchunk, emit_v=emit_v)
    if alias_buf is not None:
        inner = body
        body = lambda *refs: inner(*refs[:13], *refs[14:])
    return pl.pallas_call(
        body, grid=grid, in_specs=in_specs, out_specs=out_specs, out_shape=out_shape,
        scratch_shapes=[pltpu.VMEM((rows, EVEN_IN), F32), pltpu.VMEM((rows, D_MODEL), F32),
                        pltpu.VMEM((n_seq * B_HEADS, B_HDIM, B_HDIM), F32)],
        input_output_aliases=aliases, compiler_params=_cparams(2),
        name="even_mixer_s" if emit_v else "even_mixer_p")(*args)


def _memory_kv_body(mem_ref, gm_ref, wk_ref, wv_ref, kg_ref, mk_ref, mv_ref, *, n_seq, n_mem):
    m = _rms(mem_ref[...], gm_ref[0]).astype(BF16)
    kk = _dot(m, wk_ref[0])
    vv = _dot(m, wv_ref[0])
    kg = kg_ref[0]
    for hd in range(X_HEADS):
        hs = slice(hd * X_HDIM, (hd + 1) * X_HDIM)
        kh = _rms(kk[:, hs], kg)
        for s in range(n_seq):
            mk_ref[0, s, :, hs] = kh[s * n_mem:(s + 1) * n_mem]
    for s in range(n_seq):
        mv_ref[0, s] = vv[s * n_mem:(s + 1) * n_mem]


def _memory_kv(mem2d, g_mem, w_k, w_v, k_g, *, n_batch, n_mem):
    depth = w_k.shape[0]
    n_seq = ROW_TILE // n_mem
    out = jax.ShapeDtypeStruct((depth, n_batch, n_mem, X_DIM), F32)
    ospec = pl.BlockSpec((1, n_seq, n_mem, X_DIM), lambda l, t: (l, t, 0, 0))
    return pl.pallas_call(
        functools.partial(_memory_kv_body, n_seq=n_seq, n_mem=n_mem),
        grid=(depth, n_batch // n_seq),
        in_specs=[pl.BlockSpec((ROW_TILE, D_MODEL), lambda l, t: (t, 0)),
                  pl.BlockSpec((1, 1, D_MODEL), lambda l, t: (l, 0, 0)),
                  pl.BlockSpec((1, D_MODEL, X_DIM), lambda l, t: (l, 0, 0)),
                  pl.BlockSpec((1, D_MODEL, X_DIM), lambda l, t: (l, 0, 0)),
                  pl.BlockSpec((1, 1, X_HDIM), lambda l, t: (l, 0, 0))],
        out_specs=[ospec, ospec], out_shape=[out, out], compiler_params=_cparams(2),
        name="memory_kv")(mem2d, g_mem, w_k, w_v, k_g)


def _xattn_body(x_ref, mk_ref, mv_ref, nrm_ref, wq_ref, wo_ref, qg_ref, *rest, n_seq, seq_rows):
    y_ref, o_scr = rest[-2:]
    x = x_ref[...]
    h = _rms(x, nrm_ref[...]).astype(BF16)
    q = _dot(h, wq_ref[...])
    qg = qg_ref[...] * (X_HDIM ** -0.5)
    for s in range(n_seq):
        rs = slice(s * seq_rows, (s + 1) * seq_rows)
        for hd in range(X_HEADS):
            hs = slice(hd * X_HDIM, (hd + 1) * X_HDIM)
            qh = _rms(q[rs, hs], qg).astype(BF16)
            sc = _dot_nt(qh, mk_ref[s, :, hs].astype(BF16))
            p = jnp.exp(sc - jnp.max(sc, axis=-1, keepdims=True))
            den = jnp.sum(p, axis=-1, keepdims=True)
            o_scr[rs, hs] = _dot(p.astype(BF16), mv_ref[s, :, hs].astype(BF16)) / den
    y_ref[...] = x + _dot(o_scr[...].astype(BF16), wo_ref[...])


def _xattn(xbuf, mk, mv, nrm, wq, wo, qg, *, n_steps, steps_per_mem, block0, n_seq, seq_rows, alias_buf=None):
    rows = n_seq * seq_rows
    n_mem = mk.shape[1]
    whole = lambda *shape: pl.BlockSpec(shape, lambda t: (0,) * len(shape))
    in_specs = [pl.BlockSpec((rows, D_MODEL), lambda t: (block0 + t, 0)),
                pl.BlockSpec((n_seq, n_mem, X_DIM), lambda t: (t // steps_per_mem, 0, 0)),
                pl.BlockSpec((n_seq, n_mem, X_DIM), lambda t: (t // steps_per_mem, 0, 0)),
                whole(1, D_MODEL), whole(D_MODEL, X_DIM), whole(X_DIM, D_MODEL), whole(1, X_HDIM)]
    args = [xbuf, mk, mv, nrm, wq, wo, qg]
    aliases = {}
    if alias_buf is not None:
        in_specs.append(pl.BlockSpec(memory_space=pl.ANY))
        args.append(alias_buf)
        aliases = {len(args) - 1: 0}
    return pl.pallas_call(
        functools.partial(_xattn_body, n_seq=n_seq, seq_rows=seq_rows),
        grid=(n_steps,), in_specs=in_specs,
        out_specs=pl.BlockSpec((rows, D_MODEL), lambda t: (block0 + t, 0)),
        out_shape=jax.ShapeDtypeStruct(xbuf.shape, F32),
        scratch_shapes=[pltpu.VMEM((rows, X_DIM), F32)],
        input_output_aliases=aliases, compiler_params=_cparams(1),
        name="xattn_s" if alias_buf is not None else "xattn_p")(*args)


def _ff_chunk(i, j):
    return jnp.where(i % 2 == 0, j, N_FF_CHUNKS - 1 - j)


def _dense_ffn_body(x_ref, nrm_ref, w1_ref, w3_ref, w2_ref, y_ref, h_scr):
    j = pl.program_id(1)

    @pl.when(j == 0)
    def _():
        h_scr[...] = _rms(x_ref[...], nrm_ref[...]).astype(BF16)

    hb = h_scr[...]
    act = (_silu(_dot(hb, w1_ref[...])) * _dot(hb, w3_ref[...])).astype(BF16)
    part = _dot(act, w2_ref[...])

    @pl.when(j == 0)
    def _():
        y_ref[...] = x_ref[...] + part

    @pl.when(j > 0)
    def _():
        y_ref[...] += part


def _dense_ffn(xbuf, nrm, w1, w3, w2):
    n_tiles = xbuf.shape[0] // ROW_TILE
    return pl.pallas_call(
        _dense_ffn_body, grid=(n_tiles, N_FF_CHUNKS),
        in_specs=[pl.BlockSpec((ROW_TILE, D_MODEL), lambda i, j: (i, 0)),
                  pl.BlockSpec((1, D_MODEL), lambda i, j: (0, 0)),
                  pl.BlockSpec((D_MODEL, FF_CHUNK), lambda i, j: (0, _ff_chunk(i, j))),
                  pl.BlockSpec((D_MODEL, FF_CHUNK), lambda i, j: (0, _ff_chunk(i, j))),
                  pl.BlockSpec((FF_CHUNK, D_MODEL), lambda i, j: (_ff_chunk(i, j), 0))],
        out_specs=pl.BlockSpec((ROW_TILE, D_MODEL), lambda i, j: (i, 0)),
        out_shape=jax.ShapeDtypeStruct(xbuf.shape, F32),
        scratch_shapes=[pltpu.VMEM((ROW_TILE, D_MODEL), BF16)],
        compiler_params=_cparams(2), name="dense_ffn")(xbuf, nrm, w1, w3, w2)


def _grouped_ffn_body(te_ref, nu_ref, x_ref, w1_ref, w3_ref, w2_ref, y_ref):
    i = pl.program_id(0)
    j = pl.program_id(1)

    @pl.when(i < nu_ref[0])
    def _():
        hb = x_ref[...].astype(BF16)
        act = (_silu(_dot(hb, w1_ref[0])) * _dot(hb, w3_ref[0])).astype(BF16)
        part = _dot(act, w2_ref[0])

        @pl.when(j == 0)
        def _():
            y_ref[...] = part

        @pl.when(j > 0)
        def _():
            y_ref[...] += part

    @pl.when(i >= nu_ref[0])
    def _():
        y_ref[...] = jnp.zeros_like(y_ref)


def _grouped_ffn(xsorted, tile_expert, n_used, w1, w3, w2):
    n_tiles = xsorted.shape[0] // ROW_TILE

    def row_tile(i, j, te, nu):
        return (i, 0)

    def ff(i, j, nu):
        last = nu[0] - 1
        return jnp.where(i <= last, _ff_chunk(i, j), _ff_chunk(last, N_FF_CHUNKS - 1))

    grid_spec = pltpu.PrefetchScalarGridSpec(
        num_scalar_prefetch=2, grid=(n_tiles, N_FF_CHUNKS),
        in_specs=[pl.BlockSpec((ROW_TILE, D_MODEL), row_tile),
                  pl.BlockSpec((1, D_MODEL, FF_CHUNK), lambda i, j, te, nu: (te[i], 0, ff(i, j, nu))),
                  pl.BlockSpec((1, D_MODEL, FF_CHUNK), lambda i, j, te, nu: (te[i], 0, ff(i, j, nu))),
                  pl.BlockSpec((1, FF_CHUNK, D_MODEL), lambda i, j, te, nu: (te[i], ff(i, j, nu), 0))],
        out_specs=pl.BlockSpec((ROW_TILE, D_MODEL), row_tile))
    return pl.pallas_call(
        _grouped_ffn_body, grid_spec=grid_spec, out_shape=jax.ShapeDtypeStruct(xsorted.shape, F32),
        compiler_params=_cparams(2), name="grouped_ffn")(tile_expert, n_used, xsorted, w1, w3, w2)


def _rope_slab(xs, cos, sin_signed, first_half):
    rot = jnp.where(first_half, pltpu.roll(xs, LANES - C_HDIM // 2, 1), pltpu.roll(xs, C_HDIM // 2, 1))
    return xs * cos + rot * sin_signed


def _swa_body(sink_ref, x_ref, pk_ref, pv_ref, cos_ref, sin_ref, nrm_ref, win_ref, wout_ref, qg_ref, kg_ref, bd_ref,
              *rest, n_seq, seq_rows, past_valid, aliased):
    if aliased:
        rest = rest[1:]
    y_ref, ko_ref, vo_ref, k_scr, v_scr, q_scr, a_scr = rest
    rows = n_seq * seq_rows
    j = pl.program_id(1)
    tail = min(WINDOW, seq_rows)

    if past_valid:
        for s in range(n_seq):
            k_scr[s, 0:WINDOW] = pk_ref[s]
            v_scr[s, 0:WINDOW] = pv_ref[s]
    else:
        @pl.when(j == 0)
        def _():
            for s in range(n_seq):
                k_scr[s, 0:WINDOW] = jnp.zeros((WINDOW, C_KV_DIM), F32)
                v_scr[s, 0:WINDOW] = jnp.zeros((WINDOW, C_KV_DIM), F32)

        @pl.when(j > 0)
        def _():
            for s in range(n_seq):
                k_scr[s, 0:WINDOW] = k_scr[s, seq_rows:seq_rows + WINDOW]
                v_scr[s, 0:WINDOW] = v_scr[s, seq_rows:seq_rows + WINDOW]

    x = x_ref[...]
    h = _rms(x, nrm_ref[...]).astype(BF16)
    q_dim = C_HEADS * C_HDIM
    q = _dot(h, win_ref[:, 0:q_dim])
    k = _dot(h, win_ref[:, q_dim:q_dim + C_KV_DIM])
    v = _dot(h, win_ref[:, q_dim + C_KV_DIM:q_dim + 2 * C_KV_DIM])

    cos = cos_ref[...]
    sin_signed = sin_ref[...]
    first_half = (lax.broadcasted_iota(jnp.int32, (rows, LANES), 1) % C_HDIM) < (C_HDIM // 2)
    bd = bd_ref[...]
    qn = q * lax.rsqrt(_dot((q * q).astype(BF16), bd) + EPS) * qg_ref[...]
    kn = k * lax.rsqrt(_dot((k * k).astype(BF16), bd[0:C_KV_DIM, 0:C_KV_DIM]) + EPS) * kg_ref[...]
    scale = C_HDIM ** -0.5
    for sl in range(q_dim // LANES):
        ls = slice(sl * LANES, (sl + 1) * LANES)
        q_scr[:, ls] = _rope_slab(qn[:, ls], cos, sin_signed, first_half) * scale
    for sl in range(C_KV_DIM // LANES):
        ls = slice(sl * LANES, (sl + 1) * LANES)
        kr = _rope_slab(kn[:, ls], cos, sin_signed, first_half)
        for s in range(n_seq):
            k_scr[s, WINDOW:WINDOW + seq_rows, ls] = kr[s * seq_rows:(s + 1) * seq_rows]
    for s in range(n_seq):
        v_scr[s, WINDOW:WINDOW + seq_rows] = v[s * seq_rows:(s + 1) * seq_rows]
        ko_ref[s] = k_scr[s, WINDOW + seq_rows - tail:WINDOW + seq_rows]
        vo_ref[s] = v_scr[s, WINDOW + seq_rows - tail:WINDOW + seq_rows]

    chunks_per_seq = seq_rows // CHUNK
    n_keys = WINDOW + CHUNK
    key_idx = lax.broadcasted_iota(jnp.int32, (C_GROUP * CHUNK, n_keys), 1)
    row_grp = lax.broadcasted_iota(jnp.int32, (C_GROUP * CHUNK, 1), 0) // CHUNK

    def chunk_body(idx, carry):
        s = idx // chunks_per_seq
        c = idx % chunks_per_seq
        r0 = pl.multiple_of(idx * CHUNK, CHUNK)
        k0 = pl.multiple_of(c * CHUNK, CHUNK)
        kwin = k_scr[s, pl.ds(k0, n_keys), :]
        vwin = v_scr[s, pl.ds(k0, n_keys), :]
        for kvh in range(C_KV_HEADS):
            ks = slice(kvh * C_HDIM, (kvh + 1) * C_HDIM)
            qs = jnp.concatenate(
                [q_scr[pl.ds(r0, CHUNK), (kvh * C_GROUP + g) * C_HDIM:(kvh * C_GROUP + g + 1) * C_HDIM]
                 for g in range(C_GROUP)], axis=0)
            sc = _dot_nt(qs.astype(BF16), kwin[:, ks].astype(BF16))
            if not past_valid:
                sc = jnp.where((key_idx + k0 >= WINDOW) | (j > 0), sc, -jnp.inf)
            sink = jnp.zeros((C_GROUP * CHUNK, 1), F32)
            for g in range(C_GROUP):
                sink = jnp.where(row_grp == g, sink_ref[kvh * C_GROUP + g], sink)
            m = jnp.maximum(jnp.max(sc, axis=-1, keepdims=True), sink)
            p = jnp.exp(sc - m)
            den = jnp.sum(p, axis=-1, keepdims=True) + jnp.exp(sink - m)
            o = _dot(p.astype(BF16), vwin[:, ks].astype(BF16)) / den
            for g in range(C_GROUP):
                hs = slice((kvh * C_GROUP + g) * C_HDIM, (kvh * C_GROUP + g + 1) * C_HDIM)
                a_scr[pl.ds(r0, CHUNK), hs] = o[g * CHUNK:(g + 1) * CHUNK]
        return carry

    lax.fori_loop(0, rows // CHUNK, chunk_body, 0)
    y_ref[...] = x + _dot(a_scr[...].astype(BF16), wout_ref[...])


def _swa(xbuf, pk, pv, cos, sin_signed, sinks, consts, *, n_batch, tiles_per_batch, block0, n_seq, seq_rows,
         past_valid, alias_buf=None):
    rows = n_seq * seq_rows
    tail = min(WINDOW, seq_rows)
    n_str = n_batch * n_seq
    whole = lambda *shape: pl.BlockSpec(shape, lambda b, j, sk: (0,) * len(shape))
    in_specs = [pl.BlockSpec((rows, D_MODEL), lambda b, j, sk: (block0 + b * tiles_per_batch + j, 0)),
                pl.BlockSpec((n_seq, WINDOW, C_KV_DIM), lambda b, j, sk: (b, 0, 0)),
                pl.BlockSpec((n_seq, WINDOW, C_KV_DIM), lambda b, j, sk: (b, 0, 0)),
                pl.BlockSpec((rows, LANES), lambda b, j, sk: (j, 0)),
                pl.BlockSpec((rows, LANES), lambda b, j, sk: (j, 0)),
                whole(1, D_MODEL), whole(D_MODEL, ODD_IN), whole(D_MODEL, D_MODEL),
                whole(1, D_MODEL), whole(1, C_KV_DIM), whole(D_MODEL, D_MODEL)]
    args = [xbuf, pk, pv, cos, sin_signed] + list(consts)
    aliases = {}
    if alias_buf is not None:
        in_specs.append(pl.BlockSpec(memory_space=pl.ANY))
        args.append(alias_buf)
        aliases = {len(args): 0}
    kv_out = jax.ShapeDtypeStruct((n_str, tail, C_KV_DIM), F32)
    kv_spec = pl.BlockSpec((n_seq, tail, C_KV_DIM), lambda b, j, sk: (b, 0, 0))
    grid_spec = pltpu.PrefetchScalarGridSpec(
        num_scalar_prefetch=1, grid=(n_batch, tiles_per_batch), in_specs=in_specs,
        out_specs=[pl.BlockSpec((rows, D_MODEL), lambda b, j, sk: (block0 + b * tiles_per_batch + j, 0)),
                   kv_spec, kv_spec],
        scratch_shapes=[pltpu.VMEM((n_seq, WINDOW + seq_rows, C_KV_DIM), F32),
                        pltpu.VMEM((n_seq, WINDOW + seq_rows, C_KV_DIM), F32),
                        pltpu.VMEM((rows, D_MODEL), F32), pltpu.VMEM((rows, D_MODEL), F32)])
    return pl.pallas_call(
        functools.partial(_swa_body, n_seq=n_seq, seq_rows=seq_rows, past_valid=past_valid,
                          aliased=alias_buf is not None),
        grid_spec=grid_spec,
        out_shape=[jax.ShapeDtypeStruct(xbuf.shape, F32), kv_out, kv_out],
        input_output_aliases=aliases, compiler_params=_cparams(2),
        name="swa_s" if past_valid else "swa_p")(sinks, *args)


def _router_body(x_ref, nrm_ref, rhi_ref, rlo_ref, lstrict_ref, h_ref, info_ref, cnt_ref, run_scr):
    i = pl.program_id(0)

    @pl.when(i == 0)
    def _():
        run_scr[...] = jnp.zeros_like(run_scr)

    h = _rms(x_ref[...], nrm_ref[...])
    h_ref[...] = h
    hi = h.astype(BF16)
    lo = (h - hi.astype(F32)).astype(BF16)
    logits = _dot(hi, rhi_ref[...]) + _dot(lo, rhi_ref[...]) + _dot(hi, rlo_ref[...])
    lane = lax.broadcasted_iota(jnp.int32, logits.shape, 1).astype(F32)
    logits = jnp.where(lane < N_EXPERTS, logits, -jnp.inf)
    l1 = jnp.max(logits, axis=-1, keepdims=True)
    i1 = jnp.min(jnp.where(logits == l1, lane, float(LANES)), axis=-1, keepdims=True)
    rest = jnp.where(lane == i1, -jnp.inf, logits)
    l2 = jnp.max(rest, axis=-1, keepdims=True)
    i2 = jnp.min(jnp.where(rest == l2, lane, float(LANES)), axis=-1, keepdims=True)
    e = jnp.exp(l2 - l1)
    g1 = 1.0 / (1.0 + e)
    g2 = e * g1
    oh1 = (lane == i1).astype(F32)
    oh2 = (lane == i2).astype(F32)
    run = run_scr[...]
    tot1 = jnp.sum(oh1, axis=0, keepdims=True)
    tot2 = jnp.sum(oh2, axis=0, keepdims=True)
    c1 = _dot(lstrict_ref[...], oh1.astype(BF16)) + run
    c2 = _dot(lstrict_ref[...], oh2.astype(BF16)) + run + tot1
    r1 = jnp.sum(jnp.where(lane == i1, c1, 0.0), axis=-1, keepdims=True)
    r2 = jnp.sum(jnp.where(lane == i2, c2, 0.0), axis=-1, keepdims=True)
    run = run + tot1 + tot2
    run_scr[...] = run
    cnt_ref[...] = run
    cols = [i1, i2, r1, r2, g1, g2]
    info = jnp.zeros(logits.shape, F32)
    for n, col in enumerate(cols):
        info = jnp.where(lane == n, col, info)
    info_ref[...] = info


def _router(xbuf, nrm, rhi, rlo, lstrict):
    n_rows = xbuf.shape[0]
    whole = lambda *shape: pl.BlockSpec(shape, lambda i: (0,) * len(shape))
    return pl.pallas_call(
        _router_body, grid=(n_rows // ROW_TILE,),
        in_specs=[pl.BlockSpec((ROW_TILE, D_MODEL), lambda i: (i, 0)), whole(1, D_MODEL),
                  whole(D_MODEL, LANES), whole(D_MODEL, LANES), whole(ROW_TILE, ROW_TILE)],
        out_specs=[pl.BlockSpec((ROW_TILE, D_MODEL), lambda i: (i, 0)),
                   pl.BlockSpec((ROW_TILE, LANES), lambda i: (i, 0)), whole(1, LANES)],
        out_shape=[jax.ShapeDtypeStruct((n_rows, D_MODEL), F32), jax.ShapeDtypeStruct((n_rows, LANES), F32),
                   jax.ShapeDtypeStruct((1, LANES), F32)],
        scratch_shapes=[pltpu.VMEM((1, LANES), F32)],
        compiler_params=_cparams(1), name="router")(xbuf, nrm, rhi, rlo, lstrict)


def _row_copy(src_ref, src_row, dst_ref, dst_row, sem):
    return pltpu.make_async_copy(src_ref.at[pl.ds(src_row, 1)], dst_ref.at[pl.ds(dst_row, 1)], sem)


def _dispatch_body(pos_ref, h_ref, zeros_ref, out_ref, sem):
    del zeros_ref

    def start(r, carry):
        for k in range(TOP_K):
            _row_copy(h_ref, r, out_ref, pos_ref[TOP_K * r + k], sem).start()
        return carry

    lax.fori_loop(0, ROW_TILE, start, 0)

    def wait(r, carry):
        for k in range(TOP_K):
            _row_copy(h_ref, 0, out_ref, 0, sem).wait()
        return carry

    lax.fori_loop(0, ROW_TILE, wait, 0)


def _dispatch(pos_flat, hbuf, n_sorted_rows):
    n_rows = hbuf.shape[0]
    zeros = jnp.zeros((n_sorted_rows, D_MODEL), F32)
    return pl.pallas_call(
        _dispatch_body, grid=(n_rows // ROW_TILE,),
        in_specs=[pl.BlockSpec((TOP_K * ROW_TILE,), lambda i: (i,), memory_space=pltpu.SMEM),
                  pl.BlockSpec((ROW_TILE, D_MODEL), lambda i: (i, 0)),
                  pl.BlockSpec(memory_space=pl.ANY)],
        out_specs=pl.BlockSpec(memory_space=pl.ANY),
        out_shape=jax.ShapeDtypeStruct((n_sorted_rows, D_MODEL), F32),
        scratch_shapes=[pltpu.SemaphoreType.DMA(())],
        input_output_aliases={2: 0}, compiler_params=_cparams(1), name="moe_dispatch")(pos_flat, hbuf, zeros)


def _combine_body(pos_ref, info_ref, x_ref, e_ref, yp_ref, ys_ref, g_scr, sem, *, n_prompt_tiles):
    i = pl.program_id(0)

    def start(r, carry):
        for k in range(TOP_K):
            _row_copy(e_ref, pos_ref[TOP_K * r + k], g_scr.at[k], r, sem).start()
        return carry

    lax.fori_loop(0, ROW_TILE, start, 0)

    def wait(r, carry):
        for k in range(TOP_K):
            _row_copy(e_ref, 0, g_scr.at[k], 0, sem).wait()
        return carry

    lax.fori_loop(0, ROW_TILE, wait, 0)
    info = info_ref[...]
    y = x_ref[...] + info[:, 4:5] * g_scr[0] + info[:, 5:6] * g_scr[1]

    @pl.when(i < n_prompt_tiles)
    def _():
        yp_ref[...] = y

    @pl.when(i >= n_prompt_tiles)
    def _():
        ys_ref[...] = y


def _combine(pos_flat, info, xbuf, esorted, n_prompt_rows):
    n_rows = xbuf.shape[0]
    n_prompt_tiles = n_prompt_rows // ROW_TILE
    return pl.pallas_call(
        functools.partial(_combine_body, n_prompt_tiles=n_prompt_tiles), grid=(n_rows // ROW_TILE,),
        in_specs=[pl.BlockSpec((TOP_K * ROW_TILE,), lambda i: (i,), memory_space=pltpu.SMEM),
                  pl.BlockSpec((ROW_TILE, LANES), lambda i: (i, 0)),
                  pl.BlockSpec((ROW_TILE, D_MODEL), lambda i: (i, 0)),
                  pl.BlockSpec(memory_space=pl.ANY)],
        out_specs=[pl.BlockSpec((ROW_TILE, D_MODEL), lambda i: (jnp.minimum(i, n_prompt_tiles - 1), 0)),
                   pl.BlockSpec((ROW_TILE, D_MODEL), lambda i: (jnp.maximum(i - n_prompt_tiles, 0), 0))],
        out_shape=[jax.ShapeDtypeStruct((n_prompt_rows, D_MODEL), F32),
                   jax.ShapeDtypeStruct((n_rows - n_prompt_rows, D_MODEL), F32)],
        scratch_shapes=[pltpu.VMEM((TOP_K, ROW_TILE, D_MODEL), F32), pltpu.SemaphoreType.DMA(())],
        compiler_params=_cparams(1), name="moe_combine")(pos_flat, info, xbuf, esorted)


def _rope_tables(pos):
    half = C_HDIM // 2
    inv = 1.0 / (ROPE_THETA ** (jnp.arange(half, dtype=F32) / half))
    ang = pos.astype(F32)[:, None] * inv[None, :]
    cos = jnp.cos(ang)
    sin = jnp.sin(ang)
    reps = LANES // C_HDIM
    return jnp.tile(jnp.concatenate([cos, cos], axis=-1), (1, reps)), jnp.tile(jnp.concatenate([-sin, sin], axis=-1),
                                                                                 (1, reps))


def kernel(x_prompt, x_sample, mem_prompt, cache_mem_k, cache_mem_v, state_hgrn, cache_swa_k, cache_swa_v, norm_mix, norm_xattn, norm_ffn, even_w_in, even_w_out, gmlp_w_s, gmlp_b_s, gmlp_ln_g, gmlp_ln_b, hgrn_lb_logits, hgrn_out_norm, attn_w_in, attn_w_out, attn_q_norm, attn_k_norm, attn_sinks, xattn_mem_norm, xattn_w_q, xattn_w_k, xattn_w_v, xattn_w_o, xattn_q_norm, xattn_k_norm, ffn_w1, ffn_w3, ffn_w2, moe_router, moe_w1, moe_w3, moe_w2):
    n_batch, seq, d = x_prompt.shape
    dec_batch, dec_seq, _ = x_sample.shape
    n_mem = mem_prompt.shape[1]
    depth = norm_mix.shape[0]
    past_len = PAST_LEN
    assert d == D_MODEL and depth == 2 and seq % ROW_TILE == 0 and dec_batch * dec_seq == ROW_TILE
    assert dec_seq == CHUNK and ROW_TILE % n_mem == 0 and cache_swa_k.shape[2] == WINDOW
    assert even_w_in.shape[-1] == EVEN_IN and attn_w_in.shape[-1] == ODD_IN and ffn_w1.shape[-1] == D_FF
    assert moe_w1.shape[1] == N_EXPERTS
    n_prompt_rows = n_batch * seq
    n_rows = n_prompt_rows + dec_batch * dec_seq
    n_prompt_tiles = n_prompt_rows // ROW_TILE
    tiles_per_batch = seq // ROW_TILE
    bf = lambda w: w.astype(BF16)
    row = lambda g: g.reshape(1, -1).astype(F32)

    lb_all = jnp.cumsum(jax.nn.softmax(hgrn_lb_logits.astype(F32), axis=0), axis=0)
    wcum = jnp.asarray(_cumsum_matrix(), BF16)
    lmask = jnp.asarray(_level_masks(), F32)

    def even_consts(n):
        tril = jnp.tril(jnp.ones((n, n), bool))
        ws = jnp.where(tril[None], gmlp_w_s[0, :, :n, :n], 0.0).astype(BF16)
        bs = jnp.broadcast_to(gmlp_b_s[0, :, :n, None], (A_GROUPS, n, LANES)).astype(F32)
        return [row(norm_mix[0]), bf(even_w_in[0]), bf(even_w_out[0]), ws, bs,
                gmlp_ln_g[0].reshape(A_GROUPS, 1, A_GDIM), gmlp_ln_b[0].reshape(A_GROUPS, 1, A_GDIM),
                row(lb_all[0]), row(hgrn_out_norm[0]), wcum, lmask]

    xbuf, hgrn_p = _even_mixer(
        x_prompt.reshape(n_prompt_rows, d), jnp.zeros((n_batch, B_HEADS, B_HDIM, B_HDIM), F32),
        even_consts(A_CHUNK), n_batch=n_batch, tiles_per_batch=tiles_per_batch, n_seq=1, seq_rows=ROW_TILE,
        gchunk=A_CHUNK, emit_v=False, out_rows=n_rows, out_block0=0)
    xbuf, hgrn_s, gmlp_v = _even_mixer(
        x_sample.reshape(ROW_TILE, d), state_hgrn[0], even_consts(min(A_CHUNK, dec_seq)), n_batch=1,
        tiles_per_batch=1, n_seq=dec_batch, seq_rows=dec_seq, gchunk=min(A_CHUNK, dec_seq), emit_v=True,
        out_rows=n_rows, out_block0=n_prompt_tiles, alias_buf=xbuf)

    mem_k, mem_v = _memory_kv(mem_prompt.reshape(n_batch * n_mem, d), xattn_mem_norm.reshape(depth, 1, d),
                              bf(xattn_w_k), bf(xattn_w_v), xattn_k_norm.reshape(depth, 1, X_HDIM),
                              n_batch=n_batch, n_mem=n_mem)

    def cross_attention(buf, l):
        consts = (row(norm_xattn[l]), bf(xattn_w_q[l]), bf(xattn_w_o[l]), row(xattn_q_norm[l]))
        out = _xattn(buf, mem_k[l], mem_v[l], *consts, n_steps=n_prompt_tiles, steps_per_mem=tiles_per_batch,
                     block0=0, n_seq=1, seq_rows=ROW_TILE)
        return _xattn(buf, cache_mem_k[l].reshape(dec_batch, n_mem, X_DIM),
                      cache_mem_v[l].reshape(dec_batch, n_mem, X_DIM), *consts, n_steps=1, steps_per_mem=1,
                      block0=n_prompt_tiles, n_seq=dec_batch, seq_rows=dec_seq, alias_buf=out)

    xbuf = cross_attention(xbuf, 0)
    xbuf = _dense_ffn(xbuf, row(norm_ffn[0]), bf(ffn_w1[0]), bf(ffn_w3[0]), bf(ffn_w2[0]))

    reps = D_MODEL // C_HDIM
    bd = jnp.asarray(np.kron(np.eye(reps, dtype=np.float32), np.full((C_HDIM, C_HDIM), 1.0 / C_HDIM, np.float32)),
                     BF16)
    swa_consts = [row(norm_mix[1]), bf(attn_w_in[0]), bf(attn_w_out[0]), row(jnp.tile(attn_q_norm[0], C_HEADS)),
                  row(jnp.tile(attn_k_norm[0], C_KV_HEADS)), bd]
    sinks = attn_sinks[0].astype(F32)
    cos_p, sin_p = _rope_tables(jnp.arange(seq, dtype=jnp.int32))
    cos_s, sin_s = _rope_tables(past_len + jnp.arange(dec_seq, dtype=jnp.int32))
    no_past = jnp.zeros((n_batch, WINDOW, C_KV_DIM), F32)
    xnew, swk_p, swv_p = _swa(xbuf, no_past, no_past, cos_p, sin_p, sinks, swa_consts, n_batch=n_batch,
                              tiles_per_batch=tiles_per_batch, block0=0, n_seq=1, seq_rows=ROW_TILE,
                              past_valid=False)
    xbuf, swk_s, swv_s = _swa(xbuf, cache_swa_k[0].reshape(dec_batch, WINDOW, C_KV_DIM),
                              cache_swa_v[0].reshape(dec_batch, WINDOW, C_KV_DIM),
                              jnp.tile(cos_s, (dec_batch, 1)), jnp.tile(sin_s, (dec_batch, 1)), sinks, swa_consts,
                              n_batch=1, tiles_per_batch=1, block0=n_prompt_tiles, n_seq=dec_batch,
                              seq_rows=dec_seq, past_valid=True, alias_buf=xnew)
    xbuf = cross_attention(xbuf, 1)

    router_w = jnp.zeros((d, LANES), F32).at[:, :N_EXPERTS].set(moe_router[0].astype(F32))
    rhi = router_w.astype(BF16)
    rlo = (router_w - rhi.astype(F32)).astype(BF16)
    lstrict = jnp.asarray(np.tril(np.ones((ROW_TILE, ROW_TILE), np.float32), -1), BF16)
    hbuf, info, counts = _router(xbuf, row(norm_ffn[1]), rhi, rlo, lstrict)

    n_sorted_tiles = (TOP_K * n_rows) // ROW_TILE + N_EXPERTS
    counts = counts[0, :N_EXPERTS].astype(jnp.int32)
    padded = ((counts + ROW_TILE - 1) // ROW_TILE) * ROW_TILE
    ends = jnp.cumsum(padded)
    offs = ends - padded
    expert = info[:, 0:TOP_K].astype(jnp.int32)
    pos = (offs[expert] + info[:, TOP_K:2 * TOP_K].astype(jnp.int32)).reshape(-1)
    n_used = (ends[-1] // ROW_TILE).astype(jnp.int32).reshape(1)
    tile_ids = jnp.arange(n_sorted_tiles, dtype=jnp.int32)
    tile_expert = jnp.minimum(jnp.sum(tile_ids[:, None] >= (ends // ROW_TILE)[None, :], axis=1),
                              N_EXPERTS - 1).astype(jnp.int32)
    tile_expert = jnp.where(jnp.arange(n_sorted_tiles) < n_used[0], tile_expert, tile_expert[n_used[0] - 1])

    hsorted = _dispatch(pos, hbuf, n_sorted_tiles * ROW_TILE)
    esorted = _grouped_ffn(hsorted, tile_expert, n_used, bf(moe_w1[0]), bf(moe_w3[0]), bf(moe_w2[0]))
    y_prompt, y_sample = _combine(pos, info, xbuf, esorted, n_prompt_rows)

    n_even = state_hgrn.shape[0]
    n_odd = cache_swa_k.shape[0]
    return (y_prompt.reshape(n_batch, seq, d), y_sample.reshape(dec_batch, dec_seq, d),
            mem_k.reshape(depth, n_batch, n_mem, X_HEADS, X_HDIM), mem_v.reshape(depth, n_batch, n_mem, X_HEADS, X_HDIM),
            hgrn_p.reshape(n_even, n_batch, B_HEADS, B_HDIM, B_HDIM),
            gmlp_v.reshape(n_even, dec_batch, dec_seq, A_GROUPS, A_GDIM),
            hgrn_s.reshape(n_even, dec_batch, B_HEADS, B_HDIM, B_HDIM),
            swk_p.reshape(n_odd, n_batch, WINDOW, C_KV_HEADS, C_HDIM), swv_p.reshape(n_odd, n_batch, WINDOW, C_KV_HEADS, C_HDIM),
            swk_s.reshape(n_odd, dec_batch, dec_seq, C_KV_HEADS, C_HDIM), swv_s.reshape(n_odd, dec_batch, dec_seq, C_KV_HEADS, C_HDIM))
```

```python
import functools

import numpy as np
import jax
import jax.numpy as jnp
from jax import lax
from jax.experimental import pallas as pl
from jax.experimental.pallas import tpu as pltpu

F32 = jnp.float32
BF16 = jnp.bfloat16

D_MODEL = 1024
EPS = 1e-6
CHUNK = 64
A_GROUPS = 4
A_DIM = D_MODEL // 2
A_GDIM = A_DIM // A_GROUPS
A_CHUNK = 128
B_HEADS = 4
B_DIM = D_MODEL // 2
B_HDIM = B_DIM // B_HEADS
EVEN_IN = 2 * A_DIM + 4 * B_DIM
C_HEADS = 16
C_KV_HEADS = 4
C_HDIM = D_MODEL // C_HEADS
C_GROUP = C_HEADS // C_KV_HEADS
C_KV_DIM = C_KV_HEADS * C_HDIM
WINDOW = 128
ROPE_THETA = 10000.0
PAST_LEN = 4096
ODD_IN = (C_HEADS + 2 * C_KV_HEADS) * C_HDIM
X_HEADS = 4
X_HDIM = 128
X_DIM = X_HEADS * X_HDIM
D_FF = 2816
N_EXPERTS = 8
TOP_K = 2

LANES = 128
ROW_TILE = 512
FF_CHUNK = 1408
N_FF_CHUNKS = D_FF // FF_CHUNK
N_LEVELS = 6
GROUP = 8
LOCAL_ROWS = -(-(TOP_K * ROW_TILE + N_EXPERTS * (GROUP - 1)) // LANES) * LANES
VMEM_LIMIT = 52 * 1024 * 1024


def _cparams(n_axes):
    return pltpu.CompilerParams(dimension_semantics=("arbitrary",) * n_axes, vmem_limit_bytes=VMEM_LIMIT)


def _dot(a, b):
    return jnp.dot(a, b, preferred_element_type=F32)


def _dot_nt(a, b):
    return lax.dot_general(a, b, (((1,), (1,)), ((), ())), preferred_element_type=F32)


def _rms(x, g):
    return x * lax.rsqrt(jnp.mean(x * x, axis=-1, keepdims=True) + EPS) * g


def _silu(x):
    return x * (1.0 / (1.0 + jnp.exp(-x)))


def _split3(x):
    hi = x.astype(BF16)
    r1 = x - hi.astype(F32)
    mid = r1.astype(BF16)
    lo = (r1 - mid.astype(F32)).astype(BF16)
    return hi, mid, lo


def _cumsum_matrix():
    r = np.arange(CHUNK)
    s = np.arange(CHUNK)
    blocks = [(s[None, :] <= r[:, None])]
    for l in range(N_LEVELS):
        h = 1 << l
        ref = (r & ~(2 * h - 1)) + h - 1
        blocks.append(s[None, :] <= ref[:, None])
    w = np.concatenate(blocks, axis=0).astype(np.float32)
    return np.concatenate([w, w, w], axis=1)


def _level_masks():
    t = np.arange(CHUNK)[:, None]
    s = np.arange(CHUNK)[None, :]
    masks = []
    for l in range(N_LEVELS):
        masks.append(((t >> (l + 1)) == (s >> (l + 1))) & (((t >> l) & 1) == 1) & (((s >> l) & 1) == 0))
    masks.append(t == s)
    return np.stack(masks).astype(np.float32)


def _even_mixer_body(x_ref, s0_ref, nrm_ref, win_ref, wout_ref, ws_ref, bs_ref, lng_ref, lnb_ref, lb_ref, og_ref,
                     wcum_ref, lmask_ref, *rest, n_seq, seq_rows, gchunk, emit_v):
    if emit_v:
        y_ref, sout_ref, v_ref, proj_scr, mixed_scr, st_scr = rest
    else:
        y_ref, sout_ref, proj_scr, mixed_scr, st_scr = rest
        v_ref = None
    rows = n_seq * seq_rows
    j = pl.program_id(1)

    @pl.when(j == 0)
    def _():
        for s in range(n_seq):
            for hd in range(B_HEADS):
                st_scr[s * B_HEADS + hd] = s0_ref[s, hd].T

    x = x_ref[...]
    h = _rms(x, nrm_ref[...]).astype(BF16)
    n_pieces = EVEN_IN // ROW_TILE
    for n in range(n_pieces):
        cs = slice(n * ROW_TILE, (n + 1) * ROW_TILE)
        proj_scr[:, cs] = _dot(h, win_ref[:, cs])

    for g in range(A_GROUPS):
        gs = slice(g * A_GDIM, (g + 1) * A_GDIM)
        vg = jax.nn.gelu(proj_scr[:, A_DIM + g * A_GDIM:A_DIM + (g + 1) * A_GDIM])
        vc = vg - jnp.mean(vg, axis=-1, keepdims=True)
        vn = vc * lax.rsqrt(jnp.mean(vc * vc, axis=-1, keepdims=True) + EPS) * lng_ref[g] + lnb_ref[g]
        if v_ref is not None:
            v_ref[:, gs] = vn
        ug = jax.nn.gelu(proj_scr[:, gs])
        vb = vn.astype(BF16)
        for c in range(rows // gchunk):
            rs = slice(c * gchunk, (c + 1) * gchunk)
            sp = _dot(ws_ref[g], vb[rs]) + bs_ref[g]
            mixed_scr[rs, gs] = ug[rs] * sp

    q0, f0, i0, g0 = (2 * A_DIM + k * B_DIM for k in range(4))
    lb = lb_ref[...]
    fg = lb + (1.0 - lb) * jax.nn.sigmoid(proj_scr[:, f0:f0 + B_DIM])
    proj_scr[:, 0:B_DIM] = jnp.log(fg)
    proj_scr[:, B_DIM:2 * B_DIM] = 1.0 - fg
    proj_scr[:, q0:q0 + B_DIM] = _silu(proj_scr[:, q0:q0 + B_DIM])
    og = og_ref[...]
    chunks_per_seq = seq_rows // CHUNK

    def chunk_body(c, carry):
        r0 = pl.multiple_of(c * CHUNK, CHUNK)
        rs = pl.ds(r0, CHUNK)
        sidx = (c // chunks_per_seq) * B_HEADS
        hi, mid, lo = _split3(proj_scr[rs, 0:B_DIM])
        gg = _dot(wcum_ref[...], jnp.concatenate([hi, mid, lo], axis=0))
        for hd in range(B_HEADS):
            hs = slice(hd * B_HDIM, (hd + 1) * B_HDIM)
            G = gg[0:CHUNK, hs]
            q = proj_scr[rs, q0 + hd * B_HDIM:q0 + (hd + 1) * B_HDIM]
            k = proj_scr[rs, B_DIM + hd * B_HDIM:B_DIM + (hd + 1) * B_HDIM]
            v = proj_scr[rs, i0 + hd * B_HDIM:i0 + (hd + 1) * B_HDIM]
            vb = v.astype(BF16)
            st = st_scr[sidx + hd]
            o = _dot_nt((q * jnp.exp(G)).astype(BF16), st.astype(BF16))
            att = jnp.zeros((CHUNK, CHUNK), F32)
            for l in range(N_LEVELS + 1):
                if l < N_LEVELS:
                    e = jnp.exp(-jnp.abs(G - gg[(l + 1) * CHUNK:(l + 2) * CHUNK, hs]))
                    a = _dot_nt((q * e).astype(BF16), (k * e).astype(BF16))
                else:
                    a = _dot_nt(q.astype(BF16), k.astype(BF16))
                att = jnp.where(lmask_ref[l] > 0.5, a, att)
            o = o + _dot(att.astype(BF16), vb)
            g_end = G[CHUNK - 1:CHUNK, :]
            kd = (k * jnp.exp(g_end - G)).astype(BF16)
            st_scr[sidx + hd] = st * jnp.exp(g_end) + _dot(v.T.astype(BF16), kd)
            on = o * lax.rsqrt(jnp.mean(o * o, axis=-1, keepdims=True) + EPS) * og
            gate = _silu(proj_scr[rs, g0 + hd * B_HDIM:g0 + (hd + 1) * B_HDIM])
            mixed_scr[rs, A_DIM + hd * B_HDIM:A_DIM + (hd + 1) * B_HDIM] = on * gate
        return carry

    lax.fori_loop(0, rows // CHUNK, chunk_body, 0, unroll=4)

    y_ref[...] = x + _dot(mixed_scr[...].astype(BF16), wout_ref[...])

    @pl.when(j == pl.num_programs(1) - 1)
    def _():
        for s in range(n_seq):
            for hd in range(B_HEADS):
                sout_ref[s, hd] = st_scr[s * B_HEADS + hd].T


def _even_mixer(x, s0, consts, *, n_batch, tiles_per_batch, n_seq, seq_rows, gchunk, emit_v):
    rows = n_seq * seq_rows
    grid = (n_batch, tiles_per_batch)
    tile = lambda b, j: (b * tiles_per_batch + j, 0)
    whole = lambda *shape: pl.BlockSpec(shape, lambda b, j: (0,) * len(shape))
    in_specs = [
        pl.BlockSpec((rows, D_MODEL), tile),
        pl.BlockSpec((n_seq, B_HEADS, B_HDIM, B_HDIM), lambda b, j: (b, 0, 0, 0)),
        whole(1, D_MODEL), whole(D_MODEL, EVEN_IN), whole(D_MODEL, D_MODEL),
        whole(A_GROUPS, gchunk, gchunk), whole(A_GROUPS, gchunk, LANES),
        whole(A_GROUPS, 1, A_GDIM), whole(A_GROUPS, 1, A_GDIM), whole(1, B_DIM), whole(1, B_HDIM),
        whole((N_LEVELS + 1) * CHUNK, 3 * CHUNK), whole(N_LEVELS + 1, CHUNK, CHUNK),
    ]
    args = [x, s0] + list(consts)
    out_shape = [jax.ShapeDtypeStruct(x.shape, F32),
                 jax.ShapeDtypeStruct((n_batch * n_seq, B_HEADS, B_HDIM, B_HDIM), F32)]
    out_specs = [pl.BlockSpec((rows, D_MODEL), tile),
                 pl.BlockSpec((n_seq, B_HEADS, B_HDIM, B_HDIM), lambda b, j: (b, 0, 0, 0))]
    if emit_v:
        out_shape.append(jax.ShapeDtypeStruct((x.shape[0], A_DIM), F32))
        out_specs.append(pl.BlockSpec((rows, A_DIM), tile))
    body = functools.partial(_even_mixer_body, n_seq=n_seq, seq_rows=seq_rows, gchunk=gchunk, emit_v=emit_v)
    return pl.pallas_call(
        body, grid=grid, in_specs=in_specs, out_specs=out_specs, out_shape=out_shape,
        scratch_shapes=[pltpu.VMEM((rows, EVEN_IN), F32), pltpu.VMEM((rows, D_MODEL), F32),
                        pltpu.VMEM((n_seq * B_HEADS, B_HDIM, B_HDIM), F32)],
        compiler_params=_cparams(2),
        name="even_mixer_s" if emit_v else "even_mixer_p")(*args)


def _memory_kv_body(mem_ref, gm_ref, wk_ref, wv_ref, kg_ref, mk_ref, mv_ref, *, n_seq, n_mem):
    m = _rms(mem_ref[...], gm_ref[0]).astype(BF16)
    kk = _dot(m, wk_ref[0])
    vv = _dot(m, wv_ref[0])
    kg = kg_ref[0]
    for hd in range(X_HEADS):
        hs = slice(hd * X_HDIM, (hd + 1) * X_HDIM)
        kh = _rms(kk[:, hs], kg)
        for s in range(n_seq):
            mk_ref[0, s, :, hs] = kh[s * n_mem:(s + 1) * n_mem]
    for s in range(n_seq):
        mv_ref[0, s] = vv[s * n_mem:(s + 1) * n_mem]


def _memory_kv(mem2d, g_mem, w_k, w_v, k_g, *, n_batch, n_mem):
    depth = w_k.shape[0]
    n_seq = ROW_TILE // n_mem
    out = jax.ShapeDtypeStruct((depth, n_batch, n_mem, X_DIM), F32)
    ospec = pl.BlockSpec((1, n_seq, n_mem, X_DIM), lambda l, t: (l, t, 0, 0))
    return pl.pallas_call(
        functools.partial(_memory_kv_body, n_seq=n_seq, n_mem=n_mem),
        grid=(depth, n_batch // n_seq),
        in_specs=[pl.BlockSpec((ROW_TILE, D_MODEL), lambda l, t: (t, 0)),
                  pl.BlockSpec((1, 1, D_MODEL), lambda l, t: (l, 0, 0)),
                  pl.BlockSpec((1, D_MODEL, X_DIM), lambda l, t: (l, 0, 0)),
                  pl.BlockSpec((1, D_MODEL, X_DIM), lambda l, t: (l, 0, 0)),
                  pl.BlockSpec((1, 1, X_HDIM), lambda l, t: (l, 0, 0))],
        out_specs=[ospec, ospec], out_shape=[out, out], compiler_params=_cparams(2),
        name="memory_kv")(mem2d, g_mem, w_k, w_v, k_g)


def _xattn_body(x_ref, mk_ref, mv_ref, nrm_ref, wq_ref, wo_ref, qg_ref, y_ref, o_scr, *, n_seq, seq_rows):
    x = x_ref[...]
    h = _rms(x, nrm_ref[...]).astype(BF16)
    q = _dot(h, wq_ref[...])
    qg = qg_ref[...] * (X_HDIM ** -0.5)
    for s in range(n_seq):
        rs = slice(s * seq_rows, (s + 1) * seq_rows)
        for hd in range(X_HEADS):
            hs = slice(hd * X_HDIM, (hd + 1) * X_HDIM)
            qh = _rms(q[rs, hs], qg).astype(BF16)
            sc = _dot_nt(qh, mk_ref[s, :, hs].astype(BF16))
            p = jnp.exp(sc - jnp.max(sc, axis=-1, keepdims=True))
            den = jnp.sum(p, axis=-1, keepdims=True)
            o_scr[rs, hs] = _dot(p.astype(BF16), mv_ref[s, :, hs].astype(BF16)) / den
    y_ref[...] = x + _dot(o_scr[...].astype(BF16), wo_ref[...])


def _xattn(x, mk, mv, nrm, wq, wo, qg, *, steps_per_mem, n_seq, seq_rows):
    rows = n_seq * seq_rows
    n_mem = mk.shape[1]
    whole = lambda *shape: pl.BlockSpec(shape, lambda t: (0,) * len(shape))
    return pl.pallas_call(
        functools.partial(_xattn_body, n_seq=n_seq, seq_rows=seq_rows),
        grid=(x.shape[0] // rows,),
        in_specs=[pl.BlockSpec((rows, D_MODEL), lambda t: (t, 0)),
                  pl.BlockSpec((n_seq, n_mem, X_DIM), lambda t: (t // steps_per_mem, 0, 0)),
                  pl.BlockSpec((n_seq, n_mem, X_DIM), lambda t: (t // steps_per_mem, 0, 0)),
                  whole(1, D_MODEL), whole(D_MODEL, X_DIM), whole(X_DIM, D_MODEL), whole(1, X_HDIM)],
        out_specs=pl.BlockSpec((rows, D_MODEL), lambda t: (t, 0)),
        out_shape=jax.ShapeDtypeStruct(x.shape, F32),
        scratch_shapes=[pltpu.VMEM((rows, X_DIM), F32)],
        compiler_params=_cparams(1), name="xattn_s" if n_seq > 1 else "xattn_p")(x, mk, mv, nrm, wq, wo, qg)


def _ff_chunk(i, j):
    return jnp.where(i % 2 == 0, j, N_FF_CHUNKS - 1 - j)


def _two_source_specs(n_prompt_tiles, width=D_MODEL):
    return (pl.BlockSpec((ROW_TILE, width), lambda i, *_: (jnp.minimum(i, n_prompt_tiles - 1), 0)),
            pl.BlockSpec((ROW_TILE, width), lambda i, *_: (jnp.maximum(i - n_prompt_tiles, 0), 0)))


def _on_row_source(i, n_prompt_tiles, fn, prompt_refs, sample_refs):
    pl.when(i < n_prompt_tiles)(lambda: fn(*prompt_refs))
    pl.when(i >= n_prompt_tiles)(lambda: fn(*sample_refs))


def _dense_ffn_body(xp_ref, xs_ref, nrm_ref, w1_ref, w3_ref, w2_ref, yp_ref, ys_ref, h_scr, *, n_prompt_tiles):
    j = pl.program_id(1)

    def run(x_ref, y_ref):
        @pl.when(j == 0)
        def _():
            h_scr[...] = _rms(x_ref[...], nrm_ref[...]).astype(BF16)

        hb = h_scr[...]
        act = (_silu(_dot(hb, w1_ref[...])) * _dot(hb, w3_ref[...])).astype(BF16)
        part = _dot(act, w2_ref[...])

        @pl.when(j == 0)
        def _():
            y_ref[...] = x_ref[...] + part

        @pl.when(j > 0)
        def _():
            y_ref[...] += part

    _on_row_source(pl.program_id(0), n_prompt_tiles, run, (xp_ref, yp_ref), (xs_ref, ys_ref))


def _dense_ffn(xp, xs, nrm, w1, w3, w2):
    n_prompt_tiles = xp.shape[0] // ROW_TILE
    n_tiles = n_prompt_tiles + xs.shape[0] // ROW_TILE
    return pl.pallas_call(
        functools.partial(_dense_ffn_body, n_prompt_tiles=n_prompt_tiles), grid=(n_tiles, N_FF_CHUNKS),
        in_specs=[*_two_source_specs(n_prompt_tiles),
                  pl.BlockSpec((1, D_MODEL), lambda i, j: (0, 0)),
                  pl.BlockSpec((D_MODEL, FF_CHUNK), lambda i, j: (0, _ff_chunk(i, j))),
                  pl.BlockSpec((D_MODEL, FF_CHUNK), lambda i, j: (0, _ff_chunk(i, j))),
                  pl.BlockSpec((FF_CHUNK, D_MODEL), lambda i, j: (_ff_chunk(i, j), 0))],
        out_specs=list(_two_source_specs(n_prompt_tiles)),
        out_shape=[jax.ShapeDtypeStruct(xp.shape, F32), jax.ShapeDtypeStruct(xs.shape, F32)],
        scratch_shapes=[pltpu.VMEM((ROW_TILE, D_MODEL), BF16)],
        compiler_params=_cparams(2), name="dense_ffn")(xp, xs, nrm, w1, w3, w2)


def _grouped_ffn_body(te_ref, nu_ref, x_ref, w1_ref, w3_ref, w2_ref, y_ref):
    i = pl.program_id(0)
    j = pl.program_id(1)

    @pl.when(i < nu_ref[0])
    def _():
        hb = x_ref[...].astype(BF16)
        act = (_silu(_dot(hb, w1_ref[0])) * _dot(hb, w3_ref[0])).astype(BF16)
        part = _dot(act, w2_ref[0])

        @pl.when(j == 0)
        def _():
            y_ref[...] = part

        @pl.when(j > 0)
        def _():
            y_ref[...] += part

    @pl.when(i >= nu_ref[0])
    def _():
        y_ref[...] = jnp.zeros_like(y_ref)


def _grouped_ffn(xsorted, tile_expert, n_used, w1, w3, w2):
    n_tiles = xsorted.shape[0] // ROW_TILE

    def row_tile(i, j, te, nu):
        return (i, 0)

    def ff(i, j, nu):
        last = nu[0] - 1
        return jnp.where(i <= last, _ff_chunk(i, j), _ff_chunk(last, N_FF_CHUNKS - 1))

    grid_spec = pltpu.PrefetchScalarGridSpec(
        num_scalar_prefetch=2, grid=(n_tiles, N_FF_CHUNKS),
        in_specs=[pl.BlockSpec((ROW_TILE, D_MODEL), row_tile),
                  pl.BlockSpec((1, D_MODEL, FF_CHUNK), lambda i, j, te, nu: (te[i], 0, ff(i, j, nu))),
                  pl.BlockSpec((1, D_MODEL, FF_CHUNK), lambda i, j, te, nu: (te[i], 0, ff(i, j, nu))),
                  pl.BlockSpec((1, FF_CHUNK, D_MODEL), lambda i, j, te, nu: (te[i], ff(i, j, nu), 0))],
        out_specs=pl.BlockSpec((ROW_TILE, D_MODEL), row_tile))
    return pl.pallas_call(
        _grouped_ffn_body, grid_spec=grid_spec, out_shape=jax.ShapeDtypeStruct(xsorted.shape, F32),
        compiler_params=_cparams(2), name="grouped_ffn")(tile_expert, n_used, xsorted, w1, w3, w2)


def _rope_slab(xs, cos, sin_signed, first_half):
    rot = jnp.where(first_half, pltpu.roll(xs, LANES - C_HDIM // 2, 1), pltpu.roll(xs, C_HDIM // 2, 1))
    return xs * cos + rot * sin_signed


def _swa_body(sink_ref, x_ref, pk_ref, pv_ref, cos_ref, sin_ref, nrm_ref, win_ref, wout_ref, qg_ref, kg_ref, bd_ref,
              y_ref, ko_ref, vo_ref, k_scr, v_scr, q_scr, a_scr, *, n_seq, seq_rows, past_valid):
    rows = n_seq * seq_rows
    j = pl.program_id(1)
    tail = min(WINDOW, seq_rows)

    if past_valid:
        for s in range(n_seq):
            k_scr[s, 0:WINDOW] = pk_ref[s]
            v_scr[s, 0:WINDOW] = pv_ref[s]
    else:
        @pl.when(j == 0)
        def _():
            for s in range(n_seq):
                k_scr[s, 0:WINDOW] = jnp.zeros((WINDOW, C_KV_DIM), F32)
                v_scr[s, 0:WINDOW] = jnp.zeros((WINDOW, C_KV_DIM), F32)

        @pl.when(j > 0)
        def _():
            for s in range(n_seq):
                k_scr[s, 0:WINDOW] = k_scr[s, seq_rows:seq_rows + WINDOW]
                v_scr[s, 0:WINDOW] = v_scr[s, seq_rows:seq_rows + WINDOW]

    x = x_ref[...]
    h = _rms(x, nrm_ref[...]).astype(BF16)
    q_dim = C_HEADS * C_HDIM
    q = _dot(h, win_ref[:, 0:q_dim])
    k = _dot(h, win_ref[:, q_dim:q_dim + C_KV_DIM])
    v = _dot(h, win_ref[:, q_dim + C_KV_DIM:q_dim + 2 * C_KV_DIM])

    cos = cos_ref[...]
    sin_signed = sin_ref[...]
    first_half = (lax.broadcasted_iota(jnp.int32, (rows, LANES), 1) % C_HDIM) < (C_HDIM // 2)
    bd = bd_ref[...]
    qn = q * lax.rsqrt(_dot((q * q).astype(BF16), bd) + EPS) * qg_ref[...]
    kn = k * lax.rsqrt(_dot((k * k).astype(BF16), bd[0:C_KV_DIM, 0:C_KV_DIM]) + EPS) * kg_ref[...]
    scale = C_HDIM ** -0.5
    for sl in range(q_dim // LANES):
        ls = slice(sl * LANES, (sl + 1) * LANES)
        q_scr[:, ls] = _rope_slab(qn[:, ls], cos, sin_signed, first_half) * scale
    for sl in range(C_KV_DIM // LANES):
        ls = slice(sl * LANES, (sl + 1) * LANES)
        kr = _rope_slab(kn[:, ls], cos, sin_signed, first_half)
        for s in range(n_seq):
            k_scr[s, WINDOW:WINDOW + seq_rows, ls] = kr[s * seq_rows:(s + 1) * seq_rows]
    for s in range(n_seq):
        v_scr[s, WINDOW:WINDOW + seq_rows] = v[s * seq_rows:(s + 1) * seq_rows]
        ko_ref[s] = k_scr[s, WINDOW + seq_rows - tail:WINDOW + seq_rows]
        vo_ref[s] = v_scr[s, WINDOW + seq_rows - tail:WINDOW + seq_rows]

    chunks_per_seq = seq_rows // CHUNK
    n_keys = WINDOW + CHUNK
    key_idx = lax.broadcasted_iota(jnp.int32, (C_GROUP * CHUNK, n_keys), 1)
    row_grp = lax.broadcasted_iota(jnp.int32, (C_GROUP * CHUNK, 1), 0) // CHUNK

    def chunk_body(idx, carry):
        s = idx // chunks_per_seq
        c = idx % chunks_per_seq
        r0 = pl.multiple_of(idx * CHUNK, CHUNK)
        k0 = pl.multiple_of(c * CHUNK, CHUNK)
        kwin = k_scr[s, pl.ds(k0, n_keys), :]
        vwin = v_scr[s, pl.ds(k0, n_keys), :]
        for kvh in range(C_KV_HEADS):
            ks = slice(kvh * C_HDIM, (kvh + 1) * C_HDIM)
            qs = jnp.concatenate(
                [q_scr[pl.ds(r0, CHUNK), (kvh * C_GROUP + g) * C_HDIM:(kvh * C_GROUP + g + 1) * C_HDIM]
                 for g in range(C_GROUP)], axis=0)
            sc = _dot_nt(qs.astype(BF16), kwin[:, ks].astype(BF16))
            if not past_valid:
                sc = jnp.where((key_idx + k0 >= WINDOW) | (j > 0), sc, -jnp.inf)
            sink = jnp.zeros((C_GROUP * CHUNK, 1), F32)
            for g in range(C_GROUP):
                sink = jnp.where(row_grp == g, sink_ref[kvh * C_GROUP + g], sink)
            m = jnp.maximum(jnp.max(sc, axis=-1, keepdims=True), sink)
            p = jnp.exp(sc - m)
            den = jnp.sum(p, axis=-1, keepdims=True) + jnp.exp(sink - m)
            o = _dot(p.astype(BF16), vwin[:, ks].astype(BF16)) / den
            for g in range(C_GROUP):
                hs = slice((kvh * C_GROUP + g) * C_HDIM, (kvh * C_GROUP + g + 1) * C_HDIM)
                a_scr[pl.ds(r0, CHUNK), hs] = o[g * CHUNK:(g + 1) * CHUNK]
        return carry

    lax.fori_loop(0, rows // CHUNK, chunk_body, 0)
    y_ref[...] = x + _dot(a_scr[...].astype(BF16), wout_ref[...])


def _swa(x, pk, pv, cos, sin_signed, sinks, consts, *, n_batch, tiles_per_batch, n_seq, seq_rows, past_valid):
    rows = n_seq * seq_rows
    tail = min(WINDOW, seq_rows)
    n_str = n_batch * n_seq
    whole = lambda *shape: pl.BlockSpec(shape, lambda b, j, sk: (0,) * len(shape))
    in_specs = [pl.BlockSpec((rows, D_MODEL), lambda b, j, sk: (b * tiles_per_batch + j, 0)),
                pl.BlockSpec((n_seq, WINDOW, C_KV_DIM), lambda b, j, sk: (b, 0, 0)),
                pl.BlockSpec((n_seq, WINDOW, C_KV_DIM), lambda b, j, sk: (b, 0, 0)),
                pl.BlockSpec((rows, LANES), lambda b, j, sk: (j, 0)),
                pl.BlockSpec((rows, LANES), lambda b, j, sk: (j, 0)),
                whole(1, D_MODEL), whole(D_MODEL, ODD_IN), whole(D_MODEL, D_MODEL),
                whole(1, D_MODEL), whole(1, C_KV_DIM), whole(D_MODEL, D_MODEL)]
    args = [x, pk, pv, cos, sin_signed] + list(consts)
    kv_out = jax.ShapeDtypeStruct((n_str, tail, C_KV_DIM), F32)
    kv_spec = pl.BlockSpec((n_seq, tail, C_KV_DIM), lambda b, j, sk: (b, 0, 0))
    grid_spec = pltpu.PrefetchScalarGridSpec(
        num_scalar_prefetch=1, grid=(n_batch, tiles_per_batch), in_specs=in_specs,
        out_specs=[pl.BlockSpec((rows, D_MODEL), lambda b, j, sk: (b * tiles_per_batch + j, 0)),
                   kv_spec, kv_spec],
        scratch_shapes=[pltpu.VMEM((n_seq, WINDOW + seq_rows, C_KV_DIM), F32),
                        pltpu.VMEM((n_seq, WINDOW + seq_rows, C_KV_DIM), F32),
                        pltpu.VMEM((rows, D_MODEL), F32), pltpu.VMEM((rows, D_MODEL), F32)])
    return pl.pallas_call(
        functools.partial(_swa_body, n_seq=n_seq, seq_rows=seq_rows, past_valid=past_valid),
        grid_spec=grid_spec,
        out_shape=[jax.ShapeDtypeStruct(x.shape, F32), kv_out, kv_out],
        compiler_params=_cparams(2),
        name="swa_s" if past_valid else "swa_p")(sinks, *args)


def _router_body(xp_ref, xs_ref, nrm_ref, rhi_ref, rlo_ref, lstrict_ref, su_ref, info_ref, cnt_ref, *,
                 n_prompt_tiles):
    def run(x_ref):
        h = _rms(x_ref[...], nrm_ref[...])
        hi = h.astype(BF16)
        lo = (h - hi.astype(F32)).astype(BF16)
        logits = _dot(hi, rhi_ref[...]) + _dot(lo, rhi_ref[...]) + _dot(hi, rlo_ref[...])
        lane = lax.broadcasted_iota(jnp.int32, logits.shape, 1).astype(F32)
        logits = jnp.where(lane < N_EXPERTS, logits, -jnp.inf)
        l1 = jnp.max(logits, axis=-1, keepdims=True)
        i1 = jnp.min(jnp.where(logits == l1, lane, float(LANES)), axis=-1, keepdims=True)
        rest = jnp.where(lane == i1, -jnp.inf, logits)
        l2 = jnp.max(rest, axis=-1, keepdims=True)
        i2 = jnp.min(jnp.where(rest == l2, lane, float(LANES)), axis=-1, keepdims=True)
        e = jnp.exp(l2 - l1)
        g1 = 1.0 / (1.0 + e)
        g2 = e * g1
        oh1 = (lane == i1).astype(F32)
        oh2 = (lane == i2).astype(F32)
        cnt1 = jnp.sum(oh1, axis=0, keepdims=True)
        cnt2 = jnp.sum(oh2, axis=0, keepdims=True)
        seg = jnp.broadcast_to(jnp.ceil((cnt1 + cnt2) * (1.0 / GROUP)), (GROUP, LANES))
        off = _dot(seg.astype(BF16), su_ref[...])[0:1] * GROUP
        c1 = _dot(lstrict_ref[...], oh1.astype(BF16)) + off
        c2 = _dot(lstrict_ref[...], oh2.astype(BF16)) + off + cnt1
        d1 = jnp.sum(jnp.where(lane == i1, c1, 0.0), axis=-1, keepdims=True)
        d2 = jnp.sum(jnp.where(lane == i2, c2, 0.0), axis=-1, keepdims=True)
        info = jnp.zeros(logits.shape, F32)
        for n, col in enumerate([d1, d2, g1, g2]):
            info = jnp.where(lane == n, col, info)
        info_ref[...] = info
        cnt_ref[...] = seg * GROUP

    _on_row_source(pl.program_id(0), n_prompt_tiles, run, (xp_ref,), (xs_ref,))


def _router(xp, xs, nrm, rhi, rlo, lstrict, su):
    n_prompt_tiles = xp.shape[0] // ROW_TILE
    n_tiles = n_prompt_tiles + xs.shape[0] // ROW_TILE
    whole = lambda *shape: pl.BlockSpec(shape, lambda i: (0,) * len(shape))
    return pl.pallas_call(
        functools.partial(_router_body, n_prompt_tiles=n_prompt_tiles), grid=(n_tiles,),
        in_specs=[*_two_source_specs(n_prompt_tiles), whole(1, D_MODEL), whole(D_MODEL, LANES), whole(D_MODEL, LANES),
                  whole(ROW_TILE, ROW_TILE), whole(LANES, LANES)],
        out_specs=[pl.BlockSpec((ROW_TILE, LANES), lambda i: (i, 0)), pl.BlockSpec((GROUP, LANES), lambda i: (i, 0))],
        out_shape=[jax.ShapeDtypeStruct((n_tiles * ROW_TILE, LANES), F32),
                   jax.ShapeDtypeStruct((n_tiles * GROUP, LANES), F32)],
        compiler_params=_cparams(1), name="router")(xp, xs, nrm, rhi, rlo, lstrict, su)


def _group_copy(src_ref, src_row, dst_ref, dst_row, sem):
    return pltpu.make_async_copy(src_ref.at[pl.ds(pl.multiple_of(src_row, GROUP), GROUP)],
                                 dst_ref.at[pl.ds(pl.multiple_of(dst_row, GROUP), GROUP)], sem)


def _wait_groups(n, src_ref, dst_ref, sem):
    def wait(g, carry):
        _group_copy(src_ref, 0, dst_ref, 0, sem).wait()
        return carry

    lax.fori_loop(0, n, wait, 0)


def _segment_table(seg, n_sorted_tiles):
    local_off = jnp.cumsum(seg, axis=1) - seg
    expert_rows = jnp.sum(seg, axis=0)
    expert_pad = ((expert_rows + ROW_TILE - 1) // ROW_TILE) * ROW_TILE
    expert_end = jnp.cumsum(expert_pad)
    expert_off = expert_end - expert_pad
    seg_start = expert_off[None, :] + jnp.cumsum(seg, axis=0) - seg
    n_used = (expert_end[-1] // ROW_TILE).reshape(1)
    tile_ids = jnp.arange(n_sorted_tiles, dtype=jnp.int32)
    tile_expert = jnp.minimum(jnp.sum(tile_ids[:, None] >= (expert_end // ROW_TILE)[None, :], axis=1), N_EXPERTS - 1)
    tile_expert = jnp.where(tile_ids < n_used[0], tile_expert, tile_expert[n_used[0] - 1])
    table = jnp.concatenate([
        jnp.stack([seg_start, seg // GROUP, local_off], axis=-1).reshape(-1),
        jnp.sum(seg, axis=1) // GROUP,
        jnp.stack([expert_off + expert_rows, (expert_pad - expert_rows) // GROUP], axis=-1).reshape(-1),
        n_used])
    return table.astype(jnp.int32), tile_expert.astype(jnp.int32), n_used.astype(jnp.int32)


def _dispatch_body(tab_ref, info_ref, xp_ref, xs_ref, nrm_ref, sel_ref, out_ref, loc_scr, z_scr, sems, *,
                   n_prompt_tiles, n_tiles, n_sorted_tiles):
    i = pl.program_id(0)
    slot = i % 2
    tile_groups0 = 3 * n_tiles * N_EXPERTS
    tails0 = tile_groups0 + n_tiles

    @pl.when(i >= 2)
    def _():
        _wait_groups(tab_ref[tile_groups0 + i - 2], loc_scr.at[slot], out_ref, sems.at[slot])

    def run(x_ref):
        hb = _rms(x_ref[...], nrm_ref[...]).astype(BF16)
        info = info_ref[...]
        lane = lax.broadcasted_iota(jnp.int32, info.shape, 1)
        dest = jnp.where(lane < TOP_K, info, 0.0)
        drow = sum(_dot_nt(sel_ref[...], part) for part in _split3(dest))
        r = lax.broadcasted_iota(jnp.int32, (LOCAL_ROWS, ROW_TILE), 0).astype(F32)
        perm = jnp.where(r == drow[0:1], 1.0, jnp.where(r == drow[1:2], 1.0, 0.0)).astype(BF16)
        loc_scr[slot] = _dot(perm, hb)

    _on_row_source(i, n_prompt_tiles, run, (xp_ref,), (xs_ref,))

    for e in range(N_EXPERTS):
        base = 3 * (i * N_EXPERTS + e)
        dst0, n_groups, src0 = tab_ref[base], tab_ref[base + 1], tab_ref[base + 2]

        def start(g, carry):
            _group_copy(loc_scr.at[slot], src0 + g * GROUP, out_ref, dst0 + g * GROUP, sems.at[slot]).start()
            return carry

        lax.fori_loop(0, n_groups, start, 0)

    @pl.when(i == n_tiles - 1)
    def _():
        _wait_groups(tab_ref[tile_groups0 + i], loc_scr.at[slot], out_ref, sems.at[slot])
        if n_tiles > 1:
            _wait_groups(tab_ref[tile_groups0 + i - 1], loc_scr.at[1 - slot], out_ref, sems.at[1 - slot])
        z_scr[...] = jnp.zeros_like(z_scr)
        for e in range(N_EXPERTS):
            row0, n_groups = tab_ref[tails0 + 2 * e], tab_ref[tails0 + 2 * e + 1]

            def start(g, carry):
                _group_copy(z_scr, 0, out_ref, row0 + g * GROUP, sems.at[2]).start()
                return carry

            lax.fori_loop(0, n_groups, start, 0)
            _wait_groups(n_groups, z_scr, out_ref, sems.at[2])

        def zero_tile(t, carry):
            cp = pltpu.make_async_copy(z_scr, out_ref.at[pl.ds(pl.multiple_of(t * ROW_TILE, ROW_TILE), ROW_TILE)],
                                       sems.at[2])
            cp.start()
            cp.wait()
            return carry

        lax.fori_loop(tab_ref[tails0 + 2 * N_EXPERTS], n_sorted_tiles, zero_tile, 0)


def _dispatch(table, info, xp, xs, nrm, sel, n_sorted_tiles):
    n_prompt_tiles = xp.shape[0] // ROW_TILE
    n_tiles = n_prompt_tiles + xs.shape[0] // ROW_TILE
    whole = lambda *shape: pl.BlockSpec(shape, lambda i, tab: (0,) * len(shape))
    grid_spec = pltpu.PrefetchScalarGridSpec(
        num_scalar_prefetch=1, grid=(n_tiles,),
        in_specs=[pl.BlockSpec((ROW_TILE, LANES), lambda i, tab: (i, 0)), *_two_source_specs(n_prompt_tiles),
                  whole(1, D_MODEL), whole(GROUP, LANES)],
        out_specs=pl.BlockSpec(memory_space=pl.ANY),
        scratch_shapes=[pltpu.VMEM((2, LOCAL_ROWS, D_MODEL), F32), pltpu.VMEM((ROW_TILE, D_MODEL), F32),
                        pltpu.SemaphoreType.DMA((3,))])
    return pl.pallas_call(
        functools.partial(_dispatch_body, n_prompt_tiles=n_prompt_tiles, n_tiles=n_tiles,
                          n_sorted_tiles=n_sorted_tiles),
        grid_spec=grid_spec, out_shape=jax.ShapeDtypeStruct((n_sorted_tiles * ROW_TILE, D_MODEL), F32),
        compiler_params=_cparams(1), name="moe_dispatch")(table, info, xp, xs, nrm, sel)


def _combine_body(tab_ref, info_ref, xp_ref, xs_ref, e_ref, yp_ref, ys_ref, loc_scr, sems, *, n_prompt_tiles,
                  n_tiles):
    i = pl.program_id(0)
    slot = i % 2
    tile_groups0 = 3 * n_tiles * N_EXPERTS

    def fetch(t, s):
        for e in range(N_EXPERTS):
            base = 3 * (t * N_EXPERTS + e)
            src0, n_groups, dst0 = tab_ref[base], tab_ref[base + 1], tab_ref[base + 2]

            def start(g, carry):
                _group_copy(e_ref, src0 + g * GROUP, loc_scr.at[s], dst0 + g * GROUP, sems.at[s]).start()
                return carry

            lax.fori_loop(0, n_groups, start, 0)

    @pl.when(i == 0)
    def _():
        loc_scr[...] = jnp.zeros_like(loc_scr)
        fetch(0, 0)

    @pl.when(i + 1 < n_tiles)
    def _():
        fetch(i + 1, 1 - slot)

    _wait_groups(tab_ref[tile_groups0 + i], e_ref, loc_scr.at[slot], sems.at[slot])

    def run(x_ref, y_ref):
        info = info_ref[...]
        eb = loc_scr[slot].astype(BF16)
        lane = lax.broadcasted_iota(jnp.int32, (ROW_TILE, LOCAL_ROWS), 1).astype(F32)
        y = x_ref[...]
        for k in range(TOP_K):
            pick = jnp.where(lane == info[:, k:k + 1], 1.0, 0.0).astype(BF16)
            y = y + info[:, TOP_K + k:TOP_K + k + 1] * _dot(pick, eb)
        y_ref[...] = y

    _on_row_source(i, n_prompt_tiles, run, (xp_ref, yp_ref), (xs_ref, ys_ref))


def _combine(table, info, xp, xs, esorted):
    n_prompt_tiles = xp.shape[0] // ROW_TILE
    n_tiles = n_prompt_tiles + xs.shape[0] // ROW_TILE
    grid_spec = pltpu.PrefetchScalarGridSpec(
        num_scalar_prefetch=1, grid=(n_tiles,),
        in_specs=[pl.BlockSpec((ROW_TILE, LANES), lambda i, tab: (i, 0)), *_two_source_specs(n_prompt_tiles),
                  pl.BlockSpec(memory_space=pl.ANY)],
        out_specs=list(_two_source_specs(n_prompt_tiles)),
        scratch_shapes=[pltpu.VMEM((2, LOCAL_ROWS, D_MODEL), F32), pltpu.SemaphoreType.DMA((2,))])
    return pl.pallas_call(
        functools.partial(_combine_body, n_prompt_tiles=n_prompt_tiles, n_tiles=n_tiles),
        grid_spec=grid_spec,
        out_shape=[jax.ShapeDtypeStruct(xp.shape, F32), jax.ShapeDtypeStruct(xs.shape, F32)],
        compiler_params=_cparams(1), name="moe_combine")(table, info, xp, xs, esorted)


def _rope_tables(pos):
    half = C_HDIM // 2
    inv = 1.0 / (ROPE_THETA ** (jnp.arange(half, dtype=F32) / half))
    ang = pos.astype(F32)[:, None] * inv[None, :]
    cos = jnp.cos(ang)
    sin = jnp.sin(ang)
    reps = LANES // C_HDIM
    return jnp.tile(jnp.concatenate([cos, cos], axis=-1), (1, reps)), jnp.tile(jnp.concatenate([-sin, sin], axis=-1),
                                                                                 (1, reps))


def kernel(x_prompt, x_sample, mem_prompt, cache_mem_k, cache_mem_v, state_hgrn, cache_swa_k, cache_swa_v, norm_mix, norm_xattn, norm_ffn, even_w_in, even_w_out, gmlp_w_s, gmlp_b_s, gmlp_ln_g, gmlp_ln_b, hgrn_lb_logits, hgrn_out_norm, attn_w_in, attn_w_out, attn_q_norm, attn_k_norm, attn_sinks, xattn_mem_norm, xattn_w_q, xattn_w_k, xattn_w_v, xattn_w_o, xattn_q_norm, xattn_k_norm, ffn_w1, ffn_w3, ffn_w2, moe_router, moe_w1, moe_w3, moe_w2):
    n_batch, seq, d = x_prompt.shape
    dec_batch, dec_seq, _ = x_sample.shape
    n_mem = mem_prompt.shape[1]
    depth = norm_mix.shape[0]
    past_len = PAST_LEN
    assert d == D_MODEL and depth == 2 and seq % ROW_TILE == 0 and dec_batch * dec_seq == ROW_TILE
    assert dec_seq == CHUNK and ROW_TILE % n_mem == 0 and cache_swa_k.shape[2] == WINDOW
    assert even_w_in.shape[-1] == EVEN_IN and attn_w_in.shape[-1] == ODD_IN and ffn_w1.shape[-1] == D_FF
    assert moe_w1.shape[1] == N_EXPERTS
    n_prompt_rows = n_batch * seq
    n_rows = n_prompt_rows + dec_batch * dec_seq
    n_prompt_tiles = n_prompt_rows // ROW_TILE
    tiles_per_batch = seq // ROW_TILE
    bf = lambda w: w.astype(BF16)
    row = lambda g: g.reshape(1, -1).astype(F32)

    lb_all = jnp.cumsum(jax.nn.softmax(hgrn_lb_logits.astype(F32), axis=0), axis=0)
    wcum = jnp.asarray(_cumsum_matrix(), BF16)
    lmask = jnp.asarray(_level_masks(), F32)

    def even_consts(n):
        tril = jnp.tril(jnp.ones((n, n), bool))
        ws = jnp.where(tril[None], gmlp_w_s[0, :, :n, :n], 0.0).astype(BF16)
        bs = jnp.broadcast_to(gmlp_b_s[0, :, :n, None], (A_GROUPS, n, LANES)).astype(F32)
        return [row(norm_mix[0]), bf(even_w_in[0]), bf(even_w_out[0]), ws, bs,
                gmlp_ln_g[0].reshape(A_GROUPS, 1, A_GDIM), gmlp_ln_b[0].reshape(A_GROUPS, 1, A_GDIM),
                row(lb_all[0]), row(hgrn_out_norm[0]), wcum, lmask]

    xp, hgrn_p = _even_mixer(
        x_prompt.reshape(n_prompt_rows, d), jnp.zeros((n_batch, B_HEADS, B_HDIM, B_HDIM), F32),
        even_consts(A_CHUNK), n_batch=n_batch, tiles_per_batch=tiles_per_batch, n_seq=1, seq_rows=ROW_TILE,
        gchunk=A_CHUNK, emit_v=False)
    xs, hgrn_s, gmlp_v = _even_mixer(
        x_sample.reshape(ROW_TILE, d), state_hgrn[0], even_consts(min(A_CHUNK, dec_seq)), n_batch=1,
        tiles_per_batch=1, n_seq=dec_batch, seq_rows=dec_seq, gchunk=min(A_CHUNK, dec_seq), emit_v=True)

    mem_k, mem_v = _memory_kv(mem_prompt.reshape(n_batch * n_mem, d), xattn_mem_norm.reshape(depth, 1, d),
                              bf(xattn_w_k), bf(xattn_w_v), xattn_k_norm.reshape(depth, 1, X_HDIM),
                              n_batch=n_batch, n_mem=n_mem)

    def cross_attention(xp, xs, l):
        consts = (row(norm_xattn[l]), bf(xattn_w_q[l]), bf(xattn_w_o[l]), row(xattn_q_norm[l]))
        xp = _xattn(xp, mem_k[l], mem_v[l], *consts, steps_per_mem=tiles_per_batch, n_seq=1, seq_rows=ROW_TILE)
        xs = _xattn(xs, cache_mem_k[l].reshape(dec_batch, n_mem, X_DIM),
                    cache_mem_v[l].reshape(dec_batch, n_mem, X_DIM), *consts, steps_per_mem=1, n_seq=dec_batch,
                    seq_rows=dec_seq)
        return xp, xs

    xp, xs = cross_attention(xp, xs, 0)
    xp, xs = _dense_ffn(xp, xs, row(norm_ffn[0]), bf(ffn_w1[0]), bf(ffn_w3[0]), bf(ffn_w2[0]))

    reps = D_MODEL // C_HDIM
    bd = jnp.asarray(np.kron(np.eye(reps, dtype=np.float32), np.full((C_HDIM, C_HDIM), 1.0 / C_HDIM, np.float32)),
                     BF16)
    swa_consts = [row(norm_mix[1]), bf(attn_w_in[0]), bf(attn_w_out[0]), row(jnp.tile(attn_q_norm[0], C_HEADS)),
                  row(jnp.tile(attn_k_norm[0], C_KV_HEADS)), bd]
    sinks = attn_sinks[0].astype(F32)
    cos_p, sin_p = _rope_tables(jnp.arange(seq, dtype=jnp.int32))
    cos_s, sin_s = _rope_tables(past_len + jnp.arange(dec_seq, dtype=jnp.int32))
    no_past = jnp.zeros((n_batch, WINDOW, C_KV_DIM), F32)
    xp, swk_p, swv_p = _swa(xp, no_past, no_past, cos_p, sin_p, sinks, swa_consts, n_batch=n_batch,
                            tiles_per_batch=tiles_per_batch, n_seq=1, seq_rows=ROW_TILE, past_valid=False)
    xs, swk_s, swv_s = _swa(xs, cache_swa_k[0].reshape(dec_batch, WINDOW, C_KV_DIM),
                            cache_swa_v[0].reshape(dec_batch, WINDOW, C_KV_DIM),
                            jnp.tile(cos_s, (dec_batch, 1)), jnp.tile(sin_s, (dec_batch, 1)), sinks, swa_consts,
                            n_batch=1, tiles_per_batch=1, n_seq=dec_batch, seq_rows=dec_seq, past_valid=True)
    xp, xs = cross_attention(xp, xs, 1)

    n_tiles = n_rows // ROW_TILE
    router_w = jnp.zeros((d, LANES), F32).at[:, :N_EXPERTS].set(moe_router[0].astype(F32))
    rhi = router_w.astype(BF16)
    rlo = (router_w - rhi.astype(F32)).astype(BF16)
    lstrict = jnp.asarray(np.tril(np.ones((ROW_TILE, ROW_TILE), np.float32), -1), BF16)
    su = jnp.asarray(np.triu(np.ones((LANES, LANES), np.float32), 1), BF16)
    sel = jnp.asarray(np.eye(GROUP, LANES, dtype=np.float32), BF16)
    nrm_ffn = row(norm_ffn[1])
    info, seg = _router(xp, xs, nrm_ffn, rhi, rlo, lstrict, su)
    seg = seg.reshape(n_tiles, GROUP, LANES)[:, 0, :N_EXPERTS].astype(jnp.int32)
    n_sorted_tiles = -(-(TOP_K * n_rows + n_tiles * N_EXPERTS * (GROUP - 1) + N_EXPERTS * (ROW_TILE - 1)) // ROW_TILE)
    table, tile_expert, n_used = _segment_table(seg, n_sorted_tiles)
    hsorted = _dispatch(table, info, xp, xs, nrm_ffn, sel, n_sorted_tiles)
    esorted = _grouped_ffn(hsorted, tile_expert, n_used, bf(moe_w1[0]), bf(moe_w3[0]), bf(moe_w2[0]))
    y_prompt, y_sample = _combine(table, info, xp, xs, esorted)

    n_even = state_hgrn.shape[0]
    n_odd = cache_swa_k.shape[0]
    return (y_prompt.reshape(n_batch, seq, d), y_sample.reshape(dec_batch, dec_seq, d),
            mem_k.reshape(depth, n_batch, n_mem, X_HEADS, X_HDIM), mem_v.reshape(depth, n_batch, n_mem, X_HEADS, X_HDIM),
            hgrn_p.reshape(n_even, n_batch, B_HEADS, B_HDIM, B_HDIM),
            gmlp_v.reshape(n_even, dec_batch, dec_seq, A_GROUPS, A_GDIM),
            hgrn_s.reshape(n_even, dec_batch, B_HEADS, B_HDIM, B_HDIM),
            swk_p.reshape(n_odd, n_batch, WINDOW, C_KV_HEADS, C_HDIM), swv_p.reshape(n_odd, n_batch, WINDOW, C_KV_HEADS, C_HDIM),
            swk_s.reshape(n_odd, dec_batch, dec_seq, C_KV_HEADS, C_HDIM), swv_s.reshape(n_odd, dec_batch, dec_seq, C_KV_HEADS, C_HDIM))
```

```python
import functools

import numpy as np
import jax
import jax.numpy as jnp
from jax import lax
from jax.experimental import pallas as pl
from jax.experimental.pallas import tpu as pltpu

F32 = jnp.float32
BF16 = jnp.bfloat16

D_MODEL = 1024
EPS = 1e-6
CHUNK = 64
A_GROUPS = 4
A_DIM = D_MODEL // 2
A_GDIM = A_DIM // A_GROUPS
A_CHUNK = 128
B_HEADS = 4
B_DIM = D_MODEL // 2
B_HDIM = B_DIM // B_HEADS
EVEN_IN = 2 * A_DIM + 4 * B_DIM
C_HEADS = 16
C_KV_HEADS = 4
C_HDIM = D_MODEL // C_HEADS
C_GROUP = C_HEADS // C_KV_HEADS
C_KV_DIM = C_KV_HEADS * C_HDIM
WINDOW = 128
ROPE_THETA = 10000.0
PAST_LEN = 4096
ODD_IN = (C_HEADS + 2 * C_KV_HEADS) * C_HDIM
X_HEADS = 4
X_HDIM = 128
X_DIM = X_HEADS * X_HDIM
D_FF = 2816
N_EXPERTS = 8
TOP_K = 2

LANES = 128
ROW_TILE = 512
FF_CHUNK = 2816
N_FF_CHUNKS = D_FF // FF_CHUNK
FF_SUB = 512
N_LEVELS = 6
GROUP = 8
LOCAL_ROWS = -(-(TOP_K * ROW_TILE + N_EXPERTS * (GROUP - 1)) // LANES) * LANES
VMEM_LIMIT = 58 * 1024 * 1024


def _cparams(n_axes):
    return pltpu.CompilerParams(dimension_semantics=("arbitrary",) * n_axes, vmem_limit_bytes=VMEM_LIMIT)


def _dot(a, b):
    return jnp.dot(a, b, preferred_element_type=F32)


def _dot_nt(a, b):
    return lax.dot_general(a, b, (((1,), (1,)), ((), ())), preferred_element_type=F32)


def _rms(x, g):
    return x * lax.rsqrt(jnp.mean(x * x, axis=-1, keepdims=True) + EPS) * g


def _silu(x):
    return x * (1.0 / (1.0 + jnp.exp(-x)))


def _split3(x):
    hi = x.astype(BF16)
    r1 = x - hi.astype(F32)
    mid = r1.astype(BF16)
    lo = (r1 - mid.astype(F32)).astype(BF16)
    return hi, mid, lo


def _cumsum_matrix():
    r = np.arange(CHUNK)
    s = np.arange(CHUNK)
    blocks = [(s[None, :] <= r[:, None])]
    for l in range(N_LEVELS):
        h = 1 << l
        ref = (r & ~(2 * h - 1)) + h - 1
        blocks.append(s[None, :] <= ref[:, None])
    w = np.concatenate(blocks, axis=0).astype(np.float32)
    return np.concatenate([w, w, w], axis=1)


def _level_masks():
    t = np.arange(CHUNK)[:, None]
    s = np.arange(CHUNK)[None, :]
    masks = []
    for l in range(N_LEVELS):
        masks.append(((t >> (l + 1)) == (s >> (l + 1))) & (((t >> l) & 1) == 1) & (((s >> l) & 1) == 0))
    masks.append(t == s)
    return np.stack(masks).astype(np.float32)


def _even_mixer_body(x_ref, s0_ref, nrm_ref, win_ref, wout_ref, ws_ref, bs_ref, lng_ref, lnb_ref, lb_ref, og_ref,
                     wcum_ref, lmask_ref, *rest, n_seq, seq_rows, gchunk, emit_v):
    if emit_v:
        y_ref, sout_ref, v_ref, proj_scr, mixed_scr, st_scr = rest
    else:
        y_ref, sout_ref, proj_scr, mixed_scr, st_scr = rest
        v_ref = None
    rows = n_seq * seq_rows
    j = pl.program_id(1)

    @pl.when(j == 0)
    def _():
        for s in range(n_seq):
            for hd in range(B_HEADS):
                st_scr[s * B_HEADS + hd] = s0_ref[s, hd].T

    x = x_ref[...]
    h = _rms(x, nrm_ref[...]).astype(BF16)
    n_pieces = EVEN_IN // ROW_TILE
    for n in range(n_pieces):
        cs = slice(n * ROW_TILE, (n + 1) * ROW_TILE)
        proj_scr[:, cs] = _dot(h, win_ref[:, cs])

    for g in range(A_GROUPS):
        gs = slice(g * A_GDIM, (g + 1) * A_GDIM)
        vg = jax.nn.gelu(proj_scr[:, A_DIM + g * A_GDIM:A_DIM + (g + 1) * A_GDIM])
        vc = vg - jnp.mean(vg, axis=-1, keepdims=True)
        vn = vc * lax.rsqrt(jnp.mean(vc * vc, axis=-1, keepdims=True) + EPS) * lng_ref[g] + lnb_ref[g]
        if v_ref is not None:
            v_ref[:, gs] = vn
        ug = jax.nn.gelu(proj_scr[:, gs])
        vb = vn.astype(BF16)
        for c in range(rows // gchunk):
            rs = slice(c * gchunk, (c + 1) * gchunk)
            sp = _dot(ws_ref[g], vb[rs]) + bs_ref[g]
            mixed_scr[rs, gs] = ug[rs] * sp

    q0, f0, i0, g0 = (2 * A_DIM + k * B_DIM for k in range(4))
    lb = lb_ref[...]
    fg = lb + (1.0 - lb) * jax.nn.sigmoid(proj_scr[:, f0:f0 + B_DIM])
    proj_scr[:, 0:B_DIM] = jnp.log(fg)
    proj_scr[:, B_DIM:2 * B_DIM] = 1.0 - fg
    proj_scr[:, q0:q0 + B_DIM] = _silu(proj_scr[:, q0:q0 + B_DIM])
    og = og_ref[...]
    chunks_per_seq = seq_rows // CHUNK

    def chunk_body(c, carry):
        r0 = pl.multiple_of(c * CHUNK, CHUNK)
        rs = pl.ds(r0, CHUNK)
        sidx = (c // chunks_per_seq) * B_HEADS
        hi, mid, lo = _split3(proj_scr[rs, 0:B_DIM])
        gg = _dot(wcum_ref[...], jnp.concatenate([hi, mid, lo], axis=0))
        for hd in range(B_HEADS):
            hs = slice(hd * B_HDIM, (hd + 1) * B_HDIM)
            G = gg[0:CHUNK, hs]
            q = proj_scr[rs, q0 + hd * B_HDIM:q0 + (hd + 1) * B_HDIM]
            k = proj_scr[rs, B_DIM + hd * B_HDIM:B_DIM + (hd + 1) * B_HDIM]
            v = proj_scr[rs, i0 + hd * B_HDIM:i0 + (hd + 1) * B_HDIM]
            vb = v.astype(BF16)
            st = st_scr[sidx + hd]
            o = _dot_nt((q * jnp.exp(G)).astype(BF16), st.astype(BF16))
            att = jnp.zeros((CHUNK, CHUNK), F32)
            for l in range(N_LEVELS + 1):
                if l < N_LEVELS:
                    e = jnp.exp(-jnp.abs(G - gg[(l + 1) * CHUNK:(l + 2) * CHUNK, hs]))
                    a = _dot_nt((q * e).astype(BF16), (k * e).astype(BF16))
                else:
                    a = _dot_nt(q.astype(BF16), k.astype(BF16))
                att = jnp.where(lmask_ref[l] > 0.5, a, att)
            o = o + _dot(att.astype(BF16), vb)
            g_end = G[CHUNK - 1:CHUNK, :]
            kd = (k * jnp.exp(g_end - G)).astype(BF16)
            st_scr[sidx + hd] = st * jnp.exp(g_end) + _dot(v.T.astype(BF16), kd)
            on = o * lax.rsqrt(jnp.mean(o * o, axis=-1, keepdims=True) + EPS) * og
            gate = _silu(proj_scr[rs, g0 + hd * B_HDIM:g0 + (hd + 1) * B_HDIM])
            mixed_scr[rs, A_DIM + hd * B_HDIM:A_DIM + (hd + 1) * B_HDIM] = on * gate
        return carry

    lax.fori_loop(0, rows // CHUNK, chunk_body, 0, unroll=4)

    y_ref[...] = x + _dot(mixed_scr[...].astype(BF16), wout_ref[...])

    @pl.when(j == pl.num_programs(1) - 1)
    def _():
        for s in range(n_seq):
            for hd in range(B_HEADS):
                sout_ref[s, hd] = st_scr[s * B_HEADS + hd].T


def _even_mixer(x, s0, consts, *, n_batch, tiles_per_batch, n_seq, seq_rows, gchunk, emit_v):
    rows = n_seq * seq_rows
    grid = (n_batch, tiles_per_batch)
    tile = lambda b, j: (b * tiles_per_batch + j, 0)
    whole = lambda *shape: pl.BlockSpec(shape, lambda b, j: (0,) * len(shape))
    in_specs = [
        pl.BlockSpec((rows, D_MODEL), tile),
        pl.BlockSpec((n_seq, B_HEADS, B_HDIM, B_HDIM), lambda b, j: (b, 0, 0, 0)),
        whole(1, D_MODEL), whole(D_MODEL, EVEN_IN), whole(D_MODEL, D_MODEL),
        whole(A_GROUPS, gchunk, gchunk), whole(A_GROUPS, gchunk, LANES),
        whole(A_GROUPS, 1, A_GDIM), whole(A_GROUPS, 1, A_GDIM), whole(1, B_DIM), whole(1, B_HDIM),
        whole((N_LEVELS + 1) * CHUNK, 3 * CHUNK), whole(N_LEVELS + 1, CHUNK, CHUNK),
    ]
    args = [x, s0] + list(consts)
    out_shape = [jax.ShapeDtypeStruct(x.shape, F32),
                 jax.ShapeDtypeStruct((n_batch * n_seq, B_HEADS, B_HDIM, B_HDIM), F32)]
    out_specs = [pl.BlockSpec((rows, D_MODEL), tile),
                 pl.BlockSpec((n_seq, B_HEADS, B_HDIM, B_HDIM), lambda b, j: (b, 0, 0, 0))]
    if emit_v:
        out_shape.append(jax.ShapeDtypeStruct((x.shape[0], A_DIM), F32))
        out_specs.append(pl.BlockSpec((rows, A_DIM), tile))
    body = functools.partial(_even_mixer_body, n_seq=n_seq, seq_rows=seq_rows, gchunk=gchunk, emit_v=emit_v)
    return pl.pallas_call(
        body, grid=grid, in_specs=in_specs, out_specs=out_specs, out_shape=out_shape,
        scratch_shapes=[pltpu.VMEM((rows, EVEN_IN), F32), pltpu.VMEM((rows, D_MODEL), F32),
                        pltpu.VMEM((n_seq * B_HEADS, B_HDIM, B_HDIM), F32)],
        compiler_params=_cparams(2),
        name="even_mixer_s" if emit_v else "even_mixer_p")(*args)


def _memory_kv_body(mem_ref, gm_ref, wk_ref, wv_ref, kg_ref, mk_ref, mv_ref, *, n_seq, n_mem):
    m = _rms(mem_ref[...], gm_ref[0]).astype(BF16)
    kk = _dot(m, wk_ref[0])
    vv = _dot(m, wv_ref[0])
    kg = kg_ref[0]
    for hd in range(X_HEADS):
        hs = slice(hd * X_HDIM, (hd + 1) * X_HDIM)
        kh = _rms(kk[:, hs], kg)
        for s in range(n_seq):
            mk_ref[0, s, :, hs] = kh[s * n_mem:(s + 1) * n_mem]
    for s in range(n_seq):
        mv_ref[0, s] = vv[s * n_mem:(s + 1) * n_mem]


def _memory_kv(mem2d, g_mem, w_k, w_v, k_g, *, n_batch, n_mem):
    depth = w_k.shape[0]
    n_seq = ROW_TILE // n_mem
    out = jax.ShapeDtypeStruct((depth, n_batch, n_mem, X_DIM), F32)
    ospec = pl.BlockSpec((1, n_seq, n_mem, X_DIM), lambda l, t: (l, t, 0, 0))
    return pl.pallas_call(
        functools.partial(_memory_kv_body, n_seq=n_seq, n_mem=n_mem),
        grid=(depth, n_batch // n_seq),
        in_specs=[pl.BlockSpec((ROW_TILE, D_MODEL), lambda l, t: (t, 0)),
                  pl.BlockSpec((1, 1, D_MODEL), lambda l, t: (l, 0, 0)),
                  pl.BlockSpec((1, D_MODEL, X_DIM), lambda l, t: (l, 0, 0)),
                  pl.BlockSpec((1, D_MODEL, X_DIM), lambda l, t: (l, 0, 0)),
                  pl.BlockSpec((1, 1, X_HDIM), lambda l, t: (l, 0, 0))],
        out_specs=[ospec, ospec], out_shape=[out, out], compiler_params=_cparams(2),
        name="memory_kv")(mem2d, g_mem, w_k, w_v, k_g)


def _xattn_body(x_ref, mk_ref, mv_ref, nrm_ref, wq_ref, wo_ref, qg_ref, y_ref, o_scr, *, n_seq, seq_rows):
    x = x_ref[...]
    h = _rms(x, nrm_ref[...]).astype(BF16)
    q = _dot(h, wq_ref[...])
    qg = qg_ref[...] * (X_HDIM ** -0.5)
    for s in range(n_seq):
        rs = slice(s * seq_rows, (s + 1) * seq_rows)
        for hd in range(X_HEADS):
            hs = slice(hd * X_HDIM, (hd + 1) * X_HDIM)
            qh = _rms(q[rs, hs], qg).astype(BF16)
            sc = _dot_nt(qh, mk_ref[s, :, hs].astype(BF16))
            p = jnp.exp(sc - jnp.max(sc, axis=-1, keepdims=True))
            den = jnp.sum(p, axis=-1, keepdims=True)
            o_scr[rs, hs] = _dot(p.astype(BF16), mv_ref[s, :, hs].astype(BF16)) / den
    y_ref[...] = x + _dot(o_scr[...].astype(BF16), wo_ref[...])


def _xattn(x, mk, mv, nrm, wq, wo, qg, *, steps_per_mem, n_seq, seq_rows):
    rows = n_seq * seq_rows
    n_mem = mk.shape[1]
    whole = lambda *shape: pl.BlockSpec(shape, lambda t: (0,) * len(shape))
    return pl.pallas_call(
        functools.partial(_xattn_body, n_seq=n_seq, seq_rows=seq_rows),
        grid=(x.shape[0] // rows,),
        in_specs=[pl.BlockSpec((rows, D_MODEL), lambda t: (t, 0)),
                  pl.BlockSpec((n_seq, n_mem, X_DIM), lambda t: (t // steps_per_mem, 0, 0)),
                  pl.BlockSpec((n_seq, n_mem, X_DIM), lambda t: (t // steps_per_mem, 0, 0)),
                  whole(1, D_MODEL), whole(D_MODEL, X_DIM), whole(X_DIM, D_MODEL), whole(1, X_HDIM)],
        out_specs=pl.BlockSpec((rows, D_MODEL), lambda t: (t, 0)),
        out_shape=jax.ShapeDtypeStruct(x.shape, F32),
        scratch_shapes=[pltpu.VMEM((rows, X_DIM), F32)],
        compiler_params=_cparams(1), name="xattn_s" if n_seq > 1 else "xattn_p")(x, mk, mv, nrm, wq, wo, qg)


def _ff_chunk(i, j):
    return jnp.where(i % 2 == 0, j, N_FF_CHUNKS - 1 - j)


def _two_source_specs(n_prompt_tiles, width=D_MODEL):
    return (pl.BlockSpec((ROW_TILE, width), lambda i, *_: (jnp.minimum(i, n_prompt_tiles - 1), 0)),
            pl.BlockSpec((ROW_TILE, width), lambda i, *_: (jnp.maximum(i - n_prompt_tiles, 0), 0)))


def _on_row_source(i, n_prompt_tiles, fn, prompt_refs, sample_refs):
    pl.when(i < n_prompt_tiles)(lambda: fn(*prompt_refs))
    pl.when(i >= n_prompt_tiles)(lambda: fn(*sample_refs))


def _swiglu_part(hb, w1, w3, w2):
    part = None
    for c0 in range(0, FF_CHUNK, FF_SUB):
        cs = slice(c0, min(c0 + FF_SUB, FF_CHUNK))
        act = (_silu(_dot(hb, w1[:, cs])) * _dot(hb, w3[:, cs])).astype(BF16)
        p = _dot(act, w2[cs, :])
        part = p if part is None else part + p
    return part


def _dense_ffn_body(xp_ref, xs_ref, nrm_ref, w1_ref, w3_ref, w2_ref, yp_ref, ys_ref, h_scr, *, n_prompt_tiles):
    j = pl.program_id(1)

    def run(x_ref, y_ref):
        @pl.when(j == 0)
        def _():
            h_scr[...] = _rms(x_ref[...], nrm_ref[...]).astype(BF16)

        part = _swiglu_part(h_scr[...], w1_ref, w3_ref, w2_ref)

        @pl.when(j == 0)
        def _():
            y_ref[...] = x_ref[...] + part

        @pl.when(j > 0)
        def _():
            y_ref[...] += part

    _on_row_source(pl.program_id(0), n_prompt_tiles, run, (xp_ref, yp_ref), (xs_ref, ys_ref))


def _dense_ffn(xp, xs, nrm, w1, w3, w2):
    n_prompt_tiles = xp.shape[0] // ROW_TILE
    n_tiles = n_prompt_tiles + xs.shape[0] // ROW_TILE
    return pl.pallas_call(
        functools.partial(_dense_ffn_body, n_prompt_tiles=n_prompt_tiles), grid=(n_tiles, N_FF_CHUNKS),
        in_specs=[*_two_source_specs(n_prompt_tiles),
                  pl.BlockSpec((1, D_MODEL), lambda i, j: (0, 0)),
                  pl.BlockSpec((D_MODEL, FF_CHUNK), lambda i, j: (0, _ff_chunk(i, j))),
                  pl.BlockSpec((D_MODEL, FF_CHUNK), lambda i, j: (0, _ff_chunk(i, j))),
                  pl.BlockSpec((FF_CHUNK, D_MODEL), lambda i, j: (_ff_chunk(i, j), 0))],
        out_specs=list(_two_source_specs(n_prompt_tiles)),
        out_shape=[jax.ShapeDtypeStruct(xp.shape, F32), jax.ShapeDtypeStruct(xs.shape, F32)],
        scratch_shapes=[pltpu.VMEM((ROW_TILE, D_MODEL), BF16)],
        compiler_params=_cparams(2), name="dense_ffn")(xp, xs, nrm, w1, w3, w2)


def _grouped_ffn_body(te_ref, nu_ref, x_ref, w1_ref, w3_ref, w2_ref, y_ref):
    i = pl.program_id(0)
    j = pl.program_id(1)

    @pl.when(i < nu_ref[0])
    def _():
        part = _swiglu_part(x_ref[...].astype(BF16), w1_ref.at[0], w3_ref.at[0], w2_ref.at[0])

        @pl.when(j == 0)
        def _():
            y_ref[...] = part

        @pl.when(j > 0)
        def _():
            y_ref[...] += part

    @pl.when(i >= nu_ref[0])
    def _():
        y_ref[...] = jnp.zeros_like(y_ref)


def _grouped_ffn(xsorted, tile_expert, n_used, w1, w3, w2):
    n_tiles = xsorted.shape[0] // ROW_TILE

    def row_tile(i, j, te, nu):
        return (i, 0)

    def ff(i, j, nu):
        last = nu[0] - 1
        return jnp.where(i <= last, _ff_chunk(i, j), _ff_chunk(last, N_FF_CHUNKS - 1))

    grid_spec = pltpu.PrefetchScalarGridSpec(
        num_scalar_prefetch=2, grid=(n_tiles, N_FF_CHUNKS),
        in_specs=[pl.BlockSpec((ROW_TILE, D_MODEL), row_tile),
                  pl.BlockSpec((1, D_MODEL, FF_CHUNK), lambda i, j, te, nu: (te[i], 0, ff(i, j, nu))),
                  pl.BlockSpec((1, D_MODEL, FF_CHUNK), lambda i, j, te, nu: (te[i], 0, ff(i, j, nu))),
                  pl.BlockSpec((1, FF_CHUNK, D_MODEL), lambda i, j, te, nu: (te[i], ff(i, j, nu), 0))],
        out_specs=pl.BlockSpec((ROW_TILE, D_MODEL), row_tile))
    return pl.pallas_call(
        _grouped_ffn_body, grid_spec=grid_spec, out_shape=jax.ShapeDtypeStruct(xsorted.shape, F32),
        compiler_params=_cparams(2), name="grouped_ffn")(tile_expert, n_used, xsorted, w1, w3, w2)


def _rope_slab(xs, cos, sin_signed, first_half):
    rot = jnp.where(first_half, pltpu.roll(xs, LANES - C_HDIM // 2, 1), pltpu.roll(xs, C_HDIM // 2, 1))
    return xs * cos + rot * sin_signed


def _swa_body(sink_ref, x_ref, pk_ref, pv_ref, cos_ref, sin_ref, nrm_ref, win_ref, wout_ref, qg_ref, kg_ref, bd_ref,
              y_ref, ko_ref, vo_ref, k_scr, v_scr, q_scr, a_scr, *, n_seq, seq_rows, past_valid):
    rows = n_seq * seq_rows
    j = pl.program_id(1)
    tail = min(WINDOW, seq_rows)

    if past_valid:
        for s in range(n_seq):
            k_scr[s, 0:WINDOW] = pk_ref[s]
            v_scr[s, 0:WINDOW] = pv_ref[s]
    else:
        @pl.when(j == 0)
        def _():
            for s in range(n_seq):
                k_scr[s, 0:WINDOW] = jnp.zeros((WINDOW, C_KV_DIM), F32)
                v_scr[s, 0:WINDOW] = jnp.zeros((WINDOW, C_KV_DIM), F32)

        @pl.when(j > 0)
        def _():
            for s in range(n_seq):
                k_scr[s, 0:WINDOW] = k_scr[s, seq_rows:seq_rows + WINDOW]
                v_scr[s, 0:WINDOW] = v_scr[s, seq_rows:seq_rows + WINDOW]

    x = x_ref[...]
    h = _rms(x, nrm_ref[...]).astype(BF16)
    q_dim = C_HEADS * C_HDIM
    q = _dot(h, win_ref[:, 0:q_dim])
    k = _dot(h, win_ref[:, q_dim:q_dim + C_KV_DIM])
    v = _dot(h, win_ref[:, q_dim + C_KV_DIM:q_dim + 2 * C_KV_DIM])

    cos = cos_ref[...]
    sin_signed = sin_ref[...]
    first_half = (lax.broadcasted_iota(jnp.int32, (rows, LANES), 1) % C_HDIM) < (C_HDIM // 2)
    bd = bd_ref[...]
    qn = q * lax.rsqrt(_dot((q * q).astype(BF16), bd) + EPS) * qg_ref[...]
    kn = k * lax.rsqrt(_dot((k * k).astype(BF16), bd[0:C_KV_DIM, 0:C_KV_DIM]) + EPS) * kg_ref[...]
    scale = C_HDIM ** -0.5
    for sl in range(q_dim // LANES):
        ls = slice(sl * LANES, (sl + 1) * LANES)
        q_scr[:, ls] = _rope_slab(qn[:, ls], cos, sin_signed, first_half) * scale
    for sl in range(C_KV_DIM // LANES):
        ls = slice(sl * LANES, (sl + 1) * LANES)
        kr = _rope_slab(kn[:, ls], cos, sin_signed, first_half)
        for s in range(n_seq):
            k_scr[s, WINDOW:WINDOW + seq_rows, ls] = kr[s * seq_rows:(s + 1) * seq_rows]
    for s in range(n_seq):
        v_scr[s, WINDOW:WINDOW + seq_rows] = v[s * seq_rows:(s + 1) * seq_rows]
        ko_ref[s] = k_scr[s, WINDOW + seq_rows - tail:WINDOW + seq_rows]
        vo_ref[s] = v_scr[s, WINDOW + seq_rows - tail:WINDOW + seq_rows]

    chunks_per_seq = seq_rows // CHUNK
    n_keys = WINDOW + CHUNK
    key_idx = lax.broadcasted_iota(jnp.int32, (C_GROUP * CHUNK, n_keys), 1)
    row_grp = lax.broadcasted_iota(jnp.int32, (C_GROUP * CHUNK, 1), 0) // CHUNK

    def chunk_body(idx, carry):
        s = idx // chunks_per_seq
        c = idx % chunks_per_seq
        r0 = pl.multiple_of(idx * CHUNK, CHUNK)
        k0 = pl.multiple_of(c * CHUNK, CHUNK)
        kwin = k_scr[s, pl.ds(k0, n_keys), :]
        vwin = v_scr[s, pl.ds(k0, n_keys), :]
        for kvh in range(C_KV_HEADS):
            ks = slice(kvh * C_HDIM, (kvh + 1) * C_HDIM)
            qs = jnp.concatenate(
                [q_scr[pl.ds(r0, CHUNK), (kvh * C_GROUP + g) * C_HDIM:(kvh * C_GROUP + g + 1) * C_HDIM]
                 for g in range(C_GROUP)], axis=0)
            sc = _dot_nt(qs.astype(BF16), kwin[:, ks].astype(BF16))
            if not past_valid:
                sc = jnp.where((key_idx + k0 >= WINDOW) | (j > 0), sc, -jnp.inf)
            sink = jnp.zeros((C_GROUP * CHUNK, 1), F32)
            for g in range(C_GROUP):
                sink = jnp.where(row_grp == g, sink_ref[kvh * C_GROUP + g], sink)
            m = jnp.maximum(jnp.max(sc, axis=-1, keepdims=True), sink)
            p = jnp.exp(sc - m)
            den = jnp.sum(p, axis=-1, keepdims=True) + jnp.exp(sink - m)
            o = _dot(p.astype(BF16), vwin[:, ks].astype(BF16)) / den
            for g in range(C_GROUP):
                hs = slice((kvh * C_GROUP + g) * C_HDIM, (kvh * C_GROUP + g + 1) * C_HDIM)
                a_scr[pl.ds(r0, CHUNK), hs] = o[g * CHUNK:(g + 1) * CHUNK]
        return carry

    lax.fori_loop(0, rows // CHUNK, chunk_body, 0)
    y_ref[...] = x + _dot(a_scr[...].astype(BF16), wout_ref[...])


def _swa(x, pk, pv, cos, sin_signed, sinks, consts, *, n_batch, tiles_per_batch, n_seq, seq_rows, past_valid):
    rows = n_seq * seq_rows
    tail = min(WINDOW, seq_rows)
    n_str = n_batch * n_seq
    whole = lambda *shape: pl.BlockSpec(shape, lambda b, j, sk: (0,) * len(shape))
    in_specs = [pl.BlockSpec((rows, D_MODEL), lambda b, j, sk: (b * tiles_per_batch + j, 0)),
                pl.BlockSpec((n_seq, WINDOW, C_KV_DIM), lambda b, j, sk: (b, 0, 0)),
                pl.BlockSpec((n_seq, WINDOW, C_KV_DIM), lambda b, j, sk: (b, 0, 0)),
                pl.BlockSpec((rows, LANES), lambda b, j, sk: (j, 0)),
                pl.BlockSpec((rows, LANES), lambda b, j, sk: (j, 0)),
                whole(1, D_MODEL), whole(D_MODEL, ODD_IN), whole(D_MODEL, D_MODEL),
                whole(1, D_MODEL), whole(1, C_KV_DIM), whole(D_MODEL, D_MODEL)]
    args = [x, pk, pv, cos, sin_signed] + list(consts)
    kv_out = jax.ShapeDtypeStruct((n_str, tail, C_KV_DIM), F32)
    kv_spec = pl.BlockSpec((n_seq, tail, C_KV_DIM), lambda b, j, sk: (b, 0, 0))
    grid_spec = pltpu.PrefetchScalarGridSpec(
        num_scalar_prefetch=1, grid=(n_batch, tiles_per_batch), in_specs=in_specs,
        out_specs=[pl.BlockSpec((rows, D_MODEL), lambda b, j, sk: (b * tiles_per_batch + j, 0)),
                   kv_spec, kv_spec],
        scratch_shapes=[pltpu.VMEM((n_seq, WINDOW + seq_rows, C_KV_DIM), F32),
                        pltpu.VMEM((n_seq, WINDOW + seq_rows, C_KV_DIM), F32),
                        pltpu.VMEM((rows, D_MODEL), F32), pltpu.VMEM((rows, D_MODEL), F32)])
    return pl.pallas_call(
        functools.partial(_swa_body, n_seq=n_seq, seq_rows=seq_rows, past_valid=past_valid),
        grid_spec=grid_spec,
        out_shape=[jax.ShapeDtypeStruct(x.shape, F32), kv_out, kv_out],
        compiler_params=_cparams(2),
        name="swa_s" if past_valid else "swa_p")(sinks, *args)


def _router_body(xp_ref, xs_ref, nrm_ref, rhi_ref, rlo_ref, lstrict_ref, su_ref, info_ref, cnt_ref, *,
                 n_prompt_tiles):
    def run(x_ref):
        h = _rms(x_ref[...], nrm_ref[...])
        hi = h.astype(BF16)
        lo = (h - hi.astype(F32)).astype(BF16)
        logits = _dot(hi, rhi_ref[...]) + _dot(lo, rhi_ref[...]) + _dot(hi, rlo_ref[...])
        lane = lax.broadcasted_iota(jnp.int32, logits.shape, 1).astype(F32)
        logits = jnp.where(lane < N_EXPERTS, logits, -jnp.inf)
        l1 = jnp.max(logits, axis=-1, keepdims=True)
        i1 = jnp.min(jnp.where(logits == l1, lane, float(LANES)), axis=-1, keepdims=True)
        rest = jnp.where(lane == i1, -jnp.inf, logits)
        l2 = jnp.max(rest, axis=-1, keepdims=True)
        i2 = jnp.min(jnp.where(rest == l2, lane, float(LANES)), axis=-1, keepdims=True)
        e = jnp.exp(l2 - l1)
        g1 = 1.0 / (1.0 + e)
        g2 = e * g1
        oh1 = (lane == i1).astype(F32)
        oh2 = (lane == i2).astype(F32)
        cnt1 = jnp.sum(oh1, axis=0, keepdims=True)
        cnt2 = jnp.sum(oh2, axis=0, keepdims=True)
        seg = jnp.broadcast_to(jnp.ceil((cnt1 + cnt2) * (1.0 / GROUP)), (GROUP, LANES))
        off = _dot(seg.astype(BF16), su_ref[...])[0:1] * GROUP
        c1 = _dot(lstrict_ref[...], oh1.astype(BF16)) + off
        c2 = _dot(lstrict_ref[...], oh2.astype(BF16)) + off + cnt1
        d1 = jnp.sum(jnp.where(lane == i1, c1, 0.0), axis=-1, keepdims=True)
        d2 = jnp.sum(jnp.where(lane == i2, c2, 0.0), axis=-1, keepdims=True)
        info = jnp.zeros(logits.shape, F32)
        for n, col in enumerate([d1, d2, g1, g2]):
            info = jnp.where(lane == n, col, info)
        info_ref[...] = info
        cnt_ref[...] = seg * GROUP

    _on_row_source(pl.program_id(0), n_prompt_tiles, run, (xp_ref,), (xs_ref,))


def _router(xp, xs, nrm, rhi, rlo, lstrict, su):
    n_prompt_tiles = xp.shape[0] // ROW_TILE
    n_tiles = n_prompt_tiles + xs.shape[0] // ROW_TILE
    whole = lambda *shape: pl.BlockSpec(shape, lambda i: (0,) * len(shape))
    return pl.pallas_call(
        functools.partial(_router_body, n_prompt_tiles=n_prompt_tiles), grid=(n_tiles,),
        in_specs=[*_two_source_specs(n_prompt_tiles), whole(1, D_MODEL), whole(D_MODEL, LANES), whole(D_MODEL, LANES),
                  whole(ROW_TILE, ROW_TILE), whole(LANES, LANES)],
        out_specs=[pl.BlockSpec((ROW_TILE, LANES), lambda i: (i, 0)), pl.BlockSpec((GROUP, LANES), lambda i: (i, 0))],
        out_shape=[jax.ShapeDtypeStruct((n_tiles * ROW_TILE, LANES), F32),
                   jax.ShapeDtypeStruct((n_tiles * GROUP, LANES), F32)],
        compiler_params=_cparams(1), name="router")(xp, xs, nrm, rhi, rlo, lstrict, su)


BIG_ROWS = 8 * GROUP
SEG_FIELDS = 4
TILE_FIELDS = 2
TAIL_FIELDS = 3


def _rows_copy(src_ref, src_row, dst_ref, dst_row, n_rows, sem):
    return pltpu.make_async_copy(src_ref.at[pl.ds(pl.multiple_of(src_row, GROUP), n_rows)],
                                 dst_ref.at[pl.ds(pl.multiple_of(dst_row, GROUP), n_rows)], sem)


def _start_pieces(src_ref, src0, dst_ref, dst0, n_big, n_small, sem):
    def big(g, carry):
        _rows_copy(src_ref, src0 + g * BIG_ROWS, dst_ref, dst0 + g * BIG_ROWS, BIG_ROWS, sem).start()
        return carry

    lax.fori_loop(0, n_big, big, 0)
    done = n_big * BIG_ROWS

    def small(g, carry):
        _rows_copy(src_ref, src0 + done + g * GROUP, dst_ref, dst0 + done + g * GROUP, GROUP, sem).start()
        return carry

    lax.fori_loop(0, n_small, small, 0)


def _wait_pieces(n_big, n_small, src_ref, dst_ref, sem):
    def big(g, carry):
        _rows_copy(src_ref, 0, dst_ref, 0, BIG_ROWS, sem).wait()
        return carry

    lax.fori_loop(0, n_big, big, 0)

    def small(g, carry):
        _rows_copy(src_ref, 0, dst_ref, 0, GROUP, sem).wait()
        return carry

    lax.fori_loop(0, n_small, small, 0)


def _segment_table(seg, n_sorted_tiles):
    pieces = lambda rows: (rows // BIG_ROWS, (rows % BIG_ROWS) // GROUP)
    local_off = jnp.cumsum(seg, axis=1) - seg
    expert_rows = jnp.sum(seg, axis=0)
    expert_pad = ((expert_rows + ROW_TILE - 1) // ROW_TILE) * ROW_TILE
    expert_end = jnp.cumsum(expert_pad)
    expert_off = expert_end - expert_pad
    seg_start = expert_off[None, :] + jnp.cumsum(seg, axis=0) - seg
    n_used = (expert_end[-1] // ROW_TILE).reshape(1)
    tile_ids = jnp.arange(n_sorted_tiles, dtype=jnp.int32)
    tile_expert = jnp.minimum(jnp.sum(tile_ids[:, None] >= (expert_end // ROW_TILE)[None, :], axis=1), N_EXPERTS - 1)
    tile_expert = jnp.where(tile_ids < n_used[0], tile_expert, tile_expert[n_used[0] - 1])
    seg_big, seg_small = pieces(seg)
    tail_big, tail_small = pieces(expert_pad - expert_rows)
    table = jnp.concatenate([
        jnp.stack([seg_start, seg_big, seg_small, local_off], axis=-1).reshape(-1),
        jnp.stack([jnp.sum(seg_big, axis=1), jnp.sum(seg_small, axis=1)], axis=-1).reshape(-1),
        jnp.stack([expert_off + expert_rows, tail_big, tail_small], axis=-1).reshape(-1),
        n_used])
    return table.astype(jnp.int32), tile_expert.astype(jnp.int32), n_used.astype(jnp.int32)


def _table_sections(n_tiles):
    tiles0 = SEG_FIELDS * n_tiles * N_EXPERTS
    tails0 = tiles0 + TILE_FIELDS * n_tiles
    return tiles0, tails0, tails0 + TAIL_FIELDS * N_EXPERTS


def _dispatch_body(tab_ref, info_ref, xp_ref, xs_ref, nrm_ref, sel_ref, out_ref, loc_scr, z_scr, sems, *,
                   n_prompt_tiles, n_tiles, n_sorted_tiles):
    i = pl.program_id(0)
    slot = i % 2
    tiles0, tails0, used0 = _table_sections(n_tiles)

    def wait_tile(t, s):
        _wait_pieces(tab_ref[tiles0 + TILE_FIELDS * t], tab_ref[tiles0 + TILE_FIELDS * t + 1], loc_scr.at[s], out_ref,
                     sems.at[s])

    @pl.when(i >= 2)
    def _():
        wait_tile(i - 2, slot)

    def run(x_ref):
        hb = _rms(x_ref[...], nrm_ref[...]).astype(BF16)
        info = info_ref[...]
        lane = lax.broadcasted_iota(jnp.int32, info.shape, 1)
        dest = jnp.where(lane < TOP_K, info, 0.0)
        drow = sum(_dot_nt(sel_ref[...], part) for part in _split3(dest))
        r = lax.broadcasted_iota(jnp.int32, (LOCAL_ROWS, ROW_TILE), 0).astype(F32)
        perm = jnp.where(r == drow[0:1], 1.0, jnp.where(r == drow[1:2], 1.0, 0.0)).astype(BF16)
        loc_scr[slot] = _dot(perm, hb)

    _on_row_source(i, n_prompt_tiles, run, (xp_ref,), (xs_ref,))

    for e in range(N_EXPERTS):
        base = SEG_FIELDS * (i * N_EXPERTS + e)
        _start_pieces(loc_scr.at[slot], tab_ref[base + 3], out_ref, tab_ref[base], tab_ref[base + 1],
                      tab_ref[base + 2], sems.at[slot])

    @pl.when(i == n_tiles - 1)
    def _():
        wait_tile(i, slot)
        if n_tiles > 1:
            wait_tile(i - 1, 1 - slot)
        z_scr[...] = jnp.zeros_like(z_scr)
        for e in range(N_EXPERTS):
            base = tails0 + TAIL_FIELDS * e
            _start_pieces(z_scr, 0, out_ref, tab_ref[base], tab_ref[base + 1], tab_ref[base + 2], sems.at[2])
            _wait_pieces(tab_ref[base + 1], tab_ref[base + 2], z_scr, out_ref, sems.at[2])

        def zero_tile(t, carry):
            cp = _rows_copy(z_scr, 0, out_ref, t * ROW_TILE, ROW_TILE, sems.at[2])
            cp.start()
            cp.wait()
            return carry

        lax.fori_loop(tab_ref[used0], n_sorted_tiles, zero_tile, 0)


def _dispatch(table, info, xp, xs, nrm, sel, n_sorted_tiles):
    n_prompt_tiles = xp.shape[0] // ROW_TILE
    n_tiles = n_prompt_tiles + xs.shape[0] // ROW_TILE
    whole = lambda *shape: pl.BlockSpec(shape, lambda i, tab: (0,) * len(shape))
    grid_spec = pltpu.PrefetchScalarGridSpec(
        num_scalar_prefetch=1, grid=(n_tiles,),
        in_specs=[pl.BlockSpec((ROW_TILE, LANES), lambda i, tab: (i, 0)), *_two_source_specs(n_prompt_tiles),
                  whole(1, D_MODEL), whole(GROUP, LANES)],
        out_specs=pl.BlockSpec(memory_space=pl.ANY),
        scratch_shapes=[pltpu.VMEM((2, LOCAL_ROWS, D_MODEL), F32), pltpu.VMEM((ROW_TILE, D_MODEL), F32),
                        pltpu.SemaphoreType.DMA((3,))])
    return pl.pallas_call(
        functools.partial(_dispatch_body, n_prompt_tiles=n_prompt_tiles, n_tiles=n_tiles,
                          n_sorted_tiles=n_sorted_tiles),
        grid_spec=grid_spec, out_shape=jax.ShapeDtypeStruct((n_sorted_tiles * ROW_TILE, D_MODEL), F32),
        compiler_params=_cparams(1), name="moe_dispatch")(table, info, xp, xs, nrm, sel)


def _combine_body(tab_ref, info_ref, xp_ref, xs_ref, e_ref, yp_ref, ys_ref, loc_scr, sems, *, n_prompt_tiles,
                  n_tiles):
    i = pl.program_id(0)
    slot = i % 2
    tiles0, _, _ = _table_sections(n_tiles)

    def fetch(t, s):
        for e in range(N_EXPERTS):
            base = SEG_FIELDS * (t * N_EXPERTS + e)
            _start_pieces(e_ref, tab_ref[base], loc_scr.at[s], tab_ref[base + 3], tab_ref[base + 1],
                          tab_ref[base + 2], sems.at[s])

    @pl.when(i == 0)
    def _():
        loc_scr[...] = jnp.zeros_like(loc_scr)
        fetch(0, 0)

    @pl.when(i + 1 < n_tiles)
    def _():
        fetch(i + 1, 1 - slot)

    _wait_pieces(tab_ref[tiles0 + TILE_FIELDS * i], tab_ref[tiles0 + TILE_FIELDS * i + 1], e_ref, loc_scr.at[slot],
                 sems.at[slot])

    def run(x_ref, y_ref):
        info = info_ref[...]
        eb = loc_scr[slot].astype(BF16)
        lane = lax.broadcasted_iota(jnp.int32, (ROW_TILE, LOCAL_ROWS), 1).astype(F32)
        y = x_ref[...]
        for k in range(TOP_K):
            pick = jnp.where(lane == info[:, k:k + 1], 1.0, 0.0).astype(BF16)
            y = y + info[:, TOP_K + k:TOP_K + k + 1] * _dot(pick, eb)
        y_ref[...] = y

    _on_row_source(i, n_prompt_tiles, run, (xp_ref, yp_ref), (xs_ref, ys_ref))


def _combine(table, info, xp, xs, esorted):
    n_prompt_tiles = xp.shape[0] // ROW_TILE
    n_tiles = n_prompt_tiles + xs.shape[0] // ROW_TILE
    grid_spec = pltpu.PrefetchScalarGridSpec(
        num_scalar_prefetch=1, grid=(n_tiles,),
        in_specs=[pl.BlockSpec((ROW_TILE, LANES), lambda i, tab: (i, 0)), *_two_source_specs(n_prompt_tiles),
                  pl.BlockSpec(memory_space=pl.ANY)],
        out_specs=list(_two_source_specs(n_prompt_tiles)),
        scratch_shapes=[pltpu.VMEM((2, LOCAL_ROWS, D_MODEL), F32), pltpu.SemaphoreType.DMA((2,))])
    return pl.pallas_call(
        functools.partial(_combine_body, n_prompt_tiles=n_prompt_tiles, n_tiles=n_tiles),
        grid_spec=grid_spec,
        out_shape=[jax.ShapeDtypeStruct(xp.shape, F32), jax.ShapeDtypeStruct(xs.shape, F32)],
        compiler_params=_cparams(1), name="moe_combine")(table, info, xp, xs, esorted)


def _rope_tables(pos):
    half = C_HDIM // 2
    inv = 1.0 / (ROPE_THETA ** (jnp.arange(half, dtype=F32) / half))
    ang = pos.astype(F32)[:, None] * inv[None, :]
    cos = jnp.cos(ang)
    sin = jnp.sin(ang)
    reps = LANES // C_HDIM
    return jnp.tile(jnp.concatenate([cos, cos], axis=-1), (1, reps)), jnp.tile(jnp.concatenate([-sin, sin], axis=-1),
                                                                                 (1, reps))


def kernel(x_prompt, x_sample, mem_prompt, cache_mem_k, cache_mem_v, state_hgrn, cache_swa_k, cache_swa_v, norm_mix, norm_xattn, norm_ffn, even_w_in, even_w_out, gmlp_w_s, gmlp_b_s, gmlp_ln_g, gmlp_ln_b, hgrn_lb_logits, hgrn_out_norm, attn_w_in, attn_w_out, attn_q_norm, attn_k_norm, attn_sinks, xattn_mem_norm, xattn_w_q, xattn_w_k, xattn_w_v, xattn_w_o, xattn_q_norm, xattn_k_norm, ffn_w1, ffn_w3, ffn_w2, moe_router, moe_w1, moe_w3, moe_w2):
    n_batch, seq, d = x_prompt.shape
    dec_batch, dec_seq, _ = x_sample.shape
    n_mem = mem_prompt.shape[1]
    depth = norm_mix.shape[0]
    past_len = PAST_LEN
    assert d == D_MODEL and depth == 2 and seq % ROW_TILE == 0 and dec_batch * dec_seq == ROW_TILE
    assert dec_seq == CHUNK and ROW_TILE % n_mem == 0 and cache_swa_k.shape[2] == WINDOW
    assert even_w_in.shape[-1] == EVEN_IN and attn_w_in.shape[-1] == ODD_IN and ffn_w1.shape[-1] == D_FF
    assert moe_w1.shape[1] == N_EXPERTS
    n_prompt_rows = n_batch * seq
    n_rows = n_prompt_rows + dec_batch * dec_seq
    n_prompt_tiles = n_prompt_rows // ROW_TILE
    tiles_per_batch = seq // ROW_TILE
    bf = lambda w: w.astype(BF16)
    row = lambda g: g.reshape(1, -1).astype(F32)

    lb_all = jnp.cumsum(jax.nn.softmax(hgrn_lb_logits.astype(F32), axis=0), axis=0)
    wcum = jnp.asarray(_cumsum_matrix(), BF16)
    lmask = jnp.asarray(_level_masks(), F32)

    def even_consts(n):
        tril = jnp.tril(jnp.ones((n, n), bool))
        ws = jnp.where(tril[None], gmlp_w_s[0, :, :n, :n], 0.0).astype(BF16)
        bs = jnp.broadcast_to(gmlp_b_s[0, :, :n, None], (A_GROUPS, n, LANES)).astype(F32)
        return [row(norm_mix[0]), bf(even_w_in[0]), bf(even_w_out[0]), ws, bs,
                gmlp_ln_g[0].reshape(A_GROUPS, 1, A_GDIM), gmlp_ln_b[0].reshape(A_GROUPS, 1, A_GDIM),
                row(lb_all[0]), row(hgrn_out_norm[0]), wcum, lmask]

    xp, hgrn_p = _even_mixer(
        x_prompt.reshape(n_prompt_rows, d), jnp.zeros((n_batch, B_HEADS, B_HDIM, B_HDIM), F32),
        even_consts(A_CHUNK), n_batch=n_batch, tiles_per_batch=tiles_per_batch, n_seq=1, seq_rows=ROW_TILE,
        gchunk=A_CHUNK, emit_v=False)
    xs, hgrn_s, gmlp_v = _even_mixer(
        x_sample.reshape(ROW_TILE, d), state_hgrn[0], even_consts(min(A_CHUNK, dec_seq)), n_batch=1,
        tiles_per_batch=1, n_seq=dec_batch, seq_rows=dec_seq, gchunk=min(A_CHUNK, dec_seq), emit_v=True)

    mem_k, mem_v = _memory_kv(mem_prompt.reshape(n_batch * n_mem, d), xattn_mem_norm.reshape(depth, 1, d),
                              bf(xattn_w_k), bf(xattn_w_v), xattn_k_norm.reshape(depth, 1, X_HDIM),
                              n_batch=n_batch, n_mem=n_mem)

    def cross_attention(xp, xs, l):
        consts = (row(norm_xattn[l]), bf(xattn_w_q[l]), bf(xattn_w_o[l]), row(xattn_q_norm[l]))
        xp = _xattn(xp, mem_k[l], mem_v[l], *consts, steps_per_mem=tiles_per_batch, n_seq=1, seq_rows=ROW_TILE)
        xs = _xattn(xs, cache_mem_k[l].reshape(dec_batch, n_mem, X_DIM),
                    cache_mem_v[l].reshape(dec_batch, n_mem, X_DIM), *consts, steps_per_mem=1, n_seq=dec_batch,
                    seq_rows=dec_seq)
        return xp, xs

    xp, xs = cross_attention(xp, xs, 0)
    xp, xs = _dense_ffn(xp, xs, row(norm_ffn[0]), bf(ffn_w1[0]), bf(ffn_w3[0]), bf(ffn_w2[0]))

    reps = D_MODEL // C_HDIM
    bd = jnp.asarray(np.kron(np.eye(reps, dtype=np.float32), np.full((C_HDIM, C_HDIM), 1.0 / C_HDIM, np.float32)),
                     BF16)
    swa_consts = [row(norm_mix[1]), bf(attn_w_in[0]), bf(attn_w_out[0]), row(jnp.tile(attn_q_norm[0], C_HEADS)),
                  row(jnp.tile(attn_k_norm[0], C_KV_HEADS)), bd]
    sinks = attn_sinks[0].astype(F32)
    cos_p, sin_p = _rope_tables(jnp.arange(seq, dtype=jnp.int32))
    cos_s, sin_s = _rope_tables(past_len + jnp.arange(dec_seq, dtype=jnp.int32))
    no_past = jnp.zeros((n_batch, WINDOW, C_KV_DIM), F32)
    xp, swk_p, swv_p = _swa(xp, no_past, no_past, cos_p, sin_p, sinks, swa_consts, n_batch=n_batch,
                            tiles_per_batch=tiles_per_batch, n_seq=1, seq_rows=ROW_TILE, past_valid=False)
    xs, swk_s, swv_s = _swa(xs, cache_swa_k[0].reshape(dec_batch, WINDOW, C_KV_DIM),
                            cache_swa_v[0].reshape(dec_batch, WINDOW, C_KV_DIM),
                            jnp.tile(cos_s, (dec_batch, 1)), jnp.tile(sin_s, (dec_batch, 1)), sinks, swa_consts,
                            n_batch=1, tiles_per_batch=1, n_seq=dec_batch, seq_rows=dec_seq, past_valid=True)
    xp, xs = cross_attention(xp, xs, 1)

    n_tiles = n_rows // ROW_TILE
    router_w = jnp.zeros((d, LANES), F32).at[:, :N_EXPERTS].set(moe_router[0].astype(F32))
    rhi = router_w.astype(BF16)
    rlo = (router_w - rhi.astype(F32)).astype(BF16)
    lstrict = jnp.asarray(np.tril(np.ones((ROW_TILE, ROW_TILE), np.float32), -1), BF16)
    su = jnp.asarray(np.triu(np.ones((LANES, LANES), np.float32), 1), BF16)
    sel = jnp.asarray(np.eye(GROUP, LANES, dtype=np.float32), BF16)
    nrm_ffn = row(norm_ffn[1])
    info, seg = _router(xp, xs, nrm_ffn, rhi, rlo, lstrict, su)
    seg = seg.reshape(n_tiles, GROUP, LANES)[:, 0, :N_EXPERTS].astype(jnp.int32)
    n_sorted_tiles = -(-(TOP_K * n_rows + n_tiles * N_EXPERTS * (GROUP - 1) + N_EXPERTS * (ROW_TILE - 1)) // ROW_TILE)
    table, tile_expert, n_used = _segment_table(seg, n_sorted_tiles)
    hsorted = _dispatch(table, info, xp, xs, nrm_ffn, sel, n_sorted_tiles)
    esorted = _grouped_ffn(hsorted, tile_expert, n_used, bf(moe_w1[0]), bf(moe_w3[0]), bf(moe_w2[0]))
    y_prompt, y_sample = _combine(table, info, xp, xs, esorted)

    n_even = state_hgrn.shape[0]
    n_odd = cache_swa_k.shape[0]
    return (y_prompt.reshape(n_batch, seq, d), y_sample.reshape(dec_batch, dec_seq, d),
            mem_k.reshape(depth, n_batch, n_mem, X_HEADS, X_HDIM), mem_v.reshape(depth, n_batch, n_mem, X_HEADS, X_HDIM),
            hgrn_p.reshape(n_even, n_batch, B_HEADS, B_HDIM, B_HDIM),
            gmlp_v.reshape(n_even, dec_batch, dec_seq, A_GROUPS, A_GDIM),
            hgrn_s.reshape(n_even, dec_batch, B_HEADS, B_HDIM, B_HDIM),
            swk_p.reshape(n_odd, n_batch, WINDOW, C_KV_HEADS, C_HDIM), swv_p.reshape(n_odd, n_batch, WINDOW, C_KV_HEADS, C_HDIM),
            swk_s.reshape(n_odd, dec_batch, dec_seq, C_KV_HEADS, C_HDIM), swv_s.reshape(n_odd, dec_batch, dec_seq, C_KV_HEADS, C_HDIM))
```

```python
import functools

import numpy as np
import jax
import jax.numpy as jnp
from jax import lax
from jax.experimental import pallas as pl
from jax.experimental.pallas import tpu as pltpu

F32 = jnp.float32
BF16 = jnp.bfloat16

D_MODEL = 1024
EPS = 1e-6
CHUNK = 64
A_GROUPS = 4
A_DIM = D_MODEL // 2
A_GDIM = A_DIM // A_GROUPS
A_CHUNK = 128
B_HEADS = 4
B_DIM = D_MODEL // 2
B_HDIM = B_DIM // B_HEADS
EVEN_IN = 2 * A_DIM + 4 * B_DIM
C_HEADS = 16
C_KV_HEADS = 4
C_HDIM = D_MODEL // C_HEADS
C_GROUP = C_HEADS // C_KV_HEADS
C_KV_DIM = C_KV_HEADS * C_HDIM
WINDOW = 128
ROPE_THETA = 10000.0
PAST_LEN = 4096
ODD_IN = (C_HEADS + 2 * C_KV_HEADS) * C_HDIM
X_HEADS = 4
X_HDIM = 128
X_DIM = X_HEADS * X_HDIM
D_FF = 2816
N_EXPERTS = 8
TOP_K = 2

LANES = 128
ROW_TILE = 512
FF_CHUNK = 2816
N_FF_CHUNKS = D_FF // FF_CHUNK
FF_SUB = 512
N_LEVELS = 6
GROUP = 8
LOCAL_ROWS = -(-(TOP_K * ROW_TILE + N_EXPERTS * (GROUP - 1)) // LANES) * LANES
VMEM_LIMIT = 58 * 1024 * 1024


def _cparams(n_axes):
    return pltpu.CompilerParams(dimension_semantics=("arbitrary",) * n_axes, vmem_limit_bytes=VMEM_LIMIT)


def _dot(a, b):
    return jnp.dot(a, b, preferred_element_type=F32)


def _dot_nt(a, b):
    return lax.dot_general(a, b, (((1,), (1,)), ((), ())), preferred_element_type=F32)


def _rms(x, g):
    return x * lax.rsqrt(jnp.mean(x * x, axis=-1, keepdims=True) + EPS) * g


def _silu(x):
    return x * (1.0 / (1.0 + jnp.exp(-x)))


def _split3(x):
    hi = x.astype(BF16)
    r1 = x - hi.astype(F32)
    mid = r1.astype(BF16)
    lo = (r1 - mid.astype(F32)).astype(BF16)
    return hi, mid, lo


def _cumsum_matrix():
    r = np.arange(CHUNK)
    s = np.arange(CHUNK)
    blocks = [(s[None, :] <= r[:, None])]
    for l in range(N_LEVELS):
        h = 1 << l
        ref = (r & ~(2 * h - 1)) + h - 1
        blocks.append(s[None, :] <= ref[:, None])
    w = np.concatenate(blocks, axis=0).astype(np.float32)
    return np.concatenate([w, w, w], axis=1)


def _level_masks():
    t = np.arange(CHUNK)[:, None]
    s = np.arange(CHUNK)[None, :]
    masks = []
    for l in range(N_LEVELS):
        masks.append(((t >> (l + 1)) == (s >> (l + 1))) & (((t >> l) & 1) == 1) & (((s >> l) & 1) == 0))
    masks.append(t == s)
    return np.stack(masks).astype(np.float32)


def _even_mixer_body(x_ref, s0_ref, nrm_ref, win_ref, wout_ref, ws_ref, bs_ref, lng_ref, lnb_ref, lb_ref, og_ref,
                     wcum_ref, lmask_ref, *rest, n_seq, seq_rows, gchunk, emit_v):
    if emit_v:
        y_ref, sout_ref, v_ref, proj_scr, mixed_scr, st_scr = rest
    else:
        y_ref, sout_ref, proj_scr, mixed_scr, st_scr = rest
        v_ref = None
    rows = n_seq * seq_rows
    j = pl.program_id(1)

    @pl.when(j == 0)
    def _():
        for s in range(n_seq):
            for hd in range(B_HEADS):
                st_scr[s * B_HEADS + hd] = s0_ref[s, hd].T

    x = x_ref[...]
    h = _rms(x, nrm_ref[...]).astype(BF16)
    n_pieces = EVEN_IN // ROW_TILE
    for n in range(n_pieces):
        cs = slice(n * ROW_TILE, (n + 1) * ROW_TILE)
        proj_scr[:, cs] = _dot(h, win_ref[:, cs])

    groups = range(A_GROUPS)
    group_cols = lambda base, g: slice(base + g * A_GDIM, base + (g + 1) * A_GDIM)
    vgs = [jax.nn.gelu(proj_scr[:, group_cols(A_DIM, g)]) for g in groups]
    means = [jnp.mean(vg, axis=-1, keepdims=True) for vg in vgs]
    vcs = [vg - mu for vg, mu in zip(vgs, means)]
    variances = [jnp.mean(vc * vc, axis=-1, keepdims=True) for vc in vcs]
    for g in groups:
        gs = group_cols(0, g)
        vn = vcs[g] * lax.rsqrt(variances[g] + EPS) * lng_ref[g] + lnb_ref[g]
        if v_ref is not None:
            v_ref[:, gs] = vn
        ug = jax.nn.gelu(proj_scr[:, gs])
        vb = vn.astype(BF16)
        for c in range(rows // gchunk):
            rs = slice(c * gchunk, (c + 1) * gchunk)
            sp = _dot(ws_ref[g], vb[rs]) + bs_ref[g]
            mixed_scr[rs, gs] = ug[rs] * sp

    q0, f0, i0, g0 = (2 * A_DIM + k * B_DIM for k in range(4))
    lb = lb_ref[...]
    fg = lb + (1.0 - lb) * jax.nn.sigmoid(proj_scr[:, f0:f0 + B_DIM])
    proj_scr[:, 0:B_DIM] = jnp.log(fg)
    proj_scr[:, B_DIM:2 * B_DIM] = 1.0 - fg
    proj_scr[:, q0:q0 + B_DIM] = _silu(proj_scr[:, q0:q0 + B_DIM])
    og = og_ref[...]
    chunks_per_seq = seq_rows // CHUNK

    def chunk_body(c, carry):
        r0 = pl.multiple_of(c * CHUNK, CHUNK)
        rs = pl.ds(r0, CHUNK)
        sidx = (c // chunks_per_seq) * B_HEADS
        hi, mid, lo = _split3(proj_scr[rs, 0:B_DIM])
        gg = _dot(wcum_ref[...], jnp.concatenate([hi, mid, lo], axis=0))
        heads = range(B_HEADS)
        head_cols = lambda base, hd: slice(base + hd * B_HDIM, base + (hd + 1) * B_HDIM)
        vs = [proj_scr[rs, head_cols(i0, hd)] for hd in heads]
        vts = [v.T.astype(BF16) for v in vs]
        outs = []
        for hd in heads:
            hs = head_cols(0, hd)
            G = gg[0:CHUNK, hs]
            q = proj_scr[rs, head_cols(q0, hd)]
            k = proj_scr[rs, head_cols(B_DIM, hd)]
            st = st_scr[sidx + hd]
            o = _dot_nt((q * jnp.exp(G)).astype(BF16), st.astype(BF16))
            att = jnp.zeros((CHUNK, CHUNK), F32)
            for l in range(N_LEVELS + 1):
                if l < N_LEVELS:
                    e = jnp.exp(-jnp.abs(G - gg[(l + 1) * CHUNK:(l + 2) * CHUNK, hs]))
                    a = _dot_nt((q * e).astype(BF16), (k * e).astype(BF16))
                else:
                    a = _dot_nt(q.astype(BF16), k.astype(BF16))
                att = jnp.where(lmask_ref[l] > 0.5, a, att)
            outs.append(o + _dot(att.astype(BF16), vs[hd].astype(BF16)))
            g_end = G[CHUNK - 1:CHUNK, :]
            kd = (k * jnp.exp(g_end - G)).astype(BF16)
            st_scr[sidx + hd] = st * jnp.exp(g_end) + _dot(vts[hd], kd)
        for hd in heads:
            o = outs[hd]
            on = o * lax.rsqrt(jnp.mean(o * o, axis=-1, keepdims=True) + EPS) * og
            gate = _silu(proj_scr[rs, head_cols(g0, hd)])
            mixed_scr[rs, head_cols(A_DIM, hd)] = on * gate
        return carry

    lax.fori_loop(0, rows // CHUNK, chunk_body, 0, unroll=4)

    y_ref[...] = x + _dot(mixed_scr[...].astype(BF16), wout_ref[...])

    @pl.when(j == pl.num_programs(1) - 1)
    def _():
        for s in range(n_seq):
            for hd in range(B_HEADS):
                sout_ref[s, hd] = st_scr[s * B_HEADS + hd].T


def _even_mixer(x, s0, consts, *, n_batch, tiles_per_batch, n_seq, seq_rows, gchunk, emit_v):
    rows = n_seq * seq_rows
    grid = (n_batch, tiles_per_batch)
    tile = lambda b, j: (b * tiles_per_batch + j, 0)
    whole = lambda *shape: pl.BlockSpec(shape, lambda b, j: (0,) * len(shape))
    in_specs = [
        pl.BlockSpec((rows, D_MODEL), tile),
        pl.BlockSpec((n_seq, B_HEADS, B_HDIM, B_HDIM), lambda b, j: (b, 0, 0, 0)),
        whole(1, D_MODEL), whole(D_MODEL, EVEN_IN), whole(D_MODEL, D_MODEL),
        whole(A_GROUPS, gchunk, gchunk), whole(A_GROUPS, gchunk, LANES),
        whole(A_GROUPS, 1, A_GDIM), whole(A_GROUPS, 1, A_GDIM), whole(1, B_DIM), whole(1, B_HDIM),
        whole((N_LEVELS + 1) * CHUNK, 3 * CHUNK), whole(N_LEVELS + 1, CHUNK, CHUNK),
    ]
    args = [x, s0] + list(consts)
    out_shape = [jax.ShapeDtypeStruct(x.shape, F32),
                 jax.ShapeDtypeStruct((n_batch * n_seq, B_HEADS, B_HDIM, B_HDIM), F32)]
    out_specs = [pl.BlockSpec((rows, D_MODEL), tile),
                 pl.BlockSpec((n_seq, B_HEADS, B_HDIM, B_HDIM), lambda b, j: (b, 0, 0, 0))]
    if emit_v:
        out_shape.append(jax.ShapeDtypeStruct((x.shape[0], A_DIM), F32))
        out_specs.append(pl.BlockSpec((rows, A_DIM), tile))
    body = functools.partial(_even_mixer_body, n_seq=n_seq, seq_rows=seq_rows, gchunk=gchunk, emit_v=emit_v)
    return pl.pallas_call(
        body, grid=grid, in_specs=in_specs, out_specs=out_specs, out_shape=out_shape,
        scratch_shapes=[pltpu.VMEM((rows, EVEN_IN), F32), pltpu.VMEM((rows, D_MODEL), F32),
                        pltpu.VMEM((n_seq * B_HEADS, B_HDIM, B_HDIM), F32)],
        compiler_params=_cparams(2),
        name="even_mixer_s" if emit_v else "even_mixer_p")(*args)


def _memory_kv_body(mem_ref, gm_ref, wk_ref, wv_ref, kg_ref, mk_ref, mv_ref, *, n_seq, n_mem):
    m = _rms(mem_ref[...], gm_ref[0]).astype(BF16)
    kk = _dot(m, wk_ref[0])
    vv = _dot(m, wv_ref[0])
    kg = kg_ref[0]
    for hd in range(X_HEADS):
        hs = slice(hd * X_HDIM, (hd + 1) * X_HDIM)
        kh = _rms(kk[:, hs], kg)
        for s in range(n_seq):
            mk_ref[0, s, :, hs] = kh[s * n_mem:(s + 1) * n_mem]
    for s in range(n_seq):
        mv_ref[0, s] = vv[s * n_mem:(s + 1) * n_mem]


def _memory_kv(mem2d, g_mem, w_k, w_v, k_g, *, n_batch, n_mem):
    depth = w_k.shape[0]
    n_seq = ROW_TILE // n_mem
    out = jax.ShapeDtypeStruct((depth, n_batch, n_mem, X_DIM), F32)
    ospec = pl.BlockSpec((1, n_seq, n_mem, X_DIM), lambda l, t: (l, t, 0, 0))
    return pl.pallas_call(
        functools.partial(_memory_kv_body, n_seq=n_seq, n_mem=n_mem),
        grid=(depth, n_batch // n_seq),
        in_specs=[pl.BlockSpec((ROW_TILE, D_MODEL), lambda l, t: (t, 0)),
                  pl.BlockSpec((1, 1, D_MODEL), lambda l, t: (l, 0, 0)),
                  pl.BlockSpec((1, D_MODEL, X_DIM), lambda l, t: (l, 0, 0)),
                  pl.BlockSpec((1, D_MODEL, X_DIM), lambda l, t: (l, 0, 0)),
                  pl.BlockSpec((1, 1, X_HDIM), lambda l, t: (l, 0, 0))],
        out_specs=[ospec, ospec], out_shape=[out, out], compiler_params=_cparams(2),
        name="memory_kv")(mem2d, g_mem, w_k, w_v, k_g)


def _xattn_body(x_ref, mk_ref, mv_ref, nrm_ref, wq_ref, wo_ref, qg_ref, y_ref, o_scr, *, n_seq, seq_rows):
    x = x_ref[...]
    h = _rms(x, nrm_ref[...]).astype(BF16)
    q = _dot(h, wq_ref[...])
    qg = qg_ref[...] * (X_HDIM ** -0.5)
    units = [(s, hd) for s in range(n_seq) for hd in range(X_HEADS)]
    rows_of = lambda s: slice(s * seq_rows, (s + 1) * seq_rows)
    lanes_of = lambda hd: slice(hd * X_HDIM, (hd + 1) * X_HDIM)
    qh = [_rms(q[rows_of(s), lanes_of(hd)], qg).astype(BF16) for s, hd in units]
    sc = [_dot_nt(qh[n], mk_ref[s, :, lanes_of(hd)].astype(BF16)) for n, (s, hd) in enumerate(units)]
    m = [jnp.max(t, axis=-1, keepdims=True) for t in sc]
    p = [jnp.exp(t - mx) for t, mx in zip(sc, m)]
    den = [jnp.sum(t, axis=-1, keepdims=True) for t in p]
    for n, (s, hd) in enumerate(units):
        o_scr[rows_of(s), lanes_of(hd)] = _dot(p[n].astype(BF16), mv_ref[s, :, lanes_of(hd)].astype(BF16)) / den[n]
    y_ref[...] = x + _dot(o_scr[...].astype(BF16), wo_ref[...])


def _xattn(x, mk, mv, nrm, wq, wo, qg, *, steps_per_mem, n_seq, seq_rows):
    rows = n_seq * seq_rows
    n_mem = mk.shape[1]
    whole = lambda *shape: pl.BlockSpec(shape, lambda t: (0,) * len(shape))
    return pl.pallas_call(
        functools.partial(_xattn_body, n_seq=n_seq, seq_rows=seq_rows),
        grid=(x.shape[0] // rows,),
        in_specs=[pl.BlockSpec((rows, D_MODEL), lambda t: (t, 0)),
                  pl.BlockSpec((n_seq, n_mem, X_DIM), lambda t: (t // steps_per_mem, 0, 0)),
                  pl.BlockSpec((n_seq, n_mem, X_DIM), lambda t: (t // steps_per_mem, 0, 0)),
                  whole(1, D_MODEL), whole(D_MODEL, X_DIM), whole(X_DIM, D_MODEL), whole(1, X_HDIM)],
        out_specs=pl.BlockSpec((rows, D_MODEL), lambda t: (t, 0)),
        out_shape=jax.ShapeDtypeStruct(x.shape, F32),
        scratch_shapes=[pltpu.VMEM((rows, X_DIM), F32)],
        compiler_params=_cparams(1), name="xattn_s" if n_seq > 1 else "xattn_p")(x, mk, mv, nrm, wq, wo, qg)


def _ff_chunk(i, j):
    return jnp.where(i % 2 == 0, j, N_FF_CHUNKS - 1 - j)


def _two_source_specs(n_prompt_tiles, width=D_MODEL):
    return (pl.BlockSpec((ROW_TILE, width), lambda i, *_: (jnp.minimum(i, n_prompt_tiles - 1), 0)),
            pl.BlockSpec((ROW_TILE, width), lambda i, *_: (jnp.maximum(i - n_prompt_tiles, 0), 0)))


def _on_row_source(i, n_prompt_tiles, fn, prompt_refs, sample_refs):
    pl.when(i < n_prompt_tiles)(lambda: fn(*prompt_refs))
    pl.when(i >= n_prompt_tiles)(lambda: fn(*sample_refs))


def _swiglu_part(hb, w1, w3, w2):
    part = None
    for c0 in range(0, FF_CHUNK, FF_SUB):
        cs = slice(c0, min(c0 + FF_SUB, FF_CHUNK))
        act = (_silu(_dot(hb, w1[:, cs])) * _dot(hb, w3[:, cs])).astype(BF16)
        p = _dot(act, w2[cs, :])
        part = p if part is None else part + p
    return part


def _dense_ffn_body(xp_ref, xs_ref, nrm_ref, w1_ref, w3_ref, w2_ref, yp_ref, ys_ref, h_scr, *, n_prompt_tiles):
    j = pl.program_id(1)

    def run(x_ref, y_ref):
        @pl.when(j == 0)
        def _():
            h_scr[...] = _rms(x_ref[...], nrm_ref[...]).astype(BF16)

        part = _swiglu_part(h_scr[...], w1_ref, w3_ref, w2_ref)

        @pl.when(j == 0)
        def _():
            y_ref[...] = x_ref[...] + part

        @pl.when(j > 0)
        def _():
            y_ref[...] += part

    _on_row_source(pl.program_id(0), n_prompt_tiles, run, (xp_ref, yp_ref), (xs_ref, ys_ref))


def _dense_ffn(xp, xs, nrm, w1, w3, w2):
    n_prompt_tiles = xp.shape[0] // ROW_TILE
    n_tiles = n_prompt_tiles + xs.shape[0] // ROW_TILE
    return pl.pallas_call(
        functools.partial(_dense_ffn_body, n_prompt_tiles=n_prompt_tiles), grid=(n_tiles, N_FF_CHUNKS),
        in_specs=[*_two_source_specs(n_prompt_tiles),
                  pl.BlockSpec((1, D_MODEL), lambda i, j: (0, 0)),
                  pl.BlockSpec((D_MODEL, FF_CHUNK), lambda i, j: (0, _ff_chunk(i, j))),
                  pl.BlockSpec((D_MODEL, FF_CHUNK), lambda i, j: (0, _ff_chunk(i, j))),
                  pl.BlockSpec((FF_CHUNK, D_MODEL), lambda i, j: (_ff_chunk(i, j), 0))],
        out_specs=list(_two_source_specs(n_prompt_tiles)),
        out_shape=[jax.ShapeDtypeStruct(xp.shape, F32), jax.ShapeDtypeStruct(xs.shape, F32)],
        scratch_shapes=[pltpu.VMEM((ROW_TILE, D_MODEL), BF16)],
        compiler_params=_cparams(2), name="dense_ffn")(xp, xs, nrm, w1, w3, w2)


def _grouped_ffn_body(te_ref, nu_ref, x_ref, w1_ref, w3_ref, w2_ref, y_ref):
    i = pl.program_id(0)
    j = pl.program_id(1)

    @pl.when(i < nu_ref[0])
    def _():
        part = _swiglu_part(x_ref[...].astype(BF16), w1_ref.at[0], w3_ref.at[0], w2_ref.at[0])

        @pl.when(j == 0)
        def _():
            y_ref[...] = part

        @pl.when(j > 0)
        def _():
            y_ref[...] += part

    @pl.when(i >= nu_ref[0])
    def _():
        y_ref[...] = jnp.zeros_like(y_ref)


def _grouped_ffn(xsorted, tile_expert, n_used, w1, w3, w2):
    n_tiles = xsorted.shape[0] // ROW_TILE

    def row_tile(i, j, te, nu):
        return (i, 0)

    def ff(i, j, nu):
        last = nu[0] - 1
        return jnp.where(i <= last, _ff_chunk(i, j), _ff_chunk(last, N_FF_CHUNKS - 1))

    grid_spec = pltpu.PrefetchScalarGridSpec(
        num_scalar_prefetch=2, grid=(n_tiles, N_FF_CHUNKS),
        in_specs=[pl.BlockSpec((ROW_TILE, D_MODEL), row_tile),
                  pl.BlockSpec((1, D_MODEL, FF_CHUNK), lambda i, j, te, nu: (te[i], 0, ff(i, j, nu))),
                  pl.BlockSpec((1, D_MODEL, FF_CHUNK), lambda i, j, te, nu: (te[i], 0, ff(i, j, nu))),
                  pl.BlockSpec((1, FF_CHUNK, D_MODEL), lambda i, j, te, nu: (te[i], ff(i, j, nu), 0))],
        out_specs=pl.BlockSpec((ROW_TILE, D_MODEL), row_tile))
    return pl.pallas_call(
        _grouped_ffn_body, grid_spec=grid_spec, out_shape=jax.ShapeDtypeStruct(xsorted.shape, F32),
        compiler_params=_cparams(2), name="grouped_ffn")(tile_expert, n_used, xsorted, w1, w3, w2)


def _rope_slab(xs, cos, sin_signed, first_half):
    rot = jnp.where(first_half, pltpu.roll(xs, LANES - C_HDIM // 2, 1), pltpu.roll(xs, C_HDIM // 2, 1))
    return xs * cos + rot * sin_signed


def _swa_body(sink_ref, x_ref, pk_ref, pv_ref, cos_ref, sin_ref, nrm_ref, win_ref, wout_ref, qg_ref, kg_ref, bd_ref,
              y_ref, ko_ref, vo_ref, k_scr, v_scr, q_scr, a_scr, *, n_seq, seq_rows, past_valid):
    rows = n_seq * seq_rows
    j = pl.program_id(1)
    tail = min(WINDOW, seq_rows)

    if past_valid:
        for s in range(n_seq):
            k_scr[s, 0:WINDOW] = pk_ref[s]
            v_scr[s, 0:WINDOW] = pv_ref[s]
    else:
        @pl.when(j == 0)
        def _():
            for s in range(n_seq):
                k_scr[s, 0:WINDOW] = jnp.zeros((WINDOW, C_KV_DIM), F32)
                v_scr[s, 0:WINDOW] = jnp.zeros((WINDOW, C_KV_DIM), F32)

        @pl.when(j > 0)
        def _():
            for s in range(n_seq):
                k_scr[s, 0:WINDOW] = k_scr[s, seq_rows:seq_rows + WINDOW]
                v_scr[s, 0:WINDOW] = v_scr[s, seq_rows:seq_rows + WINDOW]

    x = x_ref[...]
    h = _rms(x, nrm_ref[...]).astype(BF16)
    q_dim = C_HEADS * C_HDIM
    q = _dot(h, win_ref[:, 0:q_dim])
    k = _dot(h, win_ref[:, q_dim:q_dim + C_KV_DIM])
    v = _dot(h, win_ref[:, q_dim + C_KV_DIM:q_dim + 2 * C_KV_DIM])

    cos = cos_ref[...]
    sin_signed = sin_ref[...]
    first_half = (lax.broadcasted_iota(jnp.int32, (rows, LANES), 1) % C_HDIM) < (C_HDIM // 2)
    bd = bd_ref[...]
    qn = q * lax.rsqrt(_dot((q * q).astype(BF16), bd) + EPS) * qg_ref[...]
    kn = k * lax.rsqrt(_dot((k * k).astype(BF16), bd[0:C_KV_DIM, 0:C_KV_DIM]) + EPS) * kg_ref[...]
    scale = C_HDIM ** -0.5
    for sl in range(q_dim // LANES):
        ls = slice(sl * LANES, (sl + 1) * LANES)
        q_scr[:, ls] = _rope_slab(qn[:, ls], cos, sin_signed, first_half) * scale
    for sl in range(C_KV_DIM // LANES):
        ls = slice(sl * LANES, (sl + 1) * LANES)
        kr = _rope_slab(kn[:, ls], cos, sin_signed, first_half)
        for s in range(n_seq):
            k_scr[s, WINDOW:WINDOW + seq_rows, ls] = kr[s * seq_rows:(s + 1) * seq_rows]
    for s in range(n_seq):
        v_scr[s, WINDOW:WINDOW + seq_rows] = v[s * seq_rows:(s + 1) * seq_rows]
        ko_ref[s] = k_scr[s, WINDOW + seq_rows - tail:WINDOW + seq_rows]
        vo_ref[s] = v_scr[s, WINDOW + seq_rows - tail:WINDOW + seq_rows]

    chunks_per_seq = seq_rows // CHUNK
    n_keys = WINDOW + CHUNK
    key_idx = lax.broadcasted_iota(jnp.int32, (C_GROUP * CHUNK, n_keys), 1)
    row_grp = lax.broadcasted_iota(jnp.int32, (C_GROUP * CHUNK, 1), 0) // CHUNK
    ones_keys = jnp.ones((n_keys, C_HDIM), BF16)

    def chunk_body(idx, carry):
        s = idx // chunks_per_seq
        c = idx % chunks_per_seq
        r0 = pl.multiple_of(idx * CHUNK, CHUNK)
        k0 = pl.multiple_of(c * CHUNK, CHUNK)
        kwin = k_scr[s, pl.ds(k0, n_keys), :]
        vwin = v_scr[s, pl.ds(k0, n_keys), :]
        heads = range(C_KV_HEADS)
        head_lanes = lambda kvh: slice(kvh * C_HDIM, (kvh + 1) * C_HDIM)
        qs = [jnp.concatenate(
            [q_scr[pl.ds(r0, CHUNK), (kvh * C_GROUP + g) * C_HDIM:(kvh * C_GROUP + g + 1) * C_HDIM]
             for g in range(C_GROUP)], axis=0).astype(BF16) for kvh in heads]
        kh = [kwin[:, head_lanes(kvh)].astype(BF16) for kvh in heads]
        vh = [vwin[:, head_lanes(kvh)].astype(BF16) for kvh in heads]
        sc = [_dot_nt(qs[kvh], kh[kvh]) for kvh in heads]
        if not past_valid:
            valid = (key_idx + k0 >= WINDOW) | (j > 0)
            sc = [jnp.where(valid, s_, -jnp.inf) for s_ in sc]
        sinks = []
        for kvh in heads:
            sink = jnp.zeros((C_GROUP * CHUNK, 1), F32)
            for g in range(C_GROUP):
                sink = jnp.where(row_grp == g, sink_ref[kvh * C_GROUP + g], sink)
            sinks.append(sink)
        m = [jnp.maximum(jnp.max(sc[kvh], axis=-1, keepdims=True), sinks[kvh]) for kvh in heads]
        p = [jnp.exp(sc[kvh] - m[kvh]).astype(BF16) for kvh in heads]
        den = [_dot(p[kvh], ones_keys) + jnp.exp(sinks[kvh] - m[kvh]) for kvh in heads]
        o = [_dot(p[kvh], vh[kvh]) / den[kvh] for kvh in heads]
        for kvh in heads:
            for g in range(C_GROUP):
                hs = slice((kvh * C_GROUP + g) * C_HDIM, (kvh * C_GROUP + g + 1) * C_HDIM)
                a_scr[pl.ds(r0, CHUNK), hs] = o[kvh][g * CHUNK:(g + 1) * CHUNK]
        return carry

    lax.fori_loop(0, rows // CHUNK, chunk_body, 0, unroll=2)
    y_ref[...] = x + _dot(a_scr[...].astype(BF16), wout_ref[...])


def _swa(x, pk, pv, cos, sin_signed, sinks, consts, *, n_batch, tiles_per_batch, n_seq, seq_rows, past_valid):
    rows = n_seq * seq_rows
    tail = min(WINDOW, seq_rows)
    n_str = n_batch * n_seq
    whole = lambda *shape: pl.BlockSpec(shape, lambda b, j, sk: (0,) * len(shape))
    in_specs = [pl.BlockSpec((rows, D_MODEL), lambda b, j, sk: (b * tiles_per_batch + j, 0)),
                pl.BlockSpec((n_seq, WINDOW, C_KV_DIM), lambda b, j, sk: (b, 0, 0)),
                pl.BlockSpec((n_seq, WINDOW, C_KV_DIM), lambda b, j, sk: (b, 0, 0)),
                pl.BlockSpec((rows, LANES), lambda b, j, sk: (j, 0)),
                pl.BlockSpec((rows, LANES), lambda b, j, sk: (j, 0)),
                whole(1, D_MODEL), whole(D_MODEL, ODD_IN), whole(D_MODEL, D_MODEL),
                whole(1, D_MODEL), whole(1, C_KV_DIM), whole(D_MODEL, D_MODEL)]
    args = [x, pk, pv, cos, sin_signed] + list(consts)
    kv_out = jax.ShapeDtypeStruct((n_str, tail, C_KV_DIM), F32)
    kv_spec = pl.BlockSpec((n_seq, tail, C_KV_DIM), lambda b, j, sk: (b, 0, 0))
    grid_spec = pltpu.PrefetchScalarGridSpec(
        num_scalar_prefetch=1, grid=(n_batch, tiles_per_batch), in_specs=in_specs,
        out_specs=[pl.BlockSpec((rows, D_MODEL), lambda b, j, sk: (b * tiles_per_batch + j, 0)),
                   kv_spec, kv_spec],
        scratch_shapes=[pltpu.VMEM((n_seq, WINDOW + seq_rows, C_KV_DIM), F32),
                        pltpu.VMEM((n_seq, WINDOW + seq_rows, C_KV_DIM), F32),
                        pltpu.VMEM((rows, D_MODEL), F32), pltpu.VMEM((rows, D_MODEL), F32)])
    return pl.pallas_call(
        functools.partial(_swa_body, n_seq=n_seq, seq_rows=seq_rows, past_valid=past_valid),
        grid_spec=grid_spec,
        out_shape=[jax.ShapeDtypeStruct(x.shape, F32), kv_out, kv_out],
        compiler_params=_cparams(2),
        name="swa_s" if past_valid else "swa_p")(sinks, *args)


def _router_body(xp_ref, xs_ref, nrm_ref, rhi_ref, rlo_ref, lstrict_ref, su_ref, info_ref, cnt_ref, *,
                 n_prompt_tiles):
    def run(x_ref):
        h = _rms(x_ref[...], nrm_ref[...])
        hi = h.astype(BF16)
        lo = (h - hi.astype(F32)).astype(BF16)
        logits = _dot(hi, rhi_ref[...]) + _dot(lo, rhi_ref[...]) + _dot(hi, rlo_ref[...])
        lane = lax.broadcasted_iota(jnp.int32, logits.shape, 1).astype(F32)
        logits = jnp.where(lane < N_EXPERTS, logits, -jnp.inf)
        l1 = jnp.max(logits, axis=-1, keepdims=True)
        i1 = jnp.min(jnp.where(logits == l1, lane, float(LANES)), axis=-1, keepdims=True)
        rest = jnp.where(lane == i1, -jnp.inf, logits)
        l2 = jnp.max(rest, axis=-1, keepdims=True)
        i2 = jnp.min(jnp.where(rest == l2, lane, float(LANES)), axis=-1, keepdims=True)
        e = jnp.exp(l2 - l1)
        g1 = 1.0 / (1.0 + e)
        g2 = e * g1
        oh1 = (lane == i1).astype(F32)
        oh2 = (lane == i2).astype(F32)
        cnt1 = jnp.sum(oh1, axis=0, keepdims=True)
        cnt2 = jnp.sum(oh2, axis=0, keepdims=True)
        seg = jnp.broadcast_to(jnp.ceil((cnt1 + cnt2) * (1.0 / GROUP)), (GROUP, LANES))
        off = _dot(seg.astype(BF16), su_ref[...])[0:1] * GROUP
        c1 = _dot(lstrict_ref[...], oh1.astype(BF16)) + off
        c2 = _dot(lstrict_ref[...], oh2.astype(BF16)) + off + cnt1
        d1 = jnp.sum(jnp.where(lane == i1, c1, 0.0), axis=-1, keepdims=True)
        d2 = jnp.sum(jnp.where(lane == i2, c2, 0.0), axis=-1, keepdims=True)
        info = jnp.zeros(logits.shape, F32)
        for n, col in enumerate([d1, d2, g1, g2]):
            info = jnp.where(lane == n, col, info)
        info_ref[...] = info
        cnt_ref[...] = seg * GROUP

    _on_row_source(pl.program_id(0), n_prompt_tiles, run, (xp_ref,), (xs_ref,))


def _router(xp, xs, nrm, rhi, rlo, lstrict, su):
    n_prompt_tiles = xp.shape[0] // ROW_TILE
    n_tiles = n_prompt_tiles + xs.shape[0] // ROW_TILE
    whole = lambda *shape: pl.BlockSpec(shape, lambda i: (0,) * len(shape))
    return pl.pallas_call(
        functools.partial(_router_body, n_prompt_tiles=n_prompt_tiles), grid=(n_tiles,),
        in_specs=[*_two_source_specs(n_prompt_tiles), whole(1, D_MODEL), whole(D_MODEL, LANES), whole(D_MODEL, LANES),
                  whole(ROW_TILE, ROW_TILE), whole(LANES, LANES)],
        out_specs=[pl.BlockSpec((ROW_TILE, LANES), lambda i: (i, 0)), pl.BlockSpec((GROUP, LANES), lambda i: (i, 0))],
        out_shape=[jax.ShapeDtypeStruct((n_tiles * ROW_TILE, LANES), F32),
                   jax.ShapeDtypeStruct((n_tiles * GROUP, LANES), F32)],
        compiler_params=_cparams(1), name="router")(xp, xs, nrm, rhi, rlo, lstrict, su)


BIG_ROWS = 8 * GROUP
SEG_FIELDS = 4
TILE_FIELDS = 2
TAIL_FIELDS = 3


def _rows_copy(src_ref, src_row, dst_ref, dst_row, n_rows, sem):
    return pltpu.make_async_copy(src_ref.at[pl.ds(pl.multiple_of(src_row, GROUP), n_rows)],
                                 dst_ref.at[pl.ds(pl.multiple_of(dst_row, GROUP), n_rows)], sem)


def _start_pieces(src_ref, src0, dst_ref, dst0, n_big, n_small, sem):
    def big(g, carry):
        _rows_copy(src_ref, src0 + g * BIG_ROWS, dst_ref, dst0 + g * BIG_ROWS, BIG_ROWS, sem).start()
        return carry

    lax.fori_loop(0, n_big, big, 0)
    done = n_big * BIG_ROWS

    def small(g, carry):
        _rows_copy(src_ref, src0 + done + g * GROUP, dst_ref, dst0 + done + g * GROUP, GROUP, sem).start()
        return carry

    lax.fori_loop(0, n_small, small, 0)


def _wait_pieces(n_big, n_small, src_ref, dst_ref, sem):
    def big(g, carry):
        _rows_copy(src_ref, 0, dst_ref, 0, BIG_ROWS, sem).wait()
        return carry

    lax.fori_loop(0, n_big, big, 0)

    def small(g, carry):
        _rows_copy(src_ref, 0, dst_ref, 0, GROUP, sem).wait()
        return carry

    lax.fori_loop(0, n_small, small, 0)


def _segment_table(seg, n_sorted_tiles):
    pieces = lambda rows: (rows // BIG_ROWS, (rows % BIG_ROWS) // GROUP)
    local_off = jnp.cumsum(seg, axis=1) - seg
    expert_rows = jnp.sum(seg, axis=0)
    expert_pad = ((expert_rows + ROW_TILE - 1) // ROW_TILE) * ROW_TILE
    expert_end = jnp.cumsum(expert_pad)
    expert_off = expert_end - expert_pad
    seg_start = expert_off[None, :] + jnp.cumsum(seg, axis=0) - seg
    n_used = (expert_end[-1] // ROW_TILE).reshape(1)
    tile_ids = jnp.arange(n_sorted_tiles, dtype=jnp.int32)
    tile_expert = jnp.minimum(jnp.sum(tile_ids[:, None] >= (expert_end // ROW_TILE)[None, :], axis=1), N_EXPERTS - 1)
    tile_expert = jnp.where(tile_ids < n_used[0], tile_expert, tile_expert[n_used[0] - 1])
    seg_big, seg_small = pieces(seg)
    tail_big, tail_small = pieces(expert_pad - expert_rows)
    table = jnp.concatenate([
        jnp.stack([seg_start, seg_big, seg_small, local_off], axis=-1).reshape(-1),
        jnp.stack([jnp.sum(seg_big, axis=1), jnp.sum(seg_small, axis=1)], axis=-1).reshape(-1),
        jnp.stack([expert_off + expert_rows, tail_big, tail_small], axis=-1).reshape(-1),
        n_used])
    return table.astype(jnp.int32), tile_expert.astype(jnp.int32), n_used.astype(jnp.int32)


def _table_sections(n_tiles):
    tiles0 = SEG_FIELDS * n_tiles * N_EXPERTS
    tails0 = tiles0 + TILE_FIELDS * n_tiles
    return tiles0, tails0, tails0 + TAIL_FIELDS * N_EXPERTS


def _dispatch_body(tab_ref, info_ref, xp_ref, xs_ref, nrm_ref, sel_ref, out_ref, loc_scr, z_scr, sems, *,
                   n_prompt_tiles, n_tiles, n_sorted_tiles):
    i = pl.program_id(0)
    slot = i % 2
    tiles0, tails0, used0 = _table_sections(n_tiles)

    def wait_tile(t, s):
        _wait_pieces(tab_ref[tiles0 + TILE_FIELDS * t], tab_ref[tiles0 + TILE_FIELDS * t + 1], loc_scr.at[s], out_ref,
                     sems.at[s])

    @pl.when(i >= 2)
    def _():
        wait_tile(i - 2, slot)

    def run(x_ref):
        hb = _rms(x_ref[...], nrm_ref[...]).astype(BF16)
        info = info_ref[...]
        lane = lax.broadcasted_iota(jnp.int32, info.shape, 1)
        dest = jnp.where(lane < TOP_K, info, 0.0)
        drow = sum(_dot_nt(sel_ref[...], part) for part in _split3(dest))
        r = lax.broadcasted_iota(jnp.int32, (LOCAL_ROWS, ROW_TILE), 0).astype(F32)
        perm = jnp.where(r == drow[0:1], 1.0, jnp.where(r == drow[1:2], 1.0, 0.0)).astype(BF16)
        loc_scr[slot] = _dot(perm, hb)

    _on_row_source(i, n_prompt_tiles, run, (xp_ref,), (xs_ref,))

    for e in range(N_EXPERTS):
        base = SEG_FIELDS * (i * N_EXPERTS + e)
        _start_pieces(loc_scr.at[slot], tab_ref[base + 3], out_ref, tab_ref[base], tab_ref[base + 1],
                      tab_ref[base + 2], sems.at[slot])

    @pl.when(i == n_tiles - 1)
    def _():
        wait_tile(i, slot)
        if n_tiles > 1:
            wait_tile(i - 1, 1 - slot)
        z_scr[...] = jnp.zeros_like(z_scr)
        for e in range(N_EXPERTS):
            base = tails0 + TAIL_FIELDS * e
            _start_pieces(z_scr, 0, out_ref, tab_ref[base], tab_ref[base + 1], tab_ref[base + 2], sems.at[2])
            _wait_pieces(tab_ref[base + 1], tab_ref[base + 2], z_scr, out_ref, sems.at[2])

        def zero_tile(t, carry):
            cp = _rows_copy(z_scr, 0, out_ref, t * ROW_TILE, ROW_TILE, sems.at[2])
            cp.start()
            cp.wait()
            return carry

        lax.fori_loop(tab_ref[used0], n_sorted_tiles, zero_tile, 0)


def _dispatch(table, info, xp, xs, nrm, sel, n_sorted_tiles):
    n_prompt_tiles = xp.shape[0] // ROW_TILE
    n_tiles = n_prompt_tiles + xs.shape[0] // ROW_TILE
    whole = lambda *shape: pl.BlockSpec(shape, lambda i, tab: (0,) * len(shape))
    grid_spec = pltpu.PrefetchScalarGridSpec(
        num_scalar_prefetch=1, grid=(n_tiles,),
        in_specs=[pl.BlockSpec((ROW_TILE, LANES), lambda i, tab: (i, 0)), *_two_source_specs(n_prompt_tiles),
                  whole(1, D_MODEL), whole(GROUP, LANES)],
        out_specs=pl.BlockSpec(memory_space=pl.ANY),
        scratch_shapes=[pltpu.VMEM((2, LOCAL_ROWS, D_MODEL), F32), pltpu.VMEM((ROW_TILE, D_MODEL), F32),
                        pltpu.SemaphoreType.DMA((3,))])
    return pl.pallas_call(
        functools.partial(_dispatch_body, n_prompt_tiles=n_prompt_tiles, n_tiles=n_tiles,
                          n_sorted_tiles=n_sorted_tiles),
        grid_spec=grid_spec, out_shape=jax.ShapeDtypeStruct((n_sorted_tiles * ROW_TILE, D_MODEL), F32),
        compiler_params=_cparams(1), name="moe_dispatch")(table, info, xp, xs, nrm, sel)


def _combine_body(tab_ref, info_ref, xp_ref, xs_ref, e_ref, yp_ref, ys_ref, loc_scr, sems, *, n_prompt_tiles,
                  n_tiles):
    i = pl.program_id(0)
    slot = i % 2
    tiles0, _, _ = _table_sections(n_tiles)

    def fetch(t, s):
        for e in range(N_EXPERTS):
            base = SEG_FIELDS * (t * N_EXPERTS + e)
            _start_pieces(e_ref, tab_ref[base], loc_scr.at[s], tab_ref[base + 3], tab_ref[base + 1],
                          tab_ref[base + 2], sems.at[s])

    @pl.when(i == 0)
    def _():
        loc_scr[...] = jnp.zeros_like(loc_scr)
        fetch(0, 0)

    @pl.when(i + 1 < n_tiles)
    def _():
        fetch(i + 1, 1 - slot)

    _wait_pieces(tab_ref[tiles0 + TILE_FIELDS * i], tab_ref[tiles0 + TILE_FIELDS * i + 1], e_ref, loc_scr.at[slot],
                 sems.at[slot])

    def run(x_ref, y_ref):
        info = info_ref[...]
        eb = loc_scr[slot].astype(BF16)
        lane = lax.broadcasted_iota(jnp.int32, (ROW_TILE, LOCAL_ROWS), 1).astype(F32)
        pick = jnp.zeros((ROW_TILE, LOCAL_ROWS), F32)
        for k in range(TOP_K):
            pick = jnp.where(lane == info[:, k:k + 1], info[:, TOP_K + k:TOP_K + k + 1], pick)
        y_ref[...] = x_ref[...] + _dot(pick.astype(BF16), eb)

    _on_row_source(i, n_prompt_tiles, run, (xp_ref, yp_ref), (xs_ref, ys_ref))


def _combine(table, info, xp, xs, esorted):
    n_prompt_tiles = xp.shape[0] // ROW_TILE
    n_tiles = n_prompt_tiles + xs.shape[0] // ROW_TILE
    grid_spec = pltpu.PrefetchScalarGridSpec(
        num_scalar_prefetch=1, grid=(n_tiles,),
        in_specs=[pl.BlockSpec((ROW_TILE, LANES), lambda i, tab: (i, 0)), *_two_source_specs(n_prompt_tiles),
                  pl.BlockSpec(memory_space=pl.ANY)],
        out_specs=list(_two_source_specs(n_prompt_tiles)),
        scratch_shapes=[pltpu.VMEM((2, LOCAL_ROWS, D_MODEL), F32), pltpu.SemaphoreType.DMA((2,))])
    return pl.pallas_call(
        functools.partial(_combine_body, n_prompt_tiles=n_prompt_tiles, n_tiles=n_tiles),
        grid_spec=grid_spec,
        out_shape=[jax.ShapeDtypeStruct(xp.shape, F32), jax.ShapeDtypeStruct(xs.shape, F32)],
        compiler_params=_cparams(1), name="moe_combine")(table, info, xp, xs, esorted)


def _rope_tables(pos):
    half = C_HDIM // 2
    inv = 1.0 / (ROPE_THETA ** (jnp.arange(half, dtype=F32) / half))
    ang = pos.astype(F32)[:, None] * inv[None, :]
    cos = jnp.cos(ang)
    sin = jnp.sin(ang)
    reps = LANES // C_HDIM
    return jnp.tile(jnp.concatenate([cos, cos], axis=-1), (1, reps)), jnp.tile(jnp.concatenate([-sin, sin], axis=-1),
                                                                                 (1, reps))


def kernel(x_prompt, x_sample, mem_prompt, cache_mem_k, cache_mem_v, state_hgrn, cache_swa_k, cache_swa_v, norm_mix, norm_xattn, norm_ffn, even_w_in, even_w_out, gmlp_w_s, gmlp_b_s, gmlp_ln_g, gmlp_ln_b, hgrn_lb_logits, hgrn_out_norm, attn_w_in, attn_w_out, attn_q_norm, attn_k_norm, attn_sinks, xattn_mem_norm, xattn_w_q, xattn_w_k, xattn_w_v, xattn_w_o, xattn_q_norm, xattn_k_norm, ffn_w1, ffn_w3, ffn_w2, moe_router, moe_w1, moe_w3, moe_w2):
    n_batch, seq, d = x_prompt.shape
    dec_batch, dec_seq, _ = x_sample.shape
    n_mem = mem_prompt.shape[1]
    depth = norm_mix.shape[0]
    past_len = PAST_LEN
    assert d == D_MODEL and depth == 2 and seq % ROW_TILE == 0 and dec_batch * dec_seq == ROW_TILE
    assert dec_seq == CHUNK and ROW_TILE % n_mem == 0 and cache_swa_k.shape[2] == WINDOW
    assert even_w_in.shape[-1] == EVEN_IN and attn_w_in.shape[-1] == ODD_IN and ffn_w1.shape[-1] == D_FF
    assert moe_w1.shape[1] == N_EXPERTS
    n_prompt_rows = n_batch * seq
    n_rows = n_prompt_rows + dec_batch * dec_seq
    n_prompt_tiles = n_prompt_rows // ROW_TILE
    tiles_per_batch = seq // ROW_TILE
    bf = lambda w: w.astype(BF16)
    row = lambda g: g.reshape(1, -1).astype(F32)

    lb_all = jnp.cumsum(jax.nn.softmax(hgrn_lb_logits.astype(F32), axis=0), axis=0)
    wcum = jnp.asarray(_cumsum_matrix(), BF16)
    lmask = jnp.asarray(_level_masks(), F32)

    def even_consts(n):
        tril = jnp.tril(jnp.ones((n, n), bool))
        ws = jnp.where(tril[None], gmlp_w_s[0, :, :n, :n], 0.0).astype(BF16)
        bs = jnp.broadcast_to(gmlp_b_s[0, :, :n, None], (A_GROUPS, n, LANES)).astype(F32)
        return [row(norm_mix[0]), bf(even_w_in[0]), bf(even_w_out[0]), ws, bs,
                gmlp_ln_g[0].reshape(A_GROUPS, 1, A_GDIM), gmlp_ln_b[0].reshape(A_GROUPS, 1, A_GDIM),
                row(lb_all[0]), row(hgrn_out_norm[0]), wcum, lmask]

    xp, hgrn_p = _even_mixer(
        x_prompt.reshape(n_prompt_rows, d), jnp.zeros((n_batch, B_HEADS, B_HDIM, B_HDIM), F32),
        even_consts(A_CHUNK), n_batch=n_batch, tiles_per_batch=tiles_per_batch, n_seq=1, seq_rows=ROW_TILE,
        gchunk=A_CHUNK, emit_v=False)
    xs, hgrn_s, gmlp_v = _even_mixer(
        x_sample.reshape(ROW_TILE, d), state_hgrn[0], even_consts(min(A_CHUNK, dec_seq)), n_batch=1,
        tiles_per_batch=1, n_seq=dec_batch, seq_rows=dec_seq, gchunk=min(A_CHUNK, dec_seq), emit_v=True)

    mem_k, mem_v = _memory_kv(mem_prompt.reshape(n_batch * n_mem, d), xattn_mem_norm.reshape(depth, 1, d),
                              bf(xattn_w_k), bf(xattn_w_v), xattn_k_norm.reshape(depth, 1, X_HDIM),
                              n_batch=n_batch, n_mem=n_mem)

    def cross_attention(xp, xs, l):
        consts = (row(norm_xattn[l]), bf(xattn_w_q[l]), bf(xattn_w_o[l]), row(xattn_q_norm[l]))
        xp = _xattn(xp, mem_k[l], mem_v[l], *consts, steps_per_mem=tiles_per_batch, n_seq=1, seq_rows=ROW_TILE)
        xs = _xattn(xs, cache_mem_k[l].reshape(dec_batch, n_mem, X_DIM),
                    cache_mem_v[l].reshape(dec_batch, n_mem, X_DIM), *consts, steps_per_mem=1, n_seq=dec_batch,
                    seq_rows=dec_seq)
        return xp, xs

    xp, xs = cross_attention(xp, xs, 0)
    xp, xs = _dense_ffn(xp, xs, row(norm_ffn[0]), bf(ffn_w1[0]), bf(ffn_w3[0]), bf(ffn_w2[0]))

    reps = D_MODEL // C_HDIM
    bd = jnp.asarray(np.kron(np.eye(reps, dtype=np.float32), np.full((C_HDIM, C_HDIM), 1.0 / C_HDIM, np.float32)),
                     BF16)
    swa_consts = [row(norm_mix[1]), bf(attn_w_in[0]), bf(attn_w_out[0]), row(jnp.tile(attn_q_norm[0], C_HEADS)),
                  row(jnp.tile(attn_k_norm[0], C_KV_HEADS)), bd]
    sinks = attn_sinks[0].astype(F32)
    cos_p, sin_p = _rope_tables(jnp.arange(seq, dtype=jnp.int32))
    cos_s, sin_s = _rope_tables(past_len + jnp.arange(dec_seq, dtype=jnp.int32))
    no_past = jnp.zeros((n_batch, WINDOW, C_KV_DIM), F32)
    xp, swk_p, swv_p = _swa(xp, no_past, no_past, cos_p, sin_p, sinks, swa_consts, n_batch=n_batch,
                            tiles_per_batch=tiles_per_batch, n_seq=1, seq_rows=ROW_TILE, past_valid=False)
    xs, swk_s, swv_s = _swa(xs, cache_swa_k[0].reshape(dec_batch, WINDOW, C_KV_DIM),
                            cache_swa_v[0].reshape(dec_batch, WINDOW, C_KV_DIM),
                            jnp.tile(cos_s, (dec_batch, 1)), jnp.tile(sin_s, (dec_batch, 1)), sinks, swa_consts,
                            n_batch=1, tiles_per_batch=1, n_seq=dec_batch, seq_rows=dec_seq, past_valid=True)
    xp, xs = cross_attention(xp, xs, 1)

    n_tiles = n_rows // ROW_TILE
    router_w = jnp.zeros((d, LANES), F32).at[:, :N_EXPERTS].set(moe_router[0].astype(F32))
    rhi = router_w.astype(BF16)
    rlo = (router_w - rhi.astype(F32)).astype(BF16)
    lstrict = jnp.asarray(np.tril(np.ones((ROW_TILE, ROW_TILE), np.float32), -1), BF16)
    su = jnp.asarray(np.triu(np.ones((LANES, LANES), np.float32), 1), BF16)
    sel = jnp.asarray(np.eye(GROUP, LANES, dtype=np.float32), BF16)
    nrm_ffn = row(norm_ffn[1])
    info, seg = _router(xp, xs, nrm_ffn, rhi, rlo, lstrict, su)
    seg = seg.reshape(n_tiles, GROUP, LANES)[:, 0, :N_EXPERTS].astype(jnp.int32)
    n_sorted_tiles = -(-(TOP_K * n_rows + n_tiles * N_EXPERTS * (GROUP - 1) + N_EXPERTS * (ROW_TILE - 1)) // ROW_TILE)
    table, tile_expert, n_used = _segment_table(seg, n_sorted_tiles)
    hsorted = _dispatch(table, info, xp, xs, nrm_ffn, sel, n_sorted_tiles)
    esorted = _grouped_ffn(hsorted, tile_expert, n_used, bf(moe_w1[0]), bf(moe_w3[0]), bf(moe_w2[0]))
    y_prompt, y_sample = _combine(table, info, xp, xs, esorted)

    n_even = state_hgrn.shape[0]
    n_odd = cache_swa_k.shape[0]
    return (y_prompt.reshape(n_batch, seq, d), y_sample.reshape(dec_batch, dec_seq, d),
            mem_k.reshape(depth, n_batch, n_mem, X_HEADS, X_HDIM), mem_v.reshape(depth, n_batch, n_mem, X_HEADS, X_HDIM),
            hgrn_p.reshape(n_even, n_batch, B_HEADS, B_HDIM, B_HDIM),
            gmlp_v.reshape(n_even, dec_batch, dec_seq, A_GROUPS, A_GDIM),
            hgrn_s.reshape(n_even, dec_batch, B_HEADS, B_HDIM, B_HDIM),
            swk_p.reshape(n_odd, n_batch, WINDOW, C_KV_HEADS, C_HDIM), swv_p.reshape(n_odd, n_batch, WINDOW, C_KV_HEADS, C_HDIM),
            swk_s.reshape(n_odd, dec_batch, dec_seq, C_KV_HEADS, C_HDIM), swv_s.reshape(n_odd, dec_batch, dec_seq, C_KV_HEADS, C_HDIM))
```

```python
import functools

import numpy as np
import jax
import jax.numpy as jnp
from jax import lax
from jax.experimental import pallas as pl
from jax.experimental.pallas import tpu as pltpu

F32 = jnp.float32
BF16 = jnp.bfloat16

D_MODEL = 1024
EPS = 1e-6
CHUNK = 64
A_GROUPS = 4
A_DIM = D_MODEL // 2
A_GDIM = A_DIM // A_GROUPS
A_CHUNK = 128
B_HEADS = 4
B_DIM = D_MODEL // 2
B_HDIM = B_DIM // B_HEADS
EVEN_IN = 2 * A_DIM + 4 * B_DIM
C_HEADS = 16
C_KV_HEADS = 4
C_HDIM = D_MODEL // C_HEADS
C_GROUP = C_HEADS // C_KV_HEADS
C_KV_DIM = C_KV_HEADS * C_HDIM
WINDOW = 128
ROPE_THETA = 10000.0
PAST_LEN = 4096
ODD_IN = (C_HEADS + 2 * C_KV_HEADS) * C_HDIM
X_HEADS = 4
X_HDIM = 128
X_DIM = X_HEADS * X_HDIM
D_FF = 2816
N_EXPERTS = 8
TOP_K = 2

LANES = 128
ROW_TILE = 512
FF_CHUNK = 2816
N_FF_CHUNKS = D_FF // FF_CHUNK
FF_SUB = 512
N_LEVELS = 6
HGRN_STAGE_CHUNKS = 2
SWA_STAGE_CHUNKS = 2
GROUP = 8
LOCAL_ROWS = -(-(TOP_K * ROW_TILE + N_EXPERTS * (GROUP - 1)) // LANES) * LANES
VMEM_LIMIT = 58 * 1024 * 1024


def _cparams(n_axes):
    return pltpu.CompilerParams(dimension_semantics=("arbitrary",) * n_axes, vmem_limit_bytes=VMEM_LIMIT)


def _dot(a, b):
    return jnp.dot(a, b, preferred_element_type=F32)


def _dot_nt(a, b):
    return lax.dot_general(a, b, (((1,), (1,)), ((), ())), preferred_element_type=F32)


def _rms(x, g):
    return x * lax.rsqrt(jnp.mean(x * x, axis=-1, keepdims=True) + EPS) * g


def _silu(x):
    return x * (1.0 / (1.0 + jnp.exp(-x)))


def _split3(x):
    hi = x.astype(BF16)
    r1 = x - hi.astype(F32)
    mid = r1.astype(BF16)
    lo = (r1 - mid.astype(F32)).astype(BF16)
    return hi, mid, lo


def _cumsum_matrix():
    r = np.arange(CHUNK)
    s = np.arange(CHUNK)
    blocks = [(s[None, :] <= r[:, None])]
    for l in range(N_LEVELS):
        h = 1 << l
        ref = (r & ~(2 * h - 1)) + h - 1
        blocks.append(s[None, :] <= ref[:, None])
    w = np.concatenate(blocks, axis=0).astype(np.float32)
    return np.concatenate([w, w, w], axis=1)


def _level_masks():
    t = np.arange(CHUNK)[:, None]
    s = np.arange(CHUNK)[None, :]
    masks = []
    for l in range(N_LEVELS):
        masks.append(((t >> (l + 1)) == (s >> (l + 1))) & (((t >> l) & 1) == 1) & (((s >> l) & 1) == 0))
    masks.append(t == s)
    return np.stack(masks).astype(np.float32)


def _even_mixer_body(x_ref, s0_ref, nrm_ref, win_ref, wout_ref, ws_ref, bs_ref, lng_ref, lnb_ref, lb_ref, og_ref,
                     wcum_ref, lmask_ref, *rest, n_seq, seq_rows, gchunk, emit_v):
    if emit_v:
        y_ref, sout_ref, v_ref, proj_scr, mixed_scr, st_scr = rest
    else:
        y_ref, sout_ref, proj_scr, mixed_scr, st_scr = rest
        v_ref = None
    rows = n_seq * seq_rows
    j = pl.program_id(1)

    @pl.when(j == 0)
    def _():
        for s in range(n_seq):
            for hd in range(B_HEADS):
                st_scr[s * B_HEADS + hd] = s0_ref[s, hd].T

    x = x_ref[...]
    h = _rms(x, nrm_ref[...]).astype(BF16)
    n_pieces = EVEN_IN // ROW_TILE
    for n in range(n_pieces):
        cs = slice(n * ROW_TILE, (n + 1) * ROW_TILE)
        proj_scr[:, cs] = _dot(h, win_ref[:, cs])

    groups = range(A_GROUPS)
    group_cols = lambda base, g: slice(base + g * A_GDIM, base + (g + 1) * A_GDIM)
    vgs = [jax.nn.gelu(proj_scr[:, group_cols(A_DIM, g)]) for g in groups]
    means = [jnp.mean(vg, axis=-1, keepdims=True) for vg in vgs]
    vcs = [vg - mu for vg, mu in zip(vgs, means)]
    variances = [jnp.mean(vc * vc, axis=-1, keepdims=True) for vc in vcs]
    for g in groups:
        gs = group_cols(0, g)
        vn = vcs[g] * lax.rsqrt(variances[g] + EPS) * lng_ref[g] + lnb_ref[g]
        if v_ref is not None:
            v_ref[:, gs] = vn
        ug = jax.nn.gelu(proj_scr[:, gs])
        vb = vn.astype(BF16)
        for c in range(rows // gchunk):
            rs = slice(c * gchunk, (c + 1) * gchunk)
            sp = _dot(ws_ref[g], vb[rs]) + bs_ref[g]
            mixed_scr[rs, gs] = ug[rs] * sp

    q0, f0, i0, g0 = (2 * A_DIM + k * B_DIM for k in range(4))
    lb = lb_ref[...]
    fg = lb + (1.0 - lb) * jax.nn.sigmoid(proj_scr[:, f0:f0 + B_DIM])
    proj_scr[:, 0:B_DIM] = jnp.log(fg)
    proj_scr[:, B_DIM:2 * B_DIM] = 1.0 - fg
    proj_scr[:, q0:q0 + B_DIM] = _silu(proj_scr[:, q0:q0 + B_DIM])
    og = og_ref[...]
    chunks_per_seq = seq_rows // CHUNK

    heads = range(B_HEADS)
    head_cols = lambda base, hd: slice(base + hd * B_HDIM, base + (hd + 1) * B_HDIM)
    chunk_rows = lambda c: slice(c * CHUNK, (c + 1) * CHUNK)

    def chunk_local(c):
        rs = chunk_rows(c)
        hi, mid, lo = _split3(proj_scr[rs, 0:B_DIM])
        gg = _dot(wcum_ref[...], jnp.concatenate([hi, mid, lo], axis=0))
        vs = [proj_scr[rs, head_cols(i0, hd)] for hd in heads]
        vts = [v.T.astype(BF16) for v in vs]
        Gs = [gg[0:CHUNK, head_cols(0, hd)] for hd in heads]
        qs = [proj_scr[rs, head_cols(q0, hd)] for hd in heads]
        ks = [proj_scr[rs, head_cols(B_DIM, hd)] for hd in heads]
        operands = []
        for hd in heads:
            for l in range(N_LEVELS + 1):
                if l < N_LEVELS:
                    e = jnp.exp(-jnp.abs(Gs[hd] - gg[(l + 1) * CHUNK:(l + 2) * CHUNK, head_cols(0, hd)]))
                    operands.append(((qs[hd] * e).astype(BF16), (ks[hd] * e).astype(BF16)))
                else:
                    operands.append((qs[hd].astype(BF16), ks[hd].astype(BF16)))
        blocks = [_dot_nt(qe, ke) for qe, ke in operands]
        parts = []
        for hd in heads:
            att = jnp.zeros((CHUNK, CHUNK), F32)
            for l in range(N_LEVELS + 1):
                att = jnp.where(lmask_ref[l] > 0.5, blocks[hd * (N_LEVELS + 1) + l], att)
            g_end = Gs[hd][CHUNK - 1:CHUNK, :]
            kd = (ks[hd] * jnp.exp(g_end - Gs[hd])).astype(BF16)
            parts.append((_dot(att.astype(BF16), vs[hd].astype(BF16)), (qs[hd] * jnp.exp(Gs[hd])).astype(BF16),
                          jnp.exp(g_end), _dot(vts[hd], kd)))
        return parts

    def chunk_state(c, parts):
        rs = chunk_rows(c)
        sidx = (c // chunks_per_seq) * B_HEADS
        outs = []
        for hd in heads:
            o_local, q_decayed, decay, increment = parts[hd]
            st = st_scr[sidx + hd]
            outs.append(o_local + _dot_nt(q_decayed, st.astype(BF16)))
            st_scr[sidx + hd] = st * decay + increment
        for hd in heads:
            o = outs[hd]
            on = o * lax.rsqrt(jnp.mean(o * o, axis=-1, keepdims=True) + EPS) * og
            gate = _silu(proj_scr[rs, head_cols(g0, hd)])
            mixed_scr[rs, head_cols(A_DIM, hd)] = on * gate

    n_chunks = rows // CHUNK
    for first in range(0, n_chunks, HGRN_STAGE_CHUNKS):
        group = range(first, min(first + HGRN_STAGE_CHUNKS, n_chunks))
        local = [chunk_local(c) for c in group]
        for c, parts in zip(group, local):
            chunk_state(c, parts)

    y_ref[...] = x + _dot(mixed_scr[...].astype(BF16), wout_ref[...])

    @pl.when(j == pl.num_programs(1) - 1)
    def _():
        for s in range(n_seq):
            for hd in range(B_HEADS):
                sout_ref[s, hd] = st_scr[s * B_HEADS + hd].T


def _even_mixer(x, s0, consts, *, n_batch, tiles_per_batch, n_seq, seq_rows, gchunk, emit_v):
    rows = n_seq * seq_rows
    grid = (n_batch, tiles_per_batch)
    tile = lambda b, j: (b * tiles_per_batch + j, 0)
    whole = lambda *shape: pl.BlockSpec(shape, lambda b, j: (0,) * len(shape))
    in_specs = [
        pl.BlockSpec((rows, D_MODEL), tile),
        pl.BlockSpec((n_seq, B_HEADS, B_HDIM, B_HDIM), lambda b, j: (b, 0, 0, 0)),
        whole(1, D_MODEL), whole(D_MODEL, EVEN_IN), whole(D_MODEL, D_MODEL),
        whole(A_GROUPS, gchunk, gchunk), whole(A_GROUPS, gchunk, LANES),
        whole(A_GROUPS, 1, A_GDIM), whole(A_GROUPS, 1, A_GDIM), whole(1, B_DIM), whole(1, B_HDIM),
        whole((N_LEVELS + 1) * CHUNK, 3 * CHUNK), whole(N_LEVELS + 1, CHUNK, CHUNK),
    ]
    args = [x, s0] + list(consts)
    out_shape = [jax.ShapeDtypeStruct(x.shape, F32),
                 jax.ShapeDtypeStruct((n_batch * n_seq, B_HEADS, B_HDIM, B_HDIM), F32)]
    out_specs = [pl.BlockSpec((rows, D_MODEL), tile),
                 pl.BlockSpec((n_seq, B_HEADS, B_HDIM, B_HDIM), lambda b, j: (b, 0, 0, 0))]
    if emit_v:
        out_shape.append(jax.ShapeDtypeStruct((x.shape[0], A_DIM), F32))
        out_specs.append(pl.BlockSpec((rows, A_DIM), tile))
    body = functools.partial(_even_mixer_body, n_seq=n_seq, seq_rows=seq_rows, gchunk=gchunk, emit_v=emit_v)
    return pl.pallas_call(
        body, grid=grid, in_specs=in_specs, out_specs=out_specs, out_shape=out_shape,
        scratch_shapes=[pltpu.VMEM((rows, EVEN_IN), F32), pltpu.VMEM((rows, D_MODEL), F32),
                        pltpu.VMEM((n_seq * B_HEADS, B_HDIM, B_HDIM), F32)],
        compiler_params=_cparams(2),
        name="even_mixer_s" if emit_v else "even_mixer_p")(*args)


def _memory_kv_body(mem_ref, gm_ref, wk_ref, wv_ref, kg_ref, mk_ref, mv_ref, *, n_seq, n_mem):
    m = _rms(mem_ref[...], gm_ref[0]).astype(BF16)
    kk = _dot(m, wk_ref[0])
    vv = _dot(m, wv_ref[0])
    kg = kg_ref[0]
    for hd in range(X_HEADS):
        hs = slice(hd * X_HDIM, (hd + 1) * X_HDIM)
        kh = _rms(kk[:, hs], kg)
        for s in range(n_seq):
            mk_ref[0, s, :, hs] = kh[s * n_mem:(s + 1) * n_mem]
    for s in range(n_seq):
        mv_ref[0, s] = vv[s * n_mem:(s + 1) * n_mem]


def _memory_kv(mem2d, g_mem, w_k, w_v, k_g, *, n_batch, n_mem):
    depth = w_k.shape[0]
    n_seq = ROW_TILE // n_mem
    out = jax.ShapeDtypeStruct((depth, n_batch, n_mem, X_DIM), F32)
    ospec = pl.BlockSpec((1, n_seq, n_mem, X_DIM), lambda l, t: (l, t, 0, 0))
    return pl.pallas_call(
        functools.partial(_memory_kv_body, n_seq=n_seq, n_mem=n_mem),
        grid=(depth, n_batch // n_seq),
        in_specs=[pl.BlockSpec((ROW_TILE, D_MODEL), lambda l, t: (t, 0)),
                  pl.BlockSpec((1, 1, D_MODEL), lambda l, t: (l, 0, 0)),
                  pl.BlockSpec((1, D_MODEL, X_DIM), lambda l, t: (l, 0, 0)),
                  pl.BlockSpec((1, D_MODEL, X_DIM), lambda l, t: (l, 0, 0)),
                  pl.BlockSpec((1, 1, X_HDIM), lambda l, t: (l, 0, 0))],
        out_specs=[ospec, ospec], out_shape=[out, out], compiler_params=_cparams(2),
        name="memory_kv")(mem2d, g_mem, w_k, w_v, k_g)


def _xattn_body(x_ref, mk_ref, mv_ref, nrm_ref, wq_ref, wo_ref, qg_ref, y_ref, o_scr, *, n_seq, seq_rows):
    x = x_ref[...]
    h = _rms(x, nrm_ref[...]).astype(BF16)
    q = _dot(h, wq_ref[...])
    qg = qg_ref[...] * (X_HDIM ** -0.5)
    units = [(s, hd) for s in range(n_seq) for hd in range(X_HEADS)]
    rows_of = lambda s: slice(s * seq_rows, (s + 1) * seq_rows)
    lanes_of = lambda hd: slice(hd * X_HDIM, (hd + 1) * X_HDIM)
    qh = [_rms(q[rows_of(s), lanes_of(hd)], qg).astype(BF16) for s, hd in units]
    sc = [_dot_nt(qh[n], mk_ref[s, :, lanes_of(hd)].astype(BF16)) for n, (s, hd) in enumerate(units)]
    m = [jnp.max(t, axis=-1, keepdims=True) for t in sc]
    p = [jnp.exp(t - mx) for t, mx in zip(sc, m)]
    den = [jnp.sum(t, axis=-1, keepdims=True) for t in p]
    for n, (s, hd) in enumerate(units):
        o_scr[rows_of(s), lanes_of(hd)] = _dot(p[n].astype(BF16), mv_ref[s, :, lanes_of(hd)].astype(BF16)) / den[n]
    y_ref[...] = x + _dot(o_scr[...].astype(BF16), wo_ref[...])


def _xattn(x, mk, mv, nrm, wq, wo, qg, *, steps_per_mem, n_seq, seq_rows):
    rows = n_seq * seq_rows
    n_mem = mk.shape[1]
    whole = lambda *shape: pl.BlockSpec(shape, lambda t: (0,) * len(shape))
    return pl.pallas_call(
        functools.partial(_xattn_body, n_seq=n_seq, seq_rows=seq_rows),
        grid=(x.shape[0] // rows,),
        in_specs=[pl.BlockSpec((rows, D_MODEL), lambda t: (t, 0)),
                  pl.BlockSpec((n_seq, n_mem, X_DIM), lambda t: (t // steps_per_mem, 0, 0)),
                  pl.BlockSpec((n_seq, n_mem, X_DIM), lambda t: (t // steps_per_mem, 0, 0)),
                  whole(1, D_MODEL), whole(D_MODEL, X_DIM), whole(X_DIM, D_MODEL), whole(1, X_HDIM)],
        out_specs=pl.BlockSpec((rows, D_MODEL), lambda t: (t, 0)),
        out_shape=jax.ShapeDtypeStruct(x.shape, F32),
        scratch_shapes=[pltpu.VMEM((rows, X_DIM), F32)],
        compiler_params=_cparams(1), name="xattn_s" if n_seq > 1 else "xattn_p")(x, mk, mv, nrm, wq, wo, qg)


def _ff_chunk(i, j):
    return jnp.where(i % 2 == 0, j, N_FF_CHUNKS - 1 - j)


def _two_source_specs(n_prompt_tiles, width=D_MODEL):
    return (pl.BlockSpec((ROW_TILE, width), lambda i, *_: (jnp.minimum(i, n_prompt_tiles - 1), 0)),
            pl.BlockSpec((ROW_TILE, width), lambda i, *_: (jnp.maximum(i - n_prompt_tiles, 0), 0)))


def _on_row_source(i, n_prompt_tiles, fn, prompt_refs, sample_refs):
    pl.when(i < n_prompt_tiles)(lambda: fn(*prompt_refs))
    pl.when(i >= n_prompt_tiles)(lambda: fn(*sample_refs))


def _swiglu_part(hb, w1, w3, w2):
    part = None
    for c0 in range(0, FF_CHUNK, FF_SUB):
        cs = slice(c0, min(c0 + FF_SUB, FF_CHUNK))
        act = (_silu(_dot(hb, w1[:, cs])) * _dot(hb, w3[:, cs])).astype(BF16)
        p = _dot(act, w2[cs, :])
        part = p if part is None else part + p
    return part


def _dense_ffn_body(xp_ref, xs_ref, nrm_ref, w1_ref, w3_ref, w2_ref, yp_ref, ys_ref, h_scr, *, n_prompt_tiles):
    j = pl.program_id(1)

    def run(x_ref, y_ref):
        @pl.when(j == 0)
        def _():
            h_scr[...] = _rms(x_ref[...], nrm_ref[...]).astype(BF16)

        part = _swiglu_part(h_scr[...], w1_ref, w3_ref, w2_ref)

        @pl.when(j == 0)
        def _():
            y_ref[...] = x_ref[...] + part

        @pl.when(j > 0)
        def _():
            y_ref[...] += part

    _on_row_source(pl.program_id(0), n_prompt_tiles, run, (xp_ref, yp_ref), (xs_ref, ys_ref))


def _dense_ffn(xp, xs, nrm, w1, w3, w2):
    n_prompt_tiles = xp.shape[0] // ROW_TILE
    n_tiles = n_prompt_tiles + xs.shape[0] // ROW_TILE
    return pl.pallas_call(
        functools.partial(_dense_ffn_body, n_prompt_tiles=n_prompt_tiles), grid=(n_tiles, N_FF_CHUNKS),
        in_specs=[*_two_source_specs(n_prompt_tiles),
                  pl.BlockSpec((1, D_MODEL), lambda i, j: (0, 0)),
                  pl.BlockSpec((D_MODEL, FF_CHUNK), lambda i, j: (0, _ff_chunk(i, j))),
                  pl.BlockSpec((D_MODEL, FF_CHUNK), lambda i, j: (0, _ff_chunk(i, j))),
                  pl.BlockSpec((FF_CHUNK, D_MODEL), lambda i, j: (_ff_chunk(i, j), 0))],
        out_specs=list(_two_source_specs(n_prompt_tiles)),
        out_shape=[jax.ShapeDtypeStruct(xp.shape, F32), jax.ShapeDtypeStruct(xs.shape, F32)],
        scratch_shapes=[pltpu.VMEM((ROW_TILE, D_MODEL), BF16)],
        compiler_params=_cparams(2), name="dense_ffn")(xp, xs, nrm, w1, w3, w2)


def _grouped_ffn_body(te_ref, nu_ref, x_ref, w1_ref, w3_ref, w2_ref, y_ref):
    i = pl.program_id(0)
    j = pl.program_id(1)

    @pl.when(i < nu_ref[0])
    def _():
        part = _swiglu_part(x_ref[...].astype(BF16), w1_ref.at[0], w3_ref.at[0], w2_ref.at[0])

        @pl.when(j == 0)
        def _():
            y_ref[...] = part

        @pl.when(j > 0)
        def _():
            y_ref[...] += part

    @pl.when(i >= nu_ref[0])
    def _():
        y_ref[...] = jnp.zeros_like(y_ref)


def _grouped_ffn(xsorted, tile_expert, n_used, w1, w3, w2):
    n_tiles = xsorted.shape[0] // ROW_TILE

    def row_tile(i, j, te, nu):
        return (i, 0)

    def ff(i, j, nu):
        last = nu[0] - 1
        return jnp.where(i <= last, _ff_chunk(i, j), _ff_chunk(last, N_FF_CHUNKS - 1))

    grid_spec = pltpu.PrefetchScalarGridSpec(
        num_scalar_prefetch=2, grid=(n_tiles, N_FF_CHUNKS),
        in_specs=[pl.BlockSpec((ROW_TILE, D_MODEL), row_tile),
                  pl.BlockSpec((1, D_MODEL, FF_CHUNK), lambda i, j, te, nu: (te[i], 0, ff(i, j, nu))),
                  pl.BlockSpec((1, D_MODEL, FF_CHUNK), lambda i, j, te, nu: (te[i], 0, ff(i, j, nu))),
                  pl.BlockSpec((1, FF_CHUNK, D_MODEL), lambda i, j, te, nu: (te[i], ff(i, j, nu), 0))],
        out_specs=pl.BlockSpec((ROW_TILE, D_MODEL), row_tile))
    return pl.pallas_call(
        _grouped_ffn_body, grid_spec=grid_spec, out_shape=jax.ShapeDtypeStruct(xsorted.shape, F32),
        compiler_params=_cparams(2), name="grouped_ffn")(tile_expert, n_used, xsorted, w1, w3, w2)


def _rope_slab(xs, cos, sin_signed, first_half):
    rot = jnp.where(first_half, pltpu.roll(xs, LANES - C_HDIM // 2, 1), pltpu.roll(xs, C_HDIM // 2, 1))
    return xs * cos + rot * sin_signed


def _swa_body(sink_ref, x_ref, pk_ref, pv_ref, cos_ref, sin_ref, nrm_ref, win_ref, wout_ref, qg_ref, kg_ref, bd_ref,
              y_ref, ko_ref, vo_ref, k_scr, v_scr, q_scr, a_scr, *, n_seq, seq_rows, past_valid):
    rows = n_seq * seq_rows
    j = pl.program_id(1)
    tail = min(WINDOW, seq_rows)

    if past_valid:
        for s in range(n_seq):
            k_scr[s, 0:WINDOW] = pk_ref[s]
            v_scr[s, 0:WINDOW] = pv_ref[s]
    else:
        @pl.when(j == 0)
        def _():
            for s in range(n_seq):
                k_scr[s, 0:WINDOW] = jnp.zeros((WINDOW, C_KV_DIM), F32)
                v_scr[s, 0:WINDOW] = jnp.zeros((WINDOW, C_KV_DIM), F32)

        @pl.when(j > 0)
        def _():
            for s in range(n_seq):
                k_scr[s, 0:WINDOW] = k_scr[s, seq_rows:seq_rows + WINDOW]
                v_scr[s, 0:WINDOW] = v_scr[s, seq_rows:seq_rows + WINDOW]

    x = x_ref[...]
    h = _rms(x, nrm_ref[...]).astype(BF16)
    q_dim = C_HEADS * C_HDIM
    q = _dot(h, win_ref[:, 0:q_dim])
    k = _dot(h, win_ref[:, q_dim:q_dim + C_KV_DIM])
    v = _dot(h, win_ref[:, q_dim + C_KV_DIM:q_dim + 2 * C_KV_DIM])

    cos = cos_ref[...]
    sin_signed = sin_ref[...]
    first_half = (lax.broadcasted_iota(jnp.int32, (rows, LANES), 1) % C_HDIM) < (C_HDIM // 2)
    bd = bd_ref[...]
    qn = q * lax.rsqrt(_dot((q * q).astype(BF16), bd) + EPS) * qg_ref[...]
    kn = k * lax.rsqrt(_dot((k * k).astype(BF16), bd[0:C_KV_DIM, 0:C_KV_DIM]) + EPS) * kg_ref[...]
    scale = C_HDIM ** -0.5
    for sl in range(q_dim // LANES):
        ls = slice(sl * LANES, (sl + 1) * LANES)
        q_scr[:, ls] = _rope_slab(qn[:, ls], cos, sin_signed, first_half) * scale
    for sl in range(C_KV_DIM // LANES):
        ls = slice(sl * LANES, (sl + 1) * LANES)
        kr = _rope_slab(kn[:, ls], cos, sin_signed, first_half)
        for s in range(n_seq):
            k_scr[s, WINDOW:WINDOW + seq_rows, ls] = kr[s * seq_rows:(s + 1) * seq_rows]
    for s in range(n_seq):
        v_scr[s, WINDOW:WINDOW + seq_rows] = v[s * seq_rows:(s + 1) * seq_rows]
        ko_ref[s] = k_scr[s, WINDOW + seq_rows - tail:WINDOW + seq_rows]
        vo_ref[s] = v_scr[s, WINDOW + seq_rows - tail:WINDOW + seq_rows]

    chunks_per_seq = seq_rows // CHUNK
    n_keys = WINDOW + CHUNK
    key_idx = lax.broadcasted_iota(jnp.int32, (C_GROUP * CHUNK, n_keys), 1)
    row_grp = lax.broadcasted_iota(jnp.int32, (C_GROUP * CHUNK, 1), 0) // CHUNK
    ones_keys = jnp.ones((n_keys, C_HDIM), BF16)

    head_sinks = []
    for kvh in range(C_KV_HEADS):
        sink = jnp.zeros((C_GROUP * CHUNK, 1), F32)
        for g in range(C_GROUP):
            sink = jnp.where(row_grp == g, sink_ref[kvh * C_GROUP + g], sink)
        head_sinks.append(sink)

    def attend(chunk_ids):
        units = [(idx, kvh) for idx in chunk_ids for kvh in range(C_KV_HEADS)]
        head_lanes = lambda kvh: slice(kvh * C_HDIM, (kvh + 1) * C_HDIM)
        q_rows = lambda idx: slice(idx * CHUNK, (idx + 1) * CHUNK)
        key_rows = lambda idx: slice((idx % chunks_per_seq) * CHUNK, (idx % chunks_per_seq) * CHUNK + n_keys)
        qs = [jnp.concatenate(
            [q_scr[q_rows(idx), (kvh * C_GROUP + g) * C_HDIM:(kvh * C_GROUP + g + 1) * C_HDIM]
             for g in range(C_GROUP)], axis=0).astype(BF16) for idx, kvh in units]
        kh = [k_scr[idx // chunks_per_seq, key_rows(idx), head_lanes(kvh)].astype(BF16) for idx, kvh in units]
        vh = [v_scr[idx // chunks_per_seq, key_rows(idx), head_lanes(kvh)].astype(BF16) for idx, kvh in units]
        sc = [_dot_nt(qn, kn) for qn, kn in zip(qs, kh)]
        if not past_valid:
            sc = [jnp.where((key_idx + (idx % chunks_per_seq) * CHUNK >= WINDOW) | (j > 0), sn, -jnp.inf)
                  for sn, (idx, kvh) in zip(sc, units)]
        m = [jnp.maximum(jnp.max(sn, axis=-1, keepdims=True), head_sinks[kvh]) for sn, (idx, kvh) in zip(sc, units)]
        p = [jnp.exp(sn - mn).astype(BF16) for sn, mn in zip(sc, m)]
        den = [_dot(pn, ones_keys) + jnp.exp(head_sinks[kvh] - mn) for pn, mn, (idx, kvh) in zip(p, m, units)]
        o = [_dot(pn, vn) / dn for pn, vn, dn in zip(p, vh, den)]
        for on, (idx, kvh) in zip(o, units):
            for g in range(C_GROUP):
                hs = slice((kvh * C_GROUP + g) * C_HDIM, (kvh * C_GROUP + g + 1) * C_HDIM)
                a_scr[q_rows(idx), hs] = on[g * CHUNK:(g + 1) * CHUNK]

    n_chunks = rows // CHUNK
    for first in range(0, n_chunks, SWA_STAGE_CHUNKS):
        attend(range(first, min(first + SWA_STAGE_CHUNKS, n_chunks)))
    y_ref[...] = x + _dot(a_scr[...].astype(BF16), wout_ref[...])


def _swa(x, pk, pv, cos, sin_signed, sinks, consts, *, n_batch, tiles_per_batch, n_seq, seq_rows, past_valid):
    rows = n_seq * seq_rows
    tail = min(WINDOW, seq_rows)
    n_str = n_batch * n_seq
    whole = lambda *shape: pl.BlockSpec(shape, lambda b, j, sk: (0,) * len(shape))
    in_specs = [pl.BlockSpec((rows, D_MODEL), lambda b, j, sk: (b * tiles_per_batch + j, 0)),
                pl.BlockSpec((n_seq, WINDOW, C_KV_DIM), lambda b, j, sk: (b, 0, 0)),
                pl.BlockSpec((n_seq, WINDOW, C_KV_DIM), lambda b, j, sk: (b, 0, 0)),
                pl.BlockSpec((rows, LANES), lambda b, j, sk: (j, 0)),
                pl.BlockSpec((rows, LANES), lambda b, j, sk: (j, 0)),
                whole(1, D_MODEL), whole(D_MODEL, ODD_IN), whole(D_MODEL, D_MODEL),
                whole(1, D_MODEL), whole(1, C_KV_DIM), whole(D_MODEL, D_MODEL)]
    args = [x, pk, pv, cos, sin_signed] + list(consts)
    kv_out = jax.ShapeDtypeStruct((n_str, tail, C_KV_DIM), F32)
    kv_spec = pl.BlockSpec((n_seq, tail, C_KV_DIM), lambda b, j, sk: (b, 0, 0))
    grid_spec = pltpu.PrefetchScalarGridSpec(
        num_scalar_prefetch=1, grid=(n_batch, tiles_per_batch), in_specs=in_specs,
        out_specs=[pl.BlockSpec((rows, D_MODEL), lambda b, j, sk: (b * tiles_per_batch + j, 0)),
                   kv_spec, kv_spec],
        scratch_shapes=[pltpu.VMEM((n_seq, WINDOW + seq_rows, C_KV_DIM), F32),
                        pltpu.VMEM((n_seq, WINDOW + seq_rows, C_KV_DIM), F32),
                        pltpu.VMEM((rows, D_MODEL), F32), pltpu.VMEM((rows, D_MODEL), F32)])
    return pl.pallas_call(
        functools.partial(_swa_body, n_seq=n_seq, seq_rows=seq_rows, past_valid=past_valid),
        grid_spec=grid_spec,
        out_shape=[jax.ShapeDtypeStruct(x.shape, F32), kv_out, kv_out],
        compiler_params=_cparams(2),
        name="swa_s" if past_valid else "swa_p")(sinks, *args)


def _router_body(xp_ref, xs_ref, nrm_ref, rhi_ref, rlo_ref, lstrict_ref, su_ref, info_ref, cnt_ref, *,
                 n_prompt_tiles):
    def run(x_ref):
        h = _rms(x_ref[...], nrm_ref[...])
        hi = h.astype(BF16)
        lo = (h - hi.astype(F32)).astype(BF16)
        logits = _dot(hi, rhi_ref[...]) + _dot(lo, rhi_ref[...]) + _dot(hi, rlo_ref[...])
        lane = lax.broadcasted_iota(jnp.int32, logits.shape, 1).astype(F32)
        logits = jnp.where(lane < N_EXPERTS, logits, -jnp.inf)
        l1 = jnp.max(logits, axis=-1, keepdims=True)
        i1 = jnp.min(jnp.where(logits == l1, lane, float(LANES)), axis=-1, keepdims=True)
        rest = jnp.where(lane == i1, -jnp.inf, logits)
        l2 = jnp.max(rest, axis=-1, keepdims=True)
        i2 = jnp.min(jnp.where(rest == l2, lane, float(LANES)), axis=-1, keepdims=True)
        e = jnp.exp(l2 - l1)
        g1 = 1.0 / (1.0 + e)
        g2 = e * g1
        oh1 = (lane == i1).astype(F32)
        oh2 = (lane == i2).astype(F32)
        cnt1 = jnp.sum(oh1, axis=0, keepdims=True)
        cnt2 = jnp.sum(oh2, axis=0, keepdims=True)
        seg = jnp.broadcast_to(jnp.ceil((cnt1 + cnt2) * (1.0 / GROUP)), (GROUP, LANES))
        off = _dot(seg.astype(BF16), su_ref[...])[0:1] * GROUP
        c1 = _dot(lstrict_ref[...], oh1.astype(BF16)) + off
        c2 = _dot(lstrict_ref[...], oh2.astype(BF16)) + off + cnt1
        d1 = jnp.sum(jnp.where(lane == i1, c1, 0.0), axis=-1, keepdims=True)
        d2 = jnp.sum(jnp.where(lane == i2, c2, 0.0), axis=-1, keepdims=True)
        info = jnp.zeros(logits.shape, F32)
        for n, col in enumerate([d1, d2, g1, g2]):
            info = jnp.where(lane == n, col, info)
        info_ref[...] = info
        cnt_ref[...] = seg * GROUP

    _on_row_source(pl.program_id(0), n_prompt_tiles, run, (xp_ref,), (xs_ref,))


def _router(xp, xs, nrm, rhi, rlo, lstrict, su):
    n_prompt_tiles = xp.shape[0] // ROW_TILE
    n_tiles = n_prompt_tiles + xs.shape[0] // ROW_TILE
    whole = lambda *shape: pl.BlockSpec(shape, lambda i: (0,) * len(shape))
    return pl.pallas_call(
        functools.partial(_router_body, n_prompt_tiles=n_prompt_tiles), grid=(n_tiles,),
        in_specs=[*_two_source_specs(n_prompt_tiles), whole(1, D_MODEL), whole(D_MODEL, LANES), whole(D_MODEL, LANES),
                  whole(ROW_TILE, ROW_TILE), whole(LANES, LANES)],
        out_specs=[pl.BlockSpec((ROW_TILE, LANES), lambda i: (i, 0)), pl.BlockSpec((GROUP, LANES), lambda i: (i, 0))],
        out_shape=[jax.ShapeDtypeStruct((n_tiles * ROW_TILE, LANES), F32),
                   jax.ShapeDtypeStruct((n_tiles * GROUP, LANES), F32)],
        compiler_params=_cparams(1), name="router")(xp, xs, nrm, rhi, rlo, lstrict, su)


BIG_ROWS = 8 * GROUP
SEG_FIELDS = 4
TILE_FIELDS = 2
TAIL_FIELDS = 3


def _rows_copy(src_ref, src_row, dst_ref, dst_row, n_rows, sem):
    return pltpu.make_async_copy(src_ref.at[pl.ds(pl.multiple_of(src_row, GROUP), n_rows)],
                                 dst_ref.at[pl.ds(pl.multiple_of(dst_row, GROUP), n_rows)], sem)


def _start_pieces(src_ref, src0, dst_ref, dst0, n_big, n_small, sem):
    def big(g, carry):
        _rows_copy(src_ref, src0 + g * BIG_ROWS, dst_ref, dst0 + g * BIG_ROWS, BIG_ROWS, sem).start()
        return carry

    lax.fori_loop(0, n_big, big, 0)
    done = n_big * BIG_ROWS

    def small(g, carry):
        _rows_copy(src_ref, src0 + done + g * GROUP, dst_ref, dst0 + done + g * GROUP, GROUP, sem).start()
        return carry

    lax.fori_loop(0, n_small, small, 0)


def _wait_pieces(n_big, n_small, src_ref, dst_ref, sem):
    def big(g, carry):
        _rows_copy(src_ref, 0, dst_ref, 0, BIG_ROWS, sem).wait()
        return carry

    lax.fori_loop(0, n_big, big, 0)

    def small(g, carry):
        _rows_copy(src_ref, 0, dst_ref, 0, GROUP, sem).wait()
        return carry

    lax.fori_loop(0, n_small, small, 0)


def _segment_table(seg, n_sorted_tiles):
    pieces = lambda rows: (rows // BIG_ROWS, (rows % BIG_ROWS) // GROUP)
    local_off = jnp.cumsum(seg, axis=1) - seg
    expert_rows = jnp.sum(seg, axis=0)
    expert_pad = ((expert_rows + ROW_TILE - 1) // ROW_TILE) * ROW_TILE
    expert_end = jnp.cumsum(expert_pad)
    expert_off = expert_end - expert_pad
    seg_start = expert_off[None, :] + jnp.cumsum(seg, axis=0) - seg
    n_used = (expert_end[-1] // ROW_TILE).reshape(1)
    tile_ids = jnp.arange(n_sorted_tiles, dtype=jnp.int32)
    tile_expert = jnp.minimum(jnp.sum(tile_ids[:, None] >= (expert_end // ROW_TILE)[None, :], axis=1), N_EXPERTS - 1)
    tile_expert = jnp.where(tile_ids < n_used[0], tile_expert, tile_expert[n_used[0] - 1])
    seg_big, seg_small = pieces(seg)
    tail_big, tail_small = pieces(expert_pad - expert_rows)
    table = jnp.concatenate([
        jnp.stack([seg_start, seg_big, seg_small, local_off], axis=-1).reshape(-1),
        jnp.stack([jnp.sum(seg_big, axis=1), jnp.sum(seg_small, axis=1)], axis=-1).reshape(-1),
        jnp.stack([expert_off + expert_rows, tail_big, tail_small], axis=-1).reshape(-1),
        n_used])
    return table.astype(jnp.int32), tile_expert.astype(jnp.int32), n_used.astype(jnp.int32)


def _table_sections(n_tiles):
    tiles0 = SEG_FIELDS * n_tiles * N_EXPERTS
    tails0 = tiles0 + TILE_FIELDS * n_tiles
    return tiles0, tails0, tails0 + TAIL_FIELDS * N_EXPERTS


def _dispatch_body(tab_ref, info_ref, xp_ref, xs_ref, nrm_ref, sel_ref, out_ref, loc_scr, z_scr, sems, *,
                   n_prompt_tiles, n_tiles, n_sorted_tiles):
    i = pl.program_id(0)
    slot = i % 2
    tiles0, tails0, used0 = _table_sections(n_tiles)

    def wait_tile(t, s):
        _wait_pieces(tab_ref[tiles0 + TILE_FIELDS * t], tab_ref[tiles0 + TILE_FIELDS * t + 1], loc_scr.at[s], out_ref,
                     sems.at[s])

    @pl.when(i >= 2)
    def _():
        wait_tile(i - 2, slot)

    def run(x_ref):
        hb = _rms(x_ref[...], nrm_ref[...]).astype(BF16)
        info = info_ref[...]
        lane = lax.broadcasted_iota(jnp.int32, info.shape, 1)
        dest = jnp.where(lane < TOP_K, info, 0.0)
        drow = sum(_dot_nt(sel_ref[...], part) for part in _split3(dest))
        r = lax.broadcasted_iota(jnp.int32, (LOCAL_ROWS, ROW_TILE), 0).astype(F32)
        perm = jnp.where(r == drow[0:1], 1.0, jnp.where(r == drow[1:2], 1.0, 0.0)).astype(BF16)
        loc_scr[slot] = _dot(perm, hb)

    _on_row_source(i, n_prompt_tiles, run, (xp_ref,), (xs_ref,))

    for e in range(N_EXPERTS):
        base = SEG_FIELDS * (i * N_EXPERTS + e)
        _start_pieces(loc_scr.at[slot], tab_ref[base + 3], out_ref, tab_ref[base], tab_ref[base + 1],
                      tab_ref[base + 2], sems.at[slot])

    @pl.when(i == n_tiles - 1)
    def _():
        wait_tile(i, slot)
        if n_tiles > 1:
            wait_tile(i - 1, 1 - slot)
        z_scr[...] = jnp.zeros_like(z_scr)
        for e in range(N_EXPERTS):
            base = tails0 + TAIL_FIELDS * e
            _start_pieces(z_scr, 0, out_ref, tab_ref[base], tab_ref[base + 1], tab_ref[base + 2], sems.at[2])
            _wait_pieces(tab_ref[base + 1], tab_ref[base + 2], z_scr, out_ref, sems.at[2])

        def zero_tile(t, carry):
            cp = _rows_copy(z_scr, 0, out_ref, t * ROW_TILE, ROW_TILE, sems.at[2])
            cp.start()
            cp.wait()
            return carry

        lax.fori_loop(tab_ref[used0], n_sorted_tiles, zero_tile, 0)


def _dispatch(table, info, xp, xs, nrm, sel, n_sorted_tiles):
    n_prompt_tiles = xp.shape[0] // ROW_TILE
    n_tiles = n_prompt_tiles + xs.shape[0] // ROW_TILE
    whole = lambda *shape: pl.BlockSpec(shape, lambda i, tab: (0,) * len(shape))
    grid_spec = pltpu.PrefetchScalarGridSpec(
        num_scalar_prefetch=1, grid=(n_tiles,),
        in_specs=[pl.BlockSpec((ROW_TILE, LANES), lambda i, tab: (i, 0)), *_two_source_specs(n_prompt_tiles),
                  whole(1, D_MODEL), whole(GROUP, LANES)],
        out_specs=pl.BlockSpec(memory_space=pl.ANY),
        scratch_shapes=[pltpu.VMEM((2, LOCAL_ROWS, D_MODEL), F32), pltpu.VMEM((ROW_TILE, D_MODEL), F32),
                        pltpu.SemaphoreType.DMA((3,))])
    return pl.pallas_call(
        functools.partial(_dispatch_body, n_prompt_tiles=n_prompt_tiles, n_tiles=n_tiles,
                          n_sorted_tiles=n_sorted_tiles),
        grid_spec=grid_spec, out_shape=jax.ShapeDtypeStruct((n_sorted_tiles * ROW_TILE, D_MODEL), F32),
        compiler_params=_cparams(1), name="moe_dispatch")(table, info, xp, xs, nrm, sel)


def _combine_body(tab_ref, info_ref, xp_ref, xs_ref, e_ref, yp_ref, ys_ref, loc_scr, sems, *, n_prompt_tiles,
                  n_tiles):
    i = pl.program_id(0)
    slot = i % 2
    tiles0, _, _ = _table_sections(n_tiles)

    def fetch(t, s):
        for e in range(N_EXPERTS):
            base = SEG_FIELDS * (t * N_EXPERTS + e)
            _start_pieces(e_ref, tab_ref[base], loc_scr.at[s], tab_ref[base + 3], tab_ref[base + 1],
                          tab_ref[base + 2], sems.at[s])

    @pl.when(i == 0)
    def _():
        loc_scr[...] = jnp.zeros_like(loc_scr)
        fetch(0, 0)

    @pl.when(i + 1 < n_tiles)
    def _():
        fetch(i + 1, 1 - slot)

    _wait_pieces(tab_ref[tiles0 + TILE_FIELDS * i], tab_ref[tiles0 + TILE_FIELDS * i + 1], e_ref, loc_scr.at[slot],
                 sems.at[slot])

    def run(x_ref, y_ref):
        info = info_ref[...]
        eb = loc_scr[slot].astype(BF16)
        lane = lax.broadcasted_iota(jnp.int32, (ROW_TILE, LOCAL_ROWS), 1).astype(F32)
        pick = jnp.zeros((ROW_TILE, LOCAL_ROWS), F32)
        for k in range(TOP_K):
            pick = jnp.where(lane == info[:, k:k + 1], info[:, TOP_K + k:TOP_K + k + 1], pick)
        y_ref[...] = x_ref[...] + _dot(pick.astype(BF16), eb)

    _on_row_source(i, n_prompt_tiles, run, (xp_ref, yp_ref), (xs_ref, ys_ref))


def _combine(table, info, xp, xs, esorted):
    n_prompt_tiles = xp.shape[0] // ROW_TILE
    n_tiles = n_prompt_tiles + xs.shape[0] // ROW_TILE
    grid_spec = pltpu.PrefetchScalarGridSpec(
        num_scalar_prefetch=1, grid=(n_tiles,),
        in_specs=[pl.BlockSpec((ROW_TILE, LANES), lambda i, tab: (i, 0)), *_two_source_specs(n_prompt_tiles),
                  pl.BlockSpec(memory_space=pl.ANY)],
        out_specs=list(_two_source_specs(n_prompt_tiles)),
        scratch_shapes=[pltpu.VMEM((2, LOCAL_ROWS, D_MODEL), F32), pltpu.SemaphoreType.DMA((2,))])
    return pl.pallas_call(
        functools.partial(_combine_body, n_prompt_tiles=n_prompt_tiles, n_tiles=n_tiles),
        grid_spec=grid_spec,
        out_shape=[jax.ShapeDtypeStruct(xp.shape, F32), jax.ShapeDtypeStruct(xs.shape, F32)],
        compiler_params=_cparams(1), name="moe_combine")(table, info, xp, xs, esorted)


def _rope_tables(pos):
    half = C_HDIM // 2
    inv = 1.0 / (ROPE_THETA ** (jnp.arange(half, dtype=F32) / half))
    ang = pos.astype(F32)[:, None] * inv[None, :]
    cos = jnp.cos(ang)
    sin = jnp.sin(ang)
    reps = LANES // C_HDIM
    return jnp.tile(jnp.concatenate([cos, cos], axis=-1), (1, reps)), jnp.tile(jnp.concatenate([-sin, sin], axis=-1),
                                                                                 (1, reps))


def kernel(x_prompt, x_sample, mem_prompt, cache_mem_k, cache_mem_v, state_hgrn, cache_swa_k, cache_swa_v, norm_mix, norm_xattn, norm_ffn, even_w_in, even_w_out, gmlp_w_s, gmlp_b_s, gmlp_ln_g, gmlp_ln_b, hgrn_lb_logits, hgrn_out_norm, attn_w_in, attn_w_out, attn_q_norm, attn_k_norm, attn_sinks, xattn_mem_norm, xattn_w_q, xattn_w_k, xattn_w_v, xattn_w_o, xattn_q_norm, xattn_k_norm, ffn_w1, ffn_w3, ffn_w2, moe_router, moe_w1, moe_w3, moe_w2):
    n_batch, seq, d = x_prompt.shape
    dec_batch, dec_seq, _ = x_sample.shape
    n_mem = mem_prompt.shape[1]
    depth = norm_mix.shape[0]
    past_len = PAST_LEN
    assert d == D_MODEL and depth == 2 and seq % ROW_TILE == 0 and dec_batch * dec_seq == ROW_TILE
    assert dec_seq == CHUNK and ROW_TILE % n_mem == 0 and cache_swa_k.shape[2] == WINDOW
    assert even_w_in.shape[-1] == EVEN_IN and attn_w_in.shape[-1] == ODD_IN and ffn_w1.shape[-1] == D_FF
    assert moe_w1.shape[1] == N_EXPERTS
    n_prompt_rows = n_batch * seq
    n_rows = n_prompt_rows + dec_batch * dec_seq
    n_prompt_tiles = n_prompt_rows // ROW_TILE
    tiles_per_batch = seq // ROW_TILE
    bf = lambda w: w.astype(BF16)
    row = lambda g: g.reshape(1, -1).astype(F32)

    lb_all = jnp.cumsum(jax.nn.softmax(hgrn_lb_logits.astype(F32), axis=0), axis=0)
    wcum = jnp.asarray(_cumsum_matrix(), BF16)
    lmask = jnp.asarray(_level_masks(), F32)

    def even_consts(n):
        tril = jnp.tril(jnp.ones((n, n), bool))
        ws = jnp.where(tril[None], gmlp_w_s[0, :, :n, :n], 0.0).astype(BF16)
        bs = jnp.broadcast_to(gmlp_b_s[0, :, :n, None], (A_GROUPS, n, LANES)).astype(F32)
        return [row(norm_mix[0]), bf(even_w_in[0]), bf(even_w_out[0]), ws, bs,
                gmlp_ln_g[0].reshape(A_GROUPS, 1, A_GDIM), gmlp_ln_b[0].reshape(A_GROUPS, 1, A_GDIM),
                row(lb_all[0]), row(hgrn_out_norm[0]), wcum, lmask]

    xp, hgrn_p = _even_mixer(
        x_prompt.reshape(n_prompt_rows, d), jnp.zeros((n_batch, B_HEADS, B_HDIM, B_HDIM), F32),
        even_consts(A_CHUNK), n_batch=n_batch, tiles_per_batch=tiles_per_batch, n_seq=1, seq_rows=ROW_TILE,
        gchunk=A_CHUNK, emit_v=False)
    xs, hgrn_s, gmlp_v = _even_mixer(
        x_sample.reshape(ROW_TILE, d), state_hgrn[0], even_consts(min(A_CHUNK, dec_seq)), n_batch=1,
        tiles_per_batch=1, n_seq=dec_batch, seq_rows=dec_seq, gchunk=min(A_CHUNK, dec_seq), emit_v=True)

    mem_k, mem_v = _memory_kv(mem_prompt.reshape(n_batch * n_mem, d), xattn_mem_norm.reshape(depth, 1, d),
                              bf(xattn_w_k), bf(xattn_w_v), xattn_k_norm.reshape(depth, 1, X_HDIM),
                              n_batch=n_batch, n_mem=n_mem)

    def cross_attention(xp, xs, l):
        consts = (row(norm_xattn[l]), bf(xattn_w_q[l]), bf(xattn_w_o[l]), row(xattn_q_norm[l]))
        xp = _xattn(xp, mem_k[l], mem_v[l], *consts, steps_per_mem=tiles_per_batch, n_seq=1, seq_rows=ROW_TILE)
        xs = _xattn(xs, cache_mem_k[l].reshape(dec_batch, n_mem, X_DIM),
                    cache_mem_v[l].reshape(dec_batch, n_mem, X_DIM), *consts, steps_per_mem=1, n_seq=dec_batch,
                    seq_rows=dec_seq)
        return xp, xs

    xp, xs = cross_attention(xp, xs, 0)
    xp, xs = _dense_ffn(xp, xs, row(norm_ffn[0]), bf(ffn_w1[0]), bf(ffn_w3[0]), bf(ffn_w2[0]))

    reps = D_MODEL // C_HDIM
    bd = jnp.asarray(np.kron(np.eye(reps, dtype=np.float32), np.full((C_HDIM, C_HDIM), 1.0 / C_HDIM, np.float32)),
                     BF16)
    swa_consts = [row(norm_mix[1]), bf(attn_w_in[0]), bf(attn_w_out[0]), row(jnp.tile(attn_q_norm[0], C_HEADS)),
                  row(jnp.tile(attn_k_norm[0], C_KV_HEADS)), bd]
    sinks = attn_sinks[0].astype(F32)
    cos_p, sin_p = _rope_tables(jnp.arange(seq, dtype=jnp.int32))
    cos_s, sin_s = _rope_tables(past_len + jnp.arange(dec_seq, dtype=jnp.int32))
    no_past = jnp.zeros((n_batch, WINDOW, C_KV_DIM), F32)
    xp, swk_p, swv_p = _swa(xp, no_past, no_past, cos_p, sin_p, sinks, swa_consts, n_batch=n_batch,
                            tiles_per_batch=tiles_per_batch, n_seq=1, seq_rows=ROW_TILE, past_valid=False)
    xs, swk_s, swv_s = _swa(xs, cache_swa_k[0].reshape(dec_batch, WINDOW, C_KV_DIM),
                            cache_swa_v[0].reshape(dec_batch, WINDOW, C_KV_DIM),
                            jnp.tile(cos_s, (dec_batch, 1)), jnp.tile(sin_s, (dec_batch, 1)), sinks, swa_consts,
                            n_batch=1, tiles_per_batch=1, n_seq=dec_batch, seq_rows=dec_seq, past_valid=True)
    xp, xs = cross_attention(xp, xs, 1)

    n_tiles = n_rows // ROW_TILE
    router_w = jnp.zeros((d, LANES), F32).at[:, :N_EXPERTS].set(moe_router[0].astype(F32))
    rhi = router_w.astype(BF16)
    rlo = (router_w - rhi.astype(F32)).astype(BF16)
    lstrict = jnp.asarray(np.tril(np.ones((ROW_TILE, ROW_TILE), np.float32), -1), BF16)
    su = jnp.asarray(np.triu(np.ones((LANES, LANES), np.float32), 1), BF16)
    sel = jnp.asarray(np.eye(GROUP, LANES, dtype=np.float32), BF16)
    nrm_ffn = row(norm_ffn[1])
    info, seg = _router(xp, xs, nrm_ffn, rhi, rlo, lstrict, su)
    seg = seg.reshape(n_tiles, GROUP, LANES)[:, 0, :N_EXPERTS].astype(jnp.int32)
    n_sorted_tiles = -(-(TOP_K * n_rows + n_tiles * N_EXPERTS * (GROUP - 1) + N_EXPERTS * (ROW_TILE - 1)) // ROW_TILE)
    table, tile_expert, n_used = _segment_table(seg, n_sorted_tiles)
    hsorted = _dispatch(table, info, xp, xs, nrm_ffn, sel, n_sorted_tiles)
    esorted = _grouped_ffn(hsorted, tile_expert, n_used, bf(moe_w1[0]), bf(moe_w3[0]), bf(moe_w2[0]))
    y_prompt, y_sample = _combine(table, info, xp, xs, esorted)

    n_even = state_hgrn.shape[0]
    n_odd = cache_swa_k.shape[0]
    return (y_prompt.reshape(n_batch, seq, d), y_sample.reshape(dec_batch, dec_seq, d),
            mem_k.reshape(depth, n_batch, n_mem, X_HEADS, X_HDIM), mem_v.reshape(depth, n_batch, n_mem, X_HEADS, X_HDIM),
            hgrn_p.reshape(n_even, n_batch, B_HEADS, B_HDIM, B_HDIM),
            gmlp_v.reshape(n_even, dec_batch, dec_seq, A_GROUPS, A_GDIM),
            hgrn_s.reshape(n_even, dec_batch, B_HEADS, B_HDIM, B_HDIM),
            swk_p.reshape(n_odd, n_batch, WINDOW, C_KV_HEADS, C_HDIM), swv_p.reshape(n_odd, n_batch, WINDOW, C_KV_HEADS, C_HDIM),
            swk_s.reshape(n_odd, dec_batch, dec_seq, C_KV_HEADS, C_HDIM), swv_s.reshape(n_odd, dec_batch, dec_seq, C_KV_HEADS, C_HDIM))
```

```python
import functools

import numpy as np
import jax
import jax.numpy as jnp
from jax import lax
from jax.experimental import pallas as pl
from jax.experimental.pallas import tpu as pltpu

F32 = jnp.float32
BF16 = jnp.bfloat16

D_MODEL = 1024
EPS = 1e-6
CHUNK = 64
A_GROUPS = 4
A_DIM = D_MODEL // 2
A_GDIM = A_DIM // A_GROUPS
A_CHUNK = 128
B_HEADS = 4
B_DIM = D_MODEL // 2
B_HDIM = B_DIM // B_HEADS
EVEN_IN = 2 * A_DIM + 4 * B_DIM
C_HEADS = 16
C_KV_HEADS = 4
C_HDIM = D_MODEL // C_HEADS
C_GROUP = C_HEADS // C_KV_HEADS
C_KV_DIM = C_KV_HEADS * C_HDIM
WINDOW = 128
ROPE_THETA = 10000.0
PAST_LEN = 4096
ODD_IN = (C_HEADS + 2 * C_KV_HEADS) * C_HDIM
X_HEADS = 4
X_HDIM = 128
X_DIM = X_HEADS * X_HDIM
D_FF = 2816
N_EXPERTS = 8
TOP_K = 2

LANES = 128
ROW_TILE = 512
FF_CHUNK = 2816
N_FF_CHUNKS = D_FF // FF_CHUNK
FF_SUB = 512
N_LEVELS = 6
HGRN_STAGE_CHUNKS = 2
SWA_STAGE_CHUNKS = 2
GROUP = 8
LOCAL_ROWS = -(-(TOP_K * ROW_TILE + N_EXPERTS * (GROUP - 1)) // LANES) * LANES
VMEM_LIMIT = 58 * 1024 * 1024


def _cparams(n_axes):
    return pltpu.CompilerParams(dimension_semantics=("arbitrary",) * n_axes, vmem_limit_bytes=VMEM_LIMIT)


def _dot(a, b):
    return jnp.dot(a, b, preferred_element_type=F32)


def _dot_nt(a, b):
    return lax.dot_general(a, b, (((1,), (1,)), ((), ())), preferred_element_type=F32)


def _rms(x, g):
    return x * lax.rsqrt(jnp.mean(x * x, axis=-1, keepdims=True) + EPS) * g


def _silu(x):
    return x * (1.0 / (1.0 + jnp.exp(-x)))


def _split3(x):
    hi = x.astype(BF16)
    r1 = x - hi.astype(F32)
    mid = r1.astype(BF16)
    lo = (r1 - mid.astype(F32)).astype(BF16)
    return hi, mid, lo


def _cumsum_matrix():
    r = np.arange(CHUNK)
    s = np.arange(CHUNK)
    blocks = [(s[None, :] <= r[:, None])]
    for l in range(N_LEVELS):
        h = 1 << l
        ref = (r & ~(2 * h - 1)) + h - 1
        blocks.append(s[None, :] <= ref[:, None])
    w = np.concatenate(blocks, axis=0).astype(np.float32)
    return np.concatenate([w, w, w], axis=1)


def _level_masks():
    t = np.arange(CHUNK)[:, None]
    s = np.arange(CHUNK)[None, :]
    masks = []
    for l in range(N_LEVELS):
        masks.append(((t >> (l + 1)) == (s >> (l + 1))) & (((t >> l) & 1) == 1) & (((s >> l) & 1) == 0))
    masks.append(t == s)
    return np.stack(masks).astype(np.float32)


def _even_mixer_body(x_ref, s0_ref, nrm_ref, win32_ref, wout32_ref, ws_ref, bs_ref, lng_ref, lnb_ref, lb_ref, og_ref,
                     wcum_ref, lmask_ref, *rest, n_seq, seq_rows, gchunk, emit_v):
    if emit_v:
        y_ref, sout_ref, v_ref, proj_scr, mixed_scr, st_scr, win_ref, wout_ref = rest
    else:
        y_ref, sout_ref, proj_scr, mixed_scr, st_scr, win_ref, wout_ref = rest
        v_ref = None
    rows = n_seq * seq_rows
    j = pl.program_id(1)

    @pl.when((pl.program_id(0) == 0) & (j == 0))
    def _():
        win_ref[...] = win32_ref[...].astype(BF16)
        wout_ref[...] = wout32_ref[...].astype(BF16)

    @pl.when(j == 0)
    def _():
        for s in range(n_seq):
            for hd in range(B_HEADS):
                st_scr[s * B_HEADS + hd] = s0_ref[s, hd].T

    x = x_ref[...]
    h = _rms(x, nrm_ref[...]).astype(BF16)
    n_pieces = EVEN_IN // ROW_TILE
    for n in range(n_pieces):
        cs = slice(n * ROW_TILE, (n + 1) * ROW_TILE)
        proj_scr[:, cs] = _dot(h, win_ref[:, cs])

    groups = range(A_GROUPS)
    group_cols = lambda base, g: slice(base + g * A_GDIM, base + (g + 1) * A_GDIM)
    vgs = [jax.nn.gelu(proj_scr[:, group_cols(A_DIM, g)]) for g in groups]
    means = [jnp.mean(vg, axis=-1, keepdims=True) for vg in vgs]
    vcs = [vg - mu for vg, mu in zip(vgs, means)]
    variances = [jnp.mean(vc * vc, axis=-1, keepdims=True) for vc in vcs]
    for g in groups:
        gs = group_cols(0, g)
        vn = vcs[g] * lax.rsqrt(variances[g] + EPS) * lng_ref[g] + lnb_ref[g]
        if v_ref is not None:
            v_ref[:, gs] = vn
        ug = jax.nn.gelu(proj_scr[:, gs])
        vb = vn.astype(BF16)
        for c in range(rows // gchunk):
            rs = slice(c * gchunk, (c + 1) * gchunk)
            sp = _dot(ws_ref[g], vb[rs]) + bs_ref[g]
            mixed_scr[rs, gs] = ug[rs] * sp

    q0, f0, i0, g0 = (2 * A_DIM + k * B_DIM for k in range(4))
    lb = lb_ref[...]
    fg = lb + (1.0 - lb) * jax.nn.sigmoid(proj_scr[:, f0:f0 + B_DIM])
    proj_scr[:, 0:B_DIM] = jnp.log(fg)
    proj_scr[:, B_DIM:2 * B_DIM] = 1.0 - fg
    proj_scr[:, q0:q0 + B_DIM] = _silu(proj_scr[:, q0:q0 + B_DIM])
    og = og_ref[...]
    chunks_per_seq = seq_rows // CHUNK

    heads = range(B_HEADS)
    head_cols = lambda base, hd: slice(base + hd * B_HDIM, base + (hd + 1) * B_HDIM)
    chunk_rows = lambda c: slice(c * CHUNK, (c + 1) * CHUNK)

    def chunk_local(c):
        rs = chunk_rows(c)
        hi, mid, lo = _split3(proj_scr[rs, 0:B_DIM])
        gg = _dot(wcum_ref[...], jnp.concatenate([hi, mid, lo], axis=0))
        vs = [proj_scr[rs, head_cols(i0, hd)] for hd in heads]
        vts = [v.T.astype(BF16) for v in vs]
        Gs = [gg[0:CHUNK, head_cols(0, hd)] for hd in heads]
        qs = [proj_scr[rs, head_cols(q0, hd)] for hd in heads]
        ks = [proj_scr[rs, head_cols(B_DIM, hd)] for hd in heads]
        operands = []
        for hd in heads:
            for l in range(N_LEVELS + 1):
                if l < N_LEVELS:
                    e = jnp.exp(-jnp.abs(Gs[hd] - gg[(l + 1) * CHUNK:(l + 2) * CHUNK, head_cols(0, hd)]))
                    operands.append(((qs[hd] * e).astype(BF16), (ks[hd] * e).astype(BF16)))
                else:
                    operands.append((qs[hd].astype(BF16), ks[hd].astype(BF16)))
        blocks = [_dot_nt(qe, ke) for qe, ke in operands]
        parts = []
        for hd in heads:
            att = jnp.zeros((CHUNK, CHUNK), F32)
            for l in range(N_LEVELS + 1):
                att = jnp.where(lmask_ref[l] > 0.5, blocks[hd * (N_LEVELS + 1) + l], att)
            g_end = Gs[hd][CHUNK - 1:CHUNK, :]
            kd = (ks[hd] * jnp.exp(g_end - Gs[hd])).astype(BF16)
            parts.append((_dot(att.astype(BF16), vs[hd].astype(BF16)), (qs[hd] * jnp.exp(Gs[hd])).astype(BF16),
                          jnp.exp(g_end), _dot(vts[hd], kd)))
        return parts

    def chunk_state(c, parts):
        rs = chunk_rows(c)
        sidx = (c // chunks_per_seq) * B_HEADS
        outs = []
        for hd in heads:
            o_local, q_decayed, decay, increment = parts[hd]
            st = st_scr[sidx + hd]
            outs.append(o_local + _dot_nt(q_decayed, st.astype(BF16)))
            st_scr[sidx + hd] = st * decay + increment
        for hd in heads:
            o = outs[hd]
            on = o * lax.rsqrt(jnp.mean(o * o, axis=-1, keepdims=True) + EPS) * og
            gate = _silu(proj_scr[rs, head_cols(g0, hd)])
            mixed_scr[rs, head_cols(A_DIM, hd)] = on * gate

    n_chunks = rows // CHUNK
    for first in range(0, n_chunks, HGRN_STAGE_CHUNKS):
        group = range(first, min(first + HGRN_STAGE_CHUNKS, n_chunks))
        local = [chunk_local(c) for c in group]
        for c, parts in zip(group, local):
            chunk_state(c, parts)

    y_ref[...] = x + _dot(mixed_scr[...].astype(BF16), wout_ref[...])

    @pl.when(j == pl.num_programs(1) - 1)
    def _():
        for s in range(n_seq):
            for hd in range(B_HEADS):
                sout_ref[s, hd] = st_scr[s * B_HEADS + hd].T


def _even_mixer(x, s0, consts, *, n_batch, tiles_per_batch, n_seq, seq_rows, gchunk, emit_v):
    rows = n_seq * seq_rows
    grid = (n_batch, tiles_per_batch)
    tile = lambda b, j: (b * tiles_per_batch + j, 0)
    whole = lambda *shape: pl.BlockSpec(shape, lambda b, j: (0,) * len(shape))
    once = lambda *shape: pl.BlockSpec(shape, lambda b, j: (0,) * len(shape), pipeline_mode=pl.Buffered(1))
    in_specs = [
        pl.BlockSpec((rows, D_MODEL), tile),
        pl.BlockSpec((n_seq, B_HEADS, B_HDIM, B_HDIM), lambda b, j: (b, 0, 0, 0)),
        whole(1, D_MODEL), once(D_MODEL, EVEN_IN), once(D_MODEL, D_MODEL),
        whole(A_GROUPS, gchunk, gchunk), whole(A_GROUPS, gchunk, LANES),
        whole(A_GROUPS, 1, A_GDIM), whole(A_GROUPS, 1, A_GDIM), whole(1, B_DIM), whole(1, B_HDIM),
        whole((N_LEVELS + 1) * CHUNK, 3 * CHUNK), whole(N_LEVELS + 1, CHUNK, CHUNK),
    ]
    args = [x, s0] + list(consts)
    out_shape = [jax.ShapeDtypeStruct(x.shape, F32),
                 jax.ShapeDtypeStruct((n_batch * n_seq, B_HEADS, B_HDIM, B_HDIM), F32)]
    out_specs = [pl.BlockSpec((rows, D_MODEL), tile),
                 pl.BlockSpec((n_seq, B_HEADS, B_HDIM, B_HDIM), lambda b, j: (b, 0, 0, 0))]
    if emit_v:
        out_shape.append(jax.ShapeDtypeStruct((x.shape[0], A_DIM), F32))
        out_specs.append(pl.BlockSpec((rows, A_DIM), tile))
    body = functools.partial(_even_mixer_body, n_seq=n_seq, seq_rows=seq_rows, gchunk=gchunk, emit_v=emit_v)
    return pl.pallas_call(
        body, grid=grid, in_specs=in_specs, out_specs=out_specs, out_shape=out_shape,
        scratch_shapes=[pltpu.VMEM((rows, EVEN_IN), F32), pltpu.VMEM((rows, D_MODEL), F32),
                        pltpu.VMEM((n_seq * B_HEADS, B_HDIM, B_HDIM), F32),
                        pltpu.VMEM((D_MODEL, EVEN_IN), BF16), pltpu.VMEM((D_MODEL, D_MODEL), BF16)],
        compiler_params=_cparams(2),
        name="even_mixer_s" if emit_v else "even_mixer_p")(*args)


def _memory_kv_body(mem_ref, gm_ref, wk_ref, wv_ref, kg_ref, mk_ref, mv_ref, *, n_seq, n_mem):
    m = _rms(mem_ref[...], gm_ref[0]).astype(BF16)
    kk = _dot(m, wk_ref[0].astype(BF16))
    vv = _dot(m, wv_ref[0].astype(BF16))
    kg = kg_ref[0]
    for hd in range(X_HEADS):
        hs = slice(hd * X_HDIM, (hd + 1) * X_HDIM)
        kh = _rms(kk[:, hs], kg)
        for s in range(n_seq):
            mk_ref[0, s, :, hs] = kh[s * n_mem:(s + 1) * n_mem]
    for s in range(n_seq):
        mv_ref[0, s] = vv[s * n_mem:(s + 1) * n_mem]


def _memory_kv(mem2d, g_mem, w_k, w_v, k_g, *, n_batch, n_mem):
    depth = w_k.shape[0]
    n_seq = ROW_TILE // n_mem
    out = jax.ShapeDtypeStruct((depth, n_batch, n_mem, X_DIM), F32)
    ospec = pl.BlockSpec((1, n_seq, n_mem, X_DIM), lambda l, t: (l, t, 0, 0))
    return pl.pallas_call(
        functools.partial(_memory_kv_body, n_seq=n_seq, n_mem=n_mem),
        grid=(depth, n_batch // n_seq),
        in_specs=[pl.BlockSpec((ROW_TILE, D_MODEL), lambda l, t: (t, 0)),
                  pl.BlockSpec((1, 1, D_MODEL), lambda l, t: (l, 0, 0)),
                  pl.BlockSpec((1, D_MODEL, X_DIM), lambda l, t: (l, 0, 0)),
                  pl.BlockSpec((1, D_MODEL, X_DIM), lambda l, t: (l, 0, 0)),
                  pl.BlockSpec((1, 1, X_HDIM), lambda l, t: (l, 0, 0))],
        out_specs=[ospec, ospec], out_shape=[out, out], compiler_params=_cparams(2),
        name="memory_kv")(mem2d, g_mem, w_k, w_v, k_g)


def _xattn_body(x_ref, mk_ref, mv_ref, nrm_ref, wq32_ref, wo32_ref, qg_ref, y_ref, o_scr, wq_ref, wo_ref, *, n_seq,
                seq_rows):
    @pl.when(pl.program_id(0) == 0)
    def _():
        wq_ref[...] = wq32_ref[...].astype(BF16)
        wo_ref[...] = wo32_ref[...].astype(BF16)

    x = x_ref[...]
    h = _rms(x, nrm_ref[...]).astype(BF16)
    q = _dot(h, wq_ref[...])
    qg = qg_ref[...] * (X_HDIM ** -0.5)
    units = [(s, hd) for s in range(n_seq) for hd in range(X_HEADS)]
    rows_of = lambda s: slice(s * seq_rows, (s + 1) * seq_rows)
    lanes_of = lambda hd: slice(hd * X_HDIM, (hd + 1) * X_HDIM)
    qh = [_rms(q[rows_of(s), lanes_of(hd)], qg).astype(BF16) for s, hd in units]
    sc = [_dot_nt(qh[n], mk_ref[s, :, lanes_of(hd)].astype(BF16)) for n, (s, hd) in enumerate(units)]
    m = [jnp.max(t, axis=-1, keepdims=True) for t in sc]
    p = [jnp.exp(t - mx) for t, mx in zip(sc, m)]
    den = [jnp.sum(t, axis=-1, keepdims=True) for t in p]
    for n, (s, hd) in enumerate(units):
        o_scr[rows_of(s), lanes_of(hd)] = _dot(p[n].astype(BF16), mv_ref[s, :, lanes_of(hd)].astype(BF16)) / den[n]
    y_ref[...] = x + _dot(o_scr[...].astype(BF16), wo_ref[...])


def _xattn(x, mk, mv, nrm, wq, wo, qg, *, steps_per_mem, n_seq, seq_rows):
    rows = n_seq * seq_rows
    n_mem = mk.shape[1]
    whole = lambda *shape: pl.BlockSpec(shape, lambda t: (0,) * len(shape))
    once = lambda *shape: pl.BlockSpec(shape, lambda t: (0,) * len(shape), pipeline_mode=pl.Buffered(1))
    return pl.pallas_call(
        functools.partial(_xattn_body, n_seq=n_seq, seq_rows=seq_rows),
        grid=(x.shape[0] // rows,),
        in_specs=[pl.BlockSpec((rows, D_MODEL), lambda t: (t, 0)),
                  pl.BlockSpec((n_seq, n_mem, X_DIM), lambda t: (t // steps_per_mem, 0, 0)),
                  pl.BlockSpec((n_seq, n_mem, X_DIM), lambda t: (t // steps_per_mem, 0, 0)),
                  whole(1, D_MODEL), once(D_MODEL, X_DIM), once(X_DIM, D_MODEL), whole(1, X_HDIM)],
        out_specs=pl.BlockSpec((rows, D_MODEL), lambda t: (t, 0)),
        out_shape=jax.ShapeDtypeStruct(x.shape, F32),
        scratch_shapes=[pltpu.VMEM((rows, X_DIM), F32), pltpu.VMEM((D_MODEL, X_DIM), BF16),
                        pltpu.VMEM((X_DIM, D_MODEL), BF16)],
        compiler_params=_cparams(1), name="xattn_s" if n_seq > 1 else "xattn_p")(x, mk, mv, nrm, wq, wo, qg)


def _ff_chunk(i, j):
    return jnp.where(i % 2 == 0, j, N_FF_CHUNKS - 1 - j)


def _two_source_specs(n_prompt_tiles, width=D_MODEL):
    return (pl.BlockSpec((ROW_TILE, width), lambda i, *_: (jnp.minimum(i, n_prompt_tiles - 1), 0)),
            pl.BlockSpec((ROW_TILE, width), lambda i, *_: (jnp.maximum(i - n_prompt_tiles, 0), 0)))


def _on_row_source(i, n_prompt_tiles, fn, prompt_refs, sample_refs):
    pl.when(i < n_prompt_tiles)(lambda: fn(*prompt_refs))
    pl.when(i >= n_prompt_tiles)(lambda: fn(*sample_refs))


def _swiglu_part(hb, w1, w3, w2):
    part = None
    for c0 in range(0, FF_CHUNK, FF_SUB):
        cs = slice(c0, min(c0 + FF_SUB, FF_CHUNK))
        act = (_silu(_dot(hb, w1[:, cs])) * _dot(hb, w3[:, cs])).astype(BF16)
        p = _dot(act, w2[cs, :])
        part = p if part is None else part + p
    return part


W_STAGE = 256


def _load_weights_as_bf16(w1_hbm, w3_hbm, w2_hbm, w1_scr, w3_scr, w2_scr, col_stage, row_stage, sems):
    n_pieces = D_FF // W_STAGE
    jobs = ([(w_hbm, w_scr, True, c) for w_hbm, w_scr in ((w1_hbm, w1_scr), (w3_hbm, w3_scr))
             for c in range(n_pieces)] + [(w2_hbm, w2_scr, False, c) for c in range(n_pieces)])

    def piece(n):
        w_hbm, w_scr, by_cols, c = jobs[n]
        span = pl.ds(c * W_STAGE, W_STAGE)
        if by_cols:
            return w_hbm.at[:, span], col_stage.at[n % 2], (slice(None), span), w_scr
        return w_hbm.at[span, :], row_stage.at[n % 2], (span, slice(None)), w_scr

    def copy(n):
        src, stage, _, _ = piece(n)
        return pltpu.make_async_copy(src, stage, sems.at[n % 2])

    copy(0).start()
    for n in range(len(jobs)):
        if n + 1 < len(jobs):
            copy(n + 1).start()
        copy(n).wait()
        _, stage, where, w_scr = piece(n)
        w_scr[where] = stage[...].astype(BF16)


def _dense_ffn_body(xp_ref, xs_ref, nrm_ref, w1_hbm, w3_hbm, w2_hbm, yp_ref, ys_ref, w1_scr, w3_scr, w2_scr,
                    col_stage, row_stage, sems, *, n_prompt_tiles):
    i = pl.program_id(0)

    @pl.when(i == 0)
    def _():
        _load_weights_as_bf16(w1_hbm, w3_hbm, w2_hbm, w1_scr, w3_scr, w2_scr, col_stage, row_stage, sems)

    def run(x_ref, y_ref):
        x = x_ref[...]
        y_ref[...] = x + _swiglu_part(_rms(x, nrm_ref[...]).astype(BF16), w1_scr, w3_scr, w2_scr)

    _on_row_source(i, n_prompt_tiles, run, (xp_ref, yp_ref), (xs_ref, ys_ref))


def _dense_ffn(xp, xs, nrm, w1, w3, w2):
    n_prompt_tiles = xp.shape[0] // ROW_TILE
    n_tiles = n_prompt_tiles + xs.shape[0] // ROW_TILE
    assert FF_CHUNK == D_FF
    in_hbm = pl.BlockSpec(memory_space=pl.ANY)
    return pl.pallas_call(
        functools.partial(_dense_ffn_body, n_prompt_tiles=n_prompt_tiles), grid=(n_tiles,),
        in_specs=[*_two_source_specs(n_prompt_tiles), pl.BlockSpec((1, D_MODEL), lambda i: (0, 0)),
                  in_hbm, in_hbm, in_hbm],
        out_specs=list(_two_source_specs(n_prompt_tiles)),
        out_shape=[jax.ShapeDtypeStruct(xp.shape, F32), jax.ShapeDtypeStruct(xs.shape, F32)],
        scratch_shapes=[pltpu.VMEM((D_MODEL, D_FF), BF16), pltpu.VMEM((D_MODEL, D_FF), BF16),
                        pltpu.VMEM((D_FF, D_MODEL), BF16), pltpu.VMEM((2, D_MODEL, W_STAGE), F32),
                        pltpu.VMEM((2, W_STAGE, D_MODEL), F32), pltpu.SemaphoreType.DMA((2,))],
        compiler_params=_cparams(1), name="dense_ffn")(xp, xs, nrm, w1, w3, w2)


def _grouped_ffn_body(te_ref, nu_ref, x_ref, w1_ref, w3_ref, w2_ref, y_ref):
    i = pl.program_id(0)
    j = pl.program_id(1)

    @pl.when(i < nu_ref[0])
    def _():
        part = _swiglu_part(x_ref[...].astype(BF16), w1_ref.at[0], w3_ref.at[0], w2_ref.at[0])

        @pl.when(j == 0)
        def _():
            y_ref[...] = part

        @pl.when(j > 0)
        def _():
            y_ref[...] += part

    @pl.when(i >= nu_ref[0])
    def _():
        y_ref[...] = jnp.zeros_like(y_ref)


def _grouped_ffn(xsorted, tile_expert, n_used, w1, w3, w2):
    n_tiles = xsorted.shape[0] // ROW_TILE

    def row_tile(i, j, te, nu):
        return (i, 0)

    def ff(i, j, nu):
        last = nu[0] - 1
        return jnp.where(i <= last, _ff_chunk(i, j), _ff_chunk(last, N_FF_CHUNKS - 1))

    grid_spec = pltpu.PrefetchScalarGridSpec(
        num_scalar_prefetch=2, grid=(n_tiles, N_FF_CHUNKS),
        in_specs=[pl.BlockSpec((ROW_TILE, D_MODEL), row_tile),
                  pl.BlockSpec((1, D_MODEL, FF_CHUNK), lambda i, j, te, nu: (te[i], 0, ff(i, j, nu))),
                  pl.BlockSpec((1, D_MODEL, FF_CHUNK), lambda i, j, te, nu: (te[i], 0, ff(i, j, nu))),
                  pl.BlockSpec((1, FF_CHUNK, D_MODEL), lambda i, j, te, nu: (te[i], ff(i, j, nu), 0))],
        out_specs=pl.BlockSpec((ROW_TILE, D_MODEL), row_tile))
    return pl.pallas_call(
        _grouped_ffn_body, grid_spec=grid_spec, out_shape=jax.ShapeDtypeStruct(xsorted.shape, F32),
        compiler_params=_cparams(2), name="grouped_ffn")(tile_expert, n_used, xsorted, w1, w3, w2)


def _rope_slab(xs, cos, sin_signed, first_half):
    rot = jnp.where(first_half, pltpu.roll(xs, LANES - C_HDIM // 2, 1), pltpu.roll(xs, C_HDIM // 2, 1))
    return xs * cos + rot * sin_signed


def _swa_body(sink_ref, x_ref, pk_ref, pv_ref, cos_ref, sin_ref, nrm_ref, win32_ref, wout32_ref, qg_ref, kg_ref,
              bd_ref, y_ref, ko_ref, vo_ref, k_scr, v_scr, q_scr, a_scr, win_ref, wout_ref, *, n_seq, seq_rows,
              past_valid):
    rows = n_seq * seq_rows
    j = pl.program_id(1)
    tail = min(WINDOW, seq_rows)

    @pl.when((pl.program_id(0) == 0) & (j == 0))
    def _():
        win_ref[...] = win32_ref[...].astype(BF16)
        wout_ref[...] = wout32_ref[...].astype(BF16)

    if past_valid:
        for s in range(n_seq):
            k_scr[s, 0:WINDOW] = pk_ref[s]
            v_scr[s, 0:WINDOW] = pv_ref[s]
    else:
        @pl.when(j == 0)
        def _():
            for s in range(n_seq):
                k_scr[s, 0:WINDOW] = jnp.zeros((WINDOW, C_KV_DIM), F32)
                v_scr[s, 0:WINDOW] = jnp.zeros((WINDOW, C_KV_DIM), F32)

        @pl.when(j > 0)
        def _():
            for s in range(n_seq):
                k_scr[s, 0:WINDOW] = k_scr[s, seq_rows:seq_rows + WINDOW]
                v_scr[s, 0:WINDOW] = v_scr[s, seq_rows:seq_rows + WINDOW]

    x = x_ref[...]
    h = _rms(x, nrm_ref[...]).astype(BF16)
    q_dim = C_HEADS * C_HDIM
    q = _dot(h, win_ref[:, 0:q_dim])
    k = _dot(h, win_ref[:, q_dim:q_dim + C_KV_DIM])
    v = _dot(h, win_ref[:, q_dim + C_KV_DIM:q_dim + 2 * C_KV_DIM])

    cos = cos_ref[...]
    sin_signed = sin_ref[...]
    first_half = (lax.broadcasted_iota(jnp.int32, (rows, LANES), 1) % C_HDIM) < (C_HDIM // 2)
    bd = bd_ref[...]
    qn = q * lax.rsqrt(_dot((q * q).astype(BF16), bd) + EPS) * qg_ref[...]
    kn = k * lax.rsqrt(_dot((k * k).astype(BF16), bd[0:C_KV_DIM, 0:C_KV_DIM]) + EPS) * kg_ref[...]
    scale = C_HDIM ** -0.5
    for sl in range(q_dim // LANES):
        ls = slice(sl * LANES, (sl + 1) * LANES)
        q_scr[:, ls] = _rope_slab(qn[:, ls], cos, sin_signed, first_half) * scale
    for sl in range(C_KV_DIM // LANES):
        ls = slice(sl * LANES, (sl + 1) * LANES)
        kr = _rope_slab(kn[:, ls], cos, sin_signed, first_half)
        for s in range(n_seq):
            k_scr[s, WINDOW:WINDOW + seq_rows, ls] = kr[s * seq_rows:(s + 1) * seq_rows]
    for s in range(n_seq):
        v_scr[s, WINDOW:WINDOW + seq_rows] = v[s * seq_rows:(s + 1) * seq_rows]
        ko_ref[s] = k_scr[s, WINDOW + seq_rows - tail:WINDOW + seq_rows]
        vo_ref[s] = v_scr[s, WINDOW + seq_rows - tail:WINDOW + seq_rows]

    chunks_per_seq = seq_rows // CHUNK
    n_keys = WINDOW + CHUNK
    key_idx = lax.broadcasted_iota(jnp.int32, (C_GROUP * CHUNK, n_keys), 1)
    row_grp = lax.broadcasted_iota(jnp.int32, (C_GROUP * CHUNK, 1), 0) // CHUNK
    ones_keys = jnp.ones((n_keys, C_HDIM), BF16)

    head_sinks = []
    for kvh in range(C_KV_HEADS):
        sink = jnp.zeros((C_GROUP * CHUNK, 1), F32)
        for g in range(C_GROUP):
            sink = jnp.where(row_grp == g, sink_ref[kvh * C_GROUP + g], sink)
        head_sinks.append(sink)

    def attend(chunk_ids):
        units = [(idx, kvh) for idx in chunk_ids for kvh in range(C_KV_HEADS)]
        head_lanes = lambda kvh: slice(kvh * C_HDIM, (kvh + 1) * C_HDIM)
        q_rows = lambda idx: slice(idx * CHUNK, (idx + 1) * CHUNK)
        key_rows = lambda idx: slice((idx % chunks_per_seq) * CHUNK, (idx % chunks_per_seq) * CHUNK + n_keys)
        qs = [jnp.concatenate(
            [q_scr[q_rows(idx), (kvh * C_GROUP + g) * C_HDIM:(kvh * C_GROUP + g + 1) * C_HDIM]
             for g in range(C_GROUP)], axis=0).astype(BF16) for idx, kvh in units]
        kh = [k_scr[idx // chunks_per_seq, key_rows(idx), head_lanes(kvh)].astype(BF16) for idx, kvh in units]
        vh = [v_scr[idx // chunks_per_seq, key_rows(idx), head_lanes(kvh)].astype(BF16) for idx, kvh in units]
        sc = [_dot_nt(qn, kn) for qn, kn in zip(qs, kh)]
        if not past_valid:
            sc = [jnp.where((key_idx + (idx % chunks_per_seq) * CHUNK >= WINDOW) | (j > 0), sn, -jnp.inf)
                  for sn, (idx, kvh) in zip(sc, units)]
        m = [jnp.maximum(jnp.max(sn, axis=-1, keepdims=True), head_sinks[kvh]) for sn, (idx, kvh) in zip(sc, units)]
        p = [jnp.exp(sn - mn).astype(BF16) for sn, mn in zip(sc, m)]
        den = [_dot(pn, ones_keys) + jnp.exp(head_sinks[kvh] - mn) for pn, mn, (idx, kvh) in zip(p, m, units)]
        o = [_dot(pn, vn) / dn for pn, vn, dn in zip(p, vh, den)]
        for on, (idx, kvh) in zip(o, units):
            for g in range(C_GROUP):
                hs = slice((kvh * C_GROUP + g) * C_HDIM, (kvh * C_GROUP + g + 1) * C_HDIM)
                a_scr[q_rows(idx), hs] = on[g * CHUNK:(g + 1) * CHUNK]

    n_chunks = rows // CHUNK
    for first in range(0, n_chunks, SWA_STAGE_CHUNKS):
        attend(range(first, min(first + SWA_STAGE_CHUNKS, n_chunks)))
    y_ref[...] = x + _dot(a_scr[...].astype(BF16), wout_ref[...])


def _swa(x, pk, pv, cos, sin_signed, sinks, consts, *, n_batch, tiles_per_batch, n_seq, seq_rows, past_valid):
    rows = n_seq * seq_rows
    tail = min(WINDOW, seq_rows)
    n_str = n_batch * n_seq
    whole = lambda *shape: pl.BlockSpec(shape, lambda b, j, sk: (0,) * len(shape))
    once = lambda *shape: pl.BlockSpec(shape, lambda b, j, sk: (0,) * len(shape), pipeline_mode=pl.Buffered(1))
    in_specs = [pl.BlockSpec((rows, D_MODEL), lambda b, j, sk: (b * tiles_per_batch + j, 0)),
                pl.BlockSpec((n_seq, WINDOW, C_KV_DIM), lambda b, j, sk: (b, 0, 0)),
                pl.BlockSpec((n_seq, WINDOW, C_KV_DIM), lambda b, j, sk: (b, 0, 0)),
                pl.BlockSpec((rows, LANES), lambda b, j, sk: (j, 0)),
                pl.BlockSpec((rows, LANES), lambda b, j, sk: (j, 0)),
                whole(1, D_MODEL), once(D_MODEL, ODD_IN), once(D_MODEL, D_MODEL),
                whole(1, D_MODEL), whole(1, C_KV_DIM), whole(D_MODEL, D_MODEL)]
    args = [x, pk, pv, cos, sin_signed] + list(consts)
    kv_out = jax.ShapeDtypeStruct((n_str, tail, C_KV_DIM), F32)
    kv_spec = pl.BlockSpec((n_seq, tail, C_KV_DIM), lambda b, j, sk: (b, 0, 0))
    grid_spec = pltpu.PrefetchScalarGridSpec(
        num_scalar_prefetch=1, grid=(n_batch, tiles_per_batch), in_specs=in_specs,
        out_specs=[pl.BlockSpec((rows, D_MODEL), lambda b, j, sk: (b * tiles_per_batch + j, 0)),
                   kv_spec, kv_spec],
        scratch_shapes=[pltpu.VMEM((n_seq, WINDOW + seq_rows, C_KV_DIM), F32),
                        pltpu.VMEM((n_seq, WINDOW + seq_rows, C_KV_DIM), F32),
                        pltpu.VMEM((rows, D_MODEL), F32), pltpu.VMEM((rows, D_MODEL), F32),
                        pltpu.VMEM((D_MODEL, ODD_IN), BF16), pltpu.VMEM((D_MODEL, D_MODEL), BF16)])
    return pl.pallas_call(
        functools.partial(_swa_body, n_seq=n_seq, seq_rows=seq_rows, past_valid=past_valid),
        grid_spec=grid_spec,
        out_shape=[jax.ShapeDtypeStruct(x.shape, F32), kv_out, kv_out],
        compiler_params=_cparams(2),
        name="swa_s" if past_valid else "swa_p")(sinks, *args)


def _router_body(xp_ref, xs_ref, nrm_ref, rhi_ref, rlo_ref, lstrict_ref, su_ref, info_ref, cnt_ref, *,
                 n_prompt_tiles):
    def run(x_ref):
        h = _rms(x_ref[...], nrm_ref[...])
        hi = h.astype(BF16)
        lo = (h - hi.astype(F32)).astype(BF16)
        logits = _dot(hi, rhi_ref[...]) + _dot(lo, rhi_ref[...]) + _dot(hi, rlo_ref[...])
        lane = lax.broadcasted_iota(jnp.int32, logits.shape, 1).astype(F32)
        logits = jnp.where(lane < N_EXPERTS, logits, -jnp.inf)
        l1 = jnp.max(logits, axis=-1, keepdims=True)
        i1 = jnp.min(jnp.where(logits == l1, lane, float(LANES)), axis=-1, keepdims=True)
        rest = jnp.where(lane == i1, -jnp.inf, logits)
        l2 = jnp.max(rest, axis=-1, keepdims=True)
        i2 = jnp.min(jnp.where(rest == l2, lane, float(LANES)), axis=-1, keepdims=True)
        e = jnp.exp(l2 - l1)
        g1 = 1.0 / (1.0 + e)
        g2 = e * g1
        oh1 = (lane == i1).astype(F32)
        oh2 = (lane == i2).astype(F32)
        cnt1 = jnp.sum(oh1, axis=0, keepdims=True)
        cnt2 = jnp.sum(oh2, axis=0, keepdims=True)
        seg = jnp.broadcast_to(jnp.ceil((cnt1 + cnt2) * (1.0 / GROUP)), (GROUP, LANES))
        off = _dot(seg.astype(BF16), su_ref[...])[0:1] * GROUP
        c1 = _dot(lstrict_ref[...], oh1.astype(BF16)) + off
        c2 = _dot(lstrict_ref[...], oh2.astype(BF16)) + off + cnt1
        d1 = jnp.sum(jnp.where(lane == i1, c1, 0.0), axis=-1, keepdims=True)
        d2 = jnp.sum(jnp.where(lane == i2, c2, 0.0), axis=-1, keepdims=True)
        info = jnp.zeros(logits.shape, F32)
        for n, col in enumerate([d1, d2, g1, g2]):
            info = jnp.where(lane == n, col, info)
        info_ref[...] = info
        cnt_ref[...] = seg * GROUP

    _on_row_source(pl.program_id(0), n_prompt_tiles, run, (xp_ref,), (xs_ref,))


def _router(xp, xs, nrm, rhi, rlo, lstrict, su):
    n_prompt_tiles = xp.shape[0] // ROW_TILE
    n_tiles = n_prompt_tiles + xs.shape[0] // ROW_TILE
    whole = lambda *shape: pl.BlockSpec(shape, lambda i: (0,) * len(shape))
    return pl.pallas_call(
        functools.partial(_router_body, n_prompt_tiles=n_prompt_tiles), grid=(n_tiles,),
        in_specs=[*_two_source_specs(n_prompt_tiles), whole(1, D_MODEL), whole(D_MODEL, LANES), whole(D_MODEL, LANES),
                  whole(ROW_TILE, ROW_TILE), whole(LANES, LANES)],
        out_specs=[pl.BlockSpec((ROW_TILE, LANES), lambda i: (i, 0)), pl.BlockSpec((GROUP, LANES), lambda i: (i, 0))],
        out_shape=[jax.ShapeDtypeStruct((n_tiles * ROW_TILE, LANES), F32),
                   jax.ShapeDtypeStruct((n_tiles * GROUP, LANES), F32)],
        compiler_params=_cparams(1), name="router")(xp, xs, nrm, rhi, rlo, lstrict, su)


BIG_ROWS = 8 * GROUP
SEG_FIELDS = 4
TILE_FIELDS = 2
TAIL_FIELDS = 3


def _rows_copy(src_ref, src_row, dst_ref, dst_row, n_rows, sem):
    return pltpu.make_async_copy(src_ref.at[pl.ds(pl.multiple_of(src_row, GROUP), n_rows)],
                                 dst_ref.at[pl.ds(pl.multiple_of(dst_row, GROUP), n_rows)], sem)


def _start_pieces(src_ref, src0, dst_ref, dst0, n_big, n_small, sem):
    def big(g, carry):
        _rows_copy(src_ref, src0 + g * BIG_ROWS, dst_ref, dst0 + g * BIG_ROWS, BIG_ROWS, sem).start()
        return carry

    lax.fori_loop(0, n_big, big, 0)
    done = n_big * BIG_ROWS

    def small(g, carry):
        _rows_copy(src_ref, src0 + done + g * GROUP, dst_ref, dst0 + done + g * GROUP, GROUP, sem).start()
        return carry

    lax.fori_loop(0, n_small, small, 0)


def _wait_pieces(n_big, n_small, src_ref, dst_ref, sem):
    def big(g, carry):
        _rows_copy(src_ref, 0, dst_ref, 0, BIG_ROWS, sem).wait()
        return carry

    lax.fori_loop(0, n_big, big, 0)

    def small(g, carry):
        _rows_copy(src_ref, 0, dst_ref, 0, GROUP, sem).wait()
        return carry

    lax.fori_loop(0, n_small, small, 0)


def _segment_table(seg, n_sorted_tiles):
    pieces = lambda rows: (rows // BIG_ROWS, (rows % BIG_ROWS) // GROUP)
    local_off = jnp.cumsum(seg, axis=1) - seg
    expert_rows = jnp.sum(seg, axis=0)
    expert_pad = ((expert_rows + ROW_TILE - 1) // ROW_TILE) * ROW_TILE
    expert_end = jnp.cumsum(expert_pad)
    expert_off = expert_end - expert_pad
    seg_start = expert_off[None, :] + jnp.cumsum(seg, axis=0) - seg
    n_used = (expert_end[-1] // ROW_TILE).reshape(1)
    tile_ids = jnp.arange(n_sorted_tiles, dtype=jnp.int32)
    tile_expert = jnp.minimum(jnp.sum(tile_ids[:, None] >= (expert_end // ROW_TILE)[None, :], axis=1), N_EXPERTS - 1)
    tile_expert = jnp.where(tile_ids < n_used[0], tile_expert, tile_expert[n_used[0] - 1])
    seg_big, seg_small = pieces(seg)
    tail_big, tail_small = pieces(expert_pad - expert_rows)
    table = jnp.concatenate([
        jnp.stack([seg_start, seg_big, seg_small, local_off], axis=-1).reshape(-1),
        jnp.stack([jnp.sum(seg_big, axis=1), jnp.sum(seg_small, axis=1)], axis=-1).reshape(-1),
        jnp.stack([expert_off + expert_rows, tail_big, tail_small], axis=-1).reshape(-1),
        n_used])
    return table.astype(jnp.int32), tile_expert.astype(jnp.int32), n_used.astype(jnp.int32)


def _table_sections(n_tiles):
    tiles0 = SEG_FIELDS * n_tiles * N_EXPERTS
    tails0 = tiles0 + TILE_FIELDS * n_tiles
    return tiles0, tails0, tails0 + TAIL_FIELDS * N_EXPERTS


def _dispatch_body(tab_ref, info_ref, xp_ref, xs_ref, nrm_ref, sel_ref, out_ref, loc_scr, z_scr, sems, *,
                   n_prompt_tiles, n_tiles, n_sorted_tiles):
    i = pl.program_id(0)
    slot = i % 2
    tiles0, tails0, used0 = _table_sections(n_tiles)

    def wait_tile(t, s):
        _wait_pieces(tab_ref[tiles0 + TILE_FIELDS * t], tab_ref[tiles0 + TILE_FIELDS * t + 1], loc_scr.at[s], out_ref,
                     sems.at[s])

    @pl.when(i >= 2)
    def _():
        wait_tile(i - 2, slot)

    def run(x_ref):
        hb = _rms(x_ref[...], nrm_ref[...]).astype(BF16)
        info = info_ref[...]
        lane = lax.broadcasted_iota(jnp.int32, info.shape, 1)
        dest = jnp.where(lane < TOP_K, info, 0.0)
        drow = sum(_dot_nt(sel_ref[...], part) for part in _split3(dest))
        r = lax.broadcasted_iota(jnp.int32, (LOCAL_ROWS, ROW_TILE), 0).astype(F32)
        perm = jnp.where(r == drow[0:1], 1.0, jnp.where(r == drow[1:2], 1.0, 0.0)).astype(BF16)
        loc_scr[slot] = _dot(perm, hb)

    _on_row_source(i, n_prompt_tiles, run, (xp_ref,), (xs_ref,))

    for e in range(N_EXPERTS):
        base = SEG_FIELDS * (i * N_EXPERTS + e)
        _start_pieces(loc_scr.at[slot], tab_ref[base + 3], out_ref, tab_ref[base], tab_ref[base + 1],
                      tab_ref[base + 2], sems.at[slot])

    @pl.when(i == n_tiles - 1)
    def _():
        wait_tile(i, slot)
        if n_tiles > 1:
            wait_tile(i - 1, 1 - slot)
        z_scr[...] = jnp.zeros_like(z_scr)
        for e in range(N_EXPERTS):
            base = tails0 + TAIL_FIELDS * e
            _start_pieces(z_scr, 0, out_ref, tab_ref[base], tab_ref[base + 1], tab_ref[base + 2], sems.at[2])
            _wait_pieces(tab_ref[base + 1], tab_ref[base + 2], z_scr, out_ref, sems.at[2])

        def zero_tile(t, carry):
            cp = _rows_copy(z_scr, 0, out_ref, t * ROW_TILE, ROW_TILE, sems.at[2])
            cp.start()
            cp.wait()
            return carry

        lax.fori_loop(tab_ref[used0], n_sorted_tiles, zero_tile, 0)


def _dispatch(table, info, xp, xs, nrm, sel, n_sorted_tiles):
    n_prompt_tiles = xp.shape[0] // ROW_TILE
    n_tiles = n_prompt_tiles + xs.shape[0] // ROW_TILE
    whole = lambda *shape: pl.BlockSpec(shape, lambda i, tab: (0,) * len(shape))
    grid_spec = pltpu.PrefetchScalarGridSpec(
        num_scalar_prefetch=1, grid=(n_tiles,),
        in_specs=[pl.BlockSpec((ROW_TILE, LANES), lambda i, tab: (i, 0)), *_two_source_specs(n_prompt_tiles),
                  whole(1, D_MODEL), whole(GROUP, LANES)],
        out_specs=pl.BlockSpec(memory_space=pl.ANY),
        scratch_shapes=[pltpu.VMEM((2, LOCAL_ROWS, D_MODEL), F32), pltpu.VMEM((ROW_TILE, D_MODEL), F32),
                        pltpu.SemaphoreType.DMA((3,))])
    return pl.pallas_call(
        functools.partial(_dispatch_body, n_prompt_tiles=n_prompt_tiles, n_tiles=n_tiles,
                          n_sorted_tiles=n_sorted_tiles),
        grid_spec=grid_spec, out_shape=jax.ShapeDtypeStruct((n_sorted_tiles * ROW_TILE, D_MODEL), F32),
        compiler_params=_cparams(1), name="moe_dispatch")(table, info, xp, xs, nrm, sel)


def _combine_body(tab_ref, info_ref, xp_ref, xs_ref, e_ref, yp_ref, ys_ref, loc_scr, sems, *, n_prompt_tiles,
                  n_tiles):
    i = pl.program_id(0)
    slot = i % 2
    tiles0, _, _ = _table_sections(n_tiles)

    def fetch(t, s):
        for e in range(N_EXPERTS):
            base = SEG_FIELDS * (t * N_EXPERTS + e)
            _start_pieces(e_ref, tab_ref[base], loc_scr.at[s], tab_ref[base + 3], tab_ref[base + 1],
                          tab_ref[base + 2], sems.at[s])

    @pl.when(i == 0)
    def _():
        loc_scr[...] = jnp.zeros_like(loc_scr)
        fetch(0, 0)

    @pl.when(i + 1 < n_tiles)
    def _():
        fetch(i + 1, 1 - slot)

    _wait_pieces(tab_ref[tiles0 + TILE_FIELDS * i], tab_ref[tiles0 + TILE_FIELDS * i + 1], e_ref, loc_scr.at[slot],
                 sems.at[slot])

    def run(x_ref, y_ref):
        info = info_ref[...]
        eb = loc_scr[slot].astype(BF16)
        lane = lax.broadcasted_iota(jnp.int32, (ROW_TILE, LOCAL_ROWS), 1).astype(F32)
        pick = jnp.zeros((ROW_TILE, LOCAL_ROWS), F32)
        for k in range(TOP_K):
            pick = jnp.where(lane == info[:, k:k + 1], info[:, TOP_K + k:TOP_K + k + 1], pick)
        y_ref[...] = x_ref[...] + _dot(pick.astype(BF16), eb)

    _on_row_source(i, n_prompt_tiles, run, (xp_ref, yp_ref), (xs_ref, ys_ref))


def _combine(table, info, xp, xs, esorted):
    n_prompt_tiles = xp.shape[0] // ROW_TILE
    n_tiles = n_prompt_tiles + xs.shape[0] // ROW_TILE
    grid_spec = pltpu.PrefetchScalarGridSpec(
        num_scalar_prefetch=1, grid=(n_tiles,),
        in_specs=[pl.BlockSpec((ROW_TILE, LANES), lambda i, tab: (i, 0)), *_two_source_specs(n_prompt_tiles),
                  pl.BlockSpec(memory_space=pl.ANY)],
        out_specs=list(_two_source_specs(n_prompt_tiles)),
        scratch_shapes=[pltpu.VMEM((2, LOCAL_ROWS, D_MODEL), F32), pltpu.SemaphoreType.DMA((2,))])
    return pl.pallas_call(
        functools.partial(_combine_body, n_prompt_tiles=n_prompt_tiles, n_tiles=n_tiles),
        grid_spec=grid_spec,
        out_shape=[jax.ShapeDtypeStruct(xp.shape, F32), jax.ShapeDtypeStruct(xs.shape, F32)],
        compiler_params=_cparams(1), name="moe_combine")(table, info, xp, xs, esorted)


def _rope_tables(pos):
    half = C_HDIM // 2
    inv = 1.0 / (ROPE_THETA ** (jnp.arange(half, dtype=F32) / half))
    ang = pos.astype(F32)[:, None] * inv[None, :]
    cos = jnp.cos(ang)
    sin = jnp.sin(ang)
    reps = LANES // C_HDIM
    return jnp.tile(jnp.concatenate([cos, cos], axis=-1), (1, reps)), jnp.tile(jnp.concatenate([-sin, sin], axis=-1),
                                                                                 (1, reps))


def kernel(x_prompt, x_sample, mem_prompt, cache_mem_k, cache_mem_v, state_hgrn, cache_swa_k, cache_swa_v, norm_mix, norm_xattn, norm_ffn, even_w_in, even_w_out, gmlp_w_s, gmlp_b_s, gmlp_ln_g, gmlp_ln_b, hgrn_lb_logits, hgrn_out_norm, attn_w_in, attn_w_out, attn_q_norm, attn_k_norm, attn_sinks, xattn_mem_norm, xattn_w_q, xattn_w_k, xattn_w_v, xattn_w_o, xattn_q_norm, xattn_k_norm, ffn_w1, ffn_w3, ffn_w2, moe_router, moe_w1, moe_w3, moe_w2):
    n_batch, seq, d = x_prompt.shape
    dec_batch, dec_seq, _ = x_sample.shape
    n_mem = mem_prompt.shape[1]
    depth = norm_mix.shape[0]
    past_len = PAST_LEN
    assert d == D_MODEL and depth == 2 and seq % ROW_TILE == 0 and dec_batch * dec_seq == ROW_TILE
    assert dec_seq == CHUNK and ROW_TILE % n_mem == 0 and cache_swa_k.shape[2] == WINDOW
    assert even_w_in.shape[-1] == EVEN_IN and attn_w_in.shape[-1] == ODD_IN and ffn_w1.shape[-1] == D_FF
    assert moe_w1.shape[1] == N_EXPERTS
    n_prompt_rows = n_batch * seq
    n_rows = n_prompt_rows + dec_batch * dec_seq
    n_prompt_tiles = n_prompt_rows // ROW_TILE
    tiles_per_batch = seq // ROW_TILE
    bf = lambda w: w.astype(BF16)
    row = lambda g: g.reshape(1, -1).astype(F32)

    lb_all = jnp.cumsum(jax.nn.softmax(hgrn_lb_logits.astype(F32), axis=0), axis=0)
    wcum = jnp.asarray(_cumsum_matrix(), BF16)
    lmask = jnp.asarray(_level_masks(), F32)

    def even_consts(n):
        tril = jnp.tril(jnp.ones((n, n), bool))
        ws = jnp.where(tril[None], gmlp_w_s[0, :, :n, :n], 0.0).astype(BF16)
        bs = jnp.broadcast_to(gmlp_b_s[0, :, :n, None], (A_GROUPS, n, LANES)).astype(F32)
        return [row(norm_mix[0]), even_w_in[0], even_w_out[0], ws, bs,
                gmlp_ln_g[0].reshape(A_GROUPS, 1, A_GDIM), gmlp_ln_b[0].reshape(A_GROUPS, 1, A_GDIM),
                row(lb_all[0]), row(hgrn_out_norm[0]), wcum, lmask]

    xp, hgrn_p = _even_mixer(
        x_prompt.reshape(n_prompt_rows, d), jnp.zeros((n_batch, B_HEADS, B_HDIM, B_HDIM), F32),
        even_consts(A_CHUNK), n_batch=n_batch, tiles_per_batch=tiles_per_batch, n_seq=1, seq_rows=ROW_TILE,
        gchunk=A_CHUNK, emit_v=False)
    xs, hgrn_s, gmlp_v = _even_mixer(
        x_sample.reshape(ROW_TILE, d), state_hgrn[0], even_consts(min(A_CHUNK, dec_seq)), n_batch=1,
        tiles_per_batch=1, n_seq=dec_batch, seq_rows=dec_seq, gchunk=min(A_CHUNK, dec_seq), emit_v=True)

    mem_k, mem_v = _memory_kv(mem_prompt.reshape(n_batch * n_mem, d), xattn_mem_norm.reshape(depth, 1, d),
                              xattn_w_k, xattn_w_v, xattn_k_norm.reshape(depth, 1, X_HDIM),
                              n_batch=n_batch, n_mem=n_mem)

    def cross_attention(xp, xs, l):
        consts = (row(norm_xattn[l]), xattn_w_q[l], xattn_w_o[l], row(xattn_q_norm[l]))
        xp = _xattn(xp, mem_k[l], mem_v[l], *consts, steps_per_mem=tiles_per_batch, n_seq=1, seq_rows=ROW_TILE)
        xs = _xattn(xs, cache_mem_k[l].reshape(dec_batch, n_mem, X_DIM),
                    cache_mem_v[l].reshape(dec_batch, n_mem, X_DIM), *consts, steps_per_mem=1, n_seq=dec_batch,
                    seq_rows=dec_seq)
        return xp, xs

    xp, xs = cross_attention(xp, xs, 0)
    xp, xs = _dense_ffn(xp, xs, row(norm_ffn[0]), ffn_w1[0], ffn_w3[0], ffn_w2[0])

    reps = D_MODEL // C_HDIM
    bd = jnp.asarray(np.kron(np.eye(reps, dtype=np.float32), np.full((C_HDIM, C_HDIM), 1.0 / C_HDIM, np.float32)),
                     BF16)
    swa_consts = [row(norm_mix[1]), attn_w_in[0], attn_w_out[0], row(jnp.tile(attn_q_norm[0], C_HEADS)),
                  row(jnp.tile(attn_k_norm[0], C_KV_HEADS)), bd]
    sinks = attn_sinks[0].astype(F32)
    cos_p, sin_p = _rope_tables(jnp.arange(seq, dtype=jnp.int32))
    cos_s, sin_s = _rope_tables(past_len + jnp.arange(dec_seq, dtype=jnp.int32))
    no_past = jnp.zeros((n_batch, WINDOW, C_KV_DIM), F32)
    xp, swk_p, swv_p = _swa(xp, no_past, no_past, cos_p, sin_p, sinks, swa_consts, n_batch=n_batch,
                            tiles_per_batch=tiles_per_batch, n_seq=1, seq_rows=ROW_TILE, past_valid=False)
    xs, swk_s, swv_s = _swa(xs, cache_swa_k[0].reshape(dec_batch, WINDOW, C_KV_DIM),
                            cache_swa_v[0].reshape(dec_batch, WINDOW, C_KV_DIM),
                            jnp.tile(cos_s, (dec_batch, 1)), jnp.tile(sin_s, (dec_batch, 1)), sinks, swa_consts,
                            n_batch=1, tiles_per_batch=1, n_seq=dec_batch, seq_rows=dec_seq, past_valid=True)
    xp, xs = cross_attention(xp, xs, 1)

    n_tiles = n_rows // ROW_TILE
    router_w = jnp.zeros((d, LANES), F32).at[:, :N_EXPERTS].set(moe_router[0].astype(F32))
    rhi = router_w.astype(BF16)
    rlo = (router_w - rhi.astype(F32)).astype(BF16)
    lstrict = jnp.asarray(np.tril(np.ones((ROW_TILE, ROW_TILE), np.float32), -1), BF16)
    su = jnp.asarray(np.triu(np.ones((LANES, LANES), np.float32), 1), BF16)
    sel = jnp.asarray(np.eye(GROUP, LANES, dtype=np.float32), BF16)
    nrm_ffn = row(norm_ffn[1])
    info, seg = _router(xp, xs, nrm_ffn, rhi, rlo, lstrict, su)
    seg = seg.reshape(n_tiles, GROUP, LANES)[:, 0, :N_EXPERTS].astype(jnp.int32)
    n_sorted_tiles = -(-(TOP_K * n_rows + n_tiles * N_EXPERTS * (GROUP - 1) + N_EXPERTS * (ROW_TILE - 1)) // ROW_TILE)
    table, tile_expert, n_used = _segment_table(seg, n_sorted_tiles)
    hsorted = _dispatch(table, info, xp, xs, nrm_ffn, sel, n_sorted_tiles)
    esorted = _grouped_ffn(hsorted, tile_expert, n_used, bf(moe_w1[0]), bf(moe_w3[0]), bf(moe_w2[0]))
    y_prompt, y_sample = _combine(table, info, xp, xs, esorted)

    n_even = state_hgrn.shape[0]
    n_odd = cache_swa_k.shape[0]
    return (y_prompt.reshape(n_batch, seq, d), y_sample.reshape(dec_batch, dec_seq, d),
            mem_k.reshape(depth, n_batch, n_mem, X_HEADS, X_HDIM), mem_v.reshape(depth, n_batch, n_mem, X_HEADS, X_HDIM),
            hgrn_p.reshape(n_even, n_batch, B_HEADS, B_HDIM, B_HDIM),
            gmlp_v.reshape(n_even, dec_batch, dec_seq, A_GROUPS, A_GDIM),
            hgrn_s.reshape(n_even, dec_batch, B_HEADS, B_HDIM, B_HDIM),
            swk_p.reshape(n_odd, n_batch, WINDOW, C_KV_HEADS, C_HDIM), swv_p.reshape(n_odd, n_batch, WINDOW, C_KV_HEADS, C_HDIM),
            swk_s.reshape(n_odd, dec_batch, dec_seq, C_KV_HEADS, C_HDIM), swv_s.reshape(n_odd, dec_batch, dec_seq, C_KV_HEADS, C_HDIM))
```

```python
import functools

import numpy as np
import jax
import jax.numpy as jnp
from jax import lax
from jax.experimental import pallas as pl
from jax.experimental.pallas import tpu as pltpu

F32 = jnp.float32
BF16 = jnp.bfloat16

D_MODEL = 1024
EPS = 1e-6
LOG2_E = 1.4426950408889634
CHUNK = 64
A_GROUPS = 4
A_DIM = D_MODEL // 2
A_GDIM = A_DIM // A_GROUPS
A_CHUNK = 128
B_HEADS = 4
B_DIM = D_MODEL // 2
B_HDIM = B_DIM // B_HEADS
EVEN_IN = 2 * A_DIM + 4 * B_DIM
C_HEADS = 16
C_KV_HEADS = 4
C_HDIM = D_MODEL // C_HEADS
C_GROUP = C_HEADS // C_KV_HEADS
C_KV_DIM = C_KV_HEADS * C_HDIM
WINDOW = 128
ROPE_THETA = 10000.0
PAST_LEN = 4096
ODD_IN = (C_HEADS + 2 * C_KV_HEADS) * C_HDIM
X_HEADS = 4
X_HDIM = 128
X_DIM = X_HEADS * X_HDIM
D_FF = 2816
N_EXPERTS = 8
TOP_K = 2

LANES = 128
ROW_TILE = 512
FF_CHUNK = 2816
N_FF_CHUNKS = D_FF // FF_CHUNK
FF_SUB = 512
N_LEVELS = 6
HGRN_STAGE_CHUNKS = 2
SWA_STAGE_CHUNKS = 2
GROUP = 8
LOCAL_ROWS = -(-(TOP_K * ROW_TILE + N_EXPERTS * (GROUP - 1)) // LANES) * LANES
VMEM_LIMIT = 58 * 1024 * 1024


def _cparams(n_axes):
    return pltpu.CompilerParams(dimension_semantics=("arbitrary",) * n_axes, vmem_limit_bytes=VMEM_LIMIT)


def _dot(a, b):
    return jnp.dot(a, b, preferred_element_type=F32)


def _dot_nt(a, b):
    return lax.dot_general(a, b, (((1,), (1,)), ((), ())), preferred_element_type=F32)


def _rms(x, g):
    return x * lax.rsqrt(jnp.mean(x * x, axis=-1, keepdims=True) + EPS) * g


def _silu(x):
    return x * (1.0 / (1.0 + jnp.exp(-x)))


def _split3(x):
    hi = x.astype(BF16)
    r1 = x - hi.astype(F32)
    mid = r1.astype(BF16)
    lo = (r1 - mid.astype(F32)).astype(BF16)
    return hi, mid, lo


def _cumsum_matrix():
    r = np.arange(CHUNK)
    s = np.arange(CHUNK)
    blocks = [(s[None, :] <= r[:, None])]
    for l in range(N_LEVELS):
        h = 1 << l
        ref = (r & ~(2 * h - 1)) + h - 1
        blocks.append(s[None, :] <= ref[:, None])
    w = np.concatenate(blocks, axis=0).astype(np.float32)
    return np.concatenate([w, w, w], axis=1)


def _level_masks():
    t = np.arange(CHUNK)[:, None]
    s = np.arange(CHUNK)[None, :]
    masks = []
    for l in range(N_LEVELS):
        masks.append(((t >> (l + 1)) == (s >> (l + 1))) & (((t >> l) & 1) == 1) & (((s >> l) & 1) == 0))
    masks.append(t == s)
    return np.stack(masks).astype(np.float32)


def _even_mixer_body(x_ref, s0_ref, nrm_ref, win32_ref, wout32_ref, ws_ref, bs_ref, lng_ref, lnb_ref, lb_ref, og_ref,
                     wcum_ref, lmask_ref, *rest, n_seq, seq_rows, gchunk, emit_v):
    if emit_v:
        y_ref, sout_ref, v_ref, proj_scr, mixed_scr, st_scr, win_ref, wout_ref = rest
    else:
        y_ref, sout_ref, proj_scr, mixed_scr, st_scr, win_ref, wout_ref = rest
        v_ref = None
    rows = n_seq * seq_rows
    j = pl.program_id(1)

    @pl.when((pl.program_id(0) == 0) & (j == 0))
    def _():
        win_ref[...] = win32_ref[...].astype(BF16)
        wout_ref[...] = wout32_ref[...].astype(BF16)

    @pl.when(j == 0)
    def _():
        for s in range(n_seq):
            for hd in range(B_HEADS):
                st_scr[s * B_HEADS + hd] = s0_ref[s, hd].T

    x = x_ref[...]
    h = _rms(x, nrm_ref[...]).astype(BF16)
    n_pieces = EVEN_IN // ROW_TILE
    for n in range(n_pieces):
        cs = slice(n * ROW_TILE, (n + 1) * ROW_TILE)
        proj_scr[:, cs] = _dot(h, win_ref[:, cs])

    groups = range(A_GROUPS)
    group_cols = lambda base, g: slice(base + g * A_GDIM, base + (g + 1) * A_GDIM)
    vgs = [jax.nn.gelu(proj_scr[:, group_cols(A_DIM, g)]) for g in groups]
    means = [jnp.mean(vg, axis=-1, keepdims=True) for vg in vgs]
    vcs = [vg - mu for vg, mu in zip(vgs, means)]
    variances = [jnp.mean(vc * vc, axis=-1, keepdims=True) for vc in vcs]
    for g in groups:
        gs = group_cols(0, g)
        vn = vcs[g] * lax.rsqrt(variances[g] + EPS) * lng_ref[g] + lnb_ref[g]
        if v_ref is not None:
            v_ref[:, gs] = vn
        ug = jax.nn.gelu(proj_scr[:, gs])
        vb = vn.astype(BF16)
        for c in range(rows // gchunk):
            rs = slice(c * gchunk, (c + 1) * gchunk)
            sp = _dot(ws_ref[g], vb[rs]) + bs_ref[g]
            mixed_scr[rs, gs] = ug[rs] * sp

    q0, f0, i0, g0 = (2 * A_DIM + k * B_DIM for k in range(4))
    lb = lb_ref[...]
    fg = lb + (1.0 - lb) * jax.nn.sigmoid(proj_scr[:, f0:f0 + B_DIM])
    proj_scr[:, 0:B_DIM] = jnp.log(fg) * LOG2_E
    proj_scr[:, B_DIM:2 * B_DIM] = 1.0 - fg
    proj_scr[:, q0:q0 + B_DIM] = _silu(proj_scr[:, q0:q0 + B_DIM])
    og = og_ref[...]
    chunks_per_seq = seq_rows // CHUNK

    heads = range(B_HEADS)
    head_cols = lambda base, hd: slice(base + hd * B_HDIM, base + (hd + 1) * B_HDIM)
    chunk_rows = lambda c: slice(c * CHUNK, (c + 1) * CHUNK)

    def chunk_local(c):
        rs = chunk_rows(c)
        hi, mid, lo = _split3(proj_scr[rs, 0:B_DIM])
        gg = _dot(wcum_ref[...], jnp.concatenate([hi, mid, lo], axis=0))
        vs = [proj_scr[rs, head_cols(i0, hd)] for hd in heads]
        vts = [v.T.astype(BF16) for v in vs]
        Gs = [gg[0:CHUNK, head_cols(0, hd)] for hd in heads]
        qs = [proj_scr[rs, head_cols(q0, hd)] for hd in heads]
        ks = [proj_scr[rs, head_cols(B_DIM, hd)] for hd in heads]
        operands = []
        for hd in heads:
            for l in range(N_LEVELS + 1):
                if l < N_LEVELS:
                    e = jnp.exp2(-jnp.abs(Gs[hd] - gg[(l + 1) * CHUNK:(l + 2) * CHUNK, head_cols(0, hd)]))
                    operands.append(((qs[hd] * e).astype(BF16), (ks[hd] * e).astype(BF16)))
                else:
                    operands.append((qs[hd].astype(BF16), ks[hd].astype(BF16)))
        blocks = [_dot_nt(qe, ke) for qe, ke in operands]
        parts = []
        for hd in heads:
            att = jnp.zeros((CHUNK, CHUNK), F32)
            for l in range(N_LEVELS + 1):
                att = jnp.where(lmask_ref[l] > 0.5, blocks[hd * (N_LEVELS + 1) + l], att)
            g_end = Gs[hd][CHUNK - 1:CHUNK, :]
            kd = (ks[hd] * jnp.exp2(g_end - Gs[hd])).astype(BF16)
            parts.append((_dot(att.astype(BF16), vs[hd].astype(BF16)), (qs[hd] * jnp.exp2(Gs[hd])).astype(BF16),
                          jnp.exp2(g_end), _dot(vts[hd], kd)))
        return parts

    def chunk_state(c, parts):
        rs = chunk_rows(c)
        sidx = (c // chunks_per_seq) * B_HEADS
        outs = []
        for hd in heads:
            o_local, q_decayed, decay, increment = parts[hd]
            st = st_scr[sidx + hd]
            outs.append(o_local + _dot_nt(q_decayed, st.astype(BF16)))
            st_scr[sidx + hd] = st * decay + increment
        for hd in heads:
            o = outs[hd]
            on = o * lax.rsqrt(jnp.mean(o * o, axis=-1, keepdims=True) + EPS) * og
            gate = _silu(proj_scr[rs, head_cols(g0, hd)])
            mixed_scr[rs, head_cols(A_DIM, hd)] = on * gate

    n_chunks = rows // CHUNK
    for first in range(0, n_chunks, HGRN_STAGE_CHUNKS):
        group = range(first, min(first + HGRN_STAGE_CHUNKS, n_chunks))
        local = [chunk_local(c) for c in group]
        for c, parts in zip(group, local):
            chunk_state(c, parts)

    y_ref[...] = x + _dot(mixed_scr[...].astype(BF16), wout_ref[...])

    @pl.when(j == pl.num_programs(1) - 1)
    def _():
        for s in range(n_seq):
            for hd in range(B_HEADS):
                sout_ref[s, hd] = st_scr[s * B_HEADS + hd].T


def _even_mixer(x, s0, consts, *, n_batch, tiles_per_batch, n_seq, seq_rows, gchunk, emit_v):
    rows = n_seq * seq_rows
    grid = (n_batch, tiles_per_batch)
    tile = lambda b, j: (b * tiles_per_batch + j, 0)
    whole = lambda *shape: pl.BlockSpec(shape, lambda b, j: (0,) * len(shape))
    once = lambda *shape: pl.BlockSpec(shape, lambda b, j: (0,) * len(shape), pipeline_mode=pl.Buffered(1))
    in_specs = [
        pl.BlockSpec((rows, D_MODEL), tile),
        pl.BlockSpec((n_seq, B_HEADS, B_HDIM, B_HDIM), lambda b, j: (b, 0, 0, 0)),
        whole(1, D_MODEL), once(D_MODEL, EVEN_IN), once(D_MODEL, D_MODEL),
        whole(A_GROUPS, gchunk, gchunk), whole(A_GROUPS, gchunk, LANES),
        whole(A_GROUPS, 1, A_GDIM), whole(A_GROUPS, 1, A_GDIM), whole(1, B_DIM), whole(1, B_HDIM),
        whole((N_LEVELS + 1) * CHUNK, 3 * CHUNK), whole(N_LEVELS + 1, CHUNK, CHUNK),
    ]
    args = [x, s0] + list(consts)
    out_shape = [jax.ShapeDtypeStruct(x.shape, F32),
                 jax.ShapeDtypeStruct((n_batch * n_seq, B_HEADS, B_HDIM, B_HDIM), F32)]
    out_specs = [pl.BlockSpec((rows, D_MODEL), tile),
                 pl.BlockSpec((n_seq, B_HEADS, B_HDIM, B_HDIM), lambda b, j: (b, 0, 0, 0))]
    if emit_v:
        out_shape.append(jax.ShapeDtypeStruct((x.shape[0], A_DIM), F32))
        out_specs.append(pl.BlockSpec((rows, A_DIM), tile))
    body = functools.partial(_even_mixer_body, n_seq=n_seq, seq_rows=seq_rows, gchunk=gchunk, emit_v=emit_v)
    return pl.pallas_call(
        body, grid=grid, in_specs=in_specs, out_specs=out_specs, out_shape=out_shape,
        scratch_shapes=[pltpu.VMEM((rows, EVEN_IN), F32), pltpu.VMEM((rows, D_MODEL), F32),
                        pltpu.VMEM((n_seq * B_HEADS, B_HDIM, B_HDIM), F32),
                        pltpu.VMEM((D_MODEL, EVEN_IN), BF16), pltpu.VMEM((D_MODEL, D_MODEL), BF16)],
        compiler_params=_cparams(2),
        name="even_mixer_s" if emit_v else "even_mixer_p")(*args)


def _memory_kv_body(mem_ref, gm_ref, wk_ref, wv_ref, kg_ref, mk_ref, mv_ref, *, n_seq, n_mem):
    m = _rms(mem_ref[...], gm_ref[0]).astype(BF16)
    kk = _dot(m, wk_ref[0].astype(BF16))
    vv = _dot(m, wv_ref[0].astype(BF16))
    kg = kg_ref[0]
    for hd in range(X_HEADS):
        hs = slice(hd * X_HDIM, (hd + 1) * X_HDIM)
        kh = _rms(kk[:, hs], kg)
        for s in range(n_seq):
            mk_ref[0, s, :, hs] = kh[s * n_mem:(s + 1) * n_mem]
    for s in range(n_seq):
        mv_ref[0, s] = vv[s * n_mem:(s + 1) * n_mem]


def _memory_kv(mem2d, g_mem, w_k, w_v, k_g, *, n_batch, n_mem):
    depth = w_k.shape[0]
    n_seq = ROW_TILE // n_mem
    out = jax.ShapeDtypeStruct((depth, n_batch, n_mem, X_DIM), F32)
    ospec = pl.BlockSpec((1, n_seq, n_mem, X_DIM), lambda l, t: (l, t, 0, 0))
    return pl.pallas_call(
        functools.partial(_memory_kv_body, n_seq=n_seq, n_mem=n_mem),
        grid=(depth, n_batch // n_seq),
        in_specs=[pl.BlockSpec((ROW_TILE, D_MODEL), lambda l, t: (t, 0)),
                  pl.BlockSpec((1, 1, D_MODEL), lambda l, t: (l, 0, 0)),
                  pl.BlockSpec((1, D_MODEL, X_DIM), lambda l, t: (l, 0, 0)),
                  pl.BlockSpec((1, D_MODEL, X_DIM), lambda l, t: (l, 0, 0)),
                  pl.BlockSpec((1, 1, X_HDIM), lambda l, t: (l, 0, 0))],
        out_specs=[ospec, ospec], out_shape=[out, out], compiler_params=_cparams(2),
        name="memory_kv")(mem2d, g_mem, w_k, w_v, k_g)


def _xattn_body(x_ref, mk_ref, mv_ref, nrm_ref, wq32_ref, wo32_ref, qg_ref, y_ref, o_scr, wq_ref, wo_ref, *, n_seq,
                seq_rows):
    @pl.when(pl.program_id(0) == 0)
    def _():
        wq_ref[...] = wq32_ref[...].astype(BF16)
        wo_ref[...] = wo32_ref[...].astype(BF16)

    x = x_ref[...]
    h = _rms(x, nrm_ref[...]).astype(BF16)
    q = _dot(h, wq_ref[...])
    qg = qg_ref[...] * (X_HDIM ** -0.5)
    units = [(s, hd) for s in range(n_seq) for hd in range(X_HEADS)]
    rows_of = lambda s: slice(s * seq_rows, (s + 1) * seq_rows)
    lanes_of = lambda hd: slice(hd * X_HDIM, (hd + 1) * X_HDIM)
    qh = [_rms(q[rows_of(s), lanes_of(hd)], qg).astype(BF16) for s, hd in units]
    sc = [_dot_nt(qh[n], mk_ref[s, :, lanes_of(hd)].astype(BF16)) for n, (s, hd) in enumerate(units)]
    m = [jnp.max(t, axis=-1, keepdims=True) for t in sc]
    p = [jnp.exp(t - mx) for t, mx in zip(sc, m)]
    den = [jnp.sum(t, axis=-1, keepdims=True) for t in p]
    for n, (s, hd) in enumerate(units):
        o_scr[rows_of(s), lanes_of(hd)] = _dot(p[n].astype(BF16), mv_ref[s, :, lanes_of(hd)].astype(BF16)) / den[n]
    y_ref[...] = x + _dot(o_scr[...].astype(BF16), wo_ref[...])


def _xattn(x, mk, mv, nrm, wq, wo, qg, *, layer, steps_per_mem, n_seq, seq_rows):
    rows = n_seq * seq_rows
    n_mem = mk.shape[2]
    mem_spec = pl.BlockSpec((None, n_seq, n_mem, X_DIM), lambda t: (layer, t // steps_per_mem, 0, 0))
    whole = lambda *shape: pl.BlockSpec(shape, lambda t: (0,) * len(shape))
    once = lambda *shape: pl.BlockSpec(shape, lambda t: (0,) * len(shape), pipeline_mode=pl.Buffered(1))
    return pl.pallas_call(
        functools.partial(_xattn_body, n_seq=n_seq, seq_rows=seq_rows),
        grid=(x.shape[0] // rows,),
        in_specs=[pl.BlockSpec((rows, D_MODEL), lambda t: (t, 0)), mem_spec, mem_spec,
                  whole(1, D_MODEL), once(D_MODEL, X_DIM), once(X_DIM, D_MODEL), whole(1, X_HDIM)],
        out_specs=pl.BlockSpec((rows, D_MODEL), lambda t: (t, 0)),
        out_shape=jax.ShapeDtypeStruct(x.shape, F32),
        scratch_shapes=[pltpu.VMEM((rows, X_DIM), F32), pltpu.VMEM((D_MODEL, X_DIM), BF16),
                        pltpu.VMEM((X_DIM, D_MODEL), BF16)],
        compiler_params=_cparams(1), name="xattn_s" if n_seq > 1 else "xattn_p")(x, mk, mv, nrm, wq, wo, qg)


def _ff_chunk(i, j):
    return jnp.where(i % 2 == 0, j, N_FF_CHUNKS - 1 - j)


def _two_source_specs(n_prompt_tiles, width=D_MODEL):
    return (pl.BlockSpec((ROW_TILE, width), lambda i, *_: (jnp.minimum(i, n_prompt_tiles - 1), 0)),
            pl.BlockSpec((ROW_TILE, width), lambda i, *_: (jnp.maximum(i - n_prompt_tiles, 0), 0)))


def _on_row_source(i, n_prompt_tiles, fn, prompt_refs, sample_refs):
    pl.when(i < n_prompt_tiles)(lambda: fn(*prompt_refs))
    pl.when(i >= n_prompt_tiles)(lambda: fn(*sample_refs))


def _swiglu_part(hb, w1, w3, w2):
    part = None
    for c0 in range(0, FF_CHUNK, FF_SUB):
        cs = slice(c0, min(c0 + FF_SUB, FF_CHUNK))
        act = (_silu(_dot(hb, w1[:, cs])) * _dot(hb, w3[:, cs])).astype(BF16)
        p = _dot(act, w2[cs, :])
        part = p if part is None else part + p
    return part


W_IN_STAGE_ROWS = 128
W_OUT_STAGE_ROWS = 256


def _load_weights_as_bf16(w1_hbm, w3_hbm, w2_hbm, w1_scr, w3_scr, w2_scr, in_stage, out_stage, sems):
    jobs = ([(w_hbm, w_scr, in_stage, r0, W_IN_STAGE_ROWS) for w_hbm, w_scr in ((w1_hbm, w1_scr), (w3_hbm, w3_scr))
             for r0 in range(0, D_MODEL, W_IN_STAGE_ROWS)]
            + [(w2_hbm, w2_scr, out_stage, r0, W_OUT_STAGE_ROWS) for r0 in range(0, D_FF, W_OUT_STAGE_ROWS)])

    def copy(n):
        w_hbm, _, stage, r0, n_rows = jobs[n]
        return pltpu.make_async_copy(w_hbm.at[pl.ds(r0, n_rows)], stage.at[n % 2], sems.at[n % 2])

    copy(0).start()
    for n in range(len(jobs)):
        if n + 1 < len(jobs):
            copy(n + 1).start()
        copy(n).wait()
        _, w_scr, stage, r0, n_rows = jobs[n]
        w_scr[pl.ds(r0, n_rows), :] = stage[n % 2].astype(BF16)


def _dense_ffn_body(xp_ref, xs_ref, nrm_ref, w1_hbm, w3_hbm, w2_hbm, yp_ref, ys_ref, w1_scr, w3_scr, w2_scr,
                    in_stage, out_stage, sems, *, n_prompt_tiles):
    i = pl.program_id(0)

    @pl.when(i == 0)
    def _():
        _load_weights_as_bf16(w1_hbm, w3_hbm, w2_hbm, w1_scr, w3_scr, w2_scr, in_stage, out_stage, sems)

    def run(x_ref, y_ref):
        x = x_ref[...]
        y_ref[...] = x + _swiglu_part(_rms(x, nrm_ref[...]).astype(BF16), w1_scr, w3_scr, w2_scr)

    _on_row_source(i, n_prompt_tiles, run, (xp_ref, yp_ref), (xs_ref, ys_ref))


def _dense_ffn(xp, xs, nrm, w1, w3, w2):
    n_prompt_tiles = xp.shape[0] // ROW_TILE
    n_tiles = n_prompt_tiles + xs.shape[0] // ROW_TILE
    assert FF_CHUNK == D_FF
    in_hbm = pl.BlockSpec(memory_space=pl.ANY)
    return pl.pallas_call(
        functools.partial(_dense_ffn_body, n_prompt_tiles=n_prompt_tiles), grid=(n_tiles,),
        in_specs=[*_two_source_specs(n_prompt_tiles), pl.BlockSpec((1, D_MODEL), lambda i: (0, 0)),
                  in_hbm, in_hbm, in_hbm],
        out_specs=list(_two_source_specs(n_prompt_tiles)),
        out_shape=[jax.ShapeDtypeStruct(xp.shape, F32), jax.ShapeDtypeStruct(xs.shape, F32)],
        scratch_shapes=[pltpu.VMEM((D_MODEL, D_FF), BF16), pltpu.VMEM((D_MODEL, D_FF), BF16),
                        pltpu.VMEM((D_FF, D_MODEL), BF16), pltpu.VMEM((2, W_IN_STAGE_ROWS, D_FF), F32),
                        pltpu.VMEM((2, W_OUT_STAGE_ROWS, D_MODEL), F32), pltpu.SemaphoreType.DMA((2,))],
        compiler_params=_cparams(1), name="dense_ffn")(xp, xs, nrm, w1, w3, w2)


def _grouped_ffn_body(te_ref, nu_ref, x_ref, w1_ref, w3_ref, w2_ref, y_ref):
    i = pl.program_id(0)
    j = pl.program_id(1)

    @pl.when(i < nu_ref[0])
    def _():
        part = _swiglu_part(x_ref[...].astype(BF16), w1_ref.at[0], w3_ref.at[0], w2_ref.at[0])

        @pl.when(j == 0)
        def _():
            y_ref[...] = part

        @pl.when(j > 0)
        def _():
            y_ref[...] += part

    @pl.when(i >= nu_ref[0])
    def _():
        y_ref[...] = jnp.zeros_like(y_ref)


def _grouped_ffn(xsorted, tile_expert, n_used, w1, w3, w2):
    n_tiles = xsorted.shape[0] // ROW_TILE

    def row_tile(i, j, te, nu):
        return (i, 0)

    def ff(i, j, nu):
        last = nu[0] - 1
        return jnp.where(i <= last, _ff_chunk(i, j), _ff_chunk(last, N_FF_CHUNKS - 1))

    grid_spec = pltpu.PrefetchScalarGridSpec(
        num_scalar_prefetch=2, grid=(n_tiles, N_FF_CHUNKS),
        in_specs=[pl.BlockSpec((ROW_TILE, D_MODEL), row_tile),
                  pl.BlockSpec((1, D_MODEL, FF_CHUNK), lambda i, j, te, nu: (te[i], 0, ff(i, j, nu))),
                  pl.BlockSpec((1, D_MODEL, FF_CHUNK), lambda i, j, te, nu: (te[i], 0, ff(i, j, nu))),
                  pl.BlockSpec((1, FF_CHUNK, D_MODEL), lambda i, j, te, nu: (te[i], ff(i, j, nu), 0))],
        out_specs=pl.BlockSpec((ROW_TILE, D_MODEL), row_tile))
    return pl.pallas_call(
        _grouped_ffn_body, grid_spec=grid_spec, out_shape=jax.ShapeDtypeStruct(xsorted.shape, F32),
        compiler_params=_cparams(2), name="grouped_ffn")(tile_expert, n_used, xsorted, w1, w3, w2)


def _rope_slab(xs, cos, sin_signed, first_half):
    rot = jnp.where(first_half, pltpu.roll(xs, LANES - C_HDIM // 2, 1), pltpu.roll(xs, C_HDIM // 2, 1))
    return xs * cos + rot * sin_signed


def _swa_body(sink_ref, x_ref, pk_ref, pv_ref, cos_ref, sin_ref, nrm_ref, win32_ref, wout32_ref, qg_ref, kg_ref,
              bd_ref, y_ref, ko_ref, vo_ref, k_scr, v_scr, q_scr, a_scr, win_ref, wout_ref, *, n_seq, seq_rows,
              past_valid):
    rows = n_seq * seq_rows
    j = pl.program_id(1)
    tail = min(WINDOW, seq_rows)

    @pl.when((pl.program_id(0) == 0) & (j == 0))
    def _():
        win_ref[...] = win32_ref[...].astype(BF16)
        wout_ref[...] = wout32_ref[...].astype(BF16)

    if past_valid:
        for s in range(n_seq):
            k_scr[s, 0:WINDOW] = pk_ref[s]
            v_scr[s, 0:WINDOW] = pv_ref[s]
    else:
        @pl.when(j == 0)
        def _():
            for s in range(n_seq):
                k_scr[s, 0:WINDOW] = jnp.zeros((WINDOW, C_KV_DIM), F32)
                v_scr[s, 0:WINDOW] = jnp.zeros((WINDOW, C_KV_DIM), F32)

        @pl.when(j > 0)
        def _():
            for s in range(n_seq):
                k_scr[s, 0:WINDOW] = k_scr[s, seq_rows:seq_rows + WINDOW]
                v_scr[s, 0:WINDOW] = v_scr[s, seq_rows:seq_rows + WINDOW]

    x = x_ref[...]
    h = _rms(x, nrm_ref[...]).astype(BF16)
    q_dim = C_HEADS * C_HDIM
    q = _dot(h, win_ref[:, 0:q_dim])
    k = _dot(h, win_ref[:, q_dim:q_dim + C_KV_DIM])
    v = _dot(h, win_ref[:, q_dim + C_KV_DIM:q_dim + 2 * C_KV_DIM])

    cos = cos_ref[...]
    sin_signed = sin_ref[...]
    first_half = (lax.broadcasted_iota(jnp.int32, (rows, LANES), 1) % C_HDIM) < (C_HDIM // 2)
    bd = bd_ref[...]
    qn = q * lax.rsqrt(_dot((q * q).astype(BF16), bd) + EPS) * qg_ref[...]
    kn = k * lax.rsqrt(_dot((k * k).astype(BF16), bd[0:C_KV_DIM, 0:C_KV_DIM]) + EPS) * kg_ref[...]
    scale = C_HDIM ** -0.5
    for sl in range(q_dim // LANES):
        ls = slice(sl * LANES, (sl + 1) * LANES)
        q_scr[:, ls] = _rope_slab(qn[:, ls], cos, sin_signed, first_half) * scale
    for sl in range(C_KV_DIM // LANES):
        ls = slice(sl * LANES, (sl + 1) * LANES)
        kr = _rope_slab(kn[:, ls], cos, sin_signed, first_half)
        for s in range(n_seq):
            k_scr[s, WINDOW:WINDOW + seq_rows, ls] = kr[s * seq_rows:(s + 1) * seq_rows]
    for s in range(n_seq):
        v_scr[s, WINDOW:WINDOW + seq_rows] = v[s * seq_rows:(s + 1) * seq_rows]
        ko_ref[s] = k_scr[s, WINDOW + seq_rows - tail:WINDOW + seq_rows]
        vo_ref[s] = v_scr[s, WINDOW + seq_rows - tail:WINDOW + seq_rows]

    chunks_per_seq = seq_rows // CHUNK
    n_keys = WINDOW + CHUNK
    key_idx = lax.broadcasted_iota(jnp.int32, (C_GROUP * CHUNK, n_keys), 1)
    row_grp = lax.broadcasted_iota(jnp.int32, (C_GROUP * CHUNK, 1), 0) // CHUNK
    ones_keys = jnp.ones((n_keys, C_HDIM), BF16)

    head_sinks = []
    for kvh in range(C_KV_HEADS):
        sink = jnp.zeros((C_GROUP * CHUNK, 1), F32)
        for g in range(C_GROUP):
            sink = jnp.where(row_grp == g, sink_ref[kvh * C_GROUP + g], sink)
        head_sinks.append(sink)

    def attend(chunk_ids):
        units = [(idx, kvh) for idx in chunk_ids for kvh in range(C_KV_HEADS)]
        head_lanes = lambda kvh: slice(kvh * C_HDIM, (kvh + 1) * C_HDIM)
        q_rows = lambda idx: slice(idx * CHUNK, (idx + 1) * CHUNK)
        key_rows = lambda idx: slice((idx % chunks_per_seq) * CHUNK, (idx % chunks_per_seq) * CHUNK + n_keys)
        qs = [jnp.concatenate(
            [q_scr[q_rows(idx), (kvh * C_GROUP + g) * C_HDIM:(kvh * C_GROUP + g + 1) * C_HDIM]
             for g in range(C_GROUP)], axis=0).astype(BF16) for idx, kvh in units]
        kh = [k_scr[idx // chunks_per_seq, key_rows(idx), head_lanes(kvh)].astype(BF16) for idx, kvh in units]
        vh = [v_scr[idx // chunks_per_seq, key_rows(idx), head_lanes(kvh)].astype(BF16) for idx, kvh in units]
        sc = [_dot_nt(qn, kn) for qn, kn in zip(qs, kh)]
        if not past_valid:
            sc = [jnp.where((key_idx + (idx % chunks_per_seq) * CHUNK >= WINDOW) | (j > 0), sn, -jnp.inf)
                  for sn, (idx, kvh) in zip(sc, units)]
        m = [jnp.maximum(jnp.max(sn, axis=-1, keepdims=True), head_sinks[kvh]) for sn, (idx, kvh) in zip(sc, units)]
        p = [jnp.exp(sn - mn).astype(BF16) for sn, mn in zip(sc, m)]
        den = [_dot(pn, ones_keys) + jnp.exp(head_sinks[kvh] - mn) for pn, mn, (idx, kvh) in zip(p, m, units)]
        o = [_dot(pn, vn) / dn for pn, vn, dn in zip(p, vh, den)]
        for on, (idx, kvh) in zip(o, units):
            for g in range(C_GROUP):
                hs = slice((kvh * C_GROUP + g) * C_HDIM, (kvh * C_GROUP + g + 1) * C_HDIM)
                a_scr[q_rows(idx), hs] = on[g * CHUNK:(g + 1) * CHUNK]

    n_chunks = rows // CHUNK
    for first in range(0, n_chunks, SWA_STAGE_CHUNKS):
        attend(range(first, min(first + SWA_STAGE_CHUNKS, n_chunks)))
    y_ref[...] = x + _dot(a_scr[...].astype(BF16), wout_ref[...])


def _swa(x, pk, pv, cos, sin_signed, sinks, consts, *, n_batch, tiles_per_batch, n_seq, seq_rows, past_valid):
    rows = n_seq * seq_rows
    tail = min(WINDOW, seq_rows)
    n_str = n_batch * n_seq
    whole = lambda *shape: pl.BlockSpec(shape, lambda b, j, sk: (0,) * len(shape))
    once = lambda *shape: pl.BlockSpec(shape, lambda b, j, sk: (0,) * len(shape), pipeline_mode=pl.Buffered(1))
    in_specs = [pl.BlockSpec((rows, D_MODEL), lambda b, j, sk: (b * tiles_per_batch + j, 0)),
                pl.BlockSpec((n_seq, WINDOW, C_KV_DIM), lambda b, j, sk: (b, 0, 0)),
                pl.BlockSpec((n_seq, WINDOW, C_KV_DIM), lambda b, j, sk: (b, 0, 0)),
                pl.BlockSpec((rows, LANES), lambda b, j, sk: (j, 0)),
                pl.BlockSpec((rows, LANES), lambda b, j, sk: (j, 0)),
                whole(1, D_MODEL), once(D_MODEL, ODD_IN), once(D_MODEL, D_MODEL),
                whole(1, D_MODEL), whole(1, C_KV_DIM), whole(D_MODEL, D_MODEL)]
    args = [x, pk, pv, cos, sin_signed] + list(consts)
    kv_out = jax.ShapeDtypeStruct((n_str, tail, C_KV_DIM), F32)
    kv_spec = pl.BlockSpec((n_seq, tail, C_KV_DIM), lambda b, j, sk: (b, 0, 0))
    grid_spec = pltpu.PrefetchScalarGridSpec(
        num_scalar_prefetch=1, grid=(n_batch, tiles_per_batch), in_specs=in_specs,
        out_specs=[pl.BlockSpec((rows, D_MODEL), lambda b, j, sk: (b * tiles_per_batch + j, 0)),
                   kv_spec, kv_spec],
        scratch_shapes=[pltpu.VMEM((n_seq, WINDOW + seq_rows, C_KV_DIM), F32),
                        pltpu.VMEM((n_seq, WINDOW + seq_rows, C_KV_DIM), F32),
                        pltpu.VMEM((rows, D_MODEL), F32), pltpu.VMEM((rows, D_MODEL), F32),
                        pltpu.VMEM((D_MODEL, ODD_IN), BF16), pltpu.VMEM((D_MODEL, D_MODEL), BF16)])
    return pl.pallas_call(
        functools.partial(_swa_body, n_seq=n_seq, seq_rows=seq_rows, past_valid=past_valid),
        grid_spec=grid_spec,
        out_shape=[jax.ShapeDtypeStruct(x.shape, F32), kv_out, kv_out],
        compiler_params=_cparams(2),
        name="swa_s" if past_valid else "swa_p")(sinks, *args)


def _router_body(xp_ref, xs_ref, nrm_ref, rboth_ref, lstrict_ref, su_ref, info_ref, cnt_ref, *, n_prompt_tiles):
    def run(x_ref):
        h = _rms(x_ref[...], nrm_ref[...])
        hi = h.astype(BF16)
        lo = (h - hi.astype(F32)).astype(BF16)
        both = _dot(hi, rboth_ref[...])
        logits = both[:, 0:LANES] + both[:, LANES:2 * LANES] + _dot(lo, rboth_ref[:, 0:LANES])
        lane = lax.broadcasted_iota(jnp.int32, logits.shape, 1).astype(F32)
        logits = jnp.where(lane < N_EXPERTS, logits, -jnp.inf)
        l1 = jnp.max(logits, axis=-1, keepdims=True)
        i1 = jnp.min(jnp.where(logits == l1, lane, float(LANES)), axis=-1, keepdims=True)
        rest = jnp.where(lane == i1, -jnp.inf, logits)
        l2 = jnp.max(rest, axis=-1, keepdims=True)
        i2 = jnp.min(jnp.where(rest == l2, lane, float(LANES)), axis=-1, keepdims=True)
        e = jnp.exp(l2 - l1)
        g1 = 1.0 / (1.0 + e)
        g2 = e * g1
        oh1 = (lane == i1).astype(F32)
        oh2 = (lane == i2).astype(F32)
        cnt1 = jnp.sum(oh1, axis=0, keepdims=True)
        cnt2 = jnp.sum(oh2, axis=0, keepdims=True)
        seg = jnp.broadcast_to(jnp.ceil((cnt1 + cnt2) * (1.0 / GROUP)), (GROUP, LANES))
        off = _dot(seg.astype(BF16), su_ref[...])[0:1] * GROUP
        c1 = _dot(lstrict_ref[...], oh1.astype(BF16)) + off
        c2 = _dot(lstrict_ref[...], oh2.astype(BF16)) + off + cnt1
        d1 = jnp.sum(jnp.where(lane == i1, c1, 0.0), axis=-1, keepdims=True)
        d2 = jnp.sum(jnp.where(lane == i2, c2, 0.0), axis=-1, keepdims=True)
        info = jnp.zeros(logits.shape, F32)
        for n, col in enumerate([d1, d2, g1, g2]):
            info = jnp.where(lane == n, col, info)
        info_ref[...] = info
        cnt_ref[...] = seg * GROUP

    _on_row_source(pl.program_id(0), n_prompt_tiles, run, (xp_ref,), (xs_ref,))


def _router(xp, xs, nrm, rboth, lstrict, su):
    n_prompt_tiles = xp.shape[0] // ROW_TILE
    n_tiles = n_prompt_tiles + xs.shape[0] // ROW_TILE
    whole = lambda *shape: pl.BlockSpec(shape, lambda i: (0,) * len(shape))
    return pl.pallas_call(
        functools.partial(_router_body, n_prompt_tiles=n_prompt_tiles), grid=(n_tiles,),
        in_specs=[*_two_source_specs(n_prompt_tiles), whole(1, D_MODEL), whole(D_MODEL, 2 * LANES),
                  whole(ROW_TILE, ROW_TILE), whole(LANES, LANES)],
        out_specs=[pl.BlockSpec((ROW_TILE, LANES), lambda i: (i, 0)), pl.BlockSpec((GROUP, LANES), lambda i: (i, 0))],
        out_shape=[jax.ShapeDtypeStruct((n_tiles * ROW_TILE, LANES), F32),
                   jax.ShapeDtypeStruct((n_tiles * GROUP, LANES), F32)],
        compiler_params=_cparams(1), name="router")(xp, xs, nrm, rboth, lstrict, su)


BIG_ROWS = 8 * GROUP
SEG_FIELDS = 4
TILE_FIELDS = 2
TAIL_FIELDS = 3


def _rows_copy(src_ref, src_row, dst_ref, dst_row, n_rows, sem):
    return pltpu.make_async_copy(src_ref.at[pl.ds(pl.multiple_of(src_row, GROUP), n_rows)],
                                 dst_ref.at[pl.ds(pl.multiple_of(dst_row, GROUP), n_rows)], sem)


def _start_pieces(src_ref, src0, dst_ref, dst0, n_big, n_small, sem):
    def big(g, carry):
        _rows_copy(src_ref, src0 + g * BIG_ROWS, dst_ref, dst0 + g * BIG_ROWS, BIG_ROWS, sem).start()
        return carry

    lax.fori_loop(0, n_big, big, 0)
    done = n_big * BIG_ROWS

    def small(g, carry):
        _rows_copy(src_ref, src0 + done + g * GROUP, dst_ref, dst0 + done + g * GROUP, GROUP, sem).start()
        return carry

    lax.fori_loop(0, n_small, small, 0)


def _wait_pieces(n_big, n_small, src_ref, dst_ref, sem):
    def big(g, carry):
        _rows_copy(src_ref, 0, dst_ref, 0, BIG_ROWS, sem).wait()
        return carry

    lax.fori_loop(0, n_big, big, 0)

    def small(g, carry):
        _rows_copy(src_ref, 0, dst_ref, 0, GROUP, sem).wait()
        return carry

    lax.fori_loop(0, n_small, small, 0)


def _segment_table(seg, n_sorted_tiles):
    pieces = lambda rows: (rows // BIG_ROWS, (rows % BIG_ROWS) // GROUP)
    local_off = jnp.cumsum(seg, axis=1) - seg
    expert_rows = jnp.sum(seg, axis=0)
    expert_pad = ((expert_rows + ROW_TILE - 1) // ROW_TILE) * ROW_TILE
    expert_end = jnp.cumsum(expert_pad)
    expert_off = expert_end - expert_pad
    seg_start = expert_off[None, :] + jnp.cumsum(seg, axis=0) - seg
    n_used = (expert_end[-1] // ROW_TILE).reshape(1)
    tile_ids = jnp.arange(n_sorted_tiles, dtype=jnp.int32)
    tile_expert = jnp.minimum(jnp.sum(tile_ids[:, None] >= (expert_end // ROW_TILE)[None, :], axis=1), N_EXPERTS - 1)
    tile_expert = jnp.where(tile_ids < n_used[0], tile_expert, tile_expert[n_used[0] - 1])
    seg_big, seg_small = pieces(seg)
    tail_big, tail_small = pieces(expert_pad - expert_rows)
    table = jnp.concatenate([
        jnp.stack([seg_start, seg_big, seg_small, local_off], axis=-1).reshape(-1),
        jnp.stack([jnp.sum(seg_big, axis=1), jnp.sum(seg_small, axis=1)], axis=-1).reshape(-1),
        jnp.stack([expert_off + expert_rows, tail_big, tail_small], axis=-1).reshape(-1),
        n_used])
    return table.astype(jnp.int32), tile_expert.astype(jnp.int32), n_used.astype(jnp.int32)


def _table_sections(n_tiles):
    tiles0 = SEG_FIELDS * n_tiles * N_EXPERTS
    tails0 = tiles0 + TILE_FIELDS * n_tiles
    return tiles0, tails0, tails0 + TAIL_FIELDS * N_EXPERTS


def _dispatch_body(tab_ref, info_ref, xp_ref, xs_ref, nrm_ref, sel_ref, out_ref, loc_scr, z_scr, sems, *,
                   n_prompt_tiles, n_tiles, n_sorted_tiles):
    i = pl.program_id(0)
    slot = i % 2
    tiles0, tails0, used0 = _table_sections(n_tiles)

    def wait_tile(t, s):
        _wait_pieces(tab_ref[tiles0 + TILE_FIELDS * t], tab_ref[tiles0 + TILE_FIELDS * t + 1], loc_scr.at[s], out_ref,
                     sems.at[s])

    @pl.when(i >= 2)
    def _():
        wait_tile(i - 2, slot)

    def run(x_ref):
        hb = _rms(x_ref[...], nrm_ref[...]).astype(BF16)
        info = info_ref[...]
        lane = lax.broadcasted_iota(jnp.int32, info.shape, 1)
        dest = jnp.where(lane < TOP_K, info, 0.0)
        drow = sum(_dot_nt(sel_ref[...], part) for part in _split3(dest))
        r = lax.broadcasted_iota(jnp.int32, (LOCAL_ROWS, ROW_TILE), 0).astype(F32)
        perm = jnp.where(r == drow[0:1], 1.0, jnp.where(r == drow[1:2], 1.0, 0.0)).astype(BF16)
        loc_scr[slot] = _dot(perm, hb)

    _on_row_source(i, n_prompt_tiles, run, (xp_ref,), (xs_ref,))

    for e in range(N_EXPERTS):
        base = SEG_FIELDS * (i * N_EXPERTS + e)
        _start_pieces(loc_scr.at[slot], tab_ref[base + 3], out_ref, tab_ref[base], tab_ref[base + 1],
                      tab_ref[base + 2], sems.at[slot])

    @pl.when(i == n_tiles - 1)
    def _():
        wait_tile(i, slot)
        if n_tiles > 1:
            wait_tile(i - 1, 1 - slot)
        z_scr[...] = jnp.zeros_like(z_scr)
        for e in range(N_EXPERTS):
            base = tails0 + TAIL_FIELDS * e
            _start_pieces(z_scr, 0, out_ref, tab_ref[base], tab_ref[base + 1], tab_ref[base + 2], sems.at[2])
            _wait_pieces(tab_ref[base + 1], tab_ref[base + 2], z_scr, out_ref, sems.at[2])

        def zero_tile(t, carry):
            cp = _rows_copy(z_scr, 0, out_ref, t * ROW_TILE, ROW_TILE, sems.at[2])
            cp.start()
            cp.wait()
            return carry

        lax.fori_loop(tab_ref[used0], n_sorted_tiles, zero_tile, 0)


def _dispatch(table, info, xp, xs, nrm, sel, n_sorted_tiles):
    n_prompt_tiles = xp.shape[0] // ROW_TILE
    n_tiles = n_prompt_tiles + xs.shape[0] // ROW_TILE
    whole = lambda *shape: pl.BlockSpec(shape, lambda i, tab: (0,) * len(shape))
    grid_spec = pltpu.PrefetchScalarGridSpec(
        num_scalar_prefetch=1, grid=(n_tiles,),
        in_specs=[pl.BlockSpec((ROW_TILE, LANES), lambda i, tab: (i, 0)), *_two_source_specs(n_prompt_tiles),
                  whole(1, D_MODEL), whole(GROUP, LANES)],
        out_specs=pl.BlockSpec(memory_space=pl.ANY),
        scratch_shapes=[pltpu.VMEM((2, LOCAL_ROWS, D_MODEL), F32), pltpu.VMEM((ROW_TILE, D_MODEL), F32),
                        pltpu.SemaphoreType.DMA((3,))])
    return pl.pallas_call(
        functools.partial(_dispatch_body, n_prompt_tiles=n_prompt_tiles, n_tiles=n_tiles,
                          n_sorted_tiles=n_sorted_tiles),
        grid_spec=grid_spec, out_shape=jax.ShapeDtypeStruct((n_sorted_tiles * ROW_TILE, D_MODEL), F32),
        compiler_params=_cparams(1), name="moe_dispatch")(table, info, xp, xs, nrm, sel)


def _combine_body(tab_ref, info_ref, xp_ref, xs_ref, e_ref, yp_ref, ys_ref, loc_scr, sems, *, n_prompt_tiles,
                  n_tiles):
    i = pl.program_id(0)
    slot = i % 2
    tiles0, _, _ = _table_sections(n_tiles)

    def fetch(t, s):
        for e in range(N_EXPERTS):
            base = SEG_FIELDS * (t * N_EXPERTS + e)
            _start_pieces(e_ref, tab_ref[base], loc_scr.at[s], tab_ref[base + 3], tab_ref[base + 1],
                          tab_ref[base + 2], sems.at[s])

    @pl.when(i == 0)
    def _():
        loc_scr[...] = jnp.zeros_like(loc_scr)
        fetch(0, 0)

    @pl.when(i + 1 < n_tiles)
    def _():
        fetch(i + 1, 1 - slot)

    _wait_pieces(tab_ref[tiles0 + TILE_FIELDS * i], tab_ref[tiles0 + TILE_FIELDS * i + 1], e_ref, loc_scr.at[slot],
                 sems.at[slot])

    def run(x_ref, y_ref):
        info = info_ref[...]
        eb = loc_scr[slot].astype(BF16)
        lane = lax.broadcasted_iota(jnp.int32, (ROW_TILE, LOCAL_ROWS), 1).astype(F32)
        pick = jnp.zeros((ROW_TILE, LOCAL_ROWS), F32)
        for k in range(TOP_K):
            pick = jnp.where(lane == info[:, k:k + 1], info[:, TOP_K + k:TOP_K + k + 1], pick)
        y_ref[...] = x_ref[...] + _dot(pick.astype(BF16), eb)

    _on_row_source(i, n_prompt_tiles, run, (xp_ref, yp_ref), (xs_ref, ys_ref))


def _combine(table, info, xp, xs, esorted):
    n_prompt_tiles = xp.shape[0] // ROW_TILE
    n_tiles = n_prompt_tiles + xs.shape[0] // ROW_TILE
    grid_spec = pltpu.PrefetchScalarGridSpec(
        num_scalar_prefetch=1, grid=(n_tiles,),
        in_specs=[pl.BlockSpec((ROW_TILE, LANES), lambda i, tab: (i, 0)), *_two_source_specs(n_prompt_tiles),
                  pl.BlockSpec(memory_space=pl.ANY)],
        out_specs=list(_two_source_specs(n_prompt_tiles)),
        scratch_shapes=[pltpu.VMEM((2, LOCAL_ROWS, D_MODEL), F32), pltpu.SemaphoreType.DMA((2,))])
    return pl.pallas_call(
        functools.partial(_combine_body, n_prompt_tiles=n_prompt_tiles, n_tiles=n_tiles),
        grid_spec=grid_spec,
        out_shape=[jax.ShapeDtypeStruct(xp.shape, F32), jax.ShapeDtypeStruct(xs.shape, F32)],
        compiler_params=_cparams(1), name="moe_combine")(table, info, xp, xs, esorted)


def _rope_tables(pos):
    half = C_HDIM // 2
    inv = 1.0 / (ROPE_THETA ** (jnp.arange(half, dtype=F32) / half))
    ang = pos.astype(F32)[:, None] * inv[None, :]
    cos = jnp.cos(ang)
    sin = jnp.sin(ang)
    reps = LANES // C_HDIM
    return jnp.tile(jnp.concatenate([cos, cos], axis=-1), (1, reps)), jnp.tile(jnp.concatenate([-sin, sin], axis=-1),
                                                                                 (1, reps))


def kernel(x_prompt, x_sample, mem_prompt, cache_mem_k, cache_mem_v, state_hgrn, cache_swa_k, cache_swa_v, norm_mix, norm_xattn, norm_ffn, even_w_in, even_w_out, gmlp_w_s, gmlp_b_s, gmlp_ln_g, gmlp_ln_b, hgrn_lb_logits, hgrn_out_norm, attn_w_in, attn_w_out, attn_q_norm, attn_k_norm, attn_sinks, xattn_mem_norm, xattn_w_q, xattn_w_k, xattn_w_v, xattn_w_o, xattn_q_norm, xattn_k_norm, ffn_w1, ffn_w3, ffn_w2, moe_router, moe_w1, moe_w3, moe_w2):
    n_batch, seq, d = x_prompt.shape
    dec_batch, dec_seq, _ = x_sample.shape
    n_mem = mem_prompt.shape[1]
    depth = norm_mix.shape[0]
    past_len = PAST_LEN
    assert d == D_MODEL and depth == 2 and seq % ROW_TILE == 0 and dec_batch * dec_seq == ROW_TILE
    assert dec_seq == CHUNK and ROW_TILE % n_mem == 0 and cache_swa_k.shape[2] == WINDOW
    assert even_w_in.shape[-1] == EVEN_IN and attn_w_in.shape[-1] == ODD_IN and ffn_w1.shape[-1] == D_FF
    assert moe_w1.shape[1] == N_EXPERTS
    n_prompt_rows = n_batch * seq
    n_rows = n_prompt_rows + dec_batch * dec_seq
    n_prompt_tiles = n_prompt_rows // ROW_TILE
    tiles_per_batch = seq // ROW_TILE
    bf = lambda w: w.astype(BF16)
    row = lambda g: g.reshape(1, -1).astype(F32)

    lb_all = jnp.cumsum(jax.nn.softmax(hgrn_lb_logits.astype(F32), axis=0), axis=0)
    wcum = jnp.asarray(_cumsum_matrix(), BF16)
    lmask = jnp.asarray(_level_masks(), F32)

    def even_consts(n):
        tril = jnp.tril(jnp.ones((n, n), bool))
        ws = jnp.where(tril[None], gmlp_w_s[0, :, :n, :n], 0.0).astype(BF16)
        bs = jnp.broadcast_to(gmlp_b_s[0, :, :n, None], (A_GROUPS, n, LANES)).astype(F32)
        return [row(norm_mix[0]), even_w_in[0], even_w_out[0], ws, bs,
                gmlp_ln_g[0].reshape(A_GROUPS, 1, A_GDIM), gmlp_ln_b[0].reshape(A_GROUPS, 1, A_GDIM),
                row(lb_all[0]), row(hgrn_out_norm[0]), wcum, lmask]

    xp, hgrn_p = _even_mixer(
        x_prompt.reshape(n_prompt_rows, d), jnp.zeros((n_batch, B_HEADS, B_HDIM, B_HDIM), F32),
        even_consts(A_CHUNK), n_batch=n_batch, tiles_per_batch=tiles_per_batch, n_seq=1, seq_rows=ROW_TILE,
        gchunk=A_CHUNK, emit_v=False)
    xs, hgrn_s, gmlp_v = _even_mixer(
        x_sample.reshape(ROW_TILE, d), state_hgrn[0], even_consts(min(A_CHUNK, dec_seq)), n_batch=1,
        tiles_per_batch=1, n_seq=dec_batch, seq_rows=dec_seq, gchunk=min(A_CHUNK, dec_seq), emit_v=True)

    mem_k, mem_v = _memory_kv(mem_prompt.reshape(n_batch * n_mem, d), xattn_mem_norm.reshape(depth, 1, d),
                              xattn_w_k, xattn_w_v, xattn_k_norm.reshape(depth, 1, X_HDIM),
                              n_batch=n_batch, n_mem=n_mem)

    def cross_attention(xp, xs, l):
        consts = (row(norm_xattn[l]), xattn_w_q[l], xattn_w_o[l], row(xattn_q_norm[l]))
        xp = _xattn(xp, mem_k, mem_v, *consts, layer=l, steps_per_mem=tiles_per_batch, n_seq=1, seq_rows=ROW_TILE)
        xs = _xattn(xs, cache_k, cache_v, *consts, layer=l, steps_per_mem=1, n_seq=dec_batch, seq_rows=dec_seq)
        return xp, xs

    cache_k = cache_mem_k.reshape(depth, dec_batch, n_mem, X_DIM)
    cache_v = cache_mem_v.reshape(depth, dec_batch, n_mem, X_DIM)

    xp, xs = cross_attention(xp, xs, 0)
    xp, xs = _dense_ffn(xp, xs, row(norm_ffn[0]), ffn_w1[0], ffn_w3[0], ffn_w2[0])

    reps = D_MODEL // C_HDIM
    bd = jnp.asarray(np.kron(np.eye(reps, dtype=np.float32), np.full((C_HDIM, C_HDIM), 1.0 / C_HDIM, np.float32)),
                     BF16)
    swa_consts = [row(norm_mix[1]), attn_w_in[0], attn_w_out[0], row(jnp.tile(attn_q_norm[0], C_HEADS)),
                  row(jnp.tile(attn_k_norm[0], C_KV_HEADS)), bd]
    sinks = attn_sinks[0].astype(F32)
    cos_p, sin_p = _rope_tables(jnp.arange(seq, dtype=jnp.int32))
    cos_s, sin_s = _rope_tables(past_len + jnp.arange(dec_seq, dtype=jnp.int32))
    no_past = jnp.zeros((n_batch, WINDOW, C_KV_DIM), F32)
    xp, swk_p, swv_p = _swa(xp, no_past, no_past, cos_p, sin_p, sinks, swa_consts, n_batch=n_batch,
                            tiles_per_batch=tiles_per_batch, n_seq=1, seq_rows=ROW_TILE, past_valid=False)
    xs, swk_s, swv_s = _swa(xs, cache_swa_k[0].reshape(dec_batch, WINDOW, C_KV_DIM),
                            cache_swa_v[0].reshape(dec_batch, WINDOW, C_KV_DIM),
                            jnp.tile(cos_s, (dec_batch, 1)), jnp.tile(sin_s, (dec_batch, 1)), sinks, swa_consts,
                            n_batch=1, tiles_per_batch=1, n_seq=dec_batch, seq_rows=dec_seq, past_valid=True)
    xp, xs = cross_attention(xp, xs, 1)

    n_tiles = n_rows // ROW_TILE
    router_w = jnp.zeros((d, LANES), F32).at[:, :N_EXPERTS].set(moe_router[0].astype(F32))
    rhi = router_w.astype(BF16)
    rlo = (router_w - rhi.astype(F32)).astype(BF16)
    lstrict = jnp.asarray(np.tril(np.ones((ROW_TILE, ROW_TILE), np.float32), -1), BF16)
    su = jnp.asarray(np.triu(np.ones((LANES, LANES), np.float32), 1), BF16)
    sel = jnp.asarray(np.eye(GROUP, LANES, dtype=np.float32), BF16)
    nrm_ffn = row(norm_ffn[1])
    info, seg = _router(xp, xs, nrm_ffn, jnp.concatenate([rhi, rlo], axis=1), lstrict, su)
    seg = seg.reshape(n_tiles, GROUP, LANES)[:, 0, :N_EXPERTS].astype(jnp.int32)
    n_sorted_tiles = -(-(TOP_K * n_rows + n_tiles * N_EXPERTS * (GROUP - 1) + N_EXPERTS * (ROW_TILE - 1)) // ROW_TILE)
    table, tile_expert, n_used = _segment_table(seg, n_sorted_tiles)
    hsorted = _dispatch(table, info, xp, xs, nrm_ffn, sel, n_sorted_tiles)
    esorted = _grouped_ffn(hsorted, tile_expert, n_used, bf(moe_w1[0]), bf(moe_w3[0]), bf(moe_w2[0]))
    y_prompt, y_sample = _combine(table, info, xp, xs, esorted)

    n_even = state_hgrn.shape[0]
    n_odd = cache_swa_k.shape[0]
    return (y_prompt.reshape(n_batch, seq, d), y_sample.reshape(dec_batch, dec_seq, d),
            mem_k.reshape(depth, n_batch, n_mem, X_HEADS, X_HDIM), mem_v.reshape(depth, n_batch, n_mem, X_HEADS, X_HDIM),
            hgrn_p.reshape(n_even, n_batch, B_HEADS, B_HDIM, B_HDIM),
            gmlp_v.reshape(n_even, dec_batch, dec_seq, A_GROUPS, A_GDIM),
            hgrn_s.reshape(n_even, dec_batch, B_HEADS, B_HDIM, B_HDIM),
            swk_p.reshape(n_odd, n_batch, WINDOW, C_KV_HEADS, C_HDIM), swv_p.reshape(n_odd, n_batch, WINDOW, C_KV_HEADS, C_HDIM),
            swk_s.reshape(n_odd, dec_batch, dec_seq, C_KV_HEADS, C_HDIM), swv_s.reshape(n_odd, dec_batch, dec_seq, C_KV_HEADS, C_HDIM))
```

```python
import functools

import numpy as np
import jax
import jax.numpy as jnp
from jax import lax
from jax.experimental import pallas as pl
from jax.experimental.pallas import tpu as pltpu

F32 = jnp.float32
BF16 = jnp.bfloat16

D_MODEL = 1024
EPS = 1e-6
LOG2_E = 1.4426950408889634
CHUNK = 64
A_GROUPS = 4
A_DIM = D_MODEL // 2
A_GDIM = A_DIM // A_GROUPS
A_CHUNK = 128
B_HEADS = 4
B_DIM = D_MODEL // 2
B_HDIM = B_DIM // B_HEADS
EVEN_IN = 2 * A_DIM + 4 * B_DIM
C_HEADS = 16
C_KV_HEADS = 4
C_HDIM = D_MODEL // C_HEADS
C_GROUP = C_HEADS // C_KV_HEADS
C_KV_DIM = C_KV_HEADS * C_HDIM
WINDOW = 128
ROPE_THETA = 10000.0
PAST_LEN = 4096
ODD_IN = (C_HEADS + 2 * C_KV_HEADS) * C_HDIM
X_HEADS = 4
X_HDIM = 128
X_DIM = X_HEADS * X_HDIM
D_FF = 2816
N_EXPERTS = 8
TOP_K = 2

LANES = 128
ROW_TILE = 512
FF_SUB = 512
N_LEVELS = 6
HGRN_STAGE_CHUNKS = 2
SWA_STAGE_CHUNKS = 2
GROUP = 8
LOCAL_ROWS = -(-(TOP_K * ROW_TILE + N_EXPERTS * (GROUP - 1)) // LANES) * LANES
VMEM_LIMIT = 58 * 1024 * 1024


def _cparams(n_axes):
    return pltpu.CompilerParams(dimension_semantics=("arbitrary",) * n_axes, vmem_limit_bytes=VMEM_LIMIT)


def _dot(a, b):
    return jnp.dot(a, b, preferred_element_type=F32)


def _dot_nt(a, b):
    return lax.dot_general(a, b, (((1,), (1,)), ((), ())), preferred_element_type=F32)


def _rms(x, g):
    return x * lax.rsqrt(jnp.mean(x * x, axis=-1, keepdims=True) + EPS) * g


def _silu(x):
    return x * (1.0 / (1.0 + jnp.exp(-x)))


def _split3(x):
    hi = x.astype(BF16)
    r1 = x - hi.astype(F32)
    mid = r1.astype(BF16)
    lo = (r1 - mid.astype(F32)).astype(BF16)
    return hi, mid, lo


def _cumsum_matrix():
    r = np.arange(CHUNK)
    s = np.arange(CHUNK)
    blocks = [(s[None, :] <= r[:, None])]
    for l in range(N_LEVELS):
        h = 1 << l
        ref = (r & ~(2 * h - 1)) + h - 1
        blocks.append(s[None, :] <= ref[:, None])
    w = np.concatenate(blocks, axis=0).astype(np.float32)
    return np.concatenate([w, w, w], axis=1)


def _level_masks():
    t = np.arange(CHUNK)[:, None]
    s = np.arange(CHUNK)[None, :]
    masks = []
    for l in range(N_LEVELS):
        masks.append(((t >> (l + 1)) == (s >> (l + 1))) & (((t >> l) & 1) == 1) & (((s >> l) & 1) == 0))
    masks.append(t == s)
    return np.stack(masks).astype(np.float32)


def _even_mixer_body(x_ref, s0_ref, nrm_ref, win32_ref, wout32_ref, ws_ref, bs_ref, lng_ref, lnb_ref, lb_ref, og_ref,
                     wcum_ref, lmask_ref, *rest, n_seq, seq_rows, gchunk, emit_v):
    if emit_v:
        y_ref, sout_ref, v_ref, proj_scr, mixed_scr, st_scr, win_ref, wout_ref = rest
    else:
        y_ref, sout_ref, proj_scr, mixed_scr, st_scr, win_ref, wout_ref = rest
        v_ref = None
    rows = n_seq * seq_rows
    j = pl.program_id(1)

    @pl.when((pl.program_id(0) == 0) & (j == 0))
    def _():
        win_ref[...] = win32_ref[...].astype(BF16)
        wout_ref[...] = wout32_ref[...].astype(BF16)

    @pl.when(j == 0)
    def _():
        for s in range(n_seq):
            for hd in range(B_HEADS):
                st_scr[s * B_HEADS + hd] = s0_ref[s, hd].T

    x = x_ref[...]
    h = _rms(x, nrm_ref[...]).astype(BF16)
    n_pieces = EVEN_IN // ROW_TILE
    for n in range(n_pieces):
        cs = slice(n * ROW_TILE, (n + 1) * ROW_TILE)
        proj_scr[:, cs] = _dot(h, win_ref[:, cs])

    groups = range(A_GROUPS)
    group_cols = lambda base, g: slice(base + g * A_GDIM, base + (g + 1) * A_GDIM)
    vgs = [jax.nn.gelu(proj_scr[:, group_cols(A_DIM, g)]) for g in groups]
    means = [jnp.mean(vg, axis=-1, keepdims=True) for vg in vgs]
    vcs = [vg - mu for vg, mu in zip(vgs, means)]
    variances = [jnp.mean(vc * vc, axis=-1, keepdims=True) for vc in vcs]
    for g in groups:
        gs = group_cols(0, g)
        vn = vcs[g] * lax.rsqrt(variances[g] + EPS) * lng_ref[g] + lnb_ref[g]
        if v_ref is not None:
            v_ref[:, gs] = vn
        ug = jax.nn.gelu(proj_scr[:, gs])
        vb = vn.astype(BF16)
        for c in range(rows // gchunk):
            rs = slice(c * gchunk, (c + 1) * gchunk)
            sp = _dot(ws_ref[g], vb[rs]) + bs_ref[g]
            mixed_scr[rs, gs] = ug[rs] * sp

    q0, f0, i0, g0 = (2 * A_DIM + k * B_DIM for k in range(4))
    lb = lb_ref[...]
    fg = lb + (1.0 - lb) * jax.nn.sigmoid(proj_scr[:, f0:f0 + B_DIM])
    proj_scr[:, 0:B_DIM] = jnp.log(fg) * LOG2_E
    proj_scr[:, B_DIM:2 * B_DIM] = 1.0 - fg
    proj_scr[:, q0:q0 + B_DIM] = _silu(proj_scr[:, q0:q0 + B_DIM])
    og = og_ref[...]
    chunks_per_seq = seq_rows // CHUNK

    heads = range(B_HEADS)
    head_cols = lambda base, hd: slice(base + hd * B_HDIM, base + (hd + 1) * B_HDIM)
    chunk_rows = lambda c: slice(c * CHUNK, (c + 1) * CHUNK)

    def chunk_local(c):
        rs = chunk_rows(c)
        hi, mid, lo = _split3(proj_scr[rs, 0:B_DIM])
        gg = _dot(wcum_ref[...], jnp.concatenate([hi, mid, lo], axis=0))
        vs = [proj_scr[rs, head_cols(i0, hd)] for hd in heads]
        vts = [v.T.astype(BF16) for v in vs]
        Gs = [gg[0:CHUNK, head_cols(0, hd)] for hd in heads]
        qs = [proj_scr[rs, head_cols(q0, hd)] for hd in heads]
        ks = [proj_scr[rs, head_cols(B_DIM, hd)] for hd in heads]
        operands = []
        for hd in heads:
            for l in range(N_LEVELS + 1):
                if l < N_LEVELS:
                    e = jnp.exp2(-jnp.abs(Gs[hd] - gg[(l + 1) * CHUNK:(l + 2) * CHUNK, head_cols(0, hd)]))
                    operands.append(((qs[hd] * e).astype(BF16), (ks[hd] * e).astype(BF16)))
                else:
                    operands.append((qs[hd].astype(BF16), ks[hd].astype(BF16)))
        blocks = [_dot_nt(qe, ke) for qe, ke in operands]
        parts = []
        for hd in heads:
            att = jnp.zeros((CHUNK, CHUNK), F32)
            for l in range(N_LEVELS + 1):
                att = jnp.where(lmask_ref[l] > 0.5, blocks[hd * (N_LEVELS + 1) + l], att)
            g_end = Gs[hd][CHUNK - 1:CHUNK, :]
            kd = (ks[hd] * jnp.exp2(g_end - Gs[hd])).astype(BF16)
            parts.append((_dot(att.astype(BF16), vs[hd].astype(BF16)), (qs[hd] * jnp.exp2(Gs[hd])).astype(BF16),
                          jnp.exp2(g_end), _dot(vts[hd], kd)))
        return parts

    def chunk_state(c, parts):
        rs = chunk_rows(c)
        sidx = (c // chunks_per_seq) * B_HEADS
        outs = []
        for hd in heads:
            o_local, q_decayed, decay, increment = parts[hd]
            st = st_scr[sidx + hd]
            outs.append(o_local + _dot_nt(q_decayed, st.astype(BF16)))
            st_scr[sidx + hd] = st * decay + increment
        for hd in heads:
            o = outs[hd]
            on = o * lax.rsqrt(jnp.mean(o * o, axis=-1, keepdims=True) + EPS) * og
            gate = _silu(proj_scr[rs, head_cols(g0, hd)])
            mixed_scr[rs, head_cols(A_DIM, hd)] = on * gate

    n_chunks = rows // CHUNK
    for first in range(0, n_chunks, HGRN_STAGE_CHUNKS):
        group = range(first, min(first + HGRN_STAGE_CHUNKS, n_chunks))
        local = [chunk_local(c) for c in group]
        for c, parts in zip(group, local):
            chunk_state(c, parts)

    y_ref[...] = x + _dot(mixed_scr[...].astype(BF16), wout_ref[...])

    @pl.when(j == pl.num_programs(1) - 1)
    def _():
        for s in range(n_seq):
            for hd in range(B_HEADS):
                sout_ref[s, hd] = st_scr[s * B_HEADS + hd].T


def _even_mixer(x, s0, consts, *, n_batch, tiles_per_batch, n_seq, seq_rows, gchunk, emit_v):
    rows = n_seq * seq_rows
    grid = (n_batch, tiles_per_batch)
    tile = lambda b, j: (b * tiles_per_batch + j, 0)
    whole = lambda *shape: pl.BlockSpec(shape, lambda b, j: (0,) * len(shape))
    once = lambda *shape: pl.BlockSpec(shape, lambda b, j: (0,) * len(shape), pipeline_mode=pl.Buffered(1))
    in_specs = [
        pl.BlockSpec((rows, D_MODEL), tile),
        pl.BlockSpec((n_seq, B_HEADS, B_HDIM, B_HDIM), lambda b, j: (b, 0, 0, 0)),
        whole(1, D_MODEL), once(D_MODEL, EVEN_IN), once(D_MODEL, D_MODEL),
        whole(A_GROUPS, gchunk, gchunk), whole(A_GROUPS, gchunk, LANES),
        whole(A_GROUPS, 1, A_GDIM), whole(A_GROUPS, 1, A_GDIM), whole(1, B_DIM), whole(1, B_HDIM),
        whole((N_LEVELS + 1) * CHUNK, 3 * CHUNK), whole(N_LEVELS + 1, CHUNK, CHUNK),
    ]
    args = [x, s0] + list(consts)
    out_shape = [jax.ShapeDtypeStruct(x.shape, F32),
                 jax.ShapeDtypeStruct((n_batch * n_seq, B_HEADS, B_HDIM, B_HDIM), F32)]
    out_specs = [pl.BlockSpec((rows, D_MODEL), tile),
                 pl.BlockSpec((n_seq, B_HEADS, B_HDIM, B_HDIM), lambda b, j: (b, 0, 0, 0))]
    if emit_v:
        out_shape.append(jax.ShapeDtypeStruct((x.shape[0], A_DIM), F32))
        out_specs.append(pl.BlockSpec((rows, A_DIM), tile))
    body = functools.partial(_even_mixer_body, n_seq=n_seq, seq_rows=seq_rows, gchunk=gchunk, emit_v=emit_v)
    return pl.pallas_call(
        body, grid=grid, in_specs=in_specs, out_specs=out_specs, out_shape=out_shape,
        scratch_shapes=[pltpu.VMEM((rows, EVEN_IN), F32), pltpu.VMEM((rows, D_MODEL), F32),
                        pltpu.VMEM((n_seq * B_HEADS, B_HDIM, B_HDIM), F32),
                        pltpu.VMEM((D_MODEL, EVEN_IN), BF16), pltpu.VMEM((D_MODEL, D_MODEL), BF16)],
        compiler_params=_cparams(2),
        name="even_mixer_s" if emit_v else "even_mixer_p")(*args)


def _memory_kv_body(mem_ref, gm_ref, wk_ref, wv_ref, kg_ref, mk_ref, mv_ref, *, n_seq, n_mem):
    m = _rms(mem_ref[...], gm_ref[0]).astype(BF16)
    kk = _dot(m, wk_ref[0].astype(BF16))
    vv = _dot(m, wv_ref[0].astype(BF16))
    kg = kg_ref[0]
    for hd in range(X_HEADS):
        hs = slice(hd * X_HDIM, (hd + 1) * X_HDIM)
        kh = _rms(kk[:, hs], kg)
        for s in range(n_seq):
            mk_ref[0, s, pl.ds(hd, n_mem, stride=X_HEADS), :] = kh[s * n_mem:(s + 1) * n_mem]
            mv_ref[0, s, pl.ds(hd, n_mem, stride=X_HEADS), :] = vv[s * n_mem:(s + 1) * n_mem, hs]


def _memory_kv(mem2d, g_mem, w_k, w_v, k_g, *, n_batch, n_mem):
    depth = w_k.shape[0]
    n_seq = ROW_TILE // n_mem
    out = jax.ShapeDtypeStruct((depth, n_batch, n_mem * X_HEADS, X_HDIM), F32)
    ospec = pl.BlockSpec((1, n_seq, n_mem * X_HEADS, X_HDIM), lambda l, t: (l, t, 0, 0))
    return pl.pallas_call(
        functools.partial(_memory_kv_body, n_seq=n_seq, n_mem=n_mem),
        grid=(depth, n_batch // n_seq),
        in_specs=[pl.BlockSpec((ROW_TILE, D_MODEL), lambda l, t: (t, 0)),
                  pl.BlockSpec((1, 1, D_MODEL), lambda l, t: (l, 0, 0)),
                  pl.BlockSpec((1, D_MODEL, X_DIM), lambda l, t: (l, 0, 0)),
                  pl.BlockSpec((1, D_MODEL, X_DIM), lambda l, t: (l, 0, 0)),
                  pl.BlockSpec((1, 1, X_HDIM), lambda l, t: (l, 0, 0))],
        out_specs=[ospec, ospec], out_shape=[out, out], compiler_params=_cparams(2),
        name="memory_kv")(mem2d, g_mem, w_k, w_v, k_g)


def _xattn_body(x_ref, mk_ref, mv_ref, nrm_ref, wq32_ref, wo32_ref, qg_ref, y_ref, o_scr, wq_ref, wo_ref, *, n_seq,
                seq_rows):
    @pl.when(pl.program_id(0) == 0)
    def _():
        wq_ref[...] = wq32_ref[...].astype(BF16)
        wo_ref[...] = wo32_ref[...].astype(BF16)

    x = x_ref[...]
    h = _rms(x, nrm_ref[...]).astype(BF16)
    q = _dot(h, wq_ref[...])
    qg = qg_ref[...] * (X_HDIM ** -0.5)
    units = [(s, hd) for s in range(n_seq) for hd in range(X_HEADS)]
    rows_of = lambda s: slice(s * seq_rows, (s + 1) * seq_rows)
    lanes_of = lambda hd: slice(hd * X_HDIM, (hd + 1) * X_HDIM)
    n_mem = mk_ref.shape[1] // X_HEADS
    head_rows = lambda hd: pl.ds(hd, n_mem, stride=X_HEADS)
    qh = [_rms(q[rows_of(s), lanes_of(hd)], qg).astype(BF16) for s, hd in units]
    sc = [_dot_nt(qh[n], mk_ref[s, head_rows(hd), :].astype(BF16)) for n, (s, hd) in enumerate(units)]
    m = [jnp.max(t, axis=-1, keepdims=True) for t in sc]
    p = [jnp.exp(t - mx) for t, mx in zip(sc, m)]
    den = [jnp.sum(t, axis=-1, keepdims=True) for t in p]
    for n, (s, hd) in enumerate(units):
        o_scr[rows_of(s), lanes_of(hd)] = _dot(p[n].astype(BF16), mv_ref[s, head_rows(hd), :].astype(BF16)) / den[n]
    y_ref[...] = x + _dot(o_scr[...].astype(BF16), wo_ref[...])


def _xattn(x, mk, mv, nrm, wq, wo, qg, *, layer, steps_per_mem, n_seq, seq_rows):
    rows = n_seq * seq_rows
    mem_spec = pl.BlockSpec((None, n_seq) + mk.shape[2:], lambda t: (layer, t // steps_per_mem, 0, 0))
    whole = lambda *shape: pl.BlockSpec(shape, lambda t: (0,) * len(shape))
    once = lambda *shape: pl.BlockSpec(shape, lambda t: (0,) * len(shape), pipeline_mode=pl.Buffered(1))
    return pl.pallas_call(
        functools.partial(_xattn_body, n_seq=n_seq, seq_rows=seq_rows),
        grid=(x.shape[0] // rows,),
        in_specs=[pl.BlockSpec((rows, D_MODEL), lambda t: (t, 0)), mem_spec, mem_spec,
                  whole(1, D_MODEL), once(D_MODEL, X_DIM), once(X_DIM, D_MODEL), whole(1, X_HDIM)],
        out_specs=pl.BlockSpec((rows, D_MODEL), lambda t: (t, 0)),
        out_shape=jax.ShapeDtypeStruct(x.shape, F32),
        scratch_shapes=[pltpu.VMEM((rows, X_DIM), F32), pltpu.VMEM((D_MODEL, X_DIM), BF16),
                        pltpu.VMEM((X_DIM, D_MODEL), BF16)],
        compiler_params=_cparams(1), name="xattn_s" if n_seq > 1 else "xattn_p")(x, mk, mv, nrm, wq, wo, qg)


def _two_source_specs(n_prompt_tiles, width=D_MODEL):
    return (pl.BlockSpec((ROW_TILE, width), lambda i, *_: (jnp.minimum(i, n_prompt_tiles - 1), 0)),
            pl.BlockSpec((ROW_TILE, width), lambda i, *_: (jnp.maximum(i - n_prompt_tiles, 0), 0)))


def _on_row_source(i, n_prompt_tiles, fn, prompt_refs, sample_refs):
    pl.when(i < n_prompt_tiles)(lambda: fn(*prompt_refs))
    pl.when(i >= n_prompt_tiles)(lambda: fn(*sample_refs))


def _swiglu_part(hb, w1, w3, w2):
    part = None
    for c0 in range(0, D_FF, FF_SUB):
        cs = slice(c0, min(c0 + FF_SUB, D_FF))
        act = (_silu(_dot(hb, w1[:, cs])) * _dot(hb, w3[:, cs])).astype(BF16)
        p = _dot(act, w2[cs, :])
        part = p if part is None else part + p
    return part


W_IN_STAGE_ROWS = 128
W_OUT_STAGE_ROWS = 256


def _load_weights_as_bf16(w1_hbm, w3_hbm, w2_hbm, w1_scr, w3_scr, w2_scr, in_stage, out_stage, sems):
    jobs = ([(w_hbm, w_scr, in_stage, r0, W_IN_STAGE_ROWS) for w_hbm, w_scr in ((w1_hbm, w1_scr), (w3_hbm, w3_scr))
             for r0 in range(0, D_MODEL, W_IN_STAGE_ROWS)]
            + [(w2_hbm, w2_scr, out_stage, r0, W_OUT_STAGE_ROWS) for r0 in range(0, D_FF, W_OUT_STAGE_ROWS)])

    def copy(n):
        w_hbm, _, stage, r0, n_rows = jobs[n]
        return pltpu.make_async_copy(w_hbm.at[pl.ds(r0, n_rows)], stage.at[n % 2], sems.at[n % 2])

    copy(0).start()
    for n in range(len(jobs)):
        if n + 1 < len(jobs):
            copy(n + 1).start()
        copy(n).wait()
        _, w_scr, stage, r0, n_rows = jobs[n]
        w_scr[pl.ds(r0, n_rows), :] = stage[n % 2].astype(BF16)


def _dense_ffn_body(xp_ref, xs_ref, nrm_ref, w1_hbm, w3_hbm, w2_hbm, yp_ref, ys_ref, w1_scr, w3_scr, w2_scr,
                    in_stage, out_stage, sems, *, n_prompt_tiles):
    i = pl.program_id(0)

    @pl.when(i == 0)
    def _():
        _load_weights_as_bf16(w1_hbm, w3_hbm, w2_hbm, w1_scr, w3_scr, w2_scr, in_stage, out_stage, sems)

    def run(x_ref, y_ref):
        x = x_ref[...]
        y_ref[...] = x + _swiglu_part(_rms(x, nrm_ref[...]).astype(BF16), w1_scr, w3_scr, w2_scr)

    _on_row_source(i, n_prompt_tiles, run, (xp_ref, yp_ref), (xs_ref, ys_ref))


def _dense_ffn(xp, xs, nrm, w1, w3, w2):
    n_prompt_tiles = xp.shape[0] // ROW_TILE
    n_tiles = n_prompt_tiles + xs.shape[0] // ROW_TILE
    in_hbm = pl.BlockSpec(memory_space=pl.ANY)
    return pl.pallas_call(
        functools.partial(_dense_ffn_body, n_prompt_tiles=n_prompt_tiles), grid=(n_tiles,),
        in_specs=[*_two_source_specs(n_prompt_tiles), pl.BlockSpec((1, D_MODEL), lambda i: (0, 0)),
                  in_hbm, in_hbm, in_hbm],
        out_specs=list(_two_source_specs(n_prompt_tiles)),
        out_shape=[jax.ShapeDtypeStruct(xp.shape, F32), jax.ShapeDtypeStruct(xs.shape, F32)],
        scratch_shapes=[pltpu.VMEM((D_MODEL, D_FF), BF16), pltpu.VMEM((D_MODEL, D_FF), BF16),
                        pltpu.VMEM((D_FF, D_MODEL), BF16), pltpu.VMEM((2, W_IN_STAGE_ROWS, D_FF), F32),
                        pltpu.VMEM((2, W_OUT_STAGE_ROWS, D_MODEL), F32), pltpu.SemaphoreType.DMA((2,))],
        compiler_params=_cparams(1), name="dense_ffn")(xp, xs, nrm, w1, w3, w2)


def _grouped_ffn_body(te_ref, tv_ref, x_ref, w1_ref, w3_ref, w2_ref, y_ref):
    valid = tv_ref[pl.program_id(0)]
    half = ROW_TILE // 2
    weights = (w1_ref.at[0], w3_ref.at[0], w2_ref.at[0])

    @pl.when(valid > half)
    def _():
        y_ref[...] = _swiglu_part(x_ref[...].astype(BF16), *weights)

    @pl.when((valid > 0) & (valid <= half))
    def _():
        y_ref[0:half] = _swiglu_part(x_ref[0:half].astype(BF16), *weights)
        y_ref[half:ROW_TILE] = jnp.zeros((ROW_TILE - half, D_MODEL), F32)

    @pl.when(valid == 0)
    def _():
        y_ref[...] = jnp.zeros_like(y_ref)


def _grouped_ffn(xsorted, tile_expert, tile_valid, w1, w3, w2):
    n_tiles = xsorted.shape[0] // ROW_TILE
    grid_spec = pltpu.PrefetchScalarGridSpec(
        num_scalar_prefetch=2, grid=(n_tiles,),
        in_specs=[pl.BlockSpec((ROW_TILE, D_MODEL), lambda i, te, tv: (i, 0)),
                  pl.BlockSpec((1, D_MODEL, D_FF), lambda i, te, tv: (te[i], 0, 0)),
                  pl.BlockSpec((1, D_MODEL, D_FF), lambda i, te, tv: (te[i], 0, 0)),
                  pl.BlockSpec((1, D_FF, D_MODEL), lambda i, te, tv: (te[i], 0, 0))],
        out_specs=pl.BlockSpec((ROW_TILE, D_MODEL), lambda i, te, tv: (i, 0)))
    return pl.pallas_call(
        _grouped_ffn_body, grid_spec=grid_spec, out_shape=jax.ShapeDtypeStruct(xsorted.shape, F32),
        compiler_params=_cparams(1), name="grouped_ffn")(tile_expert, tile_valid, xsorted, w1, w3, w2)


def _rope_slab(xs, cos, sin_signed, first_half):
    rot = jnp.where(first_half, pltpu.roll(xs, LANES - C_HDIM // 2, 1), pltpu.roll(xs, C_HDIM // 2, 1))
    return xs * cos + rot * sin_signed


def _swa_body(sink_ref, x_ref, pk_ref, pv_ref, cos_ref, sin_ref, nrm_ref, win32_ref, wout32_ref, qg_ref, kg_ref,
              bd_ref, y_ref, ko_ref, vo_ref, k_scr, v_scr, q_scr, a_scr, win_ref, wout_ref, *, n_seq, seq_rows,
              past_valid):
    rows = n_seq * seq_rows
    j = pl.program_id(1)
    tail = min(WINDOW, seq_rows)

    @pl.when((pl.program_id(0) == 0) & (j == 0))
    def _():
        win_ref[...] = win32_ref[...].astype(BF16)
        wout_ref[...] = wout32_ref[...].astype(BF16)

    if past_valid:
        for s in range(n_seq):
            k_scr[s, 0:WINDOW] = pk_ref[s]
            v_scr[s, 0:WINDOW] = pv_ref[s]
    else:
        @pl.when(j == 0)
        def _():
            for s in range(n_seq):
                k_scr[s, 0:WINDOW] = jnp.zeros((WINDOW, C_KV_DIM), F32)
                v_scr[s, 0:WINDOW] = jnp.zeros((WINDOW, C_KV_DIM), F32)

        @pl.when(j > 0)
        def _():
            for s in range(n_seq):
                k_scr[s, 0:WINDOW] = k_scr[s, seq_rows:seq_rows + WINDOW]
                v_scr[s, 0:WINDOW] = v_scr[s, seq_rows:seq_rows + WINDOW]

    x = x_ref[...]
    h = _rms(x, nrm_ref[...]).astype(BF16)
    q_dim = C_HEADS * C_HDIM
    q = _dot(h, win_ref[:, 0:q_dim])
    k = _dot(h, win_ref[:, q_dim:q_dim + C_KV_DIM])
    v = _dot(h, win_ref[:, q_dim + C_KV_DIM:q_dim + 2 * C_KV_DIM])

    cos = cos_ref[...]
    sin_signed = sin_ref[...]
    first_half = (lax.broadcasted_iota(jnp.int32, (rows, LANES), 1) % C_HDIM) < (C_HDIM // 2)
    bd = bd_ref[...]
    qn = q * lax.rsqrt(_dot((q * q).astype(BF16), bd) + EPS) * qg_ref[...]
    kn = k * lax.rsqrt(_dot((k * k).astype(BF16), bd[0:C_KV_DIM, 0:C_KV_DIM]) + EPS) * kg_ref[...]
    scale = C_HDIM ** -0.5
    for sl in range(q_dim // LANES):
        ls = slice(sl * LANES, (sl + 1) * LANES)
        q_scr[:, ls] = _rope_slab(qn[:, ls], cos, sin_signed, first_half) * scale
    for sl in range(C_KV_DIM // LANES):
        ls = slice(sl * LANES, (sl + 1) * LANES)
        kr = _rope_slab(kn[:, ls], cos, sin_signed, first_half)
        for s in range(n_seq):
            k_scr[s, WINDOW:WINDOW + seq_rows, ls] = kr[s * seq_rows:(s + 1) * seq_rows]
    for s in range(n_seq):
        v_scr[s, WINDOW:WINDOW + seq_rows] = v[s * seq_rows:(s + 1) * seq_rows]
        ko_ref[s] = k_scr[s, WINDOW + seq_rows - tail:WINDOW + seq_rows]
        vo_ref[s] = v_scr[s, WINDOW + seq_rows - tail:WINDOW + seq_rows]

    chunks_per_seq = seq_rows // CHUNK
    n_keys = WINDOW + CHUNK
    key_idx = lax.broadcasted_iota(jnp.int32, (C_GROUP * CHUNK, n_keys), 1)
    row_grp = lax.broadcasted_iota(jnp.int32, (C_GROUP * CHUNK, 1), 0) // CHUNK
    ones_keys = jnp.ones((n_keys, C_HDIM), BF16)

    head_sinks = []
    for kvh in range(C_KV_HEADS):
        sink = jnp.zeros((C_GROUP * CHUNK, 1), F32)
        for g in range(C_GROUP):
            sink = jnp.where(row_grp == g, sink_ref[kvh * C_GROUP + g], sink)
        head_sinks.append(sink)

    def attend(chunk_ids):
        units = [(idx, kvh) for idx in chunk_ids for kvh in range(C_KV_HEADS)]
        head_lanes = lambda kvh: slice(kvh * C_HDIM, (kvh + 1) * C_HDIM)
        q_rows = lambda idx: slice(idx * CHUNK, (idx + 1) * CHUNK)
        key_rows = lambda idx: slice((idx % chunks_per_seq) * CHUNK, (idx % chunks_per_seq) * CHUNK + n_keys)
        qs = [jnp.concatenate(
            [q_scr[q_rows(idx), (kvh * C_GROUP + g) * C_HDIM:(kvh * C_GROUP + g + 1) * C_HDIM]
             for g in range(C_GROUP)], axis=0).astype(BF16) for idx, kvh in units]
        kh = [k_scr[idx // chunks_per_seq, key_rows(idx), head_lanes(kvh)].astype(BF16) for idx, kvh in units]
        vh = [v_scr[idx // chunks_per_seq, key_rows(idx), head_lanes(kvh)].astype(BF16) for idx, kvh in units]
        sc = [_dot_nt(qn, kn) for qn, kn in zip(qs, kh)]
        if not past_valid:
            sc = [jnp.where((key_idx + (idx % chunks_per_seq) * CHUNK >= WINDOW) | (j > 0), sn, -jnp.inf)
                  for sn, (idx, kvh) in zip(sc, units)]
        m = [jnp.maximum(jnp.max(sn, axis=-1, keepdims=True), head_sinks[kvh]) for sn, (idx, kvh) in zip(sc, units)]
        p = [jnp.exp(sn - mn).astype(BF16) for sn, mn in zip(sc, m)]
        den = [_dot(pn, ones_keys) + jnp.exp(head_sinks[kvh] - mn) for pn, mn, (idx, kvh) in zip(p, m, units)]
        o = [_dot(pn, vn) / dn for pn, vn, dn in zip(p, vh, den)]
        for on, (idx, kvh) in zip(o, units):
            for g in range(C_GROUP):
                hs = slice((kvh * C_GROUP + g) * C_HDIM, (kvh * C_GROUP + g + 1) * C_HDIM)
                a_scr[q_rows(idx), hs] = on[g * CHUNK:(g + 1) * CHUNK]

    n_chunks = rows // CHUNK
    for first in range(0, n_chunks, SWA_STAGE_CHUNKS):
        attend(range(first, min(first + SWA_STAGE_CHUNKS, n_chunks)))
    y_ref[...] = x + _dot(a_scr[...].astype(BF16), wout_ref[...])


def _swa(x, pk, pv, cos, sin_signed, sinks, consts, *, n_batch, tiles_per_batch, n_seq, seq_rows, past_valid):
    rows = n_seq * seq_rows
    tail = min(WINDOW, seq_rows)
    n_str = n_batch * n_seq
    whole = lambda *shape: pl.BlockSpec(shape, lambda b, j, sk: (0,) * len(shape))
    once = lambda *shape: pl.BlockSpec(shape, lambda b, j, sk: (0,) * len(shape), pipeline_mode=pl.Buffered(1))
    in_specs = [pl.BlockSpec((rows, D_MODEL), lambda b, j, sk: (b * tiles_per_batch + j, 0)),
                pl.BlockSpec((n_seq, WINDOW, C_KV_DIM), lambda b, j, sk: (b, 0, 0)),
                pl.BlockSpec((n_seq, WINDOW, C_KV_DIM), lambda b, j, sk: (b, 0, 0)),
                pl.BlockSpec((rows, LANES), lambda b, j, sk: (j, 0)),
                pl.BlockSpec((rows, LANES), lambda b, j, sk: (j, 0)),
                whole(1, D_MODEL), once(D_MODEL, ODD_IN), once(D_MODEL, D_MODEL),
                whole(1, D_MODEL), whole(1, C_KV_DIM), whole(D_MODEL, D_MODEL)]
    args = [x, pk, pv, cos, sin_signed] + list(consts)
    kv_out = jax.ShapeDtypeStruct((n_str, tail, C_KV_DIM), F32)
    kv_spec = pl.BlockSpec((n_seq, tail, C_KV_DIM), lambda b, j, sk: (b, 0, 0))
    grid_spec = pltpu.PrefetchScalarGridSpec(
        num_scalar_prefetch=1, grid=(n_batch, tiles_per_batch), in_specs=in_specs,
        out_specs=[pl.BlockSpec((rows, D_MODEL), lambda b, j, sk: (b * tiles_per_batch + j, 0)),
                   kv_spec, kv_spec],
        scratch_shapes=[pltpu.VMEM((n_seq, WINDOW + seq_rows, C_KV_DIM), F32),
                        pltpu.VMEM((n_seq, WINDOW + seq_rows, C_KV_DIM), F32),
                        pltpu.VMEM((rows, D_MODEL), F32), pltpu.VMEM((rows, D_MODEL), F32),
                        pltpu.VMEM((D_MODEL, ODD_IN), BF16), pltpu.VMEM((D_MODEL, D_MODEL), BF16)])
    return pl.pallas_call(
        functools.partial(_swa_body, n_seq=n_seq, seq_rows=seq_rows, past_valid=past_valid),
        grid_spec=grid_spec,
        out_shape=[jax.ShapeDtypeStruct(x.shape, F32), kv_out, kv_out],
        compiler_params=_cparams(2),
        name="swa_s" if past_valid else "swa_p")(sinks, *args)


def _router_body(xp_ref, xs_ref, nrm_ref, rboth_ref, lstrict_ref, su_ref, info_ref, cnt_ref, *, n_prompt_tiles):
    def run(x_ref):
        h = _rms(x_ref[...], nrm_ref[...])
        hi = h.astype(BF16)
        lo = (h - hi.astype(F32)).astype(BF16)
        both = _dot(hi, rboth_ref[...])
        logits = both[:, 0:LANES] + both[:, LANES:2 * LANES] + _dot(lo, rboth_ref[:, 0:LANES])
        lane = lax.broadcasted_iota(jnp.int32, logits.shape, 1).astype(F32)
        logits = jnp.where(lane < N_EXPERTS, logits, -jnp.inf)
        l1 = jnp.max(logits, axis=-1, keepdims=True)
        i1 = jnp.min(jnp.where(logits == l1, lane, float(LANES)), axis=-1, keepdims=True)
        rest = jnp.where(lane == i1, -jnp.inf, logits)
        l2 = jnp.max(rest, axis=-1, keepdims=True)
        i2 = jnp.min(jnp.where(rest == l2, lane, float(LANES)), axis=-1, keepdims=True)
        e = jnp.exp(l2 - l1)
        g1 = 1.0 / (1.0 + e)
        g2 = e * g1
        oh1 = (lane == i1).astype(F32)
        oh2 = (lane == i2).astype(F32)
        cnt1 = jnp.sum(oh1, axis=0, keepdims=True)
        cnt2 = jnp.sum(oh2, axis=0, keepdims=True)
        seg = jnp.broadcast_to(jnp.ceil((cnt1 + cnt2) * (1.0 / GROUP)), (GROUP, LANES))
        off = _dot(seg.astype(BF16), su_ref[...])[0:1] * GROUP
        c1 = _dot(lstrict_ref[...], oh1.astype(BF16)) + off
        c2 = _dot(lstrict_ref[...], oh2.astype(BF16)) + off + cnt1
        d1 = jnp.sum(jnp.where(lane == i1, c1, 0.0), axis=-1, keepdims=True)
        d2 = jnp.sum(jnp.where(lane == i2, c2, 0.0), axis=-1, keepdims=True)
        info = jnp.zeros(logits.shape, F32)
        for n, col in enumerate([d1, d2, g1, g2]):
            info = jnp.where(lane == n, col, info)
        info_ref[...] = info
        cnt_ref[...] = seg * GROUP

    _on_row_source(pl.program_id(0), n_prompt_tiles, run, (xp_ref,), (xs_ref,))


def _router(xp, xs, nrm, rboth, lstrict, su):
    n_prompt_tiles = xp.shape[0] // ROW_TILE
    n_tiles = n_prompt_tiles + xs.shape[0] // ROW_TILE
    whole = lambda *shape: pl.BlockSpec(shape, lambda i: (0,) * len(shape))
    return pl.pallas_call(
        functools.partial(_router_body, n_prompt_tiles=n_prompt_tiles), grid=(n_tiles,),
        in_specs=[*_two_source_specs(n_prompt_tiles), whole(1, D_MODEL), whole(D_MODEL, 2 * LANES),
                  whole(ROW_TILE, ROW_TILE), whole(LANES, LANES)],
        out_specs=[pl.BlockSpec((ROW_TILE, LANES), lambda i: (i, 0)), pl.BlockSpec((GROUP, LANES), lambda i: (i, 0))],
        out_shape=[jax.ShapeDtypeStruct((n_tiles * ROW_TILE, LANES), F32),
                   jax.ShapeDtypeStruct((n_tiles * GROUP, LANES), F32)],
        compiler_params=_cparams(1), name="router")(xp, xs, nrm, rboth, lstrict, su)


BIG_ROWS = 8 * GROUP
SEG_FIELDS = 4
TILE_FIELDS = 2
TAIL_FIELDS = 3


def _rows_copy(src_ref, src_row, dst_ref, dst_row, n_rows, sem):
    return pltpu.make_async_copy(src_ref.at[pl.ds(pl.multiple_of(src_row, GROUP), n_rows)],
                                 dst_ref.at[pl.ds(pl.multiple_of(dst_row, GROUP), n_rows)], sem)


def _start_pieces(src_ref, src0, dst_ref, dst0, n_big, n_small, sem):
    def big(g, carry):
        _rows_copy(src_ref, src0 + g * BIG_ROWS, dst_ref, dst0 + g * BIG_ROWS, BIG_ROWS, sem).start()
        return carry

    lax.fori_loop(0, n_big, big, 0)
    done = n_big * BIG_ROWS

    def small(g, carry):
        _rows_copy(src_ref, src0 + done + g * GROUP, dst_ref, dst0 + done + g * GROUP, GROUP, sem).start()
        return carry

    lax.fori_loop(0, n_small, small, 0)


def _wait_pieces(n_big, n_small, src_ref, dst_ref, sem):
    def big(g, carry):
        _rows_copy(src_ref, 0, dst_ref, 0, BIG_ROWS, sem).wait()
        return carry

    lax.fori_loop(0, n_big, big, 0)

    def small(g, carry):
        _rows_copy(src_ref, 0, dst_ref, 0, GROUP, sem).wait()
        return carry

    lax.fori_loop(0, n_small, small, 0)


def _segment_table(seg, n_sorted_tiles):
    pieces = lambda rows: (rows // BIG_ROWS, (rows % BIG_ROWS) // GROUP)
    local_off = jnp.cumsum(seg, axis=1) - seg
    expert_rows = jnp.sum(seg, axis=0)
    expert_pad = ((expert_rows + ROW_TILE - 1) // ROW_TILE) * ROW_TILE
    expert_end = jnp.cumsum(expert_pad)
    expert_off = expert_end - expert_pad
    seg_start = expert_off[None, :] + jnp.cumsum(seg, axis=0) - seg
    n_used = (expert_end[-1] // ROW_TILE).reshape(1)
    tile_ids = jnp.arange(n_sorted_tiles, dtype=jnp.int32)
    tile_expert = jnp.minimum(jnp.sum(tile_ids[:, None] >= (expert_end // ROW_TILE)[None, :], axis=1), N_EXPERTS - 1)
    tile_expert = jnp.where(tile_ids < n_used[0], tile_expert, tile_expert[n_used[0] - 1])
    tile_valid = jnp.clip((expert_off + expert_rows)[tile_expert] - tile_ids * ROW_TILE, 0, ROW_TILE)
    tile_valid = jnp.where(tile_ids < n_used[0], tile_valid, 0)
    seg_big, seg_small = pieces(seg)
    tail_big, tail_small = pieces(expert_pad - expert_rows)
    table = jnp.concatenate([
        jnp.stack([seg_start, seg_big, seg_small, local_off], axis=-1).reshape(-1),
        jnp.stack([jnp.sum(seg_big, axis=1), jnp.sum(seg_small, axis=1)], axis=-1).reshape(-1),
        jnp.stack([expert_off + expert_rows, tail_big, tail_small], axis=-1).reshape(-1),
        n_used])
    return table.astype(jnp.int32), tile_expert.astype(jnp.int32), tile_valid.astype(jnp.int32)


def _table_sections(n_tiles):
    tiles0 = SEG_FIELDS * n_tiles * N_EXPERTS
    tails0 = tiles0 + TILE_FIELDS * n_tiles
    return tiles0, tails0, tails0 + TAIL_FIELDS * N_EXPERTS


def _dispatch_body(tab_ref, info_ref, xp_ref, xs_ref, nrm_ref, sel_ref, out_ref, loc_scr, z_scr, sems, *,
                   n_prompt_tiles, n_tiles, n_sorted_tiles):
    i = pl.program_id(0)
    slot = i % 2
    tiles0, tails0, used0 = _table_sections(n_tiles)

    def wait_tile(t, s):
        _wait_pieces(tab_ref[tiles0 + TILE_FIELDS * t], tab_ref[tiles0 + TILE_FIELDS * t + 1], loc_scr.at[s], out_ref,
                     sems.at[s])

    @pl.when(i >= 2)
    def _():
        wait_tile(i - 2, slot)

    def run(x_ref):
        hb = _rms(x_ref[...], nrm_ref[...]).astype(BF16)
        info = info_ref[...]
        lane = lax.broadcasted_iota(jnp.int32, info.shape, 1)
        dest = jnp.where(lane < TOP_K, info, 0.0)
        drow = sum(_dot_nt(sel_ref[...], part) for part in _split3(dest))
        r = lax.broadcasted_iota(jnp.int32, (LOCAL_ROWS, ROW_TILE), 0).astype(F32)
        perm = jnp.where(r == drow[0:1], 1.0, jnp.where(r == drow[1:2], 1.0, 0.0)).astype(BF16)
        loc_scr[slot] = _dot(perm, hb)

    _on_row_source(i, n_prompt_tiles, run, (xp_ref,), (xs_ref,))

    for e in range(N_EXPERTS):
        base = SEG_FIELDS * (i * N_EXPERTS + e)
        _start_pieces(loc_scr.at[slot], tab_ref[base + 3], out_ref, tab_ref[base], tab_ref[base + 1],
                      tab_ref[base + 2], sems.at[slot])

    @pl.when(i == n_tiles - 1)
    def _():
        wait_tile(i, slot)
        if n_tiles > 1:
            wait_tile(i - 1, 1 - slot)
        z_scr[...] = jnp.zeros_like(z_scr)
        for e in range(N_EXPERTS):
            base = tails0 + TAIL_FIELDS * e
            _start_pieces(z_scr, 0, out_ref, tab_ref[base], tab_ref[base + 1], tab_ref[base + 2], sems.at[2])
            _wait_pieces(tab_ref[base + 1], tab_ref[base + 2], z_scr, out_ref, sems.at[2])

        def zero_tile(t, carry):
            cp = _rows_copy(z_scr, 0, out_ref, t * ROW_TILE, ROW_TILE, sems.at[2])
            cp.start()
            cp.wait()
            return carry

        lax.fori_loop(tab_ref[used0], n_sorted_tiles, zero_tile, 0)


def _dispatch(table, info, xp, xs, nrm, sel, n_sorted_tiles):
    n_prompt_tiles = xp.shape[0] // ROW_TILE
    n_tiles = n_prompt_tiles + xs.shape[0] // ROW_TILE
    whole = lambda *shape: pl.BlockSpec(shape, lambda i, tab: (0,) * len(shape))
    grid_spec = pltpu.PrefetchScalarGridSpec(
        num_scalar_prefetch=1, grid=(n_tiles,),
        in_specs=[pl.BlockSpec((ROW_TILE, LANES), lambda i, tab: (i, 0)), *_two_source_specs(n_prompt_tiles),
                  whole(1, D_MODEL), whole(GROUP, LANES)],
        out_specs=pl.BlockSpec(memory_space=pl.ANY),
        scratch_shapes=[pltpu.VMEM((2, LOCAL_ROWS, D_MODEL), F32), pltpu.VMEM((ROW_TILE, D_MODEL), F32),
                        pltpu.SemaphoreType.DMA((3,))])
    return pl.pallas_call(
        functools.partial(_dispatch_body, n_prompt_tiles=n_prompt_tiles, n_tiles=n_tiles,
                          n_sorted_tiles=n_sorted_tiles),
        grid_spec=grid_spec, out_shape=jax.ShapeDtypeStruct((n_sorted_tiles * ROW_TILE, D_MODEL), F32),
        compiler_params=_cparams(1), name="moe_dispatch")(table, info, xp, xs, nrm, sel)


def _combine_body(tab_ref, info_ref, xp_ref, xs_ref, e_ref, yp_ref, ys_ref, loc_scr, sems, *, n_prompt_tiles,
                  n_tiles):
    i = pl.program_id(0)
    slot = i % 2
    tiles0, _, _ = _table_sections(n_tiles)

    def fetch(t, s):
        for e in range(N_EXPERTS):
            base = SEG_FIELDS * (t * N_EXPERTS + e)
            _start_pieces(e_ref, tab_ref[base], loc_scr.at[s], tab_ref[base + 3], tab_ref[base + 1],
                          tab_ref[base + 2], sems.at[s])

    @pl.when(i == 0)
    def _():
        loc_scr[...] = jnp.zeros_like(loc_scr)
        fetch(0, 0)

    @pl.when(i + 1 < n_tiles)
    def _():
        fetch(i + 1, 1 - slot)

    _wait_pieces(tab_ref[tiles0 + TILE_FIELDS * i], tab_ref[tiles0 + TILE_FIELDS * i + 1], e_ref, loc_scr.at[slot],
                 sems.at[slot])

    def run(x_ref, y_ref):
        info = info_ref[...]
        eb = loc_scr[slot].astype(BF16)
        lane = lax.broadcasted_iota(jnp.int32, (ROW_TILE, LOCAL_ROWS), 1).astype(F32)
        pick = jnp.zeros((ROW_TILE, LOCAL_ROWS), F32)
        for k in range(TOP_K):
            pick = jnp.where(lane == info[:, k:k + 1], info[:, TOP_K + k:TOP_K + k + 1], pick)
        y_ref[...] = x_ref[...] + _dot(pick.astype(BF16), eb)

    _on_row_source(i, n_prompt_tiles, run, (xp_ref, yp_ref), (xs_ref, ys_ref))


def _combine(table, info, xp, xs, esorted):
    n_prompt_tiles = xp.shape[0] // ROW_TILE
    n_tiles = n_prompt_tiles + xs.shape[0] // ROW_TILE
    grid_spec = pltpu.PrefetchScalarGridSpec(
        num_scalar_prefetch=1, grid=(n_tiles,),
        in_specs=[pl.BlockSpec((ROW_TILE, LANES), lambda i, tab: (i, 0)), *_two_source_specs(n_prompt_tiles),
                  pl.BlockSpec(memory_space=pl.ANY)],
        out_specs=list(_two_source_specs(n_prompt_tiles)),
        scratch_shapes=[pltpu.VMEM((2, LOCAL_ROWS, D_MODEL), F32), pltpu.SemaphoreType.DMA((2,))])
    return pl.pallas_call(
        functools.partial(_combine_body, n_prompt_tiles=n_prompt_tiles, n_tiles=n_tiles),
        grid_spec=grid_spec,
        out_shape=[jax.ShapeDtypeStruct(xp.shape, F32), jax.ShapeDtypeStruct(xs.shape, F32)],
        compiler_params=_cparams(1), name="moe_combine")(table, info, xp, xs, esorted)


def _rope_tables(pos):
    half = C_HDIM // 2
    inv = 1.0 / (ROPE_THETA ** (jnp.arange(half, dtype=F32) / half))
    ang = pos.astype(F32)[:, None] * inv[None, :]
    cos = jnp.cos(ang)
    sin = jnp.sin(ang)
    reps = LANES // C_HDIM
    return jnp.tile(jnp.concatenate([cos, cos], axis=-1), (1, reps)), jnp.tile(jnp.concatenate([-sin, sin], axis=-1),
                                                                                 (1, reps))


def kernel(x_prompt, x_sample, mem_prompt, cache_mem_k, cache_mem_v, state_hgrn, cache_swa_k, cache_swa_v, norm_mix, norm_xattn, norm_ffn, even_w_in, even_w_out, gmlp_w_s, gmlp_b_s, gmlp_ln_g, gmlp_ln_b, hgrn_lb_logits, hgrn_out_norm, attn_w_in, attn_w_out, attn_q_norm, attn_k_norm, attn_sinks, xattn_mem_norm, xattn_w_q, xattn_w_k, xattn_w_v, xattn_w_o, xattn_q_norm, xattn_k_norm, ffn_w1, ffn_w3, ffn_w2, moe_router, moe_w1, moe_w3, moe_w2):
    n_batch, seq, d = x_prompt.shape
    dec_batch, dec_seq, _ = x_sample.shape
    n_mem = mem_prompt.shape[1]
    depth = norm_mix.shape[0]
    past_len = PAST_LEN
    assert d == D_MODEL and depth == 2 and seq % ROW_TILE == 0 and dec_batch * dec_seq == ROW_TILE
    assert dec_seq == CHUNK and ROW_TILE % n_mem == 0 and cache_swa_k.shape[2] == WINDOW
    assert even_w_in.shape[-1] == EVEN_IN and attn_w_in.shape[-1] == ODD_IN and ffn_w1.shape[-1] == D_FF
    assert moe_w1.shape[1] == N_EXPERTS
    n_prompt_rows = n_batch * seq
    n_rows = n_prompt_rows + dec_batch * dec_seq
    n_prompt_tiles = n_prompt_rows // ROW_TILE
    tiles_per_batch = seq // ROW_TILE
    bf = lambda w: w.astype(BF16)
    row = lambda g: g.reshape(1, -1).astype(F32)

    lb_all = jnp.cumsum(jax.nn.softmax(hgrn_lb_logits.astype(F32), axis=0), axis=0)
    wcum = jnp.asarray(_cumsum_matrix(), BF16)
    lmask = jnp.asarray(_level_masks(), F32)

    def even_consts(n):
        tril = jnp.tril(jnp.ones((n, n), bool))
        ws = jnp.where(tril[None], gmlp_w_s[0, :, :n, :n], 0.0).astype(BF16)
        bs = jnp.broadcast_to(gmlp_b_s[0, :, :n, None], (A_GROUPS, n, LANES)).astype(F32)
        return [row(norm_mix[0]), even_w_in[0], even_w_out[0], ws, bs,
                gmlp_ln_g[0].reshape(A_GROUPS, 1, A_GDIM), gmlp_ln_b[0].reshape(A_GROUPS, 1, A_GDIM),
                row(lb_all[0]), row(hgrn_out_norm[0]), wcum, lmask]

    xp, hgrn_p = _even_mixer(
        x_prompt.reshape(n_prompt_rows, d), jnp.zeros((n_batch, B_HEADS, B_HDIM, B_HDIM), F32),
        even_consts(A_CHUNK), n_batch=n_batch, tiles_per_batch=tiles_per_batch, n_seq=1, seq_rows=ROW_TILE,
        gchunk=A_CHUNK, emit_v=False)
    xs, hgrn_s, gmlp_v = _even_mixer(
        x_sample.reshape(ROW_TILE, d), state_hgrn[0], even_consts(min(A_CHUNK, dec_seq)), n_batch=1,
        tiles_per_batch=1, n_seq=dec_batch, seq_rows=dec_seq, gchunk=min(A_CHUNK, dec_seq), emit_v=True)

    mem_k, mem_v = _memory_kv(mem_prompt.reshape(n_batch * n_mem, d), xattn_mem_norm.reshape(depth, 1, d),
                              xattn_w_k, xattn_w_v, xattn_k_norm.reshape(depth, 1, X_HDIM),
                              n_batch=n_batch, n_mem=n_mem)

    def cross_attention(xp, xs, l):
        consts = (row(norm_xattn[l]), xattn_w_q[l], xattn_w_o[l], row(xattn_q_norm[l]))
        xp = _xattn(xp, mem_k, mem_v, *consts, layer=l, steps_per_mem=tiles_per_batch, n_seq=1, seq_rows=ROW_TILE)
        xs = _xattn(xs, cache_k, cache_v, *consts, layer=l, steps_per_mem=1, n_seq=dec_batch, seq_rows=dec_seq)
        return xp, xs

    cache_k = cache_mem_k.reshape(depth, dec_batch, n_mem * X_HEADS, X_HDIM)
    cache_v = cache_mem_v.reshape(depth, dec_batch, n_mem * X_HEADS, X_HDIM)

    xp, xs = cross_attention(xp, xs, 0)
    xp, xs = _dense_ffn(xp, xs, row(norm_ffn[0]), ffn_w1[0], ffn_w3[0], ffn_w2[0])

    reps = D_MODEL // C_HDIM
    bd = jnp.asarray(np.kron(np.eye(reps, dtype=np.float32), np.full((C_HDIM, C_HDIM), 1.0 / C_HDIM, np.float32)),
                     BF16)
    swa_consts = [row(norm_mix[1]), attn_w_in[0], attn_w_out[0], row(jnp.tile(attn_q_norm[0], C_HEADS)),
                  row(jnp.tile(attn_k_norm[0], C_KV_HEADS)), bd]
    sinks = attn_sinks[0].astype(F32)
    cos_p, sin_p = _rope_tables(jnp.arange(seq, dtype=jnp.int32))
    cos_s, sin_s = _rope_tables(past_len + jnp.arange(dec_seq, dtype=jnp.int32))
    no_past = jnp.zeros((n_batch, WINDOW, C_KV_DIM), F32)
    xp, swk_p, swv_p = _swa(xp, no_past, no_past, cos_p, sin_p, sinks, swa_consts, n_batch=n_batch,
                            tiles_per_batch=tiles_per_batch, n_seq=1, seq_rows=ROW_TILE, past_valid=False)
    xs, swk_s, swv_s = _swa(xs, cache_swa_k[0].reshape(dec_batch, WINDOW, C_KV_DIM),
                            cache_swa_v[0].reshape(dec_batch, WINDOW, C_KV_DIM),
                            jnp.tile(cos_s, (dec_batch, 1)), jnp.tile(sin_s, (dec_batch, 1)), sinks, swa_consts,
                            n_batch=1, tiles_per_batch=1, n_seq=dec_batch, seq_rows=dec_seq, past_valid=True)
    xp, xs = cross_attention(xp, xs, 1)

    n_tiles = n_rows // ROW_TILE
    router_w = jnp.zeros((d, LANES), F32).at[:, :N_EXPERTS].set(moe_router[0].astype(F32))
    rhi = router_w.astype(BF16)
    rlo = (router_w - rhi.astype(F32)).astype(BF16)
    lstrict = jnp.asarray(np.tril(np.ones((ROW_TILE, ROW_TILE), np.float32), -1), BF16)
    su = jnp.asarray(np.triu(np.ones((LANES, LANES), np.float32), 1), BF16)
    sel = jnp.asarray(np.eye(GROUP, LANES, dtype=np.float32), BF16)
    nrm_ffn = row(norm_ffn[1])
    info, seg = _router(xp, xs, nrm_ffn, jnp.concatenate([rhi, rlo], axis=1), lstrict, su)
    seg = seg.reshape(n_tiles, GROUP, LANES)[:, 0, :N_EXPERTS].astype(jnp.int32)
    n_sorted_tiles = -(-(TOP_K * n_rows + n_tiles * N_EXPERTS * (GROUP - 1) + N_EXPERTS * (ROW_TILE - 1)) // ROW_TILE)
    table, tile_expert, tile_valid = _segment_table(seg, n_sorted_tiles)
    hsorted = _dispatch(table, info, xp, xs, nrm_ffn, sel, n_sorted_tiles)
    esorted = _grouped_ffn(hsorted, tile_expert, tile_valid, bf(moe_w1[0]), bf(moe_w3[0]), bf(moe_w2[0]))
    y_prompt, y_sample = _combine(table, info, xp, xs, esorted)

    n_even = state_hgrn.shape[0]
    n_odd = cache_swa_k.shape[0]
    return (y_prompt.reshape(n_batch, seq, d), y_sample.reshape(dec_batch, dec_seq, d),
            mem_k.reshape(depth, n_batch, n_mem, X_HEADS, X_HDIM), mem_v.reshape(depth, n_batch, n_mem, X_HEADS, X_HDIM),
            hgrn_p.reshape(n_even, n_batch, B_HEADS, B_HDIM, B_HDIM),
            gmlp_v.reshape(n_even, dec_batch, dec_seq, A_GROUPS, A_GDIM),
            hgrn_s.reshape(n_even, dec_batch, B_HEADS, B_HDIM, B_HDIM),
            swk_p.reshape(n_odd, n_batch, WINDOW, C_KV_HEADS, C_HDIM), swv_p.reshape(n_odd, n_batch, WINDOW, C_KV_HEADS, C_HDIM),
            swk_s.reshape(n_odd, dec_batch, dec_seq, C_KV_HEADS, C_HDIM), swv_s.reshape(n_odd, dec_batch, dec_seq, C_KV_HEADS, C_HDIM))
```

```python
import functools

import numpy as np
import jax
import jax.numpy as jnp
from jax import lax
from jax.experimental import pallas as pl
from jax.experimental.pallas import tpu as pltpu

F32 = jnp.float32
BF16 = jnp.bfloat16

D_MODEL = 1024
EPS = 1e-6
LOG2_E = 1.4426950408889634
CHUNK = 64
A_GROUPS = 4
A_DIM = D_MODEL // 2
A_GDIM = A_DIM // A_GROUPS
A_CHUNK = 128
B_HEADS = 4
B_DIM = D_MODEL // 2
B_HDIM = B_DIM // B_HEADS
EVEN_IN = 2 * A_DIM + 4 * B_DIM
C_HEADS = 16
C_KV_HEADS = 4
C_HDIM = D_MODEL // C_HEADS
C_GROUP = C_HEADS // C_KV_HEADS
C_KV_DIM = C_KV_HEADS * C_HDIM
WINDOW = 128
ROPE_THETA = 10000.0
PAST_LEN = 4096
ODD_IN = (C_HEADS + 2 * C_KV_HEADS) * C_HDIM
X_HEADS = 4
X_HDIM = 128
X_DIM = X_HEADS * X_HDIM
D_FF = 2816
N_EXPERTS = 8
TOP_K = 2

LANES = 128
ROW_TILE = 512
FF_SUB = 512
N_LEVELS = 6
HGRN_STAGE_CHUNKS = 2
SWA_STAGE_CHUNKS = 2
GROUP = 8
LOCAL_ROWS = -(-(TOP_K * ROW_TILE + N_EXPERTS * (GROUP - 1)) // LANES) * LANES
VMEM_LIMIT = 58 * 1024 * 1024


def _cparams(n_axes):
    return pltpu.CompilerParams(dimension_semantics=("arbitrary",) * n_axes, vmem_limit_bytes=VMEM_LIMIT)


def _dot(a, b):
    return jnp.dot(a, b, preferred_element_type=F32)


def _dot_nt(a, b):
    return lax.dot_general(a, b, (((1,), (1,)), ((), ())), preferred_element_type=F32)


def _rms(x, g):
    return x * lax.rsqrt(jnp.mean(x * x, axis=-1, keepdims=True) + EPS) * g


def _silu(x):
    return x * (1.0 / (1.0 + jnp.exp(-x)))


def _split3(x):
    hi = x.astype(BF16)
    r1 = x - hi.astype(F32)
    mid = r1.astype(BF16)
    lo = (r1 - mid.astype(F32)).astype(BF16)
    return hi, mid, lo


def _cumsum_matrix():
    r = np.arange(CHUNK)
    s = np.arange(CHUNK)
    blocks = [(s[None, :] <= r[:, None])]
    for l in range(N_LEVELS):
        h = 1 << l
        ref = (r & ~(2 * h - 1)) + h - 1
        blocks.append(s[None, :] <= ref[:, None])
    w = np.concatenate(blocks, axis=0).astype(np.float32)
    return np.concatenate([w, w, w], axis=1)


def _level_masks():
    t = np.arange(CHUNK)[:, None]
    s = np.arange(CHUNK)[None, :]
    masks = []
    for l in range(N_LEVELS):
        masks.append(((t >> (l + 1)) == (s >> (l + 1))) & (((t >> l) & 1) == 1) & (((s >> l) & 1) == 0))
    masks.append(t == s)
    return np.stack(masks).astype(np.float32)


def _even_mixer_body(x_ref, s0_ref, nrm_ref, win32_ref, wout32_ref, ws_ref, bs_ref, lng_ref, lnb_ref, lb_ref, og_ref,
                     wcum_ref, lmask_ref, *rest, n_seq, seq_rows, gchunk, emit_v):
    if emit_v:
        y_ref, sout_ref, v_ref, proj_scr, mixed_scr, st_scr, win_ref, wout_ref = rest
    else:
        y_ref, sout_ref, proj_scr, mixed_scr, st_scr, win_ref, wout_ref = rest
        v_ref = None
    rows = n_seq * seq_rows
    j = pl.program_id(1)

    @pl.when((pl.program_id(0) == 0) & (j == 0))
    def _():
        win_ref[...] = win32_ref[...].astype(BF16)
        wout_ref[...] = wout32_ref[...].astype(BF16)

    @pl.when(j == 0)
    def _():
        for s in range(n_seq):
            for hd in range(B_HEADS):
                st_scr[s * B_HEADS + hd] = s0_ref[s, hd].T

    x = x_ref[...]
    h = _rms(x, nrm_ref[...]).astype(BF16)
    n_pieces = EVEN_IN // ROW_TILE
    for n in range(n_pieces):
        cs = slice(n * ROW_TILE, (n + 1) * ROW_TILE)
        proj_scr[:, cs] = _dot(h, win_ref[:, cs])

    groups = range(A_GROUPS)
    group_cols = lambda base, g: slice(base + g * A_GDIM, base + (g + 1) * A_GDIM)
    vgs = [jax.nn.gelu(proj_scr[:, group_cols(A_DIM, g)]) for g in groups]
    means = [jnp.mean(vg, axis=-1, keepdims=True) for vg in vgs]
    vcs = [vg - mu for vg, mu in zip(vgs, means)]
    variances = [jnp.mean(vc * vc, axis=-1, keepdims=True) for vc in vcs]
    for g in groups:
        gs = group_cols(0, g)
        vn = vcs[g] * lax.rsqrt(variances[g] + EPS) * lng_ref[g] + lnb_ref[g]
        if v_ref is not None:
            v_ref[:, gs] = vn
        ug = jax.nn.gelu(proj_scr[:, gs])
        vb = vn.astype(BF16)
        for c in range(rows // gchunk):
            rs = slice(c * gchunk, (c + 1) * gchunk)
            sp = _dot(ws_ref[g], vb[rs]) + bs_ref[g]
            mixed_scr[rs, gs] = ug[rs] * sp

    q0, f0, i0, g0 = (2 * A_DIM + k * B_DIM for k in range(4))
    lb = lb_ref[...]
    fg = lb + (1.0 - lb) * jax.nn.sigmoid(proj_scr[:, f0:f0 + B_DIM])
    proj_scr[:, 0:B_DIM] = jnp.log(fg) * LOG2_E
    proj_scr[:, B_DIM:2 * B_DIM] = 1.0 - fg
    proj_scr[:, q0:q0 + B_DIM] = _silu(proj_scr[:, q0:q0 + B_DIM])
    og = og_ref[...]
    chunks_per_seq = seq_rows // CHUNK

    heads = range(B_HEADS)
    head_cols = lambda base, hd: slice(base + hd * B_HDIM, base + (hd + 1) * B_HDIM)
    chunk_rows = lambda c: slice(c * CHUNK, (c + 1) * CHUNK)

    def chunk_local(c):
        rs = chunk_rows(c)
        hi, mid, lo = _split3(proj_scr[rs, 0:B_DIM])
        gg = _dot(wcum_ref[...], jnp.concatenate([hi, mid, lo], axis=0))
        vs = [proj_scr[rs, head_cols(i0, hd)] for hd in heads]
        vts = [v.T.astype(BF16) for v in vs]
        Gs = [gg[0:CHUNK, head_cols(0, hd)] for hd in heads]
        qs = [proj_scr[rs, head_cols(q0, hd)] for hd in heads]
        ks = [proj_scr[rs, head_cols(B_DIM, hd)] for hd in heads]
        operands = []
        for hd in heads:
            for l in range(N_LEVELS + 1):
                if l < N_LEVELS:
                    e = jnp.exp2(-jnp.abs(Gs[hd] - gg[(l + 1) * CHUNK:(l + 2) * CHUNK, head_cols(0, hd)]))
                    operands.append(((qs[hd] * e).astype(BF16), (ks[hd] * e).astype(BF16)))
                else:
                    operands.append((qs[hd].astype(BF16), ks[hd].astype(BF16)))
        blocks = [_dot_nt(qe, ke) for qe, ke in operands]
        parts = []
        for hd in heads:
            att = jnp.zeros((CHUNK, CHUNK), F32)
            for l in range(N_LEVELS + 1):
                att = jnp.where(lmask_ref[l] > 0.5, blocks[hd * (N_LEVELS + 1) + l], att)
            g_end = Gs[hd][CHUNK - 1:CHUNK, :]
            kd = (ks[hd] * jnp.exp2(g_end - Gs[hd])).astype(BF16)
            parts.append((_dot(att.astype(BF16), vs[hd].astype(BF16)), (qs[hd] * jnp.exp2(Gs[hd])).astype(BF16),
                          jnp.exp2(g_end), _dot(vts[hd], kd)))
        return parts

    def chunk_state(c, parts):
        rs = chunk_rows(c)
        sidx = (c // chunks_per_seq) * B_HEADS
        outs = []
        for hd in heads:
            o_local, q_decayed, decay, increment = parts[hd]
            st = st_scr[sidx + hd]
            outs.append(o_local + _dot_nt(q_decayed, st.astype(BF16)))
            st_scr[sidx + hd] = st * decay + increment
        for hd in heads:
            o = outs[hd]
            on = o * lax.rsqrt(jnp.mean(o * o, axis=-1, keepdims=True) + EPS) * og
            gate = _silu(proj_scr[rs, head_cols(g0, hd)])
            mixed_scr[rs, head_cols(A_DIM, hd)] = on * gate

    n_chunks = rows // CHUNK
    for first in range(0, n_chunks, HGRN_STAGE_CHUNKS):
        group = range(first, min(first + HGRN_STAGE_CHUNKS, n_chunks))
        local = [chunk_local(c) for c in group]
        for c, parts in zip(group, local):
            chunk_state(c, parts)

    y_ref[...] = x + _dot(mixed_scr[...].astype(BF16), wout_ref[...])

    @pl.when(j == pl.num_programs(1) - 1)
    def _():
        for s in range(n_seq):
            for hd in range(B_HEADS):
                sout_ref[s, hd] = st_scr[s * B_HEADS + hd].T


def _even_mixer(x, s0, consts, *, n_batch, tiles_per_batch, n_seq, seq_rows, gchunk, emit_v):
    rows = n_seq * seq_rows
    grid = (n_batch, tiles_per_batch)
    tile = lambda b, j: (b * tiles_per_batch + j, 0)
    whole = lambda *shape: pl.BlockSpec(shape, lambda b, j: (0,) * len(shape))
    once = lambda *shape: pl.BlockSpec(shape, lambda b, j: (0,) * len(shape), pipeline_mode=pl.Buffered(1))
    in_specs = [
        pl.BlockSpec((rows, D_MODEL), tile),
        pl.BlockSpec((n_seq, B_HEADS, B_HDIM, B_HDIM), lambda b, j: (b, 0, 0, 0)),
        whole(1, D_MODEL), once(D_MODEL, EVEN_IN), once(D_MODEL, D_MODEL),
        whole(A_GROUPS, gchunk, gchunk), whole(A_GROUPS, gchunk, LANES),
        whole(A_GROUPS, 1, A_GDIM), whole(A_GROUPS, 1, A_GDIM), whole(1, B_DIM), whole(1, B_HDIM),
        whole((N_LEVELS + 1) * CHUNK, 3 * CHUNK), whole(N_LEVELS + 1, CHUNK, CHUNK),
    ]
    args = [x, s0] + list(consts)
    out_shape = [jax.ShapeDtypeStruct(x.shape, F32),
                 jax.ShapeDtypeStruct((n_batch * n_seq, B_HEADS, B_HDIM, B_HDIM), F32)]
    out_specs = [pl.BlockSpec((rows, D_MODEL), tile),
                 pl.BlockSpec((n_seq, B_HEADS, B_HDIM, B_HDIM), lambda b, j: (b, 0, 0, 0))]
    if emit_v:
        out_shape.append(jax.ShapeDtypeStruct((x.shape[0], A_DIM), F32))
        out_specs.append(pl.BlockSpec((rows, A_DIM), tile))
    body = functools.partial(_even_mixer_body, n_seq=n_seq, seq_rows=seq_rows, gchunk=gchunk, emit_v=emit_v)
    return pl.pallas_call(
        body, grid=grid, in_specs=in_specs, out_specs=out_specs, out_shape=out_shape,
        scratch_shapes=[pltpu.VMEM((rows, EVEN_IN), F32), pltpu.VMEM((rows, D_MODEL), F32),
                        pltpu.VMEM((n_seq * B_HEADS, B_HDIM, B_HDIM), F32),
                        pltpu.VMEM((D_MODEL, EVEN_IN), BF16), pltpu.VMEM((D_MODEL, D_MODEL), BF16)],
        compiler_params=_cparams(2),
        name="even_mixer_s" if emit_v else "even_mixer_p")(*args)


def _memory_kv_body(mem_ref, gm_ref, wk_ref, wv_ref, kg_ref, mk_ref, mv_ref, *, n_seq, n_mem):
    m = _rms(mem_ref[...], gm_ref[0]).astype(BF16)
    kk = _dot(m, wk_ref[0].astype(BF16))
    vv = _dot(m, wv_ref[0].astype(BF16))
    kg = kg_ref[0]
    for hd in range(X_HEADS):
        hs = slice(hd * X_HDIM, (hd + 1) * X_HDIM)
        kh = _rms(kk[:, hs], kg)
        for s in range(n_seq):
            mk_ref[0, s, pl.ds(hd, n_mem, stride=X_HEADS), :] = kh[s * n_mem:(s + 1) * n_mem]
            mv_ref[0, s, pl.ds(hd, n_mem, stride=X_HEADS), :] = vv[s * n_mem:(s + 1) * n_mem, hs]


def _memory_kv(mem2d, g_mem, w_k, w_v, k_g, *, n_batch, n_mem):
    depth = w_k.shape[0]
    n_seq = ROW_TILE // n_mem
    out = jax.ShapeDtypeStruct((depth, n_batch, n_mem * X_HEADS, X_HDIM), F32)
    ospec = pl.BlockSpec((1, n_seq, n_mem * X_HEADS, X_HDIM), lambda l, t: (l, t, 0, 0))
    return pl.pallas_call(
        functools.partial(_memory_kv_body, n_seq=n_seq, n_mem=n_mem),
        grid=(depth, n_batch // n_seq),
        in_specs=[pl.BlockSpec((ROW_TILE, D_MODEL), lambda l, t: (t, 0)),
                  pl.BlockSpec((1, 1, D_MODEL), lambda l, t: (l, 0, 0)),
                  pl.BlockSpec((1, D_MODEL, X_DIM), lambda l, t: (l, 0, 0)),
                  pl.BlockSpec((1, D_MODEL, X_DIM), lambda l, t: (l, 0, 0)),
                  pl.BlockSpec((1, 1, X_HDIM), lambda l, t: (l, 0, 0))],
        out_specs=[ospec, ospec], out_shape=[out, out], compiler_params=_cparams(2),
        name="memory_kv")(mem2d, g_mem, w_k, w_v, k_g)


def _xattn_body(x_ref, mk_ref, mv_ref, nrm_ref, wq32_ref, wo32_ref, qg_ref, y_ref, o_scr, wq_ref, wo_ref, *, n_seq,
                seq_rows):
    @pl.when(pl.program_id(0) == 0)
    def _():
        wq_ref[...] = wq32_ref[...].astype(BF16)
        wo_ref[...] = wo32_ref[...].astype(BF16)

    x = x_ref[...]
    h = _rms(x, nrm_ref[...]).astype(BF16)
    q = _dot(h, wq_ref[...])
    qg = qg_ref[...] * (X_HDIM ** -0.5)
    units = [(s, hd) for s in range(n_seq) for hd in range(X_HEADS)]
    rows_of = lambda s: slice(s * seq_rows, (s + 1) * seq_rows)
    lanes_of = lambda hd: slice(hd * X_HDIM, (hd + 1) * X_HDIM)
    n_mem = mk_ref.shape[1] // X_HEADS
    head_rows = lambda hd: pl.ds(hd, n_mem, stride=X_HEADS)
    qh = [_rms(q[rows_of(s), lanes_of(hd)], qg).astype(BF16) for s, hd in units]
    sc = [_dot_nt(qh[n], mk_ref[s, head_rows(hd), :].astype(BF16)) for n, (s, hd) in enumerate(units)]
    m = [jnp.max(t, axis=-1, keepdims=True) for t in sc]
    p = [jnp.exp(t - mx) for t, mx in zip(sc, m)]
    den = [jnp.sum(t, axis=-1, keepdims=True) for t in p]
    for n, (s, hd) in enumerate(units):
        o_scr[rows_of(s), lanes_of(hd)] = _dot(p[n].astype(BF16), mv_ref[s, head_rows(hd), :].astype(BF16)) / den[n]
    y_ref[...] = x + _dot(o_scr[...].astype(BF16), wo_ref[...])


def _xattn(x, mk, mv, nrm, wq, wo, qg, *, layer, steps_per_mem, n_seq, seq_rows):
    rows = n_seq * seq_rows
    mem_spec = pl.BlockSpec((None, n_seq) + mk.shape[2:], lambda t: (layer, t // steps_per_mem, 0, 0))
    whole = lambda *shape: pl.BlockSpec(shape, lambda t: (0,) * len(shape))
    once = lambda *shape: pl.BlockSpec(shape, lambda t: (0,) * len(shape), pipeline_mode=pl.Buffered(1))
    return pl.pallas_call(
        functools.partial(_xattn_body, n_seq=n_seq, seq_rows=seq_rows),
        grid=(x.shape[0] // rows,),
        in_specs=[pl.BlockSpec((rows, D_MODEL), lambda t: (t, 0)), mem_spec, mem_spec,
                  whole(1, D_MODEL), once(D_MODEL, X_DIM), once(X_DIM, D_MODEL), whole(1, X_HDIM)],
        out_specs=pl.BlockSpec((rows, D_MODEL), lambda t: (t, 0)),
        out_shape=jax.ShapeDtypeStruct(x.shape, F32),
        scratch_shapes=[pltpu.VMEM((rows, X_DIM), F32), pltpu.VMEM((D_MODEL, X_DIM), BF16),
                        pltpu.VMEM((X_DIM, D_MODEL), BF16)],
        compiler_params=_cparams(1), name="xattn_s" if n_seq > 1 else "xattn_p")(x, mk, mv, nrm, wq, wo, qg)


def _two_source_specs(n_prompt_tiles, width=D_MODEL):
    return (pl.BlockSpec((ROW_TILE, width), lambda i, *_: (jnp.minimum(i, n_prompt_tiles - 1), 0)),
            pl.BlockSpec((ROW_TILE, width), lambda i, *_: (jnp.maximum(i - n_prompt_tiles, 0), 0)))


def _on_row_source(i, n_prompt_tiles, fn, prompt_refs, sample_refs):
    pl.when(i < n_prompt_tiles)(lambda: fn(*prompt_refs))
    pl.when(i >= n_prompt_tiles)(lambda: fn(*sample_refs))


def _swiglu_part(hb, w1, w3, w2):
    part = None
    for c0 in range(0, D_FF, FF_SUB):
        cs = slice(c0, min(c0 + FF_SUB, D_FF))
        act = (_silu(_dot(hb, w1[:, cs])) * _dot(hb, w3[:, cs])).astype(BF16)
        p = _dot(act, w2[cs, :])
        part = p if part is None else part + p
    return part


W_IN_STAGE_ROWS = 128
W_OUT_STAGE_ROWS = 256


def _load_weights_as_bf16(w1_hbm, w3_hbm, w2_hbm, w1_scr, w3_scr, w2_scr, in_stage, out_stage, sems):
    jobs = ([(w_hbm, w_scr, in_stage, r0, W_IN_STAGE_ROWS) for w_hbm, w_scr in ((w1_hbm, w1_scr), (w3_hbm, w3_scr))
             for r0 in range(0, D_MODEL, W_IN_STAGE_ROWS)]
            + [(w2_hbm, w2_scr, out_stage, r0, W_OUT_STAGE_ROWS) for r0 in range(0, D_FF, W_OUT_STAGE_ROWS)])

    def copy(n):
        w_hbm, _, stage, r0, n_rows = jobs[n]
        return pltpu.make_async_copy(w_hbm.at[pl.ds(r0, n_rows)], stage.at[n % 2], sems.at[n % 2])

    copy(0).start()
    for n in range(len(jobs)):
        if n + 1 < len(jobs):
            copy(n + 1).start()
        copy(n).wait()
        _, w_scr, stage, r0, n_rows = jobs[n]
        w_scr[pl.ds(r0, n_rows), :] = stage[n % 2].astype(BF16)


def _dense_ffn_body(xp_ref, xs_ref, nrm_ref, w1_hbm, w3_hbm, w2_hbm, yp_ref, ys_ref, w1_scr, w3_scr, w2_scr,
                    in_stage, out_stage, sems, *, n_prompt_tiles):
    i = pl.program_id(0)

    @pl.when(i == 0)
    def _():
        _load_weights_as_bf16(w1_hbm, w3_hbm, w2_hbm, w1_scr, w3_scr, w2_scr, in_stage, out_stage, sems)

    def run(x_ref, y_ref):
        x = x_ref[...]
        y_ref[...] = x + _swiglu_part(_rms(x, nrm_ref[...]).astype(BF16), w1_scr, w3_scr, w2_scr)

    _on_row_source(i, n_prompt_tiles, run, (xp_ref, yp_ref), (xs_ref, ys_ref))


def _dense_ffn(xp, xs, nrm, w1, w3, w2):
    n_prompt_tiles = xp.shape[0] // ROW_TILE
    n_tiles = n_prompt_tiles + xs.shape[0] // ROW_TILE
    in_hbm = pl.BlockSpec(memory_space=pl.ANY)
    return pl.pallas_call(
        functools.partial(_dense_ffn_body, n_prompt_tiles=n_prompt_tiles), grid=(n_tiles,),
        in_specs=[*_two_source_specs(n_prompt_tiles), pl.BlockSpec((1, D_MODEL), lambda i: (0, 0)),
                  in_hbm, in_hbm, in_hbm],
        out_specs=list(_two_source_specs(n_prompt_tiles)),
        out_shape=[jax.ShapeDtypeStruct(xp.shape, F32), jax.ShapeDtypeStruct(xs.shape, F32)],
        scratch_shapes=[pltpu.VMEM((D_MODEL, D_FF), BF16), pltpu.VMEM((D_MODEL, D_FF), BF16),
                        pltpu.VMEM((D_FF, D_MODEL), BF16), pltpu.VMEM((2, W_IN_STAGE_ROWS, D_FF), F32),
                        pltpu.VMEM((2, W_OUT_STAGE_ROWS, D_MODEL), F32), pltpu.SemaphoreType.DMA((2,))],
        compiler_params=_cparams(1), name="dense_ffn")(xp, xs, nrm, w1, w3, w2)


def _grouped_ffn_body(te_ref, tv_ref, x_ref, w1_ref, w3_ref, w2_ref, y_ref):
    valid = tv_ref[pl.program_id(0)]
    half = ROW_TILE // 2
    weights = (w1_ref.at[0], w3_ref.at[0], w2_ref.at[0])

    @pl.when(valid > half)
    def _():
        y_ref[...] = _swiglu_part(x_ref[...].astype(BF16), *weights)

    @pl.when((valid > 0) & (valid <= half))
    def _():
        y_ref[0:half] = _swiglu_part(x_ref[0:half].astype(BF16), *weights)
        y_ref[half:ROW_TILE] = jnp.zeros((ROW_TILE - half, D_MODEL), F32)

    @pl.when(valid == 0)
    def _():
        y_ref[...] = jnp.zeros_like(y_ref)


def _grouped_ffn(xsorted, tile_expert, tile_valid, w1, w3, w2):
    n_tiles = xsorted.shape[0] // ROW_TILE
    grid_spec = pltpu.PrefetchScalarGridSpec(
        num_scalar_prefetch=2, grid=(n_tiles,),
        in_specs=[pl.BlockSpec((ROW_TILE, D_MODEL), lambda i, te, tv: (i, 0)),
                  pl.BlockSpec((1, D_MODEL, D_FF), lambda i, te, tv: (te[i], 0, 0)),
                  pl.BlockSpec((1, D_MODEL, D_FF), lambda i, te, tv: (te[i], 0, 0)),
                  pl.BlockSpec((1, D_FF, D_MODEL), lambda i, te, tv: (te[i], 0, 0))],
        out_specs=pl.BlockSpec((ROW_TILE, D_MODEL), lambda i, te, tv: (i, 0)))
    return pl.pallas_call(
        _grouped_ffn_body, grid_spec=grid_spec, out_shape=jax.ShapeDtypeStruct(xsorted.shape, F32),
        compiler_params=_cparams(1), name="grouped_ffn")(tile_expert, tile_valid, xsorted, w1, w3, w2)


def _rope_slab(xs, cos, sin_signed, first_half):
    rot = jnp.where(first_half, pltpu.roll(xs, LANES - C_HDIM // 2, 1), pltpu.roll(xs, C_HDIM // 2, 1))
    return xs * cos + rot * sin_signed


def _swa_body(sink_ref, x_ref, pk_ref, pv_ref, cos_ref, sin_ref, nrm_ref, win32_ref, wout32_ref, qg_ref, kg_ref,
              bd_ref, y_ref, ko_ref, vo_ref, k_scr, v_scr, q_scr, a_scr, win_ref, wout_ref, *, n_seq, seq_rows,
              past_valid):
    rows = n_seq * seq_rows
    j = pl.program_id(1)
    tail = min(WINDOW, seq_rows)

    @pl.when((pl.program_id(0) == 0) & (j == 0))
    def _():
        win_ref[...] = win32_ref[...].astype(BF16)
        wout_ref[...] = wout32_ref[...].astype(BF16)

    if past_valid:
        for s in range(n_seq):
            k_scr[s, 0:WINDOW] = pk_ref[s]
            v_scr[s, 0:WINDOW] = pv_ref[s]
    else:
        @pl.when(j == 0)
        def _():
            for s in range(n_seq):
                k_scr[s, 0:WINDOW] = jnp.zeros((WINDOW, C_KV_DIM), F32)
                v_scr[s, 0:WINDOW] = jnp.zeros((WINDOW, C_KV_DIM), F32)

        @pl.when(j > 0)
        def _():
            for s in range(n_seq):
                k_scr[s, 0:WINDOW] = k_scr[s, seq_rows:seq_rows + WINDOW]
                v_scr[s, 0:WINDOW] = v_scr[s, seq_rows:seq_rows + WINDOW]

    x = x_ref[...]
    h = _rms(x, nrm_ref[...]).astype(BF16)
    q_dim = C_HEADS * C_HDIM
    q = _dot(h, win_ref[:, 0:q_dim])
    k = _dot(h, win_ref[:, q_dim:q_dim + C_KV_DIM])
    v = _dot(h, win_ref[:, q_dim + C_KV_DIM:q_dim + 2 * C_KV_DIM])

    cos = cos_ref[...]
    sin_signed = sin_ref[...]
    first_half = (lax.broadcasted_iota(jnp.int32, (rows, LANES), 1) % C_HDIM) < (C_HDIM // 2)
    bd = bd_ref[...]
    qn = q * lax.rsqrt(_dot((q * q).astype(BF16), bd) + EPS) * qg_ref[...]
    kn = k * lax.rsqrt(_dot((k * k).astype(BF16), bd[0:C_KV_DIM, 0:C_KV_DIM]) + EPS) * kg_ref[...]
    scale = C_HDIM ** -0.5
    for sl in range(q_dim // LANES):
        ls = slice(sl * LANES, (sl + 1) * LANES)
        q_scr[:, ls] = _rope_slab(qn[:, ls], cos, sin_signed, first_half) * scale
    for sl in range(C_KV_DIM // LANES):
        ls = slice(sl * LANES, (sl + 1) * LANES)
        kr = _rope_slab(kn[:, ls], cos, sin_signed, first_half)
        for s in range(n_seq):
            k_scr[s, WINDOW:WINDOW + seq_rows, ls] = kr[s * seq_rows:(s + 1) * seq_rows]
    for s in range(n_seq):
        v_scr[s, WINDOW:WINDOW + seq_rows] = v[s * seq_rows:(s + 1) * seq_rows]
        ko_ref[s] = k_scr[s, WINDOW + seq_rows - tail:WINDOW + seq_rows]
        vo_ref[s] = v_scr[s, WINDOW + seq_rows - tail:WINDOW + seq_rows]

    chunks_per_seq = seq_rows // CHUNK
    n_keys = WINDOW + CHUNK
    key_idx = lax.broadcasted_iota(jnp.int32, (C_GROUP * CHUNK, n_keys), 1)
    row_grp = lax.broadcasted_iota(jnp.int32, (C_GROUP * CHUNK, 1), 0) // CHUNK
    ones_keys = jnp.ones((n_keys, C_HDIM), BF16)

    head_sinks = []
    for kvh in range(C_KV_HEADS):
        sink = jnp.zeros((C_GROUP * CHUNK, 1), F32)
        for g in range(C_GROUP):
            sink = jnp.where(row_grp == g, sink_ref[kvh * C_GROUP + g], sink)
        head_sinks.append(sink)

    def attend(chunk_ids):
        units = [(idx, kvh) for idx in chunk_ids for kvh in range(C_KV_HEADS)]
        head_lanes = lambda kvh: slice(kvh * C_HDIM, (kvh + 1) * C_HDIM)
        q_rows = lambda idx: slice(idx * CHUNK, (idx + 1) * CHUNK)
        key_rows = lambda idx: slice((idx % chunks_per_seq) * CHUNK, (idx % chunks_per_seq) * CHUNK + n_keys)
        qs = [jnp.concatenate(
            [q_scr[q_rows(idx), (kvh * C_GROUP + g) * C_HDIM:(kvh * C_GROUP + g + 1) * C_HDIM]
             for g in range(C_GROUP)], axis=0).astype(BF16) for idx, kvh in units]
        kh = [k_scr[idx // chunks_per_seq, key_rows(idx), head_lanes(kvh)].astype(BF16) for idx, kvh in units]
        vh = [v_scr[idx // chunks_per_seq, key_rows(idx), head_lanes(kvh)].astype(BF16) for idx, kvh in units]
        sc = [_dot_nt(qn, kn) for qn, kn in zip(qs, kh)]
        if not past_valid:
            sc = [jnp.where((key_idx + (idx % chunks_per_seq) * CHUNK >= WINDOW) | (j > 0), sn, -jnp.inf)
                  for sn, (idx, kvh) in zip(sc, units)]
        m = [jnp.maximum(jnp.max(sn, axis=-1, keepdims=True), head_sinks[kvh]) for sn, (idx, kvh) in zip(sc, units)]
        p = [jnp.exp(sn - mn).astype(BF16) for sn, mn in zip(sc, m)]
        den = [_dot(pn, ones_keys) + jnp.exp(head_sinks[kvh] - mn) for pn, mn, (idx, kvh) in zip(p, m, units)]
        o = [_dot(pn, vn) / dn for pn, vn, dn in zip(p, vh, den)]
        for on, (idx, kvh) in zip(o, units):
            for g in range(C_GROUP):
                hs = slice((kvh * C_GROUP + g) * C_HDIM, (kvh * C_GROUP + g + 1) * C_HDIM)
                a_scr[q_rows(idx), hs] = on[g * CHUNK:(g + 1) * CHUNK]

    n_chunks = rows // CHUNK
    for first in range(0, n_chunks, SWA_STAGE_CHUNKS):
        attend(range(first, min(first + SWA_STAGE_CHUNKS, n_chunks)))
    y_ref[...] = x + _dot(a_scr[...].astype(BF16), wout_ref[...])


def _swa(x, pk, pv, cos, sin_signed, sinks, consts, *, n_batch, tiles_per_batch, n_seq, seq_rows, past_valid):
    rows = n_seq * seq_rows
    tail = min(WINDOW, seq_rows)
    n_str = n_batch * n_seq
    whole = lambda *shape: pl.BlockSpec(shape, lambda b, j, sk: (0,) * len(shape))
    once = lambda *shape: pl.BlockSpec(shape, lambda b, j, sk: (0,) * len(shape), pipeline_mode=pl.Buffered(1))
    in_specs = [pl.BlockSpec((rows, D_MODEL), lambda b, j, sk: (b * tiles_per_batch + j, 0)),
                pl.BlockSpec((n_seq, WINDOW, C_KV_DIM), lambda b, j, sk: (b, 0, 0)),
                pl.BlockSpec((n_seq, WINDOW, C_KV_DIM), lambda b, j, sk: (b, 0, 0)),
                pl.BlockSpec((rows, LANES), lambda b, j, sk: (j, 0)),
                pl.BlockSpec((rows, LANES), lambda b, j, sk: (j, 0)),
                whole(1, D_MODEL), once(D_MODEL, ODD_IN), once(D_MODEL, D_MODEL),
                whole(1, D_MODEL), whole(1, C_KV_DIM), whole(D_MODEL, D_MODEL)]
    args = [x, pk, pv, cos, sin_signed] + list(consts)
    kv_out = jax.ShapeDtypeStruct((n_str, tail, C_KV_DIM), F32)
    kv_spec = pl.BlockSpec((n_seq, tail, C_KV_DIM), lambda b, j, sk: (b, 0, 0))
    grid_spec = pltpu.PrefetchScalarGridSpec(
        num_scalar_prefetch=1, grid=(n_batch, tiles_per_batch), in_specs=in_specs,
        out_specs=[pl.BlockSpec((rows, D_MODEL), lambda b, j, sk: (b * tiles_per_batch + j, 0)),
                   kv_spec, kv_spec],
        scratch_shapes=[pltpu.VMEM((n_seq, WINDOW + seq_rows, C_KV_DIM), F32),
                        pltpu.VMEM((n_seq, WINDOW + seq_rows, C_KV_DIM), F32),
                        pltpu.VMEM((rows, D_MODEL), F32), pltpu.VMEM((rows, D_MODEL), F32),
                        pltpu.VMEM((D_MODEL, ODD_IN), BF16), pltpu.VMEM((D_MODEL, D_MODEL), BF16)])
    return pl.pallas_call(
        functools.partial(_swa_body, n_seq=n_seq, seq_rows=seq_rows, past_valid=past_valid),
        grid_spec=grid_spec,
        out_shape=[jax.ShapeDtypeStruct(x.shape, F32), kv_out, kv_out],
        compiler_params=_cparams(2),
        name="swa_s" if past_valid else "swa_p")(sinks, *args)


def _cast_steps(n_grid_steps):
    return 1 << (n_grid_steps.bit_length() - 1)


def _cast_piece_rows(w, n_cast):
    assert w.shape[0] % (n_cast * 16) == 0
    return w.shape[0] // n_cast


def _cast_rows_step(step, src_hbm, dst_hbm, in_stage, out_stage, sems):
    n_rows = in_stage.shape[1]
    n_cast = src_hbm.shape[0] // n_rows
    slot = step % 2

    def read(s, sl):
        return pltpu.make_async_copy(src_hbm.at[pl.ds(pl.multiple_of(s * n_rows, 16), n_rows)], in_stage.at[sl],
                                     sems.at[sl])

    def write(s, sl):
        return pltpu.make_async_copy(out_stage.at[sl], dst_hbm.at[pl.ds(pl.multiple_of(s * n_rows, 16), n_rows)],
                                     sems.at[2 + sl])

    @pl.when(step == 0)
    def _():
        read(0, 0).start()

    @pl.when(step < n_cast)
    def _():
        @pl.when(step + 1 < n_cast)
        def _():
            read(step + 1, 1 - slot).start()

        read(step, slot).wait()

        @pl.when(step >= 2)
        def _():
            write(step - 2, slot).wait()

        out_stage[slot] = in_stage[slot].astype(BF16)
        write(step, slot).start()

        @pl.when(step == n_cast - 1)
        def _():
            write(step, slot).wait()
            if n_cast > 1:
                write(step - 1, 1 - slot).wait()


def _cast_scratch(w, n_cast):
    rows = _cast_piece_rows(w, n_cast)
    return [pltpu.VMEM((2, rows, w.shape[1]), F32), pltpu.VMEM((2, rows, w.shape[1]), BF16),
            pltpu.SemaphoreType.DMA((4,))]


def _router_body(xp_ref, xs_ref, nrm_ref, rboth_ref, lstrict_ref, su_ref, w_hbm, info_ref, cnt_ref, wb_hbm,
                 w_in, w_out, w_sems, *, n_prompt_tiles):
    _cast_rows_step(pl.program_id(0), w_hbm, wb_hbm, w_in, w_out, w_sems)
    def run(x_ref):
        h = _rms(x_ref[...], nrm_ref[...])
        hi = h.astype(BF16)
        lo = (h - hi.astype(F32)).astype(BF16)
        both = _dot(hi, rboth_ref[...])
        logits = both[:, 0:LANES] + both[:, LANES:2 * LANES] + _dot(lo, rboth_ref[:, 0:LANES])
        lane = lax.broadcasted_iota(jnp.int32, logits.shape, 1).astype(F32)
        logits = jnp.where(lane < N_EXPERTS, logits, -jnp.inf)
        l1 = jnp.max(logits, axis=-1, keepdims=True)
        i1 = jnp.min(jnp.where(logits == l1, lane, float(LANES)), axis=-1, keepdims=True)
        rest = jnp.where(lane == i1, -jnp.inf, logits)
        l2 = jnp.max(rest, axis=-1, keepdims=True)
        i2 = jnp.min(jnp.where(rest == l2, lane, float(LANES)), axis=-1, keepdims=True)
        e = jnp.exp(l2 - l1)
        g1 = 1.0 / (1.0 + e)
        g2 = e * g1
        oh1 = (lane == i1).astype(F32)
        oh2 = (lane == i2).astype(F32)
        cnt1 = jnp.sum(oh1, axis=0, keepdims=True)
        cnt2 = jnp.sum(oh2, axis=0, keepdims=True)
        seg = jnp.broadcast_to(jnp.ceil((cnt1 + cnt2) * (1.0 / GROUP)), (GROUP, LANES))
        off = _dot(seg.astype(BF16), su_ref[...])[0:1] * GROUP
        c1 = _dot(lstrict_ref[...], oh1.astype(BF16)) + off
        c2 = _dot(lstrict_ref[...], oh2.astype(BF16)) + off + cnt1
        d1 = jnp.sum(jnp.where(lane == i1, c1, 0.0), axis=-1, keepdims=True)
        d2 = jnp.sum(jnp.where(lane == i2, c2, 0.0), axis=-1, keepdims=True)
        info = jnp.zeros(logits.shape, F32)
        for n, col in enumerate([d1, d2, g1, g2]):
            info = jnp.where(lane == n, col, info)
        info_ref[...] = info
        cnt_ref[...] = seg * GROUP

    _on_row_source(pl.program_id(0), n_prompt_tiles, run, (xp_ref,), (xs_ref,))


def _router(xp, xs, nrm, rboth, lstrict, su, w_f32):
    n_prompt_tiles = xp.shape[0] // ROW_TILE
    n_tiles = n_prompt_tiles + xs.shape[0] // ROW_TILE
    whole = lambda *shape: pl.BlockSpec(shape, lambda i: (0,) * len(shape))
    in_hbm = pl.BlockSpec(memory_space=pl.ANY)
    return pl.pallas_call(
        functools.partial(_router_body, n_prompt_tiles=n_prompt_tiles), grid=(n_tiles,),
        in_specs=[*_two_source_specs(n_prompt_tiles), whole(1, D_MODEL), whole(D_MODEL, 2 * LANES),
                  whole(ROW_TILE, ROW_TILE), whole(LANES, LANES), in_hbm],
        out_specs=[pl.BlockSpec((ROW_TILE, LANES), lambda i: (i, 0)), pl.BlockSpec((GROUP, LANES), lambda i: (i, 0)),
                   in_hbm],
        out_shape=[jax.ShapeDtypeStruct((n_tiles * ROW_TILE, LANES), F32),
                   jax.ShapeDtypeStruct((n_tiles * GROUP, LANES), F32), jax.ShapeDtypeStruct(w_f32.shape, BF16)],
        scratch_shapes=_cast_scratch(w_f32, _cast_steps(n_tiles)),
        compiler_params=_cparams(1), name="router")(xp, xs, nrm, rboth, lstrict, su, w_f32)


BIG_ROWS = 8 * GROUP
SEG_FIELDS = 4
TILE_FIELDS = 2
TAIL_FIELDS = 3


def _rows_copy(src_ref, src_row, dst_ref, dst_row, n_rows, sem):
    return pltpu.make_async_copy(src_ref.at[pl.ds(pl.multiple_of(src_row, GROUP), n_rows)],
                                 dst_ref.at[pl.ds(pl.multiple_of(dst_row, GROUP), n_rows)], sem)


def _start_pieces(src_ref, src0, dst_ref, dst0, n_big, n_small, sem):
    def big(g, carry):
        _rows_copy(src_ref, src0 + g * BIG_ROWS, dst_ref, dst0 + g * BIG_ROWS, BIG_ROWS, sem).start()
        return carry

    lax.fori_loop(0, n_big, big, 0)
    done = n_big * BIG_ROWS

    def small(g, carry):
        _rows_copy(src_ref, src0 + done + g * GROUP, dst_ref, dst0 + done + g * GROUP, GROUP, sem).start()
        return carry

    lax.fori_loop(0, n_small, small, 0)


def _wait_pieces(n_big, n_small, src_ref, dst_ref, sem):
    def big(g, carry):
        _rows_copy(src_ref, 0, dst_ref, 0, BIG_ROWS, sem).wait()
        return carry

    lax.fori_loop(0, n_big, big, 0)

    def small(g, carry):
        _rows_copy(src_ref, 0, dst_ref, 0, GROUP, sem).wait()
        return carry

    lax.fori_loop(0, n_small, small, 0)


def _segment_table(seg, n_sorted_tiles):
    pieces = lambda rows: (rows // BIG_ROWS, (rows % BIG_ROWS) // GROUP)
    local_off = jnp.cumsum(seg, axis=1) - seg
    expert_rows = jnp.sum(seg, axis=0)
    expert_pad = ((expert_rows + ROW_TILE - 1) // ROW_TILE) * ROW_TILE
    expert_end = jnp.cumsum(expert_pad)
    expert_off = expert_end - expert_pad
    seg_start = expert_off[None, :] + jnp.cumsum(seg, axis=0) - seg
    n_used = (expert_end[-1] // ROW_TILE).reshape(1)
    tile_ids = jnp.arange(n_sorted_tiles, dtype=jnp.int32)
    tile_expert = jnp.minimum(jnp.sum(tile_ids[:, None] >= (expert_end // ROW_TILE)[None, :], axis=1), N_EXPERTS - 1)
    tile_expert = jnp.where(tile_ids < n_used[0], tile_expert, tile_expert[n_used[0] - 1])
    tile_valid = jnp.clip((expert_off + expert_rows)[tile_expert] - tile_ids * ROW_TILE, 0, ROW_TILE)
    tile_valid = jnp.where(tile_ids < n_used[0], tile_valid, 0)
    seg_big, seg_small = pieces(seg)
    tail_big, tail_small = pieces(expert_pad - expert_rows)
    table = jnp.concatenate([
        jnp.stack([seg_start, seg_big, seg_small, local_off], axis=-1).reshape(-1),
        jnp.stack([jnp.sum(seg_big, axis=1), jnp.sum(seg_small, axis=1)], axis=-1).reshape(-1),
        jnp.stack([expert_off + expert_rows, tail_big, tail_small], axis=-1).reshape(-1),
        n_used])
    return table.astype(jnp.int32), tile_expert.astype(jnp.int32), tile_valid.astype(jnp.int32)


def _table_sections(n_tiles):
    tiles0 = SEG_FIELDS * n_tiles * N_EXPERTS
    tails0 = tiles0 + TILE_FIELDS * n_tiles
    return tiles0, tails0, tails0 + TAIL_FIELDS * N_EXPERTS


def _dispatch_body(tab_ref, info_ref, xp_ref, xs_ref, nrm_ref, sel_ref, wa_hbm, wb_hbm, out_ref, wa_bf_hbm, wb_bf_hbm,
                   loc_scr, z_scr, sems, wa_in, wa_out, wa_sems, wb_in, wb_out, wb_sems, *, n_prompt_tiles, n_tiles,
                   n_sorted_tiles):
    i = pl.program_id(0)
    slot = i % 2
    tiles0, tails0, used0 = _table_sections(n_tiles)
    _cast_rows_step(i, wa_hbm, wa_bf_hbm, wa_in, wa_out, wa_sems)
    _cast_rows_step(i, wb_hbm, wb_bf_hbm, wb_in, wb_out, wb_sems)

    def wait_tile(t, s):
        _wait_pieces(tab_ref[tiles0 + TILE_FIELDS * t], tab_ref[tiles0 + TILE_FIELDS * t + 1], loc_scr.at[s], out_ref,
                     sems.at[s])

    @pl.when(i >= 2)
    def _():
        wait_tile(i - 2, slot)

    def run(x_ref):
        hb = _rms(x_ref[...], nrm_ref[...]).astype(BF16)
        info = info_ref[...]
        lane = lax.broadcasted_iota(jnp.int32, info.shape, 1)
        dest = jnp.where(lane < TOP_K, info, 0.0)
        drow = sum(_dot_nt(sel_ref[...], part) for part in _split3(dest))
        r = lax.broadcasted_iota(jnp.int32, (LOCAL_ROWS, ROW_TILE), 0).astype(F32)
        perm = jnp.where(r == drow[0:1], 1.0, jnp.where(r == drow[1:2], 1.0, 0.0)).astype(BF16)
        loc_scr[slot] = _dot(perm, hb)

    _on_row_source(i, n_prompt_tiles, run, (xp_ref,), (xs_ref,))

    for e in range(N_EXPERTS):
        base = SEG_FIELDS * (i * N_EXPERTS + e)
        _start_pieces(loc_scr.at[slot], tab_ref[base + 3], out_ref, tab_ref[base], tab_ref[base + 1],
                      tab_ref[base + 2], sems.at[slot])

    @pl.when(i == n_tiles - 1)
    def _():
        wait_tile(i, slot)
        if n_tiles > 1:
            wait_tile(i - 1, 1 - slot)
        z_scr[...] = jnp.zeros_like(z_scr)
        for e in range(N_EXPERTS):
            base = tails0 + TAIL_FIELDS * e
            _start_pieces(z_scr, 0, out_ref, tab_ref[base], tab_ref[base + 1], tab_ref[base + 2], sems.at[2])
            _wait_pieces(tab_ref[base + 1], tab_ref[base + 2], z_scr, out_ref, sems.at[2])

        def zero_tile(t, carry):
            cp = _rows_copy(z_scr, 0, out_ref, t * ROW_TILE, ROW_TILE, sems.at[2])
            cp.start()
            cp.wait()
            return carry

        lax.fori_loop(tab_ref[used0], n_sorted_tiles, zero_tile, 0)


def _dispatch(table, info, xp, xs, nrm, sel, n_sorted_tiles, wa_f32, wb_f32):
    n_prompt_tiles = xp.shape[0] // ROW_TILE
    n_tiles = n_prompt_tiles + xs.shape[0] // ROW_TILE
    n_cast = _cast_steps(n_tiles)
    whole = lambda *shape: pl.BlockSpec(shape, lambda i, tab: (0,) * len(shape))
    in_hbm = pl.BlockSpec(memory_space=pl.ANY)
    grid_spec = pltpu.PrefetchScalarGridSpec(
        num_scalar_prefetch=1, grid=(n_tiles,),
        in_specs=[pl.BlockSpec((ROW_TILE, LANES), lambda i, tab: (i, 0)), *_two_source_specs(n_prompt_tiles),
                  whole(1, D_MODEL), whole(GROUP, LANES), in_hbm, in_hbm],
        out_specs=[in_hbm, in_hbm, in_hbm],
        scratch_shapes=[pltpu.VMEM((2, LOCAL_ROWS, D_MODEL), F32), pltpu.VMEM((ROW_TILE, D_MODEL), F32),
                        pltpu.SemaphoreType.DMA((3,))] + _cast_scratch(wa_f32, n_cast) + _cast_scratch(wb_f32, n_cast))
    return pl.pallas_call(
        functools.partial(_dispatch_body, n_prompt_tiles=n_prompt_tiles, n_tiles=n_tiles,
                          n_sorted_tiles=n_sorted_tiles),
        grid_spec=grid_spec,
        out_shape=[jax.ShapeDtypeStruct((n_sorted_tiles * ROW_TILE, D_MODEL), F32),
                   jax.ShapeDtypeStruct(wa_f32.shape, BF16), jax.ShapeDtypeStruct(wb_f32.shape, BF16)],
        compiler_params=_cparams(1), name="moe_dispatch")(table, info, xp, xs, nrm, sel, wa_f32, wb_f32)


def _combine_body(tab_ref, info_ref, xp_ref, xs_ref, e_ref, yp_ref, ys_ref, loc_scr, sems, *, n_prompt_tiles,
                  n_tiles):
    i = pl.program_id(0)
    slot = i % 2
    tiles0, _, _ = _table_sections(n_tiles)

    def fetch(t, s):
        for e in range(N_EXPERTS):
            base = SEG_FIELDS * (t * N_EXPERTS + e)
            _start_pieces(e_ref, tab_ref[base], loc_scr.at[s], tab_ref[base + 3], tab_ref[base + 1],
                          tab_ref[base + 2], sems.at[s])

    @pl.when(i == 0)
    def _():
        loc_scr[...] = jnp.zeros_like(loc_scr)
        fetch(0, 0)

    @pl.when(i + 1 < n_tiles)
    def _():
        fetch(i + 1, 1 - slot)

    _wait_pieces(tab_ref[tiles0 + TILE_FIELDS * i], tab_ref[tiles0 + TILE_FIELDS * i + 1], e_ref, loc_scr.at[slot],
                 sems.at[slot])

    def run(x_ref, y_ref):
        info = info_ref[...]
        eb = loc_scr[slot].astype(BF16)
        lane = lax.broadcasted_iota(jnp.int32, (ROW_TILE, LOCAL_ROWS), 1).astype(F32)
        pick = jnp.zeros((ROW_TILE, LOCAL_ROWS), F32)
        for k in range(TOP_K):
            pick = jnp.where(lane == info[:, k:k + 1], info[:, TOP_K + k:TOP_K + k + 1], pick)
        y_ref[...] = x_ref[...] + _dot(pick.astype(BF16), eb)

    _on_row_source(i, n_prompt_tiles, run, (xp_ref, yp_ref), (xs_ref, ys_ref))


def _combine(table, info, xp, xs, esorted):
    n_prompt_tiles = xp.shape[0] // ROW_TILE
    n_tiles = n_prompt_tiles + xs.shape[0] // ROW_TILE
    grid_spec = pltpu.PrefetchScalarGridSpec(
        num_scalar_prefetch=1, grid=(n_tiles,),
        in_specs=[pl.BlockSpec((ROW_TILE, LANES), lambda i, tab: (i, 0)), *_two_source_specs(n_prompt_tiles),
                  pl.BlockSpec(memory_space=pl.ANY)],
        out_specs=list(_two_source_specs(n_prompt_tiles)),
        scratch_shapes=[pltpu.VMEM((2, LOCAL_ROWS, D_MODEL), F32), pltpu.SemaphoreType.DMA((2,))])
    return pl.pallas_call(
        functools.partial(_combine_body, n_prompt_tiles=n_prompt_tiles, n_tiles=n_tiles),
        grid_spec=grid_spec,
        out_shape=[jax.ShapeDtypeStruct(xp.shape, F32), jax.ShapeDtypeStruct(xs.shape, F32)],
        compiler_params=_cparams(1), name="moe_combine")(table, info, xp, xs, esorted)


def _rope_tables(pos):
    half = C_HDIM // 2
    inv = 1.0 / (ROPE_THETA ** (jnp.arange(half, dtype=F32) / half))
    ang = pos.astype(F32)[:, None] * inv[None, :]
    cos = jnp.cos(ang)
    sin = jnp.sin(ang)
    reps = LANES // C_HDIM
    return jnp.tile(jnp.concatenate([cos, cos], axis=-1), (1, reps)), jnp.tile(jnp.concatenate([-sin, sin], axis=-1),
                                                                                 (1, reps))


def kernel(x_prompt, x_sample, mem_prompt, cache_mem_k, cache_mem_v, state_hgrn, cache_swa_k, cache_swa_v, norm_mix, norm_xattn, norm_ffn, even_w_in, even_w_out, gmlp_w_s, gmlp_b_s, gmlp_ln_g, gmlp_ln_b, hgrn_lb_logits, hgrn_out_norm, attn_w_in, attn_w_out, attn_q_norm, attn_k_norm, attn_sinks, xattn_mem_norm, xattn_w_q, xattn_w_k, xattn_w_v, xattn_w_o, xattn_q_norm, xattn_k_norm, ffn_w1, ffn_w3, ffn_w2, moe_router, moe_w1, moe_w3, moe_w2):
    n_batch, seq, d = x_prompt.shape
    dec_batch, dec_seq, _ = x_sample.shape
    n_mem = mem_prompt.shape[1]
    depth = norm_mix.shape[0]
    past_len = PAST_LEN
    assert d == D_MODEL and depth == 2 and seq % ROW_TILE == 0 and dec_batch * dec_seq == ROW_TILE
    assert dec_seq == CHUNK and ROW_TILE % n_mem == 0 and cache_swa_k.shape[2] == WINDOW
    assert even_w_in.shape[-1] == EVEN_IN and attn_w_in.shape[-1] == ODD_IN and ffn_w1.shape[-1] == D_FF
    assert moe_w1.shape[1] == N_EXPERTS
    n_prompt_rows = n_batch * seq
    n_rows = n_prompt_rows + dec_batch * dec_seq
    n_prompt_tiles = n_prompt_rows // ROW_TILE
    tiles_per_batch = seq // ROW_TILE
    row = lambda g: g.reshape(1, -1).astype(F32)

    lb_all = jnp.cumsum(jax.nn.softmax(hgrn_lb_logits.astype(F32), axis=0), axis=0)
    wcum = jnp.asarray(_cumsum_matrix(), BF16)
    lmask = jnp.asarray(_level_masks(), F32)

    def even_consts(n):
        tril = jnp.tril(jnp.ones((n, n), bool))
        ws = jnp.where(tril[None], gmlp_w_s[0, :, :n, :n], 0.0).astype(BF16)
        bs = jnp.broadcast_to(gmlp_b_s[0, :, :n, None], (A_GROUPS, n, LANES)).astype(F32)
        return [row(norm_mix[0]), even_w_in[0], even_w_out[0], ws, bs,
                gmlp_ln_g[0].reshape(A_GROUPS, 1, A_GDIM), gmlp_ln_b[0].reshape(A_GROUPS, 1, A_GDIM),
                row(lb_all[0]), row(hgrn_out_norm[0]), wcum, lmask]

    xp, hgrn_p = _even_mixer(
        x_prompt.reshape(n_prompt_rows, d), jnp.zeros((n_batch, B_HEADS, B_HDIM, B_HDIM), F32),
        even_consts(A_CHUNK), n_batch=n_batch, tiles_per_batch=tiles_per_batch, n_seq=1, seq_rows=ROW_TILE,
        gchunk=A_CHUNK, emit_v=False)
    xs, hgrn_s, gmlp_v = _even_mixer(
        x_sample.reshape(ROW_TILE, d), state_hgrn[0], even_consts(min(A_CHUNK, dec_seq)), n_batch=1,
        tiles_per_batch=1, n_seq=dec_batch, seq_rows=dec_seq, gchunk=min(A_CHUNK, dec_seq), emit_v=True)

    mem_k, mem_v = _memory_kv(mem_prompt.reshape(n_batch * n_mem, d), xattn_mem_norm.reshape(depth, 1, d),
                              xattn_w_k, xattn_w_v, xattn_k_norm.reshape(depth, 1, X_HDIM),
                              n_batch=n_batch, n_mem=n_mem)

    def cross_attention(xp, xs, l):
        consts = (row(norm_xattn[l]), xattn_w_q[l], xattn_w_o[l], row(xattn_q_norm[l]))
        xp = _xattn(xp, mem_k, mem_v, *consts, layer=l, steps_per_mem=tiles_per_batch, n_seq=1, seq_rows=ROW_TILE)
        xs = _xattn(xs, cache_k, cache_v, *consts, layer=l, steps_per_mem=1, n_seq=dec_batch, seq_rows=dec_seq)
        return xp, xs

    cache_k = cache_mem_k.reshape(depth, dec_batch, n_mem * X_HEADS, X_HDIM)
    cache_v = cache_mem_v.reshape(depth, dec_batch, n_mem * X_HEADS, X_HDIM)

    xp, xs = cross_attention(xp, xs, 0)
    xp, xs = _dense_ffn(xp, xs, row(norm_ffn[0]), ffn_w1[0], ffn_w3[0], ffn_w2[0])

    reps = D_MODEL // C_HDIM
    bd = jnp.asarray(np.kron(np.eye(reps, dtype=np.float32), np.full((C_HDIM, C_HDIM), 1.0 / C_HDIM, np.float32)),
                     BF16)
    swa_consts = [row(norm_mix[1]), attn_w_in[0], attn_w_out[0], row(jnp.tile(attn_q_norm[0], C_HEADS)),
                  row(jnp.tile(attn_k_norm[0], C_KV_HEADS)), bd]
    sinks = attn_sinks[0].astype(F32)
    cos_p, sin_p = _rope_tables(jnp.arange(seq, dtype=jnp.int32))
    cos_s, sin_s = _rope_tables(past_len + jnp.arange(dec_seq, dtype=jnp.int32))
    no_past = jnp.zeros((n_batch, WINDOW, C_KV_DIM), F32)
    xp, swk_p, swv_p = _swa(xp, no_past, no_past, cos_p, sin_p, sinks, swa_consts, n_batch=n_batch,
                            tiles_per_batch=tiles_per_batch, n_seq=1, seq_rows=ROW_TILE, past_valid=False)
    xs, swk_s, swv_s = _swa(xs, cache_swa_k[0].reshape(dec_batch, WINDOW, C_KV_DIM),
                            cache_swa_v[0].reshape(dec_batch, WINDOW, C_KV_DIM),
                            jnp.tile(cos_s, (dec_batch, 1)), jnp.tile(sin_s, (dec_batch, 1)), sinks, swa_consts,
                            n_batch=1, tiles_per_batch=1, n_seq=dec_batch, seq_rows=dec_seq, past_valid=True)
    xp, xs = cross_attention(xp, xs, 1)

    n_tiles = n_rows // ROW_TILE
    router_w = jnp.zeros((d, LANES), F32).at[:, :N_EXPERTS].set(moe_router[0].astype(F32))
    rhi = router_w.astype(BF16)
    rlo = (router_w - rhi.astype(F32)).astype(BF16)
    lstrict = jnp.asarray(np.tril(np.ones((ROW_TILE, ROW_TILE), np.float32), -1), BF16)
    su = jnp.asarray(np.triu(np.ones((LANES, LANES), np.float32), 1), BF16)
    sel = jnp.asarray(np.eye(GROUP, LANES, dtype=np.float32), BF16)
    nrm_ffn = row(norm_ffn[1])
    rows2d = lambda w: w.reshape(-1, w.shape[-1])
    info, seg, w1b = _router(xp, xs, nrm_ffn, jnp.concatenate([rhi, rlo], axis=1), lstrict, su, rows2d(moe_w1[0]))
    seg = seg.reshape(n_tiles, GROUP, LANES)[:, 0, :N_EXPERTS].astype(jnp.int32)
    n_sorted_tiles = -(-(TOP_K * n_rows + n_tiles * N_EXPERTS * (GROUP - 1) + N_EXPERTS * (ROW_TILE - 1)) // ROW_TILE)
    table, tile_expert, tile_valid = _segment_table(seg, n_sorted_tiles)
    hsorted, w3b, w2b = _dispatch(table, info, xp, xs, nrm_ffn, sel, n_sorted_tiles, rows2d(moe_w3[0]),
                                  rows2d(moe_w2[0]))
    esorted = _grouped_ffn(hsorted, tile_expert, tile_valid, w1b.reshape(moe_w1.shape[1:]),
                           w3b.reshape(moe_w3.shape[1:]), w2b.reshape(moe_w2.shape[1:]))
    y_prompt, y_sample = _combine(table, info, xp, xs, esorted)

    n_even = state_hgrn.shape[0]
    n_odd = cache_swa_k.shape[0]
    return (y_prompt.reshape(n_batch, seq, d), y_sample.reshape(dec_batch, dec_seq, d),
            mem_k.reshape(depth, n_batch, n_mem, X_HEADS, X_HDIM), mem_v.reshape(depth, n_batch, n_mem, X_HEADS, X_HDIM),
            hgrn_p.reshape(n_even, n_batch, B_HEADS, B_HDIM, B_HDIM),
            gmlp_v.reshape(n_even, dec_batch, dec_seq, A_GROUPS, A_GDIM),
            hgrn_s.reshape(n_even, dec_batch, B_HEADS, B_HDIM, B_HDIM),
            swk_p.reshape(n_odd, n_batch, WINDOW, C_KV_HEADS, C_HDIM), swv_p.reshape(n_odd, n_batch, WINDOW, C_KV_HEADS, C_HDIM),
            swk_s.reshape(n_odd, dec_batch, dec_seq, C_KV_HEADS, C_HDIM), swv_s.reshape(n_odd, dec_batch, dec_seq, C_KV_HEADS, C_HDIM))
```

```python
import functools

import numpy as np
import jax
import jax.numpy as jnp
from jax import lax
from jax.experimental import pallas as pl
from jax.experimental.pallas import tpu as pltpu

F32 = jnp.float32
BF16 = jnp.bfloat16

D_MODEL = 1024
EPS = 1e-6
LOG2_E = 1.4426950408889634
CHUNK = 64
A_GROUPS = 4
A_DIM = D_MODEL // 2
A_GDIM = A_DIM // A_GROUPS
A_CHUNK = 128
B_HEADS = 4
B_DIM = D_MODEL // 2
B_HDIM = B_DIM // B_HEADS
EVEN_IN = 2 * A_DIM + 4 * B_DIM
C_HEADS = 16
C_KV_HEADS = 4
C_HDIM = D_MODEL // C_HEADS
C_GROUP = C_HEADS // C_KV_HEADS
C_KV_DIM = C_KV_HEADS * C_HDIM
WINDOW = 128
ROPE_THETA = 10000.0
PAST_LEN = 4096
ODD_IN = (C_HEADS + 2 * C_KV_HEADS) * C_HDIM
X_HEADS = 4
X_HDIM = 128
X_DIM = X_HEADS * X_HDIM
D_FF = 2816
N_EXPERTS = 8
TOP_K = 2

LANES = 128
ROW_TILE = 512
FF_SUB = 512
N_LEVELS = 6
HGRN_STAGE_CHUNKS = 2
SWA_STAGE_CHUNKS = 2
GROUP = 8
LOCAL_ROWS = -(-(TOP_K * ROW_TILE + N_EXPERTS * (GROUP - 1)) // LANES) * LANES
VMEM_LIMIT = 58 * 1024 * 1024


def _cparams(n_axes):
    return pltpu.CompilerParams(dimension_semantics=("arbitrary",) * n_axes, vmem_limit_bytes=VMEM_LIMIT)


def _dot(a, b):
    return jnp.dot(a, b, preferred_element_type=F32)


def _dot_nt(a, b):
    return lax.dot_general(a, b, (((1,), (1,)), ((), ())), preferred_element_type=F32)


def _rms(x, g):
    return x * lax.rsqrt(jnp.mean(x * x, axis=-1, keepdims=True) + EPS) * g


def _silu(x):
    return x * (1.0 / (1.0 + jnp.exp(-x)))


def _split3(x):
    hi = x.astype(BF16)
    r1 = x - hi.astype(F32)
    mid = r1.astype(BF16)
    lo = (r1 - mid.astype(F32)).astype(BF16)
    return hi, mid, lo


def _cumsum_matrix():
    r = np.arange(CHUNK)
    s = np.arange(CHUNK)
    blocks = [(s[None, :] <= r[:, None])]
    for l in range(N_LEVELS):
        h = 1 << l
        ref = (r & ~(2 * h - 1)) + h - 1
        blocks.append(s[None, :] <= ref[:, None])
    w = np.concatenate(blocks, axis=0).astype(np.float32)
    return np.concatenate([w, w, w], axis=1)


def _level_masks():
    t = np.arange(CHUNK)[:, None]
    s = np.arange(CHUNK)[None, :]
    masks = []
    for l in range(N_LEVELS):
        masks.append(((t >> (l + 1)) == (s >> (l + 1))) & (((t >> l) & 1) == 1) & (((s >> l) & 1) == 0))
    masks.append(t == s)
    return np.stack(masks).astype(np.float32)


def _even_mixer_body(x_ref, s0_ref, nrm_ref, win32_ref, wout32_ref, ws_ref, bs_ref, lng_ref, lnb_ref, lb_ref, og_ref,
                     wcum_ref, lmask_ref, *rest, n_seq, seq_rows, gchunk, emit_v, side_cast):
    rest = list(rest)
    side_src = rest.pop(0) if side_cast else None
    y_ref, sout_ref = rest.pop(0), rest.pop(0)
    v_ref = rest.pop(0) if emit_v else None
    side_dst = rest.pop(0) if side_cast else None
    proj_scr, mixed_scr, st_scr, win_ref, wout_ref = rest[:5]
    rows = n_seq * seq_rows
    j = pl.program_id(1)
    if side_cast:
        _cast_rows_step(pl.program_id(0) * pl.num_programs(1) + j, side_src, side_dst, *rest[5:])

    @pl.when((pl.program_id(0) == 0) & (j == 0))
    def _():
        win_ref[...] = win32_ref[...].astype(BF16)
        wout_ref[...] = wout32_ref[...].astype(BF16)

    @pl.when(j == 0)
    def _():
        for s in range(n_seq):
            for hd in range(B_HEADS):
                st_scr[s * B_HEADS + hd] = s0_ref[s, hd].T

    x = x_ref[...]
    h = _rms(x, nrm_ref[...]).astype(BF16)
    n_pieces = EVEN_IN // ROW_TILE
    for n in range(n_pieces):
        cs = slice(n * ROW_TILE, (n + 1) * ROW_TILE)
        proj_scr[:, cs] = _dot(h, win_ref[:, cs])

    groups = range(A_GROUPS)
    group_cols = lambda base, g: slice(base + g * A_GDIM, base + (g + 1) * A_GDIM)
    vgs = [jax.nn.gelu(proj_scr[:, group_cols(A_DIM, g)]) for g in groups]
    means = [jnp.mean(vg, axis=-1, keepdims=True) for vg in vgs]
    vcs = [vg - mu for vg, mu in zip(vgs, means)]
    variances = [jnp.mean(vc * vc, axis=-1, keepdims=True) for vc in vcs]
    for g in groups:
        gs = group_cols(0, g)
        vn = vcs[g] * lax.rsqrt(variances[g] + EPS) * lng_ref[g] + lnb_ref[g]
        if v_ref is not None:
            v_ref[:, gs] = vn
        ug = jax.nn.gelu(proj_scr[:, gs])
        vb = vn.astype(BF16)
        for c in range(rows // gchunk):
            rs = slice(c * gchunk, (c + 1) * gchunk)
            sp = _dot(ws_ref[g], vb[rs]) + bs_ref[g]
            mixed_scr[rs, gs] = ug[rs] * sp

    q0, f0, i0, g0 = (2 * A_DIM + k * B_DIM for k in range(4))
    lb = lb_ref[...]
    fg = lb + (1.0 - lb) * jax.nn.sigmoid(proj_scr[:, f0:f0 + B_DIM])
    proj_scr[:, 0:B_DIM] = jnp.log(fg) * LOG2_E
    proj_scr[:, B_DIM:2 * B_DIM] = 1.0 - fg
    proj_scr[:, q0:q0 + B_DIM] = _silu(proj_scr[:, q0:q0 + B_DIM])
    og = og_ref[...]
    chunks_per_seq = seq_rows // CHUNK

    heads = range(B_HEADS)
    head_cols = lambda base, hd: slice(base + hd * B_HDIM, base + (hd + 1) * B_HDIM)
    chunk_rows = lambda c: slice(c * CHUNK, (c + 1) * CHUNK)

    def chunk_local(c):
        rs = chunk_rows(c)
        hi, mid, lo = _split3(proj_scr[rs, 0:B_DIM])
        gg = _dot(wcum_ref[...], jnp.concatenate([hi, mid, lo], axis=0))
        vs = [proj_scr[rs, head_cols(i0, hd)] for hd in heads]
        vts = [v.T.astype(BF16) for v in vs]
        Gs = [gg[0:CHUNK, head_cols(0, hd)] for hd in heads]
        qs = [proj_scr[rs, head_cols(q0, hd)] for hd in heads]
        ks = [proj_scr[rs, head_cols(B_DIM, hd)] for hd in heads]
        operands = []
        for hd in heads:
            for l in range(N_LEVELS + 1):
                if l < N_LEVELS:
                    e = jnp.exp2(-jnp.abs(Gs[hd] - gg[(l + 1) * CHUNK:(l + 2) * CHUNK, head_cols(0, hd)]))
                    operands.append(((qs[hd] * e).astype(BF16), (ks[hd] * e).astype(BF16)))
                else:
                    operands.append((qs[hd].astype(BF16), ks[hd].astype(BF16)))
        blocks = [_dot_nt(qe, ke) for qe, ke in operands]
        parts = []
        for hd in heads:
            att = jnp.zeros((CHUNK, CHUNK), F32)
            for l in range(N_LEVELS + 1):
                att = jnp.where(lmask_ref[l] > 0.5, blocks[hd * (N_LEVELS + 1) + l], att)
            g_end = Gs[hd][CHUNK - 1:CHUNK, :]
            kd = (ks[hd] * jnp.exp2(g_end - Gs[hd])).astype(BF16)
            parts.append((_dot(att.astype(BF16), vs[hd].astype(BF16)), (qs[hd] * jnp.exp2(Gs[hd])).astype(BF16),
                          jnp.exp2(g_end), _dot(vts[hd], kd)))
        return parts

    def chunk_state(c, parts):
        rs = chunk_rows(c)
        sidx = (c // chunks_per_seq) * B_HEADS
        outs = []
        for hd in heads:
            o_local, q_decayed, decay, increment = parts[hd]
            st = st_scr[sidx + hd]
            outs.append(o_local + _dot_nt(q_decayed, st.astype(BF16)))
            st_scr[sidx + hd] = st * decay + increment
        for hd in heads:
            o = outs[hd]
            on = o * lax.rsqrt(jnp.mean(o * o, axis=-1, keepdims=True) + EPS) * og
            gate = _silu(proj_scr[rs, head_cols(g0, hd)])
            mixed_scr[rs, head_cols(A_DIM, hd)] = on * gate

    n_chunks = rows // CHUNK
    for first in range(0, n_chunks, HGRN_STAGE_CHUNKS):
        group = range(first, min(first + HGRN_STAGE_CHUNKS, n_chunks))
        local = [chunk_local(c) for c in group]
        for c, parts in zip(group, local):
            chunk_state(c, parts)

    y_ref[...] = x + _dot(mixed_scr[...].astype(BF16), wout_ref[...])

    @pl.when(j == pl.num_programs(1) - 1)
    def _():
        for s in range(n_seq):
            for hd in range(B_HEADS):
                sout_ref[s, hd] = st_scr[s * B_HEADS + hd].T


def _even_mixer(x, s0, consts, *, n_batch, tiles_per_batch, n_seq, seq_rows, gchunk, emit_v, side_cast=None):
    rows = n_seq * seq_rows
    grid = (n_batch, tiles_per_batch)
    tile = lambda b, j: (b * tiles_per_batch + j, 0)
    whole = lambda *shape: pl.BlockSpec(shape, lambda b, j: (0,) * len(shape))
    once = lambda *shape: pl.BlockSpec(shape, lambda b, j: (0,) * len(shape), pipeline_mode=pl.Buffered(1))
    in_specs = [
        pl.BlockSpec((rows, D_MODEL), tile),
        pl.BlockSpec((n_seq, B_HEADS, B_HDIM, B_HDIM), lambda b, j: (b, 0, 0, 0)),
        whole(1, D_MODEL), once(D_MODEL, EVEN_IN), once(D_MODEL, D_MODEL),
        whole(A_GROUPS, gchunk, gchunk), whole(A_GROUPS, gchunk, LANES),
        whole(A_GROUPS, 1, A_GDIM), whole(A_GROUPS, 1, A_GDIM), whole(1, B_DIM), whole(1, B_HDIM),
        whole((N_LEVELS + 1) * CHUNK, 3 * CHUNK), whole(N_LEVELS + 1, CHUNK, CHUNK),
    ]
    args = [x, s0] + list(consts)
    out_shape = [jax.ShapeDtypeStruct(x.shape, F32),
                 jax.ShapeDtypeStruct((n_batch * n_seq, B_HEADS, B_HDIM, B_HDIM), F32)]
    out_specs = [pl.BlockSpec((rows, D_MODEL), tile),
                 pl.BlockSpec((n_seq, B_HEADS, B_HDIM, B_HDIM), lambda b, j: (b, 0, 0, 0))]
    if emit_v:
        out_shape.append(jax.ShapeDtypeStruct((x.shape[0], A_DIM), F32))
        out_specs.append(pl.BlockSpec((rows, A_DIM), tile))
    scratch = [pltpu.VMEM((rows, EVEN_IN), F32), pltpu.VMEM((rows, D_MODEL), F32),
               pltpu.VMEM((n_seq * B_HEADS, B_HDIM, B_HDIM), F32),
               pltpu.VMEM((D_MODEL, EVEN_IN), BF16), pltpu.VMEM((D_MODEL, D_MODEL), BF16)]
    if side_cast is not None:
        in_specs.append(pl.BlockSpec(memory_space=pl.ANY))
        args.append(side_cast)
        out_shape.append(jax.ShapeDtypeStruct(side_cast.shape, BF16))
        out_specs.append(pl.BlockSpec(memory_space=pl.ANY))
        scratch += _cast_scratch(side_cast, _cast_steps(n_batch * tiles_per_batch))
    body = functools.partial(_even_mixer_body, n_seq=n_seq, seq_rows=seq_rows, gchunk=gchunk, emit_v=emit_v,
                             side_cast=side_cast is not None)
    return pl.pallas_call(
        body, grid=grid, in_specs=in_specs, out_specs=out_specs, out_shape=out_shape, scratch_shapes=scratch,
        compiler_params=_cparams(2),
        name="even_mixer_s" if emit_v else "even_mixer_p")(*args)


def _memory_kv_body(mem_ref, gm_ref, wk_ref, wv_ref, kg_ref, mk_ref, mv_ref, *, n_seq, n_mem):
    m = _rms(mem_ref[...], gm_ref[0]).astype(BF16)
    kk = _dot(m, wk_ref[0].astype(BF16))
    vv = _dot(m, wv_ref[0].astype(BF16))
    kg = kg_ref[0]
    for hd in range(X_HEADS):
        hs = slice(hd * X_HDIM, (hd + 1) * X_HDIM)
        kh = _rms(kk[:, hs], kg)
        for s in range(n_seq):
            mk_ref[0, s, pl.ds(hd, n_mem, stride=X_HEADS), :] = kh[s * n_mem:(s + 1) * n_mem]
            mv_ref[0, s, pl.ds(hd, n_mem, stride=X_HEADS), :] = vv[s * n_mem:(s + 1) * n_mem, hs]


def _memory_kv(mem2d, g_mem, w_k, w_v, k_g, *, n_batch, n_mem):
    depth = w_k.shape[0]
    n_seq = ROW_TILE // n_mem
    out = jax.ShapeDtypeStruct((depth, n_batch, n_mem * X_HEADS, X_HDIM), F32)
    ospec = pl.BlockSpec((1, n_seq, n_mem * X_HEADS, X_HDIM), lambda l, t: (l, t, 0, 0))
    return pl.pallas_call(
        functools.partial(_memory_kv_body, n_seq=n_seq, n_mem=n_mem),
        grid=(depth, n_batch // n_seq),
        in_specs=[pl.BlockSpec((ROW_TILE, D_MODEL), lambda l, t: (t, 0)),
                  pl.BlockSpec((1, 1, D_MODEL), lambda l, t: (l, 0, 0)),
                  pl.BlockSpec((1, D_MODEL, X_DIM), lambda l, t: (l, 0, 0)),
                  pl.BlockSpec((1, D_MODEL, X_DIM), lambda l, t: (l, 0, 0)),
                  pl.BlockSpec((1, 1, X_HDIM), lambda l, t: (l, 0, 0))],
        out_specs=[ospec, ospec], out_shape=[out, out], compiler_params=_cparams(2),
        name="memory_kv")(mem2d, g_mem, w_k, w_v, k_g)


def _xattn_body(x_ref, mk_ref, mv_ref, nrm_ref, wq32_ref, wo32_ref, qg_ref, y_ref, o_scr, wq_ref, wo_ref, *, n_seq,
                seq_rows):
    @pl.when(pl.program_id(0) == 0)
    def _():
        wq_ref[...] = wq32_ref[...].astype(BF16)
        wo_ref[...] = wo32_ref[...].astype(BF16)

    x = x_ref[...]
    h = _rms(x, nrm_ref[...]).astype(BF16)
    q = _dot(h, wq_ref[...])
    qg = qg_ref[...] * (X_HDIM ** -0.5)
    units = [(s, hd) for s in range(n_seq) for hd in range(X_HEADS)]
    rows_of = lambda s: slice(s * seq_rows, (s + 1) * seq_rows)
    lanes_of = lambda hd: slice(hd * X_HDIM, (hd + 1) * X_HDIM)
    n_mem = mk_ref.shape[1] // X_HEADS
    head_rows = lambda hd: pl.ds(hd, n_mem, stride=X_HEADS)
    qh = [_rms(q[rows_of(s), lanes_of(hd)], qg).astype(BF16) for s, hd in units]
    sc = [_dot_nt(qh[n], mk_ref[s, head_rows(hd), :].astype(BF16)) for n, (s, hd) in enumerate(units)]
    m = [jnp.max(t, axis=-1, keepdims=True) for t in sc]
    p = [jnp.exp(t - mx) for t, mx in zip(sc, m)]
    den = [jnp.sum(t, axis=-1, keepdims=True) for t in p]
    for n, (s, hd) in enumerate(units):
        o_scr[rows_of(s), lanes_of(hd)] = _dot(p[n].astype(BF16), mv_ref[s, head_rows(hd), :].astype(BF16)) / den[n]
    y_ref[...] = x + _dot(o_scr[...].astype(BF16), wo_ref[...])


def _xattn(x, mk, mv, nrm, wq, wo, qg, *, layer, steps_per_mem, n_seq, seq_rows):
    rows = n_seq * seq_rows
    mem_spec = pl.BlockSpec((None, n_seq) + mk.shape[2:], lambda t: (layer, t // steps_per_mem, 0, 0))
    whole = lambda *shape: pl.BlockSpec(shape, lambda t: (0,) * len(shape))
    once = lambda *shape: pl.BlockSpec(shape, lambda t: (0,) * len(shape), pipeline_mode=pl.Buffered(1))
    return pl.pallas_call(
        functools.partial(_xattn_body, n_seq=n_seq, seq_rows=seq_rows),
        grid=(x.shape[0] // rows,),
        in_specs=[pl.BlockSpec((rows, D_MODEL), lambda t: (t, 0)), mem_spec, mem_spec,
                  whole(1, D_MODEL), once(D_MODEL, X_DIM), once(X_DIM, D_MODEL), whole(1, X_HDIM)],
        out_specs=pl.BlockSpec((rows, D_MODEL), lambda t: (t, 0)),
        out_shape=jax.ShapeDtypeStruct(x.shape, F32),
        scratch_shapes=[pltpu.VMEM((rows, X_DIM), F32), pltpu.VMEM((D_MODEL, X_DIM), BF16),
                        pltpu.VMEM((X_DIM, D_MODEL), BF16)],
        compiler_params=_cparams(1), name="xattn_s" if n_seq > 1 else "xattn_p")(x, mk, mv, nrm, wq, wo, qg)


def _two_source_specs(n_prompt_tiles, width=D_MODEL):
    return (pl.BlockSpec((ROW_TILE, width), lambda i, *_: (jnp.minimum(i, n_prompt_tiles - 1), 0)),
            pl.BlockSpec((ROW_TILE, width), lambda i, *_: (jnp.maximum(i - n_prompt_tiles, 0), 0)))


def _on_row_source(i, n_prompt_tiles, fn, prompt_refs, sample_refs):
    pl.when(i < n_prompt_tiles)(lambda: fn(*prompt_refs))
    pl.when(i >= n_prompt_tiles)(lambda: fn(*sample_refs))


def _swiglu_part(hb, w1, w3, w2):
    part = None
    for c0 in range(0, D_FF, FF_SUB):
        cs = slice(c0, min(c0 + FF_SUB, D_FF))
        act = (_silu(_dot(hb, w1[:, cs])) * _dot(hb, w3[:, cs])).astype(BF16)
        p = _dot(act, w2[cs, :])
        part = p if part is None else part + p
    return part


W_IN_STAGE_ROWS = 128
W_OUT_STAGE_ROWS = 256


def _load_weights_as_bf16(w1_hbm, w3_hbm, w2_hbm, w1_scr, w3_scr, w2_scr, in_stage, out_stage, sems):
    jobs = ([(w_hbm, w_scr, in_stage, r0, W_IN_STAGE_ROWS) for w_hbm, w_scr in ((w1_hbm, w1_scr), (w3_hbm, w3_scr))
             for r0 in range(0, D_MODEL, W_IN_STAGE_ROWS)]
            + [(w2_hbm, w2_scr, out_stage, r0, W_OUT_STAGE_ROWS) for r0 in range(0, D_FF, W_OUT_STAGE_ROWS)])

    def copy(n):
        w_hbm, _, stage, r0, n_rows = jobs[n]
        return pltpu.make_async_copy(w_hbm.at[pl.ds(r0, n_rows)], stage.at[n % 2], sems.at[n % 2])

    copy(0).start()
    for n in range(len(jobs)):
        if n + 1 < len(jobs):
            copy(n + 1).start()
        copy(n).wait()
        _, w_scr, stage, r0, n_rows = jobs[n]
        w_scr[pl.ds(r0, n_rows), :] = stage[n % 2].astype(BF16)


def _dense_ffn_body(xp_ref, xs_ref, nrm_ref, w1_hbm, w3_hbm, w2_hbm, side_src, yp_ref, ys_ref, side_dst, w1_scr,
                    w3_scr, w2_scr, in_stage, out_stage, sems, *side_scratch, n_prompt_tiles):
    i = pl.program_id(0)
    _cast_rows_step(i, side_src, side_dst, *side_scratch)

    @pl.when(i == 0)
    def _():
        _load_weights_as_bf16(w1_hbm, w3_hbm, w2_hbm, w1_scr, w3_scr, w2_scr, in_stage, out_stage, sems)

    def run(x_ref, y_ref):
        x = x_ref[...]
        y_ref[...] = x + _swiglu_part(_rms(x, nrm_ref[...]).astype(BF16), w1_scr, w3_scr, w2_scr)

    _on_row_source(i, n_prompt_tiles, run, (xp_ref, yp_ref), (xs_ref, ys_ref))


def _dense_ffn(xp, xs, nrm, w1, w3, w2, side_cast):
    n_prompt_tiles = xp.shape[0] // ROW_TILE
    n_tiles = n_prompt_tiles + xs.shape[0] // ROW_TILE
    in_hbm = pl.BlockSpec(memory_space=pl.ANY)
    return pl.pallas_call(
        functools.partial(_dense_ffn_body, n_prompt_tiles=n_prompt_tiles), grid=(n_tiles,),
        in_specs=[*_two_source_specs(n_prompt_tiles), pl.BlockSpec((1, D_MODEL), lambda i: (0, 0)),
                  in_hbm, in_hbm, in_hbm, in_hbm],
        out_specs=[*_two_source_specs(n_prompt_tiles), in_hbm],
        out_shape=[jax.ShapeDtypeStruct(xp.shape, F32), jax.ShapeDtypeStruct(xs.shape, F32),
                   jax.ShapeDtypeStruct(side_cast.shape, BF16)],
        scratch_shapes=[pltpu.VMEM((D_MODEL, D_FF), BF16), pltpu.VMEM((D_MODEL, D_FF), BF16),
                        pltpu.VMEM((D_FF, D_MODEL), BF16), pltpu.VMEM((2, W_IN_STAGE_ROWS, D_FF), F32),
                        pltpu.VMEM((2, W_OUT_STAGE_ROWS, D_MODEL), F32), pltpu.SemaphoreType.DMA((2,))]
        + _cast_scratch(side_cast, _cast_steps(n_tiles)),
        compiler_params=_cparams(1), name="dense_ffn")(xp, xs, nrm, w1, w3, w2, side_cast)


def _grouped_ffn_body(te_ref, tv_ref, x_ref, w1_ref, w3_ref, w2_ref, y_ref):
    valid = tv_ref[pl.program_id(0)]
    half = ROW_TILE // 2
    weights = (w1_ref.at[0], w3_ref.at[0], w2_ref.at[0])

    @pl.when(valid > half)
    def _():
        y_ref[...] = _swiglu_part(x_ref[...].astype(BF16), *weights)

    @pl.when((valid > 0) & (valid <= half))
    def _():
        y_ref[0:half] = _swiglu_part(x_ref[0:half].astype(BF16), *weights)
        y_ref[half:ROW_TILE] = jnp.zeros((ROW_TILE - half, D_MODEL), F32)

    @pl.when(valid == 0)
    def _():
        y_ref[...] = jnp.zeros_like(y_ref)


def _grouped_ffn(xsorted, tile_expert, tile_valid, w1, w3, w2):
    n_tiles = xsorted.shape[0] // ROW_TILE
    grid_spec = pltpu.PrefetchScalarGridSpec(
        num_scalar_prefetch=2, grid=(n_tiles,),
        in_specs=[pl.BlockSpec((ROW_TILE, D_MODEL), lambda i, te, tv: (i, 0)),
                  pl.BlockSpec((1, D_MODEL, D_FF), lambda i, te, tv: (te[i], 0, 0)),
                  pl.BlockSpec((1, D_MODEL, D_FF), lambda i, te, tv: (te[i], 0, 0)),
                  pl.BlockSpec((1, D_FF, D_MODEL), lambda i, te, tv: (te[i], 0, 0))],
        out_specs=pl.BlockSpec((ROW_TILE, D_MODEL), lambda i, te, tv: (i, 0)))
    return pl.pallas_call(
        _grouped_ffn_body, grid_spec=grid_spec, out_shape=jax.ShapeDtypeStruct(xsorted.shape, F32),
        compiler_params=_cparams(1), name="grouped_ffn")(tile_expert, tile_valid, xsorted, w1, w3, w2)


def _rope_slab(xs, cos, sin_signed, first_half):
    rot = jnp.where(first_half, pltpu.roll(xs, LANES - C_HDIM // 2, 1), pltpu.roll(xs, C_HDIM // 2, 1))
    return xs * cos + rot * sin_signed


def _swa_body(sink_ref, x_ref, pk_ref, pv_ref, cos_ref, sin_ref, nrm_ref, win32_ref, wout32_ref, qg_ref, kg_ref,
              bd_ref, *rest, n_seq, seq_rows, past_valid, side_cast):
    rest = list(rest)
    side_src = rest.pop(0) if side_cast else None
    y_ref, ko_ref, vo_ref = rest.pop(0), rest.pop(0), rest.pop(0)
    side_dst = rest.pop(0) if side_cast else None
    k_scr, v_scr, q_scr, a_scr, win_ref, wout_ref = rest[:6]
    rows = n_seq * seq_rows
    j = pl.program_id(1)
    tail = min(WINDOW, seq_rows)
    if side_cast:
        _cast_rows_step(pl.program_id(0) * pl.num_programs(1) + j, side_src, side_dst, *rest[6:])

    @pl.when((pl.program_id(0) == 0) & (j == 0))
    def _():
        win_ref[...] = win32_ref[...].astype(BF16)
        wout_ref[...] = wout32_ref[...].astype(BF16)

    if past_valid:
        for s in range(n_seq):
            k_scr[s, 0:WINDOW] = pk_ref[s]
            v_scr[s, 0:WINDOW] = pv_ref[s]
    else:
        @pl.when(j == 0)
        def _():
            for s in range(n_seq):
                k_scr[s, 0:WINDOW] = jnp.zeros((WINDOW, C_KV_DIM), F32)
                v_scr[s, 0:WINDOW] = jnp.zeros((WINDOW, C_KV_DIM), F32)

        @pl.when(j > 0)
        def _():
            for s in range(n_seq):
                k_scr[s, 0:WINDOW] = k_scr[s, seq_rows:seq_rows + WINDOW]
                v_scr[s, 0:WINDOW] = v_scr[s, seq_rows:seq_rows + WINDOW]

    x = x_ref[...]
    h = _rms(x, nrm_ref[...]).astype(BF16)
    q_dim = C_HEADS * C_HDIM
    q = _dot(h, win_ref[:, 0:q_dim])
    k = _dot(h, win_ref[:, q_dim:q_dim + C_KV_DIM])
    v = _dot(h, win_ref[:, q_dim + C_KV_DIM:q_dim + 2 * C_KV_DIM])

    cos = cos_ref[...]
    sin_signed = sin_ref[...]
    first_half = (lax.broadcasted_iota(jnp.int32, (rows, LANES), 1) % C_HDIM) < (C_HDIM // 2)
    bd = bd_ref[...]
    qn = q * lax.rsqrt(_dot((q * q).astype(BF16), bd) + EPS) * qg_ref[...]
    kn = k * lax.rsqrt(_dot((k * k).astype(BF16), bd[0:C_KV_DIM, 0:C_KV_DIM]) + EPS) * kg_ref[...]
    scale = C_HDIM ** -0.5
    for sl in range(q_dim // LANES):
        ls = slice(sl * LANES, (sl + 1) * LANES)
        q_scr[:, ls] = _rope_slab(qn[:, ls], cos, sin_signed, first_half) * scale
    for sl in range(C_KV_DIM // LANES):
        ls = slice(sl * LANES, (sl + 1) * LANES)
        kr = _rope_slab(kn[:, ls], cos, sin_signed, first_half)
        for s in range(n_seq):
            k_scr[s, WINDOW:WINDOW + seq_rows, ls] = kr[s * seq_rows:(s + 1) * seq_rows]
    for s in range(n_seq):
        v_scr[s, WINDOW:WINDOW + seq_rows] = v[s * seq_rows:(s + 1) * seq_rows]
        ko_ref[s] = k_scr[s, WINDOW + seq_rows - tail:WINDOW + seq_rows]
        vo_ref[s] = v_scr[s, WINDOW + seq_rows - tail:WINDOW + seq_rows]

    chunks_per_seq = seq_rows // CHUNK
    n_keys = WINDOW + CHUNK
    key_idx = lax.broadcasted_iota(jnp.int32, (C_GROUP * CHUNK, n_keys), 1)
    row_grp = lax.broadcasted_iota(jnp.int32, (C_GROUP * CHUNK, 1), 0) // CHUNK
    ones_keys = jnp.ones((n_keys, C_HDIM), BF16)

    head_sinks = []
    for kvh in range(C_KV_HEADS):
        sink = jnp.zeros((C_GROUP * CHUNK, 1), F32)
        for g in range(C_GROUP):
            sink = jnp.where(row_grp == g, sink_ref[kvh * C_GROUP + g], sink)
        head_sinks.append(sink)

    def attend(chunk_ids):
        units = [(idx, kvh) for idx in chunk_ids for kvh in range(C_KV_HEADS)]
        head_lanes = lambda kvh: slice(kvh * C_HDIM, (kvh + 1) * C_HDIM)
        q_rows = lambda idx: slice(idx * CHUNK, (idx + 1) * CHUNK)
        key_rows = lambda idx: slice((idx % chunks_per_seq) * CHUNK, (idx % chunks_per_seq) * CHUNK + n_keys)
        qs = [jnp.concatenate(
            [q_scr[q_rows(idx), (kvh * C_GROUP + g) * C_HDIM:(kvh * C_GROUP + g + 1) * C_HDIM]
             for g in range(C_GROUP)], axis=0).astype(BF16) for idx, kvh in units]
        kh = [k_scr[idx // chunks_per_seq, key_rows(idx), head_lanes(kvh)].astype(BF16) for idx, kvh in units]
        vh = [v_scr[idx // chunks_per_seq, key_rows(idx), head_lanes(kvh)].astype(BF16) for idx, kvh in units]
        sc = [_dot_nt(qn, kn) for qn, kn in zip(qs, kh)]
        if not past_valid:
            sc = [jnp.where((key_idx + (idx % chunks_per_seq) * CHUNK >= WINDOW) | (j > 0), sn, -jnp.inf)
                  for sn, (idx, kvh) in zip(sc, units)]
        m = [jnp.maximum(jnp.max(sn, axis=-1, keepdims=True), head_sinks[kvh]) for sn, (idx, kvh) in zip(sc, units)]
        p = [jnp.exp(sn - mn).astype(BF16) for sn, mn in zip(sc, m)]
        den = [_dot(pn, ones_keys) + jnp.exp(head_sinks[kvh] - mn) for pn, mn, (idx, kvh) in zip(p, m, units)]
        o = [_dot(pn, vn) / dn for pn, vn, dn in zip(p, vh, den)]
        for on, (idx, kvh) in zip(o, units):
            for g in range(C_GROUP):
                hs = slice((kvh * C_GROUP + g) * C_HDIM, (kvh * C_GROUP + g + 1) * C_HDIM)
                a_scr[q_rows(idx), hs] = on[g * CHUNK:(g + 1) * CHUNK]

    n_chunks = rows // CHUNK
    for first in range(0, n_chunks, SWA_STAGE_CHUNKS):
        attend(range(first, min(first + SWA_STAGE_CHUNKS, n_chunks)))
    y_ref[...] = x + _dot(a_scr[...].astype(BF16), wout_ref[...])


def _swa(x, pk, pv, cos, sin_signed, sinks, consts, *, n_batch, tiles_per_batch, n_seq, seq_rows, past_valid,
         side_cast=None):
    rows = n_seq * seq_rows
    tail = min(WINDOW, seq_rows)
    n_str = n_batch * n_seq
    whole = lambda *shape: pl.BlockSpec(shape, lambda b, j, sk: (0,) * len(shape))
    once = lambda *shape: pl.BlockSpec(shape, lambda b, j, sk: (0,) * len(shape), pipeline_mode=pl.Buffered(1))
    in_specs = [pl.BlockSpec((rows, D_MODEL), lambda b, j, sk: (b * tiles_per_batch + j, 0)),
                pl.BlockSpec((n_seq, WINDOW, C_KV_DIM), lambda b, j, sk: (b, 0, 0)),
                pl.BlockSpec((n_seq, WINDOW, C_KV_DIM), lambda b, j, sk: (b, 0, 0)),
                pl.BlockSpec((rows, LANES), lambda b, j, sk: (j, 0)),
                pl.BlockSpec((rows, LANES), lambda b, j, sk: (j, 0)),
                whole(1, D_MODEL), once(D_MODEL, ODD_IN), once(D_MODEL, D_MODEL),
                whole(1, D_MODEL), whole(1, C_KV_DIM), whole(D_MODEL, D_MODEL)]
    args = [x, pk, pv, cos, sin_signed] + list(consts)
    kv_out = jax.ShapeDtypeStruct((n_str, tail, C_KV_DIM), F32)
    kv_spec = pl.BlockSpec((n_seq, tail, C_KV_DIM), lambda b, j, sk: (b, 0, 0))
    out_specs = [pl.BlockSpec((rows, D_MODEL), lambda b, j, sk: (b * tiles_per_batch + j, 0)), kv_spec, kv_spec]
    out_shape = [jax.ShapeDtypeStruct(x.shape, F32), kv_out, kv_out]
    scratch = [pltpu.VMEM((n_seq, WINDOW + seq_rows, C_KV_DIM), F32),
               pltpu.VMEM((n_seq, WINDOW + seq_rows, C_KV_DIM), F32),
               pltpu.VMEM((rows, D_MODEL), F32), pltpu.VMEM((rows, D_MODEL), F32),
               pltpu.VMEM((D_MODEL, ODD_IN), BF16), pltpu.VMEM((D_MODEL, D_MODEL), BF16)]
    if side_cast is not None:
        in_specs.append(pl.BlockSpec(memory_space=pl.ANY))
        args.append(side_cast)
        out_specs.append(pl.BlockSpec(memory_space=pl.ANY))
        out_shape.append(jax.ShapeDtypeStruct(side_cast.shape, BF16))
        scratch += _cast_scratch(side_cast, _cast_steps(n_batch * tiles_per_batch))
    grid_spec = pltpu.PrefetchScalarGridSpec(
        num_scalar_prefetch=1, grid=(n_batch, tiles_per_batch), in_specs=in_specs, out_specs=out_specs,
        scratch_shapes=scratch)
    return pl.pallas_call(
        functools.partial(_swa_body, n_seq=n_seq, seq_rows=seq_rows, past_valid=past_valid,
                          side_cast=side_cast is not None),
        grid_spec=grid_spec, out_shape=out_shape,
        compiler_params=_cparams(2),
        name="swa_s" if past_valid else "swa_p")(sinks, *args)


def _cast_steps(n_grid_steps):
    return 1 << (n_grid_steps.bit_length() - 1)


def _cast_piece_rows(w, n_cast):
    assert w.shape[0] % (n_cast * 16) == 0
    return w.shape[0] // n_cast


def _cast_rows_step(step, src_hbm, dst_hbm, in_stage, out_stage, sems):
    n_rows = in_stage.shape[1]
    n_cast = src_hbm.shape[0] // n_rows
    slot = step % 2

    def read(s, sl):
        return pltpu.make_async_copy(src_hbm.at[pl.ds(pl.multiple_of(s * n_rows, 16), n_rows)], in_stage.at[sl],
                                     sems.at[sl])

    def write(s, sl):
        return pltpu.make_async_copy(out_stage.at[sl], dst_hbm.at[pl.ds(pl.multiple_of(s * n_rows, 16), n_rows)],
                                     sems.at[2 + sl])

    @pl.when(step == 0)
    def _():
        read(0, 0).start()

    @pl.when(step < n_cast)
    def _():
        @pl.when(step + 1 < n_cast)
        def _():
            read(step + 1, 1 - slot).start()

        read(step, slot).wait()

        @pl.when(step >= 2)
        def _():
            write(step - 2, slot).wait()

        out_stage[slot] = in_stage[slot].astype(BF16)
        write(step, slot).start()

        @pl.when(step == n_cast - 1)
        def _():
            write(step, slot).wait()
            if n_cast > 1:
                write(step - 1, 1 - slot).wait()


def _cast_scratch(w, n_cast):
    rows = _cast_piece_rows(w, n_cast)
    return [pltpu.VMEM((2, rows, w.shape[1]), F32), pltpu.VMEM((2, rows, w.shape[1]), BF16),
            pltpu.SemaphoreType.DMA((4,))]


def _router_body(xp_ref, xs_ref, nrm_ref, rboth_ref, lstrict_ref, su_ref, info_ref, cnt_ref, *, n_prompt_tiles):
    def run(x_ref):
        h = _rms(x_ref[...], nrm_ref[...])
        hi = h.astype(BF16)
        lo = (h - hi.astype(F32)).astype(BF16)
        both = _dot(hi, rboth_ref[...])
        logits = both[:, 0:LANES] + both[:, LANES:2 * LANES] + _dot(lo, rboth_ref[:, 0:LANES])
        lane = lax.broadcasted_iota(jnp.int32, logits.shape, 1).astype(F32)
        logits = jnp.where(lane < N_EXPERTS, logits, -jnp.inf)
        l1 = jnp.max(logits, axis=-1, keepdims=True)
        i1 = jnp.min(jnp.where(logits == l1, lane, float(LANES)), axis=-1, keepdims=True)
        rest = jnp.where(lane == i1, -jnp.inf, logits)
        l2 = jnp.max(rest, axis=-1, keepdims=True)
        i2 = jnp.min(jnp.where(rest == l2, lane, float(LANES)), axis=-1, keepdims=True)
        e = jnp.exp(l2 - l1)
        g1 = 1.0 / (1.0 + e)
        g2 = e * g1
        oh1 = (lane == i1).astype(F32)
        oh2 = (lane == i2).astype(F32)
        cnt1 = jnp.sum(oh1, axis=0, keepdims=True)
        cnt2 = jnp.sum(oh2, axis=0, keepdims=True)
        seg = jnp.broadcast_to(jnp.ceil((cnt1 + cnt2) * (1.0 / GROUP)), (GROUP, LANES))
        off = _dot(seg.astype(BF16), su_ref[...])[0:1] * GROUP
        c1 = _dot(lstrict_ref[...], oh1.astype(BF16)) + off
        c2 = _dot(lstrict_ref[...], oh2.astype(BF16)) + off + cnt1
        d1 = jnp.sum(jnp.where(lane == i1, c1, 0.0), axis=-1, keepdims=True)
        d2 = jnp.sum(jnp.where(lane == i2, c2, 0.0), axis=-1, keepdims=True)
        info = jnp.zeros(logits.shape, F32)
        for n, col in enumerate([d1, d2, g1, g2]):
            info = jnp.where(lane == n, col, info)
        info_ref[...] = info
        cnt_ref[...] = seg * GROUP

    _on_row_source(pl.program_id(0), n_prompt_tiles, run, (xp_ref,), (xs_ref,))


def _router(xp, xs, nrm, rboth, lstrict, su):
    n_prompt_tiles = xp.shape[0] // ROW_TILE
    n_tiles = n_prompt_tiles + xs.shape[0] // ROW_TILE
    whole = lambda *shape: pl.BlockSpec(shape, lambda i: (0,) * len(shape))
    return pl.pallas_call(
        functools.partial(_router_body, n_prompt_tiles=n_prompt_tiles), grid=(n_tiles,),
        in_specs=[*_two_source_specs(n_prompt_tiles), whole(1, D_MODEL), whole(D_MODEL, 2 * LANES),
                  whole(ROW_TILE, ROW_TILE), whole(LANES, LANES)],
        out_specs=[pl.BlockSpec((ROW_TILE, LANES), lambda i: (i, 0)), pl.BlockSpec((GROUP, LANES), lambda i: (i, 0))],
        out_shape=[jax.ShapeDtypeStruct((n_tiles * ROW_TILE, LANES), F32),
                   jax.ShapeDtypeStruct((n_tiles * GROUP, LANES), F32)],
        compiler_params=_cparams(1), name="router")(xp, xs, nrm, rboth, lstrict, su)


BIG_ROWS = 8 * GROUP
SEG_FIELDS = 4
TILE_FIELDS = 2
TAIL_FIELDS = 3


def _rows_copy(src_ref, src_row, dst_ref, dst_row, n_rows, sem):
    return pltpu.make_async_copy(src_ref.at[pl.ds(pl.multiple_of(src_row, GROUP), n_rows)],
                                 dst_ref.at[pl.ds(pl.multiple_of(dst_row, GROUP), n_rows)], sem)


def _start_pieces(src_ref, src0, dst_ref, dst0, n_big, n_small, sem):
    def big(g, carry):
        _rows_copy(src_ref, src0 + g * BIG_ROWS, dst_ref, dst0 + g * BIG_ROWS, BIG_ROWS, sem).start()
        return carry

    lax.fori_loop(0, n_big, big, 0)
    done = n_big * BIG_ROWS

    def small(g, carry):
        _rows_copy(src_ref, src0 + done + g * GROUP, dst_ref, dst0 + done + g * GROUP, GROUP, sem).start()
        return carry

    lax.fori_loop(0, n_small, small, 0)


def _wait_pieces(n_big, n_small, src_ref, dst_ref, sem):
    def big(g, carry):
        _rows_copy(src_ref, 0, dst_ref, 0, BIG_ROWS, sem).wait()
        return carry

    lax.fori_loop(0, n_big, big, 0)

    def small(g, carry):
        _rows_copy(src_ref, 0, dst_ref, 0, GROUP, sem).wait()
        return carry

    lax.fori_loop(0, n_small, small, 0)


def _segment_table(seg, n_sorted_tiles):
    pieces = lambda rows: (rows // BIG_ROWS, (rows % BIG_ROWS) // GROUP)
    local_off = jnp.cumsum(seg, axis=1) - seg
    expert_rows = jnp.sum(seg, axis=0)
    expert_pad = ((expert_rows + ROW_TILE - 1) // ROW_TILE) * ROW_TILE
    expert_end = jnp.cumsum(expert_pad)
    expert_off = expert_end - expert_pad
    seg_start = expert_off[None, :] + jnp.cumsum(seg, axis=0) - seg
    n_used = (expert_end[-1] // ROW_TILE).reshape(1)
    tile_ids = jnp.arange(n_sorted_tiles, dtype=jnp.int32)
    tile_expert = jnp.minimum(jnp.sum(tile_ids[:, None] >= (expert_end // ROW_TILE)[None, :], axis=1), N_EXPERTS - 1)
    tile_expert = jnp.where(tile_ids < n_used[0], tile_expert, tile_expert[n_used[0] - 1])
    tile_valid = jnp.clip((expert_off + expert_rows)[tile_expert] - tile_ids * ROW_TILE, 0, ROW_TILE)
    tile_valid = jnp.where(tile_ids < n_used[0], tile_valid, 0)
    seg_big, seg_small = pieces(seg)
    tail_big, tail_small = pieces(expert_pad - expert_rows)
    table = jnp.concatenate([
        jnp.stack([seg_start, seg_big, seg_small, local_off], axis=-1).reshape(-1),
        jnp.stack([jnp.sum(seg_big, axis=1), jnp.sum(seg_small, axis=1)], axis=-1).reshape(-1),
        jnp.stack([expert_off + expert_rows, tail_big, tail_small], axis=-1).reshape(-1),
        n_used])
    return table.astype(jnp.int32), tile_expert.astype(jnp.int32), tile_valid.astype(jnp.int32)


def _table_sections(n_tiles):
    tiles0 = SEG_FIELDS * n_tiles * N_EXPERTS
    tails0 = tiles0 + TILE_FIELDS * n_tiles
    return tiles0, tails0, tails0 + TAIL_FIELDS * N_EXPERTS


def _dispatch_body(tab_ref, info_ref, xp_ref, xs_ref, nrm_ref, sel_ref, out_ref, loc_scr, z_scr, sems, *,
                   n_prompt_tiles, n_tiles, n_sorted_tiles):
    i = pl.program_id(0)
    slot = i % 2
    tiles0, tails0, used0 = _table_sections(n_tiles)

    def wait_tile(t, s):
        _wait_pieces(tab_ref[tiles0 + TILE_FIELDS * t], tab_ref[tiles0 + TILE_FIELDS * t + 1], loc_scr.at[s], out_ref,
                     sems.at[s])

    @pl.when(i >= 2)
    def _():
        wait_tile(i - 2, slot)

    def run(x_ref):
        hb = _rms(x_ref[...], nrm_ref[...]).astype(BF16)
        info = info_ref[...]
        lane = lax.broadcasted_iota(jnp.int32, info.shape, 1)
        dest = jnp.where(lane < TOP_K, info, 0.0)
        drow = sum(_dot_nt(sel_ref[...], part) for part in _split3(dest))
        r = lax.broadcasted_iota(jnp.int32, (LOCAL_ROWS, ROW_TILE), 0).astype(F32)
        perm = jnp.where(r == drow[0:1], 1.0, jnp.where(r == drow[1:2], 1.0, 0.0)).astype(BF16)
        loc_scr[slot] = _dot(perm, hb)

    _on_row_source(i, n_prompt_tiles, run, (xp_ref,), (xs_ref,))

    for e in range(N_EXPERTS):
        base = SEG_FIELDS * (i * N_EXPERTS + e)
        _start_pieces(loc_scr.at[slot], tab_ref[base + 3], out_ref, tab_ref[base], tab_ref[base + 1],
                      tab_ref[base + 2], sems.at[slot])

    @pl.when(i == n_tiles - 1)
    def _():
        wait_tile(i, slot)
        if n_tiles > 1:
            wait_tile(i - 1, 1 - slot)
        z_scr[...] = jnp.zeros_like(z_scr)
        for e in range(N_EXPERTS):
            base = tails0 + TAIL_FIELDS * e
            _start_pieces(z_scr, 0, out_ref, tab_ref[base], tab_ref[base + 1], tab_ref[base + 2], sems.at[2])
            _wait_pieces(tab_ref[base + 1], tab_ref[base + 2], z_scr, out_ref, sems.at[2])

        def zero_tile(t, carry):
            cp = _rows_copy(z_scr, 0, out_ref, t * ROW_TILE, ROW_TILE, sems.at[2])
            cp.start()
            cp.wait()
            return carry

        lax.fori_loop(tab_ref[used0], n_sorted_tiles, zero_tile, 0)


def _dispatch(table, info, xp, xs, nrm, sel, n_sorted_tiles):
    n_prompt_tiles = xp.shape[0] // ROW_TILE
    n_tiles = n_prompt_tiles + xs.shape[0] // ROW_TILE
    whole = lambda *shape: pl.BlockSpec(shape, lambda i, tab: (0,) * len(shape))
    grid_spec = pltpu.PrefetchScalarGridSpec(
        num_scalar_prefetch=1, grid=(n_tiles,),
        in_specs=[pl.BlockSpec((ROW_TILE, LANES), lambda i, tab: (i, 0)), *_two_source_specs(n_prompt_tiles),
                  whole(1, D_MODEL), whole(GROUP, LANES)],
        out_specs=pl.BlockSpec(memory_space=pl.ANY),
        scratch_shapes=[pltpu.VMEM((2, LOCAL_ROWS, D_MODEL), F32), pltpu.VMEM((ROW_TILE, D_MODEL), F32),
                        pltpu.SemaphoreType.DMA((3,))])
    return pl.pallas_call(
        functools.partial(_dispatch_body, n_prompt_tiles=n_prompt_tiles, n_tiles=n_tiles,
                          n_sorted_tiles=n_sorted_tiles),
        grid_spec=grid_spec, out_shape=jax.ShapeDtypeStruct((n_sorted_tiles * ROW_TILE, D_MODEL), F32),
        compiler_params=_cparams(1), name="moe_dispatch")(table, info, xp, xs, nrm, sel)


def _combine_body(tab_ref, info_ref, xp_ref, xs_ref, e_ref, yp_ref, ys_ref, loc_scr, sems, *, n_prompt_tiles,
                  n_tiles):
    i = pl.program_id(0)
    slot = i % 2
    tiles0, _, _ = _table_sections(n_tiles)

    def fetch(t, s):
        for e in range(N_EXPERTS):
            base = SEG_FIELDS * (t * N_EXPERTS + e)
            _start_pieces(e_ref, tab_ref[base], loc_scr.at[s], tab_ref[base + 3], tab_ref[base + 1],
                          tab_ref[base + 2], sems.at[s])

    @pl.when(i == 0)
    def _():
        loc_scr[...] = jnp.zeros_like(loc_scr)
        fetch(0, 0)

    @pl.when(i + 1 < n_tiles)
    def _():
        fetch(i + 1, 1 - slot)

    _wait_pieces(tab_ref[tiles0 + TILE_FIELDS * i], tab_ref[tiles0 + TILE_FIELDS * i + 1], e_ref, loc_scr.at[slot],
                 sems.at[slot])

    def run(x_ref, y_ref):
        info = info_ref[...]
        eb = loc_scr[slot].astype(BF16)
        lane = lax.broadcasted_iota(jnp.int32, (ROW_TILE, LOCAL_ROWS), 1).astype(F32)
        pick = jnp.zeros((ROW_TILE, LOCAL_ROWS), F32)
        for k in range(TOP_K):
            pick = jnp.where(lane == info[:, k:k + 1], info[:, TOP_K + k:TOP_K + k + 1], pick)
        y_ref[...] = x_ref[...] + _dot(pick.astype(BF16), eb)

    _on_row_source(i, n_prompt_tiles, run, (xp_ref, yp_ref), (xs_ref, ys_ref))


def _combine(table, info, xp, xs, esorted):
    n_prompt_tiles = xp.shape[0] // ROW_TILE
    n_tiles = n_prompt_tiles + xs.shape[0] // ROW_TILE
    grid_spec = pltpu.PrefetchScalarGridSpec(
        num_scalar_prefetch=1, grid=(n_tiles,),
        in_specs=[pl.BlockSpec((ROW_TILE, LANES), lambda i, tab: (i, 0)), *_two_source_specs(n_prompt_tiles),
                  pl.BlockSpec(memory_space=pl.ANY)],
        out_specs=list(_two_source_specs(n_prompt_tiles)),
        scratch_shapes=[pltpu.VMEM((2, LOCAL_ROWS, D_MODEL), F32), pltpu.SemaphoreType.DMA((2,))])
    return pl.pallas_call(
        functools.partial(_combine_body, n_prompt_tiles=n_prompt_tiles, n_tiles=n_tiles),
        grid_spec=grid_spec,
        out_shape=[jax.ShapeDtypeStruct(xp.shape, F32), jax.ShapeDtypeStruct(xs.shape, F32)],
        compiler_params=_cparams(1), name="moe_combine")(table, info, xp, xs, esorted)


def _rope_tables(pos):
    half = C_HDIM // 2
    inv = 1.0 / (ROPE_THETA ** (jnp.arange(half, dtype=F32) / half))
    ang = pos.astype(F32)[:, None] * inv[None, :]
    cos = jnp.cos(ang)
    sin = jnp.sin(ang)
    reps = LANES // C_HDIM
    return jnp.tile(jnp.concatenate([cos, cos], axis=-1), (1, reps)), jnp.tile(jnp.concatenate([-sin, sin], axis=-1),
                                                                                 (1, reps))


def kernel(x_prompt, x_sample, mem_prompt, cache_mem_k, cache_mem_v, state_hgrn, cache_swa_k, cache_swa_v, norm_mix, norm_xattn, norm_ffn, even_w_in, even_w_out, gmlp_w_s, gmlp_b_s, gmlp_ln_g, gmlp_ln_b, hgrn_lb_logits, hgrn_out_norm, attn_w_in, attn_w_out, attn_q_norm, attn_k_norm, attn_sinks, xattn_mem_norm, xattn_w_q, xattn_w_k, xattn_w_v, xattn_w_o, xattn_q_norm, xattn_k_norm, ffn_w1, ffn_w3, ffn_w2, moe_router, moe_w1, moe_w3, moe_w2):
    n_batch, seq, d = x_prompt.shape
    dec_batch, dec_seq, _ = x_sample.shape
    n_mem = mem_prompt.shape[1]
    depth = norm_mix.shape[0]
    past_len = PAST_LEN
    assert d == D_MODEL and depth == 2 and seq % ROW_TILE == 0 and dec_batch * dec_seq == ROW_TILE
    assert dec_seq == CHUNK and ROW_TILE % n_mem == 0 and cache_swa_k.shape[2] == WINDOW
    assert even_w_in.shape[-1] == EVEN_IN and attn_w_in.shape[-1] == ODD_IN and ffn_w1.shape[-1] == D_FF
    assert moe_w1.shape[1] == N_EXPERTS
    n_prompt_rows = n_batch * seq
    n_rows = n_prompt_rows + dec_batch * dec_seq
    n_prompt_tiles = n_prompt_rows // ROW_TILE
    tiles_per_batch = seq // ROW_TILE
    row = lambda g: g.reshape(1, -1).astype(F32)

    lb_all = jnp.cumsum(jax.nn.softmax(hgrn_lb_logits.astype(F32), axis=0), axis=0)
    wcum = jnp.asarray(_cumsum_matrix(), BF16)
    lmask = jnp.asarray(_level_masks(), F32)

    def even_consts(n):
        tril = jnp.tril(jnp.ones((n, n), bool))
        ws = jnp.where(tril[None], gmlp_w_s[0, :, :n, :n], 0.0).astype(BF16)
        bs = jnp.broadcast_to(gmlp_b_s[0, :, :n, None], (A_GROUPS, n, LANES)).astype(F32)
        return [row(norm_mix[0]), even_w_in[0], even_w_out[0], ws, bs,
                gmlp_ln_g[0].reshape(A_GROUPS, 1, A_GDIM), gmlp_ln_b[0].reshape(A_GROUPS, 1, A_GDIM),
                row(lb_all[0]), row(hgrn_out_norm[0]), wcum, lmask]

    rows2d = lambda w: w.reshape(-1, w.shape[-1])
    xp, hgrn_p, w1b = _even_mixer(
        x_prompt.reshape(n_prompt_rows, d), jnp.zeros((n_batch, B_HEADS, B_HDIM, B_HDIM), F32),
        even_consts(A_CHUNK), n_batch=n_batch, tiles_per_batch=tiles_per_batch, n_seq=1, seq_rows=ROW_TILE,
        gchunk=A_CHUNK, emit_v=False, side_cast=rows2d(moe_w1[0]))
    xs, hgrn_s, gmlp_v = _even_mixer(
        x_sample.reshape(ROW_TILE, d), state_hgrn[0], even_consts(min(A_CHUNK, dec_seq)), n_batch=1,
        tiles_per_batch=1, n_seq=dec_batch, seq_rows=dec_seq, gchunk=min(A_CHUNK, dec_seq), emit_v=True)

    mem_k, mem_v = _memory_kv(mem_prompt.reshape(n_batch * n_mem, d), xattn_mem_norm.reshape(depth, 1, d),
                              xattn_w_k, xattn_w_v, xattn_k_norm.reshape(depth, 1, X_HDIM),
                              n_batch=n_batch, n_mem=n_mem)

    def cross_attention(xp, xs, l):
        consts = (row(norm_xattn[l]), xattn_w_q[l], xattn_w_o[l], row(xattn_q_norm[l]))
        xp = _xattn(xp, mem_k, mem_v, *consts, layer=l, steps_per_mem=tiles_per_batch, n_seq=1, seq_rows=ROW_TILE)
        xs = _xattn(xs, cache_k, cache_v, *consts, layer=l, steps_per_mem=1, n_seq=dec_batch, seq_rows=dec_seq)
        return xp, xs

    cache_k = cache_mem_k.reshape(depth, dec_batch, n_mem * X_HEADS, X_HDIM)
    cache_v = cache_mem_v.reshape(depth, dec_batch, n_mem * X_HEADS, X_HDIM)

    xp, xs = cross_attention(xp, xs, 0)
    xp, xs, w2b = _dense_ffn(xp, xs, row(norm_ffn[0]), ffn_w1[0], ffn_w3[0], ffn_w2[0], rows2d(moe_w2[0]))

    reps = D_MODEL // C_HDIM
    bd = jnp.asarray(np.kron(np.eye(reps, dtype=np.float32), np.full((C_HDIM, C_HDIM), 1.0 / C_HDIM, np.float32)),
                     BF16)
    swa_consts = [row(norm_mix[1]), attn_w_in[0], attn_w_out[0], row(jnp.tile(attn_q_norm[0], C_HEADS)),
                  row(jnp.tile(attn_k_norm[0], C_KV_HEADS)), bd]
    sinks = attn_sinks[0].astype(F32)
    cos_p, sin_p = _rope_tables(jnp.arange(seq, dtype=jnp.int32))
    cos_s, sin_s = _rope_tables(past_len + jnp.arange(dec_seq, dtype=jnp.int32))
    no_past = jnp.zeros((n_batch, WINDOW, C_KV_DIM), F32)
    xp, swk_p, swv_p, w3b = _swa(xp, no_past, no_past, cos_p, sin_p, sinks, swa_consts, n_batch=n_batch,
                                 tiles_per_batch=tiles_per_batch, n_seq=1, seq_rows=ROW_TILE, past_valid=False,
                                 side_cast=rows2d(moe_w3[0]))
    xs, swk_s, swv_s = _swa(xs, cache_swa_k[0].reshape(dec_batch, WINDOW, C_KV_DIM),
                            cache_swa_v[0].reshape(dec_batch, WINDOW, C_KV_DIM),
                            jnp.tile(cos_s, (dec_batch, 1)), jnp.tile(sin_s, (dec_batch, 1)), sinks, swa_consts,
                            n_batch=1, tiles_per_batch=1, n_seq=dec_batch, seq_rows=dec_seq, past_valid=True)
    xp, xs = cross_attention(xp, xs, 1)

    n_tiles = n_rows // ROW_TILE
    router_w = jnp.zeros((d, LANES), F32).at[:, :N_EXPERTS].set(moe_router[0].astype(F32))
    rhi = router_w.astype(BF16)
    rlo = (router_w - rhi.astype(F32)).astype(BF16)
    lstrict = jnp.asarray(np.tril(np.ones((ROW_TILE, ROW_TILE), np.float32), -1), BF16)
    su = jnp.asarray(np.triu(np.ones((LANES, LANES), np.float32), 1), BF16)
    sel = jnp.asarray(np.eye(GROUP, LANES, dtype=np.float32), BF16)
    nrm_ffn = row(norm_ffn[1])
    info, seg = _router(xp, xs, nrm_ffn, jnp.concatenate([rhi, rlo], axis=1), lstrict, su)
    seg = seg.reshape(n_tiles, GROUP, LANES)[:, 0, :N_EXPERTS].astype(jnp.int32)
    n_sorted_tiles = -(-(TOP_K * n_rows + n_tiles * N_EXPERTS * (GROUP - 1) + N_EXPERTS * (ROW_TILE - 1)) // ROW_TILE)
    table, tile_expert, tile_valid = _segment_table(seg, n_sorted_tiles)
    hsorted = _dispatch(table, info, xp, xs, nrm_ffn, sel, n_sorted_tiles)
    esorted = _grouped_ffn(hsorted, tile_expert, tile_valid, w1b.reshape(moe_w1.shape[1:]),
                           w3b.reshape(moe_w3.shape[1:]), w2b.reshape(moe_w2.shape[1:]))
    y_prompt, y_sample = _combine(table, info, xp, xs, esorted)

    n_even = state_hgrn.shape[0]
    n_odd = cache_swa_k.shape[0]
    return (y_prompt.reshape(n_batch, seq, d), y_sample.reshape(dec_batch, dec_seq, d),
            mem_k.reshape(depth, n_batch, n_mem, X_HEADS, X_HDIM), mem_v.reshape(depth, n_batch, n_mem, X_HEADS, X_HDIM),
            hgrn_p.reshape(n_even, n_batch, B_HEADS, B_HDIM, B_HDIM),
            gmlp_v.reshape(n_even, dec_batch, dec_seq, A_GROUPS, A_GDIM),
            hgrn_s.reshape(n_even, dec_batch, B_HEADS, B_HDIM, B_HDIM),
            swk_p.reshape(n_odd, n_batch, WINDOW, C_KV_HEADS, C_HDIM), swv_p.reshape(n_odd, n_batch, WINDOW, C_KV_HEADS, C_HDIM),
            swk_s.reshape(n_odd, dec_batch, dec_seq, C_KV_HEADS, C_HDIM), swv_s.reshape(n_odd, dec_batch, dec_seq, C_KV_HEADS, C_HDIM))
```

```python
import functools

import numpy as np
import jax
import jax.numpy as jnp
from jax import lax
from jax.experimental import pallas as pl
from jax.experimental.pallas import tpu as pltpu

F32 = jnp.float32
BF16 = jnp.bfloat16

D_MODEL = 1024
EPS = 1e-6
LOG2_E = 1.4426950408889634
CHUNK = 64
A_GROUPS = 4
A_DIM = D_MODEL // 2
A_GDIM = A_DIM // A_GROUPS
A_CHUNK = 128
B_HEADS = 4
B_DIM = D_MODEL // 2
B_HDIM = B_DIM // B_HEADS
EVEN_IN = 2 * A_DIM + 4 * B_DIM
C_HEADS = 16
C_KV_HEADS = 4
C_HDIM = D_MODEL // C_HEADS
C_GROUP = C_HEADS // C_KV_HEADS
C_KV_DIM = C_KV_HEADS * C_HDIM
WINDOW = 128
ROPE_THETA = 10000.0
PAST_LEN = 4096
ODD_IN = (C_HEADS + 2 * C_KV_HEADS) * C_HDIM
X_HEADS = 4
X_HDIM = 128
X_DIM = X_HEADS * X_HDIM
D_FF = 2816
N_EXPERTS = 8
TOP_K = 2

LANES = 128
ROW_TILE = 512
FF_SUB = 512
N_LEVELS = 6
HGRN_STAGE_CHUNKS = 2
SWA_STAGE_CHUNKS = 2
GROUP = 8
LOCAL_ROWS = -(-(TOP_K * ROW_TILE + N_EXPERTS * (GROUP - 1)) // LANES) * LANES
VMEM_LIMIT = 58 * 1024 * 1024


def _cparams(n_axes):
    return pltpu.CompilerParams(dimension_semantics=("arbitrary",) * n_axes, vmem_limit_bytes=VMEM_LIMIT)


def _dot(a, b):
    return jnp.dot(a, b, preferred_element_type=F32)


def _dot_nt(a, b):
    return lax.dot_general(a, b, (((1,), (1,)), ((), ())), preferred_element_type=F32)


def _rms(x, g):
    return x * lax.rsqrt(jnp.mean(x * x, axis=-1, keepdims=True) + EPS) * g


def _silu(x):
    return x * (1.0 / (1.0 + jnp.exp(-x)))


def _split3(x):
    hi = x.astype(BF16)
    r1 = x - hi.astype(F32)
    mid = r1.astype(BF16)
    lo = (r1 - mid.astype(F32)).astype(BF16)
    return hi, mid, lo


def _cumsum_matrix():
    r = np.arange(CHUNK)
    s = np.arange(CHUNK)
    blocks = [(s[None, :] <= r[:, None])]
    for l in range(N_LEVELS):
        h = 1 << l
        ref = (r & ~(2 * h - 1)) + h - 1
        blocks.append(s[None, :] <= ref[:, None])
    w = np.concatenate(blocks, axis=0).astype(np.float32)
    return np.concatenate([w, w, w], axis=1)


def _level_masks():
    t = np.arange(CHUNK)[:, None]
    s = np.arange(CHUNK)[None, :]
    masks = []
    for l in range(N_LEVELS):
        masks.append(((t >> (l + 1)) == (s >> (l + 1))) & (((t >> l) & 1) == 1) & (((s >> l) & 1) == 0))
    masks.append(t == s)
    return np.stack(masks).astype(np.float32)


def _even_mixer_body(x_ref, s0_ref, nrm_ref, win32_ref, wout32_ref, ws_ref, bs_ref, lng_ref, lnb_ref, lb_ref, og_ref,
                     wcum_ref, lmask_ref, *rest, n_seq, seq_rows, gchunk, emit_v, side_cast):
    rest = list(rest)
    side_src = rest.pop(0) if side_cast else None
    y_ref, sout_ref = rest.pop(0), rest.pop(0)
    v_ref = rest.pop(0) if emit_v else None
    side_dst = rest.pop(0) if side_cast else None
    proj_scr, mixed_scr, st_scr, win_ref, wout_ref = rest[:5]
    rows = n_seq * seq_rows
    j = pl.program_id(1)
    if side_cast:
        _cast_rows_step(pl.program_id(0) * pl.num_programs(1) + j, side_src, side_dst, *rest[5:])

    @pl.when((pl.program_id(0) == 0) & (j == 0))
    def _():
        win_ref[...] = win32_ref[...].astype(BF16)
        wout_ref[...] = wout32_ref[...].astype(BF16)

    @pl.when(j == 0)
    def _():
        for s in range(n_seq):
            for hd in range(B_HEADS):
                st_scr[s * B_HEADS + hd] = s0_ref[s, hd].T

    x = x_ref[...]
    h = _rms(x, nrm_ref[...]).astype(BF16)
    n_pieces = EVEN_IN // ROW_TILE
    for n in range(n_pieces):
        cs = slice(n * ROW_TILE, (n + 1) * ROW_TILE)
        proj_scr[:, cs] = _dot(h, win_ref[:, cs])

    groups = range(A_GROUPS)
    group_cols = lambda base, g: slice(base + g * A_GDIM, base + (g + 1) * A_GDIM)
    vgs = [jax.nn.gelu(proj_scr[:, group_cols(A_DIM, g)]) for g in groups]
    means = [jnp.mean(vg, axis=-1, keepdims=True) for vg in vgs]
    vcs = [vg - mu for vg, mu in zip(vgs, means)]
    variances = [jnp.mean(vc * vc, axis=-1, keepdims=True) for vc in vcs]
    for g in groups:
        gs = group_cols(0, g)
        vn = vcs[g] * lax.rsqrt(variances[g] + EPS) * lng_ref[g] + lnb_ref[g]
        if v_ref is not None:
            v_ref[:, gs] = vn
        ug = jax.nn.gelu(proj_scr[:, gs])
        vb = vn.astype(BF16)
        for c in range(rows // gchunk):
            rs = slice(c * gchunk, (c + 1) * gchunk)
            sp = _dot(ws_ref[g], vb[rs]) + bs_ref[g]
            mixed_scr[rs, gs] = ug[rs] * sp

    q0, f0, i0, g0 = (2 * A_DIM + k * B_DIM for k in range(4))
    lb = lb_ref[...]
    fg = lb + (1.0 - lb) * jax.nn.sigmoid(proj_scr[:, f0:f0 + B_DIM])
    proj_scr[:, 0:B_DIM] = jnp.log(fg) * LOG2_E
    proj_scr[:, B_DIM:2 * B_DIM] = 1.0 - fg
    proj_scr[:, q0:q0 + B_DIM] = _silu(proj_scr[:, q0:q0 + B_DIM])
    og = og_ref[...]
    chunks_per_seq = seq_rows // CHUNK

    heads = range(B_HEADS)
    head_cols = lambda base, hd: slice(base + hd * B_HDIM, base + (hd + 1) * B_HDIM)
    chunk_rows = lambda c: slice(c * CHUNK, (c + 1) * CHUNK)

    def chunk_local(c):
        rs = chunk_rows(c)
        hi, mid, lo = _split3(proj_scr[rs, 0:B_DIM])
        gg = _dot(wcum_ref[...], jnp.concatenate([hi, mid, lo], axis=0))
        vs = [proj_scr[rs, head_cols(i0, hd)] for hd in heads]
        vts = [v.T.astype(BF16) for v in vs]
        Gs = [gg[0:CHUNK, head_cols(0, hd)] for hd in heads]
        qs = [proj_scr[rs, head_cols(q0, hd)] for hd in heads]
        ks = [proj_scr[rs, head_cols(B_DIM, hd)] for hd in heads]
        operands = []
        for hd in heads:
            for l in range(N_LEVELS + 1):
                if l < N_LEVELS:
                    e = jnp.exp2(-jnp.abs(Gs[hd] - gg[(l + 1) * CHUNK:(l + 2) * CHUNK, head_cols(0, hd)]))
                    operands.append(((qs[hd] * e).astype(BF16), (ks[hd] * e).astype(BF16)))
                else:
                    operands.append((qs[hd].astype(BF16), ks[hd].astype(BF16)))
        blocks = [_dot_nt(qe, ke) for qe, ke in operands]
        parts = []
        for hd in heads:
            att = jnp.zeros((CHUNK, CHUNK), F32)
            for l in range(N_LEVELS + 1):
                att = jnp.where(lmask_ref[l] > 0.5, blocks[hd * (N_LEVELS + 1) + l], att)
            g_end = Gs[hd][CHUNK - 1:CHUNK, :]
            kd = (ks[hd] * jnp.exp2(g_end - Gs[hd])).astype(BF16)
            parts.append((_dot(att.astype(BF16), vs[hd].astype(BF16)), (qs[hd] * jnp.exp2(Gs[hd])).astype(BF16),
                          jnp.exp2(g_end), _dot(vts[hd], kd)))
        return parts

    def chunk_state(c, parts):
        rs = chunk_rows(c)
        sidx = (c // chunks_per_seq) * B_HEADS
        outs = []
        for hd in heads:
            o_local, q_decayed, decay, increment = parts[hd]
            st = st_scr[sidx + hd]
            outs.append(o_local + _dot_nt(q_decayed, st.astype(BF16)))
            st_scr[sidx + hd] = st * decay + increment
        for hd in heads:
            o = outs[hd]
            on = o * lax.rsqrt(jnp.mean(o * o, axis=-1, keepdims=True) + EPS) * og
            gate = _silu(proj_scr[rs, head_cols(g0, hd)])
            mixed_scr[rs, head_cols(A_DIM, hd)] = on * gate

    n_chunks = rows // CHUNK
    for first in range(0, n_chunks, HGRN_STAGE_CHUNKS):
        group = range(first, min(first + HGRN_STAGE_CHUNKS, n_chunks))
        local = [chunk_local(c) for c in group]
        for c, parts in zip(group, local):
            chunk_state(c, parts)

    y_ref[...] = x + _dot(mixed_scr[...].astype(BF16), wout_ref[...])

    @pl.when(j == pl.num_programs(1) - 1)
    def _():
        for s in range(n_seq):
            for hd in range(B_HEADS):
                sout_ref[s, hd] = st_scr[s * B_HEADS + hd].T


def _even_mixer(x, s0, consts, *, n_batch, tiles_per_batch, n_seq, seq_rows, gchunk, emit_v, side_cast=None):
    rows = n_seq * seq_rows
    grid = (n_batch, tiles_per_batch)
    tile = lambda b, j: (b * tiles_per_batch + j, 0)
    whole = lambda *shape: pl.BlockSpec(shape, lambda b, j: (0,) * len(shape))
    once = lambda *shape: pl.BlockSpec(shape, lambda b, j: (0,) * len(shape), pipeline_mode=pl.Buffered(1))
    in_specs = [
        pl.BlockSpec((rows, D_MODEL), tile),
        pl.BlockSpec((n_seq, B_HEADS, B_HDIM, B_HDIM), lambda b, j: (b, 0, 0, 0)),
        whole(1, D_MODEL), once(D_MODEL, EVEN_IN), once(D_MODEL, D_MODEL),
        whole(A_GROUPS, gchunk, gchunk), whole(A_GROUPS, gchunk, LANES),
        whole(A_GROUPS, 1, A_GDIM), whole(A_GROUPS, 1, A_GDIM), whole(1, B_DIM), whole(1, B_HDIM),
        whole((N_LEVELS + 1) * CHUNK, 3 * CHUNK), whole(N_LEVELS + 1, CHUNK, CHUNK),
    ]
    args = [x, s0] + list(consts)
    out_shape = [jax.ShapeDtypeStruct(x.shape, F32),
                 jax.ShapeDtypeStruct((n_batch * n_seq, B_HEADS, B_HDIM, B_HDIM), F32)]
    out_specs = [pl.BlockSpec((rows, D_MODEL), tile),
                 pl.BlockSpec((n_seq, B_HEADS, B_HDIM, B_HDIM), lambda b, j: (b, 0, 0, 0))]
    if emit_v:
        out_shape.append(jax.ShapeDtypeStruct((x.shape[0], A_DIM), F32))
        out_specs.append(pl.BlockSpec((rows, A_DIM), tile))
    scratch = [pltpu.VMEM((rows, EVEN_IN), F32), pltpu.VMEM((rows, D_MODEL), F32),
               pltpu.VMEM((n_seq * B_HEADS, B_HDIM, B_HDIM), F32),
               pltpu.VMEM((D_MODEL, EVEN_IN), BF16), pltpu.VMEM((D_MODEL, D_MODEL), BF16)]
    if side_cast is not None:
        in_specs.append(pl.BlockSpec(memory_space=pl.ANY))
        args.append(side_cast)
        out_shape.append(jax.ShapeDtypeStruct(side_cast.shape, BF16))
        out_specs.append(pl.BlockSpec(memory_space=pl.ANY))
        scratch += _cast_scratch(side_cast, n_batch * tiles_per_batch)
    body = functools.partial(_even_mixer_body, n_seq=n_seq, seq_rows=seq_rows, gchunk=gchunk, emit_v=emit_v,
                             side_cast=side_cast is not None)
    return pl.pallas_call(
        body, grid=grid, in_specs=in_specs, out_specs=out_specs, out_shape=out_shape, scratch_shapes=scratch,
        compiler_params=_cparams(2),
        name="even_mixer_s" if emit_v else "even_mixer_p")(*args)


def _memory_kv_body(mem_ref, gm_ref, wk_ref, wv_ref, kg_ref, mk_ref, mv_ref, *, n_seq, n_mem):
    m = _rms(mem_ref[...], gm_ref[0]).astype(BF16)
    kk = _dot(m, wk_ref[0].astype(BF16))
    vv = _dot(m, wv_ref[0].astype(BF16))
    kg = kg_ref[0]
    for hd in range(X_HEADS):
        hs = slice(hd * X_HDIM, (hd + 1) * X_HDIM)
        kh = _rms(kk[:, hs], kg)
        for s in range(n_seq):
            mk_ref[0, s, pl.ds(hd, n_mem, stride=X_HEADS), :] = kh[s * n_mem:(s + 1) * n_mem]
            mv_ref[0, s, pl.ds(hd, n_mem, stride=X_HEADS), :] = vv[s * n_mem:(s + 1) * n_mem, hs]


def _memory_kv(mem2d, g_mem, w_k, w_v, k_g, *, n_batch, n_mem):
    depth = w_k.shape[0]
    n_seq = ROW_TILE // n_mem
    out = jax.ShapeDtypeStruct((depth, n_batch, n_mem * X_HEADS, X_HDIM), F32)
    ospec = pl.BlockSpec((1, n_seq, n_mem * X_HEADS, X_HDIM), lambda l, t: (l, t, 0, 0))
    return pl.pallas_call(
        functools.partial(_memory_kv_body, n_seq=n_seq, n_mem=n_mem),
        grid=(depth, n_batch // n_seq),
        in_specs=[pl.BlockSpec((ROW_TILE, D_MODEL), lambda l, t: (t, 0)),
                  pl.BlockSpec((1, 1, D_MODEL), lambda l, t: (l, 0, 0)),
                  pl.BlockSpec((1, D_MODEL, X_DIM), lambda l, t: (l, 0, 0)),
                  pl.BlockSpec((1, D_MODEL, X_DIM), lambda l, t: (l, 0, 0)),
                  pl.BlockSpec((1, 1, X_HDIM), lambda l, t: (l, 0, 0))],
        out_specs=[ospec, ospec], out_shape=[out, out], compiler_params=_cparams(2),
        name="memory_kv")(mem2d, g_mem, w_k, w_v, k_g)


def _xattn_body(x_ref, mk_ref, mv_ref, nrm_ref, wq32_ref, wo32_ref, qg_ref, *rest, n_seq, seq_rows, n_side):
    side_srcs, y_ref, side_dsts = rest[:n_side], rest[n_side], rest[n_side + 1:2 * n_side + 1]
    o_scr, wq_ref, wo_ref = rest[2 * n_side + 1:2 * n_side + 4]
    _run_side_casts(pl.program_id(0), side_srcs, side_dsts, rest[2 * n_side + 4:])

    @pl.when(pl.program_id(0) == 0)
    def _():
        wq_ref[...] = wq32_ref[...].astype(BF16)
        wo_ref[...] = wo32_ref[...].astype(BF16)

    x = x_ref[...]
    h = _rms(x, nrm_ref[...]).astype(BF16)
    q = _dot(h, wq_ref[...])
    qg = qg_ref[...] * (X_HDIM ** -0.5)
    units = [(s, hd) for s in range(n_seq) for hd in range(X_HEADS)]
    rows_of = lambda s: slice(s * seq_rows, (s + 1) * seq_rows)
    lanes_of = lambda hd: slice(hd * X_HDIM, (hd + 1) * X_HDIM)
    n_mem = mk_ref.shape[1] // X_HEADS
    head_rows = lambda hd: pl.ds(hd, n_mem, stride=X_HEADS)
    qh = [_rms(q[rows_of(s), lanes_of(hd)], qg).astype(BF16) for s, hd in units]
    sc = [_dot_nt(qh[n], mk_ref[s, head_rows(hd), :].astype(BF16)) for n, (s, hd) in enumerate(units)]
    m = [jnp.max(t, axis=-1, keepdims=True) for t in sc]
    p = [jnp.exp(t - mx) for t, mx in zip(sc, m)]
    den = [jnp.sum(t, axis=-1, keepdims=True) for t in p]
    for n, (s, hd) in enumerate(units):
        o_scr[rows_of(s), lanes_of(hd)] = _dot(p[n].astype(BF16), mv_ref[s, head_rows(hd), :].astype(BF16)) / den[n]
    y_ref[...] = x + _dot(o_scr[...].astype(BF16), wo_ref[...])


def _xattn(x, mk, mv, nrm, wq, wo, qg, *, layer, steps_per_mem, n_seq, seq_rows, side_casts=()):
    rows = n_seq * seq_rows
    n_steps = x.shape[0] // rows
    mem_spec = pl.BlockSpec((None, n_seq) + mk.shape[2:], lambda t: (layer, t // steps_per_mem, 0, 0))
    whole = lambda *shape: pl.BlockSpec(shape, lambda t: (0,) * len(shape))
    once = lambda *shape: pl.BlockSpec(shape, lambda t: (0,) * len(shape), pipeline_mode=pl.Buffered(1))
    side_in, side_out, side_shapes, side_scratch = _side_cast_plumbing(side_casts, n_steps)
    out = pl.pallas_call(
        functools.partial(_xattn_body, n_seq=n_seq, seq_rows=seq_rows, n_side=len(side_casts)),
        grid=(n_steps,),
        in_specs=[pl.BlockSpec((rows, D_MODEL), lambda t: (t, 0)), mem_spec, mem_spec,
                  whole(1, D_MODEL), once(D_MODEL, X_DIM), once(X_DIM, D_MODEL), whole(1, X_HDIM)] + side_in,
        out_specs=[pl.BlockSpec((rows, D_MODEL), lambda t: (t, 0))] + side_out,
        out_shape=[jax.ShapeDtypeStruct(x.shape, F32)] + side_shapes,
        scratch_shapes=[pltpu.VMEM((rows, X_DIM), F32), pltpu.VMEM((D_MODEL, X_DIM), BF16),
                        pltpu.VMEM((X_DIM, D_MODEL), BF16)] + side_scratch,
        compiler_params=_cparams(1),
        name="xattn_s" if n_seq > 1 else "xattn_p")(x, mk, mv, nrm, wq, wo, qg, *side_casts)
    return out if side_casts else out[0]


def _two_source_specs(n_prompt_tiles, width=D_MODEL):
    return (pl.BlockSpec((ROW_TILE, width), lambda i, *_: (jnp.minimum(i, n_prompt_tiles - 1), 0)),
            pl.BlockSpec((ROW_TILE, width), lambda i, *_: (jnp.maximum(i - n_prompt_tiles, 0), 0)))


def _on_row_source(i, n_prompt_tiles, fn, prompt_refs, sample_refs):
    pl.when(i < n_prompt_tiles)(lambda: fn(*prompt_refs))
    pl.when(i >= n_prompt_tiles)(lambda: fn(*sample_refs))


def _swiglu_part(hb, w1, w3, w2):
    part = None
    for c0 in range(0, D_FF, FF_SUB):
        cs = slice(c0, min(c0 + FF_SUB, D_FF))
        act = (_silu(_dot(hb, w1[:, cs])) * _dot(hb, w3[:, cs])).astype(BF16)
        p = _dot(act, w2[cs, :])
        part = p if part is None else part + p
    return part


def _dense_ffn_body(xp_ref, xs_ref, nrm_ref, w1_ref, w3_ref, w2_ref, *rest, n_prompt_tiles, n_side):
    side_srcs, (yp_ref, ys_ref), side_dsts = rest[:n_side], rest[n_side:n_side + 2], rest[n_side + 2:2 * n_side + 2]
    i = pl.program_id(0)
    _run_side_casts(i, side_srcs, side_dsts, rest[2 * n_side + 2:])

    def run(x_ref, y_ref):
        x = x_ref[...]
        y_ref[...] = x + _swiglu_part(_rms(x, nrm_ref[...]).astype(BF16), w1_ref, w3_ref, w2_ref)

    _on_row_source(i, n_prompt_tiles, run, (xp_ref, yp_ref), (xs_ref, ys_ref))


def _dense_ffn(xp, xs, nrm, w1, w3, w2, side_casts):
    n_prompt_tiles = xp.shape[0] // ROW_TILE
    n_tiles = n_prompt_tiles + xs.shape[0] // ROW_TILE
    once = lambda *shape: pl.BlockSpec(shape, lambda i: (0,) * len(shape), pipeline_mode=pl.Buffered(1))
    side_in, side_out, side_shapes, side_scratch = _side_cast_plumbing(side_casts, n_tiles)
    return pl.pallas_call(
        functools.partial(_dense_ffn_body, n_prompt_tiles=n_prompt_tiles, n_side=len(side_casts)), grid=(n_tiles,),
        in_specs=[*_two_source_specs(n_prompt_tiles), pl.BlockSpec((1, D_MODEL), lambda i: (0, 0)),
                  once(D_MODEL, D_FF), once(D_MODEL, D_FF), once(D_FF, D_MODEL)] + side_in,
        out_specs=[*_two_source_specs(n_prompt_tiles)] + side_out,
        out_shape=[jax.ShapeDtypeStruct(xp.shape, F32), jax.ShapeDtypeStruct(xs.shape, F32)] + side_shapes,
        scratch_shapes=side_scratch,
        compiler_params=_cparams(1), name="dense_ffn")(xp, xs, nrm, w1, w3, w2, *side_casts)


def _grouped_ffn_body(te_ref, tv_ref, x_ref, w1_ref, w3_ref, w2_ref, y_ref):
    valid = tv_ref[pl.program_id(0)]
    half = ROW_TILE // 2
    weights = (w1_ref.at[0], w3_ref.at[0], w2_ref.at[0])

    @pl.when(valid > half)
    def _():
        y_ref[...] = _swiglu_part(x_ref[...].astype(BF16), *weights)

    @pl.when((valid > 0) & (valid <= half))
    def _():
        y_ref[0:half] = _swiglu_part(x_ref[0:half].astype(BF16), *weights)
        y_ref[half:ROW_TILE] = jnp.zeros((ROW_TILE - half, D_MODEL), F32)

    @pl.when(valid == 0)
    def _():
        y_ref[...] = jnp.zeros_like(y_ref)


def _grouped_ffn(xsorted, tile_expert, tile_valid, w1, w3, w2):
    n_tiles = xsorted.shape[0] // ROW_TILE
    grid_spec = pltpu.PrefetchScalarGridSpec(
        num_scalar_prefetch=2, grid=(n_tiles,),
        in_specs=[pl.BlockSpec((ROW_TILE, D_MODEL), lambda i, te, tv: (i, 0)),
                  pl.BlockSpec((1, D_MODEL, D_FF), lambda i, te, tv: (te[i], 0, 0)),
                  pl.BlockSpec((1, D_MODEL, D_FF), lambda i, te, tv: (te[i], 0, 0)),
                  pl.BlockSpec((1, D_FF, D_MODEL), lambda i, te, tv: (te[i], 0, 0))],
        out_specs=pl.BlockSpec((ROW_TILE, D_MODEL), lambda i, te, tv: (i, 0)))
    return pl.pallas_call(
        _grouped_ffn_body, grid_spec=grid_spec, out_shape=jax.ShapeDtypeStruct(xsorted.shape, F32),
        compiler_params=_cparams(1), name="grouped_ffn")(tile_expert, tile_valid, xsorted, w1, w3, w2)


def _rope_slab(xs, cos, sin_signed, first_half):
    rot = jnp.where(first_half, pltpu.roll(xs, LANES - C_HDIM // 2, 1), pltpu.roll(xs, C_HDIM // 2, 1))
    return xs * cos + rot * sin_signed


def _swa_body(sink_ref, x_ref, pk_ref, pv_ref, cos_ref, sin_ref, nrm_ref, win32_ref, wout32_ref, qg_ref, kg_ref,
              bd_ref, *rest, n_seq, seq_rows, past_valid, side_cast):
    rest = list(rest)
    side_src = rest.pop(0) if side_cast else None
    y_ref, ko_ref, vo_ref = rest.pop(0), rest.pop(0), rest.pop(0)
    side_dst = rest.pop(0) if side_cast else None
    k_scr, v_scr, q_scr, a_scr, win_ref, wout_ref = rest[:6]
    rows = n_seq * seq_rows
    j = pl.program_id(1)
    tail = min(WINDOW, seq_rows)
    if side_cast:
        _cast_rows_step(pl.program_id(0) * pl.num_programs(1) + j, side_src, side_dst, *rest[6:])

    @pl.when((pl.program_id(0) == 0) & (j == 0))
    def _():
        win_ref[...] = win32_ref[...].astype(BF16)
        wout_ref[...] = wout32_ref[...].astype(BF16)

    if past_valid:
        for s in range(n_seq):
            k_scr[s, 0:WINDOW] = pk_ref[s]
            v_scr[s, 0:WINDOW] = pv_ref[s]
    else:
        @pl.when(j == 0)
        def _():
            for s in range(n_seq):
                k_scr[s, 0:WINDOW] = jnp.zeros((WINDOW, C_KV_DIM), F32)
                v_scr[s, 0:WINDOW] = jnp.zeros((WINDOW, C_KV_DIM), F32)

        @pl.when(j > 0)
        def _():
            for s in range(n_seq):
                k_scr[s, 0:WINDOW] = k_scr[s, seq_rows:seq_rows + WINDOW]
                v_scr[s, 0:WINDOW] = v_scr[s, seq_rows:seq_rows + WINDOW]

    x = x_ref[...]
    h = _rms(x, nrm_ref[...]).astype(BF16)
    q_dim = C_HEADS * C_HDIM
    q = _dot(h, win_ref[:, 0:q_dim])
    k = _dot(h, win_ref[:, q_dim:q_dim + C_KV_DIM])
    v = _dot(h, win_ref[:, q_dim + C_KV_DIM:q_dim + 2 * C_KV_DIM])

    cos = cos_ref[...]
    sin_signed = sin_ref[...]
    first_half = (lax.broadcasted_iota(jnp.int32, (rows, LANES), 1) % C_HDIM) < (C_HDIM // 2)
    bd = bd_ref[...]
    qn = q * lax.rsqrt(_dot((q * q).astype(BF16), bd) + EPS) * qg_ref[...]
    kn = k * lax.rsqrt(_dot((k * k).astype(BF16), bd[0:C_KV_DIM, 0:C_KV_DIM]) + EPS) * kg_ref[...]
    scale = C_HDIM ** -0.5
    for sl in range(q_dim // LANES):
        ls = slice(sl * LANES, (sl + 1) * LANES)
        q_scr[:, ls] = _rope_slab(qn[:, ls], cos, sin_signed, first_half) * scale
    for sl in range(C_KV_DIM // LANES):
        ls = slice(sl * LANES, (sl + 1) * LANES)
        kr = _rope_slab(kn[:, ls], cos, sin_signed, first_half)
        for s in range(n_seq):
            k_scr[s, WINDOW:WINDOW + seq_rows, ls] = kr[s * seq_rows:(s + 1) * seq_rows]
    for s in range(n_seq):
        v_scr[s, WINDOW:WINDOW + seq_rows] = v[s * seq_rows:(s + 1) * seq_rows]
        ko_ref[s] = k_scr[s, WINDOW + seq_rows - tail:WINDOW + seq_rows]
        vo_ref[s] = v_scr[s, WINDOW + seq_rows - tail:WINDOW + seq_rows]

    chunks_per_seq = seq_rows // CHUNK
    n_keys = WINDOW + CHUNK
    key_idx = lax.broadcasted_iota(jnp.int32, (C_GROUP * CHUNK, n_keys), 1)
    row_grp = lax.broadcasted_iota(jnp.int32, (C_GROUP * CHUNK, 1), 0) // CHUNK
    ones_keys = jnp.ones((n_keys, C_HDIM), BF16)

    head_sinks = []
    for kvh in range(C_KV_HEADS):
        sink = jnp.zeros((C_GROUP * CHUNK, 1), F32)
        for g in range(C_GROUP):
            sink = jnp.where(row_grp == g, sink_ref[kvh * C_GROUP + g], sink)
        head_sinks.append(sink)

    def attend(chunk_ids):
        units = [(idx, kvh) for idx in chunk_ids for kvh in range(C_KV_HEADS)]
        head_lanes = lambda kvh: slice(kvh * C_HDIM, (kvh + 1) * C_HDIM)
        q_rows = lambda idx: slice(idx * CHUNK, (idx + 1) * CHUNK)
        key_rows = lambda idx: slice((idx % chunks_per_seq) * CHUNK, (idx % chunks_per_seq) * CHUNK + n_keys)
        qs = [jnp.concatenate(
            [q_scr[q_rows(idx), (kvh * C_GROUP + g) * C_HDIM:(kvh * C_GROUP + g + 1) * C_HDIM]
             for g in range(C_GROUP)], axis=0).astype(BF16) for idx, kvh in units]
        kh = [k_scr[idx // chunks_per_seq, key_rows(idx), head_lanes(kvh)].astype(BF16) for idx, kvh in units]
        vh = [v_scr[idx // chunks_per_seq, key_rows(idx), head_lanes(kvh)].astype(BF16) for idx, kvh in units]
        sc = [_dot_nt(qn, kn) for qn, kn in zip(qs, kh)]
        if not past_valid:
            sc = [jnp.where((key_idx + (idx % chunks_per_seq) * CHUNK >= WINDOW) | (j > 0), sn, -jnp.inf)
                  for sn, (idx, kvh) in zip(sc, units)]
        m = [jnp.maximum(jnp.max(sn, axis=-1, keepdims=True), head_sinks[kvh]) for sn, (idx, kvh) in zip(sc, units)]
        p = [jnp.exp(sn - mn).astype(BF16) for sn, mn in zip(sc, m)]
        den = [_dot(pn, ones_keys) + jnp.exp(head_sinks[kvh] - mn) for pn, mn, (idx, kvh) in zip(p, m, units)]
        o = [_dot(pn, vn) / dn for pn, vn, dn in zip(p, vh, den)]
        for on, (idx, kvh) in zip(o, units):
            for g in range(C_GROUP):
                hs = slice((kvh * C_GROUP + g) * C_HDIM, (kvh * C_GROUP + g + 1) * C_HDIM)
                a_scr[q_rows(idx), hs] = on[g * CHUNK:(g + 1) * CHUNK]

    n_chunks = rows // CHUNK
    for first in range(0, n_chunks, SWA_STAGE_CHUNKS):
        attend(range(first, min(first + SWA_STAGE_CHUNKS, n_chunks)))
    y_ref[...] = x + _dot(a_scr[...].astype(BF16), wout_ref[...])


def _swa(x, pk, pv, cos, sin_signed, sinks, consts, *, n_batch, tiles_per_batch, n_seq, seq_rows, past_valid,
         side_cast=None):
    rows = n_seq * seq_rows
    tail = min(WINDOW, seq_rows)
    n_str = n_batch * n_seq
    whole = lambda *shape: pl.BlockSpec(shape, lambda b, j, sk: (0,) * len(shape))
    once = lambda *shape: pl.BlockSpec(shape, lambda b, j, sk: (0,) * len(shape), pipeline_mode=pl.Buffered(1))
    in_specs = [pl.BlockSpec((rows, D_MODEL), lambda b, j, sk: (b * tiles_per_batch + j, 0)),
                pl.BlockSpec((n_seq, WINDOW, C_KV_DIM), lambda b, j, sk: (b, 0, 0)),
                pl.BlockSpec((n_seq, WINDOW, C_KV_DIM), lambda b, j, sk: (b, 0, 0)),
                pl.BlockSpec((rows, LANES), lambda b, j, sk: (j, 0)),
                pl.BlockSpec((rows, LANES), lambda b, j, sk: (j, 0)),
                whole(1, D_MODEL), once(D_MODEL, ODD_IN), once(D_MODEL, D_MODEL),
                whole(1, D_MODEL), whole(1, C_KV_DIM), whole(D_MODEL, D_MODEL)]
    args = [x, pk, pv, cos, sin_signed] + list(consts)
    kv_out = jax.ShapeDtypeStruct((n_str, tail, C_KV_DIM), F32)
    kv_spec = pl.BlockSpec((n_seq, tail, C_KV_DIM), lambda b, j, sk: (b, 0, 0))
    out_specs = [pl.BlockSpec((rows, D_MODEL), lambda b, j, sk: (b * tiles_per_batch + j, 0)), kv_spec, kv_spec]
    out_shape = [jax.ShapeDtypeStruct(x.shape, F32), kv_out, kv_out]
    scratch = [pltpu.VMEM((n_seq, WINDOW + seq_rows, C_KV_DIM), F32),
               pltpu.VMEM((n_seq, WINDOW + seq_rows, C_KV_DIM), F32),
               pltpu.VMEM((rows, D_MODEL), F32), pltpu.VMEM((rows, D_MODEL), F32),
               pltpu.VMEM((D_MODEL, ODD_IN), BF16), pltpu.VMEM((D_MODEL, D_MODEL), BF16)]
    if side_cast is not None:
        in_specs.append(pl.BlockSpec(memory_space=pl.ANY))
        args.append(side_cast)
        out_specs.append(pl.BlockSpec(memory_space=pl.ANY))
        out_shape.append(jax.ShapeDtypeStruct(side_cast.shape, BF16))
        scratch += _cast_scratch(side_cast, n_batch * tiles_per_batch)
    grid_spec = pltpu.PrefetchScalarGridSpec(
        num_scalar_prefetch=1, grid=(n_batch, tiles_per_batch), in_specs=in_specs, out_specs=out_specs,
        scratch_shapes=scratch)
    return pl.pallas_call(
        functools.partial(_swa_body, n_seq=n_seq, seq_rows=seq_rows, past_valid=past_valid,
                          side_cast=side_cast is not None),
        grid_spec=grid_spec, out_shape=out_shape,
        compiler_params=_cparams(2),
        name="swa_s" if past_valid else "swa_p")(sinks, *args)


BF16_ROWS = 16


def _cast_piece_rows(w, n_grid_steps):
    n_cast = 1 << (n_grid_steps.bit_length() - 1)
    while w.shape[0] % (n_cast * BF16_ROWS):
        n_cast //= 2
    return w.shape[0] // n_cast


def _cast_rows_step(step, src_hbm, dst_hbm, in_stage, out_stage, sems):
    n_rows = in_stage.shape[1]
    n_cast = src_hbm.shape[0] // n_rows
    slot = step % 2

    def read(s, sl):
        return pltpu.make_async_copy(src_hbm.at[pl.ds(pl.multiple_of(s * n_rows, 16), n_rows)], in_stage.at[sl],
                                     sems.at[sl])

    def write(s, sl):
        return pltpu.make_async_copy(out_stage.at[sl], dst_hbm.at[pl.ds(pl.multiple_of(s * n_rows, 16), n_rows)],
                                     sems.at[2 + sl])

    @pl.when(step == 0)
    def _():
        read(0, 0).start()

    @pl.when(step < n_cast)
    def _():
        @pl.when(step + 1 < n_cast)
        def _():
            read(step + 1, 1 - slot).start()

        read(step, slot).wait()

        @pl.when(step >= 2)
        def _():
            write(step - 2, slot).wait()

        out_stage[slot] = in_stage[slot].astype(BF16)
        write(step, slot).start()

        @pl.when(step == n_cast - 1)
        def _():
            write(step, slot).wait()
            if n_cast > 1:
                write(step - 1, 1 - slot).wait()


def _cast_scratch(w, n_grid_steps):
    rows = _cast_piece_rows(w, n_grid_steps)
    return [pltpu.VMEM((2, rows, w.shape[1]), F32), pltpu.VMEM((2, rows, w.shape[1]), BF16),
            pltpu.SemaphoreType.DMA((4,))]


def _side_cast_plumbing(side_casts, n_grid_steps):
    any_space = pl.BlockSpec(memory_space=pl.ANY)
    scratch = [s for w in side_casts for s in _cast_scratch(w, n_grid_steps)]
    return ([any_space] * len(side_casts), [any_space] * len(side_casts),
            [jax.ShapeDtypeStruct(w.shape, BF16) for w in side_casts], scratch)


def _run_side_casts(step, srcs, dsts, scratch):
    for n, (src, dst) in enumerate(zip(srcs, dsts)):
        _cast_rows_step(step, src, dst, *scratch[3 * n:3 * n + 3])


def _router_body(xp_ref, xs_ref, nrm_ref, rboth_ref, lstrict_ref, su_ref, info_ref, cnt_ref, *, n_prompt_tiles):
    def run(x_ref):
        h = _rms(x_ref[...], nrm_ref[...])
        hi = h.astype(BF16)
        lo = (h - hi.astype(F32)).astype(BF16)
        both = _dot(hi, rboth_ref[...])
        logits = both[:, 0:LANES] + both[:, LANES:2 * LANES] + _dot(lo, rboth_ref[:, 0:LANES])
        lane = lax.broadcasted_iota(jnp.int32, logits.shape, 1).astype(F32)
        logits = jnp.where(lane < N_EXPERTS, logits, -jnp.inf)
        l1 = jnp.max(logits, axis=-1, keepdims=True)
        i1 = jnp.min(jnp.where(logits == l1, lane, float(LANES)), axis=-1, keepdims=True)
        rest = jnp.where(lane == i1, -jnp.inf, logits)
        l2 = jnp.max(rest, axis=-1, keepdims=True)
        i2 = jnp.min(jnp.where(rest == l2, lane, float(LANES)), axis=-1, keepdims=True)
        e = jnp.exp(l2 - l1)
        g1 = 1.0 / (1.0 + e)
        g2 = e * g1
        oh1 = (lane == i1).astype(F32)
        oh2 = (lane == i2).astype(F32)
        cnt1 = jnp.sum(oh1, axis=0, keepdims=True)
        cnt2 = jnp.sum(oh2, axis=0, keepdims=True)
        seg = jnp.broadcast_to(jnp.ceil((cnt1 + cnt2) * (1.0 / GROUP)), (GROUP, LANES))
        off = _dot(seg.astype(BF16), su_ref[...])[0:1] * GROUP
        c1 = _dot(lstrict_ref[...], oh1.astype(BF16)) + off
        c2 = _dot(lstrict_ref[...], oh2.astype(BF16)) + off + cnt1
        d1 = jnp.sum(jnp.where(lane == i1, c1, 0.0), axis=-1, keepdims=True)
        d2 = jnp.sum(jnp.where(lane == i2, c2, 0.0), axis=-1, keepdims=True)
        info = jnp.zeros(logits.shape, F32)
        for n, col in enumerate([d1, d2, g1, g2]):
            info = jnp.where(lane == n, col, info)
        info_ref[...] = info
        cnt_ref[...] = seg * GROUP

    _on_row_source(pl.program_id(0), n_prompt_tiles, run, (xp_ref,), (xs_ref,))


def _router(xp, xs, nrm, rboth, lstrict, su):
    n_prompt_tiles = xp.shape[0] // ROW_TILE
    n_tiles = n_prompt_tiles + xs.shape[0] // ROW_TILE
    whole = lambda *shape: pl.BlockSpec(shape, lambda i: (0,) * len(shape))
    return pl.pallas_call(
        functools.partial(_router_body, n_prompt_tiles=n_prompt_tiles), grid=(n_tiles,),
        in_specs=[*_two_source_specs(n_prompt_tiles), whole(1, D_MODEL), whole(D_MODEL, 2 * LANES),
                  whole(ROW_TILE, ROW_TILE), whole(LANES, LANES)],
        out_specs=[pl.BlockSpec((ROW_TILE, LANES), lambda i: (i, 0)), pl.BlockSpec((GROUP, LANES), lambda i: (i, 0))],
        out_shape=[jax.ShapeDtypeStruct((n_tiles * ROW_TILE, LANES), F32),
                   jax.ShapeDtypeStruct((n_tiles * GROUP, LANES), F32)],
        compiler_params=_cparams(1), name="router")(xp, xs, nrm, rboth, lstrict, su)


BIG_ROWS = 8 * GROUP
SEG_FIELDS = 4
TILE_FIELDS = 2
TAIL_FIELDS = 3


def _rows_copy(src_ref, src_row, dst_ref, dst_row, n_rows, sem):
    return pltpu.make_async_copy(src_ref.at[pl.ds(pl.multiple_of(src_row, GROUP), n_rows)],
                                 dst_ref.at[pl.ds(pl.multiple_of(dst_row, GROUP), n_rows)], sem)


def _start_pieces(src_ref, src0, dst_ref, dst0, n_big, n_small, sem):
    def big(g, carry):
        _rows_copy(src_ref, src0 + g * BIG_ROWS, dst_ref, dst0 + g * BIG_ROWS, BIG_ROWS, sem).start()
        return carry

    lax.fori_loop(0, n_big, big, 0)
    done = n_big * BIG_ROWS

    def small(g, carry):
        _rows_copy(src_ref, src0 + done + g * GROUP, dst_ref, dst0 + done + g * GROUP, GROUP, sem).start()
        return carry

    lax.fori_loop(0, n_small, small, 0)


def _wait_pieces(n_big, n_small, src_ref, dst_ref, sem):
    def big(g, carry):
        _rows_copy(src_ref, 0, dst_ref, 0, BIG_ROWS, sem).wait()
        return carry

    lax.fori_loop(0, n_big, big, 0)

    def small(g, carry):
        _rows_copy(src_ref, 0, dst_ref, 0, GROUP, sem).wait()
        return carry

    lax.fori_loop(0, n_small, small, 0)


def _segment_table(seg, n_sorted_tiles):
    pieces = lambda rows: (rows // BIG_ROWS, (rows % BIG_ROWS) // GROUP)
    local_off = jnp.cumsum(seg, axis=1) - seg
    expert_rows = jnp.sum(seg, axis=0)
    expert_pad = ((expert_rows + ROW_TILE - 1) // ROW_TILE) * ROW_TILE
    expert_end = jnp.cumsum(expert_pad)
    expert_off = expert_end - expert_pad
    seg_start = expert_off[None, :] + jnp.cumsum(seg, axis=0) - seg
    n_used = (expert_end[-1] // ROW_TILE).reshape(1)
    tile_ids = jnp.arange(n_sorted_tiles, dtype=jnp.int32)
    tile_expert = jnp.minimum(jnp.sum(tile_ids[:, None] >= (expert_end // ROW_TILE)[None, :], axis=1), N_EXPERTS - 1)
    tile_expert = jnp.where(tile_ids < n_used[0], tile_expert, tile_expert[n_used[0] - 1])
    tile_valid = jnp.clip((expert_off + expert_rows)[tile_expert] - tile_ids * ROW_TILE, 0, ROW_TILE)
    tile_valid = jnp.where(tile_ids < n_used[0], tile_valid, 0)
    seg_big, seg_small = pieces(seg)
    tail_big, tail_small = pieces(expert_pad - expert_rows)
    table = jnp.concatenate([
        jnp.stack([seg_start, seg_big, seg_small, local_off], axis=-1).reshape(-1),
        jnp.stack([jnp.sum(seg_big, axis=1), jnp.sum(seg_small, axis=1)], axis=-1).reshape(-1),
        jnp.stack([expert_off + expert_rows, tail_big, tail_small], axis=-1).reshape(-1),
        n_used])
    return table.astype(jnp.int32), tile_expert.astype(jnp.int32), tile_valid.astype(jnp.int32)


def _table_sections(n_tiles):
    tiles0 = SEG_FIELDS * n_tiles * N_EXPERTS
    tails0 = tiles0 + TILE_FIELDS * n_tiles
    return tiles0, tails0, tails0 + TAIL_FIELDS * N_EXPERTS


def _dispatch_body(tab_ref, info_ref, xp_ref, xs_ref, nrm_ref, sel_ref, out_ref, loc_scr, z_scr, sems, *,
                   n_prompt_tiles, n_tiles, n_sorted_tiles):
    i = pl.program_id(0)
    slot = i % 2
    tiles0, tails0, used0 = _table_sections(n_tiles)

    def wait_tile(t, s):
        _wait_pieces(tab_ref[tiles0 + TILE_FIELDS * t], tab_ref[tiles0 + TILE_FIELDS * t + 1], loc_scr.at[s], out_ref,
                     sems.at[s])

    @pl.when(i >= 2)
    def _():
        wait_tile(i - 2, slot)

    def run(x_ref):
        hb = _rms(x_ref[...], nrm_ref[...]).astype(BF16)
        info = info_ref[...]
        lane = lax.broadcasted_iota(jnp.int32, info.shape, 1)
        dest = jnp.where(lane < TOP_K, info, 0.0)
        drow = sum(_dot_nt(sel_ref[...], part) for part in _split3(dest))
        r = lax.broadcasted_iota(jnp.int32, (LOCAL_ROWS, ROW_TILE), 0).astype(F32)
        perm = jnp.where(r == drow[0:1], 1.0, jnp.where(r == drow[1:2], 1.0, 0.0)).astype(BF16)
        loc_scr[slot] = _dot(perm, hb)

    _on_row_source(i, n_prompt_tiles, run, (xp_ref,), (xs_ref,))

    for e in range(N_EXPERTS):
        base = SEG_FIELDS * (i * N_EXPERTS + e)
        _start_pieces(loc_scr.at[slot], tab_ref[base + 3], out_ref, tab_ref[base], tab_ref[base + 1],
                      tab_ref[base + 2], sems.at[slot])

    @pl.when(i == n_tiles - 1)
    def _():
        wait_tile(i, slot)
        if n_tiles > 1:
            wait_tile(i - 1, 1 - slot)
        z_scr[...] = jnp.zeros_like(z_scr)
        for e in range(N_EXPERTS):
            base = tails0 + TAIL_FIELDS * e
            _start_pieces(z_scr, 0, out_ref, tab_ref[base], tab_ref[base + 1], tab_ref[base + 2], sems.at[2])
            _wait_pieces(tab_ref[base + 1], tab_ref[base + 2], z_scr, out_ref, sems.at[2])

        def zero_tile(t, carry):
            cp = _rows_copy(z_scr, 0, out_ref, t * ROW_TILE, ROW_TILE, sems.at[2])
            cp.start()
            cp.wait()
            return carry

        lax.fori_loop(tab_ref[used0], n_sorted_tiles, zero_tile, 0)


def _dispatch(table, info, xp, xs, nrm, sel, n_sorted_tiles):
    n_prompt_tiles = xp.shape[0] // ROW_TILE
    n_tiles = n_prompt_tiles + xs.shape[0] // ROW_TILE
    whole = lambda *shape: pl.BlockSpec(shape, lambda i, tab: (0,) * len(shape))
    grid_spec = pltpu.PrefetchScalarGridSpec(
        num_scalar_prefetch=1, grid=(n_tiles,),
        in_specs=[pl.BlockSpec((ROW_TILE, LANES), lambda i, tab: (i, 0)), *_two_source_specs(n_prompt_tiles),
                  whole(1, D_MODEL), whole(GROUP, LANES)],
        out_specs=pl.BlockSpec(memory_space=pl.ANY),
        scratch_shapes=[pltpu.VMEM((2, LOCAL_ROWS, D_MODEL), F32), pltpu.VMEM((ROW_TILE, D_MODEL), F32),
                        pltpu.SemaphoreType.DMA((3,))])
    return pl.pallas_call(
        functools.partial(_dispatch_body, n_prompt_tiles=n_prompt_tiles, n_tiles=n_tiles,
                          n_sorted_tiles=n_sorted_tiles),
        grid_spec=grid_spec, out_shape=jax.ShapeDtypeStruct((n_sorted_tiles * ROW_TILE, D_MODEL), F32),
        compiler_params=_cparams(1), name="moe_dispatch")(table, info, xp, xs, nrm, sel)


def _combine_body(tab_ref, info_ref, xp_ref, xs_ref, e_ref, yp_ref, ys_ref, loc_scr, sems, *, n_prompt_tiles,
                  n_tiles):
    i = pl.program_id(0)
    slot = i % 2
    tiles0, _, _ = _table_sections(n_tiles)

    def fetch(t, s):
        for e in range(N_EXPERTS):
            base = SEG_FIELDS * (t * N_EXPERTS + e)
            _start_pieces(e_ref, tab_ref[base], loc_scr.at[s], tab_ref[base + 3], tab_ref[base + 1],
                          tab_ref[base + 2], sems.at[s])

    @pl.when(i == 0)
    def _():
        loc_scr[...] = jnp.zeros_like(loc_scr)
        fetch(0, 0)

    @pl.when(i + 1 < n_tiles)
    def _():
        fetch(i + 1, 1 - slot)

    _wait_pieces(tab_ref[tiles0 + TILE_FIELDS * i], tab_ref[tiles0 + TILE_FIELDS * i + 1], e_ref, loc_scr.at[slot],
                 sems.at[slot])

    def run(x_ref, y_ref):
        info = info_ref[...]
        eb = loc_scr[slot].astype(BF16)
        lane = lax.broadcasted_iota(jnp.int32, (ROW_TILE, LOCAL_ROWS), 1).astype(F32)
        pick = jnp.zeros((ROW_TILE, LOCAL_ROWS), F32)
        for k in range(TOP_K):
            pick = jnp.where(lane == info[:, k:k + 1], info[:, TOP_K + k:TOP_K + k + 1], pick)
        y_ref[...] = x_ref[...] + _dot(pick.astype(BF16), eb)

    _on_row_source(i, n_prompt_tiles, run, (xp_ref, yp_ref), (xs_ref, ys_ref))


def _combine(table, info, xp, xs, esorted):
    n_prompt_tiles = xp.shape[0] // ROW_TILE
    n_tiles = n_prompt_tiles + xs.shape[0] // ROW_TILE
    grid_spec = pltpu.PrefetchScalarGridSpec(
        num_scalar_prefetch=1, grid=(n_tiles,),
        in_specs=[pl.BlockSpec((ROW_TILE, LANES), lambda i, tab: (i, 0)), *_two_source_specs(n_prompt_tiles),
                  pl.BlockSpec(memory_space=pl.ANY)],
        out_specs=list(_two_source_specs(n_prompt_tiles)),
        scratch_shapes=[pltpu.VMEM((2, LOCAL_ROWS, D_MODEL), F32), pltpu.SemaphoreType.DMA((2,))])
    return pl.pallas_call(
        functools.partial(_combine_body, n_prompt_tiles=n_prompt_tiles, n_tiles=n_tiles),
        grid_spec=grid_spec,
        out_shape=[jax.ShapeDtypeStruct(xp.shape, F32), jax.ShapeDtypeStruct(xs.shape, F32)],
        compiler_params=_cparams(1), name="moe_combine")(table, info, xp, xs, esorted)


def _rope_tables(pos):
    half = C_HDIM // 2
    inv = 1.0 / (ROPE_THETA ** (jnp.arange(half, dtype=F32) / half))
    ang = pos.astype(F32)[:, None] * inv[None, :]
    cos = jnp.cos(ang)
    sin = jnp.sin(ang)
    reps = LANES // C_HDIM
    return jnp.tile(jnp.concatenate([cos, cos], axis=-1), (1, reps)), jnp.tile(jnp.concatenate([-sin, sin], axis=-1),
                                                                                 (1, reps))


def kernel(x_prompt, x_sample, mem_prompt, cache_mem_k, cache_mem_v, state_hgrn, cache_swa_k, cache_swa_v, norm_mix, norm_xattn, norm_ffn, even_w_in, even_w_out, gmlp_w_s, gmlp_b_s, gmlp_ln_g, gmlp_ln_b, hgrn_lb_logits, hgrn_out_norm, attn_w_in, attn_w_out, attn_q_norm, attn_k_norm, attn_sinks, xattn_mem_norm, xattn_w_q, xattn_w_k, xattn_w_v, xattn_w_o, xattn_q_norm, xattn_k_norm, ffn_w1, ffn_w3, ffn_w2, moe_router, moe_w1, moe_w3, moe_w2):
    n_batch, seq, d = x_prompt.shape
    dec_batch, dec_seq, _ = x_sample.shape
    n_mem = mem_prompt.shape[1]
    depth = norm_mix.shape[0]
    past_len = PAST_LEN
    assert d == D_MODEL and depth == 2 and seq % ROW_TILE == 0 and dec_batch * dec_seq == ROW_TILE
    assert dec_seq == CHUNK and ROW_TILE % n_mem == 0 and cache_swa_k.shape[2] == WINDOW
    assert even_w_in.shape[-1] == EVEN_IN and attn_w_in.shape[-1] == ODD_IN and ffn_w1.shape[-1] == D_FF
    assert moe_w1.shape[1] == N_EXPERTS
    n_prompt_rows = n_batch * seq
    n_rows = n_prompt_rows + dec_batch * dec_seq
    n_prompt_tiles = n_prompt_rows // ROW_TILE
    tiles_per_batch = seq // ROW_TILE
    row = lambda g: g.reshape(1, -1).astype(F32)

    lb_all = jnp.cumsum(jax.nn.softmax(hgrn_lb_logits.astype(F32), axis=0), axis=0)
    wcum = jnp.asarray(_cumsum_matrix(), BF16)
    lmask = jnp.asarray(_level_masks(), F32)

    def even_consts(n):
        tril = jnp.tril(jnp.ones((n, n), bool))
        ws = jnp.where(tril[None], gmlp_w_s[0, :, :n, :n], 0.0).astype(BF16)
        bs = jnp.broadcast_to(gmlp_b_s[0, :, :n, None], (A_GROUPS, n, LANES)).astype(F32)
        return [row(norm_mix[0]), even_w_in[0], even_w_out[0], ws, bs,
                gmlp_ln_g[0].reshape(A_GROUPS, 1, A_GDIM), gmlp_ln_b[0].reshape(A_GROUPS, 1, A_GDIM),
                row(lb_all[0]), row(hgrn_out_norm[0]), wcum, lmask]

    rows2d = lambda w: w.reshape(-1, w.shape[-1])
    xp, hgrn_p, w1b = _even_mixer(
        x_prompt.reshape(n_prompt_rows, d), jnp.zeros((n_batch, B_HEADS, B_HDIM, B_HDIM), F32),
        even_consts(A_CHUNK), n_batch=n_batch, tiles_per_batch=tiles_per_batch, n_seq=1, seq_rows=ROW_TILE,
        gchunk=A_CHUNK, emit_v=False, side_cast=rows2d(moe_w1[0]))
    xs, hgrn_s, gmlp_v = _even_mixer(
        x_sample.reshape(ROW_TILE, d), state_hgrn[0], even_consts(min(A_CHUNK, dec_seq)), n_batch=1,
        tiles_per_batch=1, n_seq=dec_batch, seq_rows=dec_seq, gchunk=min(A_CHUNK, dec_seq), emit_v=True)

    mem_k, mem_v = _memory_kv(mem_prompt.reshape(n_batch * n_mem, d), xattn_mem_norm.reshape(depth, 1, d),
                              xattn_w_k, xattn_w_v, xattn_k_norm.reshape(depth, 1, X_HDIM),
                              n_batch=n_batch, n_mem=n_mem)

    def cross_attention(xp, xs, l, side_casts=()):
        consts = (row(norm_xattn[l]), xattn_w_q[l], xattn_w_o[l], row(xattn_q_norm[l]))
        xp = _xattn(xp, mem_k, mem_v, *consts, layer=l, steps_per_mem=tiles_per_batch, n_seq=1, seq_rows=ROW_TILE,
                    side_casts=side_casts)
        xs = _xattn(xs, cache_k, cache_v, *consts, layer=l, steps_per_mem=1, n_seq=dec_batch, seq_rows=dec_seq)
        return xp, xs

    cache_k = cache_mem_k.reshape(depth, dec_batch, n_mem * X_HEADS, X_HDIM)
    cache_v = cache_mem_v.reshape(depth, dec_batch, n_mem * X_HEADS, X_HDIM)

    (xp, ffn_w1b, ffn_w3b, ffn_w2b), xs = cross_attention(xp, xs, 0, (ffn_w1[0], ffn_w3[0], ffn_w2[0]))
    xp, xs, w2b, attn_w_in_b, attn_w_out_b = _dense_ffn(xp, xs, row(norm_ffn[0]), ffn_w1b, ffn_w3b, ffn_w2b,
                                                        (rows2d(moe_w2[0]), attn_w_in[0], attn_w_out[0]))

    reps = D_MODEL // C_HDIM
    bd = jnp.asarray(np.kron(np.eye(reps, dtype=np.float32), np.full((C_HDIM, C_HDIM), 1.0 / C_HDIM, np.float32)),
                     BF16)
    swa_consts = [row(norm_mix[1]), attn_w_in_b, attn_w_out_b, row(jnp.tile(attn_q_norm[0], C_HEADS)),
                  row(jnp.tile(attn_k_norm[0], C_KV_HEADS)), bd]
    sinks = attn_sinks[0].astype(F32)
    cos_p, sin_p = _rope_tables(jnp.arange(seq, dtype=jnp.int32))
    cos_s, sin_s = _rope_tables(past_len + jnp.arange(dec_seq, dtype=jnp.int32))
    no_past = jnp.zeros((n_batch, WINDOW, C_KV_DIM), F32)
    xp, swk_p, swv_p, w3b = _swa(xp, no_past, no_past, cos_p, sin_p, sinks, swa_consts, n_batch=n_batch,
                                 tiles_per_batch=tiles_per_batch, n_seq=1, seq_rows=ROW_TILE, past_valid=False,
                                 side_cast=rows2d(moe_w3[0]))
    xs, swk_s, swv_s = _swa(xs, cache_swa_k[0].reshape(dec_batch, WINDOW, C_KV_DIM),
                            cache_swa_v[0].reshape(dec_batch, WINDOW, C_KV_DIM),
                            jnp.tile(cos_s, (dec_batch, 1)), jnp.tile(sin_s, (dec_batch, 1)), sinks, swa_consts,
                            n_batch=1, tiles_per_batch=1, n_seq=dec_batch, seq_rows=dec_seq, past_valid=True)
    xp, xs = cross_attention(xp, xs, 1)

    n_tiles = n_rows // ROW_TILE
    router_w = jnp.zeros((d, LANES), F32).at[:, :N_EXPERTS].set(moe_router[0].astype(F32))
    rhi = router_w.astype(BF16)
    rlo = (router_w - rhi.astype(F32)).astype(BF16)
    lstrict = jnp.asarray(np.tril(np.ones((ROW_TILE, ROW_TILE), np.float32), -1), BF16)
    su = jnp.asarray(np.triu(np.ones((LANES, LANES), np.float32), 1), BF16)
    sel = jnp.asarray(np.eye(GROUP, LANES, dtype=np.float32), BF16)
    nrm_ffn = row(norm_ffn[1])
    info, seg = _router(xp, xs, nrm_ffn, jnp.concatenate([rhi, rlo], axis=1), lstrict, su)
    seg = seg.reshape(n_tiles, GROUP, LANES)[:, 0, :N_EXPERTS].astype(jnp.int32)
    n_sorted_tiles = -(-(TOP_K * n_rows + n_tiles * N_EXPERTS * (GROUP - 1) + N_EXPERTS * (ROW_TILE - 1)) // ROW_TILE)
    table, tile_expert, tile_valid = _segment_table(seg, n_sorted_tiles)
    hsorted = _dispatch(table, info, xp, xs, nrm_ffn, sel, n_sorted_tiles)
    esorted = _grouped_ffn(hsorted, tile_expert, tile_valid, w1b.reshape(moe_w1.shape[1:]),
                           w3b.reshape(moe_w3.shape[1:]), w2b.reshape(moe_w2.shape[1:]))
    y_prompt, y_sample = _combine(table, info, xp, xs, esorted)

    n_even = state_hgrn.shape[0]
    n_odd = cache_swa_k.shape[0]
    return (y_prompt.reshape(n_batch, seq, d), y_sample.reshape(dec_batch, dec_seq, d),
            mem_k.reshape(depth, n_batch, n_mem, X_HEADS, X_HDIM), mem_v.reshape(depth, n_batch, n_mem, X_HEADS, X_HDIM),
            hgrn_p.reshape(n_even, n_batch, B_HEADS, B_HDIM, B_HDIM),
            gmlp_v.reshape(n_even, dec_batch, dec_seq, A_GROUPS, A_GDIM),
            hgrn_s.reshape(n_even, dec_batch, B_HEADS, B_HDIM, B_HDIM),
            swk_p.reshape(n_odd, n_batch, WINDOW, C_KV_HEADS, C_HDIM), swv_p.reshape(n_odd, n_batch, WINDOW, C_KV_HEADS, C_HDIM),
            swk_s.reshape(n_odd, dec_batch, dec_seq, C_KV_HEADS, C_HDIM), swv_s.reshape(n_odd, dec_batch, dec_seq, C_KV_HEADS, C_HDIM))
```

```python
import functools

import numpy as np
import jax
import jax.numpy as jnp
from jax import lax
from jax.experimental import pallas as pl
from jax.experimental.pallas import tpu as pltpu

F32 = jnp.float32
BF16 = jnp.bfloat16

D_MODEL = 1024
EPS = 1e-6
LOG2_E = 1.4426950408889634
CHUNK = 64
A_GROUPS = 4
A_DIM = D_MODEL // 2
A_GDIM = A_DIM // A_GROUPS
A_CHUNK = 128
B_HEADS = 4
B_DIM = D_MODEL // 2
B_HDIM = B_DIM // B_HEADS
EVEN_IN = 2 * A_DIM + 4 * B_DIM
C_HEADS = 16
C_KV_HEADS = 4
C_HDIM = D_MODEL // C_HEADS
C_GROUP = C_HEADS // C_KV_HEADS
C_KV_DIM = C_KV_HEADS * C_HDIM
WINDOW = 128
ROPE_THETA = 10000.0
PAST_LEN = 4096
ODD_IN = (C_HEADS + 2 * C_KV_HEADS) * C_HDIM
X_HEADS = 4
X_HDIM = 128
X_DIM = X_HEADS * X_HDIM
D_FF = 2816
N_EXPERTS = 8
TOP_K = 2

LANES = 128
ROW_TILE = 512
FF_SUB = 512
N_LEVELS = 6
HGRN_STAGE_CHUNKS = 2
SWA_STAGE_CHUNKS = 2
GROUP = 8
LOCAL_ROWS = -(-(TOP_K * ROW_TILE + N_EXPERTS * (GROUP - 1)) // LANES) * LANES
VMEM_LIMIT = 58 * 1024 * 1024


def _cparams(n_axes):
    return pltpu.CompilerParams(dimension_semantics=("arbitrary",) * n_axes, vmem_limit_bytes=VMEM_LIMIT)


def _dot(a, b):
    return jnp.dot(a, b, preferred_element_type=F32)


def _dot_nt(a, b):
    return lax.dot_general(a, b, (((1,), (1,)), ((), ())), preferred_element_type=F32)


def _rms(x, g):
    return x * lax.rsqrt(jnp.mean(x * x, axis=-1, keepdims=True) + EPS) * g


def _silu(x):
    return x * (1.0 / (1.0 + jnp.exp(-x)))


def _split3(x):
    hi = x.astype(BF16)
    r1 = x - hi.astype(F32)
    mid = r1.astype(BF16)
    lo = (r1 - mid.astype(F32)).astype(BF16)
    return hi, mid, lo


def _cumsum_matrix():
    r = np.arange(CHUNK)
    s = np.arange(CHUNK)
    blocks = [(s[None, :] <= r[:, None])]
    for l in range(N_LEVELS):
        h = 1 << l
        ref = (r & ~(2 * h - 1)) + h - 1
        blocks.append(s[None, :] <= ref[:, None])
    w = np.concatenate(blocks, axis=0).astype(np.float32)
    return np.concatenate([w, w, w], axis=1)


def _level_masks():
    t = np.arange(CHUNK)[:, None]
    s = np.arange(CHUNK)[None, :]
    masks = []
    for l in range(N_LEVELS):
        masks.append(((t >> (l + 1)) == (s >> (l + 1))) & (((t >> l) & 1) == 1) & (((s >> l) & 1) == 0))
    masks.append(t == s)
    return np.stack(masks).astype(np.float32)


def _even_mixer_body(x_ref, s0_ref, nrm_ref, win32_ref, wout32_ref, ws_ref, bs_ref, lng_ref, lnb_ref, lb_ref, og_ref,
                     wcum_ref, lmask_ref, *rest, n_seq, seq_rows, gchunk, emit_v, n_side):
    rest = list(rest)
    side_srcs = [rest.pop(0) for _ in range(n_side)]
    y_ref, sout_ref = rest.pop(0), rest.pop(0)
    v_ref = rest.pop(0) if emit_v else None
    side_dsts = [rest.pop(0) for _ in range(n_side)]
    proj_scr, mixed_scr, st_scr, win_ref, wout_ref = rest[:5]
    rows = n_seq * seq_rows
    j = pl.program_id(1)
    _run_side_casts(pl.program_id(0) * pl.num_programs(1) + j, side_srcs, side_dsts, rest[5:])

    @pl.when((pl.program_id(0) == 0) & (j == 0))
    def _():
        win_ref[...] = win32_ref[...].astype(BF16)
        wout_ref[...] = wout32_ref[...].astype(BF16)

    @pl.when(j == 0)
    def _():
        for s in range(n_seq):
            for hd in range(B_HEADS):
                st_scr[s * B_HEADS + hd] = s0_ref[s, hd].T

    x = x_ref[...]
    h = _rms(x, nrm_ref[...]).astype(BF16)
    n_pieces = EVEN_IN // ROW_TILE
    for n in range(n_pieces):
        cs = slice(n * ROW_TILE, (n + 1) * ROW_TILE)
        proj_scr[:, cs] = _dot(h, win_ref[:, cs])

    groups = range(A_GROUPS)
    group_cols = lambda base, g: slice(base + g * A_GDIM, base + (g + 1) * A_GDIM)
    vgs = [jax.nn.gelu(proj_scr[:, group_cols(A_DIM, g)]) for g in groups]
    means = [jnp.mean(vg, axis=-1, keepdims=True) for vg in vgs]
    vcs = [vg - mu for vg, mu in zip(vgs, means)]
    variances = [jnp.mean(vc * vc, axis=-1, keepdims=True) for vc in vcs]
    for g in groups:
        gs = group_cols(0, g)
        vn = vcs[g] * lax.rsqrt(variances[g] + EPS) * lng_ref[g] + lnb_ref[g]
        if v_ref is not None:
            v_ref[:, gs] = vn
        ug = jax.nn.gelu(proj_scr[:, gs])
        vb = vn.astype(BF16)
        for c in range(rows // gchunk):
            rs = slice(c * gchunk, (c + 1) * gchunk)
            sp = _dot(ws_ref[g], vb[rs]) + bs_ref[g]
            mixed_scr[rs, gs] = ug[rs] * sp

    q0, f0, i0, g0 = (2 * A_DIM + k * B_DIM for k in range(4))
    lb = lb_ref[...]
    fg = lb + (1.0 - lb) * jax.nn.sigmoid(proj_scr[:, f0:f0 + B_DIM])
    proj_scr[:, 0:B_DIM] = jnp.log(fg) * LOG2_E
    proj_scr[:, B_DIM:2 * B_DIM] = 1.0 - fg
    proj_scr[:, q0:q0 + B_DIM] = _silu(proj_scr[:, q0:q0 + B_DIM])
    og = og_ref[...]
    chunks_per_seq = seq_rows // CHUNK

    heads = range(B_HEADS)
    head_cols = lambda base, hd: slice(base + hd * B_HDIM, base + (hd + 1) * B_HDIM)
    chunk_rows = lambda c: slice(c * CHUNK, (c + 1) * CHUNK)

    def chunk_local(c):
        rs = chunk_rows(c)
        hi, mid, lo = _split3(proj_scr[rs, 0:B_DIM])
        gg = _dot(wcum_ref[...], jnp.concatenate([hi, mid, lo], axis=0))
        vs = [proj_scr[rs, head_cols(i0, hd)] for hd in heads]
        vts = [v.T.astype(BF16) for v in vs]
        Gs = [gg[0:CHUNK, head_cols(0, hd)] for hd in heads]
        qs = [proj_scr[rs, head_cols(q0, hd)] for hd in heads]
        ks = [proj_scr[rs, head_cols(B_DIM, hd)] for hd in heads]
        operands = []
        for hd in heads:
            for l in range(N_LEVELS + 1):
                if l < N_LEVELS:
                    e = jnp.exp2(-jnp.abs(Gs[hd] - gg[(l + 1) * CHUNK:(l + 2) * CHUNK, head_cols(0, hd)]))
                    operands.append(((qs[hd] * e).astype(BF16), (ks[hd] * e).astype(BF16)))
                else:
                    operands.append((qs[hd].astype(BF16), ks[hd].astype(BF16)))
        blocks = [_dot_nt(qe, ke) for qe, ke in operands]
        parts = []
        for hd in heads:
            att = jnp.zeros((CHUNK, CHUNK), F32)
            for l in range(N_LEVELS + 1):
                att = jnp.where(lmask_ref[l] > 0.5, blocks[hd * (N_LEVELS + 1) + l], att)
            g_end = Gs[hd][CHUNK - 1:CHUNK, :]
            kd = (ks[hd] * jnp.exp2(g_end - Gs[hd])).astype(BF16)
            parts.append((_dot(att.astype(BF16), vs[hd].astype(BF16)), (qs[hd] * jnp.exp2(Gs[hd])).astype(BF16),
                          jnp.exp2(g_end), _dot(vts[hd], kd)))
        return parts

    def chunk_state(c, parts):
        rs = chunk_rows(c)
        sidx = (c // chunks_per_seq) * B_HEADS
        outs = []
        for hd in heads:
            o_local, q_decayed, decay, increment = parts[hd]
            st = st_scr[sidx + hd]
            outs.append(o_local + _dot_nt(q_decayed, st.astype(BF16)))
            st_scr[sidx + hd] = st * decay + increment
        for hd in heads:
            o = outs[hd]
            on = o * lax.rsqrt(jnp.mean(o * o, axis=-1, keepdims=True) + EPS) * og
            gate = _silu(proj_scr[rs, head_cols(g0, hd)])
            mixed_scr[rs, head_cols(A_DIM, hd)] = on * gate

    n_chunks = rows // CHUNK
    for first in range(0, n_chunks, HGRN_STAGE_CHUNKS):
        group = range(first, min(first + HGRN_STAGE_CHUNKS, n_chunks))
        local = [chunk_local(c) for c in group]
        for c, parts in zip(group, local):
            chunk_state(c, parts)

    y_ref[...] = x + _dot(mixed_scr[...].astype(BF16), wout_ref[...])

    @pl.when(j == pl.num_programs(1) - 1)
    def _():
        for s in range(n_seq):
            for hd in range(B_HEADS):
                sout_ref[s, hd] = st_scr[s * B_HEADS + hd].T


def _even_mixer(x, s0, consts, *, n_batch, tiles_per_batch, n_seq, seq_rows, gchunk, emit_v, side_casts=()):
    rows = n_seq * seq_rows
    grid = (n_batch, tiles_per_batch)
    tile = lambda b, j: (b * tiles_per_batch + j, 0)
    whole = lambda *shape: pl.BlockSpec(shape, lambda b, j: (0,) * len(shape))
    once = lambda *shape: pl.BlockSpec(shape, lambda b, j: (0,) * len(shape), pipeline_mode=pl.Buffered(1))
    in_specs = [
        pl.BlockSpec((rows, D_MODEL), tile),
        pl.BlockSpec((n_seq, B_HEADS, B_HDIM, B_HDIM), lambda b, j: (b, 0, 0, 0)),
        whole(1, D_MODEL), once(D_MODEL, EVEN_IN), once(D_MODEL, D_MODEL),
        whole(A_GROUPS, gchunk, gchunk), whole(A_GROUPS, gchunk, LANES),
        whole(A_GROUPS, 1, A_GDIM), whole(A_GROUPS, 1, A_GDIM), whole(1, B_DIM), whole(1, B_HDIM),
        whole((N_LEVELS + 1) * CHUNK, 3 * CHUNK), whole(N_LEVELS + 1, CHUNK, CHUNK),
    ]
    args = [x, s0] + list(consts)
    out_shape = [jax.ShapeDtypeStruct(x.shape, F32),
                 jax.ShapeDtypeStruct((n_batch * n_seq, B_HEADS, B_HDIM, B_HDIM), F32)]
    out_specs = [pl.BlockSpec((rows, D_MODEL), tile),
                 pl.BlockSpec((n_seq, B_HEADS, B_HDIM, B_HDIM), lambda b, j: (b, 0, 0, 0))]
    if emit_v:
        out_shape.append(jax.ShapeDtypeStruct((x.shape[0], A_DIM), F32))
        out_specs.append(pl.BlockSpec((rows, A_DIM), tile))
    scratch = [pltpu.VMEM((rows, EVEN_IN), F32), pltpu.VMEM((rows, D_MODEL), F32),
               pltpu.VMEM((n_seq * B_HEADS, B_HDIM, B_HDIM), F32),
               pltpu.VMEM((D_MODEL, EVEN_IN), BF16), pltpu.VMEM((D_MODEL, D_MODEL), BF16)]
    side_in, side_out, side_shapes, side_scratch = _side_cast_plumbing(side_casts, n_batch * tiles_per_batch)
    in_specs += side_in
    args += list(side_casts)
    out_specs += side_out
    out_shape += side_shapes
    scratch += side_scratch
    body = functools.partial(_even_mixer_body, n_seq=n_seq, seq_rows=seq_rows, gchunk=gchunk, emit_v=emit_v,
                             n_side=len(side_casts))
    return pl.pallas_call(
        body, grid=grid, in_specs=in_specs, out_specs=out_specs, out_shape=out_shape, scratch_shapes=scratch,
        compiler_params=_cparams(2),
        name="even_mixer_s" if emit_v else "even_mixer_p")(*args)


def _memory_kv_body(mem_ref, gm_ref, wk_ref, wv_ref, kg_ref, mk_ref, mv_ref, *, n_seq, n_mem):
    m = _rms(mem_ref[...], gm_ref[0]).astype(BF16)
    kk = _dot(m, wk_ref[0].astype(BF16))
    vv = _dot(m, wv_ref[0].astype(BF16))
    kg = kg_ref[0]
    for hd in range(X_HEADS):
        hs = slice(hd * X_HDIM, (hd + 1) * X_HDIM)
        kh = _rms(kk[:, hs], kg)
        for s in range(n_seq):
            mk_ref[0, s, pl.ds(hd, n_mem, stride=X_HEADS), :] = kh[s * n_mem:(s + 1) * n_mem]
            mv_ref[0, s, pl.ds(hd, n_mem, stride=X_HEADS), :] = vv[s * n_mem:(s + 1) * n_mem, hs]


def _memory_kv(mem2d, g_mem, w_k, w_v, k_g, *, n_batch, n_mem):
    depth = w_k.shape[0]
    n_seq = ROW_TILE // n_mem
    out = jax.ShapeDtypeStruct((depth, n_batch, n_mem * X_HEADS, X_HDIM), F32)
    ospec = pl.BlockSpec((1, n_seq, n_mem * X_HEADS, X_HDIM), lambda l, t: (l, t, 0, 0))
    return pl.pallas_call(
        functools.partial(_memory_kv_body, n_seq=n_seq, n_mem=n_mem),
        grid=(depth, n_batch // n_seq),
        in_specs=[pl.BlockSpec((ROW_TILE, D_MODEL), lambda l, t: (t, 0)),
                  pl.BlockSpec((1, 1, D_MODEL), lambda l, t: (l, 0, 0)),
                  pl.BlockSpec((1, D_MODEL, X_DIM), lambda l, t: (l, 0, 0)),
                  pl.BlockSpec((1, D_MODEL, X_DIM), lambda l, t: (l, 0, 0)),
                  pl.BlockSpec((1, 1, X_HDIM), lambda l, t: (l, 0, 0))],
        out_specs=[ospec, ospec], out_shape=[out, out], compiler_params=_cparams(2),
        name="memory_kv")(mem2d, g_mem, w_k, w_v, k_g)


def _xattn_body(x_ref, mk_ref, mv_ref, nrm_ref, wq32_ref, wo32_ref, qg_ref, *rest, n_seq, seq_rows, n_side):
    side_srcs, y_ref, side_dsts = rest[:n_side], rest[n_side], rest[n_side + 1:2 * n_side + 1]
    o_scr, wq_ref, wo_ref = rest[2 * n_side + 1:2 * n_side + 4]
    _run_side_casts(pl.program_id(0), side_srcs, side_dsts, rest[2 * n_side + 4:])

    @pl.when(pl.program_id(0) == 0)
    def _():
        wq_ref[...] = wq32_ref[...].astype(BF16)
        wo_ref[...] = wo32_ref[...].astype(BF16)

    x = x_ref[...]
    h = _rms(x, nrm_ref[...]).astype(BF16)
    q = _dot(h, wq_ref[...])
    qg = qg_ref[...] * (X_HDIM ** -0.5)
    units = [(s, hd) for s in range(n_seq) for hd in range(X_HEADS)]
    rows_of = lambda s: slice(s * seq_rows, (s + 1) * seq_rows)
    lanes_of = lambda hd: slice(hd * X_HDIM, (hd + 1) * X_HDIM)
    n_mem = mk_ref.shape[1] // X_HEADS
    head_rows = lambda hd: pl.ds(hd, n_mem, stride=X_HEADS)
    qh = [_rms(q[rows_of(s), lanes_of(hd)], qg).astype(BF16) for s, hd in units]
    sc = [_dot_nt(qh[n], mk_ref[s, head_rows(hd), :].astype(BF16)) for n, (s, hd) in enumerate(units)]
    m = [jnp.max(t, axis=-1, keepdims=True) for t in sc]
    p = [jnp.exp(t - mx) for t, mx in zip(sc, m)]
    den = [jnp.sum(t, axis=-1, keepdims=True) for t in p]
    for n, (s, hd) in enumerate(units):
        o_scr[rows_of(s), lanes_of(hd)] = _dot(p[n].astype(BF16), mv_ref[s, head_rows(hd), :].astype(BF16)) / den[n]
    y_ref[...] = x + _dot(o_scr[...].astype(BF16), wo_ref[...])


def _xattn(x, mk, mv, nrm, wq, wo, qg, *, layer, steps_per_mem, n_seq, seq_rows, side_casts=()):
    rows = n_seq * seq_rows
    n_steps = x.shape[0] // rows
    mem_spec = pl.BlockSpec((None, n_seq) + mk.shape[2:], lambda t: (layer, t // steps_per_mem, 0, 0))
    whole = lambda *shape: pl.BlockSpec(shape, lambda t: (0,) * len(shape))
    once = lambda *shape: pl.BlockSpec(shape, lambda t: (0,) * len(shape), pipeline_mode=pl.Buffered(1))
    side_in, side_out, side_shapes, side_scratch = _side_cast_plumbing(side_casts, n_steps)
    out = pl.pallas_call(
        functools.partial(_xattn_body, n_seq=n_seq, seq_rows=seq_rows, n_side=len(side_casts)),
        grid=(n_steps,),
        in_specs=[pl.BlockSpec((rows, D_MODEL), lambda t: (t, 0)), mem_spec, mem_spec,
                  whole(1, D_MODEL), once(D_MODEL, X_DIM), once(X_DIM, D_MODEL), whole(1, X_HDIM)] + side_in,
        out_specs=[pl.BlockSpec((rows, D_MODEL), lambda t: (t, 0))] + side_out,
        out_shape=[jax.ShapeDtypeStruct(x.shape, F32)] + side_shapes,
        scratch_shapes=[pltpu.VMEM((rows, X_DIM), F32), pltpu.VMEM((D_MODEL, X_DIM), BF16),
                        pltpu.VMEM((X_DIM, D_MODEL), BF16)] + side_scratch,
        compiler_params=_cparams(1),
        name="xattn_s" if n_seq > 1 else "xattn_p")(x, mk, mv, nrm, wq, wo, qg, *side_casts)
    return out if side_casts else out[0]


def _two_source_specs(n_prompt_tiles, width=D_MODEL):
    return (pl.BlockSpec((ROW_TILE, width), lambda i, *_: (jnp.minimum(i, n_prompt_tiles - 1), 0)),
            pl.BlockSpec((ROW_TILE, width), lambda i, *_: (jnp.maximum(i - n_prompt_tiles, 0), 0)))


def _on_row_source(i, n_prompt_tiles, fn, prompt_refs, sample_refs):
    pl.when(i < n_prompt_tiles)(lambda: fn(*prompt_refs))
    pl.when(i >= n_prompt_tiles)(lambda: fn(*sample_refs))


def _swiglu_part(hb, w1, w3, w2):
    part = None
    for c0 in range(0, D_FF, FF_SUB):
        cs = slice(c0, min(c0 + FF_SUB, D_FF))
        act = (_silu(_dot(hb, w1[:, cs])) * _dot(hb, w3[:, cs])).astype(BF16)
        p = _dot(act, w2[cs, :])
        part = p if part is None else part + p
    return part


def _dense_ffn_body(xp_ref, xs_ref, nrm_ref, w1_ref, w3_ref, w2_ref, *rest, n_prompt_tiles, n_side):
    side_srcs, (yp_ref, ys_ref), side_dsts = rest[:n_side], rest[n_side:n_side + 2], rest[n_side + 2:2 * n_side + 2]
    i = pl.program_id(0)
    _run_side_casts(i, side_srcs, side_dsts, rest[2 * n_side + 2:])

    def run(x_ref, y_ref):
        x = x_ref[...]
        y_ref[...] = x + _swiglu_part(_rms(x, nrm_ref[...]).astype(BF16), w1_ref, w3_ref, w2_ref)

    _on_row_source(i, n_prompt_tiles, run, (xp_ref, yp_ref), (xs_ref, ys_ref))


def _dense_ffn(xp, xs, nrm, w1, w3, w2, side_casts):
    n_prompt_tiles = xp.shape[0] // ROW_TILE
    n_tiles = n_prompt_tiles + xs.shape[0] // ROW_TILE
    once = lambda *shape: pl.BlockSpec(shape, lambda i: (0,) * len(shape), pipeline_mode=pl.Buffered(1))
    side_in, side_out, side_shapes, side_scratch = _side_cast_plumbing(side_casts, n_tiles)
    return pl.pallas_call(
        functools.partial(_dense_ffn_body, n_prompt_tiles=n_prompt_tiles, n_side=len(side_casts)), grid=(n_tiles,),
        in_specs=[*_two_source_specs(n_prompt_tiles), pl.BlockSpec((1, D_MODEL), lambda i: (0, 0)),
                  once(D_MODEL, D_FF), once(D_MODEL, D_FF), once(D_FF, D_MODEL)] + side_in,
        out_specs=[*_two_source_specs(n_prompt_tiles)] + side_out,
        out_shape=[jax.ShapeDtypeStruct(xp.shape, F32), jax.ShapeDtypeStruct(xs.shape, F32)] + side_shapes,
        scratch_shapes=side_scratch,
        compiler_params=_cparams(1), name="dense_ffn")(xp, xs, nrm, w1, w3, w2, *side_casts)


def _grouped_ffn_body(te_ref, tv_ref, x_ref, w1_ref, w3_ref, w2_ref, y_ref):
    valid = tv_ref[pl.program_id(0)]
    half = ROW_TILE // 2
    weights = (w1_ref.at[0], w3_ref.at[0], w2_ref.at[0])

    @pl.when(valid > half)
    def _():
        y_ref[...] = _swiglu_part(x_ref[...].astype(BF16), *weights)

    @pl.when((valid > 0) & (valid <= half))
    def _():
        y_ref[0:half] = _swiglu_part(x_ref[0:half].astype(BF16), *weights)
        y_ref[half:ROW_TILE] = jnp.zeros((ROW_TILE - half, D_MODEL), F32)

    @pl.when(valid == 0)
    def _():
        y_ref[...] = jnp.zeros_like(y_ref)


def _grouped_ffn(xsorted, tile_expert, tile_valid, w1, w3, w2):
    n_tiles = xsorted.shape[0] // ROW_TILE
    grid_spec = pltpu.PrefetchScalarGridSpec(
        num_scalar_prefetch=2, grid=(n_tiles,),
        in_specs=[pl.BlockSpec((ROW_TILE, D_MODEL), lambda i, te, tv: (i, 0)),
                  pl.BlockSpec((1, D_MODEL, D_FF), lambda i, te, tv: (te[i], 0, 0)),
                  pl.BlockSpec((1, D_MODEL, D_FF), lambda i, te, tv: (te[i], 0, 0)),
                  pl.BlockSpec((1, D_FF, D_MODEL), lambda i, te, tv: (te[i], 0, 0))],
        out_specs=pl.BlockSpec((ROW_TILE, D_MODEL), lambda i, te, tv: (i, 0)))
    return pl.pallas_call(
        _grouped_ffn_body, grid_spec=grid_spec, out_shape=jax.ShapeDtypeStruct(xsorted.shape, F32),
        compiler_params=_cparams(1), name="grouped_ffn")(tile_expert, tile_valid, xsorted, w1, w3, w2)


def _rope_slab(xs, cos, sin_signed, first_half):
    rot = jnp.where(first_half, pltpu.roll(xs, LANES - C_HDIM // 2, 1), pltpu.roll(xs, C_HDIM // 2, 1))
    return xs * cos + rot * sin_signed


def _swa_body(sink_ref, x_ref, pk_ref, pv_ref, cos_ref, sin_ref, nrm_ref, win32_ref, wout32_ref, qg_ref, kg_ref,
              bd_ref, *rest, n_seq, seq_rows, past_valid, side_cast):
    rest = list(rest)
    side_src = rest.pop(0) if side_cast else None
    y_ref, ko_ref, vo_ref = rest.pop(0), rest.pop(0), rest.pop(0)
    side_dst = rest.pop(0) if side_cast else None
    k_scr, v_scr, q_scr, a_scr, win_ref, wout_ref = rest[:6]
    rows = n_seq * seq_rows
    j = pl.program_id(1)
    tail = min(WINDOW, seq_rows)
    if side_cast:
        _cast_rows_step(pl.program_id(0) * pl.num_programs(1) + j, side_src, side_dst, *rest[6:])

    @pl.when((pl.program_id(0) == 0) & (j == 0))
    def _():
        win_ref[...] = win32_ref[...].astype(BF16)
        wout_ref[...] = wout32_ref[...].astype(BF16)

    if past_valid:
        for s in range(n_seq):
            k_scr[s, 0:WINDOW] = pk_ref[s]
            v_scr[s, 0:WINDOW] = pv_ref[s]
    else:
        @pl.when(j == 0)
        def _():
            for s in range(n_seq):
                k_scr[s, 0:WINDOW] = jnp.zeros((WINDOW, C_KV_DIM), F32)
                v_scr[s, 0:WINDOW] = jnp.zeros((WINDOW, C_KV_DIM), F32)

        @pl.when(j > 0)
        def _():
            for s in range(n_seq):
                k_scr[s, 0:WINDOW] = k_scr[s, seq_rows:seq_rows + WINDOW]
                v_scr[s, 0:WINDOW] = v_scr[s, seq_rows:seq_rows + WINDOW]

    x = x_ref[...]
    h = _rms(x, nrm_ref[...]).astype(BF16)
    q_dim = C_HEADS * C_HDIM
    q = _dot(h, win_ref[:, 0:q_dim])
    k = _dot(h, win_ref[:, q_dim:q_dim + C_KV_DIM])
    v = _dot(h, win_ref[:, q_dim + C_KV_DIM:q_dim + 2 * C_KV_DIM])

    cos = cos_ref[...]
    sin_signed = sin_ref[...]
    first_half = (lax.broadcasted_iota(jnp.int32, (rows, LANES), 1) % C_HDIM) < (C_HDIM // 2)
    bd = bd_ref[...]
    qn = q * lax.rsqrt(_dot((q * q).astype(BF16), bd) + EPS) * qg_ref[...]
    kn = k * lax.rsqrt(_dot((k * k).astype(BF16), bd[0:C_KV_DIM, 0:C_KV_DIM]) + EPS) * kg_ref[...]
    scale = C_HDIM ** -0.5
    for sl in range(q_dim // LANES):
        ls = slice(sl * LANES, (sl + 1) * LANES)
        q_scr[:, ls] = _rope_slab(qn[:, ls], cos, sin_signed, first_half) * scale
    for sl in range(C_KV_DIM // LANES):
        ls = slice(sl * LANES, (sl + 1) * LANES)
        kr = _rope_slab(kn[:, ls], cos, sin_signed, first_half)
        for s in range(n_seq):
            k_scr[s, WINDOW:WINDOW + seq_rows, ls] = kr[s * seq_rows:(s + 1) * seq_rows]
    for s in range(n_seq):
        v_scr[s, WINDOW:WINDOW + seq_rows] = v[s * seq_rows:(s + 1) * seq_rows]
        ko_ref[s] = k_scr[s, WINDOW + seq_rows - tail:WINDOW + seq_rows]
        vo_ref[s] = v_scr[s, WINDOW + seq_rows - tail:WINDOW + seq_rows]

    chunks_per_seq = seq_rows // CHUNK
    n_keys = WINDOW + CHUNK
    key_idx = lax.broadcasted_iota(jnp.int32, (C_GROUP * CHUNK, n_keys), 1)
    row_grp = lax.broadcasted_iota(jnp.int32, (C_GROUP * CHUNK, 1), 0) // CHUNK
    ones_keys = jnp.ones((n_keys, C_HDIM), BF16)

    head_sinks = []
    for kvh in range(C_KV_HEADS):
        sink = jnp.zeros((C_GROUP * CHUNK, 1), F32)
        for g in range(C_GROUP):
            sink = jnp.where(row_grp == g, sink_ref[kvh * C_GROUP + g], sink)
        head_sinks.append(sink)

    def attend(chunk_ids):
        units = [(idx, kvh) for idx in chunk_ids for kvh in range(C_KV_HEADS)]
        head_lanes = lambda kvh: slice(kvh * C_HDIM, (kvh + 1) * C_HDIM)
        q_rows = lambda idx: slice(idx * CHUNK, (idx + 1) * CHUNK)
        key_rows = lambda idx: slice((idx % chunks_per_seq) * CHUNK, (idx % chunks_per_seq) * CHUNK + n_keys)
        qs = [jnp.concatenate(
            [q_scr[q_rows(idx), (kvh * C_GROUP + g) * C_HDIM:(kvh * C_GROUP + g + 1) * C_HDIM]
             for g in range(C_GROUP)], axis=0).astype(BF16) for idx, kvh in units]
        kh = [k_scr[idx // chunks_per_seq, key_rows(idx), head_lanes(kvh)].astype(BF16) for idx, kvh in units]
        vh = [v_scr[idx // chunks_per_seq, key_rows(idx), head_lanes(kvh)].astype(BF16) for idx, kvh in units]
        sc = [_dot_nt(qn, kn) for qn, kn in zip(qs, kh)]
        if not past_valid:
            sc = [jnp.where((key_idx + (idx % chunks_per_seq) * CHUNK >= WINDOW) | (j > 0), sn, -jnp.inf)
                  for sn, (idx, kvh) in zip(sc, units)]
        m = [jnp.maximum(jnp.max(sn, axis=-1, keepdims=True), head_sinks[kvh]) for sn, (idx, kvh) in zip(sc, units)]
        p = [jnp.exp(sn - mn).astype(BF16) for sn, mn in zip(sc, m)]
        den = [_dot(pn, ones_keys) + jnp.exp(head_sinks[kvh] - mn) for pn, mn, (idx, kvh) in zip(p, m, units)]
        o = [_dot(pn, vn) / dn for pn, vn, dn in zip(p, vh, den)]
        for on, (idx, kvh) in zip(o, units):
            for g in range(C_GROUP):
                hs = slice((kvh * C_GROUP + g) * C_HDIM, (kvh * C_GROUP + g + 1) * C_HDIM)
                a_scr[q_rows(idx), hs] = on[g * CHUNK:(g + 1) * CHUNK]

    n_chunks = rows // CHUNK
    for first in range(0, n_chunks, SWA_STAGE_CHUNKS):
        attend(range(first, min(first + SWA_STAGE_CHUNKS, n_chunks)))
    y_ref[...] = x + _dot(a_scr[...].astype(BF16), wout_ref[...])


def _swa(x, pk, pv, cos, sin_signed, sinks, consts, *, n_batch, tiles_per_batch, n_seq, seq_rows, past_valid,
         side_cast=None):
    rows = n_seq * seq_rows
    tail = min(WINDOW, seq_rows)
    n_str = n_batch * n_seq
    whole = lambda *shape: pl.BlockSpec(shape, lambda b, j, sk: (0,) * len(shape))
    once = lambda *shape: pl.BlockSpec(shape, lambda b, j, sk: (0,) * len(shape), pipeline_mode=pl.Buffered(1))
    in_specs = [pl.BlockSpec((rows, D_MODEL), lambda b, j, sk: (b * tiles_per_batch + j, 0)),
                pl.BlockSpec((n_seq, WINDOW, C_KV_DIM), lambda b, j, sk: (b, 0, 0)),
                pl.BlockSpec((n_seq, WINDOW, C_KV_DIM), lambda b, j, sk: (b, 0, 0)),
                pl.BlockSpec((rows, LANES), lambda b, j, sk: (j, 0)),
                pl.BlockSpec((rows, LANES), lambda b, j, sk: (j, 0)),
                whole(1, D_MODEL), once(D_MODEL, ODD_IN), once(D_MODEL, D_MODEL),
                whole(1, D_MODEL), whole(1, C_KV_DIM), whole(D_MODEL, D_MODEL)]
    args = [x, pk, pv, cos, sin_signed] + list(consts)
    kv_out = jax.ShapeDtypeStruct((n_str, tail, C_KV_DIM), F32)
    kv_spec = pl.BlockSpec((n_seq, tail, C_KV_DIM), lambda b, j, sk: (b, 0, 0))
    out_specs = [pl.BlockSpec((rows, D_MODEL), lambda b, j, sk: (b * tiles_per_batch + j, 0)), kv_spec, kv_spec]
    out_shape = [jax.ShapeDtypeStruct(x.shape, F32), kv_out, kv_out]
    scratch = [pltpu.VMEM((n_seq, WINDOW + seq_rows, C_KV_DIM), F32),
               pltpu.VMEM((n_seq, WINDOW + seq_rows, C_KV_DIM), F32),
               pltpu.VMEM((rows, D_MODEL), F32), pltpu.VMEM((rows, D_MODEL), F32),
               pltpu.VMEM((D_MODEL, ODD_IN), BF16), pltpu.VMEM((D_MODEL, D_MODEL), BF16)]
    if side_cast is not None:
        in_specs.append(pl.BlockSpec(memory_space=pl.ANY))
        args.append(side_cast)
        out_specs.append(pl.BlockSpec(memory_space=pl.ANY))
        out_shape.append(jax.ShapeDtypeStruct(side_cast.shape, BF16))
        scratch += _cast_scratch(side_cast, n_batch * tiles_per_batch)
    grid_spec = pltpu.PrefetchScalarGridSpec(
        num_scalar_prefetch=1, grid=(n_batch, tiles_per_batch), in_specs=in_specs, out_specs=out_specs,
        scratch_shapes=scratch)
    return pl.pallas_call(
        functools.partial(_swa_body, n_seq=n_seq, seq_rows=seq_rows, past_valid=past_valid,
                          side_cast=side_cast is not None),
        grid_spec=grid_spec, out_shape=out_shape,
        compiler_params=_cparams(2),
        name="swa_s" if past_valid else "swa_p")(sinks, *args)


BF16_ROWS = 16


def _cast_piece_rows(w, n_grid_steps):
    n_cast = 1 << (n_grid_steps.bit_length() - 1)
    while w.shape[0] % (n_cast * BF16_ROWS):
        n_cast //= 2
    return w.shape[0] // n_cast


def _cast_rows_step(step, src_hbm, dst_hbm, in_stage, out_stage, sems):
    n_rows = in_stage.shape[1]
    n_cast = src_hbm.shape[0] // n_rows
    slot = step % 2

    def read(s, sl):
        return pltpu.make_async_copy(src_hbm.at[pl.ds(pl.multiple_of(s * n_rows, 16), n_rows)], in_stage.at[sl],
                                     sems.at[sl])

    def write(s, sl):
        return pltpu.make_async_copy(out_stage.at[sl], dst_hbm.at[pl.ds(pl.multiple_of(s * n_rows, 16), n_rows)],
                                     sems.at[2 + sl])

    @pl.when(step == 0)
    def _():
        read(0, 0).start()

    @pl.when(step < n_cast)
    def _():
        @pl.when(step + 1 < n_cast)
        def _():
            read(step + 1, 1 - slot).start()

        read(step, slot).wait()

        @pl.when(step >= 2)
        def _():
            write(step - 2, slot).wait()

        out_stage[slot] = in_stage[slot].astype(BF16)
        write(step, slot).start()

        @pl.when(step == n_cast - 1)
        def _():
            write(step, slot).wait()
            if n_cast > 1:
                write(step - 1, 1 - slot).wait()


def _cast_scratch(w, n_grid_steps):
    rows = _cast_piece_rows(w, n_grid_steps)
    return [pltpu.VMEM((2, rows, w.shape[1]), F32), pltpu.VMEM((2, rows, w.shape[1]), BF16),
            pltpu.SemaphoreType.DMA((4,))]


def _side_cast_plumbing(side_casts, n_grid_steps):
    any_space = pl.BlockSpec(memory_space=pl.ANY)
    scratch = [s for w in side_casts for s in _cast_scratch(w, n_grid_steps)]
    return ([any_space] * len(side_casts), [any_space] * len(side_casts),
            [jax.ShapeDtypeStruct(w.shape, BF16) for w in side_casts], scratch)


def _run_side_casts(step, srcs, dsts, scratch):
    for n, (src, dst) in enumerate(zip(srcs, dsts)):
        _cast_rows_step(step, src, dst, *scratch[3 * n:3 * n + 3])


def _router_body(xp_ref, xs_ref, nrm_ref, rboth_ref, lstrict_ref, su_ref, info_ref, cnt_ref, *, n_prompt_tiles):
    def run(x_ref):
        h = _rms(x_ref[...], nrm_ref[...])
        hi = h.astype(BF16)
        lo = (h - hi.astype(F32)).astype(BF16)
        both = _dot(hi, rboth_ref[...])
        logits = both[:, 0:LANES] + both[:, LANES:2 * LANES] + _dot(lo, rboth_ref[:, 0:LANES])
        lane = lax.broadcasted_iota(jnp.int32, logits.shape, 1).astype(F32)
        logits = jnp.where(lane < N_EXPERTS, logits, -jnp.inf)
        l1 = jnp.max(logits, axis=-1, keepdims=True)
        i1 = jnp.min(jnp.where(logits == l1, lane, float(LANES)), axis=-1, keepdims=True)
        rest = jnp.where(lane == i1, -jnp.inf, logits)
        l2 = jnp.max(rest, axis=-1, keepdims=True)
        i2 = jnp.min(jnp.where(rest == l2, lane, float(LANES)), axis=-1, keepdims=True)
        e = jnp.exp(l2 - l1)
        g1 = 1.0 / (1.0 + e)
        g2 = e * g1
        oh1 = (lane == i1).astype(F32)
        oh2 = (lane == i2).astype(F32)
        cnt1 = jnp.sum(oh1, axis=0, keepdims=True)
        cnt2 = jnp.sum(oh2, axis=0, keepdims=True)
        seg = jnp.broadcast_to(jnp.ceil((cnt1 + cnt2) * (1.0 / GROUP)), (GROUP, LANES))
        off = _dot(seg.astype(BF16), su_ref[...])[0:1] * GROUP
        c1 = _dot(lstrict_ref[...], oh1.astype(BF16)) + off
        c2 = _dot(lstrict_ref[...], oh2.astype(BF16)) + off + cnt1
        d1 = jnp.sum(jnp.where(lane == i1, c1, 0.0), axis=-1, keepdims=True)
        d2 = jnp.sum(jnp.where(lane == i2, c2, 0.0), axis=-1, keepdims=True)
        info = jnp.zeros(logits.shape, F32)
        for n, col in enumerate([d1, d2, g1, g2]):
            info = jnp.where(lane == n, col, info)
        info_ref[...] = info
        cnt_ref[...] = seg * GROUP

    _on_row_source(pl.program_id(0), n_prompt_tiles, run, (xp_ref,), (xs_ref,))


def _router(xp, xs, nrm, rboth, lstrict, su):
    n_prompt_tiles = xp.shape[0] // ROW_TILE
    n_tiles = n_prompt_tiles + xs.shape[0] // ROW_TILE
    whole = lambda *shape: pl.BlockSpec(shape, lambda i: (0,) * len(shape))
    return pl.pallas_call(
        functools.partial(_router_body, n_prompt_tiles=n_prompt_tiles), grid=(n_tiles,),
        in_specs=[*_two_source_specs(n_prompt_tiles), whole(1, D_MODEL), whole(D_MODEL, 2 * LANES),
                  whole(ROW_TILE, ROW_TILE), whole(LANES, LANES)],
        out_specs=[pl.BlockSpec((ROW_TILE, LANES), lambda i: (i, 0)), pl.BlockSpec((GROUP, LANES), lambda i: (i, 0))],
        out_shape=[jax.ShapeDtypeStruct((n_tiles * ROW_TILE, LANES), F32),
                   jax.ShapeDtypeStruct((n_tiles * GROUP, LANES), F32)],
        compiler_params=_cparams(1), name="router")(xp, xs, nrm, rboth, lstrict, su)


BIG_ROWS = 8 * GROUP
SEG_FIELDS = 4
TILE_FIELDS = 2
TAIL_FIELDS = 3


def _rows_copy(src_ref, src_row, dst_ref, dst_row, n_rows, sem):
    return pltpu.make_async_copy(src_ref.at[pl.ds(pl.multiple_of(src_row, GROUP), n_rows)],
                                 dst_ref.at[pl.ds(pl.multiple_of(dst_row, GROUP), n_rows)], sem)


def _start_pieces(src_ref, src0, dst_ref, dst0, n_big, n_small, sem):
    def big(g, carry):
        _rows_copy(src_ref, src0 + g * BIG_ROWS, dst_ref, dst0 + g * BIG_ROWS, BIG_ROWS, sem).start()
        return carry

    lax.fori_loop(0, n_big, big, 0)
    done = n_big * BIG_ROWS

    def small(g, carry):
        _rows_copy(src_ref, src0 + done + g * GROUP, dst_ref, dst0 + done + g * GROUP, GROUP, sem).start()
        return carry

    lax.fori_loop(0, n_small, small, 0)


def _wait_pieces(n_big, n_small, src_ref, dst_ref, sem):
    def big(g, carry):
        _rows_copy(src_ref, 0, dst_ref, 0, BIG_ROWS, sem).wait()
        return carry

    lax.fori_loop(0, n_big, big, 0)

    def small(g, carry):
        _rows_copy(src_ref, 0, dst_ref, 0, GROUP, sem).wait()
        return carry

    lax.fori_loop(0, n_small, small, 0)


def _segment_table(seg, n_sorted_tiles):
    pieces = lambda rows: (rows // BIG_ROWS, (rows % BIG_ROWS) // GROUP)
    local_off = jnp.cumsum(seg, axis=1) - seg
    expert_rows = jnp.sum(seg, axis=0)
    expert_pad = ((expert_rows + ROW_TILE - 1) // ROW_TILE) * ROW_TILE
    expert_end = jnp.cumsum(expert_pad)
    expert_off = expert_end - expert_pad
    seg_start = expert_off[None, :] + jnp.cumsum(seg, axis=0) - seg
    n_used = (expert_end[-1] // ROW_TILE).reshape(1)
    tile_ids = jnp.arange(n_sorted_tiles, dtype=jnp.int32)
    tile_expert = jnp.minimum(jnp.sum(tile_ids[:, None] >= (expert_end // ROW_TILE)[None, :], axis=1), N_EXPERTS - 1)
    tile_expert = jnp.where(tile_ids < n_used[0], tile_expert, tile_expert[n_used[0] - 1])
    tile_valid = jnp.clip((expert_off + expert_rows)[tile_expert] - tile_ids * ROW_TILE, 0, ROW_TILE)
    tile_valid = jnp.where(tile_ids < n_used[0], tile_valid, 0)
    seg_big, seg_small = pieces(seg)
    tail_big, tail_small = pieces(expert_pad - expert_rows)
    table = jnp.concatenate([
        jnp.stack([seg_start, seg_big, seg_small, local_off], axis=-1).reshape(-1),
        jnp.stack([jnp.sum(seg_big, axis=1), jnp.sum(seg_small, axis=1)], axis=-1).reshape(-1),
        jnp.stack([expert_off + expert_rows, tail_big, tail_small], axis=-1).reshape(-1),
        n_used])
    return table.astype(jnp.int32), tile_expert.astype(jnp.int32), tile_valid.astype(jnp.int32)


def _table_sections(n_tiles):
    tiles0 = SEG_FIELDS * n_tiles * N_EXPERTS
    tails0 = tiles0 + TILE_FIELDS * n_tiles
    return tiles0, tails0, tails0 + TAIL_FIELDS * N_EXPERTS


def _dispatch_body(tab_ref, info_ref, xp_ref, xs_ref, nrm_ref, sel_ref, out_ref, loc_scr, z_scr, sems, *,
                   n_prompt_tiles, n_tiles, n_sorted_tiles):
    i = pl.program_id(0)
    slot = i % 2
    tiles0, tails0, used0 = _table_sections(n_tiles)

    def wait_tile(t, s):
        _wait_pieces(tab_ref[tiles0 + TILE_FIELDS * t], tab_ref[tiles0 + TILE_FIELDS * t + 1], loc_scr.at[s], out_ref,
                     sems.at[s])

    @pl.when(i >= 2)
    def _():
        wait_tile(i - 2, slot)

    def run(x_ref):
        hb = _rms(x_ref[...], nrm_ref[...]).astype(BF16)
        info = info_ref[...]
        lane = lax.broadcasted_iota(jnp.int32, info.shape, 1)
        dest = jnp.where(lane < TOP_K, info, 0.0)
        drow = sum(_dot_nt(sel_ref[...], part) for part in _split3(dest))
        r = lax.broadcasted_iota(jnp.int32, (LOCAL_ROWS, ROW_TILE), 0).astype(F32)
        perm = jnp.where(r == drow[0:1], 1.0, jnp.where(r == drow[1:2], 1.0, 0.0)).astype(BF16)
        loc_scr[slot] = _dot(perm, hb)

    _on_row_source(i, n_prompt_tiles, run, (xp_ref,), (xs_ref,))

    for e in range(N_EXPERTS):
        base = SEG_FIELDS * (i * N_EXPERTS + e)
        _start_pieces(loc_scr.at[slot], tab_ref[base + 3], out_ref, tab_ref[base], tab_ref[base + 1],
                      tab_ref[base + 2], sems.at[slot])

    @pl.when(i == n_tiles - 1)
    def _():
        wait_tile(i, slot)
        if n_tiles > 1:
            wait_tile(i - 1, 1 - slot)
        z_scr[...] = jnp.zeros_like(z_scr)
        for e in range(N_EXPERTS):
            base = tails0 + TAIL_FIELDS * e
            _start_pieces(z_scr, 0, out_ref, tab_ref[base], tab_ref[base + 1], tab_ref[base + 2], sems.at[2])
            _wait_pieces(tab_ref[base + 1], tab_ref[base + 2], z_scr, out_ref, sems.at[2])

        def zero_tile(t, carry):
            cp = _rows_copy(z_scr, 0, out_ref, t * ROW_TILE, ROW_TILE, sems.at[2])
            cp.start()
            cp.wait()
            return carry

        lax.fori_loop(tab_ref[used0], n_sorted_tiles, zero_tile, 0)


def _dispatch(table, info, xp, xs, nrm, sel, n_sorted_tiles):
    n_prompt_tiles = xp.shape[0] // ROW_TILE
    n_tiles = n_prompt_tiles + xs.shape[0] // ROW_TILE
    whole = lambda *shape: pl.BlockSpec(shape, lambda i, tab: (0,) * len(shape))
    grid_spec = pltpu.PrefetchScalarGridSpec(
        num_scalar_prefetch=1, grid=(n_tiles,),
        in_specs=[pl.BlockSpec((ROW_TILE, LANES), lambda i, tab: (i, 0)), *_two_source_specs(n_prompt_tiles),
                  whole(1, D_MODEL), whole(GROUP, LANES)],
        out_specs=pl.BlockSpec(memory_space=pl.ANY),
        scratch_shapes=[pltpu.VMEM((2, LOCAL_ROWS, D_MODEL), F32), pltpu.VMEM((ROW_TILE, D_MODEL), F32),
                        pltpu.SemaphoreType.DMA((3,))])
    return pl.pallas_call(
        functools.partial(_dispatch_body, n_prompt_tiles=n_prompt_tiles, n_tiles=n_tiles,
                          n_sorted_tiles=n_sorted_tiles),
        grid_spec=grid_spec, out_shape=jax.ShapeDtypeStruct((n_sorted_tiles * ROW_TILE, D_MODEL), F32),
        compiler_params=_cparams(1), name="moe_dispatch")(table, info, xp, xs, nrm, sel)


def _combine_body(tab_ref, info_ref, xp_ref, xs_ref, e_ref, yp_ref, ys_ref, loc_scr, sems, *, n_prompt_tiles,
                  n_tiles):
    i = pl.program_id(0)
    slot = i % 2
    tiles0, _, _ = _table_sections(n_tiles)

    def fetch(t, s):
        for e in range(N_EXPERTS):
            base = SEG_FIELDS * (t * N_EXPERTS + e)
            _start_pieces(e_ref, tab_ref[base], loc_scr.at[s], tab_ref[base + 3], tab_ref[base + 1],
                          tab_ref[base + 2], sems.at[s])

    @pl.when(i == 0)
    def _():
        loc_scr[...] = jnp.zeros_like(loc_scr)
        fetch(0, 0)

    @pl.when(i + 1 < n_tiles)
    def _():
        fetch(i + 1, 1 - slot)

    _wait_pieces(tab_ref[tiles0 + TILE_FIELDS * i], tab_ref[tiles0 + TILE_FIELDS * i + 1], e_ref, loc_scr.at[slot],
                 sems.at[slot])

    def run(x_ref, y_ref):
        info = info_ref[...]
        eb = loc_scr[slot].astype(BF16)
        lane = lax.broadcasted_iota(jnp.int32, (ROW_TILE, LOCAL_ROWS), 1).astype(F32)
        pick = jnp.zeros((ROW_TILE, LOCAL_ROWS), F32)
        for k in range(TOP_K):
            pick = jnp.where(lane == info[:, k:k + 1], info[:, TOP_K + k:TOP_K + k + 1], pick)
        y_ref[...] = x_ref[...] + _dot(pick.astype(BF16), eb)

    _on_row_source(i, n_prompt_tiles, run, (xp_ref, yp_ref), (xs_ref, ys_ref))


def _combine(table, info, xp, xs, esorted):
    n_prompt_tiles = xp.shape[0] // ROW_TILE
    n_tiles = n_prompt_tiles + xs.shape[0] // ROW_TILE
    grid_spec = pltpu.PrefetchScalarGridSpec(
        num_scalar_prefetch=1, grid=(n_tiles,),
        in_specs=[pl.BlockSpec((ROW_TILE, LANES), lambda i, tab: (i, 0)), *_two_source_specs(n_prompt_tiles),
                  pl.BlockSpec(memory_space=pl.ANY)],
        out_specs=list(_two_source_specs(n_prompt_tiles)),
        scratch_shapes=[pltpu.VMEM((2, LOCAL_ROWS, D_MODEL), F32), pltpu.SemaphoreType.DMA((2,))])
    return pl.pallas_call(
        functools.partial(_combine_body, n_prompt_tiles=n_prompt_tiles, n_tiles=n_tiles),
        grid_spec=grid_spec,
        out_shape=[jax.ShapeDtypeStruct(xp.shape, F32), jax.ShapeDtypeStruct(xs.shape, F32)],
        compiler_params=_cparams(1), name="moe_combine")(table, info, xp, xs, esorted)


def _rope_tables(pos):
    half = C_HDIM // 2
    inv = 1.0 / (ROPE_THETA ** (jnp.arange(half, dtype=F32) / half))
    ang = pos.astype(F32)[:, None] * inv[None, :]
    cos = jnp.cos(ang)
    sin = jnp.sin(ang)
    reps = LANES // C_HDIM
    return jnp.tile(jnp.concatenate([cos, cos], axis=-1), (1, reps)), jnp.tile(jnp.concatenate([-sin, sin], axis=-1),
                                                                                 (1, reps))


def kernel(x_prompt, x_sample, mem_prompt, cache_mem_k, cache_mem_v, state_hgrn, cache_swa_k, cache_swa_v, norm_mix, norm_xattn, norm_ffn, even_w_in, even_w_out, gmlp_w_s, gmlp_b_s, gmlp_ln_g, gmlp_ln_b, hgrn_lb_logits, hgrn_out_norm, attn_w_in, attn_w_out, attn_q_norm, attn_k_norm, attn_sinks, xattn_mem_norm, xattn_w_q, xattn_w_k, xattn_w_v, xattn_w_o, xattn_q_norm, xattn_k_norm, ffn_w1, ffn_w3, ffn_w2, moe_router, moe_w1, moe_w3, moe_w2):
    n_batch, seq, d = x_prompt.shape
    dec_batch, dec_seq, _ = x_sample.shape
    n_mem = mem_prompt.shape[1]
    depth = norm_mix.shape[0]
    past_len = PAST_LEN
    assert d == D_MODEL and depth == 2 and seq % ROW_TILE == 0 and dec_batch * dec_seq == ROW_TILE
    assert dec_seq == CHUNK and ROW_TILE % n_mem == 0 and cache_swa_k.shape[2] == WINDOW
    assert even_w_in.shape[-1] == EVEN_IN and attn_w_in.shape[-1] == ODD_IN and ffn_w1.shape[-1] == D_FF
    assert moe_w1.shape[1] == N_EXPERTS
    n_prompt_rows = n_batch * seq
    n_rows = n_prompt_rows + dec_batch * dec_seq
    n_prompt_tiles = n_prompt_rows // ROW_TILE
    tiles_per_batch = seq // ROW_TILE
    row = lambda g: g.reshape(1, -1).astype(F32)

    lb_all = jnp.cumsum(jax.nn.softmax(hgrn_lb_logits.astype(F32), axis=0), axis=0)
    wcum = jnp.asarray(_cumsum_matrix(), BF16)
    lmask = jnp.asarray(_level_masks(), F32)

    def even_consts(n):
        tril = jnp.tril(jnp.ones((n, n), bool))
        ws = jnp.where(tril[None], gmlp_w_s[0, :, :n, :n], 0.0).astype(BF16)
        bs = jnp.broadcast_to(gmlp_b_s[0, :, :n, None], (A_GROUPS, n, LANES)).astype(F32)
        return [row(norm_mix[0]), even_w_in[0], even_w_out[0], ws, bs,
                gmlp_ln_g[0].reshape(A_GROUPS, 1, A_GDIM), gmlp_ln_b[0].reshape(A_GROUPS, 1, A_GDIM),
                row(lb_all[0]), row(hgrn_out_norm[0]), wcum, lmask]

    rows2d = lambda w: w.reshape(-1, w.shape[-1])
    xp, hgrn_p, w1b, ffn_w1b, ffn_w3b, ffn_w2b = _even_mixer(
        x_prompt.reshape(n_prompt_rows, d), jnp.zeros((n_batch, B_HEADS, B_HDIM, B_HDIM), F32),
        even_consts(A_CHUNK), n_batch=n_batch, tiles_per_batch=tiles_per_batch, n_seq=1, seq_rows=ROW_TILE,
        gchunk=A_CHUNK, emit_v=False, side_casts=(rows2d(moe_w1[0]), ffn_w1[0], ffn_w3[0], ffn_w2[0]))
    xs, hgrn_s, gmlp_v = _even_mixer(
        x_sample.reshape(ROW_TILE, d), state_hgrn[0], even_consts(min(A_CHUNK, dec_seq)), n_batch=1,
        tiles_per_batch=1, n_seq=dec_batch, seq_rows=dec_seq, gchunk=min(A_CHUNK, dec_seq), emit_v=True)

    mem_k, mem_v = _memory_kv(mem_prompt.reshape(n_batch * n_mem, d), xattn_mem_norm.reshape(depth, 1, d),
                              xattn_w_k, xattn_w_v, xattn_k_norm.reshape(depth, 1, X_HDIM),
                              n_batch=n_batch, n_mem=n_mem)

    def cross_attention(xp, xs, l, side_casts=()):
        consts = (row(norm_xattn[l]), xattn_w_q[l], xattn_w_o[l], row(xattn_q_norm[l]))
        xp = _xattn(xp, mem_k, mem_v, *consts, layer=l, steps_per_mem=tiles_per_batch, n_seq=1, seq_rows=ROW_TILE,
                    side_casts=side_casts)
        xs = _xattn(xs, cache_k, cache_v, *consts, layer=l, steps_per_mem=1, n_seq=dec_batch, seq_rows=dec_seq)
        return xp, xs

    cache_k = cache_mem_k.reshape(depth, dec_batch, n_mem * X_HEADS, X_HDIM)
    cache_v = cache_mem_v.reshape(depth, dec_batch, n_mem * X_HEADS, X_HDIM)

    xp, xs = cross_attention(xp, xs, 0)
    xp, xs, w2b, attn_w_in_b, attn_w_out_b = _dense_ffn(xp, xs, row(norm_ffn[0]), ffn_w1b, ffn_w3b, ffn_w2b,
                                                        (rows2d(moe_w2[0]), attn_w_in[0], attn_w_out[0]))

    reps = D_MODEL // C_HDIM
    bd = jnp.asarray(np.kron(np.eye(reps, dtype=np.float32), np.full((C_HDIM, C_HDIM), 1.0 / C_HDIM, np.float32)),
                     BF16)
    swa_consts = [row(norm_mix[1]), attn_w_in_b, attn_w_out_b, row(jnp.tile(attn_q_norm[0], C_HEADS)),
                  row(jnp.tile(attn_k_norm[0], C_KV_HEADS)), bd]
    sinks = attn_sinks[0].astype(F32)
    cos_p, sin_p = _rope_tables(jnp.arange(seq, dtype=jnp.int32))
    cos_s, sin_s = _rope_tables(past_len + jnp.arange(dec_seq, dtype=jnp.int32))
    no_past = jnp.zeros((n_batch, WINDOW, C_KV_DIM), F32)
    xp, swk_p, swv_p, w3b = _swa(xp, no_past, no_past, cos_p, sin_p, sinks, swa_consts, n_batch=n_batch,
                                 tiles_per_batch=tiles_per_batch, n_seq=1, seq_rows=ROW_TILE, past_valid=False,
                                 side_cast=rows2d(moe_w3[0]))
    xs, swk_s, swv_s = _swa(xs, cache_swa_k[0].reshape(dec_batch, WINDOW, C_KV_DIM),
                            cache_swa_v[0].reshape(dec_batch, WINDOW, C_KV_DIM),
                            jnp.tile(cos_s, (dec_batch, 1)), jnp.tile(sin_s, (dec_batch, 1)), sinks, swa_consts,
                            n_batch=1, tiles_per_batch=1, n_seq=dec_batch, seq_rows=dec_seq, past_valid=True)
    xp, xs = cross_attention(xp, xs, 1)

    n_tiles = n_rows // ROW_TILE
    router_w = jnp.zeros((d, LANES), F32).at[:, :N_EXPERTS].set(moe_router[0].astype(F32))
    rhi = router_w.astype(BF16)
    rlo = (router_w - rhi.astype(F32)).astype(BF16)
    lstrict = jnp.asarray(np.tril(np.ones((ROW_TILE, ROW_TILE), np.float32), -1), BF16)
    su = jnp.asarray(np.triu(np.ones((LANES, LANES), np.float32), 1), BF16)
    sel = jnp.asarray(np.eye(GROUP, LANES, dtype=np.float32), BF16)
    nrm_ffn = row(norm_ffn[1])
    info, seg = _router(xp, xs, nrm_ffn, jnp.concatenate([rhi, rlo], axis=1), lstrict, su)
    seg = seg.reshape(n_tiles, GROUP, LANES)[:, 0, :N_EXPERTS].astype(jnp.int32)
    n_sorted_tiles = -(-(TOP_K * n_rows + n_tiles * N_EXPERTS * (GROUP - 1) + N_EXPERTS * (ROW_TILE - 1)) // ROW_TILE)
    table, tile_expert, tile_valid = _segment_table(seg, n_sorted_tiles)
    hsorted = _dispatch(table, info, xp, xs, nrm_ffn, sel, n_sorted_tiles)
    esorted = _grouped_ffn(hsorted, tile_expert, tile_valid, w1b.reshape(moe_w1.shape[1:]),
                           w3b.reshape(moe_w3.shape[1:]), w2b.reshape(moe_w2.shape[1:]))
    y_prompt, y_sample = _combine(table, info, xp, xs, esorted)

    n_even = state_hgrn.shape[0]
    n_odd = cache_swa_k.shape[0]
    return (y_prompt.reshape(n_batch, seq, d), y_sample.reshape(dec_batch, dec_seq, d),
            mem_k.reshape(depth, n_batch, n_mem, X_HEADS, X_HDIM), mem_v.reshape(depth, n_batch, n_mem, X_HEADS, X_HDIM),
            hgrn_p.reshape(n_even, n_batch, B_HEADS, B_HDIM, B_HDIM),
            gmlp_v.reshape(n_even, dec_batch, dec_seq, A_GROUPS, A_GDIM),
            hgrn_s.reshape(n_even, dec_batch, B_HEADS, B_HDIM, B_HDIM),
            swk_p.reshape(n_odd, n_batch, WINDOW, C_KV_HEADS, C_HDIM), swv_p.reshape(n_odd, n_batch, WINDOW, C_KV_HEADS, C_HDIM),
            swk_s.reshape(n_odd, dec_batch, dec_seq, C_KV_HEADS, C_HDIM), swv_s.reshape(n_odd, dec_batch, dec_seq, C_KV_HEADS, C_HDIM))
```

```python
import functools

import numpy as np
import jax
import jax.numpy as jnp
from jax import lax
from jax.experimental import pallas as pl
from jax.experimental.pallas import tpu as pltpu

F32 = jnp.float32
BF16 = jnp.bfloat16

D_MODEL = 1024
EPS = 1e-6
LOG2_E = 1.4426950408889634
CHUNK = 64
A_GROUPS = 4
A_DIM = D_MODEL // 2
A_GDIM = A_DIM // A_GROUPS
A_CHUNK = 128
B_HEADS = 4
B_DIM = D_MODEL // 2
B_HDIM = B_DIM // B_HEADS
EVEN_IN = 2 * A_DIM + 4 * B_DIM
C_HEADS = 16
C_KV_HEADS = 4
C_HDIM = D_MODEL // C_HEADS
C_GROUP = C_HEADS // C_KV_HEADS
C_KV_DIM = C_KV_HEADS * C_HDIM
WINDOW = 128
ROPE_THETA = 10000.0
PAST_LEN = 4096
ODD_IN = (C_HEADS + 2 * C_KV_HEADS) * C_HDIM
X_HEADS = 4
X_HDIM = 128
X_DIM = X_HEADS * X_HDIM
D_FF = 2816
N_EXPERTS = 8
TOP_K = 2

LANES = 128
ROW_TILE = 512
FF_SUB = 512
N_LEVELS = 6
HGRN_STAGE_CHUNKS = 2
SWA_STAGE_CHUNKS = 2
GROUP = 8
LOCAL_ROWS = -(-(TOP_K * ROW_TILE + N_EXPERTS * (GROUP - 1)) // LANES) * LANES
VMEM_LIMIT = 58 * 1024 * 1024


def _cparams(n_axes):
    return pltpu.CompilerParams(dimension_semantics=("arbitrary",) * n_axes, vmem_limit_bytes=VMEM_LIMIT)


def _dot(a, b):
    return jnp.dot(a, b, preferred_element_type=F32)


def _dot_nt(a, b):
    return lax.dot_general(a, b, (((1,), (1,)), ((), ())), preferred_element_type=F32)


def _rms(x, g):
    return x * lax.rsqrt(jnp.mean(x * x, axis=-1, keepdims=True) + EPS) * g


def _silu(x):
    return x * (1.0 / (1.0 + jnp.exp(-x)))


def _split3(x):
    hi = x.astype(BF16)
    r1 = x - hi.astype(F32)
    mid = r1.astype(BF16)
    lo = (r1 - mid.astype(F32)).astype(BF16)
    return hi, mid, lo


def _cumsum_matrix():
    r = np.arange(CHUNK)
    s = np.arange(CHUNK)
    blocks = [(s[None, :] <= r[:, None])]
    for l in range(N_LEVELS):
        h = 1 << l
        ref = (r & ~(2 * h - 1)) + h - 1
        blocks.append(s[None, :] <= ref[:, None])
    w = np.concatenate(blocks, axis=0).astype(np.float32)
    return np.concatenate([w, w, w], axis=1)


def _level_masks():
    t = np.arange(CHUNK)[:, None]
    s = np.arange(CHUNK)[None, :]
    masks = []
    for l in range(N_LEVELS):
        masks.append(((t >> (l + 1)) == (s >> (l + 1))) & (((t >> l) & 1) == 1) & (((s >> l) & 1) == 0))
    masks.append(t == s)
    return np.stack(masks).astype(np.float32)


def _even_mixer_body(x_ref, s0_ref, nrm_ref, win32_ref, wout32_ref, ws_ref, bs_ref, lng_ref, lnb_ref, lb_ref, og_ref,
                     wcum_ref, lmask_ref, *rest, n_seq, seq_rows, gchunk, emit_v, n_side):
    rest = list(rest)
    side_srcs = [rest.pop(0) for _ in range(n_side)]
    y_ref, sout_ref = rest.pop(0), rest.pop(0)
    v_ref = rest.pop(0) if emit_v else None
    side_dsts = [rest.pop(0) for _ in range(n_side)]
    proj_scr, mixed_scr, st_scr, win_ref, wout_ref = rest[:5]
    rows = n_seq * seq_rows
    j = pl.program_id(1)
    _run_side_casts(pl.program_id(0) * pl.num_programs(1) + j, side_srcs, side_dsts, rest[5:])

    @pl.when((pl.program_id(0) == 0) & (j == 0))
    def _():
        win_ref[...] = win32_ref[...].astype(BF16)
        wout_ref[...] = wout32_ref[...].astype(BF16)

    @pl.when(j == 0)
    def _():
        for s in range(n_seq):
            for hd in range(B_HEADS):
                st_scr[s * B_HEADS + hd] = s0_ref[s, hd].T

    x = x_ref[...]
    h = _rms(x, nrm_ref[...]).astype(BF16)
    n_pieces = EVEN_IN // ROW_TILE
    for n in range(n_pieces):
        cs = slice(n * ROW_TILE, (n + 1) * ROW_TILE)
        proj_scr[:, cs] = _dot(h, win_ref[:, cs])

    groups = range(A_GROUPS)
    group_cols = lambda base, g: slice(base + g * A_GDIM, base + (g + 1) * A_GDIM)
    vgs = [jax.nn.gelu(proj_scr[:, group_cols(A_DIM, g)]) for g in groups]
    means = [jnp.mean(vg, axis=-1, keepdims=True) for vg in vgs]
    vcs = [vg - mu for vg, mu in zip(vgs, means)]
    variances = [jnp.mean(vc * vc, axis=-1, keepdims=True) for vc in vcs]
    for g in groups:
        gs = group_cols(0, g)
        vn = vcs[g] * lax.rsqrt(variances[g] + EPS) * lng_ref[g] + lnb_ref[g]
        if v_ref is not None:
            v_ref[:, gs] = vn
        ug = jax.nn.gelu(proj_scr[:, gs])
        vb = vn.astype(BF16)
        for c in range(rows // gchunk):
            rs = slice(c * gchunk, (c + 1) * gchunk)
            sp = _dot(ws_ref[g], vb[rs]) + bs_ref[g]
            mixed_scr[rs, gs] = ug[rs] * sp

    q0, f0, i0, g0 = (2 * A_DIM + k * B_DIM for k in range(4))
    lb = lb_ref[...]
    fg = lb + (1.0 - lb) * jax.nn.sigmoid(proj_scr[:, f0:f0 + B_DIM])
    proj_scr[:, 0:B_DIM] = jnp.log(fg) * LOG2_E
    proj_scr[:, B_DIM:2 * B_DIM] = 1.0 - fg
    proj_scr[:, q0:q0 + B_DIM] = _silu(proj_scr[:, q0:q0 + B_DIM])
    og = og_ref[...]
    chunks_per_seq = seq_rows // CHUNK

    heads = range(B_HEADS)
    head_cols = lambda base, hd: slice(base + hd * B_HDIM, base + (hd + 1) * B_HDIM)
    chunk_rows = lambda c: slice(c * CHUNK, (c + 1) * CHUNK)

    def chunk_local(c):
        rs = chunk_rows(c)
        hi, mid, lo = _split3(proj_scr[rs, 0:B_DIM])
        gg = _dot(wcum_ref[...], jnp.concatenate([hi, mid, lo], axis=0))
        vs = [proj_scr[rs, head_cols(i0, hd)] for hd in heads]
        vts = [v.T.astype(BF16) for v in vs]
        Gs = [gg[0:CHUNK, head_cols(0, hd)] for hd in heads]
        qs = [proj_scr[rs, head_cols(q0, hd)] for hd in heads]
        ks = [proj_scr[rs, head_cols(B_DIM, hd)] for hd in heads]
        operands = []
        for hd in heads:
            for l in range(N_LEVELS + 1):
                if l < N_LEVELS:
                    e = jnp.exp2(-jnp.abs(Gs[hd] - gg[(l + 1) * CHUNK:(l + 2) * CHUNK, head_cols(0, hd)]))
                    operands.append(((qs[hd] * e).astype(BF16), (ks[hd] * e).astype(BF16)))
                else:
                    operands.append((qs[hd].astype(BF16), ks[hd].astype(BF16)))
        blocks = [_dot_nt(qe, ke) for qe, ke in operands]
        parts = []
        for hd in heads:
            att = jnp.zeros((CHUNK, CHUNK), F32)
            for l in range(N_LEVELS + 1):
                att = jnp.where(lmask_ref[l] > 0.5, blocks[hd * (N_LEVELS + 1) + l], att)
            g_end = Gs[hd][CHUNK - 1:CHUNK, :]
            kd = (ks[hd] * jnp.exp2(g_end - Gs[hd])).astype(BF16)
            parts.append((_dot(att.astype(BF16), vs[hd].astype(BF16)), (qs[hd] * jnp.exp2(Gs[hd])).astype(BF16),
                          jnp.exp2(g_end), _dot(vts[hd], kd)))
        return parts

    def chunk_state(c, parts):
        rs = chunk_rows(c)
        sidx = (c // chunks_per_seq) * B_HEADS
        outs = []
        for hd in heads:
            o_local, q_decayed, decay, increment = parts[hd]
            st = st_scr[sidx + hd]
            outs.append(o_local + _dot_nt(q_decayed, st.astype(BF16)))
            st_scr[sidx + hd] = st * decay + increment
        for hd in heads:
            o = outs[hd]
            on = o * lax.rsqrt(jnp.mean(o * o, axis=-1, keepdims=True) + EPS) * og
            gate = _silu(proj_scr[rs, head_cols(g0, hd)])
            mixed_scr[rs, head_cols(A_DIM, hd)] = on * gate

    n_chunks = rows // CHUNK
    for first in range(0, n_chunks, HGRN_STAGE_CHUNKS):
        group = range(first, min(first + HGRN_STAGE_CHUNKS, n_chunks))
        local = [chunk_local(c) for c in group]
        for c, parts in zip(group, local):
            chunk_state(c, parts)

    y_ref[...] = x + _dot(mixed_scr[...].astype(BF16), wout_ref[...])

    @pl.when(j == pl.num_programs(1) - 1)
    def _():
        for s in range(n_seq):
            for hd in range(B_HEADS):
                sout_ref[s, hd] = st_scr[s * B_HEADS + hd].T


def _even_mixer(x, s0, consts, *, n_batch, tiles_per_batch, n_seq, seq_rows, gchunk, emit_v, side_casts=()):
    rows = n_seq * seq_rows
    grid = (n_batch, tiles_per_batch)
    tile = lambda b, j: (b * tiles_per_batch + j, 0)
    whole = lambda *shape: pl.BlockSpec(shape, lambda b, j: (0,) * len(shape))
    once = lambda *shape: pl.BlockSpec(shape, lambda b, j: (0,) * len(shape), pipeline_mode=pl.Buffered(1))
    in_specs = [
        pl.BlockSpec((rows, D_MODEL), tile),
        pl.BlockSpec((n_seq, B_HEADS, B_HDIM, B_HDIM), lambda b, j: (b, 0, 0, 0)),
        whole(1, D_MODEL), once(D_MODEL, EVEN_IN), once(D_MODEL, D_MODEL),
        whole(A_GROUPS, gchunk, gchunk), whole(A_GROUPS, gchunk, LANES),
        whole(A_GROUPS, 1, A_GDIM), whole(A_GROUPS, 1, A_GDIM), whole(1, B_DIM), whole(1, B_HDIM),
        whole((N_LEVELS + 1) * CHUNK, 3 * CHUNK), whole(N_LEVELS + 1, CHUNK, CHUNK),
    ]
    args = [x, s0] + list(consts)
    out_shape = [jax.ShapeDtypeStruct(x.shape, F32),
                 jax.ShapeDtypeStruct((n_batch * n_seq, B_HEADS, B_HDIM, B_HDIM), F32)]
    out_specs = [pl.BlockSpec((rows, D_MODEL), tile),
                 pl.BlockSpec((n_seq, B_HEADS, B_HDIM, B_HDIM), lambda b, j: (b, 0, 0, 0))]
    if emit_v:
        out_shape.append(jax.ShapeDtypeStruct((x.shape[0], A_DIM), F32))
        out_specs.append(pl.BlockSpec((rows, A_DIM), tile))
    scratch = [pltpu.VMEM((rows, EVEN_IN), F32), pltpu.VMEM((rows, D_MODEL), F32),
               pltpu.VMEM((n_seq * B_HEADS, B_HDIM, B_HDIM), F32),
               pltpu.VMEM((D_MODEL, EVEN_IN), BF16), pltpu.VMEM((D_MODEL, D_MODEL), BF16)]
    side_in, side_out, side_shapes, side_scratch = _side_cast_plumbing(side_casts, n_batch * tiles_per_batch)
    in_specs += side_in
    args += list(side_casts)
    out_specs += side_out
    out_shape += side_shapes
    scratch += side_scratch
    body = functools.partial(_even_mixer_body, n_seq=n_seq, seq_rows=seq_rows, gchunk=gchunk, emit_v=emit_v,
                             n_side=len(side_casts))
    return pl.pallas_call(
        body, grid=grid, in_specs=in_specs, out_specs=out_specs, out_shape=out_shape, scratch_shapes=scratch,
        compiler_params=_cparams(2),
        name="even_mixer_s" if emit_v else "even_mixer_p")(*args)


def _memory_kv_body(mem_ref, gm_ref, wk_ref, wv_ref, kg_ref, mk_ref, mv_ref, *, n_seq, n_mem):
    m = _rms(mem_ref[...], gm_ref[0]).astype(BF16)
    kk = _dot(m, wk_ref[0].astype(BF16))
    vv = _dot(m, wv_ref[0].astype(BF16))
    kg = kg_ref[0]
    for hd in range(X_HEADS):
        hs = slice(hd * X_HDIM, (hd + 1) * X_HDIM)
        kh = _rms(kk[:, hs], kg)
        for s in range(n_seq):
            mk_ref[0, s, pl.ds(hd, n_mem, stride=X_HEADS), :] = kh[s * n_mem:(s + 1) * n_mem]
            mv_ref[0, s, pl.ds(hd, n_mem, stride=X_HEADS), :] = vv[s * n_mem:(s + 1) * n_mem, hs]


def _memory_kv(mem2d, g_mem, w_k, w_v, k_g, *, n_batch, n_mem):
    depth = w_k.shape[0]
    n_seq = ROW_TILE // n_mem
    out = jax.ShapeDtypeStruct((depth, n_batch, n_mem * X_HEADS, X_HDIM), F32)
    ospec = pl.BlockSpec((1, n_seq, n_mem * X_HEADS, X_HDIM), lambda l, t: (l, t, 0, 0))
    return pl.pallas_call(
        functools.partial(_memory_kv_body, n_seq=n_seq, n_mem=n_mem),
        grid=(depth, n_batch // n_seq),
        in_specs=[pl.BlockSpec((ROW_TILE, D_MODEL), lambda l, t: (t, 0)),
                  pl.BlockSpec((1, 1, D_MODEL), lambda l, t: (l, 0, 0)),
                  pl.BlockSpec((1, D_MODEL, X_DIM), lambda l, t: (l, 0, 0)),
                  pl.BlockSpec((1, D_MODEL, X_DIM), lambda l, t: (l, 0, 0)),
                  pl.BlockSpec((1, 1, X_HDIM), lambda l, t: (l, 0, 0))],
        out_specs=[ospec, ospec], out_shape=[out, out], compiler_params=_cparams(2),
        name="memory_kv")(mem2d, g_mem, w_k, w_v, k_g)


def _xattn_body(x_ref, mk_ref, mv_ref, nrm_ref, wq32_ref, wo32_ref, qg_ref, *rest, n_seq, seq_rows, n_side):
    side_srcs, y_ref, side_dsts = rest[:n_side], rest[n_side], rest[n_side + 1:2 * n_side + 1]
    o_scr, wq_ref, wo_ref = rest[2 * n_side + 1:2 * n_side + 4]
    _run_side_casts(pl.program_id(0), side_srcs, side_dsts, rest[2 * n_side + 4:])

    @pl.when(pl.program_id(0) == 0)
    def _():
        wq_ref[...] = wq32_ref[...].astype(BF16)
        wo_ref[...] = wo32_ref[...].astype(BF16)

    x = x_ref[...]
    h = _rms(x, nrm_ref[...]).astype(BF16)
    q = _dot(h, wq_ref[...])
    qg = qg_ref[...] * (X_HDIM ** -0.5)
    units = [(s, hd) for s in range(n_seq) for hd in range(X_HEADS)]
    rows_of = lambda s: slice(s * seq_rows, (s + 1) * seq_rows)
    lanes_of = lambda hd: slice(hd * X_HDIM, (hd + 1) * X_HDIM)
    n_mem = mk_ref.shape[1] // X_HEADS
    head_rows = lambda hd: pl.ds(hd, n_mem, stride=X_HEADS)
    qh = [_rms(q[rows_of(s), lanes_of(hd)], qg).astype(BF16) for s, hd in units]
    sc = [_dot_nt(qh[n], mk_ref[s, head_rows(hd), :].astype(BF16)) for n, (s, hd) in enumerate(units)]
    m = [jnp.max(t, axis=-1, keepdims=True) for t in sc]
    p = [jnp.exp(t - mx) for t, mx in zip(sc, m)]
    den = [jnp.sum(t, axis=-1, keepdims=True) for t in p]
    for n, (s, hd) in enumerate(units):
        o_scr[rows_of(s), lanes_of(hd)] = _dot(p[n].astype(BF16), mv_ref[s, head_rows(hd), :].astype(BF16)) / den[n]
    y_ref[...] = x + _dot(o_scr[...].astype(BF16), wo_ref[...])


def _xattn(x, mk, mv, nrm, wq, wo, qg, *, layer, steps_per_mem, n_seq, seq_rows, side_casts=()):
    rows = n_seq * seq_rows
    n_steps = x.shape[0] // rows
    mem_spec = pl.BlockSpec((None, n_seq) + mk.shape[2:], lambda t: (layer, t // steps_per_mem, 0, 0))
    whole = lambda *shape: pl.BlockSpec(shape, lambda t: (0,) * len(shape))
    once = lambda *shape: pl.BlockSpec((None,) + shape, lambda t: (layer,) + (0,) * len(shape),
                                       pipeline_mode=pl.Buffered(1))
    side_in, side_out, side_shapes, side_scratch = _side_cast_plumbing(side_casts, n_steps)
    out = pl.pallas_call(
        functools.partial(_xattn_body, n_seq=n_seq, seq_rows=seq_rows, n_side=len(side_casts)),
        grid=(n_steps,),
        in_specs=[pl.BlockSpec((rows, D_MODEL), lambda t: (t, 0)), mem_spec, mem_spec,
                  whole(1, D_MODEL), once(D_MODEL, X_DIM), once(X_DIM, D_MODEL), whole(1, X_HDIM)] + side_in,
        out_specs=[pl.BlockSpec((rows, D_MODEL), lambda t: (t, 0))] + side_out,
        out_shape=[jax.ShapeDtypeStruct(x.shape, F32)] + side_shapes,
        scratch_shapes=[pltpu.VMEM((rows, X_DIM), F32), pltpu.VMEM((D_MODEL, X_DIM), BF16),
                        pltpu.VMEM((X_DIM, D_MODEL), BF16)] + side_scratch,
        compiler_params=_cparams(1),
        name="xattn_s" if n_seq > 1 else "xattn_p")(x, mk, mv, nrm, wq, wo, qg, *side_casts)
    return out if side_casts else out[0]


def _two_source_specs(n_prompt_tiles, width=D_MODEL):
    return (pl.BlockSpec((ROW_TILE, width), lambda i, *_: (jnp.minimum(i, n_prompt_tiles - 1), 0)),
            pl.BlockSpec((ROW_TILE, width), lambda i, *_: (jnp.maximum(i - n_prompt_tiles, 0), 0)))


def _on_row_source(i, n_prompt_tiles, fn, prompt_refs, sample_refs):
    pl.when(i < n_prompt_tiles)(lambda: fn(*prompt_refs))
    pl.when(i >= n_prompt_tiles)(lambda: fn(*sample_refs))


def _swiglu_part(hb, w1, w3, w2):
    part = None
    for c0 in range(0, D_FF, FF_SUB):
        cs = slice(c0, min(c0 + FF_SUB, D_FF))
        act = (_silu(_dot(hb, w1[:, cs])) * _dot(hb, w3[:, cs])).astype(BF16)
        p = _dot(act, w2[cs, :])
        part = p if part is None else part + p
    return part


def _dense_ffn_body(xp_ref, xs_ref, nrm_ref, w1_ref, w3_ref, w2_ref, *rest, n_prompt_tiles, n_side):
    side_srcs, (yp_ref, ys_ref), side_dsts = rest[:n_side], rest[n_side:n_side + 2], rest[n_side + 2:2 * n_side + 2]
    i = pl.program_id(0)
    _run_side_casts(i, side_srcs, side_dsts, rest[2 * n_side + 2:])

    def run(x_ref, y_ref):
        x = x_ref[...]
        y_ref[...] = x + _swiglu_part(_rms(x, nrm_ref[...]).astype(BF16), w1_ref, w3_ref, w2_ref)

    _on_row_source(i, n_prompt_tiles, run, (xp_ref, yp_ref), (xs_ref, ys_ref))


def _dense_ffn(xp, xs, nrm, w1, w3, w2, side_casts):
    n_prompt_tiles = xp.shape[0] // ROW_TILE
    n_tiles = n_prompt_tiles + xs.shape[0] // ROW_TILE
    once = lambda *shape: pl.BlockSpec(shape, lambda i: (0,) * len(shape), pipeline_mode=pl.Buffered(1))
    side_in, side_out, side_shapes, side_scratch = _side_cast_plumbing(side_casts, n_tiles)
    return pl.pallas_call(
        functools.partial(_dense_ffn_body, n_prompt_tiles=n_prompt_tiles, n_side=len(side_casts)), grid=(n_tiles,),
        in_specs=[*_two_source_specs(n_prompt_tiles), pl.BlockSpec((1, D_MODEL), lambda i: (0, 0)),
                  once(D_MODEL, D_FF), once(D_MODEL, D_FF), once(D_FF, D_MODEL)] + side_in,
        out_specs=[*_two_source_specs(n_prompt_tiles)] + side_out,
        out_shape=[jax.ShapeDtypeStruct(xp.shape, F32), jax.ShapeDtypeStruct(xs.shape, F32)] + side_shapes,
        scratch_shapes=side_scratch,
        compiler_params=_cparams(1), name="dense_ffn")(xp, xs, nrm, w1, w3, w2, *side_casts)


def _grouped_ffn_body(te_ref, tv_ref, x_ref, w1_ref, w3_ref, w2_ref, y_ref):
    valid = tv_ref[pl.program_id(0)]
    half = ROW_TILE // 2
    weights = (w1_ref.at[0], w3_ref.at[0], w2_ref.at[0])

    @pl.when(valid > half)
    def _():
        y_ref[...] = _swiglu_part(x_ref[...].astype(BF16), *weights)

    @pl.when((valid > 0) & (valid <= half))
    def _():
        y_ref[0:half] = _swiglu_part(x_ref[0:half].astype(BF16), *weights)
        y_ref[half:ROW_TILE] = jnp.zeros((ROW_TILE - half, D_MODEL), F32)

    @pl.when(valid == 0)
    def _():
        y_ref[...] = jnp.zeros_like(y_ref)


def _grouped_ffn(xsorted, tile_expert, tile_valid, w1, w3, w2):
    n_tiles = xsorted.shape[0] // ROW_TILE
    grid_spec = pltpu.PrefetchScalarGridSpec(
        num_scalar_prefetch=2, grid=(n_tiles,),
        in_specs=[pl.BlockSpec((ROW_TILE, D_MODEL), lambda i, te, tv: (i, 0)),
                  pl.BlockSpec((1, D_MODEL, D_FF), lambda i, te, tv: (te[i], 0, 0)),
                  pl.BlockSpec((1, D_MODEL, D_FF), lambda i, te, tv: (te[i], 0, 0)),
                  pl.BlockSpec((1, D_FF, D_MODEL), lambda i, te, tv: (te[i], 0, 0))],
        out_specs=pl.BlockSpec((ROW_TILE, D_MODEL), lambda i, te, tv: (i, 0)))
    return pl.pallas_call(
        _grouped_ffn_body, grid_spec=grid_spec, out_shape=jax.ShapeDtypeStruct(xsorted.shape, F32),
        compiler_params=_cparams(1), name="grouped_ffn")(tile_expert, tile_valid, xsorted, w1, w3, w2)


def _rope_slab(xs, cos, sin_signed, first_half):
    rot = jnp.where(first_half, pltpu.roll(xs, LANES - C_HDIM // 2, 1), pltpu.roll(xs, C_HDIM // 2, 1))
    return xs * cos + rot * sin_signed


def _swa_body(sink_ref, x_ref, pk_ref, pv_ref, cos_ref, sin_ref, nrm_ref, win32_ref, wout32_ref, qg_ref, kg_ref,
              bd_ref, *rest, n_seq, seq_rows, past_valid, side_cast):
    rest = list(rest)
    side_src = rest.pop(0) if side_cast else None
    y_ref, ko_ref, vo_ref = rest.pop(0), rest.pop(0), rest.pop(0)
    side_dst = rest.pop(0) if side_cast else None
    k_scr, v_scr, q_scr, a_scr, win_ref, wout_ref = rest[:6]
    rows = n_seq * seq_rows
    j = pl.program_id(1)
    tail = min(WINDOW, seq_rows)
    if side_cast:
        _cast_rows_step(pl.program_id(0) * pl.num_programs(1) + j, side_src, side_dst, *rest[6:])

    @pl.when((pl.program_id(0) == 0) & (j == 0))
    def _():
        win_ref[...] = win32_ref[...].astype(BF16)
        wout_ref[...] = wout32_ref[...].astype(BF16)

    if past_valid:
        for s in range(n_seq):
            k_scr[s, 0:WINDOW] = pk_ref[s]
            v_scr[s, 0:WINDOW] = pv_ref[s]
    else:
        @pl.when(j == 0)
        def _():
            for s in range(n_seq):
                k_scr[s, 0:WINDOW] = jnp.zeros((WINDOW, C_KV_DIM), F32)
                v_scr[s, 0:WINDOW] = jnp.zeros((WINDOW, C_KV_DIM), F32)

        @pl.when(j > 0)
        def _():
            for s in range(n_seq):
                k_scr[s, 0:WINDOW] = k_scr[s, seq_rows:seq_rows + WINDOW]
                v_scr[s, 0:WINDOW] = v_scr[s, seq_rows:seq_rows + WINDOW]

    x = x_ref[...]
    h = _rms(x, nrm_ref[...]).astype(BF16)
    q_dim = C_HEADS * C_HDIM
    q = _dot(h, win_ref[:, 0:q_dim])
    k = _dot(h, win_ref[:, q_dim:q_dim + C_KV_DIM])
    v = _dot(h, win_ref[:, q_dim + C_KV_DIM:q_dim + 2 * C_KV_DIM])

    cos = cos_ref[...]
    sin_signed = sin_ref[...]
    first_half = (lax.broadcasted_iota(jnp.int32, (rows, LANES), 1) % C_HDIM) < (C_HDIM // 2)
    bd = bd_ref[...]
    qn = q * lax.rsqrt(_dot((q * q).astype(BF16), bd) + EPS) * qg_ref[...]
    kn = k * lax.rsqrt(_dot((k * k).astype(BF16), bd[0:C_KV_DIM, 0:C_KV_DIM]) + EPS) * kg_ref[...]
    scale = C_HDIM ** -0.5
    for sl in range(q_dim // LANES):
        ls = slice(sl * LANES, (sl + 1) * LANES)
        q_scr[:, ls] = _rope_slab(qn[:, ls], cos, sin_signed, first_half) * scale
    for sl in range(C_KV_DIM // LANES):
        ls = slice(sl * LANES, (sl + 1) * LANES)
        kr = _rope_slab(kn[:, ls], cos, sin_signed, first_half)
        for s in range(n_seq):
            k_scr[s, WINDOW:WINDOW + seq_rows, ls] = kr[s * seq_rows:(s + 1) * seq_rows]
    for s in range(n_seq):
        v_scr[s, WINDOW:WINDOW + seq_rows] = v[s * seq_rows:(s + 1) * seq_rows]
        ko_ref[s] = k_scr[s, WINDOW + seq_rows - tail:WINDOW + seq_rows]
        vo_ref[s] = v_scr[s, WINDOW + seq_rows - tail:WINDOW + seq_rows]

    chunks_per_seq = seq_rows // CHUNK
    n_keys = WINDOW + CHUNK
    key_idx = lax.broadcasted_iota(jnp.int32, (C_GROUP * CHUNK, n_keys), 1)
    row_grp = lax.broadcasted_iota(jnp.int32, (C_GROUP * CHUNK, 1), 0) // CHUNK
    ones_keys = jnp.ones((n_keys, C_HDIM), BF16)

    head_sinks = []
    for kvh in range(C_KV_HEADS):
        sink = jnp.zeros((C_GROUP * CHUNK, 1), F32)
        for g in range(C_GROUP):
            sink = jnp.where(row_grp == g, sink_ref[kvh * C_GROUP + g], sink)
        head_sinks.append(sink)

    def attend(chunk_ids):
        units = [(idx, kvh) for idx in chunk_ids for kvh in range(C_KV_HEADS)]
        head_lanes = lambda kvh: slice(kvh * C_HDIM, (kvh + 1) * C_HDIM)
        q_rows = lambda idx: slice(idx * CHUNK, (idx + 1) * CHUNK)
        key_rows = lambda idx: slice((idx % chunks_per_seq) * CHUNK, (idx % chunks_per_seq) * CHUNK + n_keys)
        qs = [jnp.concatenate(
            [q_scr[q_rows(idx), (kvh * C_GROUP + g) * C_HDIM:(kvh * C_GROUP + g + 1) * C_HDIM]
             for g in range(C_GROUP)], axis=0).astype(BF16) for idx, kvh in units]
        kh = [k_scr[idx // chunks_per_seq, key_rows(idx), head_lanes(kvh)].astype(BF16) for idx, kvh in units]
        vh = [v_scr[idx // chunks_per_seq, key_rows(idx), head_lanes(kvh)].astype(BF16) for idx, kvh in units]
        sc = [_dot_nt(qn, kn) for qn, kn in zip(qs, kh)]
        if not past_valid:
            sc = [jnp.where((key_idx + (idx % chunks_per_seq) * CHUNK >= WINDOW) | (j > 0), sn, -jnp.inf)
                  for sn, (idx, kvh) in zip(sc, units)]
        m = [jnp.maximum(jnp.max(sn, axis=-1, keepdims=True), head_sinks[kvh]) for sn, (idx, kvh) in zip(sc, units)]
        p = [jnp.exp(sn - mn).astype(BF16) for sn, mn in zip(sc, m)]
        den = [_dot(pn, ones_keys) + jnp.exp(head_sinks[kvh] - mn) for pn, mn, (idx, kvh) in zip(p, m, units)]
        o = [_dot(pn, vn) / dn for pn, vn, dn in zip(p, vh, den)]
        for on, (idx, kvh) in zip(o, units):
            for g in range(C_GROUP):
                hs = slice((kvh * C_GROUP + g) * C_HDIM, (kvh * C_GROUP + g + 1) * C_HDIM)
                a_scr[q_rows(idx), hs] = on[g * CHUNK:(g + 1) * CHUNK]

    n_chunks = rows // CHUNK
    for first in range(0, n_chunks, SWA_STAGE_CHUNKS):
        attend(range(first, min(first + SWA_STAGE_CHUNKS, n_chunks)))
    y_ref[...] = x + _dot(a_scr[...].astype(BF16), wout_ref[...])


def _swa(x, pk, pv, cos, sin_signed, sinks, consts, *, n_batch, tiles_per_batch, n_seq, seq_rows, past_valid,
         side_cast=None):
    rows = n_seq * seq_rows
    tail = min(WINDOW, seq_rows)
    n_str = n_batch * n_seq
    whole = lambda *shape: pl.BlockSpec(shape, lambda b, j, sk: (0,) * len(shape))
    once = lambda *shape: pl.BlockSpec(shape, lambda b, j, sk: (0,) * len(shape), pipeline_mode=pl.Buffered(1))
    in_specs = [pl.BlockSpec((rows, D_MODEL), lambda b, j, sk: (b * tiles_per_batch + j, 0)),
                pl.BlockSpec((n_seq, WINDOW, C_KV_DIM), lambda b, j, sk: (b, 0, 0)),
                pl.BlockSpec((n_seq, WINDOW, C_KV_DIM), lambda b, j, sk: (b, 0, 0)),
                pl.BlockSpec((rows, LANES), lambda b, j, sk: (j, 0)),
                pl.BlockSpec((rows, LANES), lambda b, j, sk: (j, 0)),
                whole(1, D_MODEL), once(D_MODEL, ODD_IN), once(D_MODEL, D_MODEL),
                whole(1, D_MODEL), whole(1, C_KV_DIM), whole(D_MODEL, D_MODEL)]
    args = [x, pk, pv, cos, sin_signed] + list(consts)
    kv_out = jax.ShapeDtypeStruct((n_str, tail, C_KV_DIM), F32)
    kv_spec = pl.BlockSpec((n_seq, tail, C_KV_DIM), lambda b, j, sk: (b, 0, 0))
    out_specs = [pl.BlockSpec((rows, D_MODEL), lambda b, j, sk: (b * tiles_per_batch + j, 0)), kv_spec, kv_spec]
    out_shape = [jax.ShapeDtypeStruct(x.shape, F32), kv_out, kv_out]
    scratch = [pltpu.VMEM((n_seq, WINDOW + seq_rows, C_KV_DIM), F32),
               pltpu.VMEM((n_seq, WINDOW + seq_rows, C_KV_DIM), F32),
               pltpu.VMEM((rows, D_MODEL), F32), pltpu.VMEM((rows, D_MODEL), F32),
               pltpu.VMEM((D_MODEL, ODD_IN), BF16), pltpu.VMEM((D_MODEL, D_MODEL), BF16)]
    if side_cast is not None:
        in_specs.append(pl.BlockSpec(memory_space=pl.ANY))
        args.append(side_cast)
        out_specs.append(pl.BlockSpec(memory_space=pl.ANY))
        out_shape.append(jax.ShapeDtypeStruct(side_cast.shape, BF16))
        scratch += _cast_scratch(side_cast, n_batch * tiles_per_batch)
    grid_spec = pltpu.PrefetchScalarGridSpec(
        num_scalar_prefetch=1, grid=(n_batch, tiles_per_batch), in_specs=in_specs, out_specs=out_specs,
        scratch_shapes=scratch)
    return pl.pallas_call(
        functools.partial(_swa_body, n_seq=n_seq, seq_rows=seq_rows, past_valid=past_valid,
                          side_cast=side_cast is not None),
        grid_spec=grid_spec, out_shape=out_shape,
        compiler_params=_cparams(2),
        name="swa_s" if past_valid else "swa_p")(sinks, *args)


BF16_ROWS = 16


def _cast_piece_rows(w, n_grid_steps):
    n_cast = 1 << (n_grid_steps.bit_length() - 1)
    while w.shape[0] % (n_cast * BF16_ROWS):
        n_cast //= 2
    return w.shape[0] // n_cast


def _cast_rows_step(step, src_hbm, dst_hbm, in_stage, out_stage, sems):
    n_rows = in_stage.shape[1]
    n_cast = src_hbm.shape[0] // n_rows
    slot = step % 2

    def read(s, sl):
        return pltpu.make_async_copy(src_hbm.at[pl.ds(pl.multiple_of(s * n_rows, 16), n_rows)], in_stage.at[sl],
                                     sems.at[sl])

    def write(s, sl):
        return pltpu.make_async_copy(out_stage.at[sl], dst_hbm.at[pl.ds(pl.multiple_of(s * n_rows, 16), n_rows)],
                                     sems.at[2 + sl])

    @pl.when(step == 0)
    def _():
        read(0, 0).start()

    @pl.when(step < n_cast)
    def _():
        @pl.when(step + 1 < n_cast)
        def _():
            read(step + 1, 1 - slot).start()

        read(step, slot).wait()

        @pl.when(step >= 2)
        def _():
            write(step - 2, slot).wait()

        out_stage[slot] = in_stage[slot].astype(BF16)
        write(step, slot).start()

        @pl.when(step == n_cast - 1)
        def _():
            write(step, slot).wait()
            if n_cast > 1:
                write(step - 1, 1 - slot).wait()


def _cast_scratch(w, n_grid_steps):
    rows = _cast_piece_rows(w, n_grid_steps)
    return [pltpu.VMEM((2, rows, w.shape[1]), F32), pltpu.VMEM((2, rows, w.shape[1]), BF16),
            pltpu.SemaphoreType.DMA((4,))]


def _side_cast_plumbing(side_casts, n_grid_steps):
    any_space = pl.BlockSpec(memory_space=pl.ANY)
    scratch = [s for w in side_casts for s in _cast_scratch(w, n_grid_steps)]
    return ([any_space] * len(side_casts), [any_space] * len(side_casts),
            [jax.ShapeDtypeStruct(w.shape, BF16) for w in side_casts], scratch)


def _run_side_casts(step, srcs, dsts, scratch):
    for n, (src, dst) in enumerate(zip(srcs, dsts)):
        _cast_rows_step(step, src, dst, *scratch[3 * n:3 * n + 3])


def _router_body(xp_ref, xs_ref, nrm_ref, rboth_ref, lstrict_ref, su_ref, info_ref, cnt_ref, *, n_prompt_tiles):
    def run(x_ref):
        h = _rms(x_ref[...], nrm_ref[...])
        hi = h.astype(BF16)
        lo = (h - hi.astype(F32)).astype(BF16)
        both = _dot(hi, rboth_ref[...])
        logits = both[:, 0:LANES] + both[:, LANES:2 * LANES] + _dot(lo, rboth_ref[:, 0:LANES])
        lane = lax.broadcasted_iota(jnp.int32, logits.shape, 1).astype(F32)
        logits = jnp.where(lane < N_EXPERTS, logits, -jnp.inf)
        l1 = jnp.max(logits, axis=-1, keepdims=True)
        i1 = jnp.min(jnp.where(logits == l1, lane, float(LANES)), axis=-1, keepdims=True)
        rest = jnp.where(lane == i1, -jnp.inf, logits)
        l2 = jnp.max(rest, axis=-1, keepdims=True)
        i2 = jnp.min(jnp.where(rest == l2, lane, float(LANES)), axis=-1, keepdims=True)
        e = jnp.exp(l2 - l1)
        g1 = 1.0 / (1.0 + e)
        g2 = e * g1
        oh1 = (lane == i1).astype(F32)
        oh2 = (lane == i2).astype(F32)
        cnt1 = jnp.sum(oh1, axis=0, keepdims=True)
        cnt2 = jnp.sum(oh2, axis=0, keepdims=True)
        seg = jnp.broadcast_to(jnp.ceil((cnt1 + cnt2) * (1.0 / GROUP)), (GROUP, LANES))
        off = _dot(seg.astype(BF16), su_ref[...])[0:1] * GROUP
        c1 = _dot(lstrict_ref[...], oh1.astype(BF16)) + off
        c2 = _dot(lstrict_ref[...], oh2.astype(BF16)) + off + cnt1
        d1 = jnp.sum(jnp.where(lane == i1, c1, 0.0), axis=-1, keepdims=True)
        d2 = jnp.sum(jnp.where(lane == i2, c2, 0.0), axis=-1, keepdims=True)
        info = jnp.zeros(logits.shape, F32)
        for n, col in enumerate([d1, d2, g1, g2]):
            info = jnp.where(lane == n, col, info)
        info_ref[...] = info
        cnt_ref[...] = seg * GROUP

    _on_row_source(pl.program_id(0), n_prompt_tiles, run, (xp_ref,), (xs_ref,))


def _router(xp, xs, nrm, rboth, lstrict, su):
    n_prompt_tiles = xp.shape[0] // ROW_TILE
    n_tiles = n_prompt_tiles + xs.shape[0] // ROW_TILE
    whole = lambda *shape: pl.BlockSpec(shape, lambda i: (0,) * len(shape))
    return pl.pallas_call(
        functools.partial(_router_body, n_prompt_tiles=n_prompt_tiles), grid=(n_tiles,),
        in_specs=[*_two_source_specs(n_prompt_tiles), whole(1, D_MODEL), whole(D_MODEL, 2 * LANES),
                  whole(ROW_TILE, ROW_TILE), whole(LANES, LANES)],
        out_specs=[pl.BlockSpec((ROW_TILE, LANES), lambda i: (i, 0)), pl.BlockSpec((GROUP, LANES), lambda i: (i, 0))],
        out_shape=[jax.ShapeDtypeStruct((n_tiles * ROW_TILE, LANES), F32),
                   jax.ShapeDtypeStruct((n_tiles * GROUP, LANES), F32)],
        compiler_params=_cparams(1), name="router")(xp, xs, nrm, rboth, lstrict, su)


BIG_ROWS = 8 * GROUP
SEG_FIELDS = 4
TILE_FIELDS = 2
TAIL_FIELDS = 3


def _rows_copy(src_ref, src_row, dst_ref, dst_row, n_rows, sem):
    return pltpu.make_async_copy(src_ref.at[pl.ds(pl.multiple_of(src_row, GROUP), n_rows)],
                                 dst_ref.at[pl.ds(pl.multiple_of(dst_row, GROUP), n_rows)], sem)


def _start_pieces(src_ref, src0, dst_ref, dst0, n_big, n_small, sem):
    def big(g, carry):
        _rows_copy(src_ref, src0 + g * BIG_ROWS, dst_ref, dst0 + g * BIG_ROWS, BIG_ROWS, sem).start()
        return carry

    lax.fori_loop(0, n_big, big, 0)
    done = n_big * BIG_ROWS

    def small(g, carry):
        _rows_copy(src_ref, src0 + done + g * GROUP, dst_ref, dst0 + done + g * GROUP, GROUP, sem).start()
        return carry

    lax.fori_loop(0, n_small, small, 0)


def _wait_pieces(n_big, n_small, src_ref, dst_ref, sem):
    def big(g, carry):
        _rows_copy(src_ref, 0, dst_ref, 0, BIG_ROWS, sem).wait()
        return carry

    lax.fori_loop(0, n_big, big, 0)

    def small(g, carry):
        _rows_copy(src_ref, 0, dst_ref, 0, GROUP, sem).wait()
        return carry

    lax.fori_loop(0, n_small, small, 0)


def _segment_table(seg, n_sorted_tiles):
    pieces = lambda rows: (rows // BIG_ROWS, (rows % BIG_ROWS) // GROUP)
    local_off = jnp.cumsum(seg, axis=1) - seg
    expert_rows = jnp.sum(seg, axis=0)
    expert_pad = ((expert_rows + ROW_TILE - 1) // ROW_TILE) * ROW_TILE
    expert_end = jnp.cumsum(expert_pad)
    expert_off = expert_end - expert_pad
    seg_start = expert_off[None, :] + jnp.cumsum(seg, axis=0) - seg
    n_used = (expert_end[-1] // ROW_TILE).reshape(1)
    tile_ids = jnp.arange(n_sorted_tiles, dtype=jnp.int32)
    tile_expert = jnp.minimum(jnp.sum(tile_ids[:, None] >= (expert_end // ROW_TILE)[None, :], axis=1), N_EXPERTS - 1)
    tile_rows0 = (tile_ids * ROW_TILE)[:, None]
    overlap = (jnp.minimum((expert_off + expert_rows)[None, :], tile_rows0 + ROW_TILE)
               - jnp.maximum(expert_off[None, :], tile_rows0))
    tile_valid = jnp.sum(jnp.maximum(overlap, 0), axis=1)
    seg_big, seg_small = pieces(seg)
    tail_big, tail_small = pieces(expert_pad - expert_rows)
    table = jnp.concatenate([
        jnp.stack([seg_start, seg_big, seg_small, local_off], axis=-1).reshape(-1),
        jnp.stack([jnp.sum(seg_big, axis=1), jnp.sum(seg_small, axis=1)], axis=-1).reshape(-1),
        jnp.stack([expert_off + expert_rows, tail_big, tail_small], axis=-1).reshape(-1),
        n_used])
    return table.astype(jnp.int32), tile_expert.astype(jnp.int32), tile_valid.astype(jnp.int32)


def _table_sections(n_tiles):
    tiles0 = SEG_FIELDS * n_tiles * N_EXPERTS
    tails0 = tiles0 + TILE_FIELDS * n_tiles
    return tiles0, tails0, tails0 + TAIL_FIELDS * N_EXPERTS


def _dispatch_body(tab_ref, info_ref, xp_ref, xs_ref, nrm_ref, sel_ref, out_ref, loc_scr, z_scr, sems, *,
                   n_prompt_tiles, n_tiles, n_sorted_tiles):
    i = pl.program_id(0)
    slot = i % 2
    tiles0, tails0, used0 = _table_sections(n_tiles)

    def wait_tile(t, s):
        _wait_pieces(tab_ref[tiles0 + TILE_FIELDS * t], tab_ref[tiles0 + TILE_FIELDS * t + 1], loc_scr.at[s], out_ref,
                     sems.at[s])

    @pl.when(i >= 2)
    def _():
        wait_tile(i - 2, slot)

    def run(x_ref):
        hb = _rms(x_ref[...], nrm_ref[...]).astype(BF16)
        info = info_ref[...]
        lane = lax.broadcasted_iota(jnp.int32, info.shape, 1)
        dest = jnp.where(lane < TOP_K, info, 0.0)
        drow = sum(_dot_nt(sel_ref[...], part) for part in _split3(dest))
        r = lax.broadcasted_iota(jnp.int32, (LOCAL_ROWS, ROW_TILE), 0).astype(F32)
        perm = jnp.where(r == drow[0:1], 1.0, jnp.where(r == drow[1:2], 1.0, 0.0)).astype(BF16)
        loc_scr[slot] = _dot(perm, hb)

    _on_row_source(i, n_prompt_tiles, run, (xp_ref,), (xs_ref,))

    for e in range(N_EXPERTS):
        base = SEG_FIELDS * (i * N_EXPERTS + e)
        _start_pieces(loc_scr.at[slot], tab_ref[base + 3], out_ref, tab_ref[base], tab_ref[base + 1],
                      tab_ref[base + 2], sems.at[slot])

    @pl.when(i == n_tiles - 1)
    def _():
        wait_tile(i, slot)
        if n_tiles > 1:
            wait_tile(i - 1, 1 - slot)
        z_scr[...] = jnp.zeros_like(z_scr)
        for e in range(N_EXPERTS):
            base = tails0 + TAIL_FIELDS * e
            _start_pieces(z_scr, 0, out_ref, tab_ref[base], tab_ref[base + 1], tab_ref[base + 2], sems.at[2])
            _wait_pieces(tab_ref[base + 1], tab_ref[base + 2], z_scr, out_ref, sems.at[2])

        def zero_tile(t, carry):
            cp = _rows_copy(z_scr, 0, out_ref, t * ROW_TILE, ROW_TILE, sems.at[2])
            cp.start()
            cp.wait()
            return carry

        lax.fori_loop(tab_ref[used0], n_sorted_tiles, zero_tile, 0)


def _dispatch(table, info, xp, xs, nrm, sel, n_sorted_tiles):
    n_prompt_tiles = xp.shape[0] // ROW_TILE
    n_tiles = n_prompt_tiles + xs.shape[0] // ROW_TILE
    whole = lambda *shape: pl.BlockSpec(shape, lambda i, tab: (0,) * len(shape))
    grid_spec = pltpu.PrefetchScalarGridSpec(
        num_scalar_prefetch=1, grid=(n_tiles,),
        in_specs=[pl.BlockSpec((ROW_TILE, LANES), lambda i, tab: (i, 0)), *_two_source_specs(n_prompt_tiles),
                  whole(1, D_MODEL), whole(GROUP, LANES)],
        out_specs=pl.BlockSpec(memory_space=pl.ANY),
        scratch_shapes=[pltpu.VMEM((2, LOCAL_ROWS, D_MODEL), F32), pltpu.VMEM((ROW_TILE, D_MODEL), F32),
                        pltpu.SemaphoreType.DMA((3,))])
    return pl.pallas_call(
        functools.partial(_dispatch_body, n_prompt_tiles=n_prompt_tiles, n_tiles=n_tiles,
                          n_sorted_tiles=n_sorted_tiles),
        grid_spec=grid_spec, out_shape=jax.ShapeDtypeStruct((n_sorted_tiles * ROW_TILE, D_MODEL), F32),
        compiler_params=_cparams(1), name="moe_dispatch")(table, info, xp, xs, nrm, sel)


def _combine_body(tab_ref, info_ref, xp_ref, xs_ref, e_ref, yp_ref, ys_ref, loc_scr, sems, *, n_prompt_tiles,
                  n_tiles):
    i = pl.program_id(0)
    slot = i % 2
    tiles0, _, _ = _table_sections(n_tiles)

    def fetch(t, s):
        for e in range(N_EXPERTS):
            base = SEG_FIELDS * (t * N_EXPERTS + e)
            _start_pieces(e_ref, tab_ref[base], loc_scr.at[s], tab_ref[base + 3], tab_ref[base + 1],
                          tab_ref[base + 2], sems.at[s])

    @pl.when(i == 0)
    def _():
        loc_scr[...] = jnp.zeros_like(loc_scr)
        fetch(0, 0)

    @pl.when(i + 1 < n_tiles)
    def _():
        fetch(i + 1, 1 - slot)

    _wait_pieces(tab_ref[tiles0 + TILE_FIELDS * i], tab_ref[tiles0 + TILE_FIELDS * i + 1], e_ref, loc_scr.at[slot],
                 sems.at[slot])

    def run(x_ref, y_ref):
        info = info_ref[...]
        eb = loc_scr[slot].astype(BF16)
        lane = lax.broadcasted_iota(jnp.int32, (ROW_TILE, LOCAL_ROWS), 1).astype(F32)
        pick = jnp.zeros((ROW_TILE, LOCAL_ROWS), F32)
        for k in range(TOP_K):
            pick = jnp.where(lane == info[:, k:k + 1], info[:, TOP_K + k:TOP_K + k + 1], pick)
        y_ref[...] = x_ref[...] + _dot(pick.astype(BF16), eb)

    _on_row_source(i, n_prompt_tiles, run, (xp_ref, yp_ref), (xs_ref, ys_ref))


def _combine(table, info, xp, xs, esorted):
    n_prompt_tiles = xp.shape[0] // ROW_TILE
    n_tiles = n_prompt_tiles + xs.shape[0] // ROW_TILE
    grid_spec = pltpu.PrefetchScalarGridSpec(
        num_scalar_prefetch=1, grid=(n_tiles,),
        in_specs=[pl.BlockSpec((ROW_TILE, LANES), lambda i, tab: (i, 0)), *_two_source_specs(n_prompt_tiles),
                  pl.BlockSpec(memory_space=pl.ANY)],
        out_specs=list(_two_source_specs(n_prompt_tiles)),
        scratch_shapes=[pltpu.VMEM((2, LOCAL_ROWS, D_MODEL), F32), pltpu.SemaphoreType.DMA((2,))])
    return pl.pallas_call(
        functools.partial(_combine_body, n_prompt_tiles=n_prompt_tiles, n_tiles=n_tiles),
        grid_spec=grid_spec,
        out_shape=[jax.ShapeDtypeStruct(xp.shape, F32), jax.ShapeDtypeStruct(xs.shape, F32)],
        compiler_params=_cparams(1), name="moe_combine")(table, info, xp, xs, esorted)


def _rope_tables(pos):
    half = C_HDIM // 2
    inv = 1.0 / (ROPE_THETA ** (jnp.arange(half, dtype=F32) / half))
    ang = pos.astype(F32)[:, None] * inv[None, :]
    cos = jnp.cos(ang)
    sin = jnp.sin(ang)
    reps = LANES // C_HDIM
    return jnp.tile(jnp.concatenate([cos, cos], axis=-1), (1, reps)), jnp.tile(jnp.concatenate([-sin, sin], axis=-1),
                                                                                 (1, reps))


def kernel(x_prompt, x_sample, mem_prompt, cache_mem_k, cache_mem_v, state_hgrn, cache_swa_k, cache_swa_v, norm_mix, norm_xattn, norm_ffn, even_w_in, even_w_out, gmlp_w_s, gmlp_b_s, gmlp_ln_g, gmlp_ln_b, hgrn_lb_logits, hgrn_out_norm, attn_w_in, attn_w_out, attn_q_norm, attn_k_norm, attn_sinks, xattn_mem_norm, xattn_w_q, xattn_w_k, xattn_w_v, xattn_w_o, xattn_q_norm, xattn_k_norm, ffn_w1, ffn_w3, ffn_w2, moe_router, moe_w1, moe_w3, moe_w2):
    n_batch, seq, d = x_prompt.shape
    dec_batch, dec_seq, _ = x_sample.shape
    n_mem = mem_prompt.shape[1]
    depth = norm_mix.shape[0]
    past_len = PAST_LEN
    assert d == D_MODEL and depth == 2 and seq % ROW_TILE == 0 and dec_batch * dec_seq == ROW_TILE
    assert dec_seq == CHUNK and ROW_TILE % n_mem == 0 and cache_swa_k.shape[2] == WINDOW
    assert even_w_in.shape[-1] == EVEN_IN and attn_w_in.shape[-1] == ODD_IN and ffn_w1.shape[-1] == D_FF
    assert moe_w1.shape[1] == N_EXPERTS
    n_prompt_rows = n_batch * seq
    n_rows = n_prompt_rows + dec_batch * dec_seq
    n_prompt_tiles = n_prompt_rows // ROW_TILE
    tiles_per_batch = seq // ROW_TILE
    row = lambda g: g.reshape(1, -1).astype(F32)

    lb_all = jnp.cumsum(jax.nn.softmax(hgrn_lb_logits.astype(F32), axis=0), axis=0)
    wcum = jnp.asarray(_cumsum_matrix(), BF16)
    lmask = jnp.asarray(_level_masks(), F32)

    def even_consts(n):
        tril = jnp.tril(jnp.ones((n, n), bool))
        ws = jnp.where(tril[None], gmlp_w_s[0, :, :n, :n], 0.0).astype(BF16)
        bs = jnp.broadcast_to(gmlp_b_s[0, :, :n, None], (A_GROUPS, n, LANES)).astype(F32)
        return [row(norm_mix[0]), even_w_in[0], even_w_out[0], ws, bs,
                gmlp_ln_g[0].reshape(A_GROUPS, 1, A_GDIM), gmlp_ln_b[0].reshape(A_GROUPS, 1, A_GDIM),
                row(lb_all[0]), row(hgrn_out_norm[0]), wcum, lmask]

    rows2d = lambda w: w.reshape(-1, w.shape[-1])
    xp, hgrn_p, w1b, ffn_w1b, ffn_w3b, ffn_w2b = _even_mixer(
        x_prompt.reshape(n_prompt_rows, d), jnp.zeros((n_batch, B_HEADS, B_HDIM, B_HDIM), F32),
        even_consts(A_CHUNK), n_batch=n_batch, tiles_per_batch=tiles_per_batch, n_seq=1, seq_rows=ROW_TILE,
        gchunk=A_CHUNK, emit_v=False, side_casts=(rows2d(moe_w1[0]), ffn_w1[0], ffn_w3[0], ffn_w2[0]))
    xs, hgrn_s, gmlp_v = _even_mixer(
        x_sample.reshape(ROW_TILE, d), state_hgrn[0], even_consts(min(A_CHUNK, dec_seq)), n_batch=1,
        tiles_per_batch=1, n_seq=dec_batch, seq_rows=dec_seq, gchunk=min(A_CHUNK, dec_seq), emit_v=True)

    mem_k, mem_v = _memory_kv(mem_prompt.reshape(n_batch * n_mem, d), xattn_mem_norm.reshape(depth, 1, d),
                              xattn_w_k, xattn_w_v, xattn_k_norm.reshape(depth, 1, X_HDIM),
                              n_batch=n_batch, n_mem=n_mem)

    def cross_attention(xp, xs, l, side_casts=()):
        consts = (row(norm_xattn[l]), xattn_w_q, xattn_w_o, row(xattn_q_norm[l]))
        xp = _xattn(xp, mem_k, mem_v, *consts, layer=l, steps_per_mem=tiles_per_batch, n_seq=1, seq_rows=ROW_TILE,
                    side_casts=side_casts)
        xs = _xattn(xs, cache_k, cache_v, *consts, layer=l, steps_per_mem=1, n_seq=dec_batch, seq_rows=dec_seq)
        return xp, xs

    cache_k = cache_mem_k.reshape(depth, dec_batch, n_mem * X_HEADS, X_HDIM)
    cache_v = cache_mem_v.reshape(depth, dec_batch, n_mem * X_HEADS, X_HDIM)

    xp, xs = cross_attention(xp, xs, 0)
    xp, xs, w2b, attn_w_in_b, attn_w_out_b = _dense_ffn(xp, xs, row(norm_ffn[0]), ffn_w1b, ffn_w3b, ffn_w2b,
                                                        (rows2d(moe_w2[0]), attn_w_in[0], attn_w_out[0]))

    reps = D_MODEL // C_HDIM
    bd = jnp.asarray(np.kron(np.eye(reps, dtype=np.float32), np.full((C_HDIM, C_HDIM), 1.0 / C_HDIM, np.float32)),
                     BF16)
    swa_consts = [row(norm_mix[1]), attn_w_in_b, attn_w_out_b, row(jnp.tile(attn_q_norm[0], C_HEADS)),
                  row(jnp.tile(attn_k_norm[0], C_KV_HEADS)), bd]
    sinks = attn_sinks[0].astype(F32)
    cos_p, sin_p = _rope_tables(jnp.arange(seq, dtype=jnp.int32))
    cos_s, sin_s = _rope_tables(past_len + jnp.arange(dec_seq, dtype=jnp.int32))
    no_past = jnp.zeros((n_batch, WINDOW, C_KV_DIM), F32)
    xp, swk_p, swv_p, w3b = _swa(xp, no_past, no_past, cos_p, sin_p, sinks, swa_consts, n_batch=n_batch,
                                 tiles_per_batch=tiles_per_batch, n_seq=1, seq_rows=ROW_TILE, past_valid=False,
                                 side_cast=rows2d(moe_w3[0]))
    xs, swk_s, swv_s = _swa(xs, cache_swa_k[0].reshape(dec_batch, WINDOW, C_KV_DIM),
                            cache_swa_v[0].reshape(dec_batch, WINDOW, C_KV_DIM),
                            jnp.tile(cos_s, (dec_batch, 1)), jnp.tile(sin_s, (dec_batch, 1)), sinks, swa_consts,
                            n_batch=1, tiles_per_batch=1, n_seq=dec_batch, seq_rows=dec_seq, past_valid=True)
    xp, xs = cross_attention(xp, xs, 1)

    n_tiles = n_rows // ROW_TILE
    router_w = jnp.zeros((d, LANES), F32).at[:, :N_EXPERTS].set(moe_router[0].astype(F32))
    rhi = router_w.astype(BF16)
    rlo = (router_w - rhi.astype(F32)).astype(BF16)
    lstrict = jnp.asarray(np.tril(np.ones((ROW_TILE, ROW_TILE), np.float32), -1), BF16)
    su = jnp.asarray(np.triu(np.ones((LANES, LANES), np.float32), 1), BF16)
    sel = jnp.asarray(np.eye(GROUP, LANES, dtype=np.float32), BF16)
    nrm_ffn = row(norm_ffn[1])
    info, seg = _router(xp, xs, nrm_ffn, jnp.concatenate([rhi, rlo], axis=1), lstrict, su)
    seg = seg.reshape(n_tiles, GROUP, LANES)[:, 0, :N_EXPERTS].astype(jnp.int32)
    n_sorted_tiles = -(-(TOP_K * n_rows + n_tiles * N_EXPERTS * (GROUP - 1) + N_EXPERTS * (ROW_TILE - 1)) // ROW_TILE)
    table, tile_expert, tile_valid = _segment_table(seg, n_sorted_tiles)
    hsorted = _dispatch(table, info, xp, xs, nrm_ffn, sel, n_sorted_tiles)
    esorted = _grouped_ffn(hsorted, tile_expert, tile_valid, w1b.reshape(moe_w1.shape[1:]),
                           w3b.reshape(moe_w3.shape[1:]), w2b.reshape(moe_w2.shape[1:]))
    y_prompt, y_sample = _combine(table, info, xp, xs, esorted)

    n_even = state_hgrn.shape[0]
    n_odd = cache_swa_k.shape[0]
    return (y_prompt.reshape(n_batch, seq, d), y_sample.reshape(dec_batch, dec_seq, d),
            mem_k.reshape(depth, n_batch, n_mem, X_HEADS, X_HDIM), mem_v.reshape(depth, n_batch, n_mem, X_HEADS, X_HDIM),
            hgrn_p.reshape(n_even, n_batch, B_HEADS, B_HDIM, B_HDIM),
            gmlp_v.reshape(n_even, dec_batch, dec_seq, A_GROUPS, A_GDIM),
            hgrn_s.reshape(n_even, dec_batch, B_HEADS, B_HDIM, B_HDIM),
            swk_p.reshape(n_odd, n_batch, WINDOW, C_KV_HEADS, C_HDIM), swv_p.reshape(n_odd, n_batch, WINDOW, C_KV_HEADS, C_HDIM),
            swk_s.reshape(n_odd, dec_batch, dec_seq, C_KV_HEADS, C_HDIM), swv_s.reshape(n_odd, dec_batch, dec_seq, C_KV_HEADS, C_HDIM))
```

```python
import functools

import numpy as np
import jax
import jax.numpy as jnp
from jax import lax
from jax.experimental import pallas as pl
from jax.experimental.pallas import tpu as pltpu

F32 = jnp.float32
BF16 = jnp.bfloat16

D_MODEL = 1024
EPS = 1e-6
LOG2_E = 1.4426950408889634
CHUNK = 64
A_GROUPS = 4
A_DIM = D_MODEL // 2
A_GDIM = A_DIM // A_GROUPS
A_CHUNK = 128
B_HEADS = 4
B_DIM = D_MODEL // 2
B_HDIM = B_DIM // B_HEADS
EVEN_IN = 2 * A_DIM + 4 * B_DIM
C_HEADS = 16
C_KV_HEADS = 4
C_HDIM = D_MODEL // C_HEADS
C_GROUP = C_HEADS // C_KV_HEADS
C_KV_DIM = C_KV_HEADS * C_HDIM
WINDOW = 128
ROPE_THETA = 10000.0
PAST_LEN = 4096
ODD_IN = (C_HEADS + 2 * C_KV_HEADS) * C_HDIM
X_HEADS = 4
X_HDIM = 128
X_DIM = X_HEADS * X_HDIM
D_FF = 2816
N_EXPERTS = 8
TOP_K = 2

LANES = 128
ROW_TILE = 512
FF_SUB_DENSE = 512
FF_SUB_GROUPED = 1024
N_LEVELS = 6
HGRN_STAGE_CHUNKS = 2
SWA_STAGE_CHUNKS = 2
GROUP = 8
LOCAL_ROWS = -(-(TOP_K * ROW_TILE + N_EXPERTS * (GROUP - 1)) // LANES) * LANES
VMEM_LIMIT = 58 * 1024 * 1024


def _cparams(n_axes):
    return pltpu.CompilerParams(dimension_semantics=("arbitrary",) * n_axes, vmem_limit_bytes=VMEM_LIMIT)


def _dot(a, b):
    return jnp.dot(a, b, preferred_element_type=F32)


def _dot_nt(a, b):
    return lax.dot_general(a, b, (((1,), (1,)), ((), ())), preferred_element_type=F32)


def _rms(x, g):
    return x * lax.rsqrt(jnp.mean(x * x, axis=-1, keepdims=True) + EPS) * g


def _silu(x):
    return x * (1.0 / (1.0 + jnp.exp(-x)))


def _split3(x):
    hi = x.astype(BF16)
    r1 = x - hi.astype(F32)
    mid = r1.astype(BF16)
    lo = (r1 - mid.astype(F32)).astype(BF16)
    return hi, mid, lo


def _cumsum_matrix():
    r = np.arange(CHUNK)
    s = np.arange(CHUNK)
    blocks = [(s[None, :] <= r[:, None])]
    for l in range(N_LEVELS):
        h = 1 << l
        ref = (r & ~(2 * h - 1)) + h - 1
        blocks.append(s[None, :] <= ref[:, None])
    w = np.concatenate(blocks, axis=0).astype(np.float32)
    return np.concatenate([w, w, w], axis=1)


def _level_masks():
    t = np.arange(CHUNK)[:, None]
    s = np.arange(CHUNK)[None, :]
    masks = []
    for l in range(N_LEVELS):
        masks.append(((t >> (l + 1)) == (s >> (l + 1))) & (((t >> l) & 1) == 1) & (((s >> l) & 1) == 0))
    masks.append(t == s)
    return np.stack(masks).astype(np.float32)


def _even_mixer_body(x_ref, s0_ref, nrm_ref, win32_ref, wout32_ref, ws_ref, bs_ref, lng_ref, lnb_ref, lb_ref, og_ref,
                     wcum_ref, lmask_ref, *rest, n_seq, seq_rows, gchunk, emit_v, n_side):
    rest = list(rest)
    side_srcs = [rest.pop(0) for _ in range(n_side)]
    y_ref, sout_ref = rest.pop(0), rest.pop(0)
    v_ref = rest.pop(0) if emit_v else None
    side_dsts = [rest.pop(0) for _ in range(n_side)]
    proj_scr, mixed_scr, st_scr, win_ref, wout_ref = rest[:5]
    rows = n_seq * seq_rows
    j = pl.program_id(1)
    _run_side_casts(pl.program_id(0) * pl.num_programs(1) + j, side_srcs, side_dsts, rest[5:])

    @pl.when((pl.program_id(0) == 0) & (j == 0))
    def _():
        win_ref[...] = win32_ref[...].astype(BF16)
        wout_ref[...] = wout32_ref[...].astype(BF16)

    @pl.when(j == 0)
    def _():
        for s in range(n_seq):
            for hd in range(B_HEADS):
                st_scr[s * B_HEADS + hd] = s0_ref[s, hd].T

    x = x_ref[...]
    h = _rms(x, nrm_ref[...]).astype(BF16)
    n_pieces = EVEN_IN // ROW_TILE
    for n in range(n_pieces):
        cs = slice(n * ROW_TILE, (n + 1) * ROW_TILE)
        proj_scr[:, cs] = _dot(h, win_ref[:, cs])

    groups = range(A_GROUPS)
    group_cols = lambda base, g: slice(base + g * A_GDIM, base + (g + 1) * A_GDIM)
    vgs = [jax.nn.gelu(proj_scr[:, group_cols(A_DIM, g)]) for g in groups]
    means = [jnp.mean(vg, axis=-1, keepdims=True) for vg in vgs]
    vcs = [vg - mu for vg, mu in zip(vgs, means)]
    variances = [jnp.mean(vc * vc, axis=-1, keepdims=True) for vc in vcs]
    for g in groups:
        gs = group_cols(0, g)
        vn = vcs[g] * lax.rsqrt(variances[g] + EPS) * lng_ref[g] + lnb_ref[g]
        if v_ref is not None:
            v_ref[:, gs] = vn
        ug = jax.nn.gelu(proj_scr[:, gs])
        vb = vn.astype(BF16)
        for c in range(rows // gchunk):
            rs = slice(c * gchunk, (c + 1) * gchunk)
            sp = _dot(ws_ref[g], vb[rs]) + bs_ref[g]
            mixed_scr[rs, gs] = ug[rs] * sp

    q0, f0, i0, g0 = (2 * A_DIM + k * B_DIM for k in range(4))
    lb = lb_ref[...]
    fg = lb + (1.0 - lb) * jax.nn.sigmoid(proj_scr[:, f0:f0 + B_DIM])
    proj_scr[:, 0:B_DIM] = jnp.log(fg) * LOG2_E
    proj_scr[:, B_DIM:2 * B_DIM] = 1.0 - fg
    proj_scr[:, q0:q0 + B_DIM] = _silu(proj_scr[:, q0:q0 + B_DIM])
    og = og_ref[...]
    chunks_per_seq = seq_rows // CHUNK

    heads = range(B_HEADS)
    head_cols = lambda base, hd: slice(base + hd * B_HDIM, base + (hd + 1) * B_HDIM)
    chunk_rows = lambda c: slice(c * CHUNK, (c + 1) * CHUNK)

    def chunk_local(c):
        rs = chunk_rows(c)
        hi, mid, lo = _split3(proj_scr[rs, 0:B_DIM])
        gg = _dot(wcum_ref[...], jnp.concatenate([hi, mid, lo], axis=0))
        vs = [proj_scr[rs, head_cols(i0, hd)] for hd in heads]
        vts = [v.T.astype(BF16) for v in vs]
        Gs = [gg[0:CHUNK, head_cols(0, hd)] for hd in heads]
        qs = [proj_scr[rs, head_cols(q0, hd)] for hd in heads]
        ks = [proj_scr[rs, head_cols(B_DIM, hd)] for hd in heads]
        operands = []
        for hd in heads:
            for l in range(N_LEVELS + 1):
                if l < N_LEVELS:
                    e = jnp.exp2(-jnp.abs(Gs[hd] - gg[(l + 1) * CHUNK:(l + 2) * CHUNK, head_cols(0, hd)]))
                    operands.append(((qs[hd] * e).astype(BF16), (ks[hd] * e).astype(BF16)))
                else:
                    operands.append((qs[hd].astype(BF16), ks[hd].astype(BF16)))
        blocks = [_dot_nt(qe, ke) for qe, ke in operands]
        parts = []
        for hd in heads:
            att = jnp.zeros((CHUNK, CHUNK), F32)
            for l in range(N_LEVELS + 1):
                att = jnp.where(lmask_ref[l] > 0.5, blocks[hd * (N_LEVELS + 1) + l], att)
            g_end = Gs[hd][CHUNK - 1:CHUNK, :]
            kd = (ks[hd] * jnp.exp2(g_end - Gs[hd])).astype(BF16)
            parts.append((_dot(att.astype(BF16), vs[hd].astype(BF16)), (qs[hd] * jnp.exp2(Gs[hd])).astype(BF16),
                          jnp.exp2(g_end), _dot(vts[hd], kd)))
        return parts

    def chunk_state(c, parts):
        rs = chunk_rows(c)
        sidx = (c // chunks_per_seq) * B_HEADS
        outs = []
        for hd in heads:
            o_local, q_decayed, decay, increment = parts[hd]
            st = st_scr[sidx + hd]
            outs.append(o_local + _dot_nt(q_decayed, st.astype(BF16)))
            st_scr[sidx + hd] = st * decay + increment
        for hd in heads:
            o = outs[hd]
            on = o * lax.rsqrt(jnp.mean(o * o, axis=-1, keepdims=True) + EPS) * og
            gate = _silu(proj_scr[rs, head_cols(g0, hd)])
            mixed_scr[rs, head_cols(A_DIM, hd)] = on * gate

    n_chunks = rows // CHUNK
    for first in range(0, n_chunks, HGRN_STAGE_CHUNKS):
        group = range(first, min(first + HGRN_STAGE_CHUNKS, n_chunks))
        local = [chunk_local(c) for c in group]
        for c, parts in zip(group, local):
            chunk_state(c, parts)

    y_ref[...] = x + _dot(mixed_scr[...].astype(BF16), wout_ref[...])

    @pl.when(j == pl.num_programs(1) - 1)
    def _():
        for s in range(n_seq):
            for hd in range(B_HEADS):
                sout_ref[s, hd] = st_scr[s * B_HEADS + hd].T


def _even_mixer(x, s0, consts, *, n_batch, tiles_per_batch, n_seq, seq_rows, gchunk, emit_v, side_casts=()):
    rows = n_seq * seq_rows
    grid = (n_batch, tiles_per_batch)
    tile = lambda b, j: (b * tiles_per_batch + j, 0)
    whole = lambda *shape: pl.BlockSpec(shape, lambda b, j: (0,) * len(shape))
    once = lambda *shape: pl.BlockSpec(shape, lambda b, j: (0,) * len(shape), pipeline_mode=pl.Buffered(1))
    in_specs = [
        pl.BlockSpec((rows, D_MODEL), tile),
        pl.BlockSpec((n_seq, B_HEADS, B_HDIM, B_HDIM), lambda b, j: (b, 0, 0, 0)),
        whole(1, D_MODEL), once(D_MODEL, EVEN_IN), once(D_MODEL, D_MODEL),
        whole(A_GROUPS, gchunk, gchunk), whole(A_GROUPS, gchunk, LANES),
        whole(A_GROUPS, 1, A_GDIM), whole(A_GROUPS, 1, A_GDIM), whole(1, B_DIM), whole(1, B_HDIM),
        whole((N_LEVELS + 1) * CHUNK, 3 * CHUNK), whole(N_LEVELS + 1, CHUNK, CHUNK),
    ]
    args = [x, s0] + list(consts)
    out_shape = [jax.ShapeDtypeStruct(x.shape, F32),
                 jax.ShapeDtypeStruct((n_batch * n_seq, B_HEADS, B_HDIM, B_HDIM), F32)]
    out_specs = [pl.BlockSpec((rows, D_MODEL), tile),
                 pl.BlockSpec((n_seq, B_HEADS, B_HDIM, B_HDIM), lambda b, j: (b, 0, 0, 0))]
    if emit_v:
        out_shape.append(jax.ShapeDtypeStruct((x.shape[0], A_DIM), F32))
        out_specs.append(pl.BlockSpec((rows, A_DIM), tile))
    scratch = [pltpu.VMEM((rows, EVEN_IN), F32), pltpu.VMEM((rows, D_MODEL), F32),
               pltpu.VMEM((n_seq * B_HEADS, B_HDIM, B_HDIM), F32),
               pltpu.VMEM((D_MODEL, EVEN_IN), BF16), pltpu.VMEM((D_MODEL, D_MODEL), BF16)]
    side_in, side_out, side_shapes, side_scratch = _side_cast_plumbing(side_casts, n_batch * tiles_per_batch)
    in_specs += side_in
    args += list(side_casts)
    out_specs += side_out
    out_shape += side_shapes
    scratch += side_scratch
    body = functools.partial(_even_mixer_body, n_seq=n_seq, seq_rows=seq_rows, gchunk=gchunk, emit_v=emit_v,
                             n_side=len(side_casts))
    return pl.pallas_call(
        body, grid=grid, in_specs=in_specs, out_specs=out_specs, out_shape=out_shape, scratch_shapes=scratch,
        compiler_params=_cparams(2),
        name="even_mixer_s" if emit_v else "even_mixer_p")(*args)


def _memory_kv_body(mem_ref, gm_ref, wk_ref, wv_ref, kg_ref, mk_ref, mv_ref, *, n_seq, n_mem):
    m = _rms(mem_ref[...], gm_ref[0]).astype(BF16)
    kk = _dot(m, wk_ref[0].astype(BF16))
    vv = _dot(m, wv_ref[0].astype(BF16))
    kg = kg_ref[0]
    for hd in range(X_HEADS):
        hs = slice(hd * X_HDIM, (hd + 1) * X_HDIM)
        kh = _rms(kk[:, hs], kg)
        for s in range(n_seq):
            mk_ref[0, s, pl.ds(hd, n_mem, stride=X_HEADS), :] = kh[s * n_mem:(s + 1) * n_mem]
            mv_ref[0, s, pl.ds(hd, n_mem, stride=X_HEADS), :] = vv[s * n_mem:(s + 1) * n_mem, hs]


def _memory_kv(mem2d, g_mem, w_k, w_v, k_g, *, n_batch, n_mem):
    depth = w_k.shape[0]
    n_seq = ROW_TILE // n_mem
    out = jax.ShapeDtypeStruct((depth, n_batch, n_mem * X_HEADS, X_HDIM), F32)
    ospec = pl.BlockSpec((1, n_seq, n_mem * X_HEADS, X_HDIM), lambda l, t: (l, t, 0, 0))
    return pl.pallas_call(
        functools.partial(_memory_kv_body, n_seq=n_seq, n_mem=n_mem),
        grid=(depth, n_batch // n_seq),
        in_specs=[pl.BlockSpec((ROW_TILE, D_MODEL), lambda l, t: (t, 0)),
                  pl.BlockSpec((1, 1, D_MODEL), lambda l, t: (l, 0, 0)),
                  pl.BlockSpec((1, D_MODEL, X_DIM), lambda l, t: (l, 0, 0)),
                  pl.BlockSpec((1, D_MODEL, X_DIM), lambda l, t: (l, 0, 0)),
                  pl.BlockSpec((1, 1, X_HDIM), lambda l, t: (l, 0, 0))],
        out_specs=[ospec, ospec], out_shape=[out, out], compiler_params=_cparams(2),
        name="memory_kv")(mem2d, g_mem, w_k, w_v, k_g)


def _xattn_body(x_ref, mk_ref, mv_ref, nrm_ref, wq32_ref, wo32_ref, qg_ref, *rest, n_seq, seq_rows, n_side):
    side_srcs, y_ref, side_dsts = rest[:n_side], rest[n_side], rest[n_side + 1:2 * n_side + 1]
    o_scr, wq_ref, wo_ref = rest[2 * n_side + 1:2 * n_side + 4]
    _run_side_casts(pl.program_id(0), side_srcs, side_dsts, rest[2 * n_side + 4:])

    @pl.when(pl.program_id(0) == 0)
    def _():
        wq_ref[...] = wq32_ref[...].astype(BF16)
        wo_ref[...] = wo32_ref[...].astype(BF16)

    x = x_ref[...]
    h = _rms(x, nrm_ref[...]).astype(BF16)
    q = _dot(h, wq_ref[...])
    qg = qg_ref[...] * (X_HDIM ** -0.5)
    units = [(s, hd) for s in range(n_seq) for hd in range(X_HEADS)]
    rows_of = lambda s: slice(s * seq_rows, (s + 1) * seq_rows)
    lanes_of = lambda hd: slice(hd * X_HDIM, (hd + 1) * X_HDIM)
    n_mem = mk_ref.shape[1] // X_HEADS
    head_rows = lambda hd: pl.ds(hd, n_mem, stride=X_HEADS)
    qh = [_rms(q[rows_of(s), lanes_of(hd)], qg).astype(BF16) for s, hd in units]
    sc = [_dot_nt(qh[n], mk_ref[s, head_rows(hd), :].astype(BF16)) for n, (s, hd) in enumerate(units)]
    m = [jnp.max(t, axis=-1, keepdims=True) for t in sc]
    p = [jnp.exp(t - mx) for t, mx in zip(sc, m)]
    den = [jnp.sum(t, axis=-1, keepdims=True) for t in p]
    for n, (s, hd) in enumerate(units):
        o_scr[rows_of(s), lanes_of(hd)] = _dot(p[n].astype(BF16), mv_ref[s, head_rows(hd), :].astype(BF16)) / den[n]
    y_ref[...] = x + _dot(o_scr[...].astype(BF16), wo_ref[...])


def _xattn(x, mk, mv, nrm, wq, wo, qg, *, layer, steps_per_mem, n_seq, seq_rows, side_casts=()):
    rows = n_seq * seq_rows
    n_steps = x.shape[0] // rows
    mem_spec = pl.BlockSpec((None, n_seq) + mk.shape[2:], lambda t: (layer, t // steps_per_mem, 0, 0))
    whole = lambda *shape: pl.BlockSpec(shape, lambda t: (0,) * len(shape))
    once = lambda *shape: pl.BlockSpec((None,) + shape, lambda t: (layer,) + (0,) * len(shape),
                                       pipeline_mode=pl.Buffered(1))
    side_in, side_out, side_shapes, side_scratch = _side_cast_plumbing(side_casts, n_steps)
    out = pl.pallas_call(
        functools.partial(_xattn_body, n_seq=n_seq, seq_rows=seq_rows, n_side=len(side_casts)),
        grid=(n_steps,),
        in_specs=[pl.BlockSpec((rows, D_MODEL), lambda t: (t, 0)), mem_spec, mem_spec,
                  whole(1, D_MODEL), once(D_MODEL, X_DIM), once(X_DIM, D_MODEL), whole(1, X_HDIM)] + side_in,
        out_specs=[pl.BlockSpec((rows, D_MODEL), lambda t: (t, 0))] + side_out,
        out_shape=[jax.ShapeDtypeStruct(x.shape, F32)] + side_shapes,
        scratch_shapes=[pltpu.VMEM((rows, X_DIM), F32), pltpu.VMEM((D_MODEL, X_DIM), BF16),
                        pltpu.VMEM((X_DIM, D_MODEL), BF16)] + side_scratch,
        compiler_params=_cparams(1),
        name="xattn_s" if n_seq > 1 else "xattn_p")(x, mk, mv, nrm, wq, wo, qg, *side_casts)
    return out if side_casts else out[0]


def _two_source_specs(n_prompt_tiles, width=D_MODEL):
    return (pl.BlockSpec((ROW_TILE, width), lambda i, *_: (jnp.minimum(i, n_prompt_tiles - 1), 0)),
            pl.BlockSpec((ROW_TILE, width), lambda i, *_: (jnp.maximum(i - n_prompt_tiles, 0), 0)))


def _on_row_source(i, n_prompt_tiles, fn, prompt_refs, sample_refs):
    pl.when(i < n_prompt_tiles)(lambda: fn(*prompt_refs))
    pl.when(i >= n_prompt_tiles)(lambda: fn(*sample_refs))


def _swiglu_part(hb, w1, w3, w2, ff_sub):
    part = None
    for c0 in range(0, D_FF, ff_sub):
        cs = slice(c0, min(c0 + ff_sub, D_FF))
        act = (_silu(_dot(hb, w1[:, cs])) * _dot(hb, w3[:, cs])).astype(BF16)
        p = _dot(act, w2[cs, :])
        part = p if part is None else part + p
    return part


def _dense_ffn_body(xp_ref, xs_ref, nrm_ref, w1_ref, w3_ref, w2_ref, *rest, n_prompt_tiles, n_side):
    side_srcs, (yp_ref, ys_ref), side_dsts = rest[:n_side], rest[n_side:n_side + 2], rest[n_side + 2:2 * n_side + 2]
    i = pl.program_id(0)
    _run_side_casts(i, side_srcs, side_dsts, rest[2 * n_side + 2:])

    def run(x_ref, y_ref):
        x = x_ref[...]
        y_ref[...] = x + _swiglu_part(_rms(x, nrm_ref[...]).astype(BF16), w1_ref, w3_ref, w2_ref, FF_SUB_DENSE)

    _on_row_source(i, n_prompt_tiles, run, (xp_ref, yp_ref), (xs_ref, ys_ref))


def _dense_ffn(xp, xs, nrm, w1, w3, w2, side_casts):
    n_prompt_tiles = xp.shape[0] // ROW_TILE
    n_tiles = n_prompt_tiles + xs.shape[0] // ROW_TILE
    once = lambda *shape: pl.BlockSpec(shape, lambda i: (0,) * len(shape), pipeline_mode=pl.Buffered(1))
    side_in, side_out, side_shapes, side_scratch = _side_cast_plumbing(side_casts, n_tiles)
    return pl.pallas_call(
        functools.partial(_dense_ffn_body, n_prompt_tiles=n_prompt_tiles, n_side=len(side_casts)), grid=(n_tiles,),
        in_specs=[*_two_source_specs(n_prompt_tiles), pl.BlockSpec((1, D_MODEL), lambda i: (0, 0)),
                  once(D_MODEL, D_FF), once(D_MODEL, D_FF), once(D_FF, D_MODEL)] + side_in,
        out_specs=[*_two_source_specs(n_prompt_tiles)] + side_out,
        out_shape=[jax.ShapeDtypeStruct(xp.shape, F32), jax.ShapeDtypeStruct(xs.shape, F32)] + side_shapes,
        scratch_shapes=side_scratch,
        compiler_params=_cparams(1), name="dense_ffn")(xp, xs, nrm, w1, w3, w2, *side_casts)


def _grouped_ffn_body(te_ref, tv_ref, x_ref, w1_ref, w3_ref, w2_ref, y_ref):
    valid = tv_ref[pl.program_id(0)]
    half = ROW_TILE // 2
    weights = (w1_ref.at[0], w3_ref.at[0], w2_ref.at[0])

    @pl.when(valid > half)
    def _():
        y_ref[...] = _swiglu_part(x_ref[...].astype(BF16), *weights, FF_SUB_GROUPED)

    @pl.when((valid > 0) & (valid <= half))
    def _():
        y_ref[0:half] = _swiglu_part(x_ref[0:half].astype(BF16), *weights, FF_SUB_GROUPED)
        y_ref[half:ROW_TILE] = jnp.zeros((ROW_TILE - half, D_MODEL), F32)

    @pl.when(valid == 0)
    def _():
        y_ref[...] = jnp.zeros_like(y_ref)


def _grouped_ffn(xsorted, tile_expert, tile_valid, w1, w3, w2):
    n_tiles = xsorted.shape[0] // ROW_TILE
    grid_spec = pltpu.PrefetchScalarGridSpec(
        num_scalar_prefetch=2, grid=(n_tiles,),
        in_specs=[pl.BlockSpec((ROW_TILE, D_MODEL), lambda i, te, tv: (i, 0)),
                  pl.BlockSpec((1, D_MODEL, D_FF), lambda i, te, tv: (te[i], 0, 0)),
                  pl.BlockSpec((1, D_MODEL, D_FF), lambda i, te, tv: (te[i], 0, 0)),
                  pl.BlockSpec((1, D_FF, D_MODEL), lambda i, te, tv: (te[i], 0, 0))],
        out_specs=pl.BlockSpec((ROW_TILE, D_MODEL), lambda i, te, tv: (i, 0)))
    return pl.pallas_call(
        _grouped_ffn_body, grid_spec=grid_spec, out_shape=jax.ShapeDtypeStruct(xsorted.shape, F32),
        compiler_params=_cparams(1), name="grouped_ffn")(tile_expert, tile_valid, xsorted, w1, w3, w2)


def _rope_slab(xs, cos, sin_signed, first_half):
    rot = jnp.where(first_half, pltpu.roll(xs, LANES - C_HDIM // 2, 1), pltpu.roll(xs, C_HDIM // 2, 1))
    return xs * cos + rot * sin_signed


def _swa_body(sink_ref, x_ref, pk_ref, pv_ref, cos_ref, sin_ref, nrm_ref, win32_ref, wout32_ref, qg_ref, kg_ref,
              bd_ref, *rest, n_seq, seq_rows, past_valid, side_cast):
    rest = list(rest)
    side_src = rest.pop(0) if side_cast else None
    y_ref, ko_ref, vo_ref = rest.pop(0), rest.pop(0), rest.pop(0)
    side_dst = rest.pop(0) if side_cast else None
    k_scr, v_scr, q_scr, a_scr, win_ref, wout_ref = rest[:6]
    rows = n_seq * seq_rows
    j = pl.program_id(1)
    tail = min(WINDOW, seq_rows)
    if side_cast:
        _cast_rows_step(pl.program_id(0) * pl.num_programs(1) + j, side_src, side_dst, *rest[6:])

    @pl.when((pl.program_id(0) == 0) & (j == 0))
    def _():
        win_ref[...] = win32_ref[...].astype(BF16)
        wout_ref[...] = wout32_ref[...].astype(BF16)

    if past_valid:
        for s in range(n_seq):
            k_scr[s, 0:WINDOW] = pk_ref[s]
            v_scr[s, 0:WINDOW] = pv_ref[s]
    else:
        @pl.when(j == 0)
        def _():
            for s in range(n_seq):
                k_scr[s, 0:WINDOW] = jnp.zeros((WINDOW, C_KV_DIM), F32)
                v_scr[s, 0:WINDOW] = jnp.zeros((WINDOW, C_KV_DIM), F32)

        @pl.when(j > 0)
        def _():
            for s in range(n_seq):
                k_scr[s, 0:WINDOW] = k_scr[s, seq_rows:seq_rows + WINDOW]
                v_scr[s, 0:WINDOW] = v_scr[s, seq_rows:seq_rows + WINDOW]

    x = x_ref[...]
    h = _rms(x, nrm_ref[...]).astype(BF16)
    q_dim = C_HEADS * C_HDIM
    q = _dot(h, win_ref[:, 0:q_dim])
    k = _dot(h, win_ref[:, q_dim:q_dim + C_KV_DIM])
    v = _dot(h, win_ref[:, q_dim + C_KV_DIM:q_dim + 2 * C_KV_DIM])

    cos = cos_ref[...]
    sin_signed = sin_ref[...]
    first_half = (lax.broadcasted_iota(jnp.int32, (rows, LANES), 1) % C_HDIM) < (C_HDIM // 2)
    bd = bd_ref[...]
    qn = q * lax.rsqrt(_dot((q * q).astype(BF16), bd) + EPS) * qg_ref[...]
    kn = k * lax.rsqrt(_dot((k * k).astype(BF16), bd[0:C_KV_DIM, 0:C_KV_DIM]) + EPS) * kg_ref[...]
    scale = C_HDIM ** -0.5
    for sl in range(q_dim // LANES):
        ls = slice(sl * LANES, (sl + 1) * LANES)
        q_scr[:, ls] = _rope_slab(qn[:, ls], cos, sin_signed, first_half) * scale
    for sl in range(C_KV_DIM // LANES):
        ls = slice(sl * LANES, (sl + 1) * LANES)
        kr = _rope_slab(kn[:, ls], cos, sin_signed, first_half)
        for s in range(n_seq):
            k_scr[s, WINDOW:WINDOW + seq_rows, ls] = kr[s * seq_rows:(s + 1) * seq_rows]
    for s in range(n_seq):
        v_scr[s, WINDOW:WINDOW + seq_rows] = v[s * seq_rows:(s + 1) * seq_rows]
        ko_ref[s] = k_scr[s, WINDOW + seq_rows - tail:WINDOW + seq_rows]
        vo_ref[s] = v_scr[s, WINDOW + seq_rows - tail:WINDOW + seq_rows]

    chunks_per_seq = seq_rows // CHUNK
    n_keys = WINDOW + CHUNK
    key_idx = lax.broadcasted_iota(jnp.int32, (C_GROUP * CHUNK, n_keys), 1)
    row_grp = lax.broadcasted_iota(jnp.int32, (C_GROUP * CHUNK, 1), 0) // CHUNK
    ones_keys = jnp.ones((n_keys, C_HDIM), BF16)

    head_sinks = []
    for kvh in range(C_KV_HEADS):
        sink = jnp.zeros((C_GROUP * CHUNK, 1), F32)
        for g in range(C_GROUP):
            sink = jnp.where(row_grp == g, sink_ref[kvh * C_GROUP + g], sink)
        head_sinks.append(sink)

    def attend(chunk_ids):
        units = [(idx, kvh) for idx in chunk_ids for kvh in range(C_KV_HEADS)]
        head_lanes = lambda kvh: slice(kvh * C_HDIM, (kvh + 1) * C_HDIM)
        q_rows = lambda idx: slice(idx * CHUNK, (idx + 1) * CHUNK)
        key_rows = lambda idx: slice((idx % chunks_per_seq) * CHUNK, (idx % chunks_per_seq) * CHUNK + n_keys)
        qs = [jnp.concatenate(
            [q_scr[q_rows(idx), (kvh * C_GROUP + g) * C_HDIM:(kvh * C_GROUP + g + 1) * C_HDIM]
             for g in range(C_GROUP)], axis=0).astype(BF16) for idx, kvh in units]
        kh = [k_scr[idx // chunks_per_seq, key_rows(idx), head_lanes(kvh)].astype(BF16) for idx, kvh in units]
        vh = [v_scr[idx // chunks_per_seq, key_rows(idx), head_lanes(kvh)].astype(BF16) for idx, kvh in units]
        sc = [_dot_nt(qn, kn) for qn, kn in zip(qs, kh)]
        if not past_valid:
            sc = [jnp.where((key_idx + (idx % chunks_per_seq) * CHUNK >= WINDOW) | (j > 0), sn, -jnp.inf)
                  for sn, (idx, kvh) in zip(sc, units)]
        m = [jnp.maximum(jnp.max(sn, axis=-1, keepdims=True), head_sinks[kvh]) for sn, (idx, kvh) in zip(sc, units)]
        p = [jnp.exp(sn - mn).astype(BF16) for sn, mn in zip(sc, m)]
        den = [_dot(pn, ones_keys) + jnp.exp(head_sinks[kvh] - mn) for pn, mn, (idx, kvh) in zip(p, m, units)]
        o = [_dot(pn, vn) / dn for pn, vn, dn in zip(p, vh, den)]
        for on, (idx, kvh) in zip(o, units):
            for g in range(C_GROUP):
                hs = slice((kvh * C_GROUP + g) * C_HDIM, (kvh * C_GROUP + g + 1) * C_HDIM)
                a_scr[q_rows(idx), hs] = on[g * CHUNK:(g + 1) * CHUNK]

    n_chunks = rows // CHUNK
    for first in range(0, n_chunks, SWA_STAGE_CHUNKS):
        attend(range(first, min(first + SWA_STAGE_CHUNKS, n_chunks)))
    y_ref[...] = x + _dot(a_scr[...].astype(BF16), wout_ref[...])


def _swa(x, pk, pv, cos, sin_signed, sinks, consts, *, n_batch, tiles_per_batch, n_seq, seq_rows, past_valid,
         side_cast=None):
    rows = n_seq * seq_rows
    tail = min(WINDOW, seq_rows)
    n_str = n_batch * n_seq
    whole = lambda *shape: pl.BlockSpec(shape, lambda b, j, sk: (0,) * len(shape))
    once = lambda *shape: pl.BlockSpec(shape, lambda b, j, sk: (0,) * len(shape), pipeline_mode=pl.Buffered(1))
    in_specs = [pl.BlockSpec((rows, D_MODEL), lambda b, j, sk: (b * tiles_per_batch + j, 0)),
                pl.BlockSpec((n_seq, WINDOW, C_KV_DIM), lambda b, j, sk: (b, 0, 0)),
                pl.BlockSpec((n_seq, WINDOW, C_KV_DIM), lambda b, j, sk: (b, 0, 0)),
                pl.BlockSpec((rows, LANES), lambda b, j, sk: (j, 0)),
                pl.BlockSpec((rows, LANES), lambda b, j, sk: (j, 0)),
                whole(1, D_MODEL), once(D_MODEL, ODD_IN), once(D_MODEL, D_MODEL),
                whole(1, D_MODEL), whole(1, C_KV_DIM), whole(D_MODEL, D_MODEL)]
    args = [x, pk, pv, cos, sin_signed] + list(consts)
    kv_out = jax.ShapeDtypeStruct((n_str, tail, C_KV_DIM), F32)
    kv_spec = pl.BlockSpec((n_seq, tail, C_KV_DIM), lambda b, j, sk: (b, 0, 0))
    out_specs = [pl.BlockSpec((rows, D_MODEL), lambda b, j, sk: (b * tiles_per_batch + j, 0)), kv_spec, kv_spec]
    out_shape = [jax.ShapeDtypeStruct(x.shape, F32), kv_out, kv_out]
    scratch = [pltpu.VMEM((n_seq, WINDOW + seq_rows, C_KV_DIM), F32),
               pltpu.VMEM((n_seq, WINDOW + seq_rows, C_KV_DIM), F32),
               pltpu.VMEM((rows, D_MODEL), F32), pltpu.VMEM((rows, D_MODEL), F32),
               pltpu.VMEM((D_MODEL, ODD_IN), BF16), pltpu.VMEM((D_MODEL, D_MODEL), BF16)]
    if side_cast is not None:
        in_specs.append(pl.BlockSpec(memory_space=pl.ANY))
        args.append(side_cast)
        out_specs.append(pl.BlockSpec(memory_space=pl.ANY))
        out_shape.append(jax.ShapeDtypeStruct(side_cast.shape, BF16))
        scratch += _cast_scratch(side_cast, n_batch * tiles_per_batch)
    grid_spec = pltpu.PrefetchScalarGridSpec(
        num_scalar_prefetch=1, grid=(n_batch, tiles_per_batch), in_specs=in_specs, out_specs=out_specs,
        scratch_shapes=scratch)
    return pl.pallas_call(
        functools.partial(_swa_body, n_seq=n_seq, seq_rows=seq_rows, past_valid=past_valid,
                          side_cast=side_cast is not None),
        grid_spec=grid_spec, out_shape=out_shape,
        compiler_params=_cparams(2),
        name="swa_s" if past_valid else "swa_p")(sinks, *args)


BF16_ROWS = 16


def _cast_piece_rows(w, n_grid_steps):
    n_cast = 1 << (n_grid_steps.bit_length() - 1)
    while w.shape[0] % (n_cast * BF16_ROWS):
        n_cast //= 2
    return w.shape[0] // n_cast


def _cast_rows_step(step, src_hbm, dst_hbm, in_stage, out_stage, sems):
    n_rows = in_stage.shape[1]
    n_cast = src_hbm.shape[0] // n_rows
    slot = step % 2

    def read(s, sl):
        return pltpu.make_async_copy(src_hbm.at[pl.ds(pl.multiple_of(s * n_rows, 16), n_rows)], in_stage.at[sl],
                                     sems.at[sl])

    def write(s, sl):
        return pltpu.make_async_copy(out_stage.at[sl], dst_hbm.at[pl.ds(pl.multiple_of(s * n_rows, 16), n_rows)],
                                     sems.at[2 + sl])

    @pl.when(step == 0)
    def _():
        read(0, 0).start()

    @pl.when(step < n_cast)
    def _():
        @pl.when(step + 1 < n_cast)
        def _():
            read(step + 1, 1 - slot).start()

        read(step, slot).wait()

        @pl.when(step >= 2)
        def _():
            write(step - 2, slot).wait()

        out_stage[slot] = in_stage[slot].astype(BF16)
        write(step, slot).start()

        @pl.when(step == n_cast - 1)
        def _():
            write(step, slot).wait()
            if n_cast > 1:
                write(step - 1, 1 - slot).wait()


def _cast_scratch(w, n_grid_steps):
    rows = _cast_piece_rows(w, n_grid_steps)
    return [pltpu.VMEM((2, rows, w.shape[1]), F32), pltpu.VMEM((2, rows, w.shape[1]), BF16),
            pltpu.SemaphoreType.DMA((4,))]


def _side_cast_plumbing(side_casts, n_grid_steps):
    any_space = pl.BlockSpec(memory_space=pl.ANY)
    scratch = [s for w in side_casts for s in _cast_scratch(w, n_grid_steps)]
    return ([any_space] * len(side_casts), [any_space] * len(side_casts),
            [jax.ShapeDtypeStruct(w.shape, BF16) for w in side_casts], scratch)


def _run_side_casts(step, srcs, dsts, scratch):
    for n, (src, dst) in enumerate(zip(srcs, dsts)):
        _cast_rows_step(step, src, dst, *scratch[3 * n:3 * n + 3])


def _router_body(xp_ref, xs_ref, nrm_ref, rboth_ref, lstrict_ref, su_ref, info_ref, cnt_ref, *, n_prompt_tiles):
    def run(x_ref):
        h = _rms(x_ref[...], nrm_ref[...])
        hi = h.astype(BF16)
        lo = (h - hi.astype(F32)).astype(BF16)
        both = _dot(hi, rboth_ref[...])
        logits = both[:, 0:LANES] + both[:, LANES:2 * LANES] + _dot(lo, rboth_ref[:, 0:LANES])
        lane = lax.broadcasted_iota(jnp.int32, logits.shape, 1).astype(F32)
        logits = jnp.where(lane < N_EXPERTS, logits, -jnp.inf)
        l1 = jnp.max(logits, axis=-1, keepdims=True)
        i1 = jnp.min(jnp.where(logits == l1, lane, float(LANES)), axis=-1, keepdims=True)
        rest = jnp.where(lane == i1, -jnp.inf, logits)
        l2 = jnp.max(rest, axis=-1, keepdims=True)
        i2 = jnp.min(jnp.where(rest == l2, lane, float(LANES)), axis=-1, keepdims=True)
        e = jnp.exp(l2 - l1)
        g1 = 1.0 / (1.0 + e)
        g2 = e * g1
        oh1 = (lane == i1).astype(F32)
        oh2 = (lane == i2).astype(F32)
        cnt1 = jnp.sum(oh1, axis=0, keepdims=True)
        cnt2 = jnp.sum(oh2, axis=0, keepdims=True)
        seg = jnp.broadcast_to(jnp.ceil((cnt1 + cnt2) * (1.0 / GROUP)), (GROUP, LANES))
        off = _dot(seg.astype(BF16), su_ref[...])[0:1] * GROUP
        c1 = _dot(lstrict_ref[...], oh1.astype(BF16)) + off
        c2 = _dot(lstrict_ref[...], oh2.astype(BF16)) + off + cnt1
        d1 = jnp.sum(jnp.where(lane == i1, c1, 0.0), axis=-1, keepdims=True)
        d2 = jnp.sum(jnp.where(lane == i2, c2, 0.0), axis=-1, keepdims=True)
        info = jnp.zeros(logits.shape, F32)
        for n, col in enumerate([d1, d2, g1, g2]):
            info = jnp.where(lane == n, col, info)
        info_ref[...] = info
        cnt_ref[...] = seg * GROUP

    _on_row_source(pl.program_id(0), n_prompt_tiles, run, (xp_ref,), (xs_ref,))


def _router(xp, xs, nrm, rboth, lstrict, su):
    n_prompt_tiles = xp.shape[0] // ROW_TILE
    n_tiles = n_prompt_tiles + xs.shape[0] // ROW_TILE
    whole = lambda *shape: pl.BlockSpec(shape, lambda i: (0,) * len(shape))
    return pl.pallas_call(
        functools.partial(_router_body, n_prompt_tiles=n_prompt_tiles), grid=(n_tiles,),
        in_specs=[*_two_source_specs(n_prompt_tiles), whole(1, D_MODEL), whole(D_MODEL, 2 * LANES),
                  whole(ROW_TILE, ROW_TILE), whole(LANES, LANES)],
        out_specs=[pl.BlockSpec((ROW_TILE, LANES), lambda i: (i, 0)), pl.BlockSpec((GROUP, LANES), lambda i: (i, 0))],
        out_shape=[jax.ShapeDtypeStruct((n_tiles * ROW_TILE, LANES), F32),
                   jax.ShapeDtypeStruct((n_tiles * GROUP, LANES), F32)],
        compiler_params=_cparams(1), name="router")(xp, xs, nrm, rboth, lstrict, su)


BIG_ROWS = 8 * GROUP
SEG_FIELDS = 4
TILE_FIELDS = 2
TAIL_FIELDS = 3


def _rows_copy(src_ref, src_row, dst_ref, dst_row, n_rows, sem):
    return pltpu.make_async_copy(src_ref.at[pl.ds(pl.multiple_of(src_row, GROUP), n_rows)],
                                 dst_ref.at[pl.ds(pl.multiple_of(dst_row, GROUP), n_rows)], sem)


def _start_pieces(src_ref, src0, dst_ref, dst0, n_big, n_small, sem):
    def big(g, carry):
        _rows_copy(src_ref, src0 + g * BIG_ROWS, dst_ref, dst0 + g * BIG_ROWS, BIG_ROWS, sem).start()
        return carry

    lax.fori_loop(0, n_big, big, 0)
    done = n_big * BIG_ROWS

    def small(g, carry):
        _rows_copy(src_ref, src0 + done + g * GROUP, dst_ref, dst0 + done + g * GROUP, GROUP, sem).start()
        return carry

    lax.fori_loop(0, n_small, small, 0)


def _wait_pieces(n_big, n_small, src_ref, dst_ref, sem):
    def big(g, carry):
        _rows_copy(src_ref, 0, dst_ref, 0, BIG_ROWS, sem).wait()
        return carry

    lax.fori_loop(0, n_big, big, 0)

    def small(g, carry):
        _rows_copy(src_ref, 0, dst_ref, 0, GROUP, sem).wait()
        return carry

    lax.fori_loop(0, n_small, small, 0)


def _segment_table(seg, n_sorted_tiles):
    pieces = lambda rows: (rows // BIG_ROWS, (rows % BIG_ROWS) // GROUP)
    local_off = jnp.cumsum(seg, axis=1) - seg
    expert_rows = jnp.sum(seg, axis=0)
    expert_pad = ((expert_rows + ROW_TILE - 1) // ROW_TILE) * ROW_TILE
    expert_end = jnp.cumsum(expert_pad)
    expert_off = expert_end - expert_pad
    seg_start = expert_off[None, :] + jnp.cumsum(seg, axis=0) - seg
    n_used = (expert_end[-1] // ROW_TILE).reshape(1)
    tile_ids = jnp.arange(n_sorted_tiles, dtype=jnp.int32)
    tile_expert = jnp.minimum(jnp.sum(tile_ids[:, None] >= (expert_end // ROW_TILE)[None, :], axis=1), N_EXPERTS - 1)
    tile_rows0 = (tile_ids * ROW_TILE)[:, None]
    overlap = (jnp.minimum((expert_off + expert_rows)[None, :], tile_rows0 + ROW_TILE)
               - jnp.maximum(expert_off[None, :], tile_rows0))
    tile_valid = jnp.sum(jnp.maximum(overlap, 0), axis=1)
    seg_big, seg_small = pieces(seg)
    tail_big, tail_small = pieces(expert_pad - expert_rows)
    table = jnp.concatenate([
        jnp.stack([seg_start, seg_big, seg_small, local_off], axis=-1).reshape(-1),
        jnp.stack([jnp.sum(seg_big, axis=1), jnp.sum(seg_small, axis=1)], axis=-1).reshape(-1),
        jnp.stack([expert_off + expert_rows, tail_big, tail_small], axis=-1).reshape(-1),
        n_used])
    return table.astype(jnp.int32), tile_expert.astype(jnp.int32), tile_valid.astype(jnp.int32)


def _table_sections(n_tiles):
    tiles0 = SEG_FIELDS * n_tiles * N_EXPERTS
    tails0 = tiles0 + TILE_FIELDS * n_tiles
    return tiles0, tails0, tails0 + TAIL_FIELDS * N_EXPERTS


def _dispatch_body(tab_ref, info_ref, xp_ref, xs_ref, nrm_ref, sel_ref, out_ref, loc_scr, z_scr, sems, *,
                   n_prompt_tiles, n_tiles, n_sorted_tiles):
    i = pl.program_id(0)
    slot = i % 2
    tiles0, tails0, used0 = _table_sections(n_tiles)

    def wait_tile(t, s):
        _wait_pieces(tab_ref[tiles0 + TILE_FIELDS * t], tab_ref[tiles0 + TILE_FIELDS * t + 1], loc_scr.at[s], out_ref,
                     sems.at[s])

    @pl.when(i >= 2)
    def _():
        wait_tile(i - 2, slot)

    def run(x_ref):
        hb = _rms(x_ref[...], nrm_ref[...]).astype(BF16)
        info = info_ref[...]
        lane = lax.broadcasted_iota(jnp.int32, info.shape, 1)
        dest = jnp.where(lane < TOP_K, info, 0.0)
        drow = sum(_dot_nt(sel_ref[...], part) for part in _split3(dest))
        r = lax.broadcasted_iota(jnp.int32, (LOCAL_ROWS, ROW_TILE), 0).astype(F32)
        perm = jnp.where(r == drow[0:1], 1.0, jnp.where(r == drow[1:2], 1.0, 0.0)).astype(BF16)
        loc_scr[slot] = _dot(perm, hb)

    _on_row_source(i, n_prompt_tiles, run, (xp_ref,), (xs_ref,))

    for e in range(N_EXPERTS):
        base = SEG_FIELDS * (i * N_EXPERTS + e)
        _start_pieces(loc_scr.at[slot], tab_ref[base + 3], out_ref, tab_ref[base], tab_ref[base + 1],
                      tab_ref[base + 2], sems.at[slot])

    @pl.when(i == n_tiles - 1)
    def _():
        wait_tile(i, slot)
        if n_tiles > 1:
            wait_tile(i - 1, 1 - slot)
        z_scr[...] = jnp.zeros_like(z_scr)
        for e in range(N_EXPERTS):
            base = tails0 + TAIL_FIELDS * e
            _start_pieces(z_scr, 0, out_ref, tab_ref[base], tab_ref[base + 1], tab_ref[base + 2], sems.at[2])
            _wait_pieces(tab_ref[base + 1], tab_ref[base + 2], z_scr, out_ref, sems.at[2])

        def zero_tile(t, carry):
            cp = _rows_copy(z_scr, 0, out_ref, t * ROW_TILE, ROW_TILE, sems.at[2])
            cp.start()
            cp.wait()
            return carry

        lax.fori_loop(tab_ref[used0], n_sorted_tiles, zero_tile, 0)


def _dispatch(table, info, xp, xs, nrm, sel, n_sorted_tiles):
    n_prompt_tiles = xp.shape[0] // ROW_TILE
    n_tiles = n_prompt_tiles + xs.shape[0] // ROW_TILE
    whole = lambda *shape: pl.BlockSpec(shape, lambda i, tab: (0,) * len(shape))
    grid_spec = pltpu.PrefetchScalarGridSpec(
        num_scalar_prefetch=1, grid=(n_tiles,),
        in_specs=[pl.BlockSpec((ROW_TILE, LANES), lambda i, tab: (i, 0)), *_two_source_specs(n_prompt_tiles),
                  whole(1, D_MODEL), whole(GROUP, LANES)],
        out_specs=pl.BlockSpec(memory_space=pl.ANY),
        scratch_shapes=[pltpu.VMEM((2, LOCAL_ROWS, D_MODEL), F32), pltpu.VMEM((ROW_TILE, D_MODEL), F32),
                        pltpu.SemaphoreType.DMA((3,))])
    return pl.pallas_call(
        functools.partial(_dispatch_body, n_prompt_tiles=n_prompt_tiles, n_tiles=n_tiles,
                          n_sorted_tiles=n_sorted_tiles),
        grid_spec=grid_spec, out_shape=jax.ShapeDtypeStruct((n_sorted_tiles * ROW_TILE, D_MODEL), F32),
        compiler_params=_cparams(1), name="moe_dispatch")(table, info, xp, xs, nrm, sel)


def _combine_body(tab_ref, info_ref, xp_ref, xs_ref, e_ref, yp_ref, ys_ref, loc_scr, sems, *, n_prompt_tiles,
                  n_tiles):
    i = pl.program_id(0)
    slot = i % 2
    tiles0, _, _ = _table_sections(n_tiles)

    def fetch(t, s):
        for e in range(N_EXPERTS):
            base = SEG_FIELDS * (t * N_EXPERTS + e)
            _start_pieces(e_ref, tab_ref[base], loc_scr.at[s], tab_ref[base + 3], tab_ref[base + 1],
                          tab_ref[base + 2], sems.at[s])

    @pl.when(i == 0)
    def _():
        loc_scr[...] = jnp.zeros_like(loc_scr)
        fetch(0, 0)

    @pl.when(i + 1 < n_tiles)
    def _():
        fetch(i + 1, 1 - slot)

    _wait_pieces(tab_ref[tiles0 + TILE_FIELDS * i], tab_ref[tiles0 + TILE_FIELDS * i + 1], e_ref, loc_scr.at[slot],
                 sems.at[slot])

    def run(x_ref, y_ref):
        info = info_ref[...]
        eb = loc_scr[slot].astype(BF16)
        lane = lax.broadcasted_iota(jnp.int32, (ROW_TILE, LOCAL_ROWS), 1).astype(F32)
        pick = jnp.zeros((ROW_TILE, LOCAL_ROWS), F32)
        for k in range(TOP_K):
            pick = jnp.where(lane == info[:, k:k + 1], info[:, TOP_K + k:TOP_K + k + 1], pick)
        y_ref[...] = x_ref[...] + _dot(pick.astype(BF16), eb)

    _on_row_source(i, n_prompt_tiles, run, (xp_ref, yp_ref), (xs_ref, ys_ref))


def _combine(table, info, xp, xs, esorted):
    n_prompt_tiles = xp.shape[0] // ROW_TILE
    n_tiles = n_prompt_tiles + xs.shape[0] // ROW_TILE
    grid_spec = pltpu.PrefetchScalarGridSpec(
        num_scalar_prefetch=1, grid=(n_tiles,),
        in_specs=[pl.BlockSpec((ROW_TILE, LANES), lambda i, tab: (i, 0)), *_two_source_specs(n_prompt_tiles),
                  pl.BlockSpec(memory_space=pl.ANY)],
        out_specs=list(_two_source_specs(n_prompt_tiles)),
        scratch_shapes=[pltpu.VMEM((2, LOCAL_ROWS, D_MODEL), F32), pltpu.SemaphoreType.DMA((2,))])
    return pl.pallas_call(
        functools.partial(_combine_body, n_prompt_tiles=n_prompt_tiles, n_tiles=n_tiles),
        grid_spec=grid_spec,
        out_shape=[jax.ShapeDtypeStruct(xp.shape, F32), jax.ShapeDtypeStruct(xs.shape, F32)],
        compiler_params=_cparams(1), name="moe_combine")(table, info, xp, xs, esorted)


def _rope_tables(pos):
    half = C_HDIM // 2
    inv = 1.0 / (ROPE_THETA ** (jnp.arange(half, dtype=F32) / half))
    ang = pos.astype(F32)[:, None] * inv[None, :]
    cos = jnp.cos(ang)
    sin = jnp.sin(ang)
    reps = LANES // C_HDIM
    return jnp.tile(jnp.concatenate([cos, cos], axis=-1), (1, reps)), jnp.tile(jnp.concatenate([-sin, sin], axis=-1),
                                                                                 (1, reps))


def kernel(x_prompt, x_sample, mem_prompt, cache_mem_k, cache_mem_v, state_hgrn, cache_swa_k, cache_swa_v, norm_mix, norm_xattn, norm_ffn, even_w_in, even_w_out, gmlp_w_s, gmlp_b_s, gmlp_ln_g, gmlp_ln_b, hgrn_lb_logits, hgrn_out_norm, attn_w_in, attn_w_out, attn_q_norm, attn_k_norm, attn_sinks, xattn_mem_norm, xattn_w_q, xattn_w_k, xattn_w_v, xattn_w_o, xattn_q_norm, xattn_k_norm, ffn_w1, ffn_w3, ffn_w2, moe_router, moe_w1, moe_w3, moe_w2):
    n_batch, seq, d = x_prompt.shape
    dec_batch, dec_seq, _ = x_sample.shape
    n_mem = mem_prompt.shape[1]
    depth = norm_mix.shape[0]
    past_len = PAST_LEN
    assert d == D_MODEL and depth == 2 and seq % ROW_TILE == 0 and dec_batch * dec_seq == ROW_TILE
    assert dec_seq == CHUNK and ROW_TILE % n_mem == 0 and cache_swa_k.shape[2] == WINDOW
    assert even_w_in.shape[-1] == EVEN_IN and attn_w_in.shape[-1] == ODD_IN and ffn_w1.shape[-1] == D_FF
    assert moe_w1.shape[1] == N_EXPERTS
    n_prompt_rows = n_batch * seq
    n_rows = n_prompt_rows + dec_batch * dec_seq
    n_prompt_tiles = n_prompt_rows // ROW_TILE
    tiles_per_batch = seq // ROW_TILE
    row = lambda g: g.reshape(1, -1).astype(F32)

    lb_all = jnp.cumsum(jax.nn.softmax(hgrn_lb_logits.astype(F32), axis=0), axis=0)
    wcum = jnp.asarray(_cumsum_matrix(), BF16)
    lmask = jnp.asarray(_level_masks(), F32)

    def even_consts(n):
        tril = jnp.tril(jnp.ones((n, n), bool))
        ws = jnp.where(tril[None], gmlp_w_s[0, :, :n, :n], 0.0).astype(BF16)
        bs = jnp.broadcast_to(gmlp_b_s[0, :, :n, None], (A_GROUPS, n, LANES)).astype(F32)
        return [row(norm_mix[0]), even_w_in[0], even_w_out[0], ws, bs,
                gmlp_ln_g[0].reshape(A_GROUPS, 1, A_GDIM), gmlp_ln_b[0].reshape(A_GROUPS, 1, A_GDIM),
                row(lb_all[0]), row(hgrn_out_norm[0]), wcum, lmask]

    rows2d = lambda w: w.reshape(-1, w.shape[-1])
    xp, hgrn_p, w1b, ffn_w1b, ffn_w3b, ffn_w2b = _even_mixer(
        x_prompt.reshape(n_prompt_rows, d), jnp.zeros((n_batch, B_HEADS, B_HDIM, B_HDIM), F32),
        even_consts(A_CHUNK), n_batch=n_batch, tiles_per_batch=tiles_per_batch, n_seq=1, seq_rows=ROW_TILE,
        gchunk=A_CHUNK, emit_v=False, side_casts=(rows2d(moe_w1[0]), ffn_w1[0], ffn_w3[0], ffn_w2[0]))
    xs, hgrn_s, gmlp_v = _even_mixer(
        x_sample.reshape(ROW_TILE, d), state_hgrn[0], even_consts(min(A_CHUNK, dec_seq)), n_batch=1,
        tiles_per_batch=1, n_seq=dec_batch, seq_rows=dec_seq, gchunk=min(A_CHUNK, dec_seq), emit_v=True)

    mem_k, mem_v = _memory_kv(mem_prompt.reshape(n_batch * n_mem, d), xattn_mem_norm.reshape(depth, 1, d),
                              xattn_w_k, xattn_w_v, xattn_k_norm.reshape(depth, 1, X_HDIM),
                              n_batch=n_batch, n_mem=n_mem)

    def cross_attention(xp, xs, l, side_casts=()):
        consts = (row(norm_xattn[l]), xattn_w_q, xattn_w_o, row(xattn_q_norm[l]))
        xp = _xattn(xp, mem_k, mem_v, *consts, layer=l, steps_per_mem=tiles_per_batch, n_seq=1, seq_rows=ROW_TILE,
                    side_casts=side_casts)
        xs = _xattn(xs, cache_k, cache_v, *consts, layer=l, steps_per_mem=1, n_seq=dec_batch, seq_rows=dec_seq)
        return xp, xs

    cache_k = cache_mem_k.reshape(depth, dec_batch, n_mem * X_HEADS, X_HDIM)
    cache_v = cache_mem_v.reshape(depth, dec_batch, n_mem * X_HEADS, X_HDIM)

    xp, xs = cross_attention(xp, xs, 0)
    xp, xs, w2b, attn_w_in_b, attn_w_out_b = _dense_ffn(xp, xs, row(norm_ffn[0]), ffn_w1b, ffn_w3b, ffn_w2b,
                                                        (rows2d(moe_w2[0]), attn_w_in[0], attn_w_out[0]))

    reps = D_MODEL // C_HDIM
    bd = jnp.asarray(np.kron(np.eye(reps, dtype=np.float32), np.full((C_HDIM, C_HDIM), 1.0 / C_HDIM, np.float32)),
                     BF16)
    swa_consts = [row(norm_mix[1]), attn_w_in_b, attn_w_out_b, row(jnp.tile(attn_q_norm[0], C_HEADS)),
                  row(jnp.tile(attn_k_norm[0], C_KV_HEADS)), bd]
    sinks = attn_sinks[0].astype(F32)
    cos_p, sin_p = _rope_tables(jnp.arange(seq, dtype=jnp.int32))
    cos_s, sin_s = _rope_tables(past_len + jnp.arange(dec_seq, dtype=jnp.int32))
    no_past = jnp.zeros((n_batch, WINDOW, C_KV_DIM), F32)
    xp, swk_p, swv_p, w3b = _swa(xp, no_past, no_past, cos_p, sin_p, sinks, swa_consts, n_batch=n_batch,
                                 tiles_per_batch=tiles_per_batch, n_seq=1, seq_rows=ROW_TILE, past_valid=False,
                                 side_cast=rows2d(moe_w3[0]))
    xs, swk_s, swv_s = _swa(xs, cache_swa_k[0].reshape(dec_batch, WINDOW, C_KV_DIM),
                            cache_swa_v[0].reshape(dec_batch, WINDOW, C_KV_DIM),
                            jnp.tile(cos_s, (dec_batch, 1)), jnp.tile(sin_s, (dec_batch, 1)), sinks, swa_consts,
                            n_batch=1, tiles_per_batch=1, n_seq=dec_batch, seq_rows=dec_seq, past_valid=True)
    xp, xs = cross_attention(xp, xs, 1)

    n_tiles = n_rows // ROW_TILE
    router_w = jnp.zeros((d, LANES), F32).at[:, :N_EXPERTS].set(moe_router[0].astype(F32))
    rhi = router_w.astype(BF16)
    rlo = (router_w - rhi.astype(F32)).astype(BF16)
    lstrict = jnp.asarray(np.tril(np.ones((ROW_TILE, ROW_TILE), np.float32), -1), BF16)
    su = jnp.asarray(np.triu(np.ones((LANES, LANES), np.float32), 1), BF16)
    sel = jnp.asarray(np.eye(GROUP, LANES, dtype=np.float32), BF16)
    nrm_ffn = row(norm_ffn[1])
    info, seg = _router(xp, xs, nrm_ffn, jnp.concatenate([rhi, rlo], axis=1), lstrict, su)
    seg = seg.reshape(n_tiles, GROUP, LANES)[:, 0, :N_EXPERTS].astype(jnp.int32)
    n_sorted_tiles = -(-(TOP_K * n_rows + n_tiles * N_EXPERTS * (GROUP - 1) + N_EXPERTS * (ROW_TILE - 1)) // ROW_TILE)
    table, tile_expert, tile_valid = _segment_table(seg, n_sorted_tiles)
    hsorted = _dispatch(table, info, xp, xs, nrm_ffn, sel, n_sorted_tiles)
    esorted = _grouped_ffn(hsorted, tile_expert, tile_valid, w1b.reshape(moe_w1.shape[1:]),
                           w3b.reshape(moe_w3.shape[1:]), w2b.reshape(moe_w2.shape[1:]))
    y_prompt, y_sample = _combine(table, info, xp, xs, esorted)

    n_even = state_hgrn.shape[0]
    n_odd = cache_swa_k.shape[0]
    return (y_prompt.reshape(n_batch, seq, d), y_sample.reshape(dec_batch, dec_seq, d),
            mem_k.reshape(depth, n_batch, n_mem, X_HEADS, X_HDIM), mem_v.reshape(depth, n_batch, n_mem, X_HEADS, X_HDIM),
            hgrn_p.reshape(n_even, n_batch, B_HEADS, B_HDIM, B_HDIM),
            gmlp_v.reshape(n_even, dec_batch, dec_seq, A_GROUPS, A_GDIM),
            hgrn_s.reshape(n_even, dec_batch, B_HEADS, B_HDIM, B_HDIM),
            swk_p.reshape(n_odd, n_batch, WINDOW, C_KV_HEADS, C_HDIM), swv_p.reshape(n_odd, n_batch, WINDOW, C_KV_HEADS, C_HDIM),
            swk_s.reshape(n_odd, dec_batch, dec_seq, C_KV_HEADS, C_HDIM), swv_s.reshape(n_odd, dec_batch, dec_seq, C_KV_HEADS, C_HDIM))
```

```python
import functools

import numpy as np
import jax
import jax.numpy as jnp
from jax import lax
from jax.experimental import pallas as pl
from jax.experimental.pallas import tpu as pltpu

F32 = jnp.float32
BF16 = jnp.bfloat16

D_MODEL = 1024
EPS = 1e-6
LOG2_E = 1.4426950408889634
CHUNK = 64
A_GROUPS = 4
A_DIM = D_MODEL // 2
A_GDIM = A_DIM // A_GROUPS
A_CHUNK = 128
B_HEADS = 4
B_DIM = D_MODEL // 2
B_HDIM = B_DIM // B_HEADS
EVEN_IN = 2 * A_DIM + 4 * B_DIM
C_HEADS = 16
C_KV_HEADS = 4
C_HDIM = D_MODEL // C_HEADS
C_GROUP = C_HEADS // C_KV_HEADS
C_KV_DIM = C_KV_HEADS * C_HDIM
WINDOW = 128
ROPE_THETA = 10000.0
PAST_LEN = 4096
ODD_IN = (C_HEADS + 2 * C_KV_HEADS) * C_HDIM
X_HEADS = 4
X_HDIM = 128
X_DIM = X_HEADS * X_HDIM
D_FF = 2816
N_EXPERTS = 8
TOP_K = 2

LANES = 128
ROW_TILE = 512
FF_SUB = 512
N_LEVELS = 6
HGRN_STAGE_CHUNKS = 2
SWA_STAGE_CHUNKS = 2
GROUP = 8
LOCAL_ROWS = -(-(TOP_K * ROW_TILE + N_EXPERTS * (GROUP - 1)) // LANES) * LANES
VMEM_LIMIT = 58 * 1024 * 1024


def _cparams(n_axes):
    return pltpu.CompilerParams(dimension_semantics=("arbitrary",) * n_axes, vmem_limit_bytes=VMEM_LIMIT)


def _dot(a, b):
    return jnp.dot(a, b, preferred_element_type=F32)


def _dot_nt(a, b):
    return lax.dot_general(a, b, (((1,), (1,)), ((), ())), preferred_element_type=F32)


def _rms(x, g):
    return x * lax.rsqrt(jnp.mean(x * x, axis=-1, keepdims=True) + EPS) * g


def _silu(x):
    return x * (1.0 / (1.0 + jnp.exp(-x)))


def _split3(x):
    hi = x.astype(BF16)
    r1 = x - hi.astype(F32)
    mid = r1.astype(BF16)
    lo = (r1 - mid.astype(F32)).astype(BF16)
    return hi, mid, lo


def _cumsum_matrix():
    r = np.arange(CHUNK)
    s = np.arange(CHUNK)
    blocks = [(s[None, :] <= r[:, None])]
    for l in range(N_LEVELS):
        h = 1 << l
        ref = (r & ~(2 * h - 1)) + h - 1
        blocks.append(s[None, :] <= ref[:, None])
    w = np.concatenate(blocks, axis=0).astype(np.float32)
    return np.concatenate([w, w, w], axis=1)


def _level_masks():
    t = np.arange(CHUNK)[:, None]
    s = np.arange(CHUNK)[None, :]
    masks = []
    for l in range(N_LEVELS):
        masks.append(((t >> (l + 1)) == (s >> (l + 1))) & (((t >> l) & 1) == 1) & (((s >> l) & 1) == 0))
    masks.append(t == s)
    return np.stack(masks).astype(np.float32)


def _even_mixer_body(x_ref, s0_ref, nrm_ref, win32_ref, wout32_ref, ws_ref, bs_ref, lng_ref, lnb_ref, lb_ref, og_ref,
                     wcum_ref, lmask_ref, *rest, n_seq, seq_rows, gchunk, emit_v, n_side):
    rest = list(rest)
    side_srcs = [rest.pop(0) for _ in range(n_side)]
    y_ref, sout_ref = rest.pop(0), rest.pop(0)
    v_ref = rest.pop(0) if emit_v else None
    side_dsts = [rest.pop(0) for _ in range(n_side)]
    proj_scr, mixed_scr, st_scr, win_ref, wout_ref = rest[:5]
    rows = n_seq * seq_rows
    j = pl.program_id(1)
    _run_side_casts(pl.program_id(0) * pl.num_programs(1) + j, side_srcs, side_dsts, rest[5:])

    @pl.when((pl.program_id(0) == 0) & (j == 0))
    def _():
        win_ref[...] = win32_ref[...].astype(BF16)
        wout_ref[...] = wout32_ref[...].astype(BF16)

    @pl.when(j == 0)
    def _():
        for s in range(n_seq):
            for hd in range(B_HEADS):
                st_scr[s * B_HEADS + hd] = s0_ref[s, hd].T

    x = x_ref[...]
    h = _rms(x, nrm_ref[...]).astype(BF16)
    n_pieces = EVEN_IN // ROW_TILE
    for n in range(n_pieces):
        cs = slice(n * ROW_TILE, (n + 1) * ROW_TILE)
        proj_scr[:, cs] = _dot(h, win_ref[:, cs])

    groups = range(A_GROUPS)
    group_cols = lambda base, g: slice(base + g * A_GDIM, base + (g + 1) * A_GDIM)
    vgs = [jax.nn.gelu(proj_scr[:, group_cols(A_DIM, g)]) for g in groups]
    means = [jnp.mean(vg, axis=-1, keepdims=True) for vg in vgs]
    vcs = [vg - mu for vg, mu in zip(vgs, means)]
    variances = [jnp.mean(vc * vc, axis=-1, keepdims=True) for vc in vcs]
    for g in groups:
        gs = group_cols(0, g)
        vn = vcs[g] * lax.rsqrt(variances[g] + EPS) * lng_ref[g] + lnb_ref[g]
        if v_ref is not None:
            v_ref[:, gs] = vn
        ug = jax.nn.gelu(proj_scr[:, gs])
        vb = vn.astype(BF16)
        for c in range(rows // gchunk):
            rs = slice(c * gchunk, (c + 1) * gchunk)
            sp = _dot(ws_ref[g], vb[rs]) + bs_ref[g]
            mixed_scr[rs, gs] = ug[rs] * sp

    q0, f0, i0, g0 = (2 * A_DIM + k * B_DIM for k in range(4))
    lb = lb_ref[...]
    fg = lb + (1.0 - lb) * jax.nn.sigmoid(proj_scr[:, f0:f0 + B_DIM])
    proj_scr[:, 0:B_DIM] = jnp.log(fg) * LOG2_E
    proj_scr[:, B_DIM:2 * B_DIM] = 1.0 - fg
    proj_scr[:, q0:q0 + B_DIM] = _silu(proj_scr[:, q0:q0 + B_DIM])
    og = og_ref[...]
    chunks_per_seq = seq_rows // CHUNK

    heads = range(B_HEADS)
    head_cols = lambda base, hd: slice(base + hd * B_HDIM, base + (hd + 1) * B_HDIM)
    chunk_rows = lambda c: slice(c * CHUNK, (c + 1) * CHUNK)

    def chunk_local(c):
        rs = chunk_rows(c)
        hi, mid, lo = _split3(proj_scr[rs, 0:B_DIM])
        gg = _dot(wcum_ref[...], jnp.concatenate([hi, mid, lo], axis=0))
        vs = [proj_scr[rs, head_cols(i0, hd)] for hd in heads]
        vts = [v.T.astype(BF16) for v in vs]
        Gs = [gg[0:CHUNK, head_cols(0, hd)] for hd in heads]
        qs = [proj_scr[rs, head_cols(q0, hd)] for hd in heads]
        ks = [proj_scr[rs, head_cols(B_DIM, hd)] for hd in heads]
        operands = []
        for hd in heads:
            for l in range(N_LEVELS + 1):
                if l < N_LEVELS:
                    e = jnp.exp2(-jnp.abs(Gs[hd] - gg[(l + 1) * CHUNK:(l + 2) * CHUNK, head_cols(0, hd)]))
                    operands.append(((qs[hd] * e).astype(BF16), (ks[hd] * e).astype(BF16)))
                else:
                    operands.append((qs[hd].astype(BF16), ks[hd].astype(BF16)))
        blocks = [_dot_nt(qe, ke) for qe, ke in operands]
        parts = []
        for hd in heads:
            att = jnp.zeros((CHUNK, CHUNK), F32)
            for l in range(N_LEVELS + 1):
                att = jnp.where(lmask_ref[l] > 0.5, blocks[hd * (N_LEVELS + 1) + l], att)
            g_end = Gs[hd][CHUNK - 1:CHUNK, :]
            kd = (ks[hd] * jnp.exp2(g_end - Gs[hd])).astype(BF16)
            parts.append((_dot(att.astype(BF16), vs[hd].astype(BF16)), (qs[hd] * jnp.exp2(Gs[hd])).astype(BF16),
                          jnp.exp2(g_end), _dot(vts[hd], kd)))
        return parts

    def chunk_state(c, parts):
        rs = chunk_rows(c)
        sidx = (c // chunks_per_seq) * B_HEADS
        outs = []
        for hd in heads:
            o_local, q_decayed, decay, increment = parts[hd]
            st = st_scr[sidx + hd]
            outs.append(o_local + _dot_nt(q_decayed, st.astype(BF16)))
            st_scr[sidx + hd] = st * decay + increment
        for hd in heads:
            o = outs[hd]
            on = o * lax.rsqrt(jnp.mean(o * o, axis=-1, keepdims=True) + EPS) * og
            gate = _silu(proj_scr[rs, head_cols(g0, hd)])
            mixed_scr[rs, head_cols(A_DIM, hd)] = on * gate

    n_chunks = rows // CHUNK
    for first in range(0, n_chunks, HGRN_STAGE_CHUNKS):
        group = range(first, min(first + HGRN_STAGE_CHUNKS, n_chunks))
        local = [chunk_local(c) for c in group]
        for c, parts in zip(group, local):
            chunk_state(c, parts)

    y_ref[...] = x + _dot(mixed_scr[...].astype(BF16), wout_ref[...])

    @pl.when(j == pl.num_programs(1) - 1)
    def _():
        for s in range(n_seq):
            for hd in range(B_HEADS):
                sout_ref[s, hd] = st_scr[s * B_HEADS + hd].T


def _even_mixer(x, s0, consts, *, n_batch, tiles_per_batch, n_seq, seq_rows, gchunk, emit_v, side_casts=()):
    rows = n_seq * seq_rows
    grid = (n_batch, tiles_per_batch)
    tile = lambda b, j: (b * tiles_per_batch + j, 0)
    whole = lambda *shape: pl.BlockSpec(shape, lambda b, j: (0,) * len(shape))
    once = lambda *shape: pl.BlockSpec(shape, lambda b, j: (0,) * len(shape), pipeline_mode=pl.Buffered(1))
    in_specs = [
        pl.BlockSpec((rows, D_MODEL), tile),
        pl.BlockSpec((n_seq, B_HEADS, B_HDIM, B_HDIM), lambda b, j: (b, 0, 0, 0)),
        whole(1, D_MODEL), once(D_MODEL, EVEN_IN), once(D_MODEL, D_MODEL),
        whole(A_GROUPS, gchunk, gchunk), whole(A_GROUPS, gchunk, LANES),
        whole(A_GROUPS, 1, A_GDIM), whole(A_GROUPS, 1, A_GDIM), whole(1, B_DIM), whole(1, B_HDIM),
        whole((N_LEVELS + 1) * CHUNK, 3 * CHUNK), whole(N_LEVELS + 1, CHUNK, CHUNK),
    ]
    args = [x, s0] + list(consts)
    out_shape = [jax.ShapeDtypeStruct(x.shape, F32),
                 jax.ShapeDtypeStruct((n_batch * n_seq, B_HEADS, B_HDIM, B_HDIM), F32)]
    out_specs = [pl.BlockSpec((rows, D_MODEL), tile),
                 pl.BlockSpec((n_seq, B_HEADS, B_HDIM, B_HDIM), lambda b, j: (b, 0, 0, 0))]
    if emit_v:
        out_shape.append(jax.ShapeDtypeStruct((x.shape[0], A_DIM), F32))
        out_specs.append(pl.BlockSpec((rows, A_DIM), tile))
    scratch = [pltpu.VMEM((rows, EVEN_IN), F32), pltpu.VMEM((rows, D_MODEL), F32),
               pltpu.VMEM((n_seq * B_HEADS, B_HDIM, B_HDIM), F32),
               pltpu.VMEM((D_MODEL, EVEN_IN), BF16), pltpu.VMEM((D_MODEL, D_MODEL), BF16)]
    side_in, side_out, side_shapes, side_scratch = _side_cast_plumbing(side_casts, n_batch * tiles_per_batch)
    in_specs += side_in
    args += list(side_casts)
    out_specs += side_out
    out_shape += side_shapes
    scratch += side_scratch
    body = functools.partial(_even_mixer_body, n_seq=n_seq, seq_rows=seq_rows, gchunk=gchunk, emit_v=emit_v,
                             n_side=len(side_casts))
    return pl.pallas_call(
        body, grid=grid, in_specs=in_specs, out_specs=out_specs, out_shape=out_shape, scratch_shapes=scratch,
        compiler_params=_cparams(2),
        name="even_mixer_s" if emit_v else "even_mixer_p")(*args)


def _memory_kv_body(mem_ref, gm_ref, wk_ref, wv_ref, kg_ref, mk_ref, mv_ref, *, n_seq, n_mem):
    m = _rms(mem_ref[...], gm_ref[0]).astype(BF16)
    kk = _dot(m, wk_ref[0].astype(BF16))
    vv = _dot(m, wv_ref[0].astype(BF16))
    kg = kg_ref[0]
    for hd in range(X_HEADS):
        hs = slice(hd * X_HDIM, (hd + 1) * X_HDIM)
        kh = _rms(kk[:, hs], kg)
        for s in range(n_seq):
            mk_ref[0, s, pl.ds(hd, n_mem, stride=X_HEADS), :] = kh[s * n_mem:(s + 1) * n_mem]
            mv_ref[0, s, pl.ds(hd, n_mem, stride=X_HEADS), :] = vv[s * n_mem:(s + 1) * n_mem, hs]


def _memory_kv(mem2d, g_mem, w_k, w_v, k_g, *, n_batch, n_mem):
    depth = w_k.shape[0]
    n_seq = ROW_TILE // n_mem
    out = jax.ShapeDtypeStruct((depth, n_batch, n_mem * X_HEADS, X_HDIM), F32)
    ospec = pl.BlockSpec((1, n_seq, n_mem * X_HEADS, X_HDIM), lambda l, t: (l, t, 0, 0))
    return pl.pallas_call(
        functools.partial(_memory_kv_body, n_seq=n_seq, n_mem=n_mem),
        grid=(depth, n_batch // n_seq),
        in_specs=[pl.BlockSpec((ROW_TILE, D_MODEL), lambda l, t: (t, 0)),
                  pl.BlockSpec((1, 1, D_MODEL), lambda l, t: (l, 0, 0)),
                  pl.BlockSpec((1, D_MODEL, X_DIM), lambda l, t: (l, 0, 0)),
                  pl.BlockSpec((1, D_MODEL, X_DIM), lambda l, t: (l, 0, 0)),
                  pl.BlockSpec((1, 1, X_HDIM), lambda l, t: (l, 0, 0))],
        out_specs=[ospec, ospec], out_shape=[out, out], compiler_params=_cparams(2),
        name="memory_kv")(mem2d, g_mem, w_k, w_v, k_g)


def _xattn_body(x_ref, mk_ref, mv_ref, nrm_ref, wq32_ref, wo32_ref, qg_ref, *rest, n_seq, seq_rows, n_side):
    side_srcs, y_ref, side_dsts = rest[:n_side], rest[n_side], rest[n_side + 1:2 * n_side + 1]
    o_scr, wq_ref, wo_ref = rest[2 * n_side + 1:2 * n_side + 4]
    _run_side_casts(pl.program_id(0), side_srcs, side_dsts, rest[2 * n_side + 4:])

    @pl.when(pl.program_id(0) == 0)
    def _():
        wq_ref[...] = wq32_ref[...].astype(BF16)
        wo_ref[...] = wo32_ref[...].astype(BF16)

    x = x_ref[...]
    h = _rms(x, nrm_ref[...]).astype(BF16)
    q = _dot(h, wq_ref[...])
    qg = qg_ref[...] * (X_HDIM ** -0.5)
    units = [(s, hd) for s in range(n_seq) for hd in range(X_HEADS)]
    rows_of = lambda s: slice(s * seq_rows, (s + 1) * seq_rows)
    lanes_of = lambda hd: slice(hd * X_HDIM, (hd + 1) * X_HDIM)
    n_mem = mk_ref.shape[1] // X_HEADS
    head_rows = lambda hd: pl.ds(hd, n_mem, stride=X_HEADS)
    qh = [_rms(q[rows_of(s), lanes_of(hd)], qg).astype(BF16) for s, hd in units]
    sc = [_dot_nt(qh[n], mk_ref[s, head_rows(hd), :].astype(BF16)) for n, (s, hd) in enumerate(units)]
    m = [jnp.max(t, axis=-1, keepdims=True) for t in sc]
    p = [jnp.exp(t - mx) for t, mx in zip(sc, m)]
    den = [jnp.sum(t, axis=-1, keepdims=True) for t in p]
    for n, (s, hd) in enumerate(units):
        o_scr[rows_of(s), lanes_of(hd)] = _dot(p[n].astype(BF16), mv_ref[s, head_rows(hd), :].astype(BF16)) / den[n]
    y_ref[...] = x + _dot(o_scr[...].astype(BF16), wo_ref[...])


def _xattn(x, mk, mv, nrm, wq, wo, qg, *, layer, steps_per_mem, n_seq, seq_rows, side_casts=()):
    rows = n_seq * seq_rows
    n_steps = x.shape[0] // rows
    mem_spec = pl.BlockSpec((None, n_seq) + mk.shape[2:], lambda t: (layer, t // steps_per_mem, 0, 0))
    whole = lambda *shape: pl.BlockSpec(shape, lambda t: (0,) * len(shape))
    once = lambda *shape: pl.BlockSpec((None,) + shape, lambda t: (layer,) + (0,) * len(shape),
                                       pipeline_mode=pl.Buffered(1))
    side_in, side_out, side_shapes, side_scratch = _side_cast_plumbing(side_casts, n_steps)
    out = pl.pallas_call(
        functools.partial(_xattn_body, n_seq=n_seq, seq_rows=seq_rows, n_side=len(side_casts)),
        grid=(n_steps,),
        in_specs=[pl.BlockSpec((rows, D_MODEL), lambda t: (t, 0)), mem_spec, mem_spec,
                  whole(1, D_MODEL), once(D_MODEL, X_DIM), once(X_DIM, D_MODEL), whole(1, X_HDIM)] + side_in,
        out_specs=[pl.BlockSpec((rows, D_MODEL), lambda t: (t, 0))] + side_out,
        out_shape=[jax.ShapeDtypeStruct(x.shape, F32)] + side_shapes,
        scratch_shapes=[pltpu.VMEM((rows, X_DIM), F32), pltpu.VMEM((D_MODEL, X_DIM), BF16),
                        pltpu.VMEM((X_DIM, D_MODEL), BF16)] + side_scratch,
        compiler_params=_cparams(1),
        name="xattn_s" if n_seq > 1 else "xattn_p")(x, mk, mv, nrm, wq, wo, qg, *side_casts)
    return out if side_casts else out[0]


def _two_source_specs(n_prompt_tiles, width=D_MODEL):
    return (pl.BlockSpec((ROW_TILE, width), lambda i, *_: (jnp.minimum(i, n_prompt_tiles - 1), 0)),
            pl.BlockSpec((ROW_TILE, width), lambda i, *_: (jnp.maximum(i - n_prompt_tiles, 0), 0)))


def _on_row_source(i, n_prompt_tiles, fn, prompt_refs, sample_refs):
    pl.when(i < n_prompt_tiles)(lambda: fn(*prompt_refs))
    pl.when(i >= n_prompt_tiles)(lambda: fn(*sample_refs))


def _swiglu_part(hb, w1, w3, w2):
    part = None
    for c0 in range(0, D_FF, FF_SUB):
        cs = slice(c0, min(c0 + FF_SUB, D_FF))
        act = (_silu(_dot(hb, w1[:, cs])) * _dot(hb, w3[:, cs])).astype(BF16)
        p = _dot(act, w2[cs, :])
        part = p if part is None else part + p
    return part


def _dense_ffn_body(xp_ref, xs_ref, nrm_ref, w1_ref, w3_ref, w2_ref, *rest, n_prompt_tiles, n_side):
    side_srcs, (yp_ref, ys_ref), side_dsts = rest[:n_side], rest[n_side:n_side + 2], rest[n_side + 2:2 * n_side + 2]
    i = pl.program_id(0)
    _run_side_casts(i, side_srcs, side_dsts, rest[2 * n_side + 2:])

    def run(x_ref, y_ref):
        x = x_ref[...]
        y_ref[...] = x + _swiglu_part(_rms(x, nrm_ref[...]).astype(BF16), w1_ref, w3_ref, w2_ref)

    _on_row_source(i, n_prompt_tiles, run, (xp_ref, yp_ref), (xs_ref, ys_ref))


def _dense_ffn(xp, xs, nrm, w1, w3, w2, side_casts):
    n_prompt_tiles = xp.shape[0] // ROW_TILE
    n_tiles = n_prompt_tiles + xs.shape[0] // ROW_TILE
    once = lambda *shape: pl.BlockSpec(shape, lambda i: (0,) * len(shape), pipeline_mode=pl.Buffered(1))
    side_in, side_out, side_shapes, side_scratch = _side_cast_plumbing(side_casts, n_tiles)
    return pl.pallas_call(
        functools.partial(_dense_ffn_body, n_prompt_tiles=n_prompt_tiles, n_side=len(side_casts)), grid=(n_tiles,),
        in_specs=[*_two_source_specs(n_prompt_tiles), pl.BlockSpec((1, D_MODEL), lambda i: (0, 0)),
                  once(D_MODEL, D_FF), once(D_MODEL, D_FF), once(D_FF, D_MODEL)] + side_in,
        out_specs=[*_two_source_specs(n_prompt_tiles)] + side_out,
        out_shape=[jax.ShapeDtypeStruct(xp.shape, F32), jax.ShapeDtypeStruct(xs.shape, F32)] + side_shapes,
        scratch_shapes=side_scratch,
        compiler_params=_cparams(1), name="dense_ffn")(xp, xs, nrm, w1, w3, w2, *side_casts)


def _grouped_ffn_body(te_ref, tv_ref, x_ref, w1_ref, w3_ref, w2_ref, y_ref):
    valid = tv_ref[pl.program_id(0)]
    half = ROW_TILE // 2
    weights = (w1_ref.at[0], w3_ref.at[0], w2_ref.at[0])

    @pl.when(valid > half)
    def _():
        y_ref[...] = _swiglu_part(x_ref[...].astype(BF16), *weights)

    @pl.when((valid > 0) & (valid <= half))
    def _():
        y_ref[0:half] = _swiglu_part(x_ref[0:half].astype(BF16), *weights)
        y_ref[half:ROW_TILE] = jnp.zeros((ROW_TILE - half, D_MODEL), F32)

    @pl.when(valid == 0)
    def _():
        y_ref[...] = jnp.zeros_like(y_ref)


def _grouped_ffn(xsorted, tile_expert, tile_valid, w1, w3, w2):
    n_tiles = xsorted.shape[0] // ROW_TILE
    grid_spec = pltpu.PrefetchScalarGridSpec(
        num_scalar_prefetch=2, grid=(n_tiles,),
        in_specs=[pl.BlockSpec((ROW_TILE, D_MODEL), lambda i, te, tv: (i, 0)),
                  pl.BlockSpec((1, D_MODEL, D_FF), lambda i, te, tv: (te[i], 0, 0)),
                  pl.BlockSpec((1, D_MODEL, D_FF), lambda i, te, tv: (te[i], 0, 0)),
                  pl.BlockSpec((1, D_FF, D_MODEL), lambda i, te, tv: (te[i], 0, 0))],
        out_specs=pl.BlockSpec((ROW_TILE, D_MODEL), lambda i, te, tv: (i, 0)))
    return pl.pallas_call(
        _grouped_ffn_body, grid_spec=grid_spec, out_shape=jax.ShapeDtypeStruct(xsorted.shape, F32),
        compiler_params=_cparams(1), name="grouped_ffn")(tile_expert, tile_valid, xsorted, w1, w3, w2)


def _rope_slab(xs, cos, sin_signed, first_half):
    rot = jnp.where(first_half, pltpu.roll(xs, LANES - C_HDIM // 2, 1), pltpu.roll(xs, C_HDIM // 2, 1))
    return xs * cos + rot * sin_signed


def _swa_body(sink_ref, x_ref, pk_ref, pv_ref, cos_ref, sin_ref, nrm_ref, win32_ref, wout32_ref, qg_ref, kg_ref,
              bd_ref, *rest, n_seq, seq_rows, past_valid, side_cast):
    rest = list(rest)
    side_src = rest.pop(0) if side_cast else None
    y_ref, ko_ref, vo_ref = rest.pop(0), rest.pop(0), rest.pop(0)
    side_dst = rest.pop(0) if side_cast else None
    k_scr, v_scr, q_scr, a_scr, win_ref, wout_ref = rest[:6]
    rows = n_seq * seq_rows
    j = pl.program_id(1)
    tail = min(WINDOW, seq_rows)
    if side_cast:
        _cast_rows_step(pl.program_id(0) * pl.num_programs(1) + j, side_src, side_dst, *rest[6:])

    @pl.when((pl.program_id(0) == 0) & (j == 0))
    def _():
        win_ref[...] = win32_ref[...].astype(BF16)
        wout_ref[...] = wout32_ref[...].astype(BF16)

    if past_valid:
        for s in range(n_seq):
            k_scr[s, 0:WINDOW] = pk_ref[s]
            v_scr[s, 0:WINDOW] = pv_ref[s]
    else:
        @pl.when(j == 0)
        def _():
            for s in range(n_seq):
                k_scr[s, 0:WINDOW] = jnp.zeros((WINDOW, C_KV_DIM), F32)
                v_scr[s, 0:WINDOW] = jnp.zeros((WINDOW, C_KV_DIM), F32)

        @pl.when(j > 0)
        def _():
            for s in range(n_seq):
                k_scr[s, 0:WINDOW] = k_scr[s, seq_rows:seq_rows + WINDOW]
                v_scr[s, 0:WINDOW] = v_scr[s, seq_rows:seq_rows + WINDOW]

    x = x_ref[...]
    h = _rms(x, nrm_ref[...]).astype(BF16)
    q_dim = C_HEADS * C_HDIM
    q = _dot(h, win_ref[:, 0:q_dim])
    k = _dot(h, win_ref[:, q_dim:q_dim + C_KV_DIM])
    v = _dot(h, win_ref[:, q_dim + C_KV_DIM:q_dim + 2 * C_KV_DIM])

    cos = cos_ref[...]
    sin_signed = sin_ref[...]
    first_half = (lax.broadcasted_iota(jnp.int32, (rows, LANES), 1) % C_HDIM) < (C_HDIM // 2)
    bd = bd_ref[...]
    qn = q * lax.rsqrt(_dot((q * q).astype(BF16), bd) + EPS) * qg_ref[...]
    kn = k * lax.rsqrt(_dot((k * k).astype(BF16), bd[0:C_KV_DIM, 0:C_KV_DIM]) + EPS) * kg_ref[...]
    scale = C_HDIM ** -0.5
    for sl in range(q_dim // LANES):
        ls = slice(sl * LANES, (sl + 1) * LANES)
        q_scr[:, ls] = _rope_slab(qn[:, ls], cos, sin_signed, first_half) * scale
    for sl in range(C_KV_DIM // LANES):
        ls = slice(sl * LANES, (sl + 1) * LANES)
        kr = _rope_slab(kn[:, ls], cos, sin_signed, first_half)
        for s in range(n_seq):
            k_scr[s, WINDOW:WINDOW + seq_rows, ls] = kr[s * seq_rows:(s + 1) * seq_rows]
    for s in range(n_seq):
        v_scr[s, WINDOW:WINDOW + seq_rows] = v[s * seq_rows:(s + 1) * seq_rows]
        ko_ref[s] = k_scr[s, WINDOW + seq_rows - tail:WINDOW + seq_rows]
        vo_ref[s] = v_scr[s, WINDOW + seq_rows - tail:WINDOW + seq_rows]

    chunks_per_seq = seq_rows // CHUNK
    n_keys = WINDOW + CHUNK
    key_idx = lax.broadcasted_iota(jnp.int32, (C_GROUP * CHUNK, n_keys), 1)
    row_grp = lax.broadcasted_iota(jnp.int32, (C_GROUP * CHUNK, 1), 0) // CHUNK
    ones_keys = jnp.ones((n_keys, C_HDIM), BF16)

    head_sinks = []
    for kvh in range(C_KV_HEADS):
        sink = jnp.zeros((C_GROUP * CHUNK, 1), F32)
        for g in range(C_GROUP):
            sink = jnp.where(row_grp == g, sink_ref[kvh * C_GROUP + g], sink)
        head_sinks.append(sink)

    def attend(chunk_ids):
        units = [(idx, kvh) for idx in chunk_ids for kvh in range(C_KV_HEADS)]
        head_lanes = lambda kvh: slice(kvh * C_HDIM, (kvh + 1) * C_HDIM)
        q_rows = lambda idx: slice(idx * CHUNK, (idx + 1) * CHUNK)
        key_rows = lambda idx: slice((idx % chunks_per_seq) * CHUNK, (idx % chunks_per_seq) * CHUNK + n_keys)
        qs = [jnp.concatenate(
            [q_scr[q_rows(idx), (kvh * C_GROUP + g) * C_HDIM:(kvh * C_GROUP + g + 1) * C_HDIM]
             for g in range(C_GROUP)], axis=0).astype(BF16) for idx, kvh in units]
        kh = [k_scr[idx // chunks_per_seq, key_rows(idx), head_lanes(kvh)].astype(BF16) for idx, kvh in units]
        vh = [v_scr[idx // chunks_per_seq, key_rows(idx), head_lanes(kvh)].astype(BF16) for idx, kvh in units]
        sc = [_dot_nt(qn, kn) for qn, kn in zip(qs, kh)]
        if not past_valid:
            sc = [jnp.where((key_idx + (idx % chunks_per_seq) * CHUNK >= WINDOW) | (j > 0), sn, -jnp.inf)
                  for sn, (idx, kvh) in zip(sc, units)]
        m = [jnp.maximum(jnp.max(sn, axis=-1, keepdims=True), head_sinks[kvh]) for sn, (idx, kvh) in zip(sc, units)]
        p = [jnp.exp(sn - mn).astype(BF16) for sn, mn in zip(sc, m)]
        den = [_dot(pn, ones_keys) + jnp.exp(head_sinks[kvh] - mn) for pn, mn, (idx, kvh) in zip(p, m, units)]
        o = [_dot(pn, vn) / dn for pn, vn, dn in zip(p, vh, den)]
        for on, (idx, kvh) in zip(o, units):
            for g in range(C_GROUP):
                hs = slice((kvh * C_GROUP + g) * C_HDIM, (kvh * C_GROUP + g + 1) * C_HDIM)
                a_scr[q_rows(idx), hs] = on[g * CHUNK:(g + 1) * CHUNK]

    n_chunks = rows // CHUNK
    for first in range(0, n_chunks, SWA_STAGE_CHUNKS):
        attend(range(first, min(first + SWA_STAGE_CHUNKS, n_chunks)))
    y_ref[...] = x + _dot(a_scr[...].astype(BF16), wout_ref[...])


def _swa(x, pk, pv, cos, sin_signed, sinks, consts, *, n_batch, tiles_per_batch, n_seq, seq_rows, past_valid,
         side_cast=None):
    rows = n_seq * seq_rows
    tail = min(WINDOW, seq_rows)
    n_str = n_batch * n_seq
    whole = lambda *shape: pl.BlockSpec(shape, lambda b, j, sk: (0,) * len(shape))
    once = lambda *shape: pl.BlockSpec(shape, lambda b, j, sk: (0,) * len(shape), pipeline_mode=pl.Buffered(1))
    in_specs = [pl.BlockSpec((rows, D_MODEL), lambda b, j, sk: (b * tiles_per_batch + j, 0)),
                pl.BlockSpec((n_seq, WINDOW, C_KV_DIM), lambda b, j, sk: (b, 0, 0)),
                pl.BlockSpec((n_seq, WINDOW, C_KV_DIM), lambda b, j, sk: (b, 0, 0)),
                pl.BlockSpec((rows, LANES), lambda b, j, sk: (j, 0)),
                pl.BlockSpec((rows, LANES), lambda b, j, sk: (j, 0)),
                whole(1, D_MODEL), once(D_MODEL, ODD_IN), once(D_MODEL, D_MODEL),
                whole(1, D_MODEL), whole(1, C_KV_DIM), whole(D_MODEL, D_MODEL)]
    args = [x, pk, pv, cos, sin_signed] + list(consts)
    kv_out = jax.ShapeDtypeStruct((n_str, tail, C_KV_DIM), F32)
    kv_spec = pl.BlockSpec((n_seq, tail, C_KV_DIM), lambda b, j, sk: (b, 0, 0))
    out_specs = [pl.BlockSpec((rows, D_MODEL), lambda b, j, sk: (b * tiles_per_batch + j, 0)), kv_spec, kv_spec]
    out_shape = [jax.ShapeDtypeStruct(x.shape, F32), kv_out, kv_out]
    scratch = [pltpu.VMEM((n_seq, WINDOW + seq_rows, C_KV_DIM), F32),
               pltpu.VMEM((n_seq, WINDOW + seq_rows, C_KV_DIM), F32),
               pltpu.VMEM((rows, D_MODEL), F32), pltpu.VMEM((rows, D_MODEL), F32),
               pltpu.VMEM((D_MODEL, ODD_IN), BF16), pltpu.VMEM((D_MODEL, D_MODEL), BF16)]
    if side_cast is not None:
        in_specs.append(pl.BlockSpec(memory_space=pl.ANY))
        args.append(side_cast)
        out_specs.append(pl.BlockSpec(memory_space=pl.ANY))
        out_shape.append(jax.ShapeDtypeStruct(side_cast.shape, BF16))
        scratch += _cast_scratch(side_cast, n_batch * tiles_per_batch)
    grid_spec = pltpu.PrefetchScalarGridSpec(
        num_scalar_prefetch=1, grid=(n_batch, tiles_per_batch), in_specs=in_specs, out_specs=out_specs,
        scratch_shapes=scratch)
    return pl.pallas_call(
        functools.partial(_swa_body, n_seq=n_seq, seq_rows=seq_rows, past_valid=past_valid,
                          side_cast=side_cast is not None),
        grid_spec=grid_spec, out_shape=out_shape,
        compiler_params=_cparams(2),
        name="swa_s" if past_valid else "swa_p")(sinks, *args)


BF16_ROWS = 16


def _cast_piece_rows(w, n_grid_steps):
    n_cast = 1 << (n_grid_steps.bit_length() - 1)
    while w.shape[0] % (n_cast * BF16_ROWS):
        n_cast //= 2
    return w.shape[0] // n_cast


def _cast_rows_step(step, src_hbm, dst_hbm, in_stage, out_stage, sems):
    n_rows = in_stage.shape[1]
    n_cast = src_hbm.shape[0] // n_rows
    slot = step % 2

    def read(s, sl):
        return pltpu.make_async_copy(src_hbm.at[pl.ds(pl.multiple_of(s * n_rows, 16), n_rows)], in_stage.at[sl],
                                     sems.at[sl])

    def write(s, sl):
        return pltpu.make_async_copy(out_stage.at[sl], dst_hbm.at[pl.ds(pl.multiple_of(s * n_rows, 16), n_rows)],
                                     sems.at[2 + sl])

    @pl.when(step == 0)
    def _():
        read(0, 0).start(priority=1)

    @pl.when(step < n_cast)
    def _():
        @pl.when(step + 1 < n_cast)
        def _():
            read(step + 1, 1 - slot).start(priority=1)

        read(step, slot).wait()

        @pl.when(step >= 2)
        def _():
            write(step - 2, slot).wait()

        out_stage[slot] = in_stage[slot].astype(BF16)
        write(step, slot).start(priority=1)

        @pl.when(step == n_cast - 1)
        def _():
            write(step, slot).wait()
            if n_cast > 1:
                write(step - 1, 1 - slot).wait()


def _cast_scratch(w, n_grid_steps):
    rows = _cast_piece_rows(w, n_grid_steps)
    return [pltpu.VMEM((2, rows, w.shape[1]), F32), pltpu.VMEM((2, rows, w.shape[1]), BF16),
            pltpu.SemaphoreType.DMA((4,))]


def _side_cast_plumbing(side_casts, n_grid_steps):
    any_space = pl.BlockSpec(memory_space=pl.ANY)
    scratch = [s for w in side_casts for s in _cast_scratch(w, n_grid_steps)]
    return ([any_space] * len(side_casts), [any_space] * len(side_casts),
            [jax.ShapeDtypeStruct(w.shape, BF16) for w in side_casts], scratch)


def _run_side_casts(step, srcs, dsts, scratch):
    for n, (src, dst) in enumerate(zip(srcs, dsts)):
        _cast_rows_step(step, src, dst, *scratch[3 * n:3 * n + 3])


def _router_body(xp_ref, xs_ref, nrm_ref, rboth_ref, lstrict_ref, su_ref, info_ref, cnt_ref, *, n_prompt_tiles):
    def run(x_ref):
        h = _rms(x_ref[...], nrm_ref[...])
        hi = h.astype(BF16)
        lo = (h - hi.astype(F32)).astype(BF16)
        both = _dot(hi, rboth_ref[...])
        logits = both[:, 0:LANES] + both[:, LANES:2 * LANES] + _dot(lo, rboth_ref[:, 0:LANES])
        lane = lax.broadcasted_iota(jnp.int32, logits.shape, 1).astype(F32)
        logits = jnp.where(lane < N_EXPERTS, logits, -jnp.inf)
        l1 = jnp.max(logits, axis=-1, keepdims=True)
        i1 = jnp.min(jnp.where(logits == l1, lane, float(LANES)), axis=-1, keepdims=True)
        rest = jnp.where(lane == i1, -jnp.inf, logits)
        l2 = jnp.max(rest, axis=-1, keepdims=True)
        i2 = jnp.min(jnp.where(rest == l2, lane, float(LANES)), axis=-1, keepdims=True)
        e = jnp.exp(l2 - l1)
        g1 = 1.0 / (1.0 + e)
        g2 = e * g1
        oh1 = (lane == i1).astype(F32)
        oh2 = (lane == i2).astype(F32)
        cnt1 = jnp.sum(oh1, axis=0, keepdims=True)
        cnt2 = jnp.sum(oh2, axis=0, keepdims=True)
        seg = jnp.broadcast_to(jnp.ceil((cnt1 + cnt2) * (1.0 / GROUP)), (GROUP, LANES))
        off = _dot(seg.astype(BF16), su_ref[...])[0:1] * GROUP
        c1 = _dot(lstrict_ref[...], oh1.astype(BF16)) + off
        c2 = _dot(lstrict_ref[...], oh2.astype(BF16)) + off + cnt1
        d1 = jnp.sum(jnp.where(lane == i1, c1, 0.0), axis=-1, keepdims=True)
        d2 = jnp.sum(jnp.where(lane == i2, c2, 0.0), axis=-1, keepdims=True)
        info = jnp.zeros(logits.shape, F32)
        for n, col in enumerate([d1, d2, g1, g2]):
            info = jnp.where(lane == n, col, info)
        info_ref[...] = info
        cnt_ref[...] = seg * GROUP

    _on_row_source(pl.program_id(0), n_prompt_tiles, run, (xp_ref,), (xs_ref,))


def _router(xp, xs, nrm, rboth, lstrict, su):
    n_prompt_tiles = xp.shape[0] // ROW_TILE
    n_tiles = n_prompt_tiles + xs.shape[0] // ROW_TILE
    whole = lambda *shape: pl.BlockSpec(shape, lambda i: (0,) * len(shape))
    return pl.pallas_call(
        functools.partial(_router_body, n_prompt_tiles=n_prompt_tiles), grid=(n_tiles,),
        in_specs=[*_two_source_specs(n_prompt_tiles), whole(1, D_MODEL), whole(D_MODEL, 2 * LANES),
                  whole(ROW_TILE, ROW_TILE), whole(LANES, LANES)],
        out_specs=[pl.BlockSpec((ROW_TILE, LANES), lambda i: (i, 0)), pl.BlockSpec((GROUP, LANES), lambda i: (i, 0))],
        out_shape=[jax.ShapeDtypeStruct((n_tiles * ROW_TILE, LANES), F32),
                   jax.ShapeDtypeStruct((n_tiles * GROUP, LANES), F32)],
        compiler_params=_cparams(1), name="router")(xp, xs, nrm, rboth, lstrict, su)


BIG_ROWS = 8 * GROUP
SEG_FIELDS = 4
TILE_FIELDS = 2
TAIL_FIELDS = 3


def _rows_copy(src_ref, src_row, dst_ref, dst_row, n_rows, sem):
    return pltpu.make_async_copy(src_ref.at[pl.ds(pl.multiple_of(src_row, GROUP), n_rows)],
                                 dst_ref.at[pl.ds(pl.multiple_of(dst_row, GROUP), n_rows)], sem)


def _start_pieces(src_ref, src0, dst_ref, dst0, n_big, n_small, sem):
    def big(g, carry):
        _rows_copy(src_ref, src0 + g * BIG_ROWS, dst_ref, dst0 + g * BIG_ROWS, BIG_ROWS, sem).start(priority=0)
        return carry

    lax.fori_loop(0, n_big, big, 0)
    done = n_big * BIG_ROWS

    def small(g, carry):
        _rows_copy(src_ref, src0 + done + g * GROUP, dst_ref, dst0 + done + g * GROUP, GROUP, sem).start(priority=1)
        return carry

    lax.fori_loop(0, n_small, small, 0)


def _wait_pieces(n_big, n_small, src_ref, dst_ref, sem):
    def big(g, carry):
        _rows_copy(src_ref, 0, dst_ref, 0, BIG_ROWS, sem).wait()
        return carry

    lax.fori_loop(0, n_big, big, 0)

    def small(g, carry):
        _rows_copy(src_ref, 0, dst_ref, 0, GROUP, sem).wait()
        return carry

    lax.fori_loop(0, n_small, small, 0)


def _segment_table(seg, n_sorted_tiles):
    pieces = lambda rows: (rows // BIG_ROWS, (rows % BIG_ROWS) // GROUP)
    local_off = jnp.cumsum(seg, axis=1) - seg
    expert_rows = jnp.sum(seg, axis=0)
    expert_pad = ((expert_rows + ROW_TILE - 1) // ROW_TILE) * ROW_TILE
    expert_end = jnp.cumsum(expert_pad)
    expert_off = expert_end - expert_pad
    seg_start = expert_off[None, :] + jnp.cumsum(seg, axis=0) - seg
    n_used = (expert_end[-1] // ROW_TILE).reshape(1)
    tile_ids = jnp.arange(n_sorted_tiles, dtype=jnp.int32)
    tile_expert = jnp.minimum(jnp.sum(tile_ids[:, None] >= (expert_end // ROW_TILE)[None, :], axis=1), N_EXPERTS - 1)
    tile_rows0 = (tile_ids * ROW_TILE)[:, None]
    overlap = (jnp.minimum((expert_off + expert_rows)[None, :], tile_rows0 + ROW_TILE)
               - jnp.maximum(expert_off[None, :], tile_rows0))
    tile_valid = jnp.sum(jnp.maximum(overlap, 0), axis=1)
    seg_big, seg_small = pieces(seg)
    tail_big, tail_small = pieces(expert_pad - expert_rows)
    table = jnp.concatenate([
        jnp.stack([seg_start, seg_big, seg_small, local_off], axis=-1).reshape(-1),
        jnp.stack([jnp.sum(seg_big, axis=1), jnp.sum(seg_small, axis=1)], axis=-1).reshape(-1),
        jnp.stack([expert_off + expert_rows, tail_big, tail_small], axis=-1).reshape(-1),
        n_used])
    return table.astype(jnp.int32), tile_expert.astype(jnp.int32), tile_valid.astype(jnp.int32)


def _table_sections(n_tiles):
    tiles0 = SEG_FIELDS * n_tiles * N_EXPERTS
    tails0 = tiles0 + TILE_FIELDS * n_tiles
    return tiles0, tails0, tails0 + TAIL_FIELDS * N_EXPERTS


def _dispatch_body(tab_ref, info_ref, xp_ref, xs_ref, nrm_ref, sel_ref, out_ref, loc_scr, z_scr, sems, *,
                   n_prompt_tiles, n_tiles, n_sorted_tiles):
    i = pl.program_id(0)
    slot = i % 2
    tiles0, tails0, used0 = _table_sections(n_tiles)

    def wait_tile(t, s):
        _wait_pieces(tab_ref[tiles0 + TILE_FIELDS * t], tab_ref[tiles0 + TILE_FIELDS * t + 1], loc_scr.at[s], out_ref,
                     sems.at[s])

    @pl.when(i >= 2)
    def _():
        wait_tile(i - 2, slot)

    def run(x_ref):
        hb = _rms(x_ref[...], nrm_ref[...]).astype(BF16)
        info = info_ref[...]
        lane = lax.broadcasted_iota(jnp.int32, info.shape, 1)
        dest = jnp.where(lane < TOP_K, info, 0.0)
        drow = sum(_dot_nt(sel_ref[...], part) for part in _split3(dest))
        r = lax.broadcasted_iota(jnp.int32, (LOCAL_ROWS, ROW_TILE), 0).astype(F32)
        perm = jnp.where(r == drow[0:1], 1.0, jnp.where(r == drow[1:2], 1.0, 0.0)).astype(BF16)
        loc_scr[slot] = _dot(perm, hb)

    _on_row_source(i, n_prompt_tiles, run, (xp_ref,), (xs_ref,))

    for e in range(N_EXPERTS):
        base = SEG_FIELDS * (i * N_EXPERTS + e)
        _start_pieces(loc_scr.at[slot], tab_ref[base + 3], out_ref, tab_ref[base], tab_ref[base + 1],
                      tab_ref[base + 2], sems.at[slot])

    @pl.when(i == n_tiles - 1)
    def _():
        wait_tile(i, slot)
        if n_tiles > 1:
            wait_tile(i - 1, 1 - slot)
        z_scr[...] = jnp.zeros_like(z_scr)
        for e in range(N_EXPERTS):
            base = tails0 + TAIL_FIELDS * e
            _start_pieces(z_scr, 0, out_ref, tab_ref[base], tab_ref[base + 1], tab_ref[base + 2], sems.at[2])
            _wait_pieces(tab_ref[base + 1], tab_ref[base + 2], z_scr, out_ref, sems.at[2])

        def zero_tile(t, carry):
            cp = _rows_copy(z_scr, 0, out_ref, t * ROW_TILE, ROW_TILE, sems.at[2])
            cp.start()
            cp.wait()
            return carry

        lax.fori_loop(tab_ref[used0], n_sorted_tiles, zero_tile, 0)


def _dispatch(table, info, xp, xs, nrm, sel, n_sorted_tiles):
    n_prompt_tiles = xp.shape[0] // ROW_TILE
    n_tiles = n_prompt_tiles + xs.shape[0] // ROW_TILE
    whole = lambda *shape: pl.BlockSpec(shape, lambda i, tab: (0,) * len(shape))
    grid_spec = pltpu.PrefetchScalarGridSpec(
        num_scalar_prefetch=1, grid=(n_tiles,),
        in_specs=[pl.BlockSpec((ROW_TILE, LANES), lambda i, tab: (i, 0)), *_two_source_specs(n_prompt_tiles),
                  whole(1, D_MODEL), whole(GROUP, LANES)],
        out_specs=pl.BlockSpec(memory_space=pl.ANY),
        scratch_shapes=[pltpu.VMEM((2, LOCAL_ROWS, D_MODEL), F32), pltpu.VMEM((ROW_TILE, D_MODEL), F32),
                        pltpu.SemaphoreType.DMA((3,))])
    return pl.pallas_call(
        functools.partial(_dispatch_body, n_prompt_tiles=n_prompt_tiles, n_tiles=n_tiles,
                          n_sorted_tiles=n_sorted_tiles),
        grid_spec=grid_spec, out_shape=jax.ShapeDtypeStruct((n_sorted_tiles * ROW_TILE, D_MODEL), F32),
        compiler_params=_cparams(1), name="moe_dispatch")(table, info, xp, xs, nrm, sel)


def _combine_body(tab_ref, info_ref, xp_ref, xs_ref, e_ref, yp_ref, ys_ref, loc_scr, sems, *, n_prompt_tiles,
                  n_tiles):
    i = pl.program_id(0)
    slot = i % 2
    tiles0, _, _ = _table_sections(n_tiles)

    def fetch(t, s):
        for e in range(N_EXPERTS):
            base = SEG_FIELDS * (t * N_EXPERTS + e)
            _start_pieces(e_ref, tab_ref[base], loc_scr.at[s], tab_ref[base + 3], tab_ref[base + 1],
                          tab_ref[base + 2], sems.at[s])

    @pl.when(i == 0)
    def _():
        loc_scr[...] = jnp.zeros_like(loc_scr)
        fetch(0, 0)

    @pl.when(i + 1 < n_tiles)
    def _():
        fetch(i + 1, 1 - slot)

    _wait_pieces(tab_ref[tiles0 + TILE_FIELDS * i], tab_ref[tiles0 + TILE_FIELDS * i + 1], e_ref, loc_scr.at[slot],
                 sems.at[slot])

    def run(x_ref, y_ref):
        info = info_ref[...]
        eb = loc_scr[slot].astype(BF16)
        lane = lax.broadcasted_iota(jnp.int32, (ROW_TILE, LOCAL_ROWS), 1).astype(F32)
        pick = jnp.zeros((ROW_TILE, LOCAL_ROWS), F32)
        for k in range(TOP_K):
            pick = jnp.where(lane == info[:, k:k + 1], info[:, TOP_K + k:TOP_K + k + 1], pick)
        y_ref[...] = x_ref[...] + _dot(pick.astype(BF16), eb)

    _on_row_source(i, n_prompt_tiles, run, (xp_ref, yp_ref), (xs_ref, ys_ref))


def _combine(table, info, xp, xs, esorted):
    n_prompt_tiles = xp.shape[0] // ROW_TILE
    n_tiles = n_prompt_tiles + xs.shape[0] // ROW_TILE
    grid_spec = pltpu.PrefetchScalarGridSpec(
        num_scalar_prefetch=1, grid=(n_tiles,),
        in_specs=[pl.BlockSpec((ROW_TILE, LANES), lambda i, tab: (i, 0)), *_two_source_specs(n_prompt_tiles),
                  pl.BlockSpec(memory_space=pl.ANY)],
        out_specs=list(_two_source_specs(n_prompt_tiles)),
        scratch_shapes=[pltpu.VMEM((2, LOCAL_ROWS, D_MODEL), F32), pltpu.SemaphoreType.DMA((2,))])
    return pl.pallas_call(
        functools.partial(_combine_body, n_prompt_tiles=n_prompt_tiles, n_tiles=n_tiles),
        grid_spec=grid_spec,
        out_shape=[jax.ShapeDtypeStruct(xp.shape, F32), jax.ShapeDtypeStruct(xs.shape, F32)],
        compiler_params=_cparams(1), name="moe_combine")(table, info, xp, xs, esorted)


def _rope_tables(pos):
    half = C_HDIM // 2
    inv = 1.0 / (ROPE_THETA ** (jnp.arange(half, dtype=F32) / half))
    ang = pos.astype(F32)[:, None] * inv[None, :]
    cos = jnp.cos(ang)
    sin = jnp.sin(ang)
    reps = LANES // C_HDIM
    return jnp.tile(jnp.concatenate([cos, cos], axis=-1), (1, reps)), jnp.tile(jnp.concatenate([-sin, sin], axis=-1),
                                                                                 (1, reps))


def kernel(x_prompt, x_sample, mem_prompt, cache_mem_k, cache_mem_v, state_hgrn, cache_swa_k, cache_swa_v, norm_mix, norm_xattn, norm_ffn, even_w_in, even_w_out, gmlp_w_s, gmlp_b_s, gmlp_ln_g, gmlp_ln_b, hgrn_lb_logits, hgrn_out_norm, attn_w_in, attn_w_out, attn_q_norm, attn_k_norm, attn_sinks, xattn_mem_norm, xattn_w_q, xattn_w_k, xattn_w_v, xattn_w_o, xattn_q_norm, xattn_k_norm, ffn_w1, ffn_w3, ffn_w2, moe_router, moe_w1, moe_w3, moe_w2):
    n_batch, seq, d = x_prompt.shape
    dec_batch, dec_seq, _ = x_sample.shape
    n_mem = mem_prompt.shape[1]
    depth = norm_mix.shape[0]
    past_len = PAST_LEN
    assert d == D_MODEL and depth == 2 and seq % ROW_TILE == 0 and dec_batch * dec_seq == ROW_TILE
    assert dec_seq == CHUNK and ROW_TILE % n_mem == 0 and cache_swa_k.shape[2] == WINDOW
    assert even_w_in.shape[-1] == EVEN_IN and attn_w_in.shape[-1] == ODD_IN and ffn_w1.shape[-1] == D_FF
    assert moe_w1.shape[1] == N_EXPERTS
    n_prompt_rows = n_batch * seq
    n_rows = n_prompt_rows + dec_batch * dec_seq
    n_prompt_tiles = n_prompt_rows // ROW_TILE
    tiles_per_batch = seq // ROW_TILE
    row = lambda g: g.reshape(1, -1).astype(F32)

    lb_all = jnp.cumsum(jax.nn.softmax(hgrn_lb_logits.astype(F32), axis=0), axis=0)
    wcum = jnp.asarray(_cumsum_matrix(), BF16)
    lmask = jnp.asarray(_level_masks(), F32)

    def even_consts(n):
        tril = jnp.tril(jnp.ones((n, n), bool))
        ws = jnp.where(tril[None], gmlp_w_s[0, :, :n, :n], 0.0).astype(BF16)
        bs = jnp.broadcast_to(gmlp_b_s[0, :, :n, None], (A_GROUPS, n, LANES)).astype(F32)
        return [row(norm_mix[0]), even_w_in[0], even_w_out[0], ws, bs,
                gmlp_ln_g[0].reshape(A_GROUPS, 1, A_GDIM), gmlp_ln_b[0].reshape(A_GROUPS, 1, A_GDIM),
                row(lb_all[0]), row(hgrn_out_norm[0]), wcum, lmask]

    rows2d = lambda w: w.reshape(-1, w.shape[-1])
    xp, hgrn_p, w1b, ffn_w1b, ffn_w3b, ffn_w2b = _even_mixer(
        x_prompt.reshape(n_prompt_rows, d), jnp.zeros((n_batch, B_HEADS, B_HDIM, B_HDIM), F32),
        even_consts(A_CHUNK), n_batch=n_batch, tiles_per_batch=tiles_per_batch, n_seq=1, seq_rows=ROW_TILE,
        gchunk=A_CHUNK, emit_v=False, side_casts=(rows2d(moe_w1[0]), ffn_w1[0], ffn_w3[0], ffn_w2[0]))
    xs, hgrn_s, gmlp_v = _even_mixer(
        x_sample.reshape(ROW_TILE, d), state_hgrn[0], even_consts(min(A_CHUNK, dec_seq)), n_batch=1,
        tiles_per_batch=1, n_seq=dec_batch, seq_rows=dec_seq, gchunk=min(A_CHUNK, dec_seq), emit_v=True)

    mem_k, mem_v = _memory_kv(mem_prompt.reshape(n_batch * n_mem, d), xattn_mem_norm.reshape(depth, 1, d),
                              xattn_w_k, xattn_w_v, xattn_k_norm.reshape(depth, 1, X_HDIM),
                              n_batch=n_batch, n_mem=n_mem)

    def cross_attention(xp, xs, l, side_casts=()):
        consts = (row(norm_xattn[l]), xattn_w_q, xattn_w_o, row(xattn_q_norm[l]))
        xp = _xattn(xp, mem_k, mem_v, *consts, layer=l, steps_per_mem=tiles_per_batch, n_seq=1, seq_rows=ROW_TILE,
                    side_casts=side_casts)
        xs = _xattn(xs, cache_k, cache_v, *consts, layer=l, steps_per_mem=1, n_seq=dec_batch, seq_rows=dec_seq)
        return xp, xs

    cache_k = cache_mem_k.reshape(depth, dec_batch, n_mem * X_HEADS, X_HDIM)
    cache_v = cache_mem_v.reshape(depth, dec_batch, n_mem * X_HEADS, X_HDIM)

    xp, xs = cross_attention(xp, xs, 0)
    xp, xs, w2b, attn_w_in_b, attn_w_out_b = _dense_ffn(xp, xs, row(norm_ffn[0]), ffn_w1b, ffn_w3b, ffn_w2b,
                                                        (rows2d(moe_w2[0]), attn_w_in[0], attn_w_out[0]))

    reps = D_MODEL // C_HDIM
    bd = jnp.asarray(np.kron(np.eye(reps, dtype=np.float32), np.full((C_HDIM, C_HDIM), 1.0 / C_HDIM, np.float32)),
                     BF16)
    swa_consts = [row(norm_mix[1]), attn_w_in_b, attn_w_out_b, row(jnp.tile(attn_q_norm[0], C_HEADS)),
                  row(jnp.tile(attn_k_norm[0], C_KV_HEADS)), bd]
    sinks = attn_sinks[0].astype(F32)
    cos_p, sin_p = _rope_tables(jnp.arange(seq, dtype=jnp.int32))
    cos_s, sin_s = _rope_tables(past_len + jnp.arange(dec_seq, dtype=jnp.int32))
    no_past = jnp.zeros((n_batch, WINDOW, C_KV_DIM), F32)
    xp, swk_p, swv_p, w3b = _swa(xp, no_past, no_past, cos_p, sin_p, sinks, swa_consts, n_batch=n_batch,
                                 tiles_per_batch=tiles_per_batch, n_seq=1, seq_rows=ROW_TILE, past_valid=False,
                                 side_cast=rows2d(moe_w3[0]))
    xs, swk_s, swv_s = _swa(xs, cache_swa_k[0].reshape(dec_batch, WINDOW, C_KV_DIM),
                            cache_swa_v[0].reshape(dec_batch, WINDOW, C_KV_DIM),
                            jnp.tile(cos_s, (dec_batch, 1)), jnp.tile(sin_s, (dec_batch, 1)), sinks, swa_consts,
                            n_batch=1, tiles_per_batch=1, n_seq=dec_batch, seq_rows=dec_seq, past_valid=True)
    xp, xs = cross_attention(xp, xs, 1)

    n_tiles = n_rows // ROW_TILE
    router_w = jnp.zeros((d, LANES), F32).at[:, :N_EXPERTS].set(moe_router[0].astype(F32))
    rhi = router_w.astype(BF16)
    rlo = (router_w - rhi.astype(F32)).astype(BF16)
    lstrict = jnp.asarray(np.tril(np.ones((ROW_TILE, ROW_TILE), np.float32), -1), BF16)
    su = jnp.asarray(np.triu(np.ones((LANES, LANES), np.float32), 1), BF16)
    sel = jnp.asarray(np.eye(GROUP, LANES, dtype=np.float32), BF16)
    nrm_ffn = row(norm_ffn[1])
    info, seg = _router(xp, xs, nrm_ffn, jnp.concatenate([rhi, rlo], axis=1), lstrict, su)
    seg = seg.reshape(n_tiles, GROUP, LANES)[:, 0, :N_EXPERTS].astype(jnp.int32)
    n_sorted_tiles = -(-(TOP_K * n_rows + n_tiles * N_EXPERTS * (GROUP - 1) + N_EXPERTS * (ROW_TILE - 1)) // ROW_TILE)
    table, tile_expert, tile_valid = _segment_table(seg, n_sorted_tiles)
    hsorted = _dispatch(table, info, xp, xs, nrm_ffn, sel, n_sorted_tiles)
    esorted = _grouped_ffn(hsorted, tile_expert, tile_valid, w1b.reshape(moe_w1.shape[1:]),
                           w3b.reshape(moe_w3.shape[1:]), w2b.reshape(moe_w2.shape[1:]))
    y_prompt, y_sample = _combine(table, info, xp, xs, esorted)

    n_even = state_hgrn.shape[0]
    n_odd = cache_swa_k.shape[0]
    return (y_prompt.reshape(n_batch, seq, d), y_sample.reshape(dec_batch, dec_seq, d),
            mem_k.reshape(depth, n_batch, n_mem, X_HEADS, X_HDIM), mem_v.reshape(depth, n_batch, n_mem, X_HEADS, X_HDIM),
            hgrn_p.reshape(n_even, n_batch, B_HEADS, B_HDIM, B_HDIM),
            gmlp_v.reshape(n_even, dec_batch, dec_seq, A_GROUPS, A_GDIM),
            hgrn_s.reshape(n_even, dec_batch, B_HEADS, B_HDIM, B_HDIM),
            swk_p.reshape(n_odd, n_batch, WINDOW, C_KV_HEADS, C_HDIM), swv_p.reshape(n_odd, n_batch, WINDOW, C_KV_HEADS, C_HDIM),
            swk_s.reshape(n_odd, dec_batch, dec_seq, C_KV_HEADS, C_HDIM), swv_s.reshape(n_odd, dec_batch, dec_seq, C_KV_HEADS, C_HDIM))
```

```python
import functools

import numpy as np
import jax
import jax.numpy as jnp
from jax import lax
from jax.experimental import pallas as pl
from jax.experimental.pallas import tpu as pltpu

F32 = jnp.float32
BF16 = jnp.bfloat16

D_MODEL = 1024
EPS = 1e-6
LOG2_E = 1.4426950408889634
CHUNK = 64
A_GROUPS = 4
A_DIM = D_MODEL // 2
A_GDIM = A_DIM // A_GROUPS
A_CHUNK = 128
B_HEADS = 4
B_DIM = D_MODEL // 2
B_HDIM = B_DIM // B_HEADS
EVEN_IN = 2 * A_DIM + 4 * B_DIM
C_HEADS = 16
C_KV_HEADS = 4
C_HDIM = D_MODEL // C_HEADS
C_GROUP = C_HEADS // C_KV_HEADS
C_KV_DIM = C_KV_HEADS * C_HDIM
WINDOW = 128
ROPE_THETA = 10000.0
PAST_LEN = 4096
ODD_IN = (C_HEADS + 2 * C_KV_HEADS) * C_HDIM
X_HEADS = 4
X_HDIM = 128
X_DIM = X_HEADS * X_HDIM
D_FF = 2816
N_EXPERTS = 8
TOP_K = 2

LANES = 128
ROW_TILE = 512
FF_SUB = 512
N_LEVELS = 6
HGRN_STAGE_CHUNKS = 2
SWA_STAGE_CHUNKS = 2
GROUP = 8
LOCAL_ROWS = -(-(TOP_K * ROW_TILE + N_EXPERTS * (GROUP - 1)) // LANES) * LANES
VMEM_LIMIT = 58 * 1024 * 1024


def _cparams(n_axes):
    return pltpu.CompilerParams(dimension_semantics=("arbitrary",) * n_axes, vmem_limit_bytes=VMEM_LIMIT)


def _dot(a, b):
    return jnp.dot(a, b, preferred_element_type=F32)


def _dot_nt(a, b):
    return lax.dot_general(a, b, (((1,), (1,)), ((), ())), preferred_element_type=F32)


def _rms(x, g):
    return x * lax.rsqrt(jnp.mean(x * x, axis=-1, keepdims=True) + EPS) * g


def _silu(x):
    return x * (1.0 / (1.0 + jnp.exp(-x)))


def _split3(x):
    hi = x.astype(BF16)
    r1 = x - hi.astype(F32)
    mid = r1.astype(BF16)
    lo = (r1 - mid.astype(F32)).astype(BF16)
    return hi, mid, lo


def _cumsum_matrix():
    r = np.arange(CHUNK)
    s = np.arange(CHUNK)
    blocks = [(s[None, :] <= r[:, None])]
    for l in range(N_LEVELS):
        h = 1 << l
        ref = (r & ~(2 * h - 1)) + h - 1
        blocks.append(s[None, :] <= ref[:, None])
    w = np.concatenate(blocks, axis=0).astype(np.float32)
    return np.concatenate([w, w, w], axis=1)


def _level_masks():
    t = np.arange(CHUNK)[:, None]
    s = np.arange(CHUNK)[None, :]
    masks = []
    for l in range(N_LEVELS):
        masks.append(((t >> (l + 1)) == (s >> (l + 1))) & (((t >> l) & 1) == 1) & (((s >> l) & 1) == 0))
    masks.append(t == s)
    return np.stack(masks).astype(np.float32)


def _even_mixer_body(x_ref, s0_ref, nrm_ref, win32_ref, wout32_ref, ws_ref, bs_ref, lng_ref, lnb_ref, lb_ref, og_ref,
                     wcum_ref, lmask_ref, *rest, n_seq, seq_rows, gchunk, emit_v, n_side):
    rest = list(rest)
    side_srcs = [rest.pop(0) for _ in range(n_side)]
    y_ref, sout_ref = rest.pop(0), rest.pop(0)
    v_ref = rest.pop(0) if emit_v else None
    side_dsts = [rest.pop(0) for _ in range(n_side)]
    proj_scr, mixed_scr, st_scr, win_ref, wout_ref = rest[:5]
    rows = n_seq * seq_rows
    j = pl.program_id(1)

    @pl.when((pl.program_id(0) == 0) & (j == 0))
    def _():
        win_ref[...] = win32_ref[...].astype(BF16)
        wout_ref[...] = wout32_ref[...].astype(BF16)

    @pl.when(j == 0)
    def _():
        for s in range(n_seq):
            for hd in range(B_HEADS):
                st_scr[s * B_HEADS + hd] = s0_ref[s, hd].T

    x = x_ref[...]
    h = _rms(x, nrm_ref[...]).astype(BF16)
    n_pieces = EVEN_IN // ROW_TILE
    for n in range(n_pieces):
        cs = slice(n * ROW_TILE, (n + 1) * ROW_TILE)
        proj_scr[:, cs] = _dot(h, win_ref[:, cs])

    groups = range(A_GROUPS)
    group_cols = lambda base, g: slice(base + g * A_GDIM, base + (g + 1) * A_GDIM)
    vgs = [jax.nn.gelu(proj_scr[:, group_cols(A_DIM, g)]) for g in groups]
    means = [jnp.mean(vg, axis=-1, keepdims=True) for vg in vgs]
    vcs = [vg - mu for vg, mu in zip(vgs, means)]
    variances = [jnp.mean(vc * vc, axis=-1, keepdims=True) for vc in vcs]
    for g in groups:
        gs = group_cols(0, g)
        vn = vcs[g] * lax.rsqrt(variances[g] + EPS) * lng_ref[g] + lnb_ref[g]
        if v_ref is not None:
            v_ref[:, gs] = vn
        ug = jax.nn.gelu(proj_scr[:, gs])
        vb = vn.astype(BF16)
        for c in range(rows // gchunk):
            rs = slice(c * gchunk, (c + 1) * gchunk)
            sp = _dot(ws_ref[g], vb[rs]) + bs_ref[g]
            mixed_scr[rs, gs] = ug[rs] * sp

    q0, f0, i0, g0 = (2 * A_DIM + k * B_DIM for k in range(4))
    lb = lb_ref[...]
    fg = lb + (1.0 - lb) * jax.nn.sigmoid(proj_scr[:, f0:f0 + B_DIM])
    proj_scr[:, 0:B_DIM] = jnp.log(fg) * LOG2_E
    proj_scr[:, B_DIM:2 * B_DIM] = 1.0 - fg
    proj_scr[:, q0:q0 + B_DIM] = _silu(proj_scr[:, q0:q0 + B_DIM])
    og = og_ref[...]
    chunks_per_seq = seq_rows // CHUNK

    heads = range(B_HEADS)
    head_cols = lambda base, hd: slice(base + hd * B_HDIM, base + (hd + 1) * B_HDIM)
    chunk_rows = lambda c: slice(c * CHUNK, (c + 1) * CHUNK)

    def chunk_local(c):
        rs = chunk_rows(c)
        hi, mid, lo = _split3(proj_scr[rs, 0:B_DIM])
        gg = _dot(wcum_ref[...], jnp.concatenate([hi, mid, lo], axis=0))
        vs = [proj_scr[rs, head_cols(i0, hd)] for hd in heads]
        vts = [v.T.astype(BF16) for v in vs]
        Gs = [gg[0:CHUNK, head_cols(0, hd)] for hd in heads]
        qs = [proj_scr[rs, head_cols(q0, hd)] for hd in heads]
        ks = [proj_scr[rs, head_cols(B_DIM, hd)] for hd in heads]
        operands = []
        for hd in heads:
            for l in range(N_LEVELS + 1):
                if l < N_LEVELS:
                    e = jnp.exp2(-jnp.abs(Gs[hd] - gg[(l + 1) * CHUNK:(l + 2) * CHUNK, head_cols(0, hd)]))
                    operands.append(((qs[hd] * e).astype(BF16), (ks[hd] * e).astype(BF16)))
                else:
                    operands.append((qs[hd].astype(BF16), ks[hd].astype(BF16)))
        blocks = [_dot_nt(qe, ke) for qe, ke in operands]
        parts = []
        for hd in heads:
            att = jnp.zeros((CHUNK, CHUNK), F32)
            for l in range(N_LEVELS + 1):
                att = jnp.where(lmask_ref[l] > 0.5, blocks[hd * (N_LEVELS + 1) + l], att)
            g_end = Gs[hd][CHUNK - 1:CHUNK, :]
            kd = (ks[hd] * jnp.exp2(g_end - Gs[hd])).astype(BF16)
            parts.append((_dot(att.astype(BF16), vs[hd].astype(BF16)), (qs[hd] * jnp.exp2(Gs[hd])).astype(BF16),
                          jnp.exp2(g_end), _dot(vts[hd], kd)))
        return parts

    def chunk_state(c, parts):
        rs = chunk_rows(c)
        sidx = (c // chunks_per_seq) * B_HEADS
        outs = []
        for hd in heads:
            o_local, q_decayed, decay, increment = parts[hd]
            st = st_scr[sidx + hd]
            outs.append(o_local + _dot_nt(q_decayed, st.astype(BF16)))
            st_scr[sidx + hd] = st * decay + increment
        for hd in heads:
            o = outs[hd]
            on = o * lax.rsqrt(jnp.mean(o * o, axis=-1, keepdims=True) + EPS) * og
            gate = _silu(proj_scr[rs, head_cols(g0, hd)])
            mixed_scr[rs, head_cols(A_DIM, hd)] = on * gate

    n_chunks = rows // CHUNK
    for first in range(0, n_chunks, HGRN_STAGE_CHUNKS):
        group = range(first, min(first + HGRN_STAGE_CHUNKS, n_chunks))
        local = [chunk_local(c) for c in group]
        for c, parts in zip(group, local):
            chunk_state(c, parts)

    y_ref[...] = x + _dot(mixed_scr[...].astype(BF16), wout_ref[...])

    @pl.when(j == pl.num_programs(1) - 1)
    def _():
        for s in range(n_seq):
            for hd in range(B_HEADS):
                sout_ref[s, hd] = st_scr[s * B_HEADS + hd].T

    _run_side_casts(pl.program_id(0) * pl.num_programs(1) + j, side_srcs, side_dsts, rest[5:])


def _even_mixer(x, s0, consts, *, n_batch, tiles_per_batch, n_seq, seq_rows, gchunk, emit_v, side_casts=()):
    rows = n_seq * seq_rows
    grid = (n_batch, tiles_per_batch)
    tile = lambda b, j: (b * tiles_per_batch + j, 0)
    whole = lambda *shape: pl.BlockSpec(shape, lambda b, j: (0,) * len(shape))
    once = lambda *shape: pl.BlockSpec(shape, lambda b, j: (0,) * len(shape), pipeline_mode=pl.Buffered(1))
    in_specs = [
        pl.BlockSpec((rows, D_MODEL), tile),
        pl.BlockSpec((n_seq, B_HEADS, B_HDIM, B_HDIM), lambda b, j: (b, 0, 0, 0)),
        whole(1, D_MODEL), once(D_MODEL, EVEN_IN), once(D_MODEL, D_MODEL),
        whole(A_GROUPS, gchunk, gchunk), whole(A_GROUPS, gchunk, LANES),
        whole(A_GROUPS, 1, A_GDIM), whole(A_GROUPS, 1, A_GDIM), whole(1, B_DIM), whole(1, B_HDIM),
        whole((N_LEVELS + 1) * CHUNK, 3 * CHUNK), whole(N_LEVELS + 1, CHUNK, CHUNK),
    ]
    args = [x, s0] + list(consts)
    out_shape = [jax.ShapeDtypeStruct(x.shape, F32),
                 jax.ShapeDtypeStruct((n_batch * n_seq, B_HEADS, B_HDIM, B_HDIM), F32)]
    out_specs = [pl.BlockSpec((rows, D_MODEL), tile),
                 pl.BlockSpec((n_seq, B_HEADS, B_HDIM, B_HDIM), lambda b, j: (b, 0, 0, 0))]
    if emit_v:
        out_shape.append(jax.ShapeDtypeStruct((x.shape[0], A_DIM), F32))
        out_specs.append(pl.BlockSpec((rows, A_DIM), tile))
    scratch = [pltpu.VMEM((rows, EVEN_IN), F32), pltpu.VMEM((rows, D_MODEL), F32),
               pltpu.VMEM((n_seq * B_HEADS, B_HDIM, B_HDIM), F32),
               pltpu.VMEM((D_MODEL, EVEN_IN), BF16), pltpu.VMEM((D_MODEL, D_MODEL), BF16)]
    side_in, side_out, side_shapes, side_scratch = _side_cast_plumbing(side_casts, n_batch * tiles_per_batch)
    in_specs += side_in
    args += list(side_casts)
    out_specs += side_out
    out_shape += side_shapes
    scratch += side_scratch
    body = functools.partial(_even_mixer_body, n_seq=n_seq, seq_rows=seq_rows, gchunk=gchunk, emit_v=emit_v,
                             n_side=len(side_casts))
    return pl.pallas_call(
        body, grid=grid, in_specs=in_specs, out_specs=out_specs, out_shape=out_shape, scratch_shapes=scratch,
        compiler_params=_cparams(2),
        name="even_mixer_s" if emit_v else "even_mixer_p")(*args)


def _memory_kv_body(mem_ref, gm_ref, wk_ref, wv_ref, kg_ref, mk_ref, mv_ref, *, n_seq, n_mem):
    m = _rms(mem_ref[...], gm_ref[0]).astype(BF16)
    kk = _dot(m, wk_ref[0].astype(BF16))
    vv = _dot(m, wv_ref[0].astype(BF16))
    kg = kg_ref[0]
    for hd in range(X_HEADS):
        hs = slice(hd * X_HDIM, (hd + 1) * X_HDIM)
        kh = _rms(kk[:, hs], kg)
        for s in range(n_seq):
            mk_ref[0, s, pl.ds(hd, n_mem, stride=X_HEADS), :] = kh[s * n_mem:(s + 1) * n_mem]
            mv_ref[0, s, pl.ds(hd, n_mem, stride=X_HEADS), :] = vv[s * n_mem:(s + 1) * n_mem, hs]


def _memory_kv(mem2d, g_mem, w_k, w_v, k_g, *, n_batch, n_mem):
    depth = w_k.shape[0]
    n_seq = ROW_TILE // n_mem
    out = jax.ShapeDtypeStruct((depth, n_batch, n_mem * X_HEADS, X_HDIM), F32)
    ospec = pl.BlockSpec((1, n_seq, n_mem * X_HEADS, X_HDIM), lambda l, t: (l, t, 0, 0))
    return pl.pallas_call(
        functools.partial(_memory_kv_body, n_seq=n_seq, n_mem=n_mem),
        grid=(depth, n_batch // n_seq),
        in_specs=[pl.BlockSpec((ROW_TILE, D_MODEL), lambda l, t: (t, 0)),
                  pl.BlockSpec((1, 1, D_MODEL), lambda l, t: (l, 0, 0)),
                  pl.BlockSpec((1, D_MODEL, X_DIM), lambda l, t: (l, 0, 0)),
                  pl.BlockSpec((1, D_MODEL, X_DIM), lambda l, t: (l, 0, 0)),
                  pl.BlockSpec((1, 1, X_HDIM), lambda l, t: (l, 0, 0))],
        out_specs=[ospec, ospec], out_shape=[out, out], compiler_params=_cparams(2),
        name="memory_kv")(mem2d, g_mem, w_k, w_v, k_g)


def _xattn_body(x_ref, mk_ref, mv_ref, nrm_ref, wq32_ref, wo32_ref, qg_ref, *rest, n_seq, seq_rows, n_side):
    side_srcs, y_ref, side_dsts = rest[:n_side], rest[n_side], rest[n_side + 1:2 * n_side + 1]
    o_scr, wq_ref, wo_ref = rest[2 * n_side + 1:2 * n_side + 4]
    _run_side_casts(pl.program_id(0), side_srcs, side_dsts, rest[2 * n_side + 4:])

    @pl.when(pl.program_id(0) == 0)
    def _():
        wq_ref[...] = wq32_ref[...].astype(BF16)
        wo_ref[...] = wo32_ref[...].astype(BF16)

    x = x_ref[...]
    h = _rms(x, nrm_ref[...]).astype(BF16)
    q = _dot(h, wq_ref[...])
    qg = qg_ref[...] * (X_HDIM ** -0.5)
    units = [(s, hd) for s in range(n_seq) for hd in range(X_HEADS)]
    rows_of = lambda s: slice(s * seq_rows, (s + 1) * seq_rows)
    lanes_of = lambda hd: slice(hd * X_HDIM, (hd + 1) * X_HDIM)
    n_mem = mk_ref.shape[1] // X_HEADS
    head_rows = lambda hd: pl.ds(hd, n_mem, stride=X_HEADS)
    qh = [_rms(q[rows_of(s), lanes_of(hd)], qg).astype(BF16) for s, hd in units]
    sc = [_dot_nt(qh[n], mk_ref[s, head_rows(hd), :].astype(BF16)) for n, (s, hd) in enumerate(units)]
    m = [jnp.max(t, axis=-1, keepdims=True) for t in sc]
    p = [jnp.exp(t - mx) for t, mx in zip(sc, m)]
    den = [jnp.sum(t, axis=-1, keepdims=True) for t in p]
    for n, (s, hd) in enumerate(units):
        o_scr[rows_of(s), lanes_of(hd)] = _dot(p[n].astype(BF16), mv_ref[s, head_rows(hd), :].astype(BF16)) / den[n]
    y_ref[...] = x + _dot(o_scr[...].astype(BF16), wo_ref[...])


def _xattn(x, mk, mv, nrm, wq, wo, qg, *, layer, steps_per_mem, n_seq, seq_rows, side_casts=()):
    rows = n_seq * seq_rows
    n_steps = x.shape[0] // rows
    mem_spec = pl.BlockSpec((None, n_seq) + mk.shape[2:], lambda t: (layer, t // steps_per_mem, 0, 0))
    whole = lambda *shape: pl.BlockSpec(shape, lambda t: (0,) * len(shape))
    once = lambda *shape: pl.BlockSpec((None,) + shape, lambda t: (layer,) + (0,) * len(shape),
                                       pipeline_mode=pl.Buffered(1))
    side_in, side_out, side_shapes, side_scratch = _side_cast_plumbing(side_casts, n_steps)
    out = pl.pallas_call(
        functools.partial(_xattn_body, n_seq=n_seq, seq_rows=seq_rows, n_side=len(side_casts)),
        grid=(n_steps,),
        in_specs=[pl.BlockSpec((rows, D_MODEL), lambda t: (t, 0)), mem_spec, mem_spec,
                  whole(1, D_MODEL), once(D_MODEL, X_DIM), once(X_DIM, D_MODEL), whole(1, X_HDIM)] + side_in,
        out_specs=[pl.BlockSpec((rows, D_MODEL), lambda t: (t, 0))] + side_out,
        out_shape=[jax.ShapeDtypeStruct(x.shape, F32)] + side_shapes,
        scratch_shapes=[pltpu.VMEM((rows, X_DIM), F32), pltpu.VMEM((D_MODEL, X_DIM), BF16),
                        pltpu.VMEM((X_DIM, D_MODEL), BF16)] + side_scratch,
        compiler_params=_cparams(1),
        name="xattn_s" if n_seq > 1 else "xattn_p")(x, mk, mv, nrm, wq, wo, qg, *side_casts)
    return out if side_casts else out[0]


def _two_source_specs(n_prompt_tiles, width=D_MODEL):
    return (pl.BlockSpec((ROW_TILE, width), lambda i, *_: (jnp.minimum(i, n_prompt_tiles - 1), 0)),
            pl.BlockSpec((ROW_TILE, width), lambda i, *_: (jnp.maximum(i - n_prompt_tiles, 0), 0)))


def _on_row_source(i, n_prompt_tiles, fn, prompt_refs, sample_refs):
    pl.when(i < n_prompt_tiles)(lambda: fn(*prompt_refs))
    pl.when(i >= n_prompt_tiles)(lambda: fn(*sample_refs))


def _swiglu_part(hb, w1, w3, w2):
    part = None
    for c0 in range(0, D_FF, FF_SUB):
        cs = slice(c0, min(c0 + FF_SUB, D_FF))
        act = (_silu(_dot(hb, w1[:, cs])) * _dot(hb, w3[:, cs])).astype(BF16)
        p = _dot(act, w2[cs, :])
        part = p if part is None else part + p
    return part


def _dense_ffn_body(xp_ref, xs_ref, nrm_ref, w1_ref, w3_ref, w2_ref, *rest, n_prompt_tiles, n_side):
    side_srcs, (yp_ref, ys_ref), side_dsts = rest[:n_side], rest[n_side:n_side + 2], rest[n_side + 2:2 * n_side + 2]
    i = pl.program_id(0)

    def run(x_ref, y_ref):
        x = x_ref[...]
        y_ref[...] = x + _swiglu_part(_rms(x, nrm_ref[...]).astype(BF16), w1_ref, w3_ref, w2_ref)

    _on_row_source(i, n_prompt_tiles, run, (xp_ref, yp_ref), (xs_ref, ys_ref))
    _run_side_casts(i, side_srcs, side_dsts, rest[2 * n_side + 2:])


def _dense_ffn(xp, xs, nrm, w1, w3, w2, side_casts):
    n_prompt_tiles = xp.shape[0] // ROW_TILE
    n_tiles = n_prompt_tiles + xs.shape[0] // ROW_TILE
    once = lambda *shape: pl.BlockSpec(shape, lambda i: (0,) * len(shape), pipeline_mode=pl.Buffered(1))
    side_in, side_out, side_shapes, side_scratch = _side_cast_plumbing(side_casts, n_tiles)
    return pl.pallas_call(
        functools.partial(_dense_ffn_body, n_prompt_tiles=n_prompt_tiles, n_side=len(side_casts)), grid=(n_tiles,),
        in_specs=[*_two_source_specs(n_prompt_tiles), pl.BlockSpec((1, D_MODEL), lambda i: (0, 0)),
                  once(D_MODEL, D_FF), once(D_MODEL, D_FF), once(D_FF, D_MODEL)] + side_in,
        out_specs=[*_two_source_specs(n_prompt_tiles)] + side_out,
        out_shape=[jax.ShapeDtypeStruct(xp.shape, F32), jax.ShapeDtypeStruct(xs.shape, F32)] + side_shapes,
        scratch_shapes=side_scratch,
        compiler_params=_cparams(1), name="dense_ffn")(xp, xs, nrm, w1, w3, w2, *side_casts)


def _grouped_ffn_body(te_ref, tv_ref, x_ref, w1_ref, w3_ref, w2_ref, y_ref):
    valid = tv_ref[pl.program_id(0)]
    half = ROW_TILE // 2
    weights = (w1_ref.at[0], w3_ref.at[0], w2_ref.at[0])

    @pl.when(valid > half)
    def _():
        y_ref[...] = _swiglu_part(x_ref[...].astype(BF16), *weights)

    @pl.when((valid > 0) & (valid <= half))
    def _():
        y_ref[0:half] = _swiglu_part(x_ref[0:half].astype(BF16), *weights)
        y_ref[half:ROW_TILE] = jnp.zeros((ROW_TILE - half, D_MODEL), F32)

    @pl.when(valid == 0)
    def _():
        y_ref[...] = jnp.zeros_like(y_ref)


def _grouped_ffn(xsorted, tile_expert, tile_valid, w1, w3, w2):
    n_tiles = xsorted.shape[0] // ROW_TILE
    grid_spec = pltpu.PrefetchScalarGridSpec(
        num_scalar_prefetch=2, grid=(n_tiles,),
        in_specs=[pl.BlockSpec((ROW_TILE, D_MODEL), lambda i, te, tv: (i, 0)),
                  pl.BlockSpec((1, D_MODEL, D_FF), lambda i, te, tv: (te[i], 0, 0)),
                  pl.BlockSpec((1, D_MODEL, D_FF), lambda i, te, tv: (te[i], 0, 0)),
                  pl.BlockSpec((1, D_FF, D_MODEL), lambda i, te, tv: (te[i], 0, 0))],
        out_specs=pl.BlockSpec((ROW_TILE, D_MODEL), lambda i, te, tv: (i, 0)))
    return pl.pallas_call(
        _grouped_ffn_body, grid_spec=grid_spec, out_shape=jax.ShapeDtypeStruct(xsorted.shape, F32),
        compiler_params=_cparams(1), name="grouped_ffn")(tile_expert, tile_valid, xsorted, w1, w3, w2)


def _rope_slab(xs, cos, sin_signed, first_half):
    rot = jnp.where(first_half, pltpu.roll(xs, LANES - C_HDIM // 2, 1), pltpu.roll(xs, C_HDIM // 2, 1))
    return xs * cos + rot * sin_signed


def _swa_body(sink_ref, x_ref, pk_ref, pv_ref, cos_ref, sin_ref, nrm_ref, win32_ref, wout32_ref, qg_ref, kg_ref,
              bd_ref, *rest, n_seq, seq_rows, past_valid, side_cast):
    rest = list(rest)
    side_src = rest.pop(0) if side_cast else None
    y_ref, ko_ref, vo_ref = rest.pop(0), rest.pop(0), rest.pop(0)
    side_dst = rest.pop(0) if side_cast else None
    k_scr, v_scr, q_scr, a_scr, win_ref, wout_ref = rest[:6]
    rows = n_seq * seq_rows
    j = pl.program_id(1)
    tail = min(WINDOW, seq_rows)

    @pl.when((pl.program_id(0) == 0) & (j == 0))
    def _():
        win_ref[...] = win32_ref[...].astype(BF16)
        wout_ref[...] = wout32_ref[...].astype(BF16)

    if past_valid:
        for s in range(n_seq):
            k_scr[s, 0:WINDOW] = pk_ref[s]
            v_scr[s, 0:WINDOW] = pv_ref[s]
    else:
        @pl.when(j == 0)
        def _():
            for s in range(n_seq):
                k_scr[s, 0:WINDOW] = jnp.zeros((WINDOW, C_KV_DIM), F32)
                v_scr[s, 0:WINDOW] = jnp.zeros((WINDOW, C_KV_DIM), F32)

        @pl.when(j > 0)
        def _():
            for s in range(n_seq):
                k_scr[s, 0:WINDOW] = k_scr[s, seq_rows:seq_rows + WINDOW]
                v_scr[s, 0:WINDOW] = v_scr[s, seq_rows:seq_rows + WINDOW]

    x = x_ref[...]
    h = _rms(x, nrm_ref[...]).astype(BF16)
    q_dim = C_HEADS * C_HDIM
    q = _dot(h, win_ref[:, 0:q_dim])
    k = _dot(h, win_ref[:, q_dim:q_dim + C_KV_DIM])
    v = _dot(h, win_ref[:, q_dim + C_KV_DIM:q_dim + 2 * C_KV_DIM])

    cos = cos_ref[...]
    sin_signed = sin_ref[...]
    first_half = (lax.broadcasted_iota(jnp.int32, (rows, LANES), 1) % C_HDIM) < (C_HDIM // 2)
    bd = bd_ref[...]
    qn = q * lax.rsqrt(_dot((q * q).astype(BF16), bd) + EPS) * qg_ref[...]
    kn = k * lax.rsqrt(_dot((k * k).astype(BF16), bd[0:C_KV_DIM, 0:C_KV_DIM]) + EPS) * kg_ref[...]
    scale = C_HDIM ** -0.5
    for sl in range(q_dim // LANES):
        ls = slice(sl * LANES, (sl + 1) * LANES)
        q_scr[:, ls] = _rope_slab(qn[:, ls], cos, sin_signed, first_half) * scale
    for sl in range(C_KV_DIM // LANES):
        ls = slice(sl * LANES, (sl + 1) * LANES)
        kr = _rope_slab(kn[:, ls], cos, sin_signed, first_half)
        for s in range(n_seq):
            k_scr[s, WINDOW:WINDOW + seq_rows, ls] = kr[s * seq_rows:(s + 1) * seq_rows]
    for s in range(n_seq):
        v_scr[s, WINDOW:WINDOW + seq_rows] = v[s * seq_rows:(s + 1) * seq_rows]
        ko_ref[s] = k_scr[s, WINDOW + seq_rows - tail:WINDOW + seq_rows]
        vo_ref[s] = v_scr[s, WINDOW + seq_rows - tail:WINDOW + seq_rows]

    chunks_per_seq = seq_rows // CHUNK
    n_keys = WINDOW + CHUNK
    key_idx = lax.broadcasted_iota(jnp.int32, (C_GROUP * CHUNK, n_keys), 1)
    row_grp = lax.broadcasted_iota(jnp.int32, (C_GROUP * CHUNK, 1), 0) // CHUNK
    ones_keys = jnp.ones((n_keys, C_HDIM), BF16)

    head_sinks = []
    for kvh in range(C_KV_HEADS):
        sink = jnp.zeros((C_GROUP * CHUNK, 1), F32)
        for g in range(C_GROUP):
            sink = jnp.where(row_grp == g, sink_ref[kvh * C_GROUP + g], sink)
        head_sinks.append(sink)

    def attend(chunk_ids):
        units = [(idx, kvh) for idx in chunk_ids for kvh in range(C_KV_HEADS)]
        head_lanes = lambda kvh: slice(kvh * C_HDIM, (kvh + 1) * C_HDIM)
        q_rows = lambda idx: slice(idx * CHUNK, (idx + 1) * CHUNK)
        key_rows = lambda idx: slice((idx % chunks_per_seq) * CHUNK, (idx % chunks_per_seq) * CHUNK + n_keys)
        qs = [jnp.concatenate(
            [q_scr[q_rows(idx), (kvh * C_GROUP + g) * C_HDIM:(kvh * C_GROUP + g + 1) * C_HDIM]
             for g in range(C_GROUP)], axis=0).astype(BF16) for idx, kvh in units]
        kh = [k_scr[idx // chunks_per_seq, key_rows(idx), head_lanes(kvh)].astype(BF16) for idx, kvh in units]
        vh = [v_scr[idx // chunks_per_seq, key_rows(idx), head_lanes(kvh)].astype(BF16) for idx, kvh in units]
        sc = [_dot_nt(qn, kn) for qn, kn in zip(qs, kh)]
        if not past_valid:
            sc = [jnp.where((key_idx + (idx % chunks_per_seq) * CHUNK >= WINDOW) | (j > 0), sn, -jnp.inf)
                  for sn, (idx, kvh) in zip(sc, units)]
        m = [jnp.maximum(jnp.max(sn, axis=-1, keepdims=True), head_sinks[kvh]) for sn, (idx, kvh) in zip(sc, units)]
        p = [jnp.exp(sn - mn).astype(BF16) for sn, mn in zip(sc, m)]
        den = [_dot(pn, ones_keys) + jnp.exp(head_sinks[kvh] - mn) for pn, mn, (idx, kvh) in zip(p, m, units)]
        o = [_dot(pn, vn) / dn for pn, vn, dn in zip(p, vh, den)]
        for on, (idx, kvh) in zip(o, units):
            for g in range(C_GROUP):
                hs = slice((kvh * C_GROUP + g) * C_HDIM, (kvh * C_GROUP + g + 1) * C_HDIM)
                a_scr[q_rows(idx), hs] = on[g * CHUNK:(g + 1) * CHUNK]

    n_chunks = rows // CHUNK
    for first in range(0, n_chunks, SWA_STAGE_CHUNKS):
        attend(range(first, min(first + SWA_STAGE_CHUNKS, n_chunks)))
    y_ref[...] = x + _dot(a_scr[...].astype(BF16), wout_ref[...])
    if side_cast:
        _cast_rows_step(pl.program_id(0) * pl.num_programs(1) + j, side_src, side_dst, *rest[6:])


def _swa(x, pk, pv, cos, sin_signed, sinks, consts, *, n_batch, tiles_per_batch, n_seq, seq_rows, past_valid,
         side_cast=None):
    rows = n_seq * seq_rows
    tail = min(WINDOW, seq_rows)
    n_str = n_batch * n_seq
    whole = lambda *shape: pl.BlockSpec(shape, lambda b, j, sk: (0,) * len(shape))
    once = lambda *shape: pl.BlockSpec(shape, lambda b, j, sk: (0,) * len(shape), pipeline_mode=pl.Buffered(1))
    in_specs = [pl.BlockSpec((rows, D_MODEL), lambda b, j, sk: (b * tiles_per_batch + j, 0)),
                pl.BlockSpec((n_seq, WINDOW, C_KV_DIM), lambda b, j, sk: (b, 0, 0)),
                pl.BlockSpec((n_seq, WINDOW, C_KV_DIM), lambda b, j, sk: (b, 0, 0)),
                pl.BlockSpec((rows, LANES), lambda b, j, sk: (j, 0)),
                pl.BlockSpec((rows, LANES), lambda b, j, sk: (j, 0)),
                whole(1, D_MODEL), once(D_MODEL, ODD_IN), once(D_MODEL, D_MODEL),
                whole(1, D_MODEL), whole(1, C_KV_DIM), whole(D_MODEL, D_MODEL)]
    args = [x, pk, pv, cos, sin_signed] + list(consts)
    kv_out = jax.ShapeDtypeStruct((n_str, tail, C_KV_DIM), F32)
    kv_spec = pl.BlockSpec((n_seq, tail, C_KV_DIM), lambda b, j, sk: (b, 0, 0))
    out_specs = [pl.BlockSpec((rows, D_MODEL), lambda b, j, sk: (b * tiles_per_batch + j, 0)), kv_spec, kv_spec]
    out_shape = [jax.ShapeDtypeStruct(x.shape, F32), kv_out, kv_out]
    scratch = [pltpu.VMEM((n_seq, WINDOW + seq_rows, C_KV_DIM), F32),
               pltpu.VMEM((n_seq, WINDOW + seq_rows, C_KV_DIM), F32),
               pltpu.VMEM((rows, D_MODEL), F32), pltpu.VMEM((rows, D_MODEL), F32),
               pltpu.VMEM((D_MODEL, ODD_IN), BF16), pltpu.VMEM((D_MODEL, D_MODEL), BF16)]
    if side_cast is not None:
        in_specs.append(pl.BlockSpec(memory_space=pl.ANY))
        args.append(side_cast)
        out_specs.append(pl.BlockSpec(memory_space=pl.ANY))
        out_shape.append(jax.ShapeDtypeStruct(side_cast.shape, BF16))
        scratch += _cast_scratch(side_cast, n_batch * tiles_per_batch)
    grid_spec = pltpu.PrefetchScalarGridSpec(
        num_scalar_prefetch=1, grid=(n_batch, tiles_per_batch), in_specs=in_specs, out_specs=out_specs,
        scratch_shapes=scratch)
    return pl.pallas_call(
        functools.partial(_swa_body, n_seq=n_seq, seq_rows=seq_rows, past_valid=past_valid,
                          side_cast=side_cast is not None),
        grid_spec=grid_spec, out_shape=out_shape,
        compiler_params=_cparams(2),
        name="swa_s" if past_valid else "swa_p")(sinks, *args)


BF16_ROWS = 16


def _cast_piece_rows(w, n_grid_steps):
    n_cast = 1 << (n_grid_steps.bit_length() - 1)
    while w.shape[0] % (n_cast * BF16_ROWS):
        n_cast //= 2
    return w.shape[0] // n_cast


def _cast_rows_step(step, src_hbm, dst_hbm, in_stage, out_stage, sems):
    n_rows = in_stage.shape[1]
    n_cast = src_hbm.shape[0] // n_rows
    slot = step % 2

    def read(s, sl):
        return pltpu.make_async_copy(src_hbm.at[pl.ds(pl.multiple_of(s * n_rows, 16), n_rows)], in_stage.at[sl],
                                     sems.at[sl])

    def write(s, sl):
        return pltpu.make_async_copy(out_stage.at[sl], dst_hbm.at[pl.ds(pl.multiple_of(s * n_rows, 16), n_rows)],
                                     sems.at[2 + sl])

    @pl.when(step == 0)
    def _():
        read(0, 0).start()

    @pl.when(step < n_cast)
    def _():
        @pl.when(step + 1 < n_cast)
        def _():
            read(step + 1, 1 - slot).start()

        read(step, slot).wait()

        @pl.when(step >= 2)
        def _():
            write(step - 2, slot).wait()

        out_stage[slot] = in_stage[slot].astype(BF16)
        write(step, slot).start()

        @pl.when(step == n_cast - 1)
        def _():
            write(step, slot).wait()
            if n_cast > 1:
                write(step - 1, 1 - slot).wait()


def _cast_scratch(w, n_grid_steps):
    rows = _cast_piece_rows(w, n_grid_steps)
    return [pltpu.VMEM((2, rows, w.shape[1]), F32), pltpu.VMEM((2, rows, w.shape[1]), BF16),
            pltpu.SemaphoreType.DMA((4,))]


def _side_cast_plumbing(side_casts, n_grid_steps):
    any_space = pl.BlockSpec(memory_space=pl.ANY)
    scratch = [s for w in side_casts for s in _cast_scratch(w, n_grid_steps)]
    return ([any_space] * len(side_casts), [any_space] * len(side_casts),
            [jax.ShapeDtypeStruct(w.shape, BF16) for w in side_casts], scratch)


def _run_side_casts(step, srcs, dsts, scratch):
    for n, (src, dst) in enumerate(zip(srcs, dsts)):
        _cast_rows_step(step, src, dst, *scratch[3 * n:3 * n + 3])


def _router_body(xp_ref, xs_ref, nrm_ref, rboth_ref, lstrict_ref, su_ref, info_ref, cnt_ref, *, n_prompt_tiles):
    def run(x_ref):
        h = _rms(x_ref[...], nrm_ref[...])
        hi = h.astype(BF16)
        lo = (h - hi.astype(F32)).astype(BF16)
        both = _dot(hi, rboth_ref[...])
        logits = both[:, 0:LANES] + both[:, LANES:2 * LANES] + _dot(lo, rboth_ref[:, 0:LANES])
        lane = lax.broadcasted_iota(jnp.int32, logits.shape, 1).astype(F32)
        logits = jnp.where(lane < N_EXPERTS, logits, -jnp.inf)
        l1 = jnp.max(logits, axis=-1, keepdims=True)
        i1 = jnp.min(jnp.where(logits == l1, lane, float(LANES)), axis=-1, keepdims=True)
        rest = jnp.where(lane == i1, -jnp.inf, logits)
        l2 = jnp.max(rest, axis=-1, keepdims=True)
        i2 = jnp.min(jnp.where(rest == l2, lane, float(LANES)), axis=-1, keepdims=True)
        e = jnp.exp(l2 - l1)
        g1 = 1.0 / (1.0 + e)
        g2 = e * g1
        oh1 = (lane == i1).astype(F32)
        oh2 = (lane == i2).astype(F32)
        cnt1 = jnp.sum(oh1, axis=0, keepdims=True)
        cnt2 = jnp.sum(oh2, axis=0, keepdims=True)
        seg = jnp.broadcast_to(jnp.ceil((cnt1 + cnt2) * (1.0 / GROUP)), (GROUP, LANES))
        off = _dot(seg.astype(BF16), su_ref[...])[0:1] * GROUP
        c1 = _dot(lstrict_ref[...], oh1.astype(BF16)) + off
        c2 = _dot(lstrict_ref[...], oh2.astype(BF16)) + off + cnt1
        d1 = jnp.sum(jnp.where(lane == i1, c1, 0.0), axis=-1, keepdims=True)
        d2 = jnp.sum(jnp.where(lane == i2, c2, 0.0), axis=-1, keepdims=True)
        info = jnp.zeros(logits.shape, F32)
        for n, col in enumerate([d1, d2, g1, g2]):
            info = jnp.where(lane == n, col, info)
        info_ref[...] = info
        cnt_ref[...] = seg * GROUP

    _on_row_source(pl.program_id(0), n_prompt_tiles, run, (xp_ref,), (xs_ref,))


def _router(xp, xs, nrm, rboth, lstrict, su):
    n_prompt_tiles = xp.shape[0] // ROW_TILE
    n_tiles = n_prompt_tiles + xs.shape[0] // ROW_TILE
    whole = lambda *shape: pl.BlockSpec(shape, lambda i: (0,) * len(shape))
    return pl.pallas_call(
        functools.partial(_router_body, n_prompt_tiles=n_prompt_tiles), grid=(n_tiles,),
        in_specs=[*_two_source_specs(n_prompt_tiles), whole(1, D_MODEL), whole(D_MODEL, 2 * LANES),
                  whole(ROW_TILE, ROW_TILE), whole(LANES, LANES)],
        out_specs=[pl.BlockSpec((ROW_TILE, LANES), lambda i: (i, 0)), pl.BlockSpec((GROUP, LANES), lambda i: (i, 0))],
        out_shape=[jax.ShapeDtypeStruct((n_tiles * ROW_TILE, LANES), F32),
                   jax.ShapeDtypeStruct((n_tiles * GROUP, LANES), F32)],
        compiler_params=_cparams(1), name="router")(xp, xs, nrm, rboth, lstrict, su)


BIG_ROWS = 8 * GROUP
SEG_FIELDS = 4
TILE_FIELDS = 2
TAIL_FIELDS = 3


def _rows_copy(src_ref, src_row, dst_ref, dst_row, n_rows, sem):
    return pltpu.make_async_copy(src_ref.at[pl.ds(pl.multiple_of(src_row, GROUP), n_rows)],
                                 dst_ref.at[pl.ds(pl.multiple_of(dst_row, GROUP), n_rows)], sem)


def _start_pieces(src_ref, src0, dst_ref, dst0, n_big, n_small, sem):
    def big(g, carry):
        _rows_copy(src_ref, src0 + g * BIG_ROWS, dst_ref, dst0 + g * BIG_ROWS, BIG_ROWS, sem).start()
        return carry

    lax.fori_loop(0, n_big, big, 0)
    done = n_big * BIG_ROWS

    def small(g, carry):
        _rows_copy(src_ref, src0 + done + g * GROUP, dst_ref, dst0 + done + g * GROUP, GROUP, sem).start()
        return carry

    lax.fori_loop(0, n_small, small, 0)


def _wait_pieces(n_big, n_small, src_ref, dst_ref, sem):
    def big(g, carry):
        _rows_copy(src_ref, 0, dst_ref, 0, BIG_ROWS, sem).wait()
        return carry

    lax.fori_loop(0, n_big, big, 0)

    def small(g, carry):
        _rows_copy(src_ref, 0, dst_ref, 0, GROUP, sem).wait()
        return carry

    lax.fori_loop(0, n_small, small, 0)


def _segment_table(seg, n_sorted_tiles):
    pieces = lambda rows: (rows // BIG_ROWS, (rows % BIG_ROWS) // GROUP)
    local_off = jnp.cumsum(seg, axis=1) - seg
    expert_rows = jnp.sum(seg, axis=0)
    expert_pad = ((expert_rows + ROW_TILE - 1) // ROW_TILE) * ROW_TILE
    expert_end = jnp.cumsum(expert_pad)
    expert_off = expert_end - expert_pad
    seg_start = expert_off[None, :] + jnp.cumsum(seg, axis=0) - seg
    n_used = (expert_end[-1] // ROW_TILE).reshape(1)
    tile_ids = jnp.arange(n_sorted_tiles, dtype=jnp.int32)
    tile_expert = jnp.minimum(jnp.sum(tile_ids[:, None] >= (expert_end // ROW_TILE)[None, :], axis=1), N_EXPERTS - 1)
    tile_rows0 = (tile_ids * ROW_TILE)[:, None]
    overlap = (jnp.minimum((expert_off + expert_rows)[None, :], tile_rows0 + ROW_TILE)
               - jnp.maximum(expert_off[None, :], tile_rows0))
    tile_valid = jnp.sum(jnp.maximum(overlap, 0), axis=1)
    seg_big, seg_small = pieces(seg)
    tail_big, tail_small = pieces(expert_pad - expert_rows)
    table = jnp.concatenate([
        jnp.stack([seg_start, seg_big, seg_small, local_off], axis=-1).reshape(-1),
        jnp.stack([jnp.sum(seg_big, axis=1), jnp.sum(seg_small, axis=1)], axis=-1).reshape(-1),
        jnp.stack([expert_off + expert_rows, tail_big, tail_small], axis=-1).reshape(-1),
        n_used])
    return table.astype(jnp.int32), tile_expert.astype(jnp.int32), tile_valid.astype(jnp.int32)


def _table_sections(n_tiles):
    tiles0 = SEG_FIELDS * n_tiles * N_EXPERTS
    tails0 = tiles0 + TILE_FIELDS * n_tiles
    return tiles0, tails0, tails0 + TAIL_FIELDS * N_EXPERTS


def _dispatch_body(tab_ref, info_ref, xp_ref, xs_ref, nrm_ref, sel_ref, out_ref, loc_scr, z_scr, sems, *,
                   n_prompt_tiles, n_tiles, n_sorted_tiles):
    i = pl.program_id(0)
    slot = i % 2
    tiles0, tails0, used0 = _table_sections(n_tiles)

    def wait_tile(t, s):
        _wait_pieces(tab_ref[tiles0 + TILE_FIELDS * t], tab_ref[tiles0 + TILE_FIELDS * t + 1], loc_scr.at[s], out_ref,
                     sems.at[s])

    @pl.when(i >= 2)
    def _():
        wait_tile(i - 2, slot)

    def run(x_ref):
        hb = _rms(x_ref[...], nrm_ref[...]).astype(BF16)
        info = info_ref[...]
        lane = lax.broadcasted_iota(jnp.int32, info.shape, 1)
        dest = jnp.where(lane < TOP_K, info, 0.0)
        drow = sum(_dot_nt(sel_ref[...], part) for part in _split3(dest))
        r = lax.broadcasted_iota(jnp.int32, (LOCAL_ROWS, ROW_TILE), 0).astype(F32)
        perm = jnp.where(r == drow[0:1], 1.0, jnp.where(r == drow[1:2], 1.0, 0.0)).astype(BF16)
        loc_scr[slot] = _dot(perm, hb)

    _on_row_source(i, n_prompt_tiles, run, (xp_ref,), (xs_ref,))

    for e in range(N_EXPERTS):
        base = SEG_FIELDS * (i * N_EXPERTS + e)
        _start_pieces(loc_scr.at[slot], tab_ref[base + 3], out_ref, tab_ref[base], tab_ref[base + 1],
                      tab_ref[base + 2], sems.at[slot])

    @pl.when(i == n_tiles - 1)
    def _():
        wait_tile(i, slot)
        if n_tiles > 1:
            wait_tile(i - 1, 1 - slot)
        z_scr[...] = jnp.zeros_like(z_scr)
        for e in range(N_EXPERTS):
            base = tails0 + TAIL_FIELDS * e
            _start_pieces(z_scr, 0, out_ref, tab_ref[base], tab_ref[base + 1], tab_ref[base + 2], sems.at[2])
            _wait_pieces(tab_ref[base + 1], tab_ref[base + 2], z_scr, out_ref, sems.at[2])

        def zero_tile(t, carry):
            cp = _rows_copy(z_scr, 0, out_ref, t * ROW_TILE, ROW_TILE, sems.at[2])
            cp.start()
            cp.wait()
            return carry

        lax.fori_loop(tab_ref[used0], n_sorted_tiles, zero_tile, 0)


def _dispatch(table, info, xp, xs, nrm, sel, n_sorted_tiles):
    n_prompt_tiles = xp.shape[0] // ROW_TILE
    n_tiles = n_prompt_tiles + xs.shape[0] // ROW_TILE
    whole = lambda *shape: pl.BlockSpec(shape, lambda i, tab: (0,) * len(shape))
    grid_spec = pltpu.PrefetchScalarGridSpec(
        num_scalar_prefetch=1, grid=(n_tiles,),
        in_specs=[pl.BlockSpec((ROW_TILE, LANES), lambda i, tab: (i, 0)), *_two_source_specs(n_prompt_tiles),
                  whole(1, D_MODEL), whole(GROUP, LANES)],
        out_specs=pl.BlockSpec(memory_space=pl.ANY),
        scratch_shapes=[pltpu.VMEM((2, LOCAL_ROWS, D_MODEL), F32), pltpu.VMEM((ROW_TILE, D_MODEL), F32),
                        pltpu.SemaphoreType.DMA((3,))])
    return pl.pallas_call(
        functools.partial(_dispatch_body, n_prompt_tiles=n_prompt_tiles, n_tiles=n_tiles,
                          n_sorted_tiles=n_sorted_tiles),
        grid_spec=grid_spec, out_shape=jax.ShapeDtypeStruct((n_sorted_tiles * ROW_TILE, D_MODEL), F32),
        compiler_params=_cparams(1), name="moe_dispatch")(table, info, xp, xs, nrm, sel)


def _combine_body(tab_ref, info_ref, xp_ref, xs_ref, e_ref, yp_ref, ys_ref, loc_scr, sems, *, n_prompt_tiles,
                  n_tiles):
    i = pl.program_id(0)
    slot = i % 2
    tiles0, _, _ = _table_sections(n_tiles)

    def fetch(t, s):
        for e in range(N_EXPERTS):
            base = SEG_FIELDS * (t * N_EXPERTS + e)
            _start_pieces(e_ref, tab_ref[base], loc_scr.at[s], tab_ref[base + 3], tab_ref[base + 1],
                          tab_ref[base + 2], sems.at[s])

    @pl.when(i == 0)
    def _():
        loc_scr[...] = jnp.zeros_like(loc_scr)
        fetch(0, 0)

    @pl.when(i + 1 < n_tiles)
    def _():
        fetch(i + 1, 1 - slot)

    _wait_pieces(tab_ref[tiles0 + TILE_FIELDS * i], tab_ref[tiles0 + TILE_FIELDS * i + 1], e_ref, loc_scr.at[slot],
                 sems.at[slot])

    def run(x_ref, y_ref):
        info = info_ref[...]
        eb = loc_scr[slot].astype(BF16)
        lane = lax.broadcasted_iota(jnp.int32, (ROW_TILE, LOCAL_ROWS), 1).astype(F32)
        pick = jnp.zeros((ROW_TILE, LOCAL_ROWS), F32)
        for k in range(TOP_K):
            pick = jnp.where(lane == info[:, k:k + 1], info[:, TOP_K + k:TOP_K + k + 1], pick)
        y_ref[...] = x_ref[...] + _dot(pick.astype(BF16), eb)

    _on_row_source(i, n_prompt_tiles, run, (xp_ref, yp_ref), (xs_ref, ys_ref))


def _combine(table, info, xp, xs, esorted):
    n_prompt_tiles = xp.shape[0] // ROW_TILE
    n_tiles = n_prompt_tiles + xs.shape[0] // ROW_TILE
    grid_spec = pltpu.PrefetchScalarGridSpec(
        num_scalar_prefetch=1, grid=(n_tiles,),
        in_specs=[pl.BlockSpec((ROW_TILE, LANES), lambda i, tab: (i, 0)), *_two_source_specs(n_prompt_tiles),
                  pl.BlockSpec(memory_space=pl.ANY)],
        out_specs=list(_two_source_specs(n_prompt_tiles)),
        scratch_shapes=[pltpu.VMEM((2, LOCAL_ROWS, D_MODEL), F32), pltpu.SemaphoreType.DMA((2,))])
    return pl.pallas_call(
        functools.partial(_combine_body, n_prompt_tiles=n_prompt_tiles, n_tiles=n_tiles),
        grid_spec=grid_spec,
        out_shape=[jax.ShapeDtypeStruct(xp.shape, F32), jax.ShapeDtypeStruct(xs.shape, F32)],
        compiler_params=_cparams(1), name="moe_combine")(table, info, xp, xs, esorted)


def _rope_tables(pos):
    half = C_HDIM // 2
    inv = 1.0 / (ROPE_THETA ** (jnp.arange(half, dtype=F32) / half))
    ang = pos.astype(F32)[:, None] * inv[None, :]
    cos = jnp.cos(ang)
    sin = jnp.sin(ang)
    reps = LANES // C_HDIM
    return jnp.tile(jnp.concatenate([cos, cos], axis=-1), (1, reps)), jnp.tile(jnp.concatenate([-sin, sin], axis=-1),
                                                                                 (1, reps))


def kernel(x_prompt, x_sample, mem_prompt, cache_mem_k, cache_mem_v, state_hgrn, cache_swa_k, cache_swa_v, norm_mix, norm_xattn, norm_ffn, even_w_in, even_w_out, gmlp_w_s, gmlp_b_s, gmlp_ln_g, gmlp_ln_b, hgrn_lb_logits, hgrn_out_norm, attn_w_in, attn_w_out, attn_q_norm, attn_k_norm, attn_sinks, xattn_mem_norm, xattn_w_q, xattn_w_k, xattn_w_v, xattn_w_o, xattn_q_norm, xattn_k_norm, ffn_w1, ffn_w3, ffn_w2, moe_router, moe_w1, moe_w3, moe_w2):
    n_batch, seq, d = x_prompt.shape
    dec_batch, dec_seq, _ = x_sample.shape
    n_mem = mem_prompt.shape[1]
    depth = norm_mix.shape[0]
    past_len = PAST_LEN
    assert d == D_MODEL and depth == 2 and seq % ROW_TILE == 0 and dec_batch * dec_seq == ROW_TILE
    assert dec_seq == CHUNK and ROW_TILE % n_mem == 0 and cache_swa_k.shape[2] == WINDOW
    assert even_w_in.shape[-1] == EVEN_IN and attn_w_in.shape[-1] == ODD_IN and ffn_w1.shape[-1] == D_FF
    assert moe_w1.shape[1] == N_EXPERTS
    n_prompt_rows = n_batch * seq
    n_rows = n_prompt_rows + dec_batch * dec_seq
    n_prompt_tiles = n_prompt_rows // ROW_TILE
    tiles_per_batch = seq // ROW_TILE
    row = lambda g: g.reshape(1, -1).astype(F32)

    lb_all = jnp.cumsum(jax.nn.softmax(hgrn_lb_logits.astype(F32), axis=0), axis=0)
    wcum = jnp.asarray(_cumsum_matrix(), BF16)
    lmask = jnp.asarray(_level_masks(), F32)

    def even_consts(n):
        tril = jnp.tril(jnp.ones((n, n), bool))
        ws = jnp.where(tril[None], gmlp_w_s[0, :, :n, :n], 0.0).astype(BF16)
        bs = jnp.broadcast_to(gmlp_b_s[0, :, :n, None], (A_GROUPS, n, LANES)).astype(F32)
        return [row(norm_mix[0]), even_w_in[0], even_w_out[0], ws, bs,
                gmlp_ln_g[0].reshape(A_GROUPS, 1, A_GDIM), gmlp_ln_b[0].reshape(A_GROUPS, 1, A_GDIM),
                row(lb_all[0]), row(hgrn_out_norm[0]), wcum, lmask]

    rows2d = lambda w: w.reshape(-1, w.shape[-1])
    xp, hgrn_p, w1b, ffn_w1b, ffn_w3b, ffn_w2b = _even_mixer(
        x_prompt.reshape(n_prompt_rows, d), jnp.zeros((n_batch, B_HEADS, B_HDIM, B_HDIM), F32),
        even_consts(A_CHUNK), n_batch=n_batch, tiles_per_batch=tiles_per_batch, n_seq=1, seq_rows=ROW_TILE,
        gchunk=A_CHUNK, emit_v=False, side_casts=(rows2d(moe_w1[0]), ffn_w1[0], ffn_w3[0], ffn_w2[0]))
    xs, hgrn_s, gmlp_v = _even_mixer(
        x_sample.reshape(ROW_TILE, d), state_hgrn[0], even_consts(min(A_CHUNK, dec_seq)), n_batch=1,
        tiles_per_batch=1, n_seq=dec_batch, seq_rows=dec_seq, gchunk=min(A_CHUNK, dec_seq), emit_v=True)

    mem_k, mem_v = _memory_kv(mem_prompt.reshape(n_batch * n_mem, d), xattn_mem_norm.reshape(depth, 1, d),
                              xattn_w_k, xattn_w_v, xattn_k_norm.reshape(depth, 1, X_HDIM),
                              n_batch=n_batch, n_mem=n_mem)

    def cross_attention(xp, xs, l, side_casts=()):
        consts = (row(norm_xattn[l]), xattn_w_q, xattn_w_o, row(xattn_q_norm[l]))
        xp = _xattn(xp, mem_k, mem_v, *consts, layer=l, steps_per_mem=tiles_per_batch, n_seq=1, seq_rows=ROW_TILE,
                    side_casts=side_casts)
        xs = _xattn(xs, cache_k, cache_v, *consts, layer=l, steps_per_mem=1, n_seq=dec_batch, seq_rows=dec_seq)
        return xp, xs

    cache_k = cache_mem_k.reshape(depth, dec_batch, n_mem * X_HEADS, X_HDIM)
    cache_v = cache_mem_v.reshape(depth, dec_batch, n_mem * X_HEADS, X_HDIM)

    xp, xs = cross_attention(xp, xs, 0)
    xp, xs, w2b, attn_w_in_b, attn_w_out_b = _dense_ffn(xp, xs, row(norm_ffn[0]), ffn_w1b, ffn_w3b, ffn_w2b,
                                                        (rows2d(moe_w2[0]), attn_w_in[0], attn_w_out[0]))

    reps = D_MODEL // C_HDIM
    bd = jnp.asarray(np.kron(np.eye(reps, dtype=np.float32), np.full((C_HDIM, C_HDIM), 1.0 / C_HDIM, np.float32)),
                     BF16)
    swa_consts = [row(norm_mix[1]), attn_w_in_b, attn_w_out_b, row(jnp.tile(attn_q_norm[0], C_HEADS)),
                  row(jnp.tile(attn_k_norm[0], C_KV_HEADS)), bd]
    sinks = attn_sinks[0].astype(F32)
    cos_p, sin_p = _rope_tables(jnp.arange(seq, dtype=jnp.int32))
    cos_s, sin_s = _rope_tables(past_len + jnp.arange(dec_seq, dtype=jnp.int32))
    no_past = jnp.zeros((n_batch, WINDOW, C_KV_DIM), F32)
    xp, swk_p, swv_p, w3b = _swa(xp, no_past, no_past, cos_p, sin_p, sinks, swa_consts, n_batch=n_batch,
                                 tiles_per_batch=tiles_per_batch, n_seq=1, seq_rows=ROW_TILE, past_valid=False,
                                 side_cast=rows2d(moe_w3[0]))
    xs, swk_s, swv_s = _swa(xs, cache_swa_k[0].reshape(dec_batch, WINDOW, C_KV_DIM),
                            cache_swa_v[0].reshape(dec_batch, WINDOW, C_KV_DIM),
                            jnp.tile(cos_s, (dec_batch, 1)), jnp.tile(sin_s, (dec_batch, 1)), sinks, swa_consts,
                            n_batch=1, tiles_per_batch=1, n_seq=dec_batch, seq_rows=dec_seq, past_valid=True)
    xp, xs = cross_attention(xp, xs, 1)

    n_tiles = n_rows // ROW_TILE
    router_w = jnp.zeros((d, LANES), F32).at[:, :N_EXPERTS].set(moe_router[0].astype(F32))
    rhi = router_w.astype(BF16)
    rlo = (router_w - rhi.astype(F32)).astype(BF16)
    lstrict = jnp.asarray(np.tril(np.ones((ROW_TILE, ROW_TILE), np.float32), -1), BF16)
    su = jnp.asarray(np.triu(np.ones((LANES, LANES), np.float32), 1), BF16)
    sel = jnp.asarray(np.eye(GROUP, LANES, dtype=np.float32), BF16)
    nrm_ffn = row(norm_ffn[1])
    info, seg = _router(xp, xs, nrm_ffn, jnp.concatenate([rhi, rlo], axis=1), lstrict, su)
    seg = seg.reshape(n_tiles, GROUP, LANES)[:, 0, :N_EXPERTS].astype(jnp.int32)
    n_sorted_tiles = -(-(TOP_K * n_rows + n_tiles * N_EXPERTS * (GROUP - 1) + N_EXPERTS * (ROW_TILE - 1)) // ROW_TILE)
    table, tile_expert, tile_valid = _segment_table(seg, n_sorted_tiles)
    hsorted = _dispatch(table, info, xp, xs, nrm_ffn, sel, n_sorted_tiles)
    esorted = _grouped_ffn(hsorted, tile_expert, tile_valid, w1b.reshape(moe_w1.shape[1:]),
                           w3b.reshape(moe_w3.shape[1:]), w2b.reshape(moe_w2.shape[1:]))
    y_prompt, y_sample = _combine(table, info, xp, xs, esorted)

    n_even = state_hgrn.shape[0]
    n_odd = cache_swa_k.shape[0]
    return (y_prompt.reshape(n_batch, seq, d), y_sample.reshape(dec_batch, dec_seq, d),
            mem_k.reshape(depth, n_batch, n_mem, X_HEADS, X_HDIM), mem_v.reshape(depth, n_batch, n_mem, X_HEADS, X_HDIM),
            hgrn_p.reshape(n_even, n_batch, B_HEADS, B_HDIM, B_HDIM),
            gmlp_v.reshape(n_even, dec_batch, dec_seq, A_GROUPS, A_GDIM),
            hgrn_s.reshape(n_even, dec_batch, B_HEADS, B_HDIM, B_HDIM),
            swk_p.reshape(n_odd, n_batch, WINDOW, C_KV_HEADS, C_HDIM), swv_p.reshape(n_odd, n_batch, WINDOW, C_KV_HEADS, C_HDIM),
            swk_s.reshape(n_odd, dec_batch, dec_seq, C_KV_HEADS, C_HDIM), swv_s.reshape(n_odd, dec_batch, dec_seq, C_KV_HEADS, C_HDIM))
```
